```python
import math
import jax, jax.numpy as jnp
from jax import lax
import numpy as np

D_MODEL = 1024
BATCH = 32
SEQ = 2048
DEPTH = 2

N_META = 16
EPS = 1e-6
A_HEADS = 4
A_DK = 128
A_DV = 128
A_CONV = 4
A_CHUNK = 64
A_WK = A_HEADS * A_DK
A_WV = A_HEADS * A_DV
B_HEADS = 4
B_DK = 128
B_DV = 128
B_CHUNK = 16
B_WK = B_HEADS * B_DK
B_WV = B_HEADS * B_DV
PROJ_SPLITS = (A_WK, A_WK, A_WV, A_HEADS, A_HEADS, A_WV, B_WK, B_WK, B_WV, B_WV, D_MODEL, D_MODEL)
PROJ_WIDTH = sum(PROJ_SPLITS)

kernel_name = 'hybrid_gdn_hgrn2_gated_merge'


def _rmsnorm(x, w):
    xf = x.astype(jnp.float32)
    y = xf * lax.rsqrt(jnp.mean(xf * xf, axis=-1, keepdims=True) + EPS)
    return (y * w.astype(jnp.float32)).astype(x.dtype)


def _l2norm(x):
    xf = x.astype(jnp.float32)
    return xf * lax.rsqrt(jnp.sum(xf * xf, axis=-1, keepdims=True) + EPS)


def _causal_conv(x, w):
    k, c = w.shape
    return lax.conv_general_dilated(
        x, w[:, None, :].astype(x.dtype), window_strides=(1,), padding=[(k - 1, 0)],
        dimension_numbers=('NWC', 'WIO', 'NWC'), feature_group_count=c)


def _masked_exp(diff, mask):
    return jnp.where(mask, jnp.exp(jnp.where(mask, diff, 0.0)), 0.0)


def _chunked_scan(step, state0, inputs, chunk):
    meta = tuple(a[:, :N_META] for a in inputs)
    real = tuple(a[:, N_META:] for a in inputs)
    state, out_meta = step(state0, meta)
    b, s = real[0].shape[:2]
    n = s // chunk
    xs = tuple(jnp.moveaxis(a.reshape(b, n, chunk, *a.shape[2:]), 1, 0) for a in real)
    _, out_real = lax.scan(step, state, xs)
    out_real = jnp.moveaxis(out_real, 0, 1).reshape(b, s, *out_real.shape[3:])
    return jnp.concatenate([out_meta, out_real], axis=1)


def _gdn_step(S, inp):
    q, k, v, beta, g = inp
    q, k, v = (jnp.swapaxes(a, 1, 2) for a in (q, k, v))
    beta, g = jnp.swapaxes(beta, 1, 2), jnp.swapaxes(g, 1, 2)
    L = q.shape[2]
    causal = jnp.tril(jnp.ones((L, L), dtype=bool))
    strict = jnp.tril(jnp.ones((L, L), dtype=bool), k=-1)
    G = jnp.cumsum(g, axis=-1)
    decay = _masked_exp(G[..., :, None] - G[..., None, :], causal)
    kb = k * beta[..., None]
    l_mat = jnp.where(strict, jnp.einsum('bhik,bhjk->bhij', kb, k) * decay, 0.0)
    t_mat = l_mat + jnp.eye(L, dtype=l_mat.dtype)
    rhs = jnp.concatenate([v * beta[..., None], kb * jnp.exp(G)[..., None]], axis=-1)
    sol = lax.linalg.triangular_solve(t_mat, rhs, left_side=True, lower=True, unit_diagonal=True)
    dv = v.shape[-1]
    u, w = sol[..., :dv], sol[..., dv:]
    v_new = u - jnp.einsum('bhik,bhkv->bhiv', w, S)
    scores = jnp.einsum('bhik,bhjk->bhij', q, k) * decay
    o = (jnp.einsum('bhik,bhkv->bhiv', q * jnp.exp(G)[..., None], S)
         + jnp.einsum('bhij,bhjv->bhiv', scores, v_new))
    g_last = G[..., -1:]
    S = (S * jnp.exp(g_last)[..., None]
         + jnp.einsum('bhjk,bhjv->bhkv', k * jnp.exp(g_last - G)[..., None], v_new))
    return S, jnp.swapaxes(o, 1, 2)


def _hgrn2_step(S, inp):
    q, k, v, log_f = (jnp.swapaxes(a, 1, 2) for a in inp)
    L = q.shape[2]
    causal = jnp.tril(jnp.ones((L, L), dtype=bool))
    Bc = jnp.cumsum(log_f, axis=2)
    pair = _masked_exp(Bc[:, :, :, None, :] - Bc[:, :, None, :, :],
                       causal[:, :, None])
    scores = jnp.einsum('bhik,bhjk,bhijk->bhij', q, k, pair)
    o = (jnp.einsum('bhik,bhkv->bhiv', q * jnp.exp(Bc), S)
         + jnp.einsum('bhij,bhjv->bhiv', scores, v))
    b_last = Bc[:, :, -1:]
    S = (S * jnp.exp(b_last[:, :, 0])[..., None]
         + jnp.einsum('bhjk,bhjv->bhkv', k * jnp.exp(b_last - Bc), v))
    return S, jnp.swapaxes(o, 1, 2)


def _layer(h, norm_w, w_in, conv_w, a_log, dt_bias, gnorm_a, gnorm_b, lb,
           w_branch_a, w_branch_b, w_out):
    b, t, _ = h.shape
    f32 = jnp.float32
    xn = _rmsnorm(h, norm_w)
    proj = xn @ w_in.astype(h.dtype)
    offsets = np.cumsum(PROJ_SPLITS)[:-1].tolist()
    (a_q, a_k, a_v, a_beta, a_alpha, a_z,
     b_q, b_f, b_i, b_g, gate_a, gate_b) = jnp.split(proj, offsets, axis=-1)

    qkv = jax.nn.silu(_causal_conv(jnp.concatenate([a_q, a_k, a_v], axis=-1), conv_w))
    q, k, v = jnp.split(qkv, [A_WK, 2 * A_WK], axis=-1)
    q = _l2norm(q.reshape(b, t, A_HEADS, A_DK)) * (A_DK ** -0.5)
    k = _l2norm(k.reshape(b, t, A_HEADS, A_DK))
    v = v.reshape(b, t, A_HEADS, A_DV).astype(f32)
    beta = jax.nn.sigmoid(a_beta.astype(f32))
    g = -jnp.exp(a_log.astype(f32)) * jax.nn.softplus(a_alpha.astype(f32) + dt_bias.astype(f32))
    s0_a = jnp.zeros((b, A_HEADS, A_DK, A_DV), f32)
    o_a = _chunked_scan(_gdn_step, s0_a, (q, k, v, beta, g), A_CHUNK)
    y_a = _rmsnorm(o_a, gnorm_a) * jax.nn.silu(a_z.astype(f32).reshape(b, t, A_HEADS, A_DV))
    y_a = y_a.reshape(b, t, A_WV).astype(h.dtype)

    qb = (jax.nn.silu(b_q.astype(f32)) * (B_DK ** -0.5)).reshape(b, t, B_HEADS, B_DK)
    lbf = lb.astype(f32)
    pos = lbf > 0.0
    log_sig = jax.nn.log_sigmoid(b_f.astype(f32))
    log_f = jnp.where(pos,
                      jnp.logaddexp(jnp.log(jnp.where(pos, lbf, 1.0)), jnp.log1p(-lbf) + log_sig),
                      log_sig)
    kb_ = -jnp.expm1(log_f)
    log_f = log_f.reshape(b, t, B_HEADS, B_DK)
    kb_ = kb_.reshape(b, t, B_HEADS, B_DK)
    vb = b_i.astype(f32).reshape(b, t, B_HEADS, B_DV)
    s0_b = jnp.zeros((b, B_HEADS, B_DK, B_DV), f32)
    o_b = _chunked_scan(_hgrn2_step, s0_b, (qb, kb_, vb, log_f), B_CHUNK)
    y_b = _rmsnorm(o_b, gnorm_b) * jax.nn.silu(b_g.astype(f32).reshape(b, t, B_HEADS, B_DV))
    y_b = y_b.reshape(b, t, B_WV).astype(h.dtype)

    y_a = y_a @ w_branch_a.astype(h.dtype)
    y_b = y_b @ w_branch_b.astype(h.dtype)
    mixed = jax.nn.sigmoid(gate_a) * y_a + jax.nn.sigmoid(gate_b) * y_b
    return h + mixed @ w_out.astype(h.dtype)


def _fwd_setup_inputs(seed: int = 0) -> dict:
    key = jax.random.key(seed)
    ks = jax.random.split(key, 15)
    nrm = jax.random.normal
    x = nrm(ks[0], (BATCH, SEQ, D_MODEL), jnp.float32)
    meta_tokens = nrm(ks[1], (N_META, D_MODEL), jnp.float32)
    norm_w = 1.0 + 0.02 * nrm(ks[2], (DEPTH, D_MODEL), jnp.float32)
    w_in = nrm(ks[3], (DEPTH, D_MODEL, PROJ_WIDTH), jnp.float32) * D_MODEL ** -0.5
    conv_w = nrm(ks[4], (DEPTH, A_CONV, 2 * A_WK + A_WV), jnp.float32) * A_CONV ** -0.5
    a_log = jnp.log(jax.random.uniform(ks[5], (DEPTH, A_HEADS), jnp.float32, 1.0, 16.0))
    dt = jnp.exp(jax.random.uniform(ks[6], (DEPTH, A_HEADS), jnp.float32,
                                    math.log(1e-3), math.log(1e-1)))
    dt_bias = dt + jnp.log(-jnp.expm1(-dt))
    gnorm_a = 1.0 + 0.02 * nrm(ks[7], (DEPTH, A_DV), jnp.float32)
    gnorm_b = 1.0 + 0.02 * nrm(ks[8], (DEPTH, B_DV), jnp.float32)
    hgrn_lower_bounds = 0.1 * nrm(ks[9], (DEPTH, B_WK), jnp.float32)
    w_branch_a = nrm(ks[10], (DEPTH, A_WV, D_MODEL), jnp.float32) * A_WV ** -0.5
    w_branch_b = nrm(ks[11], (DEPTH, B_WV, D_MODEL), jnp.float32) * B_WV ** -0.5
    w_out = nrm(ks[12], (DEPTH, D_MODEL, D_MODEL), jnp.float32) * D_MODEL ** -0.5
    final_norm_w = 1.0 + 0.02 * nrm(ks[13], (D_MODEL,), jnp.float32)
    return {'x': x, 'meta_tokens': meta_tokens, 'norm_w': norm_w, 'w_in': w_in,
            'conv_w': conv_w, 'a_log': a_log, 'dt_bias': dt_bias, 'gnorm_a': gnorm_a,
            'gnorm_b': gnorm_b, 'hgrn_lower_bounds': hgrn_lower_bounds,
            'w_branch_a': w_branch_a, 'w_branch_b': w_branch_b, 'w_out': w_out,
            'final_norm_w': final_norm_w}


def _fwd_reference(x, meta_tokens, norm_w, w_in, conv_w, a_log, dt_bias, gnorm_a, gnorm_b,
              hgrn_lower_bounds, w_branch_a, w_branch_b, w_out, final_norm_w):
    b = x.shape[0]
    meta = jnp.broadcast_to(meta_tokens.astype(x.dtype)[None], (b, N_META, D_MODEL))
    h = jnp.concatenate([meta, x], axis=1)
    lb_sm = jax.nn.softmax(hgrn_lower_bounds.astype(jnp.float32), axis=0)
    lb_all = jnp.cumsum(lb_sm, axis=0) - lb_sm[0]
    for l in range(DEPTH):
        h = _layer(h, norm_w[l], w_in[l], conv_w[l], a_log[l], dt_bias[l], gnorm_a[l],
                   gnorm_b[l], lb_all[l], w_branch_a[l], w_branch_b[l], w_out[l])
    return _rmsnorm(h, final_norm_w)[:, N_META:]


import jax as _jax
import jax.numpy as _jnp

TWIN_FORMAT = 'train_step'
FWD_PARAMS = ['x', 'meta_tokens', 'norm_w', 'w_in', 'conv_w', 'a_log', 'dt_bias', 'gnorm_a', 'gnorm_b', 'hgrn_lower_bounds', 'w_branch_a', 'w_branch_b', 'w_out', 'final_norm_w']
TWIN_WEIGHTS = ['meta_tokens', 'norm_w', 'w_in', 'conv_w', 'a_log', 'dt_bias', 'gnorm_a', 'gnorm_b', 'hgrn_lower_bounds', 'w_branch_a', 'w_branch_b', 'w_out', 'final_norm_w']
TWIN_DIFF_INPUT = 'x'
TWIN_INPUTS = ['x', 'meta_tokens', 'norm_w', 'w_in', 'conv_w', 'a_log', 'dt_bias', 'gnorm_a', 'gnorm_b', 'hgrn_lower_bounds', 'w_branch_a', 'w_branch_b', 'w_out', 'final_norm_w', 'loss_target', 'm_meta_tokens', 'm_norm_w', 'm_w_in', 'm_conv_w', 'm_a_log', 'm_dt_bias', 'm_gnorm_a', 'm_gnorm_b', 'm_hgrn_lower_bounds', 'm_w_branch_a', 'm_w_branch_b', 'm_w_out', 'm_final_norm_w', 'v_meta_tokens', 'v_norm_w', 'v_w_in', 'v_conv_w', 'v_a_log', 'v_dt_bias', 'v_gnorm_a', 'v_gnorm_b', 'v_hgrn_lower_bounds', 'v_w_branch_a', 'v_w_branch_b', 'v_w_out', 'v_final_norm_w']
TWIN_OUTPUTS = ['loss', 'grad_x', 'grad_meta_tokens', 'grad_norm_w', 'grad_w_in', 'grad_conv_w', 'grad_a_log', 'grad_dt_bias', 'grad_gnorm_a', 'grad_gnorm_b', 'grad_hgrn_lower_bounds', 'grad_w_branch_a', 'grad_w_branch_b', 'grad_w_out', 'grad_final_norm_w', 'delta_meta_tokens', 'delta_norm_w', 'delta_w_in', 'delta_conv_w', 'delta_a_log', 'delta_dt_bias', 'delta_gnorm_a', 'delta_gnorm_b', 'delta_hgrn_lower_bounds', 'delta_w_branch_a', 'delta_w_branch_b', 'delta_w_out', 'delta_final_norm_w', 'new_m_meta_tokens', 'new_m_norm_w', 'new_m_w_in', 'new_m_conv_w', 'new_m_a_log', 'new_m_dt_bias', 'new_m_gnorm_a', 'new_m_gnorm_b', 'new_m_hgrn_lower_bounds', 'new_m_w_branch_a', 'new_m_w_branch_b', 'new_m_w_out', 'new_m_final_norm_w', 'new_v_meta_tokens', 'new_v_norm_w', 'new_v_w_in', 'new_v_conv_w', 'new_v_a_log', 'new_v_dt_bias', 'new_v_gnorm_a', 'new_v_gnorm_b', 'new_v_hgrn_lower_bounds', 'new_v_w_branch_a', 'new_v_w_branch_b', 'new_v_w_out', 'new_v_final_norm_w']
TWIN_LEAF_KINDS = {'loss': 'loss', 'grad_x': 'grad_x', 'grad_meta_tokens': 'grad_w', 'grad_norm_w': 'grad_w', 'grad_w_in': 'grad_w', 'grad_conv_w': 'grad_w', 'grad_a_log': 'grad_w', 'grad_dt_bias': 'grad_w', 'grad_gnorm_a': 'grad_w', 'grad_gnorm_b': 'grad_w', 'grad_hgrn_lower_bounds': 'grad_w', 'grad_w_branch_a': 'grad_w', 'grad_w_branch_b': 'grad_w', 'grad_w_out': 'grad_w', 'grad_final_norm_w': 'grad_w', 'delta_meta_tokens': 'delta_w', 'delta_norm_w': 'delta_w', 'delta_w_in': 'delta_w', 'delta_conv_w': 'delta_w', 'delta_a_log': 'delta_w', 'delta_dt_bias': 'delta_w', 'delta_gnorm_a': 'delta_w', 'delta_gnorm_b': 'delta_w', 'delta_hgrn_lower_bounds': 'delta_w', 'delta_w_branch_a': 'delta_w', 'delta_w_branch_b': 'delta_w', 'delta_w_out': 'delta_w', 'delta_final_norm_w': 'delta_w', 'new_m_meta_tokens': 'new_m', 'new_m_norm_w': 'new_m', 'new_m_w_in': 'new_m', 'new_m_conv_w': 'new_m', 'new_m_a_log': 'new_m', 'new_m_dt_bias': 'new_m', 'new_m_gnorm_a': 'new_m', 'new_m_gnorm_b': 'new_m', 'new_m_hgrn_lower_bounds': 'new_m', 'new_m_w_branch_a': 'new_m', 'new_m_w_branch_b': 'new_m', 'new_m_w_out': 'new_m', 'new_m_final_norm_w': 'new_m', 'new_v_meta_tokens': 'new_v', 'new_v_norm_w': 'new_v', 'new_v_w_in': 'new_v', 'new_v_conv_w': 'new_v', 'new_v_a_log': 'new_v', 'new_v_dt_bias': 'new_v', 'new_v_gnorm_a': 'new_v', 'new_v_gnorm_b': 'new_v', 'new_v_hgrn_lower_bounds': 'new_v', 'new_v_w_branch_a': 'new_v', 'new_v_w_branch_b': 'new_v', 'new_v_w_out': 'new_v', 'new_v_final_norm_w': 'new_v'}


def _forward(args):
    return _fwd_reference(*[args[k] for k in FWD_PARAMS])


def _output_shape():
    out = _jax.eval_shape(lambda: _forward(_fwd_setup_inputs(0)))
    return out.shape, out.dtype

N_MICROBATCH = 1
ADAM_LR = 0.001
ADAM_B1 = 0.9
ADAM_B2 = 0.999
ADAM_EPS = 1e-08
ADAM_WD = 0.01
ADAM_STEP = 10
PER_EXAMPLE_BATCH_AXIS = {'x': 0, 'loss_target': 0}
SHARED_INPUTS = []
_WEIGHT_DTYPES = {'meta_tokens': _jnp.float32, 'norm_w': _jnp.float32, 'w_in': _jnp.float32, 'conv_w': _jnp.float32, 'a_log': _jnp.float32, 'dt_bias': _jnp.float32, 'gnorm_a': _jnp.float32, 'gnorm_b': _jnp.float32, 'hgrn_lower_bounds': _jnp.float32, 'w_branch_a': _jnp.float32, 'w_branch_b': _jnp.float32, 'w_out': _jnp.float32, 'final_norm_w': _jnp.float32}
MOMENT_SCALE = {'meta_tokens': 4.674351e-03, 'norm_w': 1.684641e-01, 'w_in': 7.023780e-02, 'conv_w': 8.354119e-02, 'a_log': 5.358126e-01, 'dt_bias': 5.205372e-01, 'gnorm_a': 2.031624e-01, 'gnorm_b': 2.259136e-01, 'hgrn_lower_bounds': 9.336927e-03, 'w_branch_a': 7.299451e-02, 'w_branch_b': 7.593258e-02, 'w_out': 1.052470e-01, 'final_norm_w': 6.401986e+01}


def _to_microbatches(a, axis):
    t = _jnp.moveaxis(a, axis, 0)
    t = t.reshape((N_MICROBATCH, t.shape[0] // N_MICROBATCH) + t.shape[1:])
    return _jnp.moveaxis(t, 1, axis + 1)


def setup_inputs(seed: int = 0) -> dict:
    inp = _fwd_setup_inputs(seed)
    key = _jax.random.fold_in(_jax.random.key(seed), 7919)
    shape, _ = _output_shape()
    out = dict(inp)
    out["loss_target"] = _jax.random.normal(_jax.random.fold_in(key, 0), shape, _jnp.float32)
    for i, name in enumerate(TWIN_WEIGHTS):
        w = inp[name].astype(_jnp.float32)
        if MOMENT_SCALE is None:
            s = _jnp.sqrt(_jnp.mean(_jnp.square(w)) + 1e-30)
        else:
            s = MOMENT_SCALE[name]
        km, kv = _jax.random.split(_jax.random.fold_in(key, i + 1))
        out[name] = w
        out["m_" + name] = s * _jax.random.normal(km, w.shape, _jnp.float32)
        out["v_" + name] = (s * s) * _jax.random.uniform(kv, w.shape, _jnp.float32, 0.5, 1.5)
    if N_MICROBATCH > 1:
        for name, axis in PER_EXAMPLE_BATCH_AXIS.items():
            out[name] = _to_microbatches(out[name], axis)
    return {'x': out['x'], 'meta_tokens': out['meta_tokens'], 'norm_w': out['norm_w'], 'w_in': out['w_in'], 'conv_w': out['conv_w'], 'a_log': out['a_log'], 'dt_bias': out['dt_bias'], 'gnorm_a': out['gnorm_a'], 'gnorm_b': out['gnorm_b'], 'hgrn_lower_bounds': out['hgrn_lower_bounds'], 'w_branch_a': out['w_branch_a'], 'w_branch_b': out['w_branch_b'], 'w_out': out['w_out'], 'final_norm_w': out['final_norm_w'], 'loss_target': out['loss_target'], 'm_meta_tokens': out['m_meta_tokens'], 'm_norm_w': out['m_norm_w'], 'm_w_in': out['m_w_in'], 'm_conv_w': out['m_conv_w'], 'm_a_log': out['m_a_log'], 'm_dt_bias': out['m_dt_bias'], 'm_gnorm_a': out['m_gnorm_a'], 'm_gnorm_b': out['m_gnorm_b'], 'm_hgrn_lower_bounds': out['m_hgrn_lower_bounds'], 'm_w_branch_a': out['m_w_branch_a'], 'm_w_branch_b': out['m_w_branch_b'], 'm_w_out': out['m_w_out'], 'm_final_norm_w': out['m_final_norm_w'], 'v_meta_tokens': out['v_meta_tokens'], 'v_norm_w': out['v_norm_w'], 'v_w_in': out['v_w_in'], 'v_conv_w': out['v_conv_w'], 'v_a_log': out['v_a_log'], 'v_dt_bias': out['v_dt_bias'], 'v_gnorm_a': out['v_gnorm_a'], 'v_gnorm_b': out['v_gnorm_b'], 'v_hgrn_lower_bounds': out['v_hgrn_lower_bounds'], 'v_w_branch_a': out['v_w_branch_a'], 'v_w_branch_b': out['v_w_branch_b'], 'v_w_out': out['v_w_out'], 'v_final_norm_w': out['v_final_norm_w']}


def _loss(weights, diff, rest, loss_target):
    with _jax.named_scope("forward"):
        args = {**rest, TWIN_DIFF_INPUT: diff, **{k: w.astype(_WEIGHT_DTYPES[k]) for k, w in weights.items()}}
        y = _forward(args)
    with _jax.named_scope("loss_head"):
        err = _jnp.square(y.astype(_jnp.float32) - loss_target)
        return 0.5 * _jnp.sum(_jnp.mean(err, axis=-1)) if err.ndim else 0.5 * err


def _adamw(w, g, m, v):
    m = ADAM_B1 * m + (1.0 - ADAM_B1) * g
    v = ADAM_B2 * v + (1.0 - ADAM_B2) * _jnp.square(g)
    m_hat = m / (1.0 - ADAM_B1 ** ADAM_STEP)
    v_hat = v / (1.0 - ADAM_B2 ** ADAM_STEP)
    delta = -ADAM_LR * (m_hat / (_jnp.sqrt(v_hat) + ADAM_EPS) + ADAM_WD * w)
    return delta, m, v


def reference(x, meta_tokens, norm_w, w_in, conv_w, a_log, dt_bias, gnorm_a, gnorm_b, hgrn_lower_bounds, w_branch_a, w_branch_b, w_out, final_norm_w, loss_target, m_meta_tokens, m_norm_w, m_w_in, m_conv_w, m_a_log, m_dt_bias, m_gnorm_a, m_gnorm_b, m_hgrn_lower_bounds, m_w_branch_a, m_w_branch_b, m_w_out, m_final_norm_w, v_meta_tokens, v_norm_w, v_w_in, v_conv_w, v_a_log, v_dt_bias, v_gnorm_a, v_gnorm_b, v_hgrn_lower_bounds, v_w_branch_a, v_w_branch_b, v_w_out, v_final_norm_w):
    given = dict(x=x, meta_tokens=meta_tokens, norm_w=norm_w, w_in=w_in, conv_w=conv_w, a_log=a_log, dt_bias=dt_bias, gnorm_a=gnorm_a, gnorm_b=gnorm_b, hgrn_lower_bounds=hgrn_lower_bounds, w_branch_a=w_branch_a, w_branch_b=w_branch_b, w_out=w_out, final_norm_w=final_norm_w, loss_target=loss_target, m_meta_tokens=m_meta_tokens, m_norm_w=m_norm_w, m_w_in=m_w_in, m_conv_w=m_conv_w, m_a_log=m_a_log, m_dt_bias=m_dt_bias, m_gnorm_a=m_gnorm_a, m_gnorm_b=m_gnorm_b, m_hgrn_lower_bounds=m_hgrn_lower_bounds, m_w_branch_a=m_w_branch_a, m_w_branch_b=m_w_branch_b, m_w_out=m_w_out, m_final_norm_w=m_final_norm_w, v_meta_tokens=v_meta_tokens, v_norm_w=v_norm_w, v_w_in=v_w_in, v_conv_w=v_conv_w, v_a_log=v_a_log, v_dt_bias=v_dt_bias, v_gnorm_a=v_gnorm_a, v_gnorm_b=v_gnorm_b, v_hgrn_lower_bounds=v_hgrn_lower_bounds, v_w_branch_a=v_w_branch_a, v_w_branch_b=v_w_branch_b, v_w_out=v_w_out, v_final_norm_w=v_final_norm_w)
    weights = {n: given[n] for n in TWIN_WEIGHTS}
    shared = {n: given[n] for n in SHARED_INPUTS}
    per_example = {n: given[n] for n in ['x']}
    grad_fn = _jax.value_and_grad(_loss, argnums=(0, 1))

    def one_microbatch(ex, loss_target):
        ex = dict(ex)
        diff = ex.pop(TWIN_DIFF_INPUT)
        return grad_fn(weights, diff, {**shared, **ex}, loss_target)

    if N_MICROBATCH == 1:
        loss, (grad_w, grad_x) = one_microbatch(per_example, given["loss_target"])
    else:
        def body(carry, xs):
            loss_sum, grad_sum = carry
            l_k, (gw_k, gx_k) = one_microbatch(xs[0], xs[1])
            with _jax.named_scope("update"):
                return (loss_sum + l_k, _jax.tree.map(_jnp.add, grad_sum, gw_k)), gx_k

        init = (_jnp.zeros((), _jnp.float32), _jax.tree.map(_jnp.zeros_like, weights))
        (loss, grad_w), grad_x = _jax.lax.scan(body, init, (per_example, given["loss_target"]))
    with _jax.named_scope("update"):
        delta_w, new_m, new_v = {}, {}, {}
        for n in TWIN_WEIGHTS:
            delta_w[n], new_m[n], new_v[n] = _adamw(weights[n], grad_w[n], given["m_" + n], given["v_" + n])
    return (loss, grad_x, *[grad_w[n] for n in TWIN_WEIGHTS], *[delta_w[n] for n in TWIN_WEIGHTS],
            *[new_m[n] for n in TWIN_WEIGHTS], *[new_v[n] for n in TWIN_WEIGHTS])
```

```python
import functools

import numpy as np
import jax
import jax.numpy as jnp
from jax import lax
from jax.experimental import pallas as pl
from jax.experimental.pallas import tpu as pltpu

F32 = jnp.float32
BF16 = jnp.bfloat16
HI = lax.Precision.HIGHEST
SDS = jax.ShapeDtypeStruct

NH = 4
HD = 128
HW = NH * HD
N_META = 16
CH = 64
SUB = 16
PAD = CH - N_META
EPS = 1e-6
Q_SCALE = HD ** -0.5
DEPTH = 2
CONV_K = 4
VMEM_LIMIT = 56 * 1024 * 1024
ADAM_LR, ADAM_B1, ADAM_B2, ADAM_EPS, ADAM_WD, ADAM_STEP = 0.001, 0.9, 0.999, 1e-08, 0.01, 10
MESH = pl.DeviceIdType.MESH


def _nn(a, b):
    return jnp.dot(a, b, precision=HI, preferred_element_type=F32)


def _nt(a, b):
    return lax.dot_general(a, b, (((1,), (1,)), ((), ())), precision=HI, preferred_element_type=F32)


def _tn(a, b):
    return _nn(a.T, b)


def _scan_rows(x, group, reverse=False):
    n = x.shape[0]
    pos = lax.bitwise_and(_iota2(x.shape, 0), group - 1)
    s = 1
    while s < group:
        if reverse:
            x = x + jnp.where(pos < group - s, pltpu.roll(x, n - s, axis=0), 0.0)
        else:
            x = x + jnp.where(pos >= s, pltpu.roll(x, s, axis=0), 0.0)
        s *= 2
    return x


def _bnn(a, b):
    return jnp.dot(a.astype(BF16), b.astype(BF16), preferred_element_type=F32)


def _bnt(a, b):
    return lax.dot_general(a.astype(BF16), b.astype(BF16), (((1,), (1,)), ((), ())), preferred_element_type=F32)


def _btn(a, b):
    return lax.dot_general(a.astype(BF16), b.astype(BF16), (((0,), (0,)), ((), ())), preferred_element_type=F32)


_rnn, _rnt, _rtn = _nn, _nt, _tn


def _rr(x):
    return x


def _sig(x):
    return jax.nn.sigmoid(x)


def _silu(x):
    return x * _sig(x)


def _dsilu(x):
    s = _sig(x)
    return s * (1.0 + x * (1.0 - s))


def _softplus(x):
    return jnp.maximum(x, 0.0) + jnp.log(1.0 + jnp.exp(-jnp.abs(x)))


def _logsig(x):
    return jnp.minimum(x, 0.0) - jnp.log(1.0 + jnp.exp(-jnp.abs(x)))


def _rs(x):
    return jnp.sum(x, axis=-1, keepdims=True)


def _params(n_axes):
    return pltpu.CompilerParams(dimension_semantics=("arbitrary",) * n_axes, vmem_limit_bytes=VMEM_LIMIT)


def _tile(n, target):
    best = 8
    for t in range(8, target + 1, 8):
        if n % t == 0:
            best = t
    return best


def _ctile(pw, most=7):
    return HD * max(k for k in range(1, most + 1) if (pw // HD) % k == 0)


def _iota2(shape, axis):
    return lax.broadcasted_iota(jnp.int32, shape, axis)


class _Layout:
    def __init__(self, d):
        self.d = d
        self.wm = 2 * HW + 2 * d
        self.c_qkv = self.wm
        self.c_b = self.wm + 3 * HW
        self.c_ba = self.wm + 6 * HW
        self.pw = self.c_ba + HD
        assert self.c_b % (3 * HW) == 0
        o = 0
        segs = {}
        for name, w in (("a_q", HW), ("a_k", HW), ("a_v", HW), ("ba", 2 * NH), ("a_z", HW), ("b_q", HW), ("b_f", HW),
                        ("b_i", HW), ("b_g", HW), ("gate_a", d), ("gate_b", d)):
            segs[name] = (o, o + w)
            o += w
        self.segs = segs
        self.width = o
        self.order = ("a_z", "b_g", "gate_a", "gate_b", "a_q", "a_k", "a_v", "b_q", "b_f", "b_i", "ba")

    def to_kernel(self, w):
        parts = [w[..., self.segs[n][0]:self.segs[n][1]] for n in self.order]
        parts.append(jnp.zeros(w.shape[:-1] + (HD - 2 * NH,), w.dtype))
        return jnp.concatenate(parts, axis=-1)

    def from_kernel(self, g):
        off, where = 0, {}
        for n in self.order:
            w = self.segs[n][1] - self.segs[n][0]
            where[n] = (off, off + w)
            off += w
        names = sorted(self.segs, key=lambda n: self.segs[n][0])
        return jnp.concatenate([g[..., where[n][0]:where[n][1]] for n in names], axis=-1)


def _norm_proj_fwd(h, nw, wp):
    n, d = h.shape
    pw = wp.shape[1]
    tm, tn = _tile(n, 768), _ctile(pw)

    def body(h_ref, nw_ref, w_ref, o_ref, xn_ref):
        @pl.when(pl.program_id(1) == 0)
        def _():
            x = h_ref[...]
            r = lax.rsqrt(jnp.mean(x * x, axis=-1, keepdims=True) + EPS)
            xn_ref[...] = (x * r * nw_ref[...]).astype(BF16)

        o_ref[...] = jnp.dot(xn_ref[...], w_ref[...], preferred_element_type=F32)

    return pl.pallas_call(
        body, grid=(n // tm, pw // tn),
        in_specs=[pl.BlockSpec((tm, d), lambda i, j: (i, 0)), pl.BlockSpec((1, d), lambda i, j: (0, 0)),
                  pl.BlockSpec((d, tn), lambda i, j: (0, j))],
        out_specs=pl.BlockSpec((tm, tn), lambda i, j: (i, j)), out_shape=SDS((n, pw), F32),
        scratch_shapes=[pltpu.VMEM((tm, d), BF16)], compiler_params=_params(2), name="norm_proj_fwd")(h, nw, wp)


def _row_valid(tm, tp, base):
    row = base + _iota2((tm, 1), 0)
    return lax.rem(row, tp) >= PAD


def _proj_bwd_dx(dproj, wp, h, nw, dhn, tp):
    n, d = h.shape
    pw = wp.shape[1]
    tm, tk = _tile(n, 768), _ctile(pw)
    nk = pw // tk

    def body(dp_ref, w_ref, h_ref, nw_ref, dhn_ref, dh_ref, dnw_ref, acc_ref):
        i, k = pl.program_id(0), pl.program_id(1)

        @pl.when(k == 0)
        def _():
            acc_ref[...] = jnp.zeros_like(acc_ref)

        @pl.when((i == 0) & (k == 0))
        def _():
            dnw_ref[...] = jnp.zeros_like(dnw_ref)

        valid = _row_valid(tm, tp, i * tm)
        dp = jnp.where(valid, dp_ref[...], 0.0)
        acc_ref[...] += _bnt(dp, w_ref[...])

        @pl.when(k == nk - 1)
        def _():
            x = h_ref[...]
            r = lax.rsqrt(jnp.mean(x * x, axis=-1, keepdims=True) + EPS)
            xh = x * r
            dxn = acc_ref[...]
            dnw_ref[...] += jnp.sum(dxn * xh, axis=0, keepdims=True)
            dxh = dxn * nw_ref[...]
            dh_ref[...] = dhn_ref[...] + r * (dxh - xh * jnp.mean(dxh * xh, axis=-1, keepdims=True))

    return pl.pallas_call(
        body, grid=(n // tm, nk),
        in_specs=[pl.BlockSpec((tm, tk), lambda i, k: (i, k)), pl.BlockSpec((d, tk), lambda i, k: (0, k)),
                  pl.BlockSpec((tm, d), lambda i, k: (i, 0)), pl.BlockSpec((1, d), lambda i, k: (0, 0)),
                  pl.BlockSpec((tm, d), lambda i, k: (i, 0))],
        out_specs=[pl.BlockSpec((tm, d), lambda i, k: (i, 0)), pl.BlockSpec((1, d), lambda i, k: (0, 0))],
        out_shape=[SDS((n, d), F32), SDS((1, d), F32)],
        scratch_shapes=[pltpu.VMEM((tm, d), F32)], compiler_params=_params(2), name="proj_bwd_dx")(dproj, wp, h, nw, dhn)


def _proj_bwd_dw(dproj, h, nw, tp):
    n, d = h.shape
    pw = dproj.shape[1]
    tm, tn = _tile(n, 768), _ctile(pw)

    def body(dp_ref, h_ref, nw_ref, dw_ref):
        i = pl.program_id(1)

        @pl.when(i == 0)
        def _():
            dw_ref[...] = jnp.zeros_like(dw_ref)

        x = h_ref[...]
        r = lax.rsqrt(jnp.mean(x * x, axis=-1, keepdims=True) + EPS)
        xn = x * r * nw_ref[...]
        dp = jnp.where(_row_valid(tm, tp, i * tm), dp_ref[...], 0.0)
        dw_ref[...] += _btn(xn, dp)

    return pl.pallas_call(
        body, grid=(pw // tn, n // tm),
        in_specs=[pl.BlockSpec((tm, tn), lambda j, i: (i, j)), pl.BlockSpec((tm, d), lambda j, i: (i, 0)),
                  pl.BlockSpec((1, d), lambda j, i: (0, 0))],
        out_specs=pl.BlockSpec((d, tn), lambda j, i: (0, j)), out_shape=SDS((d, pw), F32),
        compiler_params=_params(2), name="proj_bwd_dw")(dproj, h, nw)


def _conv_silu(x, w, row):
    c = x * w[CONV_K - 1:CONV_K, :]
    for k in range(1, CONV_K):
        c = c + jnp.where(row >= k, pltpu.roll(x, k, axis=0), 0.0) * w[CONV_K - 1 - k:CONV_K - k, :]
    return c


def _gdn_prep_fwd(proj, conv_w, lay, nb, tp):
    n = proj.shape[0]
    nblk = 3 * NH
    cb = lay.c_qkv // HD

    def body(p_ref, w_ref, o_ref):
        j = pl.program_id(1)
        x = p_ref[...]
        row = _iota2(x.shape, 0)
        c = _conv_silu(x, w_ref[...], row)
        s = _silu(c)
        r = lax.rsqrt(_rs(s * s) + EPS)
        scale = jnp.where(j < NH, Q_SCALE, 1.0)
        y = jnp.where(j < 2 * NH, s * r * scale, s)
        o_ref[...] = jnp.where(row >= PAD, y, 0.0)

    return pl.pallas_call(
        body, grid=(nb, nblk),
        in_specs=[pl.BlockSpec((tp, HD), lambda b, j: (b, cb + j)), pl.BlockSpec((CONV_K, HD), lambda b, j: (0, j))],
        out_specs=pl.BlockSpec((tp, HD), lambda b, j: (b, j)), out_shape=SDS((n, nblk * HD), F32),
        compiler_params=_params(2), name="gdn_prep_fwd")(proj, conv_w)


def _gdn_prep_bwd(proj, conv_w, dqkv, dproj, lay, nb, tp):
    nblk = 3 * NH
    cb = lay.c_qkv // HD

    def body(p_ref, w_ref, dy_ref, dp_in, dp_ref, dw_ref):
        j, b = pl.program_id(0), pl.program_id(1)
        x = p_ref[...]
        w = w_ref[...]
        row = _iota2(x.shape, 0)
        c = _conv_silu(x, w, row)
        s = _silu(c)
        dy = jnp.where(row >= PAD, dy_ref[...], 0.0)
        r = lax.rsqrt(_rs(s * s) + EPS)
        nh = s * r
        scale = jnp.where(j < NH, Q_SCALE, 1.0)
        ds_n = scale * r * (dy - nh * _rs(dy * nh))
        ds = jnp.where(j < 2 * NH, ds_n, dy)
        dc = ds * _dsilu(c)
        dx = dc * w[CONV_K - 1:CONV_K, :]
        dws = [jnp.sum(dc * x, axis=0, keepdims=True)]
        for k in range(1, CONV_K):
            dx = dx + jnp.where(row < tp - k, pltpu.roll(dc, tp - k, axis=0), 0.0) * w[CONV_K - 1 - k:CONV_K - k, :]
            xs = jnp.where(row >= k, pltpu.roll(x, k, axis=0), 0.0)
            dws.append(jnp.sum(dc * xs, axis=0, keepdims=True))
        dp_ref[...] = dx
        r4 = _iota2((CONV_K, HD), 0)
        dw = jnp.zeros((CONV_K, HD), F32)
        for k in range(CONV_K):
            dw = dw + jnp.where(r4 == CONV_K - 1 - k, dws[k], 0.0)

        @pl.when(b == 0)
        def _():
            dw_ref[...] = dw

        @pl.when(b > 0)
        def _():
            dw_ref[...] += dw

    return pl.pallas_call(
        body, grid=(nblk, nb),
        in_specs=[pl.BlockSpec((tp, HD), lambda j, b: (b, cb + j)), pl.BlockSpec((CONV_K, HD), lambda j, b: (0, j)),
                  pl.BlockSpec((tp, HD), lambda j, b: (b, j)), pl.BlockSpec(memory_space=pl.ANY)],
        out_specs=[pl.BlockSpec((tp, HD), lambda j, b: (b, cb + j)), pl.BlockSpec((CONV_K, HD), lambda j, b: (0, j))],
        out_shape=[SDS(dproj.shape, F32), SDS((CONV_K, nblk * HD), F32)],
        input_output_aliases={3: 0}, compiler_params=_params(2), name="gdn_prep_bwd")(proj, conv_w, dqkv, dproj)


def _gate_consts():
    e = np.zeros((HD, 2 * HW), np.float32)
    s = np.zeros((2 * HW, HD), np.float32)
    for h in range(NH):
        e[h, h * HD:(h + 1) * HD] = 1.0
        e[NH + h, HW + h * HD:HW + (h + 1) * HD] = 1.0
        s[h * HD, h] = 1.0
        s[HW + h * HD, NH + h] = 1.0
    return jnp.asarray(e), jnp.asarray(s)


def _gdn_tri():
    i, j = _iota2((CH, CH), 0), _iota2((CH, CH), 1)
    return i >= j, i > j


def _tri_inv(a, eye):
    p = -a
    t = eye + p
    for _ in range(5):
        p = _nn(p, p)
        t = t + _nn(t, p)
    return t


def _gdn_chunk(q, k, v, beta, g, s0):
    causal, strict = _gdn_tri()
    gc = _scan_rows(g, CH)
    gsq = gc[:, :CH]
    dm = jnp.where(causal, jnp.exp(jnp.where(causal, gsq - gsq.T, 0.0)), 0.0)
    ds = jnp.where(strict, dm, 0.0)
    kb = k * beta
    a = _rnt(kb, k) * ds
    eye = jnp.where(_iota2((CH, CH), 0) == _iota2((CH, CH), 1), 1.0, 0.0)
    tinv = _tri_inv(a, eye)
    eg = jnp.exp(gc)
    rw = kb * eg
    u = _nn(tinv, v * beta)
    w = _nn(tinv, rw)
    vn = u - _rnn(w, s0)
    p = _rnt(q, k) * dm
    qg = q * eg
    gl = gc[CH - 1:CH, :]
    ek = jnp.exp(gl - gc)
    kd = k * ek
    return dict(gc=gc, dm=dm, ds=ds, kb=kb, a=a, tinv=tinv, eg=eg, rw=rw, u=u, w=w, vn=vn, p=p, qg=qg, egl=jnp.exp(gl),
                ek=ek, kd=kd)


def _gdn_gates(ba, e, alog, dtb):
    raw = _nn(ba, e)
    beta = _sig(raw[:, :HW])
    za = raw[:, HW:] + dtb
    g = -jnp.exp(alog) * _softplus(za)
    return beta, g, za


def _gdn_fwd(qkv, proj, e_mat, alog, dtb, lay, nb, nc):
    n = qkv.shape[0]
    cba = lay.c_ba // HD

    def body(x_ref, ba_ref, e_ref, al_ref, dt_ref, o_ref, so_ref, s_ref):
        @pl.when(pl.program_id(1) == 0)
        def _():
            s_ref[...] = jnp.zeros_like(s_ref)

        beta, g, _ = _gdn_gates(ba_ref[...], e_ref[...], al_ref[...], dt_ref[...])
        outs = []
        for h in range(NH):
            hs = slice(h * HD, (h + 1) * HD)
            q, k, v = x_ref[:, hs], x_ref[:, HW + h * HD:HW + (h + 1) * HD], x_ref[:, 2 * HW + h * HD:2 * HW + (h + 1) * HD]
            s0 = s_ref[h]
            so_ref[h] = s0
            c = _gdn_chunk(q, k, v, beta[:, hs], g[:, hs], s0)
            outs.append(_rnn(c["qg"], s0) + _rnn(c["p"], c["vn"]))
            s_ref[h] = s0 * c["egl"] + _rtn(c["kd"], c["vn"])
        o_ref[...] = jnp.concatenate(outs, axis=-1)

    return pl.pallas_call(
        body, grid=(nb, nc),
        in_specs=[pl.BlockSpec((CH, 3 * HW), lambda b, c: (b * nc + c, 0)), pl.BlockSpec((CH, HD), lambda b, c: (b * nc + c, cba)),
                  pl.BlockSpec((HD, 2 * HW), lambda b, c: (0, 0)), pl.BlockSpec((1, HW), lambda b, c: (0, 0)),
                  pl.BlockSpec((1, HW), lambda b, c: (0, 0))],
        out_specs=[pl.BlockSpec((CH, HW), lambda b, c: (b * nc + c, 0)),
                   pl.BlockSpec((None, None, NH, HD, HD), lambda b, c: (b, c, 0, 0, 0))],
        out_shape=[SDS((n, HW), F32), SDS((nb, nc, NH, HD, HD), F32)],
        scratch_shapes=[pltpu.VMEM((NH, HD, HD), F32)], compiler_params=_params(2), name="gdn_fwd")(qkv, proj, e_mat, alog, dtb)


def _gdn_bwd(qkv, proj, e_mat, s_mat, alog, dtb, states, do, dproj, lay, nb, nc):
    n = qkv.shape[0]
    cba = lay.c_ba // HD

    def rev(b, c):
        return b * nc + (nc - 1 - c)

    def body(x_ref, ba_ref, e_ref, sm_ref, al_ref, dt_ref, st_ref, do_ref, dp_in, dx_ref, dba_ref, acc_ref, ds_ref):
        ci = pl.program_id(1)

        @pl.when(ci == 0)
        def _():
            ds_ref[...] = jnp.zeros_like(ds_ref)

        @pl.when((ci == 0) & (pl.program_id(0) == 0))
        def _():
            acc_ref[...] = jnp.zeros_like(acc_ref)

        causal, strict = _gdn_tri()
        alog = al_ref[...]
        beta, g, za = _gdn_gates(ba_ref[...], e_ref[...], alog, dt_ref[...])
        row = _iota2((CH, 1), 0)
        valid = (row >= PAD) | (ci < nc - 1)
        last = row == CH - 1
        dqs, dks, dvs, dbetas, dgs = [], [], [], [], []
        for h in range(NH):
            hs = slice(h * HD, (h + 1) * HD)
            q, k, v = x_ref[:, hs], x_ref[:, HW + h * HD:HW + (h + 1) * HD], x_ref[:, 2 * HW + h * HD:2 * HW + (h + 1) * HD]
            bh = beta[:, hs]
            s0 = st_ref[h]
            ds1 = ds_ref[h]
            doh = do_ref[:, hs]
            c = _gdn_chunk(q, k, v, bh, g[:, hs], s0)
            dvn = _rtn(c["p"], doh) + _rnn(c["kd"], ds1)
            dqg = _rnt(doh, s0)
            dp = jnp.where(causal, _rnt(doh, c["vn"]), 0.0)
            dkd = _rnt(c["vn"], ds1)
            dw = -_rnt(dvn, s0)
            ds_ref[h] = _rtn(c["qg"], doh) - _rtn(c["w"], dvn) + ds1 * c["egl"]
            drv = _tn(c["tinv"], dvn)
            drw = _tn(c["tinv"], dw)
            da = jnp.where(strict, -(_nt(drv, c["u"]) + _nt(drw, c["w"])), 0.0)
            m = da * c["a"] + dp * c["p"]
            dkk = da * c["ds"]
            dqk = dp * c["dm"]
            dq = _rnn(dqk, k) + dqg * c["eg"]
            dkb = _rnn(dkk, k) + drw * c["eg"]
            dk = _rtn(dqk, q) + _rtn(dkk, c["kb"]) + dkd * c["ek"] + dkb * bh
            dv = drv * bh
            dbeta = _rs(drv * v) + _rs(dkb * k)
            t_kd = _rs(dkd * c["kd"])
            dgc = _rs(m) - _rs(m.T) + _rs(dqg * c["qg"]) + _rs(drw * c["rw"]) - t_kd
            tail = jnp.sum(t_kd, axis=0, keepdims=True) + c["egl"] * jnp.sum(_rs(s0 * ds1), axis=0, keepdims=True)
            dgc = dgc + jnp.where(last, tail, 0.0)
            dg = _scan_rows(dgc + jnp.zeros((CH, HD), F32), CH, reverse=True)
            dqs.append(dq)
            dks.append(dk)
            dvs.append(dv)
            dbetas.append(dbeta + jnp.zeros((CH, HD), F32))
            dgs.append(dg)
        dx_ref[...] = jnp.concatenate(dqs + dks + dvs, axis=-1)
        dbeta = jnp.where(valid, jnp.concatenate(dbetas, axis=-1), 0.0)
        dg = jnp.where(valid, jnp.concatenate(dgs, axis=-1), 0.0)
        draw_b = dbeta * beta * (1.0 - beta)
        draw_a = dg * (-jnp.exp(alog)) * _sig(za)
        dba_ref[...] = _nn(jnp.concatenate([draw_b, draw_a], axis=-1), sm_ref[...])
        r8 = _iota2((8, HW), 0)
        upd = jnp.where(r8 == 0, jnp.sum(dg * g, axis=0, keepdims=True), 0.0) + jnp.where(
            r8 == 1, jnp.sum(draw_a, axis=0, keepdims=True), 0.0)
        acc_ref[...] += upd

    return pl.pallas_call(
        body, grid=(nb, nc),
        in_specs=[pl.BlockSpec((CH, 3 * HW), lambda b, c: (rev(b, c), 0)), pl.BlockSpec((CH, HD), lambda b, c: (rev(b, c), cba)),
                  pl.BlockSpec((HD, 2 * HW), lambda b, c: (0, 0)), pl.BlockSpec((2 * HW, HD), lambda b, c: (0, 0)),
                  pl.BlockSpec((1, HW), lambda b, c: (0, 0)), pl.BlockSpec((1, HW), lambda b, c: (0, 0)),
                  pl.BlockSpec((None, None, NH, HD, HD), lambda b, c: (b, nc - 1 - c, 0, 0, 0)),
                  pl.BlockSpec((CH, HW), lambda b, c: (rev(b, c), 0)), pl.BlockSpec(memory_space=pl.ANY)],
        out_specs=[pl.BlockSpec((CH, 3 * HW), lambda b, c: (rev(b, c), 0)), pl.BlockSpec((CH, HD), lambda b, c: (rev(b, c), cba)),
                   pl.BlockSpec((8, HW), lambda b, c: (0, 0))],
        out_shape=[SDS((n, 3 * HW), F32), SDS(dproj.shape, F32), SDS((8, HW), F32)],
        input_output_aliases={8: 1},
        scratch_shapes=[pltpu.VMEM((NH, HD, HD), F32)], compiler_params=_params(2), name="gdn_bwd")(
            qkv, proj, e_mat, s_mat, alog, dtb, states, do, dproj)


def _hgrn_inputs(zq, zf, lb):
    sg = _sig(zf)
    sgn = _sig(-zf)
    pos = lb > 0.0
    lbp = jnp.where(pos, lb, 0.0)
    fpos = lbp + (1.0 - lbp) * sg
    lf = jnp.where(pos, jnp.log(jnp.where(pos, fpos, 1.0)), _logsig(zf))
    k = (1.0 - lbp) * sgn
    q = _silu(zq) * Q_SCALE
    return q, k, lf, sg, sgn, pos, lbp, fpos


def _hgrn_consts():
    i3, j3 = _iota2((SUB, SUB, HD), 0), _iota2((SUB, SUB, HD), 1)
    mask3 = i3 >= j3
    r, c = _iota2((SUB, SUB * SUB), 0), _iota2((SUB, SUB * SUB), 1)
    sel_i = jnp.where(lax.shift_right_logical(c, 4) == r, 1.0, 0.0)
    sel_j = jnp.where(lax.bitwise_and(c, SUB - 1) == r, 1.0, 0.0)
    return mask3, sel_i, sel_j


def _pairs(a, b):
    return (a[:, None, :] * b[None, :, :]).reshape(SUB * SUB, HD)


def _hgrn_sub(q, k, v, bc, st, consts):
    mask3, sel_i, _ = consts
    bl = bc[SUB - 1:SUB, :]
    p3 = jnp.where(mask3, jnp.exp(jnp.where(mask3, bc[:, None, :] - bc[None, :, :], 0.0)), 0.0).reshape(SUB * SUB, HD)
    x = _pairs(q, k) * p3
    srep = _rs(x)
    vt = jnp.broadcast_to(v[None, :, :], (SUB, SUB, HD)).reshape(SUB * SUB, HD)
    eb = jnp.exp(bc)
    qe = q * eb
    o = _rnt(qe, st) + _nn(sel_i, _rr(srep) * _rr(vt))
    ek = jnp.exp(bl - bc)
    kd = k * ek
    ebl = jnp.exp(bl)
    st1 = st * ebl + _rtn(v, kd)
    return o, st1, dict(bc=bc, p3=p3, srep=srep, vt=vt, eb=eb, qe=qe, ek=ek, kd=kd, ebl=ebl)


def _hgrn_fwd(proj, lb, lay, nb, nc):
    n = proj.shape[0]
    cbb = lay.c_b // (3 * HW)

    def body(z_ref, lb_ref, o_ref, so_ref, s_ref):
        @pl.when(pl.program_id(1) == 0)
        def _():
            s_ref[...] = jnp.zeros_like(s_ref)

        consts = _hgrn_consts()
        outs = []
        for h in range(NH):
            hs = slice(h * HD, (h + 1) * HD)
            q, k, lf = _hgrn_inputs(z_ref[:, hs], z_ref[:, HW + h * HD:HW + (h + 1) * HD], lb_ref[:, hs])[:3]
            v = z_ref[:, 2 * HW + h * HD:2 * HW + (h + 1) * HD]
            st = s_ref[h]
            so_ref[h] = st
            bc = _scan_rows(lf, SUB)
            oh = []
            for s in range(CH // SUB):
                rs = slice(s * SUB, (s + 1) * SUB)
                o, st, _ = _hgrn_sub(q[rs], k[rs], v[rs], bc[rs], st, consts)
                oh.append(o)
            s_ref[h] = st
            outs.append(jnp.concatenate(oh, axis=0))
        o_ref[...] = jnp.concatenate(outs, axis=-1)

    return pl.pallas_call(
        body, grid=(nb, nc),
        in_specs=[pl.BlockSpec((CH, 3 * HW), lambda b, c: (b * nc + c, cbb)), pl.BlockSpec((1, HW), lambda b, c: (0, 0))],
        out_specs=[pl.BlockSpec((CH, HW), lambda b, c: (b * nc + c, 0)),
                   pl.BlockSpec((None, None, NH, HD, HD), lambda b, c: (b, c, 0, 0, 0))],
        out_shape=[SDS((n, HW), F32), SDS((nb, nc, NH, HD, HD), F32)],
        scratch_shapes=[pltpu.VMEM((NH, HD, HD), F32)], compiler_params=_params(2), name="hgrn_fwd")(proj, lb)


def _hgrn_bwd(proj, lb, states, do, dproj, lay, nb, nc):
    cbb = lay.c_b // (3 * HW)
    nsub = CH // SUB

    def rev(b, c):
        return b * nc + (nc - 1 - c)

    def body(z_ref, lb_ref, st_ref, do_ref, dp_in, dz_ref, acc_ref, ds_ref):
        ci = pl.program_id(1)

        @pl.when(ci == 0)
        def _():
            ds_ref[...] = jnp.zeros_like(ds_ref)

        @pl.when((ci == 0) & (pl.program_id(0) == 0))
        def _():
            acc_ref[...] = jnp.zeros_like(acc_ref)

        consts = _hgrn_consts()
        _, sel_i, sel_j = consts
        row = _iota2((CH, 1), 0)
        valid = (row >= PAD) | (ci < nc - 1)
        lastrow = _iota2((SUB, 1), 0) == SUB - 1
        dzq, dzf, dzi, dlbs = [], [], [], []
        for h in range(NH):
            hs = slice(h * HD, (h + 1) * HD)
            zq, zf = z_ref[:, hs], z_ref[:, HW + h * HD:HW + (h + 1) * HD]
            q, k, lf, sg, sgn, pos, lbp, fpos = _hgrn_inputs(zq, zf, lb_ref[:, hs])
            v = z_ref[:, 2 * HW + h * HD:2 * HW + (h + 1) * HD]
            doh = do_ref[:, hs]
            sts, fw = [st_ref[h]], []
            bc = _scan_rows(lf, SUB)
            for s in range(nsub):
                rs = slice(s * SUB, (s + 1) * SUB)
                _, st1, c = _hgrn_sub(q[rs], k[rs], v[rs], bc[rs], sts[-1], consts)
                sts.append(st1)
                fw.append(c)
            dst = ds_ref[h]
            dq_l, dk_l, dv_l, dlf_l = [None] * nsub, [None] * nsub, [None] * nsub, [None] * nsub
            for s in reversed(range(nsub)):
                rs = slice(s * SUB, (s + 1) * SUB)
                c, st = fw[s], sts[s]
                qs, ks, vs, dos = q[rs], k[rs], v[rs], doh[rs]
                dqe = _rnn(dos, st)
                dkd = _rnn(vs, dst)
                dsrep = _rs(_pairs(_rr(dos), _rr(vs)))
                w = dsrep * c["p3"]
                kt = jnp.broadcast_to(ks[None, :, :], (SUB, SUB, HD)).reshape(SUB * SUB, HD)
                qt = jnp.broadcast_to(qs[:, None, :], (SUB, SUB, HD)).reshape(SUB * SUB, HD)
                dq_i = _nn(sel_i, w * kt)
                dk_i = _nn(sel_j, w * qt)
                dot = jnp.broadcast_to(_rr(dos)[:, None, :], (SUB, SUB, HD)).reshape(SUB * SUB, HD)
                dvv = _nn(sel_j, _rr(c["srep"]) * dot) + _rnt(c["kd"], dst)
                t_kd = dkd * c["kd"]
                dbc = dqe * c["qe"] - t_kd + qs * dq_i - ks * dk_i
                tail = jnp.sum(t_kd, axis=0, keepdims=True) + c["ebl"] * jnp.sum(st * dst, axis=0, keepdims=True)
                dbc = dbc + jnp.where(lastrow, tail, 0.0)
                dlf_l[s] = dbc
                dq_l[s] = dq_i + dqe * c["eb"]
                dk_l[s] = dk_i + dkd * c["ek"]
                dv_l[s] = dvv
                dst = _rtn(dos, c["qe"]) + dst * c["ebl"]
            ds_ref[h] = dst
            dq, dk, dv, dbc = (jnp.concatenate(t, axis=0) for t in (dq_l, dk_l, dv_l, dlf_l))
            dlf = _scan_rows(dbc, SUB, reverse=True)
            dlft = dlf - dk * (1.0 - k)
            dlf_dz = jnp.where(pos, (1.0 - lbp) * sg * sgn / jnp.where(pos, fpos, 1.0), sgn)
            dlf_dlb = jnp.where(pos, sgn / jnp.where(pos, fpos, 1.0), 0.0)
            dzq.append(dq * Q_SCALE * _dsilu(zq))
            dzf.append(dlft * dlf_dz)
            dzi.append(dv)
            dlbs.append(jnp.sum(jnp.where(valid, dlft * dlf_dlb, 0.0), axis=0, keepdims=True))
        dz_ref[...] = jnp.concatenate(dzq + dzf + dzi, axis=-1)
        acc_ref[...] += jnp.where(_iota2((8, HW), 0) == 0, jnp.concatenate(dlbs, axis=-1), 0.0)

    return pl.pallas_call(
        body, grid=(nb, nc),
        in_specs=[pl.BlockSpec((CH, 3 * HW), lambda b, c: (rev(b, c), cbb)), pl.BlockSpec((1, HW), lambda b, c: (0, 0)),
                  pl.BlockSpec((None, None, NH, HD, HD), lambda b, c: (b, nc - 1 - c, 0, 0, 0)),
                  pl.BlockSpec((CH, HW), lambda b, c: (rev(b, c), 0)), pl.BlockSpec(memory_space=pl.ANY)],
        out_specs=[pl.BlockSpec((CH, 3 * HW), lambda b, c: (rev(b, c), cbb)), pl.BlockSpec((8, HW), lambda b, c: (0, 0))],
        out_shape=[SDS(dproj.shape, F32), SDS((8, HW), F32)],
        input_output_aliases={4: 0},
        scratch_shapes=[pltpu.VMEM((NH, HD, HD), F32)], compiler_params=_params(2), name="hgrn_bwd")(proj, lb, states, do, dproj)


def _gated_norm(o, z, gamma):
    ys, ns, rs = [], [], []
    for h in range(NH):
        hs = slice(h * HD, (h + 1) * HD)
        oh = o[:, hs]
        r = lax.rsqrt(jnp.mean(oh * oh, axis=-1, keepdims=True) + EPS)
        nh = oh * r
        ys.append(nh * gamma * _silu(z[:, hs]))
        ns.append(nh)
        rs.append(r)
    return jnp.concatenate(ys, axis=-1), ns, rs


def _merge_fwd(h, oa, ob, proj, ga, gb, wa, wb, wo, lay):
    n, d = h.shape
    tm = _tile(n, 384)
    wm = lay.wm

    def body(h_ref, oa_ref, ob_ref, p_ref, ga_ref, gb_ref, wa_ref, wb_ref, wo_ref, out_ref):
        ya, _, _ = _gated_norm(oa_ref[...], p_ref[:, 0:HW], ga_ref[...])
        yb, _, _ = _gated_norm(ob_ref[...], p_ref[:, HW:2 * HW], gb_ref[...])
        ya2 = _bnn(ya, wa_ref[...])
        yb2 = _bnn(yb, wb_ref[...])
        mixed = _sig(p_ref[:, 2 * HW:2 * HW + d]) * ya2 + _sig(p_ref[:, 2 * HW + d:2 * HW + 2 * d]) * yb2
        out_ref[...] = h_ref[...] + _bnn(mixed, wo_ref[...])

    full = lambda shape: pl.BlockSpec(shape, lambda i: (0, 0))
    return pl.pallas_call(
        body, grid=(n // tm,),
        in_specs=[pl.BlockSpec((tm, d), lambda i: (i, 0)), pl.BlockSpec((tm, HW), lambda i: (i, 0)),
                  pl.BlockSpec((tm, HW), lambda i: (i, 0)), pl.BlockSpec((tm, wm), lambda i: (i, 0)),
                  full((1, HD)), full((1, HD)), full((HW, d)), full((HW, d)), full((d, d))],
        out_specs=pl.BlockSpec((tm, d), lambda i: (i, 0)), out_shape=SDS((n, d), F32),
        compiler_params=_params(1), name="merge_fwd")(h, oa, ob, proj, ga, gb, wa, wb, wo)


def _gated_norm_bwd(dy, o, z, gamma):
    dos, dzs = [], []
    dgam = jnp.zeros((1, HD), F32)
    for h in range(NH):
        hs = slice(h * HD, (h + 1) * HD)
        oh, zh, dyh = o[:, hs], z[:, hs], dy[:, hs]
        r = lax.rsqrt(jnp.mean(oh * oh, axis=-1, keepdims=True) + EPS)
        nh = oh * r
        dzs.append(dyh * nh * gamma * _dsilu(zh))
        dng = dyh * _silu(zh)
        dgam = dgam + jnp.sum(dng * nh, axis=0, keepdims=True)
        dn = dng * gamma
        dos.append(r * (dn - nh * jnp.mean(dn * nh, axis=-1, keepdims=True)))
    return jnp.concatenate(dos, axis=-1), jnp.concatenate(dzs, axis=-1), dgam


def _merge_bwd(dhn, oa, ob, proj, ga, gb, wa, wb, wo, lay, tp):
    n, d = dhn.shape
    tm = _tile(n, 128)
    wm = lay.wm

    def body(dh_ref, oa_ref, ob_ref, p_ref, ga_ref, gb_ref, wa_ref, wb_ref, wo_ref,
             dp_ref, doa_ref, dob_ref, dwa_ref, dwb_ref, dwo_ref, dga_ref, dgb_ref):
        i = pl.program_id(0)

        @pl.when(i == 0)
        def _():
            for r in (dwa_ref, dwb_ref, dwo_ref, dga_ref, dgb_ref):
                r[...] = jnp.zeros_like(r)

        dh = jnp.where(_row_valid(tm, tp, i * tm), dh_ref[...], 0.0)
        oa, ob = oa_ref[...], ob_ref[...]
        za, zb = p_ref[:, 0:HW], p_ref[:, HW:2 * HW]
        gta, gtb = p_ref[:, 2 * HW:2 * HW + d], p_ref[:, 2 * HW + d:2 * HW + 2 * d]
        ya, _, _ = _gated_norm(oa, za, ga_ref[...])
        yb, _, _ = _gated_norm(ob, zb, gb_ref[...])
        ya2 = _bnn(ya, wa_ref[...])
        yb2 = _bnn(yb, wb_ref[...])
        sa, sb = _sig(gta), _sig(gtb)
        mixed = sa * ya2 + sb * yb2
        dmixed = _bnt(dh, wo_ref[...])
        dwo_ref[...] += _btn(mixed, dh)
        dya2 = dmixed * sa
        dyb2 = dmixed * sb
        dwa_ref[...] += _btn(ya, dya2)
        dwb_ref[...] += _btn(yb, dyb2)
        doa, dza, dga = _gated_norm_bwd(_bnt(dya2, wa_ref[...]), oa, za, ga_ref[...])
        dob, dzb, dgb = _gated_norm_bwd(_bnt(dyb2, wb_ref[...]), ob, zb, gb_ref[...])
        dga_ref[...] += dga
        dgb_ref[...] += dgb
        doa_ref[...] = doa
        dob_ref[...] = dob
        dp_ref[:, 0:HW] = dza
        dp_ref[:, HW:2 * HW] = dzb
        dp_ref[:, 2 * HW:2 * HW + d] = dmixed * ya2 * sa * (1.0 - sa)
        dp_ref[:, 2 * HW + d:2 * HW + 2 * d] = dmixed * yb2 * sb * (1.0 - sb)

    full = lambda shape: pl.BlockSpec(shape, lambda i: (0, 0))
    rows = lambda w: pl.BlockSpec((tm, w), lambda i: (i, 0))
    return pl.pallas_call(
        body, grid=(n // tm,),
        in_specs=[rows(d), rows(HW), rows(HW), rows(wm), full((1, HD)), full((1, HD)), full((HW, d)), full((HW, d)), full((d, d))],
        out_specs=[rows(wm), rows(HW), rows(HW), full((HW, d)), full((HW, d)), full((d, d)), full((1, HD)), full((1, HD))],
        out_shape=[SDS((n, lay.pw), F32), SDS((n, HW), F32), SDS((n, HW), F32), SDS((HW, d), F32), SDS((HW, d), F32),
                   SDS((d, d), F32), SDS((1, HD), F32), SDS((1, HD), F32)],
        compiler_params=_params(1), name="merge_bwd")(dhn, oa, ob, proj, ga, gb, wa, wb, wo)


def _loss_head(h, target, fw, nb, nc):
    n, d = h.shape

    def body(h_ref, t_ref, fw_ref, lp_ref, dh_ref, dfw_ref):
        b, c = pl.program_id(0), pl.program_id(1)

        @pl.when((b == 0) & (c == 0))
        def _():
            dfw_ref[...] = jnp.zeros_like(dfw_ref)

        @pl.when(c == 0)
        def _():
            dh_ref[...] = jnp.zeros_like(dh_ref)
            lp_ref[...] = jnp.zeros_like(lp_ref)

        @pl.when(c > 0)
        def _():
            x = h_ref[...]
            r = lax.rsqrt(jnp.mean(x * x, axis=-1, keepdims=True) + EPS)
            xh = x * r
            err = xh * fw_ref[...] - t_ref[...]
            lp_ref[...] = jnp.zeros_like(lp_ref) + 0.5 * jnp.sum(_rs(err * err), axis=0, keepdims=True) / d
            dy = err / d
            dfw_ref[...] += jnp.sum(dy * xh, axis=0, keepdims=True)
            dxh = dy * fw_ref[...]
            dh_ref[...] = r * (dxh - xh * jnp.mean(dxh * xh, axis=-1, keepdims=True))

    return pl.pallas_call(
        body, grid=(nb, nc),
        in_specs=[pl.BlockSpec((CH, d), lambda b, c: (b * nc + c, 0)),
                  pl.BlockSpec((CH, d), lambda b, c: (b * (nc - 1) + jnp.maximum(c - 1, 0), 0)),
                  pl.BlockSpec((1, d), lambda b, c: (0, 0))],
        out_specs=[pl.BlockSpec((8, HD), lambda b, c: (b * nc + c, 0)), pl.BlockSpec((CH, d), lambda b, c: (b * nc + c, 0)),
                   pl.BlockSpec((1, d), lambda b, c: (0, 0))],
        out_shape=[SDS((nb * nc * 8, HD), F32), SDS((n, d), F32), SDS((1, d), F32)],
        compiler_params=_params(2), name="loss_head")(h, target, fw)


def _lb_fwd(lb):
    def body(x_ref, o_ref):
        x = x_ref[...]
        mx = jnp.max(x, axis=0, keepdims=True)
        e = jnp.exp(x - mx)
        sm = e / jnp.sum(e, axis=0, keepdims=True)
        run = jnp.zeros((1, HW), F32)
        for l in range(DEPTH):
            run = run + sm[l:l + 1, :]
            o_ref[l:l + 1, :] = run - sm[0:1, :]

    return pl.pallas_call(body, out_shape=SDS(lb.shape, F32), name="lb_fwd")(lb)


def _lb_bwd(lb, dlb_all):
    def body(x_ref, d_ref, o_ref):
        x = x_ref[...]
        dl = d_ref[...]
        mx = jnp.max(x, axis=0, keepdims=True)
        e = jnp.exp(x - mx)
        sm = e / jnp.sum(e, axis=0, keepdims=True)
        tot = jnp.sum(dl, axis=0, keepdims=True)
        dsm = []
        run = tot
        for l in range(DEPTH):
            dsm.append(run - (tot if l == 0 else 0.0))
            run = run - dl[l:l + 1, :]
        inner = sum(sm[l:l + 1, :] * dsm[l] for l in range(DEPTH))
        for l in range(DEPTH):
            o_ref[l:l + 1, :] = sm[l:l + 1, :] * (dsm[l] - inner)

    return pl.pallas_call(body, out_shape=SDS(lb.shape, F32), name="lb_bwd")(lb, dlb_all)


def _adamw(g, w, m, v):
    r, c = g.shape
    tr = _tile(r, 264)
    c1 = 1.0 / (1.0 - ADAM_B1 ** ADAM_STEP)
    c2 = 1.0 / (1.0 - ADAM_B2 ** ADAM_STEP)

    def body(g_ref, w_ref, m_ref, v_ref, d_ref, mo_ref, vo_ref):
        gg = g_ref[...]
        mn = ADAM_B1 * m_ref[...] + (1.0 - ADAM_B1) * gg
        vn = ADAM_B2 * v_ref[...] + (1.0 - ADAM_B2) * gg * gg
        d_ref[...] = -ADAM_LR * ((mn * c1) / (jnp.sqrt(vn * c2) + ADAM_EPS) + ADAM_WD * w_ref[...])
        mo_ref[...] = mn
        vo_ref[...] = vn

    spec = pl.BlockSpec((tr, c), lambda i: (i, 0))
    return pl.pallas_call(body, grid=(r // tr,), in_specs=[spec] * 4, out_specs=[spec] * 3, out_shape=[SDS(g.shape, F32)] * 3,
                          compiler_params=_params(1), name="adamw")(g, w, m, v)


def _add_rows(a, b, name):
    k, r, c = a.shape
    tr = _tile(r, 264)
    spec = pl.BlockSpec((None, tr, c), lambda s, i: (s, i, 0))

    def body(a_ref, b_ref, o_ref):
        o_ref[...] = a_ref[...] + b_ref[...]

    return pl.pallas_call(body, grid=(k, r // tr), in_specs=[spec, spec], out_specs=spec, out_shape=SDS(a.shape, F32),
                          compiler_params=_params(2), name=name)(a, b)


def _sum_chips(parts):
    k, r, c = parts.shape
    tr = _tile(r, 264)

    def body(p_ref, o_ref):
        acc = p_ref[0]
        for s in range(1, k):
            acc = acc + p_ref[s]
        o_ref[...] = acc

    return pl.pallas_call(body, grid=(r // tr,), in_specs=[pl.BlockSpec((k, tr, c), lambda i: (0, i, 0))],
                          out_specs=pl.BlockSpec((tr, c), lambda i: (i, 0)), out_shape=SDS((r, c), F32),
                          compiler_params=_params(1), name="sum_chips")(parts)


def _meta_grad(dh, nb, nc):
    d = dh.shape[1]

    def body(x_ref, o_ref):
        @pl.when(pl.program_id(0) == 0)
        def _():
            o_ref[...] = jnp.zeros_like(o_ref)

        o_ref[...] += x_ref[PAD:CH, :]

    return pl.pallas_call(body, grid=(nb,), in_specs=[pl.BlockSpec((CH, d), lambda b: (b * nc, 0))],
                          out_specs=pl.BlockSpec((N_META, d), lambda b: (0, 0)), out_shape=SDS((N_META, d), F32),
                          compiler_params=_params(1), name="meta_grad")(dh)


ANY = pl.BlockSpec(memory_space=pl.ANY)


def _place():
    x, y, c = lax.axis_index("x"), lax.axis_index("y"), lax.axis_index("c")
    chips = [(1 - x, y), (x, 1 - y), (1 - x, 1 - y)]
    return x, y, c, chips


def _remote(src, dst, send_sems, recv_sems, k, to):
    return pltpu.make_async_remote_copy(src_ref=src, dst_ref=dst, send_sem=send_sems.at[k], recv_sem=recv_sems.at[k],
                                        device_id=to, device_id_type=MESH)


def _gather_weights(pb, ps):
    def body(pb_ref, ps_ref, gb_ref, gs_ref, send_sems, recv_sems, local_sems):
        x, y, c, chips = _place()
        s = 2 * x + y
        sib = (x, y, 1 - c)
        l0 = pltpu.make_async_copy(pb_ref, gb_ref.at[s], local_sems.at[0])
        l1 = pltpu.make_async_copy(ps_ref, gs_ref.at[s], local_sems.at[1])
        l0.start()
        l1.start()
        sends = []
        for k, (px, py) in enumerate(chips):
            sends.append(_remote(pb_ref.at[c], gb_ref.at[s, c], send_sems, recv_sems, k, (px, py, c)))
            sends.append(_remote(ps_ref, gs_ref.at[s], send_sems, recv_sems, 6 + k, (px, py, c)))
        for cp in sends:
            cp.start()
        for k, (px, py) in enumerate(chips):
            sk = 2 * px + py
            _remote(pb_ref.at[c], gb_ref.at[sk, c], send_sems, recv_sems, k, sib).wait_recv()
            fwd = _remote(gb_ref.at[sk, c], gb_ref.at[sk, c], send_sems, recv_sems, 3 + k, sib)
            fwd.start()
            sends.append(fwd)
        for k, (px, py) in enumerate(chips):
            sk = 2 * px + py
            _remote(pb_ref.at[c], gb_ref.at[sk, 1 - c], send_sems, recv_sems, 3 + k, sib).wait_recv()
            _remote(ps_ref, gs_ref.at[sk], send_sems, recv_sems, 6 + k, sib).wait_recv()
        for cp in sends:
            cp.wait_send()
        l0.wait()
        l1.wait()

    return pl.pallas_call(
        body, in_specs=[ANY, ANY], out_specs=[ANY, ANY],
        out_shape=[SDS((4,) + pb.shape, pb.dtype), SDS((4,) + ps.shape, ps.dtype)],
        scratch_shapes=[pltpu.SemaphoreType.DMA((9,)), pltpu.SemaphoreType.DMA((9,)), pltpu.SemaphoreType.DMA((2,))],
        name="gather_weights")(pb, ps)


def _swap_halves(g):
    def body(g_ref, own_ref, got_ref, send_sems, recv_sems, local_sems):
        x, y, c, _ = _place()
        sib = (x, y, 1 - c)
        l0 = pltpu.make_async_copy(g_ref.at[c], own_ref, local_sems.at[0])
        l0.start()
        cp = _remote(g_ref.at[1 - c], got_ref, send_sems, recv_sems, 0, sib)
        cp.start()
        cp.wait()
        l0.wait()

    return pl.pallas_call(
        body, in_specs=[ANY], out_specs=[ANY, ANY], out_shape=[SDS(g.shape[1:], g.dtype)] * 2,
        scratch_shapes=[pltpu.SemaphoreType.DMA((1,)), pltpu.SemaphoreType.DMA((1,)), pltpu.SemaphoreType.DMA((1,))],
        name="swap_halves")(g)


def _scatter_chip_sums(a):
    def body(a_ref, r_ref, send_sems, recv_sems, local_sems):
        x, y, c, chips = _place()
        s = 2 * x + y
        l0 = pltpu.make_async_copy(a_ref.at[s], r_ref.at[s], local_sems.at[0])
        l0.start()
        sends = [_remote(a_ref.at[2 * px + py], r_ref.at[s], send_sems, recv_sems, k, (px, py, c))
                 for k, (px, py) in enumerate(chips)]
        for cp in sends:
            cp.start()
        for k, (px, py) in enumerate(chips):
            _remote(a_ref.at[s], r_ref.at[2 * px + py], send_sems, recv_sems, k, (px, py, c)).wait_recv()
        for cp in sends:
            cp.wait_send()
        l0.wait()

    return pl.pallas_call(
        body, in_specs=[ANY], out_specs=ANY, out_shape=SDS(a.shape, a.dtype),
        scratch_shapes=[pltpu.SemaphoreType.DMA((3,)), pltpu.SemaphoreType.DMA((3,)), pltpu.SemaphoreType.DMA((1,))],
        name="scatter_chip_sums")(a)


def _join_halves(t):
    def body(t_ref, f_ref, send_sems, recv_sems, local_sems):
        x, y, c, _ = _place()
        sib = (x, y, 1 - c)
        l0 = pltpu.make_async_copy(t_ref, f_ref.at[c], local_sems.at[0])
        l0.start()
        cp = _remote(t_ref, f_ref.at[c], send_sems, recv_sems, 0, sib)
        cp.start()
        cp.wait_send()
        _remote(t_ref, f_ref.at[1 - c], send_sems, recv_sems, 0, sib).wait_recv()
        l0.wait()

    return pl.pallas_call(
        body, in_specs=[ANY], out_specs=ANY, out_shape=SDS((2,) + t.shape, t.dtype),
        scratch_shapes=[pltpu.SemaphoreType.DMA((1,)), pltpu.SemaphoreType.DMA((1,)), pltpu.SemaphoreType.DMA((1,))],
        name="join_halves")(t)


WEIGHTS = ("meta_tokens", "norm_w", "w_in", "conv_w", "a_log", "dt_bias", "gnorm_a", "gnorm_b", "hgrn_lower_bounds",
           "w_branch_a", "w_branch_b", "w_out", "final_norm_w")
SHARD_AXIS = {"meta_tokens": 1, "w_in": 2, "conv_w": 2, "w_branch_a": 2, "w_branch_b": 2, "w_out": 1}
FLAT_C = 1024


def _flat(parts, rows):
    v = jnp.concatenate([p.reshape(-1) for p in parts])
    return jnp.pad(v, (0, rows * FLAT_C - v.shape[0])).reshape(rows, FLAT_C)


def _unflat(flat, like):
    v = flat.reshape(-1)
    out, o = {}, 0
    for n in WEIGHTS:
        sz = int(np.prod(like[n].shape))
        out[n] = v[o:o + sz].reshape(like[n].shape)
        o += sz
    return out


def _local_step(x, target, w, lay):
    nb, seq, d = x.shape
    tp = CH + seq
    nc = tp // CH
    n = nb * tp
    e_mat, s_mat = _gate_consts()
    lb_all = _lb_fwd(w["hgrn_lower_bounds"])
    h = jnp.concatenate([jnp.zeros((nb, PAD, d), F32), jnp.broadcast_to(w["meta_tokens"][None], (nb, N_META, d)), x],
                        axis=1).reshape(n, d)
    rep = lambda a: jnp.repeat(a, HD)[None, :]
    saved = []
    for l in range(DEPTH):
        nw = w["norm_w"][l][None, :]
        proj = _norm_proj_fwd(h, nw, w["w_in"][l])
        qkv = _gdn_prep_fwd(proj, w["conv_w"][l], lay, nb, tp)
        alog, dtb = rep(w["a_log"][l]), rep(w["dt_bias"][l])
        oa, sa = _gdn_fwd(qkv, proj, e_mat, alog, dtb, lay, nb, nc)
        lbl = lb_all[l][None, :]
        ob, sb = _hgrn_fwd(proj, lbl, lay, nb, nc)
        ga, gb = w["gnorm_a"][l][None, :], w["gnorm_b"][l][None, :]
        hn = _merge_fwd(h, oa, ob, proj, ga, gb, w["w_branch_a"][l], w["w_branch_b"][l], w["w_out"][l], lay)
        saved.append((h, nw, proj, qkv, alog, dtb, oa, sa, lbl, ob, sb, ga, gb))
        h = hn
    lp, dh, dfw = _loss_head(h, target.reshape(nb * seq, d), w["final_norm_w"][None, :], nb, nc)
    loss = jnp.sum(lp[::8, 0])
    g = {n_: [None] * DEPTH for n_ in WEIGHTS}
    dlb_all = [None] * DEPTH
    for l in reversed(range(DEPTH)):
        h, nw, proj, qkv, alog, dtb, oa, sa, lbl, ob, sb, ga, gb = saved[l]
        dproj, doa, dob, dwa, dwb, dwo, dga, dgb = _merge_bwd(dh, oa, ob, proj, ga, gb, w["w_branch_a"][l],
                                                             w["w_branch_b"][l], w["w_out"][l], lay, tp)
        dproj, acc_b = _hgrn_bwd(proj, lbl, sb, dob, dproj, lay, nb, nc)
        dqkv, dproj, acc_a = _gdn_bwd(qkv, proj, e_mat, s_mat, alog, dtb, sa, doa, dproj, lay, nb, nc)
        dproj, dconv = _gdn_prep_bwd(proj, w["conv_w"][l], dqkv, dproj, lay, nb, tp)
        dh, dnw = _proj_bwd_dx(dproj, w["w_in"][l], h, nw, dh, tp)
        g["w_in"][l] = _proj_bwd_dw(dproj, h, nw, tp)
        g["norm_w"][l] = dnw[0]
        g["conv_w"][l] = dconv
        g["a_log"][l] = acc_a[0, ::HD]
        g["dt_bias"][l] = acc_a[1, ::HD]
        g["gnorm_a"][l], g["gnorm_b"][l] = dga[0], dgb[0]
        g["w_branch_a"][l], g["w_branch_b"][l], g["w_out"][l] = dwa, dwb, dwo
        dlb_all[l] = acc_b[0]
    grads = {n_: jnp.stack(v) for n_, v in g.items() if v[0] is not None}
    grads["hgrn_lower_bounds"] = _lb_bwd(w["hgrn_lower_bounds"], jnp.stack(dlb_all))
    grads["final_norm_w"] = dfw[0]
    grads["meta_tokens"] = _meta_grad(dh, nb, nc)
    grad_x = dh.reshape(nb, tp, d)[:, CH:, :]
    return loss, grad_x, grads


def kernel(x, meta_tokens, norm_w, w_in, conv_w, a_log, dt_bias, gnorm_a, gnorm_b, hgrn_lower_bounds, w_branch_a, w_branch_b, w_out, final_norm_w, loss_target, m_meta_tokens, m_norm_w, m_w_in, m_conv_w, m_a_log, m_dt_bias, m_gnorm_a, m_gnorm_b, m_hgrn_lower_bounds, m_w_branch_a, m_w_branch_b, m_w_out, m_final_norm_w, v_meta_tokens, v_norm_w, v_w_in, v_conv_w, v_a_log, v_dt_bias, v_gnorm_a, v_gnorm_b, v_hgrn_lower_bounds, v_w_branch_a, v_w_branch_b, v_w_out, v_final_norm_w):
    wl = dict(meta_tokens=meta_tokens, norm_w=norm_w, w_in=w_in, conv_w=conv_w, a_log=a_log, dt_bias=dt_bias, gnorm_a=gnorm_a,
              gnorm_b=gnorm_b, hgrn_lower_bounds=hgrn_lower_bounds, w_branch_a=w_branch_a, w_branch_b=w_branch_b, w_out=w_out,
              final_norm_w=final_norm_w)
    ml = dict(zip(WEIGHTS, (m_meta_tokens, m_norm_w, m_w_in, m_conv_w, m_a_log, m_dt_bias, m_gnorm_a, m_gnorm_b,
                            m_hgrn_lower_bounds, m_w_branch_a, m_w_branch_b, m_w_out, m_final_norm_w)))
    vl = dict(zip(WEIGHTS, (v_meta_tokens, v_norm_w, v_w_in, v_conv_w, v_a_log, v_dt_bias, v_gnorm_a, v_gnorm_b,
                            v_hgrn_lower_bounds, v_w_branch_a, v_w_branch_b, v_w_out, v_final_norm_w)))
    d = x.shape[2]
    lay = _Layout(d)
    nchip = 4

    big = ("w_in", "w_branch_a", "w_branch_b", "w_out")
    small = ("conv_w", "meta_tokens")
    nbig = sum(int(np.prod(wl[n].shape)) for n in big)
    rb = -(-nbig // (2 * FLAT_C * 16)) * 16
    pb = _flat([wl[n].astype(BF16) for n in big], 2 * rb).reshape(2, rb, FLAT_C)
    nsmall = sum(int(np.prod(wl[n].shape)) for n in small)
    rs = -(-nsmall // (HD * 8)) * 8
    ps = jnp.pad(jnp.concatenate([wl[n].reshape(-1) for n in small]), (0, rs * HD - nsmall)).reshape(rs, HD)
    gbig, gsmall = _gather_weights(pb, ps)
    gbig = gbig.reshape(nchip, -1)
    gsmall = gsmall.reshape(nchip, -1)

    def whole(flat, names, src):
        out, o = {}, 0
        for n in names:
            shp = src[n].shape
            sz = int(np.prod(shp))
            a = flat[:, o:o + sz].reshape((nchip,) + shp)
            ax = SHARD_AXIS[n]
            out[n] = jnp.concatenate([a[s] for s in range(nchip)], axis=ax)
            o += sz
        return out

    wf = dict(wl)
    wf.update(whole(gbig, big, wl))
    wf.update(whole(gsmall, small, wl))
    wf["w_in"] = lay.to_kernel(wf["w_in"])

    loss_part, grad_x, gfull = _local_step(x, loss_target, wf, lay)
    loss = lax.psum(loss_part, ("x", "y", "c"))
    gfull["w_in"] = lay.from_kernel(gfull["w_in"])

    nloc = sum(int(np.prod(wl[n].shape)) for n in WEIGHTS)
    nblk = 8
    tr = 8 * (-(-nloc // (2 * FLAT_C * 8 * nblk)))
    rh = tr * nblk
    rows = []
    for s in range(nchip):
        parts = []
        for n in WEIGHTS:
            gfn = gfull[n]
            if n in SHARD_AXIS:
                sz = wl[n].shape[SHARD_AXIS[n]]
                gfn = lax.slice_in_dim(gfn, s * sz, (s + 1) * sz, axis=SHARD_AXIS[n])
            parts.append(gfn)
        rows.append(_flat(parts, 2 * rh).reshape(2, rh, FLAT_C))
    gsend = jnp.stack(rows, axis=1)
    own, got = _swap_halves(gsend)
    chip_sums = _add_rows(own, got, "add_cores")
    by_chip = _scatter_chip_sums(chip_sums)
    mine = _sum_chips(by_chip)
    gflat = _join_halves(mine).reshape(2 * rh, FLAT_C)

    wflat = _flat([wl[n] for n in WEIGHTS], 2 * rh)
    mflat = _flat([ml[n] for n in WEIGHTS], 2 * rh)
    vflat = _flat([vl[n] for n in WEIGHTS], 2 * rh)
    delta, mnew, vnew = _adamw(gflat, wflat, mflat, vflat)
    go, do, mo, vo = (_unflat(f, wl) for f in (gflat, delta, mnew, vnew))
    return (loss, grad_x, *[go[n] for n in WEIGHTS], *[do[n] for n in WEIGHTS], *[mo[n] for n in WEIGHTS],
            *[vo[n] for n in WEIGHTS])
```

```python
import functools

import numpy as np
import jax
import jax.numpy as jnp
from jax import lax
from jax.experimental import pallas as pl
from jax.experimental.pallas import tpu as pltpu

F32 = jnp.float32
BF16 = jnp.bfloat16
HI = lax.Precision.HIGHEST
SDS = jax.ShapeDtypeStruct

NH = 4
HD = 128
HW = NH * HD
N_META = 16
CH = 64
SUB = 16
PAD = CH - N_META
EPS = 1e-6
Q_SCALE = HD ** -0.5
DEPTH = 2
CONV_K = 4
VMEM_LIMIT = 56 * 1024 * 1024
ADAM_LR, ADAM_B1, ADAM_B2, ADAM_EPS, ADAM_WD, ADAM_STEP = 0.001, 0.9, 0.999, 1e-08, 0.01, 10
MESH = pl.DeviceIdType.MESH


def _nn(a, b):
    return jnp.dot(a, b, precision=HI, preferred_element_type=F32)


def _nt(a, b):
    return lax.dot_general(a, b, (((1,), (1,)), ((), ())), precision=HI, preferred_element_type=F32)


def _tn(a, b):
    return _nn(a.T, b)


def _scan_rows(x, group, reverse=False):
    n = x.shape[0]
    pos = lax.bitwise_and(_iota2(x.shape, 0), group - 1)
    s = 1
    while s < group:
        if reverse:
            x = x + jnp.where(pos < group - s, pltpu.roll(x, n - s, axis=0), 0.0)
        else:
            x = x + jnp.where(pos >= s, pltpu.roll(x, s, axis=0), 0.0)
        s *= 2
    return x


def _bnn(a, b):
    return jnp.dot(a.astype(BF16), b.astype(BF16), preferred_element_type=F32)


def _bnt(a, b):
    return lax.dot_general(a.astype(BF16), b.astype(BF16), (((1,), (1,)), ((), ())), preferred_element_type=F32)


def _btn(a, b):
    return lax.dot_general(a.astype(BF16), b.astype(BF16), (((0,), (0,)), ((), ())), preferred_element_type=F32)


_rnn, _rnt, _rtn = _nn, _nt, _tn


def _rr(x):
    return x


def _sig(x):
    return jax.nn.sigmoid(x)


def _silu(x):
    return x * _sig(x)


def _dsilu(x):
    s = _sig(x)
    return s * (1.0 + x * (1.0 - s))


def _softplus(x):
    return jnp.maximum(x, 0.0) + jnp.log(1.0 + jnp.exp(-jnp.abs(x)))


def _logsig(x):
    return jnp.minimum(x, 0.0) - jnp.log(1.0 + jnp.exp(-jnp.abs(x)))


def _rs(x):
    return jnp.sum(x, axis=-1, keepdims=True)


def _params(n_axes):
    return pltpu.CompilerParams(dimension_semantics=("arbitrary",) * n_axes, vmem_limit_bytes=VMEM_LIMIT)


def _tile(n, target):
    best = 8
    for t in range(8, target + 1, 8):
        if n % t == 0:
            best = t
    return best


def _ctile(pw, most=7):
    return HD * max(k for k in range(1, most + 1) if (pw // HD) % k == 0)


def _iota2(shape, axis):
    return lax.broadcasted_iota(jnp.int32, shape, axis)


class _Layout:
    def __init__(self, d):
        self.d = d
        self.wm = 2 * HW + 2 * d
        self.c_qkv = self.wm
        self.c_b = self.wm + 3 * HW
        self.c_ba = self.wm + 6 * HW
        self.pw = self.c_ba + HD
        assert self.c_b % (3 * HW) == 0
        o = 0
        segs = {}
        for name, w in (("a_q", HW), ("a_k", HW), ("a_v", HW), ("ba", 2 * NH), ("a_z", HW), ("b_q", HW), ("b_f", HW),
                        ("b_i", HW), ("b_g", HW), ("gate_a", d), ("gate_b", d)):
            segs[name] = (o, o + w)
            o += w
        self.segs = segs
        self.width = o
        self.order = ("a_z", "b_g", "gate_a", "gate_b", "a_q", "a_k", "a_v", "b_q", "b_f", "b_i", "ba")

    def to_kernel(self, w):
        parts = [w[..., self.segs[n][0]:self.segs[n][1]] for n in self.order]
        parts.append(jnp.zeros(w.shape[:-1] + (HD - 2 * NH,), w.dtype))
        return jnp.concatenate(parts, axis=-1)

    def containers(self, g, nchip):
        off, where = 0, {}
        for n in self.order:
            where[n] = off
            off += self.segs[n][1] - self.segs[n][0]
        names = sorted(self.segs, key=lambda n: self.segs[n][0])
        sw = self.width // nchip
        cw = -(-sw // HD) * HD
        out, heads = [], []
        for s in range(nchip):
            lo, hi = s * sw, (s + 1) * sw
            pieces = []
            for n in names:
                a, b = max(lo, self.segs[n][0]), min(hi, self.segs[n][1])
                if a < b:
                    pieces.append((where[n] + a - self.segs[n][0], b - a))
            start, width = pieces[0]
            n_head = min((-start) % HD, width)
            body = ([(start + n_head, width - n_head)] if width > n_head else []) + pieces[1:]
            parts = [g[..., c:c + w] for c, w in body]
            parts.append(jnp.zeros(g.shape[:-1] + (cw - sw,), g.dtype))
            if n_head:
                parts.append(g[..., start:start + n_head])
            out.append(jnp.concatenate(parts, axis=-1))
            heads.append(n_head)
        return out, heads

    def from_kernel(self, g):
        off, where = 0, {}
        for n in self.order:
            w = self.segs[n][1] - self.segs[n][0]
            where[n] = (off, off + w)
            off += w
        names = sorted(self.segs, key=lambda n: self.segs[n][0])
        return jnp.concatenate([g[..., where[n][0]:where[n][1]] for n in names], axis=-1)


def _norm_proj_fwd(h, nw, wp):
    n, d = h.shape
    pw = wp.shape[1]
    tm, tn = _tile(n, 768), _ctile(pw)

    def body(h_ref, nw_ref, w_ref, o_ref, xn_ref):
        @pl.when(pl.program_id(1) == 0)
        def _():
            x = h_ref[...]
            r = lax.rsqrt(jnp.mean(x * x, axis=-1, keepdims=True) + EPS)
            xn_ref[...] = (x * r * nw_ref[...]).astype(BF16)

        o_ref[...] = jnp.dot(xn_ref[...], w_ref[...], preferred_element_type=F32)

    return pl.pallas_call(
        body, grid=(n // tm, pw // tn),
        in_specs=[pl.BlockSpec((tm, d), lambda i, j: (i, 0)), pl.BlockSpec((1, d), lambda i, j: (0, 0)),
                  pl.BlockSpec((d, tn), lambda i, j: (0, j))],
        out_specs=pl.BlockSpec((tm, tn), lambda i, j: (i, j)), out_shape=SDS((n, pw), F32),
        scratch_shapes=[pltpu.VMEM((tm, d), BF16)], compiler_params=_params(2), name="norm_proj_fwd")(h, nw, wp)


def _row_valid(tm, tp, base):
    row = base + _iota2((tm, 1), 0)
    return lax.rem(row, tp) >= PAD


def _proj_bwd_dx(dproj, wp, h, nw, dhn, tp):
    n, d = h.shape
    pw = wp.shape[1]
    tm, tk = _tile(n, 768), _ctile(pw)
    nk = pw // tk

    def body(dp_ref, w_ref, h_ref, nw_ref, dhn_ref, dh_ref, dnw_ref, acc_ref):
        i, k = pl.program_id(0), pl.program_id(1)

        @pl.when(k == 0)
        def _():
            acc_ref[...] = jnp.zeros_like(acc_ref)

        @pl.when((i == 0) & (k == 0))
        def _():
            dnw_ref[...] = jnp.zeros_like(dnw_ref)

        valid = _row_valid(tm, tp, i * tm)
        dp = jnp.where(valid, dp_ref[...], 0.0)
        acc_ref[...] += _bnt(dp, w_ref[...])

        @pl.when(k == nk - 1)
        def _():
            x = h_ref[...]
            r = lax.rsqrt(jnp.mean(x * x, axis=-1, keepdims=True) + EPS)
            xh = x * r
            dxn = acc_ref[...]
            dnw_ref[...] += jnp.sum(dxn * xh, axis=0, keepdims=True)
            dxh = dxn * nw_ref[...]
            dh_ref[...] = dhn_ref[...] + r * (dxh - xh * jnp.mean(dxh * xh, axis=-1, keepdims=True))

    return pl.pallas_call(
        body, grid=(n // tm, nk),
        in_specs=[pl.BlockSpec((tm, tk), lambda i, k: (i, k)), pl.BlockSpec((d, tk), lambda i, k: (0, k)),
                  pl.BlockSpec((tm, d), lambda i, k: (i, 0)), pl.BlockSpec((1, d), lambda i, k: (0, 0)),
                  pl.BlockSpec((tm, d), lambda i, k: (i, 0))],
        out_specs=[pl.BlockSpec((tm, d), lambda i, k: (i, 0)), pl.BlockSpec((1, d), lambda i, k: (0, 0))],
        out_shape=[SDS((n, d), F32), SDS((1, d), F32)],
        scratch_shapes=[pltpu.VMEM((tm, d), F32)], compiler_params=_params(2), name="proj_bwd_dx")(dproj, wp, h, nw, dhn)


def _proj_bwd_dw(dproj, h, nw, tp):
    n, d = h.shape
    pw = dproj.shape[1]
    tm, tn = _tile(n, 768), _ctile(pw)

    def body(dp_ref, h_ref, nw_ref, dw_ref):
        i = pl.program_id(1)

        @pl.when(i == 0)
        def _():
            dw_ref[...] = jnp.zeros_like(dw_ref)

        x = h_ref[...]
        r = lax.rsqrt(jnp.mean(x * x, axis=-1, keepdims=True) + EPS)
        xn = x * r * nw_ref[...]
        dp = jnp.where(_row_valid(tm, tp, i * tm), dp_ref[...], 0.0)
        dw_ref[...] += _btn(xn, dp)

    return pl.pallas_call(
        body, grid=(pw // tn, n // tm),
        in_specs=[pl.BlockSpec((tm, tn), lambda j, i: (i, j)), pl.BlockSpec((tm, d), lambda j, i: (i, 0)),
                  pl.BlockSpec((1, d), lambda j, i: (0, 0))],
        out_specs=pl.BlockSpec((d, tn), lambda j, i: (0, j)), out_shape=SDS((d, pw), F32),
        compiler_params=_params(2), name="proj_bwd_dw")(dproj, h, nw)


def _conv_silu(x, w, row):
    c = x * w[CONV_K - 1:CONV_K, :]
    for k in range(1, CONV_K):
        c = c + jnp.where(row >= k, pltpu.roll(x, k, axis=0), 0.0) * w[CONV_K - 1 - k:CONV_K - k, :]
    return c


def _gdn_prep_fwd(proj, conv_w, lay, nb, tp):
    n = proj.shape[0]
    nblk = 3 * NH
    cb = lay.c_qkv // HD

    def body(p_ref, w_ref, o_ref):
        j = pl.program_id(1)
        x = p_ref[...]
        row = _iota2(x.shape, 0)
        c = _conv_silu(x, w_ref[...], row)
        s = _silu(c)
        r = lax.rsqrt(_rs(s * s) + EPS)
        scale = jnp.where(j < NH, Q_SCALE, 1.0)
        y = jnp.where(j < 2 * NH, s * r * scale, s)
        o_ref[...] = jnp.where(row >= PAD, y, 0.0)

    return pl.pallas_call(
        body, grid=(nb, nblk),
        in_specs=[pl.BlockSpec((tp, HD), lambda b, j: (b, cb + j)), pl.BlockSpec((CONV_K, HD), lambda b, j: (0, j))],
        out_specs=pl.BlockSpec((tp, HD), lambda b, j: (b, j)), out_shape=SDS((n, nblk * HD), F32),
        compiler_params=_params(2), name="gdn_prep_fwd")(proj, conv_w)


def _gdn_prep_bwd(proj, conv_w, dqkv, dproj, lay, nb, tp):
    nblk = 3 * NH
    cb = lay.c_qkv // HD

    def body(p_ref, w_ref, dy_ref, dp_in, dp_ref, dw_ref):
        j, b = pl.program_id(0), pl.program_id(1)
        x = p_ref[...]
        w = w_ref[...]
        row = _iota2(x.shape, 0)
        c = _conv_silu(x, w, row)
        s = _silu(c)
        dy = jnp.where(row >= PAD, dy_ref[...], 0.0)
        r = lax.rsqrt(_rs(s * s) + EPS)
        nh = s * r
        scale = jnp.where(j < NH, Q_SCALE, 1.0)
        ds_n = scale * r * (dy - nh * _rs(dy * nh))
        ds = jnp.where(j < 2 * NH, ds_n, dy)
        dc = ds * _dsilu(c)
        dx = dc * w[CONV_K - 1:CONV_K, :]
        dws = [jnp.sum(dc * x, axis=0, keepdims=True)]
        for k in range(1, CONV_K):
            dx = dx + jnp.where(row < tp - k, pltpu.roll(dc, tp - k, axis=0), 0.0) * w[CONV_K - 1 - k:CONV_K - k, :]
            xs = jnp.where(row >= k, pltpu.roll(x, k, axis=0), 0.0)
            dws.append(jnp.sum(dc * xs, axis=0, keepdims=True))
        dp_ref[...] = dx
        r4 = _iota2((CONV_K, HD), 0)
        dw = jnp.zeros((CONV_K, HD), F32)
        for k in range(CONV_K):
            dw = dw + jnp.where(r4 == CONV_K - 1 - k, dws[k], 0.0)

        @pl.when(b == 0)
        def _():
            dw_ref[...] = dw

        @pl.when(b > 0)
        def _():
            dw_ref[...] += dw

    return pl.pallas_call(
        body, grid=(nblk, nb),
        in_specs=[pl.BlockSpec((tp, HD), lambda j, b: (b, cb + j)), pl.BlockSpec((CONV_K, HD), lambda j, b: (0, j)),
                  pl.BlockSpec((tp, HD), lambda j, b: (b, j)), pl.BlockSpec(memory_space=pl.ANY)],
        out_specs=[pl.BlockSpec((tp, HD), lambda j, b: (b, cb + j)), pl.BlockSpec((CONV_K, HD), lambda j, b: (0, j))],
        out_shape=[SDS(dproj.shape, F32), SDS((CONV_K, nblk * HD), F32)],
        input_output_aliases={3: 0}, compiler_params=_params(2), name="gdn_prep_bwd")(proj, conv_w, dqkv, dproj)


def _gate_consts():
    e = np.zeros((HD, 2 * HW), np.float32)
    s = np.zeros((2 * HW, HD), np.float32)
    for h in range(NH):
        e[h, h * HD:(h + 1) * HD] = 1.0
        e[NH + h, HW + h * HD:HW + (h + 1) * HD] = 1.0
        s[h * HD, h] = 1.0
        s[HW + h * HD, NH + h] = 1.0
    return jnp.asarray(e), jnp.asarray(s)


def _gdn_tri():
    i, j = _iota2((CH, CH), 0), _iota2((CH, CH), 1)
    return i >= j, i > j


def _each(fn, *lists):
    return [fn(*xs) for xs in zip(*lists)]


def _tri_inv(a_list, eye):
    p = [-a for a in a_list]
    t = [eye + x for x in p]
    for _ in range(5):
        p = _each(_nn, p, p)
        tp_ = _each(_nn, t, p)
        t = _each(lambda x, y: x + y, t, tp_)
    return t


def _gdn_chunks(args):
    causal, strict = _gdn_tri()
    eye = jnp.where(_iota2((CH, CH), 0) == _iota2((CH, CH), 1), 1.0, 0.0)
    q, k, v, beta, g, s0 = (list(t) for t in zip(*args))
    gc = [_scan_rows(x, CH) for x in g]
    dm = [jnp.where(causal, jnp.exp(jnp.where(causal, x[:, :CH] - x[:, :CH].T, 0.0)), 0.0) for x in gc]
    ds = [jnp.where(strict, x, 0.0) for x in dm]
    kb = _each(lambda x, y: x * y, k, beta)
    kk = _each(_rnt, kb, k)
    a = _each(lambda x, y: x * y, kk, ds)
    tinv = _tri_inv(a, eye)
    eg = [jnp.exp(x) for x in gc]
    rw = _each(lambda x, y: x * y, kb, eg)
    rv = _each(lambda x, y: x * y, v, beta)
    u = _each(_nn, tinv, rv)
    w = _each(_nn, tinv, rw)
    ws = _each(_rnn, w, s0)
    vn = _each(lambda x, y: x - y, u, ws)
    qk = _each(_rnt, q, k)
    p = _each(lambda x, y: x * y, qk, dm)
    qg = _each(lambda x, y: x * y, q, eg)
    out = []
    for i in range(len(args)):
        gl = gc[i][CH - 1:CH, :]
        ek = jnp.exp(gl - gc[i])
        out.append(dict(gc=gc[i], dm=dm[i], ds=ds[i], kb=kb[i], a=a[i], tinv=tinv[i], eg=eg[i], rw=rw[i], u=u[i], w=w[i],
                        vn=vn[i], p=p[i], qg=qg[i], egl=jnp.exp(gl), ek=ek, kd=k[i] * ek))
    return out


def _gdn_gates(ba, e, alog, dtb):
    raw = _nn(ba, e)
    beta = _sig(raw[:, :HW])
    za = raw[:, HW:] + dtb
    g = -jnp.exp(alog) * _softplus(za)
    return beta, g, za


def _seqs_per_step(nb):
    return 2 if nb % 2 == 0 else 1


def _gdn_fwd(qkv, proj, e_mat, alog, dtb, lay, nb, nc):
    n = qkv.shape[0]
    tp = n // nb
    cba = lay.c_ba // HD
    gb = _seqs_per_step(nb)

    def body(x_ref, ba_ref, e_ref, al_ref, dt_ref, o_ref, so_ref, s_ref):
        @pl.when(pl.program_id(1) == 0)
        def _():
            s_ref[...] = jnp.zeros_like(s_ref)

        args = []
        for j in range(gb):
            beta, g, _ = _gdn_gates(ba_ref[j], e_ref[...], al_ref[...], dt_ref[...])
            for h in range(NH):
                hs = slice(h * HD, (h + 1) * HD)
                args.append((x_ref[j, :, hs], x_ref[j, :, HW + h * HD:HW + (h + 1) * HD],
                             x_ref[j, :, 2 * HW + h * HD:2 * HW + (h + 1) * HD], beta[:, hs], g[:, hs], s_ref[j, h]))
        cs = _gdn_chunks(args)
        s0s = [a[5] for a in args]
        o1 = _each(lambda c, s0: _rnn(c["qg"], s0), cs, s0s)
        o2 = [_rnn(c["p"], c["vn"]) for c in cs]
        upd = [_rtn(c["kd"], c["vn"]) for c in cs]
        res = [(o1[i] + o2[i], s0s[i] * cs[i]["egl"] + upd[i]) for i in range(len(cs))]
        for j in range(gb):
            for h in range(NH):
                so_ref[j, h] = args[j * NH + h][5]
                s_ref[j, h] = res[j * NH + h][1]
            o_ref[j] = jnp.concatenate([res[j * NH + h][0] for h in range(NH)], axis=-1)

    o, st = pl.pallas_call(
        body, grid=(nb // gb, nc),
        in_specs=[pl.BlockSpec((gb, CH, 3 * HW), lambda b, c: (b, c, 0)), pl.BlockSpec((gb, CH, HD), lambda b, c: (b, c, cba)),
                  pl.BlockSpec((HD, 2 * HW), lambda b, c: (0, 0)), pl.BlockSpec((1, HW), lambda b, c: (0, 0)),
                  pl.BlockSpec((1, HW), lambda b, c: (0, 0))],
        out_specs=[pl.BlockSpec((gb, CH, HW), lambda b, c: (b, c, 0)),
                   pl.BlockSpec((gb, None, NH, HD, HD), lambda b, c: (b, c, 0, 0, 0))],
        out_shape=[SDS((nb, tp, HW), F32), SDS((nb, nc, NH, HD, HD), F32)],
        scratch_shapes=[pltpu.VMEM((gb, NH, HD, HD), F32)], compiler_params=_params(2), name="gdn_fwd")(
            qkv.reshape(nb, tp, 3 * HW), proj.reshape(nb, tp, -1), e_mat, alog, dtb)
    return o.reshape(n, HW), st


def _gdn_bwd(qkv, proj, e_mat, s_mat, alog, dtb, states, do, dproj, lay, nb, nc):
    n = qkv.shape[0]
    tp = n // nb
    cba = lay.c_ba // HD
    gb = _seqs_per_step(nb)

    def body(x_ref, ba_ref, e_ref, sm_ref, al_ref, dt_ref, st_ref, do_ref, dp_in, dx_ref, dba_ref, acc_ref, ds_ref):
        ci = pl.program_id(1)

        @pl.when(ci == 0)
        def _():
            ds_ref[...] = jnp.zeros_like(ds_ref)

        @pl.when((ci == 0) & (pl.program_id(0) == 0))
        def _():
            acc_ref[...] = jnp.zeros_like(acc_ref)

        causal, strict = _gdn_tri()
        alog = al_ref[...]
        row = _iota2((CH, 1), 0)
        valid = (row >= PAD) | (ci < nc - 1)
        last = row == CH - 1
        gates = [_gdn_gates(ba_ref[j], e_ref[...], alog, dt_ref[...]) for j in range(gb)]
        args, do, ds1 = [], [], []
        for j in range(gb):
            beta, g, _ = gates[j]
            for h in range(NH):
                hs = slice(h * HD, (h + 1) * HD)
                args.append((x_ref[j, :, hs], x_ref[j, :, HW + h * HD:HW + (h + 1) * HD],
                             x_ref[j, :, 2 * HW + h * HD:2 * HW + (h + 1) * HD], beta[:, hs], g[:, hs], st_ref[j, h]))
                do.append(do_ref[j, :, hs])
                ds1.append(ds_ref[j, h])
        q, k, v, bh, _, s0 = (list(t) for t in zip(*args))
        cs = _gdn_chunks(args)
        get = lambda name: [c[name] for c in cs]
        mul = lambda x, y: x * y
        add = lambda x, y: x + y
        dvn = _each(add, _each(_rtn, get("p"), do), _each(_rnn, get("kd"), ds1))
        dqg = _each(_rnt, do, s0)
        dp = [jnp.where(causal, x, 0.0) for x in _each(_rnt, do, get("vn"))]
        dkd = _each(_rnt, get("vn"), ds1)
        dw = [-x for x in _each(_rnt, dvn, s0)]
        ds_a = _each(_rtn, get("qg"), do)
        ds_b = _each(_rtn, get("w"), dvn)
        ds_new = [ds_a[i] - ds_b[i] + ds1[i] * cs[i]["egl"] for i in range(len(cs))]
        tinv_t = [x.T for x in get("tinv")]
        drv = _each(_nn, tinv_t, dvn)
        drw = _each(_nn, tinv_t, dw)
        da_1 = _each(_nt, drv, get("u"))
        da_2 = _each(_nt, drw, get("w"))
        da = [jnp.where(strict, -(x + y), 0.0) for x, y in zip(da_1, da_2)]
        m = [da[i] * cs[i]["a"] + dp[i] * cs[i]["p"] for i in range(len(cs))]
        dkk = _each(mul, da, get("ds"))
        dqk = _each(mul, dp, get("dm"))
        dq = _each(add, _each(_rnn, dqk, k), _each(mul, dqg, get("eg")))
        dkb = _each(add, _each(_rnn, dkk, k), _each(mul, drw, get("eg")))
        dk_1 = _each(_rtn, dqk, q)
        dk_2 = _each(_rtn, dkk, get("kb"))
        dk = [dk_1[i] + dk_2[i] + dkd[i] * cs[i]["ek"] + dkb[i] * bh[i] for i in range(len(cs))]
        dv = _each(mul, drv, bh)
        dbeta, dg = [], []
        for i, c in enumerate(cs):
            dbeta.append(_rs(drv[i] * v[i]) + _rs(dkb[i] * k[i]) + jnp.zeros((CH, HD), F32))
            t_kd = _rs(dkd[i] * c["kd"])
            dgc = _rs(m[i]) - _rs(m[i].T) + _rs(dqg[i] * c["qg"]) + _rs(drw[i] * c["rw"]) - t_kd
            tail = jnp.sum(t_kd, axis=0, keepdims=True) + c["egl"] * jnp.sum(_rs(s0[i] * ds1[i]), axis=0, keepdims=True)
            dgc = dgc + jnp.where(last, tail, 0.0)
            dg.append(_scan_rows(dgc + jnp.zeros((CH, HD), F32), CH, reverse=True))
        r8 = _iota2((8, HW), 0)
        upd = jnp.zeros((8, HW), F32)
        for j in range(gb):
            sl = slice(j * NH, (j + 1) * NH)
            beta, g, za = gates[j]
            for h in range(NH):
                ds_ref[j, h] = ds_new[j * NH + h]
            dx_ref[j] = jnp.concatenate(dq[sl] + dk[sl] + dv[sl], axis=-1)
            dbeta_j = jnp.where(valid, jnp.concatenate(dbeta[sl], axis=-1), 0.0)
            dg_j = jnp.where(valid, jnp.concatenate(dg[sl], axis=-1), 0.0)
            draw_b = dbeta_j * beta * (1.0 - beta)
            draw_a = dg_j * (-jnp.exp(alog)) * _sig(za)
            dba_ref[j] = _nn(jnp.concatenate([draw_b, draw_a], axis=-1), sm_ref[...])
            upd = upd + jnp.where(r8 == 0, jnp.sum(dg_j * g, axis=0, keepdims=True), 0.0) + jnp.where(
                r8 == 1, jnp.sum(draw_a, axis=0, keepdims=True), 0.0)
        acc_ref[...] += upd

    rc = lambda c: nc - 1 - c
    dqkv, dproj3, acc = pl.pallas_call(
        body, grid=(nb // gb, nc),
        in_specs=[pl.BlockSpec((gb, CH, 3 * HW), lambda b, c: (b, rc(c), 0)), pl.BlockSpec((gb, CH, HD), lambda b, c: (b, rc(c), cba)),
                  pl.BlockSpec((HD, 2 * HW), lambda b, c: (0, 0)), pl.BlockSpec((2 * HW, HD), lambda b, c: (0, 0)),
                  pl.BlockSpec((1, HW), lambda b, c: (0, 0)), pl.BlockSpec((1, HW), lambda b, c: (0, 0)),
                  pl.BlockSpec((gb, None, NH, HD, HD), lambda b, c: (b, rc(c), 0, 0, 0)),
                  pl.BlockSpec((gb, CH, HW), lambda b, c: (b, rc(c), 0)), pl.BlockSpec(memory_space=pl.ANY)],
        out_specs=[pl.BlockSpec((gb, CH, 3 * HW), lambda b, c: (b, rc(c), 0)), pl.BlockSpec((gb, CH, HD), lambda b, c: (b, rc(c), cba)),
                   pl.BlockSpec((8, HW), lambda b, c: (0, 0))],
        out_shape=[SDS((nb, tp, 3 * HW), F32), SDS((nb, tp, dproj.shape[1]), F32), SDS((8, HW), F32)],
        input_output_aliases={8: 1},
        scratch_shapes=[pltpu.VMEM((gb, NH, HD, HD), F32)], compiler_params=_params(2), name="gdn_bwd")(
            qkv.reshape(nb, tp, 3 * HW), proj.reshape(nb, tp, -1), e_mat, s_mat, alog, dtb, states, do.reshape(nb, tp, HW),
            dproj.reshape(nb, tp, -1))
    return dqkv.reshape(n, 3 * HW), dproj3.reshape(dproj.shape), acc


def _hgrn_inputs(zq, zf, lb):
    sg = _sig(zf)
    sgn = _sig(-zf)
    pos = lb > 0.0
    lbp = jnp.where(pos, lb, 0.0)
    fpos = lbp + (1.0 - lbp) * sg
    lf = jnp.where(pos, jnp.log(jnp.where(pos, fpos, 1.0)), _logsig(zf))
    k = (1.0 - lbp) * sgn
    q = _silu(zq) * Q_SCALE
    return q, k, lf, sg, sgn, pos, lbp, fpos


def _hgrn_consts():
    i3, j3 = _iota2((SUB, SUB, HD), 0), _iota2((SUB, SUB, HD), 1)
    return i3 >= j3


def _sum_j(x):
    return jnp.sum(x.reshape(SUB, SUB, HD), axis=1)


def _sum_i(x):
    return jnp.sum(x.reshape(SUB, SUB, HD), axis=0)


def _pairs(a, b):
    return (a[:, None, :] * b[None, :, :]).reshape(SUB * SUB, HD)


def _hgrn_sub(q, k, v, bc, st, consts):
    mask3 = consts
    bl = bc[SUB - 1:SUB, :]
    p3 = jnp.where(mask3, jnp.exp(jnp.where(mask3, bc[:, None, :] - bc[None, :, :], 0.0)), 0.0).reshape(SUB * SUB, HD)
    x = _pairs(q, k) * p3
    srep = _rs(x)
    vt = jnp.broadcast_to(v[None, :, :], (SUB, SUB, HD)).reshape(SUB * SUB, HD)
    eb = jnp.exp(bc)
    qe = q * eb
    o = _rnt(qe, st) + _sum_j(_rr(srep) * _rr(vt))
    ek = jnp.exp(bl - bc)
    kd = k * ek
    ebl = jnp.exp(bl)
    st1 = st * ebl + _rtn(v, kd)
    return o, st1, dict(bc=bc, p3=p3, srep=srep, vt=vt, eb=eb, qe=qe, ek=ek, kd=kd, ebl=ebl)


def _hgrn_fwd(proj, lb, lay, nb, nc):
    n = proj.shape[0]
    cbb = lay.c_b // (3 * HW)

    def body(z_ref, lb_ref, o_ref, so_ref, s_ref):
        @pl.when(pl.program_id(1) == 0)
        def _():
            s_ref[...] = jnp.zeros_like(s_ref)

        consts = _hgrn_consts()
        outs = []
        for h in range(NH):
            hs = slice(h * HD, (h + 1) * HD)
            q, k, lf = _hgrn_inputs(z_ref[:, hs], z_ref[:, HW + h * HD:HW + (h + 1) * HD], lb_ref[:, hs])[:3]
            v = z_ref[:, 2 * HW + h * HD:2 * HW + (h + 1) * HD]
            st = s_ref[h]
            so_ref[h] = st
            bc = _scan_rows(lf, SUB)
            oh = []
            for s in range(CH // SUB):
                rs = slice(s * SUB, (s + 1) * SUB)
                o, st, _ = _hgrn_sub(q[rs], k[rs], v[rs], bc[rs], st, consts)
                oh.append(o)
            s_ref[h] = st
            outs.append(jnp.concatenate(oh, axis=0))
        o_ref[...] = jnp.concatenate(outs, axis=-1)

    return pl.pallas_call(
        body, grid=(nb, nc),
        in_specs=[pl.BlockSpec((CH, 3 * HW), lambda b, c: (b * nc + c, cbb)), pl.BlockSpec((1, HW), lambda b, c: (0, 0))],
        out_specs=[pl.BlockSpec((CH, HW), lambda b, c: (b * nc + c, 0)),
                   pl.BlockSpec((None, None, NH, HD, HD), lambda b, c: (b, c, 0, 0, 0))],
        out_shape=[SDS((n, HW), F32), SDS((nb, nc, NH, HD, HD), F32)],
        scratch_shapes=[pltpu.VMEM((NH, HD, HD), F32)], compiler_params=_params(2), name="hgrn_fwd")(proj, lb)


def _hgrn_bwd(proj, lb, states, do, dproj, lay, nb, nc):
    cbb = lay.c_b // (3 * HW)
    nsub = CH // SUB

    def rev(b, c):
        return b * nc + (nc - 1 - c)

    def body(z_ref, lb_ref, st_ref, do_ref, dp_in, dz_ref, acc_ref, ds_ref):
        ci = pl.program_id(1)

        @pl.when(ci == 0)
        def _():
            ds_ref[...] = jnp.zeros_like(ds_ref)

        @pl.when((ci == 0) & (pl.program_id(0) == 0))
        def _():
            acc_ref[...] = jnp.zeros_like(acc_ref)

        consts = _hgrn_consts()
        row = _iota2((CH, 1), 0)
        valid = (row >= PAD) | (ci < nc - 1)
        lastrow = _iota2((SUB, 1), 0) == SUB - 1
        dzq, dzf, dzi, dlbs = [], [], [], []
        for h in range(NH):
            hs = slice(h * HD, (h + 1) * HD)
            zq, zf = z_ref[:, hs], z_ref[:, HW + h * HD:HW + (h + 1) * HD]
            q, k, lf, sg, sgn, pos, lbp, fpos = _hgrn_inputs(zq, zf, lb_ref[:, hs])
            v = z_ref[:, 2 * HW + h * HD:2 * HW + (h + 1) * HD]
            doh = do_ref[:, hs]
            sts, fw = [st_ref[h]], []
            bc = _scan_rows(lf, SUB)
            for s in range(nsub):
                rs = slice(s * SUB, (s + 1) * SUB)
                _, st1, c = _hgrn_sub(q[rs], k[rs], v[rs], bc[rs], sts[-1], consts)
                sts.append(st1)
                fw.append(c)
            dst = ds_ref[h]
            dq_l, dk_l, dv_l, dlf_l = [None] * nsub, [None] * nsub, [None] * nsub, [None] * nsub
            for s in reversed(range(nsub)):
                rs = slice(s * SUB, (s + 1) * SUB)
                c, st = fw[s], sts[s]
                qs, ks, vs, dos = q[rs], k[rs], v[rs], doh[rs]
                dqe = _rnn(dos, st)
                dkd = _rnn(vs, dst)
                dsrep = _rs(_pairs(_rr(dos), _rr(vs)))
                w = dsrep * c["p3"]
                kt = jnp.broadcast_to(ks[None, :, :], (SUB, SUB, HD)).reshape(SUB * SUB, HD)
                qt = jnp.broadcast_to(qs[:, None, :], (SUB, SUB, HD)).reshape(SUB * SUB, HD)
                dq_i = _sum_j(w * kt)
                dk_i = _sum_i(w * qt)
                dot = jnp.broadcast_to(_rr(dos)[:, None, :], (SUB, SUB, HD)).reshape(SUB * SUB, HD)
                dvv = _sum_i(_rr(c["srep"]) * dot) + _rnt(c["kd"], dst)
                t_kd = dkd * c["kd"]
                dbc = dqe * c["qe"] - t_kd + qs * dq_i - ks * dk_i
                tail = jnp.sum(t_kd, axis=0, keepdims=True) + c["ebl"] * jnp.sum(st * dst, axis=0, keepdims=True)
                dbc = dbc + jnp.where(lastrow, tail, 0.0)
                dlf_l[s] = dbc
                dq_l[s] = dq_i + dqe * c["eb"]
                dk_l[s] = dk_i + dkd * c["ek"]
                dv_l[s] = dvv
                dst = _rtn(dos, c["qe"]) + dst * c["ebl"]
            ds_ref[h] = dst
            dq, dk, dv, dbc = (jnp.concatenate(t, axis=0) for t in (dq_l, dk_l, dv_l, dlf_l))
            dlf = _scan_rows(dbc, SUB, reverse=True)
            dlft = dlf - dk * (1.0 - k)
            dlf_dz = jnp.where(pos, (1.0 - lbp) * sg * sgn / jnp.where(pos, fpos, 1.0), sgn)
            dlf_dlb = jnp.where(pos, sgn / jnp.where(pos, fpos, 1.0), 0.0)
            dzq.append(dq * Q_SCALE * _dsilu(zq))
            dzf.append(dlft * dlf_dz)
            dzi.append(dv)
            dlbs.append(jnp.sum(jnp.where(valid, dlft * dlf_dlb, 0.0), axis=0, keepdims=True))
        dz_ref[...] = jnp.concatenate(dzq + dzf + dzi, axis=-1)
        acc_ref[...] += jnp.where(_iota2((8, HW), 0) == 0, jnp.concatenate(dlbs, axis=-1), 0.0)

    return pl.pallas_call(
        body, grid=(nb, nc),
        in_specs=[pl.BlockSpec((CH, 3 * HW), lambda b, c: (rev(b, c), cbb)), pl.BlockSpec((1, HW), lambda b, c: (0, 0)),
                  pl.BlockSpec((None, None, NH, HD, HD), lambda b, c: (b, nc - 1 - c, 0, 0, 0)),
                  pl.BlockSpec((CH, HW), lambda b, c: (rev(b, c), 0)), pl.BlockSpec(memory_space=pl.ANY)],
        out_specs=[pl.BlockSpec((CH, 3 * HW), lambda b, c: (rev(b, c), cbb)), pl.BlockSpec((8, HW), lambda b, c: (0, 0))],
        out_shape=[SDS(dproj.shape, F32), SDS((8, HW), F32)],
        input_output_aliases={4: 0},
        scratch_shapes=[pltpu.VMEM((NH, HD, HD), F32)], compiler_params=_params(2), name="hgrn_bwd")(proj, lb, states, do, dproj)


def _gated_norm(o, z, gamma):
    ys, ns, rs = [], [], []
    for h in range(NH):
        hs = slice(h * HD, (h + 1) * HD)
        oh = o[:, hs]
        r = lax.rsqrt(jnp.mean(oh * oh, axis=-1, keepdims=True) + EPS)
        nh = oh * r
        ys.append(nh * gamma * _silu(z[:, hs]))
        ns.append(nh)
        rs.append(r)
    return jnp.concatenate(ys, axis=-1), ns, rs


def _merge_fwd(h, oa, ob, proj, ga, gb, wa, wb, wo, lay):
    n, d = h.shape
    tm = _tile(n, 384)
    wm = lay.wm

    def body(h_ref, oa_ref, ob_ref, p_ref, ga_ref, gb_ref, wa_ref, wb_ref, wo_ref, out_ref):
        ya, _, _ = _gated_norm(oa_ref[...], p_ref[:, 0:HW], ga_ref[...])
        yb, _, _ = _gated_norm(ob_ref[...], p_ref[:, HW:2 * HW], gb_ref[...])
        ya2 = _bnn(ya, wa_ref[...])
        yb2 = _bnn(yb, wb_ref[...])
        mixed = _sig(p_ref[:, 2 * HW:2 * HW + d]) * ya2 + _sig(p_ref[:, 2 * HW + d:2 * HW + 2 * d]) * yb2
        out_ref[...] = h_ref[...] + _bnn(mixed, wo_ref[...])

    full = lambda shape: pl.BlockSpec(shape, lambda i: (0, 0))
    return pl.pallas_call(
        body, grid=(n // tm,),
        in_specs=[pl.BlockSpec((tm, d), lambda i: (i, 0)), pl.BlockSpec((tm, HW), lambda i: (i, 0)),
                  pl.BlockSpec((tm, HW), lambda i: (i, 0)), pl.BlockSpec((tm, wm), lambda i: (i, 0)),
                  full((1, HD)), full((1, HD)), full((HW, d)), full((HW, d)), full((d, d))],
        out_specs=pl.BlockSpec((tm, d), lambda i: (i, 0)), out_shape=SDS((n, d), F32),
        compiler_params=_params(1), name="merge_fwd")(h, oa, ob, proj, ga, gb, wa, wb, wo)


def _gated_norm_bwd(dy, o, z, gamma):
    dos, dzs = [], []
    dgam = jnp.zeros((1, HD), F32)
    for h in range(NH):
        hs = slice(h * HD, (h + 1) * HD)
        oh, zh, dyh = o[:, hs], z[:, hs], dy[:, hs]
        r = lax.rsqrt(jnp.mean(oh * oh, axis=-1, keepdims=True) + EPS)
        nh = oh * r
        dzs.append(dyh * nh * gamma * _dsilu(zh))
        dng = dyh * _silu(zh)
        dgam = dgam + jnp.sum(dng * nh, axis=0, keepdims=True)
        dn = dng * gamma
        dos.append(r * (dn - nh * jnp.mean(dn * nh, axis=-1, keepdims=True)))
    return jnp.concatenate(dos, axis=-1), jnp.concatenate(dzs, axis=-1), dgam


def _merge_bwd(dhn, oa, ob, proj, ga, gb, wa, wb, wo, lay, tp):
    n, d = dhn.shape
    tm = _tile(n, 128)
    wm = lay.wm

    def body(dh_ref, oa_ref, ob_ref, p_ref, ga_ref, gb_ref, wa_ref, wb_ref, wo_ref,
             dp_ref, doa_ref, dob_ref, dwa_ref, dwb_ref, dwo_ref, dga_ref, dgb_ref):
        i = pl.program_id(0)

        @pl.when(i == 0)
        def _():
            for r in (dwa_ref, dwb_ref, dwo_ref, dga_ref, dgb_ref):
                r[...] = jnp.zeros_like(r)

        dh = jnp.where(_row_valid(tm, tp, i * tm), dh_ref[...], 0.0)
        oa, ob = oa_ref[...], ob_ref[...]
        za, zb = p_ref[:, 0:HW], p_ref[:, HW:2 * HW]
        gta, gtb = p_ref[:, 2 * HW:2 * HW + d], p_ref[:, 2 * HW + d:2 * HW + 2 * d]
        ya, _, _ = _gated_norm(oa, za, ga_ref[...])
        yb, _, _ = _gated_norm(ob, zb, gb_ref[...])
        ya2 = _bnn(ya, wa_ref[...])
        yb2 = _bnn(yb, wb_ref[...])
        sa, sb = _sig(gta), _sig(gtb)
        mixed = sa * ya2 + sb * yb2
        dmixed = _bnt(dh, wo_ref[...])
        dwo_ref[...] += _btn(mixed, dh)
        dya2 = dmixed * sa
        dyb2 = dmixed * sb
        dwa_ref[...] += _btn(ya, dya2)
        dwb_ref[...] += _btn(yb, dyb2)
        doa, dza, dga = _gated_norm_bwd(_bnt(dya2, wa_ref[...]), oa, za, ga_ref[...])
        dob, dzb, dgb = _gated_norm_bwd(_bnt(dyb2, wb_ref[...]), ob, zb, gb_ref[...])
        dga_ref[...] += dga
        dgb_ref[...] += dgb
        doa_ref[...] = doa
        dob_ref[...] = dob
        dp_ref[:, 0:HW] = dza
        dp_ref[:, HW:2 * HW] = dzb
        dp_ref[:, 2 * HW:2 * HW + d] = dmixed * ya2 * sa * (1.0 - sa)
        dp_ref[:, 2 * HW + d:2 * HW + 2 * d] = dmixed * yb2 * sb * (1.0 - sb)

    full = lambda shape: pl.BlockSpec(shape, lambda i: (0, 0))
    rows = lambda w: pl.BlockSpec((tm, w), lambda i: (i, 0))
    return pl.pallas_call(
        body, grid=(n // tm,),
        in_specs=[rows(d), rows(HW), rows(HW), rows(wm), full((1, HD)), full((1, HD)), full((HW, d)), full((HW, d)), full((d, d))],
        out_specs=[rows(wm), rows(HW), rows(HW), full((HW, d)), full((HW, d)), full((d, d)), full((1, HD)), full((1, HD))],
        out_shape=[SDS((n, lay.pw), F32), SDS((n, HW), F32), SDS((n, HW), F32), SDS((HW, d), F32), SDS((HW, d), F32),
                   SDS((d, d), F32), SDS((1, HD), F32), SDS((1, HD), F32)],
        compiler_params=_params(1), name="merge_bwd")(dhn, oa, ob, proj, ga, gb, wa, wb, wo)


def _loss_head(h, target, fw, nb, nc):
    n, d = h.shape

    def body(h_ref, t_ref, fw_ref, lp_ref, dh_ref, dfw_ref):
        b, c = pl.program_id(0), pl.program_id(1)

        @pl.when((b == 0) & (c == 0))
        def _():
            dfw_ref[...] = jnp.zeros_like(dfw_ref)

        @pl.when(c == 0)
        def _():
            dh_ref[...] = jnp.zeros_like(dh_ref)
            lp_ref[...] = jnp.zeros_like(lp_ref)

        @pl.when(c > 0)
        def _():
            x = h_ref[...]
            r = lax.rsqrt(jnp.mean(x * x, axis=-1, keepdims=True) + EPS)
            xh = x * r
            err = xh * fw_ref[...] - t_ref[...]
            lp_ref[...] = jnp.zeros_like(lp_ref) + 0.5 * jnp.sum(_rs(err * err), axis=0, keepdims=True) / d
            dy = err / d
            dfw_ref[...] += jnp.sum(dy * xh, axis=0, keepdims=True)
            dxh = dy * fw_ref[...]
            dh_ref[...] = r * (dxh - xh * jnp.mean(dxh * xh, axis=-1, keepdims=True))

    return pl.pallas_call(
        body, grid=(nb, nc),
        in_specs=[pl.BlockSpec((CH, d), lambda b, c: (b * nc + c, 0)),
                  pl.BlockSpec((CH, d), lambda b, c: (b * (nc - 1) + jnp.maximum(c - 1, 0), 0)),
                  pl.BlockSpec((1, d), lambda b, c: (0, 0))],
        out_specs=[pl.BlockSpec((8, HD), lambda b, c: (b * nc + c, 0)), pl.BlockSpec((CH, d), lambda b, c: (b * nc + c, 0)),
                   pl.BlockSpec((1, d), lambda b, c: (0, 0))],
        out_shape=[SDS((nb * nc * 8, HD), F32), SDS((n, d), F32), SDS((1, d), F32)],
        compiler_params=_params(2), name="loss_head")(h, target, fw)


def _lb_fwd(lb):
    def body(x_ref, o_ref):
        x = x_ref[...]
        mx = jnp.max(x, axis=0, keepdims=True)
        e = jnp.exp(x - mx)
        sm = e / jnp.sum(e, axis=0, keepdims=True)
        run = jnp.zeros((1, HW), F32)
        for l in range(DEPTH):
            run = run + sm[l:l + 1, :]
            o_ref[l:l + 1, :] = run - sm[0:1, :]

    return pl.pallas_call(body, out_shape=SDS(lb.shape, F32), name="lb_fwd")(lb)


def _lb_bwd(lb, dlb_all):
    def body(x_ref, d_ref, o_ref):
        x = x_ref[...]
        dl = d_ref[...]
        mx = jnp.max(x, axis=0, keepdims=True)
        e = jnp.exp(x - mx)
        sm = e / jnp.sum(e, axis=0, keepdims=True)
        tot = jnp.sum(dl, axis=0, keepdims=True)
        dsm = []
        run = tot
        for l in range(DEPTH):
            dsm.append(run - (tot if l == 0 else 0.0))
            run = run - dl[l:l + 1, :]
        inner = sum(sm[l:l + 1, :] * dsm[l] for l in range(DEPTH))
        for l in range(DEPTH):
            o_ref[l:l + 1, :] = sm[l:l + 1, :] * (dsm[l] - inner)

    return pl.pallas_call(body, out_shape=SDS(lb.shape, F32), name="lb_bwd")(lb, dlb_all)


def _adamw(g, w, m, v):
    r, c = g.shape
    tr = _tile(r, 264)
    c1 = 1.0 / (1.0 - ADAM_B1 ** ADAM_STEP)
    c2 = 1.0 / (1.0 - ADAM_B2 ** ADAM_STEP)

    def body(g_ref, w_ref, m_ref, v_ref, d_ref, mo_ref, vo_ref):
        gg = g_ref[...]
        mn = ADAM_B1 * m_ref[...] + (1.0 - ADAM_B1) * gg
        vn = ADAM_B2 * v_ref[...] + (1.0 - ADAM_B2) * gg * gg
        d_ref[...] = -ADAM_LR * ((mn * c1) / (jnp.sqrt(vn * c2) + ADAM_EPS) + ADAM_WD * w_ref[...])
        mo_ref[...] = mn
        vo_ref[...] = vn

    spec = pl.BlockSpec((tr, c), lambda i: (i, 0))
    return pl.pallas_call(body, grid=(r // tr,), in_specs=[spec] * 4, out_specs=[spec] * 3, out_shape=[SDS(g.shape, F32)] * 3,
                          compiler_params=_params(1), name="adamw")(g, w, m, v)


def _add_rows(a, b, name):
    k, r, c = a.shape
    tr = _tile(r, 264)
    spec = pl.BlockSpec((None, tr, c), lambda s, i: (s, i, 0))

    def body(a_ref, b_ref, o_ref):
        o_ref[...] = a_ref[...] + b_ref[...]

    return pl.pallas_call(body, grid=(k, r // tr), in_specs=[spec, spec], out_specs=spec, out_shape=SDS(a.shape, F32),
                          compiler_params=_params(2), name=name)(a, b)


def _sum_chips(parts):
    k, r, c = parts.shape
    tr = _tile(r, 264)

    def body(p_ref, o_ref):
        acc = p_ref[0]
        for s in range(1, k):
            acc = acc + p_ref[s]
        o_ref[...] = acc

    return pl.pallas_call(body, grid=(r // tr,), in_specs=[pl.BlockSpec((k, tr, c), lambda i: (0, i, 0))],
                          out_specs=pl.BlockSpec((tr, c), lambda i: (i, 0)), out_shape=SDS((r, c), F32),
                          compiler_params=_params(1), name="sum_chips")(parts)


def _meta_grad(dh, nb, nc):
    d = dh.shape[1]

    def body(x_ref, o_ref):
        @pl.when(pl.program_id(0) == 0)
        def _():
            o_ref[...] = jnp.zeros_like(o_ref)

        o_ref[...] += x_ref[PAD:CH, :]

    return pl.pallas_call(body, grid=(nb,), in_specs=[pl.BlockSpec((CH, d), lambda b: (b * nc, 0))],
                          out_specs=pl.BlockSpec((N_META, d), lambda b: (0, 0)), out_shape=SDS((N_META, d), F32),
                          compiler_params=_params(1), name="meta_grad")(dh)


ANY = pl.BlockSpec(memory_space=pl.ANY)


def _place():
    x, y, c = lax.axis_index("x"), lax.axis_index("y"), lax.axis_index("c")
    chips = [(1 - x, y), (x, 1 - y), (1 - x, 1 - y)]
    return x, y, c, chips


def _remote(src, dst, send_sems, recv_sems, k, to):
    return pltpu.make_async_remote_copy(src_ref=src, dst_ref=dst, send_sem=send_sems.at[k], recv_sem=recv_sems.at[k],
                                        device_id=to, device_id_type=MESH)


def _gather_weights(pb, ps):
    def body(pb_ref, ps_ref, gb_ref, gs_ref, send_sems, recv_sems, local_sems):
        x, y, c, chips = _place()
        s = 2 * x + y
        sib = (x, y, 1 - c)
        l0 = pltpu.make_async_copy(pb_ref, gb_ref.at[s], local_sems.at[0])
        l1 = pltpu.make_async_copy(ps_ref, gs_ref.at[s], local_sems.at[1])
        l0.start()
        l1.start()
        sends = []
        for k, (px, py) in enumerate(chips):
            sends.append(_remote(pb_ref.at[c], gb_ref.at[s, c], send_sems, recv_sems, k, (px, py, c)))
            sends.append(_remote(ps_ref, gs_ref.at[s], send_sems, recv_sems, 6 + k, (px, py, c)))
        for cp in sends:
            cp.start()
        for k, (px, py) in enumerate(chips):
            sk = 2 * px + py
            _remote(pb_ref.at[c], gb_ref.at[sk, c], send_sems, recv_sems, k, sib).wait_recv()
            fwd = _remote(gb_ref.at[sk, c], gb_ref.at[sk, c], send_sems, recv_sems, 3 + k, sib)
            fwd.start()
            sends.append(fwd)
        for k, (px, py) in enumerate(chips):
            sk = 2 * px + py
            _remote(pb_ref.at[c], gb_ref.at[sk, 1 - c], send_sems, recv_sems, 3 + k, sib).wait_recv()
            _remote(ps_ref, gs_ref.at[sk], send_sems, recv_sems, 6 + k, sib).wait_recv()
        for cp in sends:
            cp.wait_send()
        l0.wait()
        l1.wait()

    return pl.pallas_call(
        body, in_specs=[ANY, ANY], out_specs=[ANY, ANY],
        out_shape=[SDS((4,) + pb.shape, pb.dtype), SDS((4,) + ps.shape, ps.dtype)],
        scratch_shapes=[pltpu.SemaphoreType.DMA((9,)), pltpu.SemaphoreType.DMA((9,)), pltpu.SemaphoreType.DMA((2,))],
        name="gather_weights")(pb, ps)


def _sem_scratch(n_remote, n_local):
    return [pltpu.SemaphoreType.DMA((n_remote,)), pltpu.SemaphoreType.DMA((n_remote,)), pltpu.SemaphoreType.DMA((n_local,))]


def _swap_halves(gs):
    nt = len(gs)

    def body(*refs):
        g_refs, own_refs, got_refs = refs[:nt], refs[nt:2 * nt], refs[2 * nt:3 * nt]
        send_sems, recv_sems, local_sems = refs[3 * nt:]
        x, y, c, _ = _place()
        sib = (x, y, 1 - c)
        local, remote = [], []
        for t in range(nt):
            for s in range(4):
                local.append(pltpu.make_async_copy(g_refs[t].at[c, s], own_refs[t].at[s], local_sems.at[4 * t + s]))
                remote.append(_remote(g_refs[t].at[1 - c, s], got_refs[t].at[s], send_sems, recv_sems, 4 * t + s, sib))
        for cp in remote + local:
            cp.start()
        for cp in remote + local:
            cp.wait()

    shapes = [SDS(g.shape[1:], g.dtype) for g in gs]
    out = pl.pallas_call(
        body, in_specs=[ANY] * nt, out_specs=[ANY] * (2 * nt), out_shape=shapes + shapes,
        scratch_shapes=_sem_scratch(4 * nt, 4 * nt), name="swap_halves")(*gs)
    return out[:nt], out[nt:]


def _scatter_chip_sums(parts):
    nt = len(parts)

    def body(*refs):
        a_refs, r_refs = refs[:nt], refs[nt:2 * nt]
        send_sems, recv_sems, local_sems = refs[2 * nt:]
        x, y, c, chips = _place()
        s = 2 * x + y
        local = [pltpu.make_async_copy(a_refs[t].at[s], r_refs[t].at[s], local_sems.at[t]) for t in range(nt)]
        sends = [_remote(a_refs[t].at[2 * px + py], r_refs[t].at[s], send_sems, recv_sems, 3 * t + k, (px, py, c))
                 for t in range(nt) for k, (px, py) in enumerate(chips)]
        for cp in sends + local:
            cp.start()
        for t in range(nt):
            for k, (px, py) in enumerate(chips):
                _remote(a_refs[t].at[s], r_refs[t].at[2 * px + py], send_sems, recv_sems, 3 * t + k, (px, py, c)).wait_recv()
        for cp in sends:
            cp.wait_send()
        for cp in local:
            cp.wait()

    return pl.pallas_call(
        body, in_specs=[ANY] * nt, out_specs=[ANY] * nt, out_shape=[SDS(a.shape, a.dtype) for a in parts],
        scratch_shapes=_sem_scratch(3 * nt, nt), name="scatter_chip_sums")(*parts)


def _join_halves(ts):
    nt = len(ts)

    def body(*refs):
        t_refs, f_refs = refs[:nt], refs[nt:2 * nt]
        send_sems, recv_sems, local_sems = refs[2 * nt:]
        x, y, c, _ = _place()
        sib = (x, y, 1 - c)
        local = [pltpu.make_async_copy(t_refs[t], f_refs[t].at[c], local_sems.at[t]) for t in range(nt)]
        sends = [_remote(t_refs[t], f_refs[t].at[c], send_sems, recv_sems, t, sib) for t in range(nt)]
        for cp in sends + local:
            cp.start()
        for t in range(nt):
            _remote(t_refs[t], f_refs[t].at[1 - c], send_sems, recv_sems, t, sib).wait_recv()
        for cp in sends:
            cp.wait_send()
        for cp in local:
            cp.wait()

    return pl.pallas_call(
        body, in_specs=[ANY] * nt, out_specs=[ANY] * nt, out_shape=[SDS((2,) + t.shape, t.dtype) for t in ts],
        scratch_shapes=_sem_scratch(nt, nt), name="join_halves")(*ts)


def _uncontain(cont, n_head, width):
    r, cw = cont.shape
    tr = _tile(r, 256)

    def body(n_ref, x_ref, o_ref):
        o_ref[...] = pltpu.roll(x_ref[...], n_ref[0], axis=1)[:, :width]

    return pl.pallas_call(
        body, grid_spec=pltpu.PrefetchScalarGridSpec(
            num_scalar_prefetch=1, grid=(r // tr,), in_specs=[pl.BlockSpec((tr, cw), lambda i, n: (i, 0))],
            out_specs=pl.BlockSpec((tr, width), lambda i, n: (i, 0))),
        out_shape=SDS((r, width), F32), compiler_params=_params(1), name="uncontain")(n_head, cont)


WEIGHTS = ("meta_tokens", "norm_w", "w_in", "conv_w", "a_log", "dt_bias", "gnorm_a", "gnorm_b", "hgrn_lower_bounds",
           "w_branch_a", "w_branch_b", "w_out", "final_norm_w")
SHARD_AXIS = {"meta_tokens": 1, "w_in": 2, "conv_w": 2, "w_branch_a": 2, "w_branch_b": 2, "w_out": 1}
FLAT_C = 1024


def _flat(parts, rows, cols=FLAT_C):
    v = jnp.concatenate([p.reshape(-1) for p in parts])
    return jnp.pad(v, (0, rows * cols - v.shape[0])).reshape(rows, cols)


def _local_step(x, target, w, lay):
    nb, seq, d = x.shape
    tp = CH + seq
    nc = tp // CH
    n = nb * tp
    e_mat, s_mat = _gate_consts()
    lb_all = _lb_fwd(w["hgrn_lower_bounds"])
    h = jnp.concatenate([jnp.zeros((nb, PAD, d), F32), jnp.broadcast_to(w["meta_tokens"][None], (nb, N_META, d)), x],
                        axis=1).reshape(n, d)
    rep = lambda a: jnp.repeat(a, HD)[None, :]
    saved = []
    for l in range(DEPTH):
        nw = w["norm_w"][l][None, :]
        proj = _norm_proj_fwd(h, nw, w["w_in"][l])
        qkv = _gdn_prep_fwd(proj, w["conv_w"][l], lay, nb, tp)
        alog, dtb = rep(w["a_log"][l]), rep(w["dt_bias"][l])
        oa, sa = _gdn_fwd(qkv, proj, e_mat, alog, dtb, lay, nb, nc)
        lbl = lb_all[l][None, :]
        ob, sb = _hgrn_fwd(proj, lbl, lay, nb, nc)
        ga, gb = w["gnorm_a"][l][None, :], w["gnorm_b"][l][None, :]
        hn = _merge_fwd(h, oa, ob, proj, ga, gb, w["w_branch_a"][l], w["w_branch_b"][l], w["w_out"][l], lay)
        saved.append((h, nw, proj, qkv, alog, dtb, oa, sa, lbl, ob, sb, ga, gb))
        h = hn
    lp, dh, dfw = _loss_head(h, target.reshape(nb * seq, d), w["final_norm_w"][None, :], nb, nc)
    loss = jnp.sum(lp[::8, 0])
    g = {n_: [None] * DEPTH for n_ in WEIGHTS}
    dlb_all = [None] * DEPTH
    for l in reversed(range(DEPTH)):
        h, nw, proj, qkv, alog, dtb, oa, sa, lbl, ob, sb, ga, gb = saved[l]
        dproj, doa, dob, dwa, dwb, dwo, dga, dgb = _merge_bwd(dh, oa, ob, proj, ga, gb, w["w_branch_a"][l],
                                                             w["w_branch_b"][l], w["w_out"][l], lay, tp)
        dproj, acc_b = _hgrn_bwd(proj, lbl, sb, dob, dproj, lay, nb, nc)
        dqkv, dproj, acc_a = _gdn_bwd(qkv, proj, e_mat, s_mat, alog, dtb, sa, doa, dproj, lay, nb, nc)
        dproj, dconv = _gdn_prep_bwd(proj, w["conv_w"][l], dqkv, dproj, lay, nb, tp)
        dh, dnw = _proj_bwd_dx(dproj, w["w_in"][l], h, nw, dh, tp)
        g["w_in"][l] = _proj_bwd_dw(dproj, h, nw, tp)
        g["norm_w"][l] = dnw[0]
        g["conv_w"][l] = dconv
        g["a_log"][l] = acc_a[0, ::HD]
        g["dt_bias"][l] = acc_a[1, ::HD]
        g["gnorm_a"][l], g["gnorm_b"][l] = dga[0], dgb[0]
        g["w_branch_a"][l], g["w_branch_b"][l], g["w_out"][l] = dwa, dwb, dwo
        dlb_all[l] = acc_b[0]
    grads = {n_: jnp.stack(v) for n_, v in g.items() if v[0] is not None}
    grads["hgrn_lower_bounds"] = _lb_bwd(w["hgrn_lower_bounds"], jnp.stack(dlb_all))
    grads["final_norm_w"] = dfw[0]
    grads["meta_tokens"] = _meta_grad(dh, nb, nc)
    grad_x = dh.reshape(nb, tp, d)[:, CH:, :]
    return loss, grad_x, grads


def kernel(x, meta_tokens, norm_w, w_in, conv_w, a_log, dt_bias, gnorm_a, gnorm_b, hgrn_lower_bounds, w_branch_a, w_branch_b, w_out, final_norm_w, loss_target, m_meta_tokens, m_norm_w, m_w_in, m_conv_w, m_a_log, m_dt_bias, m_gnorm_a, m_gnorm_b, m_hgrn_lower_bounds, m_w_branch_a, m_w_branch_b, m_w_out, m_final_norm_w, v_meta_tokens, v_norm_w, v_w_in, v_conv_w, v_a_log, v_dt_bias, v_gnorm_a, v_gnorm_b, v_hgrn_lower_bounds, v_w_branch_a, v_w_branch_b, v_w_out, v_final_norm_w):
    wl = dict(meta_tokens=meta_tokens, norm_w=norm_w, w_in=w_in, conv_w=conv_w, a_log=a_log, dt_bias=dt_bias, gnorm_a=gnorm_a,
              gnorm_b=gnorm_b, hgrn_lower_bounds=hgrn_lower_bounds, w_branch_a=w_branch_a, w_branch_b=w_branch_b, w_out=w_out,
              final_norm_w=final_norm_w)
    ml = dict(zip(WEIGHTS, (m_meta_tokens, m_norm_w, m_w_in, m_conv_w, m_a_log, m_dt_bias, m_gnorm_a, m_gnorm_b,
                            m_hgrn_lower_bounds, m_w_branch_a, m_w_branch_b, m_w_out, m_final_norm_w)))
    vl = dict(zip(WEIGHTS, (v_meta_tokens, v_norm_w, v_w_in, v_conv_w, v_a_log, v_dt_bias, v_gnorm_a, v_gnorm_b,
                            v_hgrn_lower_bounds, v_w_branch_a, v_w_branch_b, v_w_out, v_final_norm_w)))
    d = x.shape[2]
    lay = _Layout(d)
    nchip = 4

    big = ("w_in", "w_branch_a", "w_branch_b", "w_out")
    small = ("conv_w", "meta_tokens")
    nbig = sum(int(np.prod(wl[n].shape)) for n in big)
    rb = -(-nbig // (2 * FLAT_C * 16)) * 16
    pb = _flat([wl[n].astype(BF16) for n in big], 2 * rb).reshape(2, rb, FLAT_C)
    nsmall = sum(int(np.prod(wl[n].shape)) for n in small)
    rs = -(-nsmall // (HD * 8)) * 8
    ps = jnp.pad(jnp.concatenate([wl[n].reshape(-1) for n in small]), (0, rs * HD - nsmall)).reshape(rs, HD)
    gbig, gsmall = _gather_weights(pb, ps)
    gbig = gbig.reshape(nchip, -1)
    gsmall = gsmall.reshape(nchip, -1)

    def whole(flat, names, src):
        out, o = {}, 0
        for n in names:
            shp = src[n].shape
            sz = int(np.prod(shp))
            a = flat[:, o:o + sz].reshape((nchip,) + shp)
            ax = SHARD_AXIS[n]
            out[n] = jnp.concatenate([a[s] for s in range(nchip)], axis=ax)
            o += sz
        return out

    wf = dict(wl)
    wf.update(whole(gbig, big, wl))
    wf.update(whole(gsmall, small, wl))
    wf["w_in"] = lay.to_kernel(wf["w_in"])

    loss_part, grad_x, gfull = _local_step(x, loss_target, wf, lay)
    loss = lax.psum(loss_part, ("x", "y", "c"))

    sw = wl["w_in"].shape[2]
    conts, heads = lay.containers(gfull["w_in"], nchip)
    dd = wl["w_branch_a"].shape[2]
    rows_o = wl["w_out"].shape[1]
    by_dest = lambda g, n: [lax.slice_in_dim(g, s * wl[n].shape[SHARD_AXIS[n]], (s + 1) * wl[n].shape[SHARD_AXIS[n]],
                                            axis=SHARD_AXIS[n]) if n in SHARD_AXIS else g for s in range(nchip)]
    small_names = tuple(n for n in WEIGHTS if n not in big)
    nsm = sum(int(np.prod(wl[n].shape)) for n in small_names)
    rsm = -(-nsm // (2 * HD * 8)) * 8
    pack_small = lambda parts: _flat(parts, 2 * rsm, HD).reshape(2, rsm, HD)
    small_by_dest = [by_dest(gfull[n], n) for n in small_names]
    gs = [jnp.stack(conts, axis=1),
          jnp.stack(by_dest(gfull["w_branch_a"], "w_branch_a"), axis=1),
          jnp.stack(by_dest(gfull["w_branch_b"], "w_branch_b"), axis=1),
          gfull["w_out"].reshape(DEPTH, nchip, rows_o, d),
          jnp.stack([pack_small([p[s] for p in small_by_dest]) for s in range(nchip)], axis=1)]
    own, got = _swap_halves(gs)
    chip_sums = [_add_rows(a.reshape((nchip, -1, a.shape[-1])), b.reshape((nchip, -1, b.shape[-1])), "add_cores")
                 for a, b in zip(own, got)]
    by_chip = _scatter_chip_sums(chip_sums)
    mine = [_sum_chips(p) for p in by_chip]
    full = _join_halves(mine)
    my_chip = 2 * lax.axis_index("x") + lax.axis_index("y")
    n_head = sum(jnp.where(my_chip == s, heads[s], 0) for s in range(nchip)).astype(jnp.int32).reshape(1)
    g_w_in = _uncontain(full[0].reshape(DEPTH * d, -1), n_head, sw)
    g2 = {"w_in": g_w_in, "w_branch_a": full[1].reshape(-1, dd), "w_branch_b": full[2].reshape(-1, dd),
          "w_out": full[3].reshape(-1, d), "small": full[4].reshape(2 * rsm, HD)}

    def two_d(src, n):
        if n == "small":
            return _flat([src[k] for k in small_names], 2 * rsm, HD)
        return src[n].reshape(g2[n].shape)

    outs = {}
    for n in big + ("small",):
        delta, mnew, vnew = _adamw(g2[n], two_d(wl, n), two_d(ml, n), two_d(vl, n))
        outs[n] = (g2[n], delta, mnew, vnew)
    res = [{}, {}, {}, {}]
    for i in range(4):
        for n in big:
            res[i][n] = outs[n][i].reshape(wl[n].shape)
        v, o = outs["small"][i].reshape(-1), 0
        for n in small_names:
            sz = int(np.prod(wl[n].shape))
            res[i][n] = v[o:o + sz].reshape(wl[n].shape)
            o += sz
    return (loss, grad_x, *[res[0][n] for n in WEIGHTS], *[res[1][n] for n in WEIGHTS], *[res[2][n] for n in WEIGHTS],
            *[res[3][n] for n in WEIGHTS])
```

```python
import functools

import numpy as np
import jax
import jax.numpy as jnp
from jax import lax
from jax.experimental import pallas as pl
from jax.experimental.pallas import tpu as pltpu

F32 = jnp.float32
BF16 = jnp.bfloat16
HI = lax.Precision.HIGHEST
SDS = jax.ShapeDtypeStruct

NH = 4
HD = 128
HW = NH * HD
N_META = 16
CH = 64
SUB = 16
PAD = CH - N_META
EPS = 1e-6
Q_SCALE = HD ** -0.5
DEPTH = 2
CONV_K = 4
VMEM_LIMIT = 56 * 1024 * 1024
ADAM_LR, ADAM_B1, ADAM_B2, ADAM_EPS, ADAM_WD, ADAM_STEP = 0.001, 0.9, 0.999, 1e-08, 0.01, 10
MESH = pl.DeviceIdType.MESH


def _nn(a, b):
    return jnp.dot(a, b, precision=HI, preferred_element_type=F32)


def _nt(a, b):
    return lax.dot_general(a, b, (((1,), (1,)), ((), ())), precision=HI, preferred_element_type=F32)


def _tn(a, b):
    return _nn(a.T, b)


def _scan_rows(x, group, reverse=False):
    n = x.shape[0]
    pos = lax.bitwise_and(_iota2(x.shape, 0), group - 1)
    s = 1
    while s < group:
        if reverse:
            x = x + jnp.where(pos < group - s, pltpu.roll(x, n - s, axis=0), 0.0)
        else:
            x = x + jnp.where(pos >= s, pltpu.roll(x, s, axis=0), 0.0)
        s *= 2
    return x


def _bnn(a, b):
    return jnp.dot(a.astype(BF16), b.astype(BF16), preferred_element_type=F32)


def _bnt(a, b):
    return lax.dot_general(a.astype(BF16), b.astype(BF16), (((1,), (1,)), ((), ())), preferred_element_type=F32)


def _btn(a, b):
    return lax.dot_general(a.astype(BF16), b.astype(BF16), (((0,), (0,)), ((), ())), preferred_element_type=F32)


_rnn, _rnt, _rtn = _nn, _nt, _tn
_hnn, _hnt, _htn = _bnn, _bnt, _btn


def _rr(x):
    return x


def _sig(x):
    return jax.nn.sigmoid(x)


def _silu(x):
    return x * _sig(x)


def _dsilu(x):
    s = _sig(x)
    return s * (1.0 + x * (1.0 - s))


def _softplus(x):
    return jnp.maximum(x, 0.0) + jnp.log(1.0 + jnp.exp(-jnp.abs(x)))


def _logsig(x):
    return jnp.minimum(x, 0.0) - jnp.log(1.0 + jnp.exp(-jnp.abs(x)))


def _rs(x):
    return jnp.sum(x, axis=-1, keepdims=True)


def _params(n_axes):
    return pltpu.CompilerParams(dimension_semantics=("arbitrary",) * n_axes, vmem_limit_bytes=VMEM_LIMIT)


def _tile(n, target):
    best = 8
    for t in range(8, target + 1, 8):
        if n % t == 0:
            best = t
    return best


def _ctile(pw, most=7):
    return HD * max(k for k in range(1, most + 1) if (pw // HD) % k == 0)


def _iota2(shape, axis):
    return lax.broadcasted_iota(jnp.int32, shape, axis)


class _Layout:
    def __init__(self, d):
        self.d = d
        self.wm = 2 * HW + 2 * d
        self.c_qkv = self.wm
        self.c_b = self.wm + 3 * HW
        self.c_ba = self.wm + 6 * HW
        self.pw = self.c_ba + HD
        assert self.c_b % (3 * HW) == 0
        o = 0
        segs = {}
        for name, w in (("a_q", HW), ("a_k", HW), ("a_v", HW), ("ba", 2 * NH), ("a_z", HW), ("b_q", HW), ("b_f", HW),
                        ("b_i", HW), ("b_g", HW), ("gate_a", d), ("gate_b", d)):
            segs[name] = (o, o + w)
            o += w
        self.segs = segs
        self.width = o
        self.order = ("a_z", "b_g", "gate_a", "gate_b", "a_q", "a_k", "a_v", "b_q", "b_f", "b_i", "ba")

    def to_kernel(self, w):
        parts = [w[..., self.segs[n][0]:self.segs[n][1]] for n in self.order]
        parts.append(jnp.zeros(w.shape[:-1] + (HD - 2 * NH,), w.dtype))
        return jnp.concatenate(parts, axis=-1)

    def containers(self, g, nchip):
        off, where = 0, {}
        for n in self.order:
            where[n] = off
            off += self.segs[n][1] - self.segs[n][0]
        names = sorted(self.segs, key=lambda n: self.segs[n][0])
        sw = self.width // nchip
        cw = -(-sw // HD) * HD
        out, heads = [], []
        for s in range(nchip):
            lo, hi = s * sw, (s + 1) * sw
            pieces = []
            for n in names:
                a, b = max(lo, self.segs[n][0]), min(hi, self.segs[n][1])
                if a < b:
                    pieces.append((where[n] + a - self.segs[n][0], b - a))
            start, width = pieces[0]
            n_head = min((-start) % HD, width)
            body = ([(start + n_head, width - n_head)] if width > n_head else []) + pieces[1:]
            parts = [g[..., c:c + w] for c, w in body]
            parts.append(jnp.zeros(g.shape[:-1] + (cw - sw,), g.dtype))
            if n_head:
                parts.append(g[..., start:start + n_head])
            out.append(jnp.concatenate(parts, axis=-1))
            heads.append(n_head)
        return out, heads

    def from_kernel(self, g):
        off, where = 0, {}
        for n in self.order:
            w = self.segs[n][1] - self.segs[n][0]
            where[n] = (off, off + w)
            off += w
        names = sorted(self.segs, key=lambda n: self.segs[n][0])
        return jnp.concatenate([g[..., where[n][0]:where[n][1]] for n in names], axis=-1)


def _norm_proj_fwd(h, nw, wp):
    n, d = h.shape
    pw = wp.shape[1]
    tm, tn = _tile(n, 768), _ctile(pw)

    def body(h_ref, nw_ref, w_ref, o_ref, xn_ref):
        @pl.when(pl.program_id(1) == 0)
        def _():
            x = h_ref[...]
            r = lax.rsqrt(jnp.mean(x * x, axis=-1, keepdims=True) + EPS)
            xn_ref[...] = (x * r * nw_ref[...]).astype(BF16)

        o_ref[...] = jnp.dot(xn_ref[...], w_ref[...], preferred_element_type=F32)

    return pl.pallas_call(
        body, grid=(n // tm, pw // tn),
        in_specs=[pl.BlockSpec((tm, d), lambda i, j: (i, 0)), pl.BlockSpec((1, d), lambda i, j: (0, 0)),
                  pl.BlockSpec((d, tn), lambda i, j: (0, j))],
        out_specs=pl.BlockSpec((tm, tn), lambda i, j: (i, j)), out_shape=SDS((n, pw), F32),
        scratch_shapes=[pltpu.VMEM((tm, d), BF16)], compiler_params=_params(2), name="norm_proj_fwd")(h, nw, wp)


def _row_valid(tm, tp, base):
    row = base + _iota2((tm, 1), 0)
    return lax.rem(row, tp) >= PAD


def _proj_bwd_dx(dproj, wp, h, nw, dhn, tp):
    n, d = h.shape
    pw = wp.shape[1]
    tm, tk = _tile(n, 768), _ctile(pw)
    nk = pw // tk

    def body(dp_ref, w_ref, h_ref, nw_ref, dhn_ref, dh_ref, dnw_ref, acc_ref):
        i, k = pl.program_id(0), pl.program_id(1)

        @pl.when(k == 0)
        def _():
            acc_ref[...] = jnp.zeros_like(acc_ref)

        @pl.when((i == 0) & (k == 0))
        def _():
            dnw_ref[...] = jnp.zeros_like(dnw_ref)

        valid = _row_valid(tm, tp, i * tm)
        dp = jnp.where(valid, dp_ref[...], 0.0)
        acc_ref[...] += _bnt(dp, w_ref[...])

        @pl.when(k == nk - 1)
        def _():
            x = h_ref[...]
            r = lax.rsqrt(jnp.mean(x * x, axis=-1, keepdims=True) + EPS)
            xh = x * r
            dxn = acc_ref[...]
            dnw_ref[...] += jnp.sum(dxn * xh, axis=0, keepdims=True)
            dxh = dxn * nw_ref[...]
            dh_ref[...] = dhn_ref[...] + r * (dxh - xh * jnp.mean(dxh * xh, axis=-1, keepdims=True))

    return pl.pallas_call(
        body, grid=(n // tm, nk),
        in_specs=[pl.BlockSpec((tm, tk), lambda i, k: (i, k)), pl.BlockSpec((d, tk), lambda i, k: (0, k)),
                  pl.BlockSpec((tm, d), lambda i, k: (i, 0)), pl.BlockSpec((1, d), lambda i, k: (0, 0)),
                  pl.BlockSpec((tm, d), lambda i, k: (i, 0))],
        out_specs=[pl.BlockSpec((tm, d), lambda i, k: (i, 0)), pl.BlockSpec((1, d), lambda i, k: (0, 0))],
        out_shape=[SDS((n, d), F32), SDS((1, d), F32)],
        scratch_shapes=[pltpu.VMEM((tm, d), F32)], compiler_params=_params(2), name="proj_bwd_dx")(dproj, wp, h, nw, dhn)


def _proj_bwd_dw(dproj, h, nw, tp):
    n, d = h.shape
    pw = dproj.shape[1]
    tm, tn = _tile(n, 768), _ctile(pw)

    def body(dp_ref, h_ref, nw_ref, dw_ref):
        i = pl.program_id(1)

        @pl.when(i == 0)
        def _():
            dw_ref[...] = jnp.zeros_like(dw_ref)

        x = h_ref[...]
        r = lax.rsqrt(jnp.mean(x * x, axis=-1, keepdims=True) + EPS)
        xn = x * r * nw_ref[...]
        dp = jnp.where(_row_valid(tm, tp, i * tm), dp_ref[...], 0.0)
        dw_ref[...] += _btn(xn, dp)

    return pl.pallas_call(
        body, grid=(pw // tn, n // tm),
        in_specs=[pl.BlockSpec((tm, tn), lambda j, i: (i, j)), pl.BlockSpec((tm, d), lambda j, i: (i, 0)),
                  pl.BlockSpec((1, d), lambda j, i: (0, 0))],
        out_specs=pl.BlockSpec((d, tn), lambda j, i: (0, j)), out_shape=SDS((d, pw), F32),
        compiler_params=_params(2), name="proj_bwd_dw")(dproj, h, nw)


def _conv_silu(x, w, row):
    c = x * w[CONV_K - 1:CONV_K, :]
    for k in range(1, CONV_K):
        c = c + jnp.where(row >= k, pltpu.roll(x, k, axis=0), 0.0) * w[CONV_K - 1 - k:CONV_K - k, :]
    return c


def _gdn_prep_fwd(proj, conv_w, lay, nb, tp):
    n = proj.shape[0]
    nblk = 3 * NH
    cb = lay.c_qkv // HD

    def body(p_ref, w_ref, o_ref):
        j = pl.program_id(1)
        x = p_ref[...]
        row = _iota2(x.shape, 0)
        c = _conv_silu(x, w_ref[...], row)
        s = _silu(c)
        r = lax.rsqrt(_rs(s * s) + EPS)
        scale = jnp.where(j < NH, Q_SCALE, 1.0)
        y = jnp.where(j < 2 * NH, s * r * scale, s)
        o_ref[...] = jnp.where(row >= PAD, y, 0.0)

    return pl.pallas_call(
        body, grid=(nb, nblk),
        in_specs=[pl.BlockSpec((tp, HD), lambda b, j: (b, cb + j)), pl.BlockSpec((CONV_K, HD), lambda b, j: (0, j))],
        out_specs=pl.BlockSpec((tp, HD), lambda b, j: (b, j)), out_shape=SDS((n, nblk * HD), F32),
        compiler_params=_params(2), name="gdn_prep_fwd")(proj, conv_w)


def _gdn_prep_bwd(proj, conv_w, dqkv, dproj, lay, nb, tp):
    nblk = 3 * NH
    cb = lay.c_qkv // HD

    def body(p_ref, w_ref, dy_ref, dp_in, dp_ref, dw_ref):
        j, b = pl.program_id(0), pl.program_id(1)
        x = p_ref[...]
        w = w_ref[...]
        row = _iota2(x.shape, 0)
        c = _conv_silu(x, w, row)
        s = _silu(c)
        dy = jnp.where(row >= PAD, dy_ref[...], 0.0)
        r = lax.rsqrt(_rs(s * s) + EPS)
        nh = s * r
        scale = jnp.where(j < NH, Q_SCALE, 1.0)
        ds_n = scale * r * (dy - nh * _rs(dy * nh))
        ds = jnp.where(j < 2 * NH, ds_n, dy)
        dc = ds * _dsilu(c)
        dx = dc * w[CONV_K - 1:CONV_K, :]
        dws = [jnp.sum(dc * x, axis=0, keepdims=True)]
        for k in range(1, CONV_K):
            dx = dx + jnp.where(row < tp - k, pltpu.roll(dc, tp - k, axis=0), 0.0) * w[CONV_K - 1 - k:CONV_K - k, :]
            xs = jnp.where(row >= k, pltpu.roll(x, k, axis=0), 0.0)
            dws.append(jnp.sum(dc * xs, axis=0, keepdims=True))
        dp_ref[...] = dx
        r4 = _iota2((CONV_K, HD), 0)
        dw = jnp.zeros((CONV_K, HD), F32)
        for k in range(CONV_K):
            dw = dw + jnp.where(r4 == CONV_K - 1 - k, dws[k], 0.0)

        @pl.when(b == 0)
        def _():
            dw_ref[...] = dw

        @pl.when(b > 0)
        def _():
            dw_ref[...] += dw

    return pl.pallas_call(
        body, grid=(nblk, nb),
        in_specs=[pl.BlockSpec((tp, HD), lambda j, b: (b, cb + j)), pl.BlockSpec((CONV_K, HD), lambda j, b: (0, j)),
                  pl.BlockSpec((tp, HD), lambda j, b: (b, j)), pl.BlockSpec(memory_space=pl.ANY)],
        out_specs=[pl.BlockSpec((tp, HD), lambda j, b: (b, cb + j)), pl.BlockSpec((CONV_K, HD), lambda j, b: (0, j))],
        out_shape=[SDS(dproj.shape, F32), SDS((CONV_K, nblk * HD), F32)],
        input_output_aliases={3: 0}, compiler_params=_params(2), name="gdn_prep_bwd")(proj, conv_w, dqkv, dproj)


def _gate_consts():
    e = np.zeros((HD, 2 * HW), np.float32)
    s = np.zeros((2 * HW, HD), np.float32)
    for h in range(NH):
        e[h, h * HD:(h + 1) * HD] = 1.0
        e[NH + h, HW + h * HD:HW + (h + 1) * HD] = 1.0
        s[h * HD, h] = 1.0
        s[HW + h * HD, NH + h] = 1.0
    return jnp.asarray(e), jnp.asarray(s)


def _gdn_tri():
    i, j = _iota2((CH, CH), 0), _iota2((CH, CH), 1)
    return i >= j, i > j


def _each(fn, *lists):
    return [fn(*xs) for xs in zip(*lists)]


def _tri_inv(a_list, eye):
    p = [-a for a in a_list]
    t = [eye + x for x in p]
    for _ in range(5):
        p = _each(_nn, p, p)
        tp_ = _each(_nn, t, p)
        t = _each(lambda x, y: x + y, t, tp_)
    return t


def _gdn_chunks(args):
    causal, strict = _gdn_tri()
    eye = jnp.where(_iota2((CH, CH), 0) == _iota2((CH, CH), 1), 1.0, 0.0)
    q, k, v, beta, g, s0 = (list(t) for t in zip(*args))
    gc = [_scan_rows(x, CH) for x in g]
    dm = [jnp.where(causal, jnp.exp(jnp.where(causal, x[:, :CH] - x[:, :CH].T, 0.0)), 0.0) for x in gc]
    ds = [jnp.where(strict, x, 0.0) for x in dm]
    kb = _each(lambda x, y: x * y, k, beta)
    kk = _each(_rnt, kb, k)
    a = _each(lambda x, y: x * y, kk, ds)
    tinv = _tri_inv(a, eye)
    eg = [jnp.exp(x) for x in gc]
    rw = _each(lambda x, y: x * y, kb, eg)
    rv = _each(lambda x, y: x * y, v, beta)
    u = _each(_nn, tinv, rv)
    w = _each(_nn, tinv, rw)
    ws = _each(_rnn, w, s0)
    vn = _each(lambda x, y: x - y, u, ws)
    qk = _each(_rnt, q, k)
    p = _each(lambda x, y: x * y, qk, dm)
    qg = _each(lambda x, y: x * y, q, eg)
    out = []
    for i in range(len(args)):
        gl = gc[i][CH - 1:CH, :]
        ek = jnp.exp(gl - gc[i])
        out.append(dict(gc=gc[i], dm=dm[i], ds=ds[i], kb=kb[i], a=a[i], tinv=tinv[i], eg=eg[i], rw=rw[i], u=u[i], w=w[i],
                        vn=vn[i], p=p[i], qg=qg[i], egl=jnp.exp(gl), ek=ek, kd=k[i] * ek))
    return out


def _gdn_gates(ba, e, alog, dtb):
    raw = _nn(ba, e)
    beta = _sig(raw[:, :HW])
    za = raw[:, HW:] + dtb
    g = -jnp.exp(alog) * _softplus(za)
    return beta, g, za


def _seqs_per_step(nb):
    return 2 if nb % 2 == 0 else 1


def _gdn_fwd(qkv, proj, e_mat, alog, dtb, lay, nb, nc):
    n = qkv.shape[0]
    tp = n // nb
    cba = lay.c_ba // HD
    gb = _seqs_per_step(nb)

    def body(x_ref, ba_ref, e_ref, al_ref, dt_ref, o_ref, so_ref, s_ref):
        @pl.when(pl.program_id(1) == 0)
        def _():
            s_ref[...] = jnp.zeros_like(s_ref)

        args = []
        for j in range(gb):
            beta, g, _ = _gdn_gates(ba_ref[j], e_ref[...], al_ref[...], dt_ref[...])
            for h in range(NH):
                hs = slice(h * HD, (h + 1) * HD)
                args.append((x_ref[j, :, hs], x_ref[j, :, HW + h * HD:HW + (h + 1) * HD],
                             x_ref[j, :, 2 * HW + h * HD:2 * HW + (h + 1) * HD], beta[:, hs], g[:, hs], s_ref[j, h]))
        cs = _gdn_chunks(args)
        s0s = [a[5] for a in args]
        o1 = _each(lambda c, s0: _rnn(c["qg"], s0), cs, s0s)
        o2 = [_rnn(c["p"], c["vn"]) for c in cs]
        upd = [_rtn(c["kd"], c["vn"]) for c in cs]
        res = [(o1[i] + o2[i], s0s[i] * cs[i]["egl"] + upd[i]) for i in range(len(cs))]
        for j in range(gb):
            for h in range(NH):
                so_ref[j, h] = args[j * NH + h][5]
                s_ref[j, h] = res[j * NH + h][1]
            o_ref[j] = jnp.concatenate([res[j * NH + h][0] for h in range(NH)], axis=-1)

    o, st = pl.pallas_call(
        body, grid=(nb // gb, nc),
        in_specs=[pl.BlockSpec((gb, CH, 3 * HW), lambda b, c: (b, c, 0)), pl.BlockSpec((gb, CH, HD), lambda b, c: (b, c, cba)),
                  pl.BlockSpec((HD, 2 * HW), lambda b, c: (0, 0)), pl.BlockSpec((1, HW), lambda b, c: (0, 0)),
                  pl.BlockSpec((1, HW), lambda b, c: (0, 0))],
        out_specs=[pl.BlockSpec((gb, CH, HW), lambda b, c: (b, c, 0)),
                   pl.BlockSpec((gb, None, NH, HD, HD), lambda b, c: (b, c, 0, 0, 0))],
        out_shape=[SDS((nb, tp, HW), F32), SDS((nb, nc, NH, HD, HD), F32)],
        scratch_shapes=[pltpu.VMEM((gb, NH, HD, HD), F32)], compiler_params=_params(2), name="gdn_fwd")(
            qkv.reshape(nb, tp, 3 * HW), proj.reshape(nb, tp, -1), e_mat, alog, dtb)
    return o.reshape(n, HW), st


def _gdn_bwd(qkv, proj, e_mat, s_mat, alog, dtb, states, do, dproj, lay, nb, nc):
    n = qkv.shape[0]
    tp = n // nb
    cba = lay.c_ba // HD
    gb = _seqs_per_step(nb)

    def body(x_ref, ba_ref, e_ref, sm_ref, al_ref, dt_ref, st_ref, do_ref, dp_in, dx_ref, dba_ref, acc_ref, ds_ref):
        ci = pl.program_id(1)

        @pl.when(ci == 0)
        def _():
            ds_ref[...] = jnp.zeros_like(ds_ref)

        @pl.when((ci == 0) & (pl.program_id(0) == 0))
        def _():
            acc_ref[...] = jnp.zeros_like(acc_ref)

        causal, strict = _gdn_tri()
        alog = al_ref[...]
        row = _iota2((CH, 1), 0)
        valid = (row >= PAD) | (ci < nc - 1)
        last = row == CH - 1
        gates = [_gdn_gates(ba_ref[j], e_ref[...], alog, dt_ref[...]) for j in range(gb)]
        args, do, ds1 = [], [], []
        for j in range(gb):
            beta, g, _ = gates[j]
            for h in range(NH):
                hs = slice(h * HD, (h + 1) * HD)
                args.append((x_ref[j, :, hs], x_ref[j, :, HW + h * HD:HW + (h + 1) * HD],
                             x_ref[j, :, 2 * HW + h * HD:2 * HW + (h + 1) * HD], beta[:, hs], g[:, hs], st_ref[j, h]))
                do.append(do_ref[j, :, hs])
                ds1.append(ds_ref[j, h])
        q, k, v, bh, _, s0 = (list(t) for t in zip(*args))
        cs = _gdn_chunks(args)
        get = lambda name: [c[name] for c in cs]
        mul = lambda x, y: x * y
        add = lambda x, y: x + y
        dvn = _each(add, _each(_rtn, get("p"), do), _each(_rnn, get("kd"), ds1))
        dqg = _each(_rnt, do, s0)
        dp = [jnp.where(causal, x, 0.0) for x in _each(_rnt, do, get("vn"))]
        dkd = _each(_rnt, get("vn"), ds1)
        dw = [-x for x in _each(_rnt, dvn, s0)]
        ds_a = _each(_rtn, get("qg"), do)
        ds_b = _each(_rtn, get("w"), dvn)
        ds_new = [ds_a[i] - ds_b[i] + ds1[i] * cs[i]["egl"] for i in range(len(cs))]
        tinv_t = [x.T for x in get("tinv")]
        drv = _each(_nn, tinv_t, dvn)
        drw = _each(_nn, tinv_t, dw)
        da_1 = _each(_nt, drv, get("u"))
        da_2 = _each(_nt, drw, get("w"))
        da = [jnp.where(strict, -(x + y), 0.0) for x, y in zip(da_1, da_2)]
        m = [da[i] * cs[i]["a"] + dp[i] * cs[i]["p"] for i in range(len(cs))]
        dkk = _each(mul, da, get("ds"))
        dqk = _each(mul, dp, get("dm"))
        dq = _each(add, _each(_rnn, dqk, k), _each(mul, dqg, get("eg")))
        dkb = _each(add, _each(_rnn, dkk, k), _each(mul, drw, get("eg")))
        dk_1 = _each(_rtn, dqk, q)
        dk_2 = _each(_rtn, dkk, get("kb"))
        dk = [dk_1[i] + dk_2[i] + dkd[i] * cs[i]["ek"] + dkb[i] * bh[i] for i in range(len(cs))]
        dv = _each(mul, drv, bh)
        dbeta, dg = [], []
        for i, c in enumerate(cs):
            dbeta.append(_rs(drv[i] * v[i]) + _rs(dkb[i] * k[i]) + jnp.zeros((CH, HD), F32))
            t_kd = _rs(dkd[i] * c["kd"])
            dgc = _rs(m[i]) - _rs(m[i].T) + _rs(dqg[i] * c["qg"]) + _rs(drw[i] * c["rw"]) - t_kd
            tail = jnp.sum(t_kd, axis=0, keepdims=True) + c["egl"] * jnp.sum(_rs(s0[i] * ds1[i]), axis=0, keepdims=True)
            dgc = dgc + jnp.where(last, tail, 0.0)
            dg.append(_scan_rows(dgc + jnp.zeros((CH, HD), F32), CH, reverse=True))
        r8 = _iota2((8, HW), 0)
        upd = jnp.zeros((8, HW), F32)
        for j in range(gb):
            sl = slice(j * NH, (j + 1) * NH)
            beta, g, za = gates[j]
            for h in range(NH):
                ds_ref[j, h] = ds_new[j * NH + h]
            dx_ref[j] = jnp.concatenate(dq[sl] + dk[sl] + dv[sl], axis=-1)
            dbeta_j = jnp.where(valid, jnp.concatenate(dbeta[sl], axis=-1), 0.0)
            dg_j = jnp.where(valid, jnp.concatenate(dg[sl], axis=-1), 0.0)
            draw_b = dbeta_j * beta * (1.0 - beta)
            draw_a = dg_j * (-jnp.exp(alog)) * _sig(za)
            dba_ref[j] = _nn(jnp.concatenate([draw_b, draw_a], axis=-1), sm_ref[...])
            upd = upd + jnp.where(r8 == 0, jnp.sum(dg_j * g, axis=0, keepdims=True), 0.0) + jnp.where(
                r8 == 1, jnp.sum(draw_a, axis=0, keepdims=True), 0.0)
        acc_ref[...] += upd

    rc = lambda c: nc - 1 - c
    dqkv, dproj3, acc = pl.pallas_call(
        body, grid=(nb // gb, nc),
        in_specs=[pl.BlockSpec((gb, CH, 3 * HW), lambda b, c: (b, rc(c), 0)), pl.BlockSpec((gb, CH, HD), lambda b, c: (b, rc(c), cba)),
                  pl.BlockSpec((HD, 2 * HW), lambda b, c: (0, 0)), pl.BlockSpec((2 * HW, HD), lambda b, c: (0, 0)),
                  pl.BlockSpec((1, HW), lambda b, c: (0, 0)), pl.BlockSpec((1, HW), lambda b, c: (0, 0)),
                  pl.BlockSpec((gb, None, NH, HD, HD), lambda b, c: (b, rc(c), 0, 0, 0)),
                  pl.BlockSpec((gb, CH, HW), lambda b, c: (b, rc(c), 0)), pl.BlockSpec(memory_space=pl.ANY)],
        out_specs=[pl.BlockSpec((gb, CH, 3 * HW), lambda b, c: (b, rc(c), 0)), pl.BlockSpec((gb, CH, HD), lambda b, c: (b, rc(c), cba)),
                   pl.BlockSpec((8, HW), lambda b, c: (0, 0))],
        out_shape=[SDS((nb, tp, 3 * HW), F32), SDS((nb, tp, dproj.shape[1]), F32), SDS((8, HW), F32)],
        input_output_aliases={8: 1},
        scratch_shapes=[pltpu.VMEM((gb, NH, HD, HD), F32)], compiler_params=_params(2), name="gdn_bwd")(
            qkv.reshape(nb, tp, 3 * HW), proj.reshape(nb, tp, -1), e_mat, s_mat, alog, dtb, states, do.reshape(nb, tp, HW),
            dproj.reshape(nb, tp, -1))
    return dqkv.reshape(n, 3 * HW), dproj3.reshape(dproj.shape), acc


def _hgrn_inputs(zq, zf, lb):
    sg = _sig(zf)
    sgn = _sig(-zf)
    pos = lb > 0.0
    lbp = jnp.where(pos, lb, 0.0)
    fpos = lbp + (1.0 - lbp) * sg
    lf = jnp.where(pos, jnp.log(jnp.where(pos, fpos, 1.0)), _logsig(zf))
    k = (1.0 - lbp) * sgn
    q = _silu(zq) * Q_SCALE
    return q, k, lf, sg, sgn, pos, lbp, fpos


def _hgrn_consts():
    i3, j3 = _iota2((SUB, SUB, HD), 0), _iota2((SUB, SUB, HD), 1)
    return i3 >= j3


def _sum_j(x):
    return jnp.sum(x.reshape(SUB, SUB, HD), axis=1)


def _sum_i(x):
    return jnp.sum(x.reshape(SUB, SUB, HD), axis=0)


def _pairs(a, b):
    return (a[:, None, :] * b[None, :, :]).reshape(SUB * SUB, HD)


def _hgrn_sub(q, k, v, bc, st, consts):
    mask3 = consts
    bl = bc[SUB - 1:SUB, :]
    p3 = jnp.where(mask3, jnp.exp(jnp.where(mask3, bc[:, None, :] - bc[None, :, :], 0.0)), 0.0).reshape(SUB * SUB, HD)
    x = _pairs(q, k) * p3
    srep = _rs(x)
    vt = jnp.broadcast_to(v[None, :, :], (SUB, SUB, HD)).reshape(SUB * SUB, HD)
    eb = jnp.exp(bc)
    qe = q * eb
    o = _hnt(qe, st) + _sum_j(_rr(srep) * _rr(vt))
    ek = jnp.exp(bl - bc)
    kd = k * ek
    ebl = jnp.exp(bl)
    st1 = st * ebl + _htn(v, kd)
    return o, st1, dict(bc=bc, p3=p3, srep=srep, vt=vt, eb=eb, qe=qe, ek=ek, kd=kd, ebl=ebl)


def _hgrn_fwd(proj, lb, lay, nb, nc):
    n = proj.shape[0]
    cbb = lay.c_b // (3 * HW)

    def body(z_ref, lb_ref, o_ref, so_ref, s_ref):
        @pl.when(pl.program_id(1) == 0)
        def _():
            s_ref[...] = jnp.zeros_like(s_ref)

        consts = _hgrn_consts()
        outs = []
        for h in range(NH):
            hs = slice(h * HD, (h + 1) * HD)
            q, k, lf = _hgrn_inputs(z_ref[:, hs], z_ref[:, HW + h * HD:HW + (h + 1) * HD], lb_ref[:, hs])[:3]
            v = z_ref[:, 2 * HW + h * HD:2 * HW + (h + 1) * HD]
            st = s_ref[h]
            so_ref[h] = st
            bc = _scan_rows(lf, SUB)
            oh = []
            for s in range(CH // SUB):
                rs = slice(s * SUB, (s + 1) * SUB)
                o, st, _ = _hgrn_sub(q[rs], k[rs], v[rs], bc[rs], st, consts)
                oh.append(o)
            s_ref[h] = st
            outs.append(jnp.concatenate(oh, axis=0))
        o_ref[...] = jnp.concatenate(outs, axis=-1)

    return pl.pallas_call(
        body, grid=(nb, nc),
        in_specs=[pl.BlockSpec((CH, 3 * HW), lambda b, c: (b * nc + c, cbb)), pl.BlockSpec((1, HW), lambda b, c: (0, 0))],
        out_specs=[pl.BlockSpec((CH, HW), lambda b, c: (b * nc + c, 0)),
                   pl.BlockSpec((None, None, NH, HD, HD), lambda b, c: (b, c, 0, 0, 0))],
        out_shape=[SDS((n, HW), F32), SDS((nb, nc, NH, HD, HD), F32)],
        scratch_shapes=[pltpu.VMEM((NH, HD, HD), F32)], compiler_params=_params(2), name="hgrn_fwd")(proj, lb)


def _hgrn_bwd(proj, lb, states, do, dproj, lay, nb, nc):
    cbb = lay.c_b // (3 * HW)
    nsub = CH // SUB

    def rev(b, c):
        return b * nc + (nc - 1 - c)

    def body(z_ref, lb_ref, st_ref, do_ref, dp_in, dz_ref, acc_ref, ds_ref):
        ci = pl.program_id(1)

        @pl.when(ci == 0)
        def _():
            ds_ref[...] = jnp.zeros_like(ds_ref)

        @pl.when((ci == 0) & (pl.program_id(0) == 0))
        def _():
            acc_ref[...] = jnp.zeros_like(acc_ref)

        consts = _hgrn_consts()
        row = _iota2((CH, 1), 0)
        valid = (row >= PAD) | (ci < nc - 1)
        lastrow = _iota2((SUB, 1), 0) == SUB - 1
        dzq, dzf, dzi, dlbs = [], [], [], []
        for h in range(NH):
            hs = slice(h * HD, (h + 1) * HD)
            zq, zf = z_ref[:, hs], z_ref[:, HW + h * HD:HW + (h + 1) * HD]
            q, k, lf, sg, sgn, pos, lbp, fpos = _hgrn_inputs(zq, zf, lb_ref[:, hs])
            v = z_ref[:, 2 * HW + h * HD:2 * HW + (h + 1) * HD]
            doh = do_ref[:, hs]
            sts, fw = [st_ref[h]], []
            bc = _scan_rows(lf, SUB)
            for s in range(nsub):
                rs = slice(s * SUB, (s + 1) * SUB)
                _, st1, c = _hgrn_sub(q[rs], k[rs], v[rs], bc[rs], sts[-1], consts)
                sts.append(st1)
                fw.append(c)
            dst = ds_ref[h]
            dq_l, dk_l, dv_l, dlf_l = [None] * nsub, [None] * nsub, [None] * nsub, [None] * nsub
            for s in reversed(range(nsub)):
                rs = slice(s * SUB, (s + 1) * SUB)
                c, st = fw[s], sts[s]
                qs, ks, vs, dos = q[rs], k[rs], v[rs], doh[rs]
                dqe = _hnn(dos, st)
                dkd = _hnn(vs, dst)
                dsrep = _rs(_pairs(_rr(dos), _rr(vs)))
                w = dsrep * c["p3"]
                kt = jnp.broadcast_to(ks[None, :, :], (SUB, SUB, HD)).reshape(SUB * SUB, HD)
                qt = jnp.broadcast_to(qs[:, None, :], (SUB, SUB, HD)).reshape(SUB * SUB, HD)
                dq_i = _sum_j(w * kt)
                dk_i = _sum_i(w * qt)
                dot = jnp.broadcast_to(_rr(dos)[:, None, :], (SUB, SUB, HD)).reshape(SUB * SUB, HD)
                dvv = _sum_i(_rr(c["srep"]) * dot) + _hnt(c["kd"], dst)
                t_kd = dkd * c["kd"]
                dbc = dqe * c["qe"] - t_kd + qs * dq_i - ks * dk_i
                tail = jnp.sum(t_kd, axis=0, keepdims=True) + c["ebl"] * jnp.sum(st * dst, axis=0, keepdims=True)
                dbc = dbc + jnp.where(lastrow, tail, 0.0)
                dlf_l[s] = dbc
                dq_l[s] = dq_i + dqe * c["eb"]
                dk_l[s] = dk_i + dkd * c["ek"]
                dv_l[s] = dvv
                dst = _htn(dos, c["qe"]) + dst * c["ebl"]
            ds_ref[h] = dst
            dq, dk, dv, dbc = (jnp.concatenate(t, axis=0) for t in (dq_l, dk_l, dv_l, dlf_l))
            dlf = _scan_rows(dbc, SUB, reverse=True)
            dlft = dlf - dk * (1.0 - k)
            dlf_dz = jnp.where(pos, (1.0 - lbp) * sg * sgn / jnp.where(pos, fpos, 1.0), sgn)
            dlf_dlb = jnp.where(pos, sgn / jnp.where(pos, fpos, 1.0), 0.0)
            dzq.append(dq * Q_SCALE * _dsilu(zq))
            dzf.append(dlft * dlf_dz)
            dzi.append(dv)
            dlbs.append(jnp.sum(jnp.where(valid, dlft * dlf_dlb, 0.0), axis=0, keepdims=True))
        dz_ref[...] = jnp.concatenate(dzq + dzf + dzi, axis=-1)
        acc_ref[...] += jnp.where(_iota2((8, HW), 0) == 0, jnp.concatenate(dlbs, axis=-1), 0.0)

    return pl.pallas_call(
        body, grid=(nb, nc),
        in_specs=[pl.BlockSpec((CH, 3 * HW), lambda b, c: (rev(b, c), cbb)), pl.BlockSpec((1, HW), lambda b, c: (0, 0)),
                  pl.BlockSpec((None, None, NH, HD, HD), lambda b, c: (b, nc - 1 - c, 0, 0, 0)),
                  pl.BlockSpec((CH, HW), lambda b, c: (rev(b, c), 0)), pl.BlockSpec(memory_space=pl.ANY)],
        out_specs=[pl.BlockSpec((CH, 3 * HW), lambda b, c: (rev(b, c), cbb)), pl.BlockSpec((8, HW), lambda b, c: (0, 0))],
        out_shape=[SDS(dproj.shape, F32), SDS((8, HW), F32)],
        input_output_aliases={4: 0},
        scratch_shapes=[pltpu.VMEM((NH, HD, HD), F32)], compiler_params=_params(2), name="hgrn_bwd")(proj, lb, states, do, dproj)


def _gated_norm(o, z, gamma):
    ys, ns, rs = [], [], []
    for h in range(NH):
        hs = slice(h * HD, (h + 1) * HD)
        oh = o[:, hs]
        r = lax.rsqrt(jnp.mean(oh * oh, axis=-1, keepdims=True) + EPS)
        nh = oh * r
        ys.append(nh * gamma * _silu(z[:, hs]))
        ns.append(nh)
        rs.append(r)
    return jnp.concatenate(ys, axis=-1), ns, rs


def _merge_fwd(h, oa, ob, proj, ga, gb, wa, wb, wo, lay):
    n, d = h.shape
    tm = _tile(n, 384)
    wm = lay.wm

    def body(h_ref, oa_ref, ob_ref, p_ref, ga_ref, gb_ref, wa_ref, wb_ref, wo_ref, out_ref):
        ya, _, _ = _gated_norm(oa_ref[...], p_ref[:, 0:HW], ga_ref[...])
        yb, _, _ = _gated_norm(ob_ref[...], p_ref[:, HW:2 * HW], gb_ref[...])
        ya2 = _bnn(ya, wa_ref[...])
        yb2 = _bnn(yb, wb_ref[...])
        mixed = _sig(p_ref[:, 2 * HW:2 * HW + d]) * ya2 + _sig(p_ref[:, 2 * HW + d:2 * HW + 2 * d]) * yb2
        out_ref[...] = h_ref[...] + _bnn(mixed, wo_ref[...])

    full = lambda shape: pl.BlockSpec(shape, lambda i: (0, 0))
    return pl.pallas_call(
        body, grid=(n // tm,),
        in_specs=[pl.BlockSpec((tm, d), lambda i: (i, 0)), pl.BlockSpec((tm, HW), lambda i: (i, 0)),
                  pl.BlockSpec((tm, HW), lambda i: (i, 0)), pl.BlockSpec((tm, wm), lambda i: (i, 0)),
                  full((1, HD)), full((1, HD)), full((HW, d)), full((HW, d)), full((d, d))],
        out_specs=pl.BlockSpec((tm, d), lambda i: (i, 0)), out_shape=SDS((n, d), F32),
        compiler_params=_params(1), name="merge_fwd")(h, oa, ob, proj, ga, gb, wa, wb, wo)


def _gated_norm_bwd(dy, o, z, gamma):
    dos, dzs = [], []
    dgam = jnp.zeros((1, HD), F32)
    for h in range(NH):
        hs = slice(h * HD, (h + 1) * HD)
        oh, zh, dyh = o[:, hs], z[:, hs], dy[:, hs]
        r = lax.rsqrt(jnp.mean(oh * oh, axis=-1, keepdims=True) + EPS)
        nh = oh * r
        dzs.append(dyh * nh * gamma * _dsilu(zh))
        dng = dyh * _silu(zh)
        dgam = dgam + jnp.sum(dng * nh, axis=0, keepdims=True)
        dn = dng * gamma
        dos.append(r * (dn - nh * jnp.mean(dn * nh, axis=-1, keepdims=True)))
    return jnp.concatenate(dos, axis=-1), jnp.concatenate(dzs, axis=-1), dgam


def _merge_bwd(dhn, oa, ob, proj, ga, gb, wa, wb, wo, lay, tp):
    n, d = dhn.shape
    tm = _tile(n, 256)
    wm = lay.wm

    def body(dh_ref, oa_ref, ob_ref, p_ref, ga_ref, gb_ref, wa_ref, wb_ref, wo_ref,
             dp_ref, doa_ref, dob_ref, dwa_ref, dwb_ref, dwo_ref, dga_ref, dgb_ref):
        i = pl.program_id(0)

        @pl.when(i == 0)
        def _():
            for r in (dwa_ref, dwb_ref, dwo_ref, dga_ref, dgb_ref):
                r[...] = jnp.zeros_like(r)

        dh = jnp.where(_row_valid(tm, tp, i * tm), dh_ref[...], 0.0)
        oa, ob = oa_ref[...], ob_ref[...]
        za, zb = p_ref[:, 0:HW], p_ref[:, HW:2 * HW]
        gta, gtb = p_ref[:, 2 * HW:2 * HW + d], p_ref[:, 2 * HW + d:2 * HW + 2 * d]
        ya, _, _ = _gated_norm(oa, za, ga_ref[...])
        yb, _, _ = _gated_norm(ob, zb, gb_ref[...])
        ya2 = _bnn(ya, wa_ref[...])
        yb2 = _bnn(yb, wb_ref[...])
        sa, sb = _sig(gta), _sig(gtb)
        mixed = sa * ya2 + sb * yb2
        dmixed = _bnt(dh, wo_ref[...])
        dwo_ref[...] += _btn(mixed, dh)
        dya2 = dmixed * sa
        dyb2 = dmixed * sb
        dwa_ref[...] += _btn(ya, dya2)
        dwb_ref[...] += _btn(yb, dyb2)
        doa, dza, dga = _gated_norm_bwd(_bnt(dya2, wa_ref[...]), oa, za, ga_ref[...])
        dob, dzb, dgb = _gated_norm_bwd(_bnt(dyb2, wb_ref[...]), ob, zb, gb_ref[...])
        dga_ref[...] += dga
        dgb_ref[...] += dgb
        doa_ref[...] = doa
        dob_ref[...] = dob
        dp_ref[:, 0:HW] = dza
        dp_ref[:, HW:2 * HW] = dzb
        dp_ref[:, 2 * HW:2 * HW + d] = dmixed * ya2 * sa * (1.0 - sa)
        dp_ref[:, 2 * HW + d:2 * HW + 2 * d] = dmixed * yb2 * sb * (1.0 - sb)

    full = lambda shape: pl.BlockSpec(shape, lambda i: (0, 0))
    rows = lambda w: pl.BlockSpec((tm, w), lambda i: (i, 0))
    return pl.pallas_call(
        body, grid=(n // tm,),
        in_specs=[rows(d), rows(HW), rows(HW), rows(wm), full((1, HD)), full((1, HD)), full((HW, d)), full((HW, d)), full((d, d))],
        out_specs=[rows(wm), rows(HW), rows(HW), full((HW, d)), full((HW, d)), full((d, d)), full((1, HD)), full((1, HD))],
        out_shape=[SDS((n, lay.pw), F32), SDS((n, HW), F32), SDS((n, HW), F32), SDS((HW, d), F32), SDS((HW, d), F32),
                   SDS((d, d), F32), SDS((1, HD), F32), SDS((1, HD), F32)],
        compiler_params=_params(1), name="merge_bwd")(dhn, oa, ob, proj, ga, gb, wa, wb, wo)


def _loss_head(h, target, fw, nb, nc):
    n, d = h.shape

    def body(h_ref, t_ref, fw_ref, lp_ref, dh_ref, dfw_ref):
        b, c = pl.program_id(0), pl.program_id(1)

        @pl.when((b == 0) & (c == 0))
        def _():
            dfw_ref[...] = jnp.zeros_like(dfw_ref)

        @pl.when(c == 0)
        def _():
            dh_ref[...] = jnp.zeros_like(dh_ref)
            lp_ref[...] = jnp.zeros_like(lp_ref)

        @pl.when(c > 0)
        def _():
            x = h_ref[...]
            r = lax.rsqrt(jnp.mean(x * x, axis=-1, keepdims=True) + EPS)
            xh = x * r
            err = xh * fw_ref[...] - t_ref[...]
            lp_ref[...] = jnp.zeros_like(lp_ref) + 0.5 * jnp.sum(_rs(err * err), axis=0, keepdims=True) / d
            dy = err / d
            dfw_ref[...] += jnp.sum(dy * xh, axis=0, keepdims=True)
            dxh = dy * fw_ref[...]
            dh_ref[...] = r * (dxh - xh * jnp.mean(dxh * xh, axis=-1, keepdims=True))

    return pl.pallas_call(
        body, grid=(nb, nc),
        in_specs=[pl.BlockSpec((CH, d), lambda b, c: (b * nc + c, 0)),
                  pl.BlockSpec((CH, d), lambda b, c: (b * (nc - 1) + jnp.maximum(c - 1, 0), 0)),
                  pl.BlockSpec((1, d), lambda b, c: (0, 0))],
        out_specs=[pl.BlockSpec((8, HD), lambda b, c: (b * nc + c, 0)), pl.BlockSpec((CH, d), lambda b, c: (b * nc + c, 0)),
                   pl.BlockSpec((1, d), lambda b, c: (0, 0))],
        out_shape=[SDS((nb * nc * 8, HD), F32), SDS((n, d), F32), SDS((1, d), F32)],
        compiler_params=_params(2), name="loss_head")(h, target, fw)


def _lb_fwd(lb):
    def body(x_ref, o_ref):
        x = x_ref[...]
        mx = jnp.max(x, axis=0, keepdims=True)
        e = jnp.exp(x - mx)
        sm = e / jnp.sum(e, axis=0, keepdims=True)
        run = jnp.zeros((1, HW), F32)
        for l in range(DEPTH):
            run = run + sm[l:l + 1, :]
            o_ref[l:l + 1, :] = run - sm[0:1, :]

    return pl.pallas_call(body, out_shape=SDS(lb.shape, F32), name="lb_fwd")(lb)


def _lb_bwd(lb, dlb_all):
    def body(x_ref, d_ref, o_ref):
        x = x_ref[...]
        dl = d_ref[...]
        mx = jnp.max(x, axis=0, keepdims=True)
        e = jnp.exp(x - mx)
        sm = e / jnp.sum(e, axis=0, keepdims=True)
        tot = jnp.sum(dl, axis=0, keepdims=True)
        dsm = []
        run = tot
        for l in range(DEPTH):
            dsm.append(run - (tot if l == 0 else 0.0))
            run = run - dl[l:l + 1, :]
        inner = sum(sm[l:l + 1, :] * dsm[l] for l in range(DEPTH))
        for l in range(DEPTH):
            o_ref[l:l + 1, :] = sm[l:l + 1, :] * (dsm[l] - inner)

    return pl.pallas_call(body, out_shape=SDS(lb.shape, F32), name="lb_bwd")(lb, dlb_all)


def _adamw(g, w, m, v):
    r, c = g.shape
    tr = _tile(r, 264)
    c1 = 1.0 / (1.0 - ADAM_B1 ** ADAM_STEP)
    c2 = 1.0 / (1.0 - ADAM_B2 ** ADAM_STEP)

    def body(g_ref, w_ref, m_ref, v_ref, d_ref, mo_ref, vo_ref):
        gg = g_ref[...]
        mn = ADAM_B1 * m_ref[...] + (1.0 - ADAM_B1) * gg
        vn = ADAM_B2 * v_ref[...] + (1.0 - ADAM_B2) * gg * gg
        d_ref[...] = -ADAM_LR * ((mn * c1) / (jnp.sqrt(vn * c2) + ADAM_EPS) + ADAM_WD * w_ref[...])
        mo_ref[...] = mn
        vo_ref[...] = vn

    spec = pl.BlockSpec((tr, c), lambda i: (i, 0))
    return pl.pallas_call(body, grid=(r // tr,), in_specs=[spec] * 4, out_specs=[spec] * 3, out_shape=[SDS(g.shape, F32)] * 3,
                          compiler_params=_params(1), name="adamw")(g, w, m, v)


def _tile16(n, target):
    return _tile(n // 2, target // 2) * 2 if n % 16 == 0 else _tile(n, target)


def _add_rows(a, b, name):
    k, r, c = a.shape
    tr = _tile16(r, 264)
    spec = pl.BlockSpec((None, tr, c), lambda s, i: (s, i, 0))

    def body(a_ref, b_ref, o_ref):
        o_ref[...] = (a_ref[...] + b_ref[...].astype(F32)).astype(o_ref.dtype)

    return pl.pallas_call(body, grid=(k, r // tr), in_specs=[spec, spec], out_specs=spec, out_shape=SDS(a.shape, b.dtype),
                          compiler_params=_params(2), name=name)(a, b)


def _sum_chips(parts):
    k, r, c = parts.shape
    tr = _tile16(r, 264)

    def body(p_ref, o_ref):
        acc = p_ref[0].astype(F32)
        for s in range(1, k):
            acc = acc + p_ref[s].astype(F32)
        o_ref[...] = acc

    return pl.pallas_call(body, grid=(r // tr,), in_specs=[pl.BlockSpec((k, tr, c), lambda i: (0, i, 0))],
                          out_specs=pl.BlockSpec((tr, c), lambda i: (i, 0)), out_shape=SDS((r, c), F32),
                          compiler_params=_params(1), name="sum_chips")(parts)


def _meta_grad(dh, nb, nc):
    d = dh.shape[1]

    def body(x_ref, o_ref):
        @pl.when(pl.program_id(0) == 0)
        def _():
            o_ref[...] = jnp.zeros_like(o_ref)

        o_ref[...] += x_ref[PAD:CH, :]

    return pl.pallas_call(body, grid=(nb,), in_specs=[pl.BlockSpec((CH, d), lambda b: (b * nc, 0))],
                          out_specs=pl.BlockSpec((N_META, d), lambda b: (0, 0)), out_shape=SDS((N_META, d), F32),
                          compiler_params=_params(1), name="meta_grad")(dh)


ANY = pl.BlockSpec(memory_space=pl.ANY)


def _place():
    x, y, c = lax.axis_index("x"), lax.axis_index("y"), lax.axis_index("c")
    chips = [(1 - x, y), (x, 1 - y), (1 - x, 1 - y)]
    return x, y, c, chips


def _remote(src, dst, send_sems, recv_sems, k, to):
    return pltpu.make_async_remote_copy(src_ref=src, dst_ref=dst, send_sem=send_sems.at[k], recv_sem=recv_sems.at[k],
                                        device_id=to, device_id_type=MESH)


def _gather_weights(pb, ps):
    def body(pb_ref, ps_ref, gb_ref, gs_ref, send_sems, recv_sems, local_sems):
        x, y, c, chips = _place()
        s = 2 * x + y
        sib = (x, y, 1 - c)
        l0 = pltpu.make_async_copy(pb_ref, gb_ref.at[s], local_sems.at[0])
        l1 = pltpu.make_async_copy(ps_ref, gs_ref.at[s], local_sems.at[1])
        l0.start()
        l1.start()
        sends = []
        for k, (px, py) in enumerate(chips):
            sends.append(_remote(pb_ref.at[c], gb_ref.at[s, c], send_sems, recv_sems, k, (px, py, c)))
            sends.append(_remote(ps_ref, gs_ref.at[s], send_sems, recv_sems, 6 + k, (px, py, c)))
        for cp in sends:
            cp.start()
        for k, (px, py) in enumerate(chips):
            sk = 2 * px + py
            _remote(pb_ref.at[c], gb_ref.at[sk, c], send_sems, recv_sems, k, sib).wait_recv()
            fwd = _remote(gb_ref.at[sk, c], gb_ref.at[sk, c], send_sems, recv_sems, 3 + k, sib)
            fwd.start()
            sends.append(fwd)
        for k, (px, py) in enumerate(chips):
            sk = 2 * px + py
            _remote(pb_ref.at[c], gb_ref.at[sk, 1 - c], send_sems, recv_sems, 3 + k, sib).wait_recv()
            _remote(ps_ref, gs_ref.at[sk], send_sems, recv_sems, 6 + k, sib).wait_recv()
        for cp in sends:
            cp.wait_send()
        l0.wait()
        l1.wait()

    return pl.pallas_call(
        body, in_specs=[ANY, ANY], out_specs=[ANY, ANY],
        out_shape=[SDS((4,) + pb.shape, pb.dtype), SDS((4,) + ps.shape, ps.dtype)],
        scratch_shapes=[pltpu.SemaphoreType.DMA((9,)), pltpu.SemaphoreType.DMA((9,)), pltpu.SemaphoreType.DMA((2,))],
        name="gather_weights")(pb, ps)


def _sem_scratch(n_remote, n_local):
    return [pltpu.SemaphoreType.DMA((n_remote,)), pltpu.SemaphoreType.DMA((n_remote,)), pltpu.SemaphoreType.DMA((n_local,))]


def _swap_halves(gs, sends):
    nt = len(gs)

    def body(*refs):
        g_refs, s_refs, own_refs, got_refs = refs[:nt], refs[nt:2 * nt], refs[2 * nt:3 * nt], refs[3 * nt:4 * nt]
        send_sems, recv_sems, local_sems = refs[4 * nt:]
        x, y, c, _ = _place()
        sib = (x, y, 1 - c)
        local, remote = [], []
        for t in range(nt):
            for s in range(4):
                local.append(pltpu.make_async_copy(g_refs[t].at[c, s], own_refs[t].at[s], local_sems.at[4 * t + s]))
                remote.append(_remote(s_refs[t].at[1 - c, s], got_refs[t].at[s], send_sems, recv_sems, 4 * t + s, sib))
        for cp in remote + local:
            cp.start()
        for cp in remote + local:
            cp.wait()

    out = pl.pallas_call(
        body, in_specs=[ANY] * (2 * nt), out_specs=[ANY] * (2 * nt),
        out_shape=[SDS(g.shape[1:], g.dtype) for g in gs] + [SDS(g.shape[1:], g.dtype) for g in sends],
        scratch_shapes=_sem_scratch(4 * nt, 4 * nt), name="swap_halves")(*gs, *sends)
    return out[:nt], out[nt:]


def _scatter_chip_sums(parts):
    nt = len(parts)

    def body(*refs):
        a_refs, r_refs = refs[:nt], refs[nt:2 * nt]
        send_sems, recv_sems, local_sems = refs[2 * nt:]
        x, y, c, chips = _place()
        s = 2 * x + y
        local = [pltpu.make_async_copy(a_refs[t].at[s], r_refs[t].at[s], local_sems.at[t]) for t in range(nt)]
        sends = [_remote(a_refs[t].at[2 * px + py], r_refs[t].at[s], send_sems, recv_sems, 3 * t + k, (px, py, c))
                 for t in range(nt) for k, (px, py) in enumerate(chips)]
        for cp in sends + local:
            cp.start()
        for t in range(nt):
            for k, (px, py) in enumerate(chips):
                _remote(a_refs[t].at[s], r_refs[t].at[2 * px + py], send_sems, recv_sems, 3 * t + k, (px, py, c)).wait_recv()
        for cp in sends:
            cp.wait_send()
        for cp in local:
            cp.wait()

    return pl.pallas_call(
        body, in_specs=[ANY] * nt, out_specs=[ANY] * nt, out_shape=[SDS(a.shape, a.dtype) for a in parts],
        scratch_shapes=_sem_scratch(3 * nt, nt), name="scatter_chip_sums")(*parts)


def _join_halves(ts):
    nt = len(ts)

    def body(*refs):
        t_refs, f_refs = refs[:nt], refs[nt:2 * nt]
        send_sems, recv_sems, local_sems = refs[2 * nt:]
        x, y, c, _ = _place()
        sib = (x, y, 1 - c)
        local = [pltpu.make_async_copy(t_refs[t], f_refs[t].at[c], local_sems.at[t]) for t in range(nt)]
        sends = [_remote(t_refs[t], f_refs[t].at[c], send_sems, recv_sems, t, sib) for t in range(nt)]
        for cp in sends + local:
            cp.start()
        for t in range(nt):
            _remote(t_refs[t], f_refs[t].at[1 - c], send_sems, recv_sems, t, sib).wait_recv()
        for cp in sends:
            cp.wait_send()
        for cp in local:
            cp.wait()

    return pl.pallas_call(
        body, in_specs=[ANY] * nt, out_specs=[ANY] * nt, out_shape=[SDS((2,) + t.shape, t.dtype) for t in ts],
        scratch_shapes=_sem_scratch(nt, nt), name="join_halves")(*ts)


def _uncontain(cont, n_head, width):
    r, cw = cont.shape
    tr = _tile(r, 256)

    def body(n_ref, x_ref, o_ref):
        o_ref[...] = pltpu.roll(x_ref[...], n_ref[0], axis=1)[:, :width]

    return pl.pallas_call(
        body, grid_spec=pltpu.PrefetchScalarGridSpec(
            num_scalar_prefetch=1, grid=(r // tr,), in_specs=[pl.BlockSpec((tr, cw), lambda i, n: (i, 0))],
            out_specs=pl.BlockSpec((tr, width), lambda i, n: (i, 0))),
        out_shape=SDS((r, width), F32), compiler_params=_params(1), name="uncontain")(n_head, cont)


WEIGHTS = ("meta_tokens", "norm_w", "w_in", "conv_w", "a_log", "dt_bias", "gnorm_a", "gnorm_b", "hgrn_lower_bounds",
           "w_branch_a", "w_branch_b", "w_out", "final_norm_w")
SHARD_AXIS = {"meta_tokens": 1, "w_in": 2, "conv_w": 2, "w_branch_a": 2, "w_branch_b": 2, "w_out": 1}
FLAT_C = 1024


def _flat(parts, rows, cols=FLAT_C):
    v = jnp.concatenate([p.reshape(-1) for p in parts])
    return jnp.pad(v, (0, rows * cols - v.shape[0])).reshape(rows, cols)


def _local_step(x, target, w, lay):
    nb, seq, d = x.shape
    tp = CH + seq
    nc = tp // CH
    n = nb * tp
    e_mat, s_mat = _gate_consts()
    lb_all = _lb_fwd(w["hgrn_lower_bounds"])
    h = jnp.concatenate([jnp.zeros((nb, PAD, d), F32), jnp.broadcast_to(w["meta_tokens"][None], (nb, N_META, d)), x],
                        axis=1).reshape(n, d)
    rep = lambda a: jnp.repeat(a, HD)[None, :]
    saved = []
    for l in range(DEPTH):
        nw = w["norm_w"][l][None, :]
        proj = _norm_proj_fwd(h, nw, w["w_in"][l])
        qkv = _gdn_prep_fwd(proj, w["conv_w"][l], lay, nb, tp)
        alog, dtb = rep(w["a_log"][l]), rep(w["dt_bias"][l])
        oa, sa = _gdn_fwd(qkv, proj, e_mat, alog, dtb, lay, nb, nc)
        lbl = lb_all[l][None, :]
        ob, sb = _hgrn_fwd(proj, lbl, lay, nb, nc)
        ga, gb = w["gnorm_a"][l][None, :], w["gnorm_b"][l][None, :]
        hn = _merge_fwd(h, oa, ob, proj, ga, gb, w["w_branch_a"][l], w["w_branch_b"][l], w["w_out"][l], lay)
        saved.append((h, nw, proj, qkv, alog, dtb, oa, sa, lbl, ob, sb, ga, gb))
        h = hn
    lp, dh, dfw = _loss_head(h, target.reshape(nb * seq, d), w["final_norm_w"][None, :], nb, nc)
    loss = jnp.sum(lp[::8, 0])
    g = {n_: [None] * DEPTH for n_ in WEIGHTS}
    dlb_all = [None] * DEPTH
    for l in reversed(range(DEPTH)):
        h, nw, proj, qkv, alog, dtb, oa, sa, lbl, ob, sb, ga, gb = saved[l]
        dproj, doa, dob, dwa, dwb, dwo, dga, dgb = _merge_bwd(dh, oa, ob, proj, ga, gb, w["w_branch_a"][l],
                                                             w["w_branch_b"][l], w["w_out"][l], lay, tp)
        dproj, acc_b = _hgrn_bwd(proj, lbl, sb, dob, dproj, lay, nb, nc)
        dqkv, dproj, acc_a = _gdn_bwd(qkv, proj, e_mat, s_mat, alog, dtb, sa, doa, dproj, lay, nb, nc)
        dproj, dconv = _gdn_prep_bwd(proj, w["conv_w"][l], dqkv, dproj, lay, nb, tp)
        dh, dnw = _proj_bwd_dx(dproj, w["w_in"][l], h, nw, dh, tp)
        g["w_in"][l] = _proj_bwd_dw(dproj, h, nw, tp)
        g["norm_w"][l] = dnw[0]
        g["conv_w"][l] = dconv
        g["a_log"][l] = acc_a[0, ::HD]
        g["dt_bias"][l] = acc_a[1, ::HD]
        g["gnorm_a"][l], g["gnorm_b"][l] = dga[0], dgb[0]
        g["w_branch_a"][l], g["w_branch_b"][l], g["w_out"][l] = dwa, dwb, dwo
        dlb_all[l] = acc_b[0]
    grads = {n_: jnp.stack(v) for n_, v in g.items() if v[0] is not None}
    grads["hgrn_lower_bounds"] = _lb_bwd(w["hgrn_lower_bounds"], jnp.stack(dlb_all))
    grads["final_norm_w"] = dfw[0]
    grads["meta_tokens"] = _meta_grad(dh, nb, nc)
    grad_x = dh.reshape(nb, tp, d)[:, CH:, :]
    return loss, grad_x, grads


def kernel(x, meta_tokens, norm_w, w_in, conv_w, a_log, dt_bias, gnorm_a, gnorm_b, hgrn_lower_bounds, w_branch_a, w_branch_b, w_out, final_norm_w, loss_target, m_meta_tokens, m_norm_w, m_w_in, m_conv_w, m_a_log, m_dt_bias, m_gnorm_a, m_gnorm_b, m_hgrn_lower_bounds, m_w_branch_a, m_w_branch_b, m_w_out, m_final_norm_w, v_meta_tokens, v_norm_w, v_w_in, v_conv_w, v_a_log, v_dt_bias, v_gnorm_a, v_gnorm_b, v_hgrn_lower_bounds, v_w_branch_a, v_w_branch_b, v_w_out, v_final_norm_w):
    wl = dict(meta_tokens=meta_tokens, norm_w=norm_w, w_in=w_in, conv_w=conv_w, a_log=a_log, dt_bias=dt_bias, gnorm_a=gnorm_a,
              gnorm_b=gnorm_b, hgrn_lower_bounds=hgrn_lower_bounds, w_branch_a=w_branch_a, w_branch_b=w_branch_b, w_out=w_out,
              final_norm_w=final_norm_w)
    ml = dict(zip(WEIGHTS, (m_meta_tokens, m_norm_w, m_w_in, m_conv_w, m_a_log, m_dt_bias, m_gnorm_a, m_gnorm_b,
                            m_hgrn_lower_bounds, m_w_branch_a, m_w_branch_b, m_w_out, m_final_norm_w)))
    vl = dict(zip(WEIGHTS, (v_meta_tokens, v_norm_w, v_w_in, v_conv_w, v_a_log, v_dt_bias, v_gnorm_a, v_gnorm_b,
                            v_hgrn_lower_bounds, v_w_branch_a, v_w_branch_b, v_w_out, v_final_norm_w)))
    d = x.shape[2]
    lay = _Layout(d)
    nchip = 4

    big = ("w_in", "w_branch_a", "w_branch_b", "w_out")
    small = ("conv_w", "meta_tokens")
    nbig = sum(int(np.prod(wl[n].shape)) for n in big)
    rb = -(-nbig // (2 * FLAT_C * 16)) * 16
    pb = _flat([wl[n].astype(BF16) for n in big], 2 * rb).reshape(2, rb, FLAT_C)
    nsmall = sum(int(np.prod(wl[n].shape)) for n in small)
    rs = -(-nsmall // (HD * 8)) * 8
    ps = jnp.pad(jnp.concatenate([wl[n].reshape(-1) for n in small]), (0, rs * HD - nsmall)).reshape(rs, HD)
    gbig, gsmall = _gather_weights(pb, ps)
    gbig = gbig.reshape(nchip, -1)
    gsmall = gsmall.reshape(nchip, -1)

    def whole(flat, names, src):
        out, o = {}, 0
        for n in names:
            shp = src[n].shape
            sz = int(np.prod(shp))
            a = flat[:, o:o + sz].reshape((nchip,) + shp)
            ax = SHARD_AXIS[n]
            out[n] = jnp.concatenate([a[s] for s in range(nchip)], axis=ax)
            o += sz
        return out

    wf = dict(wl)
    wf.update(whole(gbig, big, wl))
    wf.update(whole(gsmall, small, wl))
    wf["w_in"] = lay.to_kernel(wf["w_in"])

    loss_part, grad_x, gfull = _local_step(x, loss_target, wf, lay)
    loss = lax.psum(loss_part, ("x", "y", "c"))

    sw = wl["w_in"].shape[2]
    conts, heads = lay.containers(gfull["w_in"], nchip)
    dd = wl["w_branch_a"].shape[2]
    rows_o = wl["w_out"].shape[1]
    by_dest = lambda g, n: [lax.slice_in_dim(g, s * wl[n].shape[SHARD_AXIS[n]], (s + 1) * wl[n].shape[SHARD_AXIS[n]],
                                            axis=SHARD_AXIS[n]) if n in SHARD_AXIS else g for s in range(nchip)]
    small_names = tuple(n for n in WEIGHTS if n not in big)
    nsm = sum(int(np.prod(wl[n].shape)) for n in small_names)
    rsm = -(-nsm // (2 * HD * 8)) * 8
    pack_small = lambda parts: _flat(parts, 2 * rsm, HD).reshape(2, rsm, HD)
    small_by_dest = [by_dest(gfull[n], n) for n in small_names]
    gs = [jnp.stack(conts, axis=1),
          jnp.stack(by_dest(gfull["w_branch_a"], "w_branch_a"), axis=1),
          jnp.stack(by_dest(gfull["w_branch_b"], "w_branch_b"), axis=1),
          gfull["w_out"].reshape(DEPTH, nchip, rows_o, d),
          jnp.stack([pack_small([p[s] for p in small_by_dest]) for s in range(nchip)], axis=1)]
    own, got = _swap_halves(gs, [g.astype(BF16) for g in gs[:4]] + gs[4:])
    chip_sums = [_add_rows(a.reshape((nchip, -1, a.shape[-1])), b.reshape((nchip, -1, b.shape[-1])), "add_cores")
                 for a, b in zip(own, got)]
    by_chip = _scatter_chip_sums(chip_sums)
    mine = [_sum_chips(p) for p in by_chip]
    full = _join_halves(mine)
    my_chip = 2 * lax.axis_index("x") + lax.axis_index("y")
    n_head = sum(jnp.where(my_chip == s, heads[s], 0) for s in range(nchip)).astype(jnp.int32).reshape(1)
    g_w_in = _uncontain(full[0].reshape(DEPTH * d, -1), n_head, sw)
    g2 = {"w_in": g_w_in, "w_branch_a": full[1].reshape(-1, dd), "w_branch_b": full[2].reshape(-1, dd),
          "w_out": full[3].reshape(-1, d), "small": full[4].reshape(2 * rsm, HD)}

    def two_d(src, n):
        if n == "small":
            return _flat([src[k] for k in small_names], 2 * rsm, HD)
        return src[n].reshape(g2[n].shape)

    outs = {}
    for n in big + ("small",):
        delta, mnew, vnew = _adamw(g2[n], two_d(wl, n), two_d(ml, n), two_d(vl, n))
        outs[n] = (g2[n], delta, mnew, vnew)
    res = [{}, {}, {}, {}]
    for i in range(4):
        for n in big:
            res[i][n] = outs[n][i].reshape(wl[n].shape)
        v, o = outs["small"][i].reshape(-1), 0
        for n in small_names:
            sz = int(np.prod(wl[n].shape))
            res[i][n] = v[o:o + sz].reshape(wl[n].shape)
            o += sz
    return (loss, grad_x, *[res[0][n] for n in WEIGHTS], *[res[1][n] for n in WEIGHTS], *[res[2][n] for n in WEIGHTS],
            *[res[3][n] for n in WEIGHTS])
```

```python
import functools

import numpy as np
import jax
import jax.numpy as jnp
from jax import lax
from jax.experimental import pallas as pl
from jax.experimental.pallas import tpu as pltpu

F32 = jnp.float32
BF16 = jnp.bfloat16
HI = lax.Precision.HIGHEST
SDS = jax.ShapeDtypeStruct

NH = 4
HD = 128
HW = NH * HD
N_META = 16
CH = 64
SUB = 16
PAD = CH - N_META
EPS = 1e-6
Q_SCALE = HD ** -0.5
DEPTH = 2
CONV_K = 4
VMEM_LIMIT = 56 * 1024 * 1024
ADAM_LR, ADAM_B1, ADAM_B2, ADAM_EPS, ADAM_WD, ADAM_STEP = 0.001, 0.9, 0.999, 1e-08, 0.01, 10
MESH = pl.DeviceIdType.MESH


def _nn(a, b):
    return jnp.dot(a, b, precision=HI, preferred_element_type=F32)


def _nt(a, b):
    return lax.dot_general(a, b, (((1,), (1,)), ((), ())), precision=HI, preferred_element_type=F32)


def _tn(a, b):
    return _nn(a.T, b)


def _scan_rows(x, group, reverse=False):
    n = x.shape[0]
    pos = lax.bitwise_and(_iota2(x.shape, 0), group - 1)
    s = 1
    while s < group:
        if reverse:
            x = x + jnp.where(pos < group - s, pltpu.roll(x, n - s, axis=0), 0.0)
        else:
            x = x + jnp.where(pos >= s, pltpu.roll(x, s, axis=0), 0.0)
        s *= 2
    return x


def _bnn(a, b):
    return jnp.dot(a.astype(BF16), b.astype(BF16), preferred_element_type=F32)


def _bnt(a, b):
    return lax.dot_general(a.astype(BF16), b.astype(BF16), (((1,), (1,)), ((), ())), preferred_element_type=F32)


def _btn(a, b):
    return lax.dot_general(a.astype(BF16), b.astype(BF16), (((0,), (0,)), ((), ())), preferred_element_type=F32)


def _hi_lo(x):
    hi = x.astype(jnp.bfloat16)
    return hi, (x - hi.astype(F32)).astype(jnp.bfloat16)


def _dot3(dims):
    def f(a, b):
        ah, al = _hi_lo(a)
        bh, bl = _hi_lo(b)
        d = lambda p, q: lax.dot_general(p, q, (dims, ((), ())), preferred_element_type=F32)
        return d(ah, bh) + (d(ah, bl) + d(al, bh))
    return f


_rnn, _rnt, _rtn = _dot3(((1,), (0,))), _dot3(((1,), (1,))), _dot3(((0,), (0,)))
_hnn, _hnt, _htn = _bnn, _bnt, _btn


def _rr(x):
    return x


def _sig(x):
    return jax.nn.sigmoid(x)


def _silu(x):
    return x * _sig(x)


def _dsilu(x):
    s = _sig(x)
    return s * (1.0 + x * (1.0 - s))


def _softplus(x):
    return jnp.maximum(x, 0.0) + jnp.log(1.0 + jnp.exp(-jnp.abs(x)))


def _logsig(x):
    return jnp.minimum(x, 0.0) - jnp.log(1.0 + jnp.exp(-jnp.abs(x)))


def _rs(x):
    return jnp.sum(x, axis=-1, keepdims=True)


def _params(n_axes):
    return pltpu.CompilerParams(dimension_semantics=("arbitrary",) * n_axes, vmem_limit_bytes=VMEM_LIMIT)


def _tile(n, target):
    best = 8
    for t in range(8, target + 1, 8):
        if n % t == 0:
            best = t
    return best


def _ctile(pw, most=7):
    return HD * max(k for k in range(1, most + 1) if (pw // HD) % k == 0)


def _iota2(shape, axis):
    return lax.broadcasted_iota(jnp.int32, shape, axis)


class _Layout:
    def __init__(self, d):
        self.d = d
        self.wm = 2 * HW + 2 * d
        self.c_qkv = self.wm
        self.c_b = self.wm + 3 * HW
        self.c_ba = self.wm + 6 * HW
        self.pw = self.c_ba + HD
        assert self.c_b % (3 * HW) == 0
        o = 0
        segs = {}
        for name, w in (("a_q", HW), ("a_k", HW), ("a_v", HW), ("ba", 2 * NH), ("a_z", HW), ("b_q", HW), ("b_f", HW),
                        ("b_i", HW), ("b_g", HW), ("gate_a", d), ("gate_b", d)):
            segs[name] = (o, o + w)
            o += w
        self.segs = segs
        self.width = o
        self.order = ("a_z", "b_g", "gate_a", "gate_b", "a_q", "a_k", "a_v", "b_q", "b_f", "b_i", "ba")

    def to_kernel(self, w):
        parts = [w[..., self.segs[n][0]:self.segs[n][1]] for n in self.order]
        parts.append(jnp.zeros(w.shape[:-1] + (HD - 2 * NH,), w.dtype))
        return jnp.concatenate(parts, axis=-1)

    def containers(self, g, nchip):
        off, where = 0, {}
        for n in self.order:
            where[n] = off
            off += self.segs[n][1] - self.segs[n][0]
        names = sorted(self.segs, key=lambda n: self.segs[n][0])
        sw = self.width // nchip
        cw = -(-sw // HD) * HD
        out, heads = [], []
        for s in range(nchip):
            lo, hi = s * sw, (s + 1) * sw
            pieces = []
            for n in names:
                a, b = max(lo, self.segs[n][0]), min(hi, self.segs[n][1])
                if a < b:
                    pieces.append((where[n] + a - self.segs[n][0], b - a))
            start, width = pieces[0]
            n_head = min((-start) % HD, width)
            body = ([(start + n_head, width - n_head)] if width > n_head else []) + pieces[1:]
            parts = [g[..., c:c + w] for c, w in body]
            parts.append(jnp.zeros(g.shape[:-1] + (cw - sw,), g.dtype))
            if n_head:
                parts.append(g[..., start:start + n_head])
            out.append(jnp.concatenate(parts, axis=-1))
            heads.append(n_head)
        return out, heads

    def from_kernel(self, g):
        off, where = 0, {}
        for n in self.order:
            w = self.segs[n][1] - self.segs[n][0]
            where[n] = (off, off + w)
            off += w
        names = sorted(self.segs, key=lambda n: self.segs[n][0])
        return jnp.concatenate([g[..., where[n][0]:where[n][1]] for n in names], axis=-1)


def _norm_proj_fwd(h, nw, wp):
    n, d = h.shape
    pw = wp.shape[1]
    tm, tn = _tile(n, 768), _ctile(pw)

    def body(h_ref, nw_ref, w_ref, o_ref, xn_ref):
        @pl.when(pl.program_id(1) == 0)
        def _():
            x = h_ref[...]
            r = lax.rsqrt(jnp.mean(x * x, axis=-1, keepdims=True) + EPS)
            xn_ref[...] = (x * r * nw_ref[...]).astype(BF16)

        o_ref[...] = jnp.dot(xn_ref[...], w_ref[...], preferred_element_type=F32)

    return pl.pallas_call(
        body, grid=(n // tm, pw // tn),
        in_specs=[pl.BlockSpec((tm, d), lambda i, j: (i, 0)), pl.BlockSpec((1, d), lambda i, j: (0, 0)),
                  pl.BlockSpec((d, tn), lambda i, j: (0, j))],
        out_specs=pl.BlockSpec((tm, tn), lambda i, j: (i, j)), out_shape=SDS((n, pw), F32),
        scratch_shapes=[pltpu.VMEM((tm, d), BF16)], compiler_params=_params(2), name="norm_proj_fwd")(h, nw, wp)


def _row_valid(tm, tp, base):
    row = base + _iota2((tm, 1), 0)
    return lax.rem(row, tp) >= PAD


def _proj_bwd_dx(dproj, wp, h, nw, dhn, tp):
    n, d = h.shape
    pw = wp.shape[1]
    tm, tk = _tile(n, 768), _ctile(pw)
    nk = pw // tk

    def body(dp_ref, w_ref, h_ref, nw_ref, dhn_ref, dh_ref, dnw_ref, acc_ref):
        i, k = pl.program_id(0), pl.program_id(1)

        @pl.when(k == 0)
        def _():
            acc_ref[...] = jnp.zeros_like(acc_ref)

        @pl.when((i == 0) & (k == 0))
        def _():
            dnw_ref[...] = jnp.zeros_like(dnw_ref)

        valid = _row_valid(tm, tp, i * tm)
        dp = jnp.where(valid, dp_ref[...], 0.0)
        acc_ref[...] += _bnt(dp, w_ref[...])

        @pl.when(k == nk - 1)
        def _():
            x = h_ref[...]
            r = lax.rsqrt(jnp.mean(x * x, axis=-1, keepdims=True) + EPS)
            xh = x * r
            dxn = acc_ref[...]
            dnw_ref[...] += jnp.sum(dxn * xh, axis=0, keepdims=True)
            dxh = dxn * nw_ref[...]
            dh_ref[...] = dhn_ref[...] + r * (dxh - xh * jnp.mean(dxh * xh, axis=-1, keepdims=True))

    return pl.pallas_call(
        body, grid=(n // tm, nk),
        in_specs=[pl.BlockSpec((tm, tk), lambda i, k: (i, k)), pl.BlockSpec((d, tk), lambda i, k: (0, k)),
                  pl.BlockSpec((tm, d), lambda i, k: (i, 0)), pl.BlockSpec((1, d), lambda i, k: (0, 0)),
                  pl.BlockSpec((tm, d), lambda i, k: (i, 0))],
        out_specs=[pl.BlockSpec((tm, d), lambda i, k: (i, 0)), pl.BlockSpec((1, d), lambda i, k: (0, 0))],
        out_shape=[SDS((n, d), F32), SDS((1, d), F32)],
        scratch_shapes=[pltpu.VMEM((tm, d), F32)], compiler_params=_params(2), name="proj_bwd_dx")(dproj, wp, h, nw, dhn)


def _proj_bwd_dw(dproj, h, nw, tp):
    n, d = h.shape
    pw = dproj.shape[1]
    tm, tn = _tile(n, 768), _ctile(pw)

    def body(dp_ref, h_ref, nw_ref, dw_ref):
        i = pl.program_id(1)

        @pl.when(i == 0)
        def _():
            dw_ref[...] = jnp.zeros_like(dw_ref)

        x = h_ref[...]
        r = lax.rsqrt(jnp.mean(x * x, axis=-1, keepdims=True) + EPS)
        xn = x * r * nw_ref[...]
        dp = jnp.where(_row_valid(tm, tp, i * tm), dp_ref[...], 0.0)
        dw_ref[...] += _btn(xn, dp)

    return pl.pallas_call(
        body, grid=(pw // tn, n // tm),
        in_specs=[pl.BlockSpec((tm, tn), lambda j, i: (i, j)), pl.BlockSpec((tm, d), lambda j, i: (i, 0)),
                  pl.BlockSpec((1, d), lambda j, i: (0, 0))],
        out_specs=pl.BlockSpec((d, tn), lambda j, i: (0, j)), out_shape=SDS((d, pw), F32),
        compiler_params=_params(2), name="proj_bwd_dw")(dproj, h, nw)


def _conv_silu(x, w, row):
    c = x * w[CONV_K - 1:CONV_K, :]
    for k in range(1, CONV_K):
        c = c + jnp.where(row >= k, pltpu.roll(x, k, axis=0), 0.0) * w[CONV_K - 1 - k:CONV_K - k, :]
    return c


def _gdn_prep_fwd(proj, conv_w, lay, nb, tp):
    n = proj.shape[0]
    nblk = 3 * NH
    cb = lay.c_qkv // HD

    def body(p_ref, w_ref, o_ref):
        j = pl.program_id(1)
        x = p_ref[...]
        row = _iota2(x.shape, 0)
        c = _conv_silu(x, w_ref[...], row)
        s = _silu(c)
        r = lax.rsqrt(_rs(s * s) + EPS)
        scale = jnp.where(j < NH, Q_SCALE, 1.0)
        y = jnp.where(j < 2 * NH, s * r * scale, s)
        o_ref[...] = jnp.where(row >= PAD, y, 0.0)

    return pl.pallas_call(
        body, grid=(nb, nblk),
        in_specs=[pl.BlockSpec((tp, HD), lambda b, j: (b, cb + j)), pl.BlockSpec((CONV_K, HD), lambda b, j: (0, j))],
        out_specs=pl.BlockSpec((tp, HD), lambda b, j: (b, j)), out_shape=SDS((n, nblk * HD), F32),
        compiler_params=_params(2), name="gdn_prep_fwd")(proj, conv_w)


def _gdn_prep_bwd(proj, conv_w, dqkv, dproj, lay, nb, tp):
    nblk = 3 * NH
    cb = lay.c_qkv // HD

    def body(p_ref, w_ref, dy_ref, dp_in, dp_ref, dw_ref):
        j, b = pl.program_id(0), pl.program_id(1)
        x = p_ref[...]
        w = w_ref[...]
        row = _iota2(x.shape, 0)
        c = _conv_silu(x, w, row)
        s = _silu(c)
        dy = jnp.where(row >= PAD, dy_ref[...], 0.0)
        r = lax.rsqrt(_rs(s * s) + EPS)
        nh = s * r
        scale = jnp.where(j < NH, Q_SCALE, 1.0)
        ds_n = scale * r * (dy - nh * _rs(dy * nh))
        ds = jnp.where(j < 2 * NH, ds_n, dy)
        dc = ds * _dsilu(c)
        dx = dc * w[CONV_K - 1:CONV_K, :]
        dws = [jnp.sum(dc * x, axis=0, keepdims=True)]
        for k in range(1, CONV_K):
            dx = dx + jnp.where(row < tp - k, pltpu.roll(dc, tp - k, axis=0), 0.0) * w[CONV_K - 1 - k:CONV_K - k, :]
            xs = jnp.where(row >= k, pltpu.roll(x, k, axis=0), 0.0)
            dws.append(jnp.sum(dc * xs, axis=0, keepdims=True))
        dp_ref[...] = dx
        r4 = _iota2((CONV_K, HD), 0)
        dw = jnp.zeros((CONV_K, HD), F32)
        for k in range(CONV_K):
            dw = dw + jnp.where(r4 == CONV_K - 1 - k, dws[k], 0.0)

        @pl.when(b == 0)
        def _():
            dw_ref[...] = dw

        @pl.when(b > 0)
        def _():
            dw_ref[...] += dw

    return pl.pallas_call(
        body, grid=(nblk, nb),
        in_specs=[pl.BlockSpec((tp, HD), lambda j, b: (b, cb + j)), pl.BlockSpec((CONV_K, HD), lambda j, b: (0, j)),
                  pl.BlockSpec((tp, HD), lambda j, b: (b, j)), pl.BlockSpec(memory_space=pl.ANY)],
        out_specs=[pl.BlockSpec((tp, HD), lambda j, b: (b, cb + j)), pl.BlockSpec((CONV_K, HD), lambda j, b: (0, j))],
        out_shape=[SDS(dproj.shape, F32), SDS((CONV_K, nblk * HD), F32)],
        input_output_aliases={3: 0}, compiler_params=_params(2), name="gdn_prep_bwd")(proj, conv_w, dqkv, dproj)


def _gate_consts():
    e = np.zeros((HD, 2 * HW), np.float32)
    s = np.zeros((2 * HW, HD), np.float32)
    for h in range(NH):
        e[h, h * HD:(h + 1) * HD] = 1.0
        e[NH + h, HW + h * HD:HW + (h + 1) * HD] = 1.0
        s[h * HD, h] = 1.0
        s[HW + h * HD, NH + h] = 1.0
    return jnp.asarray(e), jnp.asarray(s)


def _gdn_tri():
    i, j = _iota2((CH, CH), 0), _iota2((CH, CH), 1)
    return i >= j, i > j


def _each(fn, *lists):
    return [fn(*xs) for xs in zip(*lists)]


def _tri_inv(a_list, eye):
    p = [-a for a in a_list]
    t = [eye + x for x in p]
    for _ in range(5):
        p = _each(_rnn, p, p)
        tp_ = _each(_rnn, t, p)
        t = _each(lambda x, y: x + y, t, tp_)
    return t


def _gdn_chunks(args):
    causal, strict = _gdn_tri()
    eye = jnp.where(_iota2((CH, CH), 0) == _iota2((CH, CH), 1), 1.0, 0.0)
    q, k, v, beta, g, s0 = (list(t) for t in zip(*args))
    gc = [_scan_rows(x, CH) for x in g]
    dm = [jnp.where(causal, jnp.exp(jnp.where(causal, x[:, :CH] - x[:, :CH].T, 0.0)), 0.0) for x in gc]
    ds = [jnp.where(strict, x, 0.0) for x in dm]
    kb = _each(lambda x, y: x * y, k, beta)
    kk = _each(_rnt, kb, k)
    a = _each(lambda x, y: x * y, kk, ds)
    tinv = _tri_inv(a, eye)
    eg = [jnp.exp(x) for x in gc]
    rw = _each(lambda x, y: x * y, kb, eg)
    rv = _each(lambda x, y: x * y, v, beta)
    u = _each(_rnn, tinv, rv)
    w = _each(_rnn, tinv, rw)
    ws = _each(_rnn, w, s0)
    vn = _each(lambda x, y: x - y, u, ws)
    qk = _each(_rnt, q, k)
    p = _each(lambda x, y: x * y, qk, dm)
    qg = _each(lambda x, y: x * y, q, eg)
    out = []
    for i in range(len(args)):
        gl = gc[i][CH - 1:CH, :]
        ek = jnp.exp(gl - gc[i])
        out.append(dict(gc=gc[i], dm=dm[i], ds=ds[i], kb=kb[i], a=a[i], tinv=tinv[i], eg=eg[i], rw=rw[i], u=u[i], w=w[i],
                        vn=vn[i], p=p[i], qg=qg[i], egl=jnp.exp(gl), ek=ek, kd=k[i] * ek))
    return out


def _gdn_gates(ba, e, alog, dtb):
    raw = _nn(ba, e)
    beta = _sig(raw[:, :HW])
    za = raw[:, HW:] + dtb
    g = -jnp.exp(alog) * _softplus(za)
    return beta, g, za


def _seqs_per_step(nb):
    return 2 if nb % 2 == 0 else 1


def _gdn_fwd(qkv, proj, e_mat, alog, dtb, lay, nb, nc):
    n = qkv.shape[0]
    tp = n // nb
    cba = lay.c_ba // HD
    gb = _seqs_per_step(nb)

    def body(x_ref, ba_ref, e_ref, al_ref, dt_ref, o_ref, so_ref, s_ref):
        @pl.when(pl.program_id(1) == 0)
        def _():
            s_ref[...] = jnp.zeros_like(s_ref)

        args = []
        for j in range(gb):
            beta, g, _ = _gdn_gates(ba_ref[j], e_ref[...], al_ref[...], dt_ref[...])
            for h in range(NH):
                hs = slice(h * HD, (h + 1) * HD)
                args.append((x_ref[j, :, hs], x_ref[j, :, HW + h * HD:HW + (h + 1) * HD],
                             x_ref[j, :, 2 * HW + h * HD:2 * HW + (h + 1) * HD], beta[:, hs], g[:, hs], s_ref[j, h]))
        cs = _gdn_chunks(args)
        s0s = [a[5] for a in args]
        o1 = _each(lambda c, s0: _rnn(c["qg"], s0), cs, s0s)
        o2 = [_rnn(c["p"], c["vn"]) for c in cs]
        upd = [_rtn(c["kd"], c["vn"]) for c in cs]
        res = [(o1[i] + o2[i], s0s[i] * cs[i]["egl"] + upd[i]) for i in range(len(cs))]
        for j in range(gb):
            for h in range(NH):
                so_ref[j, h] = args[j * NH + h][5]
                s_ref[j, h] = res[j * NH + h][1]
            o_ref[j] = jnp.concatenate([res[j * NH + h][0] for h in range(NH)], axis=-1)

    o, st = pl.pallas_call(
        body, grid=(nb // gb, nc),
        in_specs=[pl.BlockSpec((gb, CH, 3 * HW), lambda b, c: (b, c, 0)), pl.BlockSpec((gb, CH, HD), lambda b, c: (b, c, cba)),
                  pl.BlockSpec((HD, 2 * HW), lambda b, c: (0, 0)), pl.BlockSpec((1, HW), lambda b, c: (0, 0)),
                  pl.BlockSpec((1, HW), lambda b, c: (0, 0))],
        out_specs=[pl.BlockSpec((gb, CH, HW), lambda b, c: (b, c, 0)),
                   pl.BlockSpec((gb, None, NH, HD, HD), lambda b, c: (b, c, 0, 0, 0))],
        out_shape=[SDS((nb, tp, HW), F32), SDS((nb, nc, NH, HD, HD), F32)],
        scratch_shapes=[pltpu.VMEM((gb, NH, HD, HD), F32)], compiler_params=_params(2), name="gdn_fwd")(
            qkv.reshape(nb, tp, 3 * HW), proj.reshape(nb, tp, -1), e_mat, alog, dtb)
    return o.reshape(n, HW), st


def _gdn_bwd(qkv, proj, e_mat, s_mat, alog, dtb, states, do, dproj, lay, nb, nc):
    n = qkv.shape[0]
    tp = n // nb
    cba = lay.c_ba // HD
    gb = _seqs_per_step(nb)

    def body(x_ref, ba_ref, e_ref, sm_ref, al_ref, dt_ref, st_ref, do_ref, dp_in, dx_ref, dba_ref, acc_ref, ds_ref):
        ci = pl.program_id(1)

        @pl.when(ci == 0)
        def _():
            ds_ref[...] = jnp.zeros_like(ds_ref)

        @pl.when((ci == 0) & (pl.program_id(0) == 0))
        def _():
            acc_ref[...] = jnp.zeros_like(acc_ref)

        causal, strict = _gdn_tri()
        alog = al_ref[...]
        row = _iota2((CH, 1), 0)
        valid = (row >= PAD) | (ci < nc - 1)
        last = row == CH - 1
        gates = [_gdn_gates(ba_ref[j], e_ref[...], alog, dt_ref[...]) for j in range(gb)]
        args, do, ds1 = [], [], []
        for j in range(gb):
            beta, g, _ = gates[j]
            for h in range(NH):
                hs = slice(h * HD, (h + 1) * HD)
                args.append((x_ref[j, :, hs], x_ref[j, :, HW + h * HD:HW + (h + 1) * HD],
                             x_ref[j, :, 2 * HW + h * HD:2 * HW + (h + 1) * HD], beta[:, hs], g[:, hs], st_ref[j, h]))
                do.append(do_ref[j, :, hs])
                ds1.append(ds_ref[j, h])
        q, k, v, bh, _, s0 = (list(t) for t in zip(*args))
        cs = _gdn_chunks(args)
        get = lambda name: [c[name] for c in cs]
        mul = lambda x, y: x * y
        add = lambda x, y: x + y
        dvn = _each(add, _each(_rtn, get("p"), do), _each(_rnn, get("kd"), ds1))
        dqg = _each(_rnt, do, s0)
        dp = [jnp.where(causal, x, 0.0) for x in _each(_rnt, do, get("vn"))]
        dkd = _each(_rnt, get("vn"), ds1)
        dw = [-x for x in _each(_rnt, dvn, s0)]
        ds_a = _each(_rtn, get("qg"), do)
        ds_b = _each(_rtn, get("w"), dvn)
        ds_new = [ds_a[i] - ds_b[i] + ds1[i] * cs[i]["egl"] for i in range(len(cs))]
        drv = _each(_rtn, get("tinv"), dvn)
        drw = _each(_rtn, get("tinv"), dw)
        da_1 = _each(_rnt, drv, get("u"))
        da_2 = _each(_rnt, drw, get("w"))
        da = [jnp.where(strict, -(x + y), 0.0) for x, y in zip(da_1, da_2)]
        m = [da[i] * cs[i]["a"] + dp[i] * cs[i]["p"] for i in range(len(cs))]
        dkk = _each(mul, da, get("ds"))
        dqk = _each(mul, dp, get("dm"))
        dq = _each(add, _each(_rnn, dqk, k), _each(mul, dqg, get("eg")))
        dkb = _each(add, _each(_rnn, dkk, k), _each(mul, drw, get("eg")))
        dk_1 = _each(_rtn, dqk, q)
        dk_2 = _each(_rtn, dkk, get("kb"))
        dk = [dk_1[i] + dk_2[i] + dkd[i] * cs[i]["ek"] + dkb[i] * bh[i] for i in range(len(cs))]
        dv = _each(mul, drv, bh)
        dbeta, dg = [], []
        for i, c in enumerate(cs):
            dbeta.append(_rs(drv[i] * v[i]) + _rs(dkb[i] * k[i]) + jnp.zeros((CH, HD), F32))
            t_kd = _rs(dkd[i] * c["kd"])
            dgc = _rs(m[i]) - _rs(m[i].T) + _rs(dqg[i] * c["qg"]) + _rs(drw[i] * c["rw"]) - t_kd
            tail = jnp.sum(t_kd, axis=0, keepdims=True) + c["egl"] * jnp.sum(_rs(s0[i] * ds1[i]), axis=0, keepdims=True)
            dgc = dgc + jnp.where(last, tail, 0.0)
            dg.append(_scan_rows(dgc + jnp.zeros((CH, HD), F32), CH, reverse=True))
        r8 = _iota2((8, HW), 0)
        upd = jnp.zeros((8, HW), F32)
        for j in range(gb):
            sl = slice(j * NH, (j + 1) * NH)
            beta, g, za = gates[j]
            for h in range(NH):
                ds_ref[j, h] = ds_new[j * NH + h]
            dx_ref[j] = jnp.concatenate(dq[sl] + dk[sl] + dv[sl], axis=-1)
            dbeta_j = jnp.where(valid, jnp.concatenate(dbeta[sl], axis=-1), 0.0)
            dg_j = jnp.where(valid, jnp.concatenate(dg[sl], axis=-1), 0.0)
            draw_b = dbeta_j * beta * (1.0 - beta)
            draw_a = dg_j * (-jnp.exp(alog)) * _sig(za)
            dba_ref[j] = _nn(jnp.concatenate([draw_b, draw_a], axis=-1), sm_ref[...])
            upd = upd + jnp.where(r8 == 0, jnp.sum(dg_j * g, axis=0, keepdims=True), 0.0) + jnp.where(
                r8 == 1, jnp.sum(draw_a, axis=0, keepdims=True), 0.0)
        acc_ref[...] += upd

    rc = lambda c: nc - 1 - c
    dqkv, dproj3, acc = pl.pallas_call(
        body, grid=(nb // gb, nc),
        in_specs=[pl.BlockSpec((gb, CH, 3 * HW), lambda b, c: (b, rc(c), 0)), pl.BlockSpec((gb, CH, HD), lambda b, c: (b, rc(c), cba)),
                  pl.BlockSpec((HD, 2 * HW), lambda b, c: (0, 0)), pl.BlockSpec((2 * HW, HD), lambda b, c: (0, 0)),
                  pl.BlockSpec((1, HW), lambda b, c: (0, 0)), pl.BlockSpec((1, HW), lambda b, c: (0, 0)),
                  pl.BlockSpec((gb, None, NH, HD, HD), lambda b, c: (b, rc(c), 0, 0, 0)),
                  pl.BlockSpec((gb, CH, HW), lambda b, c: (b, rc(c), 0)), pl.BlockSpec(memory_space=pl.ANY)],
        out_specs=[pl.BlockSpec((gb, CH, 3 * HW), lambda b, c: (b, rc(c), 0)), pl.BlockSpec((gb, CH, HD), lambda b, c: (b, rc(c), cba)),
                   pl.BlockSpec((8, HW), lambda b, c: (0, 0))],
        out_shape=[SDS((nb, tp, 3 * HW), F32), SDS((nb, tp, dproj.shape[1]), F32), SDS((8, HW), F32)],
        input_output_aliases={8: 1},
        scratch_shapes=[pltpu.VMEM((gb, NH, HD, HD), F32)], compiler_params=_params(2), name="gdn_bwd")(
            qkv.reshape(nb, tp, 3 * HW), proj.reshape(nb, tp, -1), e_mat, s_mat, alog, dtb, states, do.reshape(nb, tp, HW),
            dproj.reshape(nb, tp, -1))
    return dqkv.reshape(n, 3 * HW), dproj3.reshape(dproj.shape), acc


def _hgrn_inputs(zq, zf, lb):
    sg = _sig(zf)
    sgn = _sig(-zf)
    pos = lb > 0.0
    lbp = jnp.where(pos, lb, 0.0)
    fpos = lbp + (1.0 - lbp) * sg
    lf = jnp.where(pos, jnp.log(jnp.where(pos, fpos, 1.0)), _logsig(zf))
    k = (1.0 - lbp) * sgn
    q = _silu(zq) * Q_SCALE
    return q, k, lf, sg, sgn, pos, lbp, fpos


def _hgrn_consts():
    i3, j3 = _iota2((SUB, SUB, HD), 0), _iota2((SUB, SUB, HD), 1)
    return i3 >= j3


def _sum_j(x):
    return jnp.sum(x.reshape(SUB, SUB, HD), axis=1)


def _sum_i(x):
    return jnp.sum(x.reshape(SUB, SUB, HD), axis=0)


def _pairs(a, b):
    return (a[:, None, :] * b[None, :, :]).reshape(SUB * SUB, HD)


def _hgrn_sub(q, k, v, bc, st, consts):
    mask3 = consts
    bl = bc[SUB - 1:SUB, :]
    p3 = jnp.where(mask3, jnp.exp(jnp.where(mask3, bc[:, None, :] - bc[None, :, :], 0.0)), 0.0).reshape(SUB * SUB, HD)
    x = _pairs(q, k) * p3
    srep = _rs(x)
    vt = jnp.broadcast_to(v[None, :, :], (SUB, SUB, HD)).reshape(SUB * SUB, HD)
    eb = jnp.exp(bc)
    qe = q * eb
    o = _hnt(qe, st) + _sum_j(_rr(srep) * _rr(vt))
    ek = jnp.exp(bl - bc)
    kd = k * ek
    ebl = jnp.exp(bl)
    st1 = st * ebl + _htn(v, kd)
    return o, st1, dict(bc=bc, p3=p3, srep=srep, vt=vt, eb=eb, qe=qe, ek=ek, kd=kd, ebl=ebl)


def _hgrn_fwd(proj, lb, lay, nb, nc):
    n = proj.shape[0]
    cbb = lay.c_b // (3 * HW)

    def body(z_ref, lb_ref, o_ref, so_ref, s_ref):
        @pl.when(pl.program_id(1) == 0)
        def _():
            s_ref[...] = jnp.zeros_like(s_ref)

        consts = _hgrn_consts()
        outs = []
        for h in range(NH):
            hs = slice(h * HD, (h + 1) * HD)
            q, k, lf = _hgrn_inputs(z_ref[:, hs], z_ref[:, HW + h * HD:HW + (h + 1) * HD], lb_ref[:, hs])[:3]
            v = z_ref[:, 2 * HW + h * HD:2 * HW + (h + 1) * HD]
            st = s_ref[h]
            so_ref[h] = st
            bc = _scan_rows(lf, SUB)
            oh = []
            for s in range(CH // SUB):
                rs = slice(s * SUB, (s + 1) * SUB)
                o, st, _ = _hgrn_sub(q[rs], k[rs], v[rs], bc[rs], st, consts)
                oh.append(o)
            s_ref[h] = st
            outs.append(jnp.concatenate(oh, axis=0))
        o_ref[...] = jnp.concatenate(outs, axis=-1)

    return pl.pallas_call(
        body, grid=(nb, nc),
        in_specs=[pl.BlockSpec((CH, 3 * HW), lambda b, c: (b * nc + c, cbb)), pl.BlockSpec((1, HW), lambda b, c: (0, 0))],
        out_specs=[pl.BlockSpec((CH, HW), lambda b, c: (b * nc + c, 0)),
                   pl.BlockSpec((None, None, NH, HD, HD), lambda b, c: (b, c, 0, 0, 0))],
        out_shape=[SDS((n, HW), F32), SDS((nb, nc, NH, HD, HD), F32)],
        scratch_shapes=[pltpu.VMEM((NH, HD, HD), F32)], compiler_params=_params(2), name="hgrn_fwd")(proj, lb)


def _hgrn_bwd(proj, lb, states, do, dproj, lay, nb, nc):
    cbb = lay.c_b // (3 * HW)
    nsub = CH // SUB

    def rev(b, c):
        return b * nc + (nc - 1 - c)

    def body(z_ref, lb_ref, st_ref, do_ref, dp_in, dz_ref, acc_ref, ds_ref):
        ci = pl.program_id(1)

        @pl.when(ci == 0)
        def _():
            ds_ref[...] = jnp.zeros_like(ds_ref)

        @pl.when((ci == 0) & (pl.program_id(0) == 0))
        def _():
            acc_ref[...] = jnp.zeros_like(acc_ref)

        consts = _hgrn_consts()
        row = _iota2((CH, 1), 0)
        valid = (row >= PAD) | (ci < nc - 1)
        lastrow = _iota2((SUB, 1), 0) == SUB - 1
        dzq, dzf, dzi, dlbs = [], [], [], []
        for h in range(NH):
            hs = slice(h * HD, (h + 1) * HD)
            zq, zf = z_ref[:, hs], z_ref[:, HW + h * HD:HW + (h + 1) * HD]
            q, k, lf, sg, sgn, pos, lbp, fpos = _hgrn_inputs(zq, zf, lb_ref[:, hs])
            v = z_ref[:, 2 * HW + h * HD:2 * HW + (h + 1) * HD]
            doh = do_ref[:, hs]
            sts, fw = [st_ref[h]], []
            bc = _scan_rows(lf, SUB)
            for s in range(nsub):
                rs = slice(s * SUB, (s + 1) * SUB)
                _, st1, c = _hgrn_sub(q[rs], k[rs], v[rs], bc[rs], sts[-1], consts)
                sts.append(st1)
                fw.append(c)
            dst = ds_ref[h]
            dq_l, dk_l, dv_l, dlf_l = [None] * nsub, [None] * nsub, [None] * nsub, [None] * nsub
            for s in reversed(range(nsub)):
                rs = slice(s * SUB, (s + 1) * SUB)
                c, st = fw[s], sts[s]
                qs, ks, vs, dos = q[rs], k[rs], v[rs], doh[rs]
                dqe = _hnn(dos, st)
                dkd = _hnn(vs, dst)
                dsrep = _rs(_pairs(_rr(dos), _rr(vs)))
                w = dsrep * c["p3"]
                kt = jnp.broadcast_to(ks[None, :, :], (SUB, SUB, HD)).reshape(SUB * SUB, HD)
                qt = jnp.broadcast_to(qs[:, None, :], (SUB, SUB, HD)).reshape(SUB * SUB, HD)
                dq_i = _sum_j(w * kt)
                dk_i = _sum_i(w * qt)
                dot = jnp.broadcast_to(_rr(dos)[:, None, :], (SUB, SUB, HD)).reshape(SUB * SUB, HD)
                dvv = _sum_i(_rr(c["srep"]) * dot) + _hnt(c["kd"], dst)
                t_kd = dkd * c["kd"]
                dbc = dqe * c["qe"] - t_kd + qs * dq_i - ks * dk_i
                tail = jnp.sum(t_kd, axis=0, keepdims=True) + c["ebl"] * jnp.sum(st * dst, axis=0, keepdims=True)
                dbc = dbc + jnp.where(lastrow, tail, 0.0)
                dlf_l[s] = dbc
                dq_l[s] = dq_i + dqe * c["eb"]
                dk_l[s] = dk_i + dkd * c["ek"]
                dv_l[s] = dvv
                dst = _htn(dos, c["qe"]) + dst * c["ebl"]
            ds_ref[h] = dst
            dq, dk, dv, dbc = (jnp.concatenate(t, axis=0) for t in (dq_l, dk_l, dv_l, dlf_l))
            dlf = _scan_rows(dbc, SUB, reverse=True)
            dlft = dlf - dk * (1.0 - k)
            dlf_dz = jnp.where(pos, (1.0 - lbp) * sg * sgn / jnp.where(pos, fpos, 1.0), sgn)
            dlf_dlb = jnp.where(pos, sgn / jnp.where(pos, fpos, 1.0), 0.0)
            dzq.append(dq * Q_SCALE * _dsilu(zq))
            dzf.append(dlft * dlf_dz)
            dzi.append(dv)
            dlbs.append(jnp.sum(jnp.where(valid, dlft * dlf_dlb, 0.0), axis=0, keepdims=True))
        dz_ref[...] = jnp.concatenate(dzq + dzf + dzi, axis=-1)
        acc_ref[...] += jnp.where(_iota2((8, HW), 0) == 0, jnp.concatenate(dlbs, axis=-1), 0.0)

    return pl.pallas_call(
        body, grid=(nb, nc),
        in_specs=[pl.BlockSpec((CH, 3 * HW), lambda b, c: (rev(b, c), cbb)), pl.BlockSpec((1, HW), lambda b, c: (0, 0)),
                  pl.BlockSpec((None, None, NH, HD, HD), lambda b, c: (b, nc - 1 - c, 0, 0, 0)),
                  pl.BlockSpec((CH, HW), lambda b, c: (rev(b, c), 0)), pl.BlockSpec(memory_space=pl.ANY)],
        out_specs=[pl.BlockSpec((CH, 3 * HW), lambda b, c: (rev(b, c), cbb)), pl.BlockSpec((8, HW), lambda b, c: (0, 0))],
        out_shape=[SDS(dproj.shape, F32), SDS((8, HW), F32)],
        input_output_aliases={4: 0},
        scratch_shapes=[pltpu.VMEM((NH, HD, HD), F32)], compiler_params=_params(2), name="hgrn_bwd")(proj, lb, states, do, dproj)


def _gated_norm(o, z, gamma):
    ys, ns, rs = [], [], []
    for h in range(NH):
        hs = slice(h * HD, (h + 1) * HD)
        oh = o[:, hs]
        r = lax.rsqrt(jnp.mean(oh * oh, axis=-1, keepdims=True) + EPS)
        nh = oh * r
        ys.append(nh * gamma * _silu(z[:, hs]))
        ns.append(nh)
        rs.append(r)
    return jnp.concatenate(ys, axis=-1), ns, rs


def _merge_fwd(h, oa, ob, proj, ga, gb, wa, wb, wo, lay):
    n, d = h.shape
    tm = _tile(n, 384)
    wm = lay.wm

    def body(h_ref, oa_ref, ob_ref, p_ref, ga_ref, gb_ref, wa_ref, wb_ref, wo_ref, out_ref):
        ya, _, _ = _gated_norm(oa_ref[...], p_ref[:, 0:HW], ga_ref[...])
        yb, _, _ = _gated_norm(ob_ref[...], p_ref[:, HW:2 * HW], gb_ref[...])
        ya2 = _bnn(ya, wa_ref[...])
        yb2 = _bnn(yb, wb_ref[...])
        mixed = _sig(p_ref[:, 2 * HW:2 * HW + d]) * ya2 + _sig(p_ref[:, 2 * HW + d:2 * HW + 2 * d]) * yb2
        out_ref[...] = h_ref[...] + _bnn(mixed, wo_ref[...])

    full = lambda shape: pl.BlockSpec(shape, lambda i: (0, 0))
    return pl.pallas_call(
        body, grid=(n // tm,),
        in_specs=[pl.BlockSpec((tm, d), lambda i: (i, 0)), pl.BlockSpec((tm, HW), lambda i: (i, 0)),
                  pl.BlockSpec((tm, HW), lambda i: (i, 0)), pl.BlockSpec((tm, wm), lambda i: (i, 0)),
                  full((1, HD)), full((1, HD)), full((HW, d)), full((HW, d)), full((d, d))],
        out_specs=pl.BlockSpec((tm, d), lambda i: (i, 0)), out_shape=SDS((n, d), F32),
        compiler_params=_params(1), name="merge_fwd")(h, oa, ob, proj, ga, gb, wa, wb, wo)


def _gated_norm_bwd(dy, o, z, gamma):
    dos, dzs = [], []
    dgam = jnp.zeros((1, HD), F32)
    for h in range(NH):
        hs = slice(h * HD, (h + 1) * HD)
        oh, zh, dyh = o[:, hs], z[:, hs], dy[:, hs]
        r = lax.rsqrt(jnp.mean(oh * oh, axis=-1, keepdims=True) + EPS)
        nh = oh * r
        dzs.append(dyh * nh * gamma * _dsilu(zh))
        dng = dyh * _silu(zh)
        dgam = dgam + jnp.sum(dng * nh, axis=0, keepdims=True)
        dn = dng * gamma
        dos.append(r * (dn - nh * jnp.mean(dn * nh, axis=-1, keepdims=True)))
    return jnp.concatenate(dos, axis=-1), jnp.concatenate(dzs, axis=-1), dgam


def _merge_bwd(dhn, oa, ob, proj, ga, gb, wa, wb, wo, lay, tp):
    n, d = dhn.shape
    tm = _tile(n, 256)
    wm = lay.wm

    def body(dh_ref, oa_ref, ob_ref, p_ref, ga_ref, gb_ref, wa_ref, wb_ref, wo_ref,
             dp_ref, doa_ref, dob_ref, dwa_ref, dwb_ref, dwo_ref, dga_ref, dgb_ref):
        i = pl.program_id(0)

        @pl.when(i == 0)
        def _():
            for r in (dwa_ref, dwb_ref, dwo_ref, dga_ref, dgb_ref):
                r[...] = jnp.zeros_like(r)

        dh = jnp.where(_row_valid(tm, tp, i * tm), dh_ref[...], 0.0)
        oa, ob = oa_ref[...], ob_ref[...]
        za, zb = p_ref[:, 0:HW], p_ref[:, HW:2 * HW]
        gta, gtb = p_ref[:, 2 * HW:2 * HW + d], p_ref[:, 2 * HW + d:2 * HW + 2 * d]
        ya, _, _ = _gated_norm(oa, za, ga_ref[...])
        yb, _, _ = _gated_norm(ob, zb, gb_ref[...])
        ya2 = _bnn(ya, wa_ref[...])
        yb2 = _bnn(yb, wb_ref[...])
        sa, sb = _sig(gta), _sig(gtb)
        mixed = sa * ya2 + sb * yb2
        dmixed = _bnt(dh, wo_ref[...])
        dwo_ref[...] += _btn(mixed, dh)
        dya2 = dmixed * sa
        dyb2 = dmixed * sb
        dwa_ref[...] += _btn(ya, dya2)
        dwb_ref[...] += _btn(yb, dyb2)
        doa, dza, dga = _gated_norm_bwd(_bnt(dya2, wa_ref[...]), oa, za, ga_ref[...])
        dob, dzb, dgb = _gated_norm_bwd(_bnt(dyb2, wb_ref[...]), ob, zb, gb_ref[...])
        dga_ref[...] += dga
        dgb_ref[...] += dgb
        doa_ref[...] = doa
        dob_ref[...] = dob
        dp_ref[:, 0:HW] = dza
        dp_ref[:, HW:2 * HW] = dzb
        dp_ref[:, 2 * HW:2 * HW + d] = dmixed * ya2 * sa * (1.0 - sa)
        dp_ref[:, 2 * HW + d:2 * HW + 2 * d] = dmixed * yb2 * sb * (1.0 - sb)

    full = lambda shape: pl.BlockSpec(shape, lambda i: (0, 0))
    rows = lambda w: pl.BlockSpec((tm, w), lambda i: (i, 0))
    return pl.pallas_call(
        body, grid=(n // tm,),
        in_specs=[rows(d), rows(HW), rows(HW), rows(wm), full((1, HD)), full((1, HD)), full((HW, d)), full((HW, d)), full((d, d))],
        out_specs=[rows(wm), rows(HW), rows(HW), full((HW, d)), full((HW, d)), full((d, d)), full((1, HD)), full((1, HD))],
        out_shape=[SDS((n, lay.pw), F32), SDS((n, HW), F32), SDS((n, HW), F32), SDS((HW, d), F32), SDS((HW, d), F32),
                   SDS((d, d), F32), SDS((1, HD), F32), SDS((1, HD), F32)],
        compiler_params=_params(1), name="merge_bwd")(dhn, oa, ob, proj, ga, gb, wa, wb, wo)


def _loss_head(h, target, fw, nb, nc):
    n, d = h.shape

    def body(h_ref, t_ref, fw_ref, lp_ref, dh_ref, dfw_ref):
        b, c = pl.program_id(0), pl.program_id(1)

        @pl.when((b == 0) & (c == 0))
        def _():
            dfw_ref[...] = jnp.zeros_like(dfw_ref)

        @pl.when(c == 0)
        def _():
            dh_ref[...] = jnp.zeros_like(dh_ref)
            lp_ref[...] = jnp.zeros_like(lp_ref)

        @pl.when(c > 0)
        def _():
            x = h_ref[...]
            r = lax.rsqrt(jnp.mean(x * x, axis=-1, keepdims=True) + EPS)
            xh = x * r
            err = xh * fw_ref[...] - t_ref[...]
            lp_ref[...] = jnp.zeros_like(lp_ref) + 0.5 * jnp.sum(_rs(err * err), axis=0, keepdims=True) / d
            dy = err / d
            dfw_ref[...] += jnp.sum(dy * xh, axis=0, keepdims=True)
            dxh = dy * fw_ref[...]
            dh_ref[...] = r * (dxh - xh * jnp.mean(dxh * xh, axis=-1, keepdims=True))

    return pl.pallas_call(
        body, grid=(nb, nc),
        in_specs=[pl.BlockSpec((CH, d), lambda b, c: (b * nc + c, 0)),
                  pl.BlockSpec((CH, d), lambda b, c: (b * (nc - 1) + jnp.maximum(c - 1, 0), 0)),
                  pl.BlockSpec((1, d), lambda b, c: (0, 0))],
        out_specs=[pl.BlockSpec((8, HD), lambda b, c: (b * nc + c, 0)), pl.BlockSpec((CH, d), lambda b, c: (b * nc + c, 0)),
                   pl.BlockSpec((1, d), lambda b, c: (0, 0))],
        out_shape=[SDS((nb * nc * 8, HD), F32), SDS((n, d), F32), SDS((1, d), F32)],
        compiler_params=_params(2), name="loss_head")(h, target, fw)


def _lb_fwd(lb):
    def body(x_ref, o_ref):
        x = x_ref[...]
        mx = jnp.max(x, axis=0, keepdims=True)
        e = jnp.exp(x - mx)
        sm = e / jnp.sum(e, axis=0, keepdims=True)
        run = jnp.zeros((1, HW), F32)
        for l in range(DEPTH):
            run = run + sm[l:l + 1, :]
            o_ref[l:l + 1, :] = run - sm[0:1, :]

    return pl.pallas_call(body, out_shape=SDS(lb.shape, F32), name="lb_fwd")(lb)


def _lb_bwd(lb, dlb_all):
    def body(x_ref, d_ref, o_ref):
        x = x_ref[...]
        dl = d_ref[...]
        mx = jnp.max(x, axis=0, keepdims=True)
        e = jnp.exp(x - mx)
        sm = e / jnp.sum(e, axis=0, keepdims=True)
        tot = jnp.sum(dl, axis=0, keepdims=True)
        dsm = []
        run = tot
        for l in range(DEPTH):
            dsm.append(run - (tot if l == 0 else 0.0))
            run = run - dl[l:l + 1, :]
        inner = sum(sm[l:l + 1, :] * dsm[l] for l in range(DEPTH))
        for l in range(DEPTH):
            o_ref[l:l + 1, :] = sm[l:l + 1, :] * (dsm[l] - inner)

    return pl.pallas_call(body, out_shape=SDS(lb.shape, F32), name="lb_bwd")(lb, dlb_all)


def _adamw(g, w, m, v):
    r, c = g.shape
    tr = _tile(r, 264)
    c1 = 1.0 / (1.0 - ADAM_B1 ** ADAM_STEP)
    c2 = 1.0 / (1.0 - ADAM_B2 ** ADAM_STEP)

    def body(g_ref, w_ref, m_ref, v_ref, d_ref, mo_ref, vo_ref):
        gg = g_ref[...]
        mn = ADAM_B1 * m_ref[...] + (1.0 - ADAM_B1) * gg
        vn = ADAM_B2 * v_ref[...] + (1.0 - ADAM_B2) * gg * gg
        d_ref[...] = -ADAM_LR * ((mn * c1) / (jnp.sqrt(vn * c2) + ADAM_EPS) + ADAM_WD * w_ref[...])
        mo_ref[...] = mn
        vo_ref[...] = vn

    spec = pl.BlockSpec((tr, c), lambda i: (i, 0))
    return pl.pallas_call(body, grid=(r // tr,), in_specs=[spec] * 4, out_specs=[spec] * 3, out_shape=[SDS(g.shape, F32)] * 3,
                          compiler_params=_params(1), name="adamw")(g, w, m, v)


def _tile16(n, target):
    return _tile(n // 2, target // 2) * 2 if n % 16 == 0 else _tile(n, target)


def _add_cores(g, got, core):
    k, r, c = got.shape
    tr = _tile16(r, 264)

    def body(c_ref, a_ref, b_ref, o_ref):
        o_ref[...] = (a_ref[...] + b_ref[...].astype(F32)).astype(o_ref.dtype)

    spec = pl.BlockSpec((None, tr, c), lambda s, i, cr: (s, i, 0))
    return pl.pallas_call(
        body, grid_spec=pltpu.PrefetchScalarGridSpec(
            num_scalar_prefetch=1, grid=(k, r // tr),
            in_specs=[pl.BlockSpec((None, None, tr, c), lambda s, i, cr: (cr[0], s, i, 0)), spec], out_specs=spec),
        out_shape=SDS(got.shape, got.dtype), compiler_params=_params(2), name="add_cores")(core, g, got)


def _sum_chips(parts, own, place):
    k, r, c = parts.shape
    tr = _tile16(r, 264)

    def body(p_ref, *refs):
        part_refs, own_ref, o_ref = refs[:k], refs[k], refs[k + 1]
        mine = own_ref[...].astype(F32)
        acc = None
        for s in range(k):
            term = jnp.where(p_ref[0] == s, mine, part_refs[s][...].astype(F32))
            acc = term if acc is None else acc + term
        o_ref[...] = acc

    slots = jnp.stack([jnp.where(place[0] == s, (s + 1) % k, s) for s in range(k)]).astype(jnp.int32)
    other = lambda s: pl.BlockSpec((None, tr, c), lambda i, p: (p[2 + s], i, 0))
    return pl.pallas_call(
        body, grid_spec=pltpu.PrefetchScalarGridSpec(
            num_scalar_prefetch=1, grid=(r // tr,),
            in_specs=[other(s) for s in range(k)] + [pl.BlockSpec((None, tr, c), lambda i, p: (p[0], i, 0))],
            out_specs=pl.BlockSpec((None, tr, c), lambda i, p: (p[1], i, 0))),
        out_shape=SDS((2, r, c), F32), compiler_params=_params(1), name="sum_chips")(
            jnp.concatenate([place, slots]), *([parts] * k), own)


def _meta_grad(dh, nb, nc):
    d = dh.shape[1]

    def body(x_ref, o_ref):
        @pl.when(pl.program_id(0) == 0)
        def _():
            o_ref[...] = jnp.zeros_like(o_ref)

        o_ref[...] += x_ref[PAD:CH, :]

    return pl.pallas_call(body, grid=(nb,), in_specs=[pl.BlockSpec((CH, d), lambda b: (b * nc, 0))],
                          out_specs=pl.BlockSpec((N_META, d), lambda b: (0, 0)), out_shape=SDS((N_META, d), F32),
                          compiler_params=_params(1), name="meta_grad")(dh)


ANY = pl.BlockSpec(memory_space=pl.ANY)


def _place():
    x, y, c = lax.axis_index("x"), lax.axis_index("y"), lax.axis_index("c")
    chips = [(1 - x, y), (x, 1 - y), (1 - x, 1 - y)]
    return x, y, c, chips


def _remote(src, dst, send_sems, recv_sems, k, to):
    return pltpu.make_async_remote_copy(src_ref=src, dst_ref=dst, send_sem=send_sems.at[k], recv_sem=recv_sems.at[k],
                                        device_id=to, device_id_type=MESH)


def _gather_weights(pb, ps):
    def body(pb_ref, ps_ref, gb_ref, gs_ref, send_sems, recv_sems, local_sems):
        x, y, c, chips = _place()
        s = 2 * x + y
        sib = (x, y, 1 - c)
        l1 = pltpu.make_async_copy(ps_ref, gs_ref.at[s], local_sems.at[0])
        l1.start()
        sends = []
        for k, (px, py) in enumerate(chips):
            sends.append(_remote(pb_ref.at[c], gb_ref.at[s, c], send_sems, recv_sems, k, (px, py, c)))
            sends.append(_remote(ps_ref, gs_ref.at[s], send_sems, recv_sems, 6 + k, (px, py, c)))
        for cp in sends:
            cp.start()
        for k, (px, py) in enumerate(chips):
            sk = 2 * px + py
            _remote(pb_ref.at[c], gb_ref.at[sk, c], send_sems, recv_sems, k, sib).wait_recv()
            fwd = _remote(gb_ref.at[sk, c], gb_ref.at[sk, c], send_sems, recv_sems, 3 + k, sib)
            fwd.start()
            sends.append(fwd)
        for k, (px, py) in enumerate(chips):
            sk = 2 * px + py
            _remote(pb_ref.at[c], gb_ref.at[sk, 1 - c], send_sems, recv_sems, 3 + k, sib).wait_recv()
            _remote(ps_ref, gs_ref.at[sk], send_sems, recv_sems, 6 + k, sib).wait_recv()
        for cp in sends:
            cp.wait_send()
        l1.wait()

    return pl.pallas_call(
        body, in_specs=[ANY, ANY], out_specs=[ANY, ANY],
        out_shape=[SDS((4,) + pb.shape, pb.dtype), SDS((4,) + ps.shape, ps.dtype)],
        scratch_shapes=[pltpu.SemaphoreType.DMA((9,)), pltpu.SemaphoreType.DMA((9,)), pltpu.SemaphoreType.DMA((1,))],
        name="gather_weights")(pb, ps)


def _place_own(gb, pb, chip):
    _, _, r, c = gb.shape
    tr = _tile16(r, 1100)

    def body(s_ref, p_ref, g_in, o_ref):
        o_ref[...] = p_ref[...]

    return pl.pallas_call(
        body, grid_spec=pltpu.PrefetchScalarGridSpec(
            num_scalar_prefetch=1, grid=(2, r // tr),
            in_specs=[pl.BlockSpec((None, tr, c), lambda h, i, s: (h, i, 0)), ANY],
            out_specs=pl.BlockSpec((None, None, tr, c), lambda h, i, s: (s[0], h, i, 0))),
        out_shape=SDS(gb.shape, gb.dtype), input_output_aliases={2: 0}, compiler_params=_params(2),
        name="place_own")(chip, pb, gb)


def _sem_scratch(n_remote, n_local):
    return [pltpu.SemaphoreType.DMA((n_remote,)), pltpu.SemaphoreType.DMA((n_remote,)), pltpu.SemaphoreType.DMA((n_local,))]


def _swap_halves(sends):
    nt = len(sends)

    def body(*refs):
        s_refs, got_refs = refs[:nt], refs[nt:2 * nt]
        send_sems, recv_sems = refs[2 * nt:]
        x, y, c, _ = _place()
        sib = (x, y, 1 - c)
        remote = [_remote(s_refs[t].at[1 - c, s], got_refs[t].at[s], send_sems, recv_sems, 4 * t + s, sib)
                  for t in range(nt) for s in range(4)]
        for cp in remote:
            cp.start()
        for cp in remote:
            cp.wait()

    return pl.pallas_call(
        body, in_specs=[ANY] * nt, out_specs=[ANY] * nt, out_shape=[SDS(g.shape[1:], g.dtype) for g in sends],
        scratch_shapes=[pltpu.SemaphoreType.DMA((4 * nt,)), pltpu.SemaphoreType.DMA((4 * nt,))], name="swap_halves")(*sends)


def _scatter_chip_sums(parts):
    nt = len(parts)

    def body(*refs):
        a_refs, r_refs = refs[:nt], refs[nt:2 * nt]
        send_sems, recv_sems = refs[2 * nt:]
        x, y, c, chips = _place()
        s = 2 * x + y
        sends = [_remote(a_refs[t].at[2 * px + py], r_refs[t].at[s], send_sems, recv_sems, 3 * t + k, (px, py, c))
                 for t in range(nt) for k, (px, py) in enumerate(chips)]
        for cp in sends:
            cp.start()
        for t in range(nt):
            for k, (px, py) in enumerate(chips):
                _remote(a_refs[t].at[s], r_refs[t].at[2 * px + py], send_sems, recv_sems, 3 * t + k, (px, py, c)).wait_recv()
        for cp in sends:
            cp.wait_send()

    return pl.pallas_call(
        body, in_specs=[ANY] * nt, out_specs=[ANY] * nt, out_shape=[SDS(a.shape, a.dtype) for a in parts],
        scratch_shapes=[pltpu.SemaphoreType.DMA((3 * nt,)), pltpu.SemaphoreType.DMA((3 * nt,))],
        name="scatter_chip_sums")(*parts)


def _join_halves(fs):
    nt = len(fs)

    def body(*refs):
        f_refs = refs[nt:2 * nt]
        send_sems, recv_sems = refs[2 * nt:]
        x, y, c, _ = _place()
        sib = (x, y, 1 - c)
        sends = [_remote(f_refs[t].at[c], f_refs[t].at[c], send_sems, recv_sems, t, sib) for t in range(nt)]
        for cp in sends:
            cp.start()
        for t in range(nt):
            _remote(f_refs[t].at[c], f_refs[t].at[1 - c], send_sems, recv_sems, t, sib).wait_recv()
        for cp in sends:
            cp.wait_send()

    return pl.pallas_call(
        body, in_specs=[ANY] * nt, out_specs=[ANY] * nt, out_shape=[SDS(f.shape, f.dtype) for f in fs],
        input_output_aliases={t: t for t in range(nt)},
        scratch_shapes=[pltpu.SemaphoreType.DMA((nt,)), pltpu.SemaphoreType.DMA((nt,))], name="join_halves")(*fs)


def _uncontain(cont, n_head, width):
    r, cw = cont.shape
    tr = _tile(r, 256)

    def body(n_ref, x_ref, o_ref):
        o_ref[...] = pltpu.roll(x_ref[...], n_ref[0], axis=1)[:, :width]

    return pl.pallas_call(
        body, grid_spec=pltpu.PrefetchScalarGridSpec(
            num_scalar_prefetch=1, grid=(r // tr,), in_specs=[pl.BlockSpec((tr, cw), lambda i, n: (i, 0))],
            out_specs=pl.BlockSpec((tr, width), lambda i, n: (i, 0))),
        out_shape=SDS((r, width), F32), compiler_params=_params(1), name="uncontain")(n_head, cont)


WEIGHTS = ("meta_tokens", "norm_w", "w_in", "conv_w", "a_log", "dt_bias", "gnorm_a", "gnorm_b", "hgrn_lower_bounds",
           "w_branch_a", "w_branch_b", "w_out", "final_norm_w")
SHARD_AXIS = {"meta_tokens": 1, "w_in": 2, "conv_w": 2, "w_branch_a": 2, "w_branch_b": 2, "w_out": 1}
FLAT_C = 1024


def _flat(parts, rows, cols=FLAT_C):
    v = jnp.concatenate([p.reshape(-1) for p in parts])
    return jnp.pad(v, (0, rows * cols - v.shape[0])).reshape(rows, cols)


def _local_step(x, target, w, lay):
    nb, seq, d = x.shape
    tp = CH + seq
    nc = tp // CH
    n = nb * tp
    e_mat, s_mat = _gate_consts()
    lb_all = _lb_fwd(w["hgrn_lower_bounds"])
    h = jnp.concatenate([jnp.zeros((nb, PAD, d), F32), jnp.broadcast_to(w["meta_tokens"][None], (nb, N_META, d)), x],
                        axis=1).reshape(n, d)
    rep = lambda a: jnp.repeat(a, HD)[None, :]
    saved = []
    for l in range(DEPTH):
        nw = w["norm_w"][l][None, :]
        proj = _norm_proj_fwd(h, nw, w["w_in"][l])
        qkv = _gdn_prep_fwd(proj, w["conv_w"][l], lay, nb, tp)
        alog, dtb = rep(w["a_log"][l]), rep(w["dt_bias"][l])
        oa, sa = _gdn_fwd(qkv, proj, e_mat, alog, dtb, lay, nb, nc)
        lbl = lb_all[l][None, :]
        ob, sb = _hgrn_fwd(proj, lbl, lay, nb, nc)
        ga, gb = w["gnorm_a"][l][None, :], w["gnorm_b"][l][None, :]
        hn = _merge_fwd(h, oa, ob, proj, ga, gb, w["w_branch_a"][l], w["w_branch_b"][l], w["w_out"][l], lay)
        saved.append((h, nw, proj, qkv, alog, dtb, oa, sa, lbl, ob, sb, ga, gb))
        h = hn
    lp, dh, dfw = _loss_head(h, target.reshape(nb * seq, d), w["final_norm_w"][None, :], nb, nc)
    loss = jnp.sum(lp[::8, 0])
    g = {n_: [None] * DEPTH for n_ in WEIGHTS}
    dlb_all = [None] * DEPTH
    for l in reversed(range(DEPTH)):
        h, nw, proj, qkv, alog, dtb, oa, sa, lbl, ob, sb, ga, gb = saved[l]
        dproj, doa, dob, dwa, dwb, dwo, dga, dgb = _merge_bwd(dh, oa, ob, proj, ga, gb, w["w_branch_a"][l],
                                                             w["w_branch_b"][l], w["w_out"][l], lay, tp)
        dproj, acc_b = _hgrn_bwd(proj, lbl, sb, dob, dproj, lay, nb, nc)
        dqkv, dproj, acc_a = _gdn_bwd(qkv, proj, e_mat, s_mat, alog, dtb, sa, doa, dproj, lay, nb, nc)
        dproj, dconv = _gdn_prep_bwd(proj, w["conv_w"][l], dqkv, dproj, lay, nb, tp)
        dh, dnw = _proj_bwd_dx(dproj, w["w_in"][l], h, nw, dh, tp)
        g["w_in"][l] = _proj_bwd_dw(dproj, h, nw, tp)
        g["norm_w"][l] = dnw[0]
        g["conv_w"][l] = dconv
        g["a_log"][l] = acc_a[0, ::HD]
        g["dt_bias"][l] = acc_a[1, ::HD]
        g["gnorm_a"][l], g["gnorm_b"][l] = dga[0], dgb[0]
        g["w_branch_a"][l], g["w_branch_b"][l], g["w_out"][l] = dwa, dwb, dwo
        dlb_all[l] = acc_b[0]
    grads = {n_: jnp.stack(v) for n_, v in g.items() if v[0] is not None}
    grads["hgrn_lower_bounds"] = _lb_bwd(w["hgrn_lower_bounds"], jnp.stack(dlb_all))
    grads["final_norm_w"] = dfw[0]
    grads["meta_tokens"] = _meta_grad(dh, nb, nc)
    grad_x = dh.reshape(nb, tp, d)[:, CH:, :]
    return loss, grad_x, grads


def kernel(x, meta_tokens, norm_w, w_in, conv_w, a_log, dt_bias, gnorm_a, gnorm_b, hgrn_lower_bounds, w_branch_a, w_branch_b, w_out, final_norm_w, loss_target, m_meta_tokens, m_norm_w, m_w_in, m_conv_w, m_a_log, m_dt_bias, m_gnorm_a, m_gnorm_b, m_hgrn_lower_bounds, m_w_branch_a, m_w_branch_b, m_w_out, m_final_norm_w, v_meta_tokens, v_norm_w, v_w_in, v_conv_w, v_a_log, v_dt_bias, v_gnorm_a, v_gnorm_b, v_hgrn_lower_bounds, v_w_branch_a, v_w_branch_b, v_w_out, v_final_norm_w):
    wl = dict(meta_tokens=meta_tokens, norm_w=norm_w, w_in=w_in, conv_w=conv_w, a_log=a_log, dt_bias=dt_bias, gnorm_a=gnorm_a,
              gnorm_b=gnorm_b, hgrn_lower_bounds=hgrn_lower_bounds, w_branch_a=w_branch_a, w_branch_b=w_branch_b, w_out=w_out,
              final_norm_w=final_norm_w)
    ml = dict(zip(WEIGHTS, (m_meta_tokens, m_norm_w, m_w_in, m_conv_w, m_a_log, m_dt_bias, m_gnorm_a, m_gnorm_b,
                            m_hgrn_lower_bounds, m_w_branch_a, m_w_branch_b, m_w_out, m_final_norm_w)))
    vl = dict(zip(WEIGHTS, (v_meta_tokens, v_norm_w, v_w_in, v_conv_w, v_a_log, v_dt_bias, v_gnorm_a, v_gnorm_b,
                            v_hgrn_lower_bounds, v_w_branch_a, v_w_branch_b, v_w_out, v_final_norm_w)))
    d = x.shape[2]
    lay = _Layout(d)
    nchip = 4

    big = ("w_in", "w_branch_a", "w_branch_b", "w_out")
    small = ("conv_w", "meta_tokens")
    nbig = sum(int(np.prod(wl[n].shape)) for n in big)
    rb = -(-nbig // (2 * FLAT_C * 16)) * 16
    pb = _flat([wl[n].astype(BF16) for n in big], 2 * rb).reshape(2, rb, FLAT_C)
    nsmall = sum(int(np.prod(wl[n].shape)) for n in small)
    rs = -(-nsmall // (HD * 8)) * 8
    ps = jnp.pad(jnp.concatenate([wl[n].reshape(-1) for n in small]), (0, rs * HD - nsmall)).reshape(rs, HD)
    gbig, gsmall = _gather_weights(pb, ps)
    chip_id = (2 * lax.axis_index("x") + lax.axis_index("y")).astype(jnp.int32).reshape(1)
    gbig = _place_own(gbig, pb, chip_id).reshape(nchip, -1)
    gsmall = gsmall.reshape(nchip, -1)

    def whole(flat, names, src):
        out, o = {}, 0
        for n in names:
            shp = src[n].shape
            sz = int(np.prod(shp))
            a = flat[:, o:o + sz].reshape((nchip,) + shp)
            ax = SHARD_AXIS[n]
            out[n] = jnp.concatenate([a[s] for s in range(nchip)], axis=ax)
            o += sz
        return out

    wf = dict(wl)
    wf.update(whole(gbig, big, wl))
    wf.update(whole(gsmall, small, wl))
    wf["w_in"] = lay.to_kernel(wf["w_in"])

    loss_part, grad_x, gfull = _local_step(x, loss_target, wf, lay)
    loss = lax.psum(loss_part, ("x", "y", "c"))

    sw = wl["w_in"].shape[2]
    conts, heads = lay.containers(gfull["w_in"], nchip)
    dd = wl["w_branch_a"].shape[2]
    rows_o = wl["w_out"].shape[1]
    by_dest = lambda g, n: [lax.slice_in_dim(g, s * wl[n].shape[SHARD_AXIS[n]], (s + 1) * wl[n].shape[SHARD_AXIS[n]],
                                            axis=SHARD_AXIS[n]) if n in SHARD_AXIS else g for s in range(nchip)]
    small_names = tuple(n for n in WEIGHTS if n not in big)
    nsm = sum(int(np.prod(wl[n].shape)) for n in small_names)
    rsm = -(-nsm // (2 * HD * 8)) * 8
    pack_small = lambda parts: _flat(parts, 2 * rsm, HD).reshape(2, rsm, HD)
    small_by_dest = [by_dest(gfull[n], n) for n in small_names]
    gs = [jnp.stack(conts, axis=1),
          jnp.stack(by_dest(gfull["w_branch_a"], "w_branch_a"), axis=1),
          jnp.stack(by_dest(gfull["w_branch_b"], "w_branch_b"), axis=1),
          gfull["w_out"].reshape(DEPTH, nchip, rows_o, d),
          jnp.stack([pack_small([p[s] for p in small_by_dest]) for s in range(nchip)], axis=1)]
    gs = [g.reshape((2, nchip, -1, g.shape[-1])) for g in gs]
    my_chip = (2 * lax.axis_index("x") + lax.axis_index("y")).astype(jnp.int32)
    my_core = lax.axis_index("c").astype(jnp.int32)
    got = _swap_halves([g.astype(BF16) for g in gs[:4]] + gs[4:])
    chip_sums = [_add_cores(g, b, my_core.reshape(1)) for g, b in zip(gs, got)]
    by_chip = _scatter_chip_sums(chip_sums)
    place = jnp.stack([my_chip, my_core])
    full = _join_halves([_sum_chips(p, a, place) for p, a in zip(by_chip, chip_sums)])
    n_head = sum(jnp.where(my_chip == s, heads[s], 0) for s in range(nchip)).astype(jnp.int32).reshape(1)
    g_w_in = _uncontain(full[0].reshape(DEPTH * d, -1), n_head, sw)
    g2 = {"w_in": g_w_in, "w_branch_a": full[1].reshape(-1, dd), "w_branch_b": full[2].reshape(-1, dd),
          "w_out": full[3].reshape(-1, d), "small": full[4].reshape(2 * rsm, HD)}

    def two_d(src, n):
        if n == "small":
            return _flat([src[k] for k in small_names], 2 * rsm, HD)
        return src[n].reshape(g2[n].shape)

    outs = {}
    for n in big + ("small",):
        delta, mnew, vnew = _adamw(g2[n], two_d(wl, n), two_d(ml, n), two_d(vl, n))
        outs[n] = (g2[n], delta, mnew, vnew)
    res = [{}, {}, {}, {}]
    for i in range(4):
        for n in big:
            res[i][n] = outs[n][i].reshape(wl[n].shape)
        v, o = outs["small"][i].reshape(-1), 0
        for n in small_names:
            sz = int(np.prod(wl[n].shape))
            res[i][n] = v[o:o + sz].reshape(wl[n].shape)
            o += sz
    return (loss, grad_x, *[res[0][n] for n in WEIGHTS], *[res[1][n] for n in WEIGHTS], *[res[2][n] for n in WEIGHTS],
            *[res[3][n] for n in WEIGHTS])
```

```python
import functools

import numpy as np
import jax
import jax.numpy as jnp
from jax import lax
from jax.experimental import pallas as pl
from jax.experimental.pallas import tpu as pltpu

F32 = jnp.float32
BF16 = jnp.bfloat16
HI = lax.Precision.HIGHEST
SDS = jax.ShapeDtypeStruct

NH = 4
HD = 128
HW = NH * HD
N_META = 16
CH = 64
SUB = 16
PAD = CH - N_META
EPS = 1e-6
Q_SCALE = HD ** -0.5
DEPTH = 2
CONV_K = 4
VMEM_LIMIT = 56 * 1024 * 1024
ADAM_LR, ADAM_B1, ADAM_B2, ADAM_EPS, ADAM_WD, ADAM_STEP = 0.001, 0.9, 0.999, 1e-08, 0.01, 10
MESH = pl.DeviceIdType.MESH


def _nn(a, b):
    return jnp.dot(a, b, precision=HI, preferred_element_type=F32)


def _nt(a, b):
    return lax.dot_general(a, b, (((1,), (1,)), ((), ())), precision=HI, preferred_element_type=F32)


def _tn(a, b):
    return _nn(a.T, b)


def _scan_rows(x, group, reverse=False):
    n = x.shape[0]
    pos = lax.bitwise_and(_iota2(x.shape, 0), group - 1)
    s = 1
    while s < group:
        if reverse:
            x = x + jnp.where(pos < group - s, pltpu.roll(x, n - s, axis=0), 0.0)
        else:
            x = x + jnp.where(pos >= s, pltpu.roll(x, s, axis=0), 0.0)
        s *= 2
    return x


def _bnn(a, b):
    return jnp.dot(a.astype(BF16), b.astype(BF16), preferred_element_type=F32)


def _bnt(a, b):
    return lax.dot_general(a.astype(BF16), b.astype(BF16), (((1,), (1,)), ((), ())), preferred_element_type=F32)


def _btn(a, b):
    return lax.dot_general(a.astype(BF16), b.astype(BF16), (((0,), (0,)), ((), ())), preferred_element_type=F32)


def _hi_lo(x):
    hi = x.astype(jnp.bfloat16)
    return hi, (x - hi.astype(F32)).astype(jnp.bfloat16)


def _dot3(dims):
    def f(a, b):
        ah, al = _hi_lo(a)
        bh, bl = _hi_lo(b)
        d = lambda p, q: lax.dot_general(p, q, (dims, ((), ())), preferred_element_type=F32)
        return d(ah, bh) + (d(ah, bl) + d(al, bh))
    return f


_rnn, _rnt, _rtn = _dot3(((1,), (0,))), _dot3(((1,), (1,))), _dot3(((0,), (0,)))
_hnn, _hnt, _htn = _bnn, _bnt, _btn


def _rr(x):
    return x


def _sig(x):
    return jax.nn.sigmoid(x)


def _silu(x):
    return x * _sig(x)


def _dsilu(x):
    s = _sig(x)
    return s * (1.0 + x * (1.0 - s))


def _softplus(x):
    return jnp.maximum(x, 0.0) + jnp.log(1.0 + jnp.exp(-jnp.abs(x)))


def _logsig(x):
    return jnp.minimum(x, 0.0) - jnp.log(1.0 + jnp.exp(-jnp.abs(x)))


def _rs(x):
    return jnp.sum(x, axis=-1, keepdims=True)


def _params(n_axes):
    return pltpu.CompilerParams(dimension_semantics=("arbitrary",) * n_axes, vmem_limit_bytes=VMEM_LIMIT)


def _tile(n, target):
    best = 8
    for t in range(8, target + 1, 8):
        if n % t == 0:
            best = t
    return best


def _ctile(pw, most=7):
    return HD * max(k for k in range(1, most + 1) if (pw // HD) % k == 0)


def _iota2(shape, axis):
    return lax.broadcasted_iota(jnp.int32, shape, axis)


class _Layout:
    def __init__(self, d):
        self.d = d
        self.wm = 2 * HW + 2 * d
        self.c_qkv = self.wm
        self.c_b = self.wm + 3 * HW
        self.c_ba = self.wm + 6 * HW
        self.pw = self.c_ba + HD
        assert self.c_b % (3 * HW) == 0
        o = 0
        segs = {}
        for name, w in (("a_q", HW), ("a_k", HW), ("a_v", HW), ("ba", 2 * NH), ("a_z", HW), ("b_q", HW), ("b_f", HW),
                        ("b_i", HW), ("b_g", HW), ("gate_a", d), ("gate_b", d)):
            segs[name] = (o, o + w)
            o += w
        self.segs = segs
        self.width = o
        self.order = ("a_z", "b_g", "gate_a", "gate_b", "a_q", "a_k", "a_v", "b_q", "b_f", "b_i", "ba")

    def to_kernel(self, w):
        parts = [w[..., self.segs[n][0]:self.segs[n][1]] for n in self.order]
        parts.append(jnp.zeros(w.shape[:-1] + (HD - 2 * NH,), w.dtype))
        return jnp.concatenate(parts, axis=-1)

    def containers(self, g, nchip):
        table, heads, cw = self.pieces(nchip)
        out = []
        for s in range(nchip):
            parts, at = [], 0
            for kcol, w, ccol in sorted(table[s], key=lambda p: p[2]):
                if ccol > at:
                    parts.append(jnp.zeros(g.shape[:-1] + (ccol - at,), g.dtype))
                parts.append(g[..., kcol:kcol + w])
                at = ccol + w
            if at < cw:
                parts.append(jnp.zeros(g.shape[:-1] + (cw - at,), g.dtype))
            out.append(jnp.concatenate(parts, axis=-1))
        return out, heads

    def pieces(self, nchip):
        off, where = 0, {}
        for n in self.order:
            where[n] = off
            off += self.segs[n][1] - self.segs[n][0]
        names = sorted(self.segs, key=lambda n: self.segs[n][0])
        sw = self.width // nchip
        cw = -(-sw // HD) * HD
        table, heads = [], []
        for s in range(nchip):
            lo, hi = s * sw, (s + 1) * sw
            pieces = []
            for n in names:
                a, b = max(lo, self.segs[n][0]), min(hi, self.segs[n][1])
                if a < b:
                    pieces.append((where[n] + a - self.segs[n][0], b - a))
            start, width = pieces[0]
            n_head = min((-start) % HD, width)
            body = ([(start + n_head, width - n_head)] if width > n_head else []) + pieces[1:]
            rows, at = [], 0
            for c, w in body:
                rows.append((c, w, at))
                at += w
            if n_head:
                rows.append((start, n_head, cw - n_head))
            table.append(rows)
            heads.append(n_head)
        return table, heads, cw

    def from_containers(self, conts):
        table, _, _ = self.pieces(len(conts))
        cut = sorted((kcol, w, s, ccol) for s, rows in enumerate(table) for kcol, w, ccol in rows)
        parts, at = [], 0
        for kcol, w, s, ccol in cut:
            assert kcol == at, (kcol, at)
            parts.append(conts[s][..., ccol:ccol + w])
            at = kcol + w
        parts.append(jnp.zeros(conts[0].shape[:-1] + (self.pw - at,), conts[0].dtype))
        return jnp.concatenate(parts, axis=-1)

    def from_kernel(self, g):
        off, where = 0, {}
        for n in self.order:
            w = self.segs[n][1] - self.segs[n][0]
            where[n] = (off, off + w)
            off += w
        names = sorted(self.segs, key=lambda n: self.segs[n][0])
        return jnp.concatenate([g[..., where[n][0]:where[n][1]] for n in names], axis=-1)


def _norm_proj_fwd(h, nw, wp):
    n, d = h.shape
    pw = wp.shape[1]
    tm, tn = _tile16(n, 768), _ctile(pw)

    def body(h_ref, nw_ref, w_ref, o_ref, xn_ref):
        @pl.when(pl.program_id(1) == 0)
        def _():
            x = h_ref[...]
            r = lax.rsqrt(jnp.mean(x * x, axis=-1, keepdims=True) + EPS)
            xn_ref[...] = (x * r * nw_ref[...]).astype(BF16)

        o_ref[...] = jnp.dot(xn_ref[...], w_ref[...], preferred_element_type=F32)

    return pl.pallas_call(
        body, grid=(n // tm, pw // tn),
        in_specs=[pl.BlockSpec((tm, d), lambda i, j: (i, 0)), pl.BlockSpec((1, d), lambda i, j: (0, 0)),
                  pl.BlockSpec((d, tn), lambda i, j: (0, j))],
        out_specs=[pl.BlockSpec((tm, tn), lambda i, j: (i, j)), pl.BlockSpec((tm, d), lambda i, j: (i, 0))],
        out_shape=[SDS((n, pw), F32), SDS((n, d), BF16)], compiler_params=_params(2), name="norm_proj_fwd")(h, nw, wp)


def _row_valid(tm, tp, base):
    row = base + _iota2((tm, 1), 0)
    return lax.rem(row, tp) >= PAD


def _proj_bwd_dx(dproj, wp, h, nw, dhn, tp):
    n, d = h.shape
    pw = wp.shape[1]
    tm, tk = _tile16(n, 768), _ctile(pw)
    nk = pw // tk

    def body(dp_ref, w_ref, h_ref, nw_ref, dhn_ref, dh_ref, dnw_ref, acc_ref):
        i, k = pl.program_id(0), pl.program_id(1)

        @pl.when(k == 0)
        def _():
            acc_ref[...] = jnp.zeros_like(acc_ref)

        @pl.when((i == 0) & (k == 0))
        def _():
            dnw_ref[...] = jnp.zeros_like(dnw_ref)

        valid = _row_valid(tm, tp, i * tm)
        dp = jnp.where(valid, dp_ref[...], 0.0)
        acc_ref[...] += _bnt(dp, w_ref[...])

        @pl.when(k == nk - 1)
        def _():
            x = h_ref[...]
            r = lax.rsqrt(jnp.mean(x * x, axis=-1, keepdims=True) + EPS)
            xh = x * r
            dxn = acc_ref[...]
            dnw_ref[...] += jnp.sum(dxn * xh, axis=0, keepdims=True)
            dxh = dxn * nw_ref[...]
            dh_ref[...] = dhn_ref[...] + r * (dxh - xh * jnp.mean(dxh * xh, axis=-1, keepdims=True))

    return pl.pallas_call(
        body, grid=(n // tm, nk),
        in_specs=[pl.BlockSpec((tm, tk), lambda i, k: (i, k)), pl.BlockSpec((d, tk), lambda i, k: (0, k)),
                  pl.BlockSpec((tm, d), lambda i, k: (i, 0)), pl.BlockSpec((1, d), lambda i, k: (0, 0)),
                  pl.BlockSpec((tm, d), lambda i, k: (i, 0))],
        out_specs=[pl.BlockSpec((tm, d), lambda i, k: (i, 0)), pl.BlockSpec((1, d), lambda i, k: (0, 0))],
        out_shape=[SDS((n, d), F32), SDS((1, d), F32)],
        scratch_shapes=[pltpu.VMEM((tm, d), F32)], compiler_params=_params(2), name="proj_bwd_dx")(dproj, wp, h, nw, dhn)


def _proj_bwd_dw(dproj, xn, tp):
    n, d = xn.shape
    pw = dproj.shape[1]
    tm, tn = _tile16(n, 768), _ctile(pw)

    def body(dp_ref, xn_ref, dw_ref):
        i = pl.program_id(1)

        @pl.when(i == 0)
        def _():
            dw_ref[...] = jnp.zeros_like(dw_ref)

        dp = jnp.where(_row_valid(tm, tp, i * tm), dp_ref[...], 0.0)
        dw_ref[...] += _btn(xn_ref[...], dp)

    return pl.pallas_call(
        body, grid=(pw // tn, n // tm),
        in_specs=[pl.BlockSpec((tm, tn), lambda j, i: (i, j)), pl.BlockSpec((tm, d), lambda j, i: (i, 0))],
        out_specs=pl.BlockSpec((d, tn), lambda j, i: (0, j)), out_shape=SDS((d, pw), F32),
        compiler_params=_params(2), name="proj_bwd_dw")(dproj, xn)


def _conv_silu(x, w, row):
    c = x * w[CONV_K - 1:CONV_K, :]
    for k in range(1, CONV_K):
        c = c + jnp.where(row >= k, pltpu.roll(x, k, axis=0), 0.0) * w[CONV_K - 1 - k:CONV_K - k, :]
    return c


def _gdn_prep_fwd(proj, conv_w, lay, nb, tp):
    n = proj.shape[0]
    nblk = 3 * NH
    cb = lay.c_qkv // HD

    def body(p_ref, w_ref, o_ref):
        j = pl.program_id(1)
        x = p_ref[...]
        row = _iota2(x.shape, 0)
        c = _conv_silu(x, w_ref[...], row)
        s = _silu(c)
        r = lax.rsqrt(_rs(s * s) + EPS)
        scale = jnp.where(j < NH, Q_SCALE, 1.0)
        y = jnp.where(j < 2 * NH, s * r * scale, s)
        o_ref[...] = jnp.where(row >= PAD, y, 0.0)

    return pl.pallas_call(
        body, grid=(nb, nblk),
        in_specs=[pl.BlockSpec((tp, HD), lambda b, j: (b, cb + j)), pl.BlockSpec((CONV_K, HD), lambda b, j: (0, j))],
        out_specs=pl.BlockSpec((tp, HD), lambda b, j: (b, j)), out_shape=SDS((n, nblk * HD), F32),
        compiler_params=_params(2), name="gdn_prep_fwd")(proj, conv_w)


def _gdn_prep_bwd(proj, conv_w, dqkv, dproj, lay, nb, tp):
    nblk = 3 * NH
    cb = lay.c_qkv // HD

    def body(p_ref, w_ref, dy_ref, dp_in, dp_ref, dw_ref):
        j, b = pl.program_id(0), pl.program_id(1)
        x = p_ref[...]
        w = w_ref[...]
        row = _iota2(x.shape, 0)
        c = _conv_silu(x, w, row)
        s = _silu(c)
        dy = jnp.where(row >= PAD, dy_ref[...], 0.0)
        r = lax.rsqrt(_rs(s * s) + EPS)
        nh = s * r
        scale = jnp.where(j < NH, Q_SCALE, 1.0)
        ds_n = scale * r * (dy - nh * _rs(dy * nh))
        ds = jnp.where(j < 2 * NH, ds_n, dy)
        dc = ds * _dsilu(c)
        dx = dc * w[CONV_K - 1:CONV_K, :]
        dws = [jnp.sum(dc * x, axis=0, keepdims=True)]
        for k in range(1, CONV_K):
            dx = dx + jnp.where(row < tp - k, pltpu.roll(dc, tp - k, axis=0), 0.0) * w[CONV_K - 1 - k:CONV_K - k, :]
            xs = jnp.where(row >= k, pltpu.roll(x, k, axis=0), 0.0)
            dws.append(jnp.sum(dc * xs, axis=0, keepdims=True))
        dp_ref[...] = dx.astype(dp_ref.dtype)
        r4 = _iota2((CONV_K, HD), 0)
        dw = jnp.zeros((CONV_K, HD), F32)
        for k in range(CONV_K):
            dw = dw + jnp.where(r4 == CONV_K - 1 - k, dws[k], 0.0)

        @pl.when(b == 0)
        def _():
            dw_ref[...] = dw

        @pl.when(b > 0)
        def _():
            dw_ref[...] += dw

    return pl.pallas_call(
        body, grid=(nblk, nb),
        in_specs=[pl.BlockSpec((tp, HD), lambda j, b: (b, cb + j)), pl.BlockSpec((CONV_K, HD), lambda j, b: (0, j)),
                  pl.BlockSpec((tp, HD), lambda j, b: (b, j)), pl.BlockSpec(memory_space=pl.ANY)],
        out_specs=[pl.BlockSpec((tp, HD), lambda j, b: (b, cb + j)), pl.BlockSpec((CONV_K, HD), lambda j, b: (0, j))],
        out_shape=[SDS(dproj.shape, dproj.dtype), SDS((CONV_K, nblk * HD), F32)],
        input_output_aliases={3: 0}, compiler_params=_params(2), name="gdn_prep_bwd")(proj, conv_w, dqkv, dproj)


def _gate_consts():
    e = np.zeros((HD, 2 * HW), np.float32)
    s = np.zeros((2 * HW, HD), np.float32)
    for h in range(NH):
        e[h, h * HD:(h + 1) * HD] = 1.0
        e[NH + h, HW + h * HD:HW + (h + 1) * HD] = 1.0
        s[h * HD, h] = 1.0
        s[HW + h * HD, NH + h] = 1.0
    return jnp.asarray(e), jnp.asarray(s)


def _gdn_tri():
    i, j = _iota2((CH, CH), 0), _iota2((CH, CH), 1)
    return i >= j, i > j


def _each(fn, *lists):
    return [fn(*xs) for xs in zip(*lists)]


def _tri_inv(a_list, eye):
    p = [-a for a in a_list]
    t = [eye + x for x in p]
    for _ in range(5):
        p = _each(_rnn, p, p)
        tp_ = _each(_rnn, t, p)
        t = _each(lambda x, y: x + y, t, tp_)
    return t


def _gdn_chunks(args):
    causal, strict = _gdn_tri()
    eye = jnp.where(_iota2((CH, CH), 0) == _iota2((CH, CH), 1), 1.0, 0.0)
    q, k, v, beta, g, s0 = (list(t) for t in zip(*args))
    gc = [_scan_rows(x, CH) for x in g]
    dm = [jnp.where(causal, jnp.exp(jnp.where(causal, x[:, :CH] - x[:, :CH].T, 0.0)), 0.0) for x in gc]
    ds = [jnp.where(strict, x, 0.0) for x in dm]
    kb = _each(lambda x, y: x * y, k, beta)
    kk = _each(_rnt, kb, k)
    a = _each(lambda x, y: x * y, kk, ds)
    tinv = _tri_inv(a, eye)
    eg = [jnp.exp(x) for x in gc]
    rw = _each(lambda x, y: x * y, kb, eg)
    rv = _each(lambda x, y: x * y, v, beta)
    u = _each(_rnn, tinv, rv)
    w = _each(_rnn, tinv, rw)
    ws = _each(_rnn, w, s0)
    vn = _each(lambda x, y: x - y, u, ws)
    qk = _each(_rnt, q, k)
    p = _each(lambda x, y: x * y, qk, dm)
    qg = _each(lambda x, y: x * y, q, eg)
    out = []
    for i in range(len(args)):
        gl = gc[i][CH - 1:CH, :]
        ek = jnp.exp(gl - gc[i])
        out.append(dict(gc=gc[i], dm=dm[i], ds=ds[i], kb=kb[i], a=a[i], tinv=tinv[i], eg=eg[i], rw=rw[i], u=u[i], w=w[i],
                        vn=vn[i], p=p[i], qg=qg[i], egl=jnp.exp(gl), ek=ek, kd=k[i] * ek))
    return out


def _gdn_gates(ba, e, alog, dtb):
    raw = _nn(ba, e)
    beta = _sig(raw[:, :HW])
    za = raw[:, HW:] + dtb
    g = -jnp.exp(alog) * _softplus(za)
    return beta, g, za


def _seqs_per_step(nb):
    return 2 if nb % 2 == 0 else 1


def _gdn_fwd(qkv, proj, e_mat, alog, dtb, lay, nb, nc):
    n = qkv.shape[0]
    tp = n // nb
    cba = lay.c_ba // HD
    gb = _seqs_per_step(nb)

    def body(x_ref, ba_ref, e_ref, al_ref, dt_ref, o_ref, so_ref, s_ref):
        @pl.when(pl.program_id(1) == 0)
        def _():
            s_ref[...] = jnp.zeros_like(s_ref)

        args = []
        for j in range(gb):
            beta, g, _ = _gdn_gates(ba_ref[j], e_ref[...], al_ref[...], dt_ref[...])
            for h in range(NH):
                hs = slice(h * HD, (h + 1) * HD)
                args.append((x_ref[j, :, hs], x_ref[j, :, HW + h * HD:HW + (h + 1) * HD],
                             x_ref[j, :, 2 * HW + h * HD:2 * HW + (h + 1) * HD], beta[:, hs], g[:, hs], s_ref[j, h]))
        cs = _gdn_chunks(args)
        s0s = [a[5] for a in args]
        o1 = _each(lambda c, s0: _rnn(c["qg"], s0), cs, s0s)
        o2 = [_rnn(c["p"], c["vn"]) for c in cs]
        upd = [_rtn(c["kd"], c["vn"]) for c in cs]
        res = [(o1[i] + o2[i], s0s[i] * cs[i]["egl"] + upd[i]) for i in range(len(cs))]
        for j in range(gb):
            for h in range(NH):
                so_ref[j, h] = args[j * NH + h][5]
                s_ref[j, h] = res[j * NH + h][1]
            o_ref[j] = jnp.concatenate([res[j * NH + h][0] for h in range(NH)], axis=-1)

    o, st = pl.pallas_call(
        body, grid=(nb // gb, nc),
        in_specs=[pl.BlockSpec((gb, CH, 3 * HW), lambda b, c: (b, c, 0)), pl.BlockSpec((gb, CH, HD), lambda b, c: (b, c, cba)),
                  pl.BlockSpec((HD, 2 * HW), lambda b, c: (0, 0)), pl.BlockSpec((1, HW), lambda b, c: (0, 0)),
                  pl.BlockSpec((1, HW), lambda b, c: (0, 0))],
        out_specs=[pl.BlockSpec((gb, CH, HW), lambda b, c: (b, c, 0)),
                   pl.BlockSpec((gb, None, NH, HD, HD), lambda b, c: (b, c, 0, 0, 0))],
        out_shape=[SDS((nb, tp, HW), F32), SDS((nb, nc, NH, HD, HD), F32)],
        scratch_shapes=[pltpu.VMEM((gb, NH, HD, HD), F32)], compiler_params=_params(2), name="gdn_fwd")(
            qkv.reshape(nb, tp, 3 * HW), proj.reshape(nb, tp, -1), e_mat, alog, dtb)
    return o.reshape(n, HW), st


def _gdn_bwd(qkv, proj, e_mat, s_mat, alog, dtb, states, do, dproj, lay, nb, nc):
    n = qkv.shape[0]
    tp = n // nb
    cba = lay.c_ba // HD
    gb = _seqs_per_step(nb)

    def body(x_ref, ba_ref, e_ref, sm_ref, al_ref, dt_ref, st_ref, do_ref, dp_in, dx_ref, dba_ref, acc_ref, ds_ref):
        ci = pl.program_id(1)

        @pl.when(ci == 0)
        def _():
            ds_ref[...] = jnp.zeros_like(ds_ref)

        @pl.when((ci == 0) & (pl.program_id(0) == 0))
        def _():
            acc_ref[...] = jnp.zeros_like(acc_ref)

        causal, strict = _gdn_tri()
        alog = al_ref[...]
        row = _iota2((CH, 1), 0)
        valid = (row >= PAD) | (ci < nc - 1)
        last = row == CH - 1
        gates = [_gdn_gates(ba_ref[j], e_ref[...], alog, dt_ref[...]) for j in range(gb)]
        args, do, ds1 = [], [], []
        for j in range(gb):
            beta, g, _ = gates[j]
            for h in range(NH):
                hs = slice(h * HD, (h + 1) * HD)
                args.append((x_ref[j, :, hs], x_ref[j, :, HW + h * HD:HW + (h + 1) * HD],
                             x_ref[j, :, 2 * HW + h * HD:2 * HW + (h + 1) * HD], beta[:, hs], g[:, hs], st_ref[j, h]))
                do.append(do_ref[j, :, hs])
                ds1.append(ds_ref[j, h])
        q, k, v, bh, _, s0 = (list(t) for t in zip(*args))
        cs = _gdn_chunks(args)
        get = lambda name: [c[name] for c in cs]
        mul = lambda x, y: x * y
        add = lambda x, y: x + y
        dvn = _each(add, _each(_rtn, get("p"), do), _each(_rnn, get("kd"), ds1))
        dqg = _each(_rnt, do, s0)
        dp = [jnp.where(causal, x, 0.0) for x in _each(_rnt, do, get("vn"))]
        dkd = _each(_rnt, get("vn"), ds1)
        dw = [-x for x in _each(_rnt, dvn, s0)]
        ds_a = _each(_rtn, get("qg"), do)
        ds_b = _each(_rtn, get("w"), dvn)
        ds_new = [ds_a[i] - ds_b[i] + ds1[i] * cs[i]["egl"] for i in range(len(cs))]
        drv = _each(_rtn, get("tinv"), dvn)
        drw = _each(_rtn, get("tinv"), dw)
        da_1 = _each(_rnt, drv, get("u"))
        da_2 = _each(_rnt, drw, get("w"))
        da = [jnp.where(strict, -(x + y), 0.0) for x, y in zip(da_1, da_2)]
        m = [da[i] * cs[i]["a"] + dp[i] * cs[i]["p"] for i in range(len(cs))]
        dkk = _each(mul, da, get("ds"))
        dqk = _each(mul, dp, get("dm"))
        dq = _each(add, _each(_rnn, dqk, k), _each(mul, dqg, get("eg")))
        dkb = _each(add, _each(_rnn, dkk, k), _each(mul, drw, get("eg")))
        dk_1 = _each(_rtn, dqk, q)
        dk_2 = _each(_rtn, dkk, get("kb"))
        dk = [dk_1[i] + dk_2[i] + dkd[i] * cs[i]["ek"] + dkb[i] * bh[i] for i in range(len(cs))]
        dv = _each(mul, drv, bh)
        dbeta, dg = [], []
        for i, c in enumerate(cs):
            dbeta.append(_rs(drv[i] * v[i]) + _rs(dkb[i] * k[i]) + jnp.zeros((CH, HD), F32))
            t_kd = _rs(dkd[i] * c["kd"])
            dgc = _rs(m[i]) - _rs(m[i].T) + _rs(dqg[i] * c["qg"]) + _rs(drw[i] * c["rw"]) - t_kd
            tail = jnp.sum(t_kd, axis=0, keepdims=True) + c["egl"] * jnp.sum(_rs(s0[i] * ds1[i]), axis=0, keepdims=True)
            dgc = dgc + jnp.where(last, tail, 0.0)
            dg.append(_scan_rows(dgc + jnp.zeros((CH, HD), F32), CH, reverse=True))
        r8 = _iota2((8, HW), 0)
        upd = jnp.zeros((8, HW), F32)
        for j in range(gb):
            sl = slice(j * NH, (j + 1) * NH)
            beta, g, za = gates[j]
            for h in range(NH):
                ds_ref[j, h] = ds_new[j * NH + h]
            dx_ref[j] = jnp.concatenate(dq[sl] + dk[sl] + dv[sl], axis=-1)
            dbeta_j = jnp.where(valid, jnp.concatenate(dbeta[sl], axis=-1), 0.0)
            dg_j = jnp.where(valid, jnp.concatenate(dg[sl], axis=-1), 0.0)
            draw_b = dbeta_j * beta * (1.0 - beta)
            draw_a = dg_j * (-jnp.exp(alog)) * _sig(za)
            dba_ref[j] = _nn(jnp.concatenate([draw_b, draw_a], axis=-1), sm_ref[...]).astype(dba_ref.dtype)
            upd = upd + jnp.where(r8 == 0, jnp.sum(dg_j * g, axis=0, keepdims=True), 0.0) + jnp.where(
                r8 == 1, jnp.sum(draw_a, axis=0, keepdims=True), 0.0)
        acc_ref[...] += upd

    rc = lambda c: nc - 1 - c
    dqkv, dproj3, acc = pl.pallas_call(
        body, grid=(nb // gb, nc),
        in_specs=[pl.BlockSpec((gb, CH, 3 * HW), lambda b, c: (b, rc(c), 0)), pl.BlockSpec((gb, CH, HD), lambda b, c: (b, rc(c), cba)),
                  pl.BlockSpec((HD, 2 * HW), lambda b, c: (0, 0)), pl.BlockSpec((2 * HW, HD), lambda b, c: (0, 0)),
                  pl.BlockSpec((1, HW), lambda b, c: (0, 0)), pl.BlockSpec((1, HW), lambda b, c: (0, 0)),
                  pl.BlockSpec((gb, None, NH, HD, HD), lambda b, c: (b, rc(c), 0, 0, 0)),
                  pl.BlockSpec((gb, CH, HW), lambda b, c: (b, rc(c), 0)), pl.BlockSpec(memory_space=pl.ANY)],
        out_specs=[pl.BlockSpec((gb, CH, 3 * HW), lambda b, c: (b, rc(c), 0)), pl.BlockSpec((gb, CH, HD), lambda b, c: (b, rc(c), cba)),
                   pl.BlockSpec((8, HW), lambda b, c: (0, 0))],
        out_shape=[SDS((nb, tp, 3 * HW), F32), SDS((nb, tp, dproj.shape[1]), dproj.dtype), SDS((8, HW), F32)],
        input_output_aliases={8: 1},
        scratch_shapes=[pltpu.VMEM((gb, NH, HD, HD), F32)], compiler_params=_params(2), name="gdn_bwd")(
            qkv.reshape(nb, tp, 3 * HW), proj.reshape(nb, tp, -1), e_mat, s_mat, alog, dtb, states, do.reshape(nb, tp, HW),
            dproj.reshape(nb, tp, -1))
    return dqkv.reshape(n, 3 * HW), dproj3.reshape(dproj.shape), acc


def _hgrn_inputs(zq, zf, lb):
    sg = _sig(zf)
    sgn = _sig(-zf)
    pos = lb > 0.0
    lbp = jnp.where(pos, lb, 0.0)
    fpos = lbp + (1.0 - lbp) * sg
    lf = jnp.where(pos, jnp.log(jnp.where(pos, fpos, 1.0)), _logsig(zf))
    k = (1.0 - lbp) * sgn
    q = _silu(zq) * Q_SCALE
    return q, k, lf, sg, sgn, pos, lbp, fpos


def _hgrn_consts():
    i3, j3 = _iota2((SUB, SUB, HD), 0), _iota2((SUB, SUB, HD), 1)
    return i3 >= j3


def _sum_j(x):
    return jnp.sum(x.reshape(SUB, SUB, HD), axis=1)


def _sum_i(x):
    return jnp.sum(x.reshape(SUB, SUB, HD), axis=0)


def _pairs(a, b):
    return (a[:, None, :] * b[None, :, :]).reshape(SUB * SUB, HD)


def _hgrn_sub(q, k, v, bc, st, consts):
    mask3 = consts
    bl = bc[SUB - 1:SUB, :]
    p3 = jnp.where(mask3, jnp.exp(jnp.where(mask3, bc[:, None, :] - bc[None, :, :], 0.0)), 0.0).reshape(SUB * SUB, HD)
    x = _pairs(q, k) * p3
    srep = _rs(x)
    vt = jnp.broadcast_to(v[None, :, :], (SUB, SUB, HD)).reshape(SUB * SUB, HD)
    eb = jnp.exp(bc)
    qe = q * eb
    o = _hnt(qe, st) + _sum_j(_rr(srep) * _rr(vt))
    ek = jnp.exp(bl - bc)
    kd = k * ek
    ebl = jnp.exp(bl)
    st1 = st * ebl + _htn(v, kd)
    return o, st1, dict(bc=bc, p3=p3, srep=srep, vt=vt, eb=eb, qe=qe, ek=ek, kd=kd, ebl=ebl)


def _hgrn_fwd(proj, lb, lay, nb, nc):
    n = proj.shape[0]
    cbb = lay.c_b // (3 * HW)

    def body(z_ref, lb_ref, o_ref, so_ref, s_ref):
        @pl.when(pl.program_id(1) == 0)
        def _():
            s_ref[...] = jnp.zeros_like(s_ref)

        consts = _hgrn_consts()
        outs = []
        for h in range(NH):
            hs = slice(h * HD, (h + 1) * HD)
            q, k, lf = _hgrn_inputs(z_ref[:, hs], z_ref[:, HW + h * HD:HW + (h + 1) * HD], lb_ref[:, hs])[:3]
            v = z_ref[:, 2 * HW + h * HD:2 * HW + (h + 1) * HD]
            st = s_ref[h]
            so_ref[h] = st
            bc = _scan_rows(lf, SUB)
            oh = []
            for s in range(CH // SUB):
                rs = slice(s * SUB, (s + 1) * SUB)
                o, st, _ = _hgrn_sub(q[rs], k[rs], v[rs], bc[rs], st, consts)
                oh.append(o)
            s_ref[h] = st
            outs.append(jnp.concatenate(oh, axis=0))
        o_ref[...] = jnp.concatenate(outs, axis=-1)

    return pl.pallas_call(
        body, grid=(nb, nc),
        in_specs=[pl.BlockSpec((CH, 3 * HW), lambda b, c: (b * nc + c, cbb)), pl.BlockSpec((1, HW), lambda b, c: (0, 0))],
        out_specs=[pl.BlockSpec((CH, HW), lambda b, c: (b * nc + c, 0)),
                   pl.BlockSpec((None, None, NH, HD, HD), lambda b, c: (b, c, 0, 0, 0))],
        out_shape=[SDS((n, HW), F32), SDS((nb, nc, NH, HD, HD), F32)],
        scratch_shapes=[pltpu.VMEM((NH, HD, HD), F32)], compiler_params=_params(2), name="hgrn_fwd")(proj, lb)


def _hgrn_bwd(proj, lb, states, do, dproj, lay, nb, nc):
    cbb = lay.c_b // (3 * HW)
    nsub = CH // SUB

    def rev(b, c):
        return b * nc + (nc - 1 - c)

    def body(z_ref, lb_ref, st_ref, do_ref, dp_in, dz_ref, acc_ref, ds_ref):
        ci = pl.program_id(1)

        @pl.when(ci == 0)
        def _():
            ds_ref[...] = jnp.zeros_like(ds_ref)

        @pl.when((ci == 0) & (pl.program_id(0) == 0))
        def _():
            acc_ref[...] = jnp.zeros_like(acc_ref)

        consts = _hgrn_consts()
        row = _iota2((CH, 1), 0)
        valid = (row >= PAD) | (ci < nc - 1)
        lastrow = _iota2((SUB, 1), 0) == SUB - 1
        dzq, dzf, dzi, dlbs = [], [], [], []
        for h in range(NH):
            hs = slice(h * HD, (h + 1) * HD)
            zq, zf = z_ref[:, hs], z_ref[:, HW + h * HD:HW + (h + 1) * HD]
            q, k, lf, sg, sgn, pos, lbp, fpos = _hgrn_inputs(zq, zf, lb_ref[:, hs])
            v = z_ref[:, 2 * HW + h * HD:2 * HW + (h + 1) * HD]
            doh = do_ref[:, hs]
            sts, fw = [st_ref[h]], []
            bc = _scan_rows(lf, SUB)
            for s in range(nsub):
                rs = slice(s * SUB, (s + 1) * SUB)
                _, st1, c = _hgrn_sub(q[rs], k[rs], v[rs], bc[rs], sts[-1], consts)
                sts.append(st1)
                fw.append(c)
            dst = ds_ref[h]
            dq_l, dk_l, dv_l, dlf_l = [None] * nsub, [None] * nsub, [None] * nsub, [None] * nsub
            for s in reversed(range(nsub)):
                rs = slice(s * SUB, (s + 1) * SUB)
                c, st = fw[s], sts[s]
                qs, ks, vs, dos = q[rs], k[rs], v[rs], doh[rs]
                dqe = _hnn(dos, st)
                dkd = _hnn(vs, dst)
                dsrep = _rs(_pairs(_rr(dos), _rr(vs)))
                w = dsrep * c["p3"]
                kt = jnp.broadcast_to(ks[None, :, :], (SUB, SUB, HD)).reshape(SUB * SUB, HD)
                qt = jnp.broadcast_to(qs[:, None, :], (SUB, SUB, HD)).reshape(SUB * SUB, HD)
                dq_i = _sum_j(w * kt)
                dk_i = _sum_i(w * qt)
                dot = jnp.broadcast_to(_rr(dos)[:, None, :], (SUB, SUB, HD)).reshape(SUB * SUB, HD)
                dvv = _sum_i(_rr(c["srep"]) * dot) + _hnt(c["kd"], dst)
                t_kd = dkd * c["kd"]
                dbc = dqe * c["qe"] - t_kd + qs * dq_i - ks * dk_i
                tail = jnp.sum(t_kd, axis=0, keepdims=True) + c["ebl"] * jnp.sum(st * dst, axis=0, keepdims=True)
                dbc = dbc + jnp.where(lastrow, tail, 0.0)
                dlf_l[s] = dbc
                dq_l[s] = dq_i + dqe * c["eb"]
                dk_l[s] = dk_i + dkd * c["ek"]
                dv_l[s] = dvv
                dst = _htn(dos, c["qe"]) + dst * c["ebl"]
            ds_ref[h] = dst
            dq, dk, dv, dbc = (jnp.concatenate(t, axis=0) for t in (dq_l, dk_l, dv_l, dlf_l))
            dlf = _scan_rows(dbc, SUB, reverse=True)
            dlft = dlf - dk * (1.0 - k)
            dlf_dz = jnp.where(pos, (1.0 - lbp) * sg * sgn / jnp.where(pos, fpos, 1.0), sgn)
            dlf_dlb = jnp.where(pos, sgn / jnp.where(pos, fpos, 1.0), 0.0)
            dzq.append(dq * Q_SCALE * _dsilu(zq))
            dzf.append(dlft * dlf_dz)
            dzi.append(dv)
            dlbs.append(jnp.sum(jnp.where(valid, dlft * dlf_dlb, 0.0), axis=0, keepdims=True))
        dz_ref[...] = jnp.concatenate(dzq + dzf + dzi, axis=-1).astype(dz_ref.dtype)
        acc_ref[...] += jnp.where(_iota2((8, HW), 0) == 0, jnp.concatenate(dlbs, axis=-1), 0.0)

    return pl.pallas_call(
        body, grid=(nb, nc),
        in_specs=[pl.BlockSpec((CH, 3 * HW), lambda b, c: (rev(b, c), cbb)), pl.BlockSpec((1, HW), lambda b, c: (0, 0)),
                  pl.BlockSpec((None, None, NH, HD, HD), lambda b, c: (b, nc - 1 - c, 0, 0, 0)),
                  pl.BlockSpec((CH, HW), lambda b, c: (rev(b, c), 0)), pl.BlockSpec(memory_space=pl.ANY)],
        out_specs=[pl.BlockSpec((CH, 3 * HW), lambda b, c: (rev(b, c), cbb)), pl.BlockSpec((8, HW), lambda b, c: (0, 0))],
        out_shape=[SDS(dproj.shape, dproj.dtype), SDS((8, HW), F32)],
        input_output_aliases={4: 0},
        scratch_shapes=[pltpu.VMEM((NH, HD, HD), F32)], compiler_params=_params(2), name="hgrn_bwd")(proj, lb, states, do, dproj)


def _gated_norm(o, z, gamma):
    ys, ns, rs = [], [], []
    for h in range(NH):
        hs = slice(h * HD, (h + 1) * HD)
        oh = o[:, hs]
        r = lax.rsqrt(jnp.mean(oh * oh, axis=-1, keepdims=True) + EPS)
        nh = oh * r
        ys.append(nh * gamma * _silu(z[:, hs]))
        ns.append(nh)
        rs.append(r)
    return jnp.concatenate(ys, axis=-1), ns, rs


def _merge_fwd(h, oa, ob, proj, ga, gb, wa, wb, wo, lay):
    n, d = h.shape
    tm = _tile(n, 384)
    wm = lay.wm

    def body(h_ref, oa_ref, ob_ref, p_ref, ga_ref, gb_ref, wa_ref, wb_ref, wo_ref, out_ref):
        ya, _, _ = _gated_norm(oa_ref[...], p_ref[:, 0:HW], ga_ref[...])
        yb, _, _ = _gated_norm(ob_ref[...], p_ref[:, HW:2 * HW], gb_ref[...])
        ya2 = _bnn(ya, wa_ref[...])
        yb2 = _bnn(yb, wb_ref[...])
        mixed = _sig(p_ref[:, 2 * HW:2 * HW + d]) * ya2 + _sig(p_ref[:, 2 * HW + d:2 * HW + 2 * d]) * yb2
        out_ref[...] = h_ref[...] + _bnn(mixed, wo_ref[...])

    full = lambda shape: pl.BlockSpec(shape, lambda i: (0, 0))
    return pl.pallas_call(
        body, grid=(n // tm,),
        in_specs=[pl.BlockSpec((tm, d), lambda i: (i, 0)), pl.BlockSpec((tm, HW), lambda i: (i, 0)),
                  pl.BlockSpec((tm, HW), lambda i: (i, 0)), pl.BlockSpec((tm, wm), lambda i: (i, 0)),
                  full((1, HD)), full((1, HD)), full((HW, d)), full((HW, d)), full((d, d))],
        out_specs=pl.BlockSpec((tm, d), lambda i: (i, 0)), out_shape=SDS((n, d), F32),
        compiler_params=_params(1), name="merge_fwd")(h, oa, ob, proj, ga, gb, wa, wb, wo)


def _gated_norm_bwd(dy, o, z, gamma):
    dos, dzs = [], []
    dgam = jnp.zeros((1, HD), F32)
    for h in range(NH):
        hs = slice(h * HD, (h + 1) * HD)
        oh, zh, dyh = o[:, hs], z[:, hs], dy[:, hs]
        r = lax.rsqrt(jnp.mean(oh * oh, axis=-1, keepdims=True) + EPS)
        nh = oh * r
        dzs.append(dyh * nh * gamma * _dsilu(zh))
        dng = dyh * _silu(zh)
        dgam = dgam + jnp.sum(dng * nh, axis=0, keepdims=True)
        dn = dng * gamma
        dos.append(r * (dn - nh * jnp.mean(dn * nh, axis=-1, keepdims=True)))
    return jnp.concatenate(dos, axis=-1), jnp.concatenate(dzs, axis=-1), dgam


def _merge_bwd(dhn, oa, ob, proj, ga, gb, wa, wb, wo, lay, tp):
    n, d = dhn.shape
    tm = _tile(n, 256)
    wm = lay.wm

    def body(dh_ref, oa_ref, ob_ref, p_ref, ga_ref, gb_ref, wa_ref, wb_ref, wo_ref,
             dp_ref, doa_ref, dob_ref, dwa_ref, dwb_ref, dwo_ref, dga_ref, dgb_ref):
        i = pl.program_id(0)

        @pl.when(i == 0)
        def _():
            for r in (dwa_ref, dwb_ref, dwo_ref, dga_ref, dgb_ref):
                r[...] = jnp.zeros_like(r)

        dh = jnp.where(_row_valid(tm, tp, i * tm), dh_ref[...], 0.0)
        oa, ob = oa_ref[...], ob_ref[...]
        za, zb = p_ref[:, 0:HW], p_ref[:, HW:2 * HW]
        gta, gtb = p_ref[:, 2 * HW:2 * HW + d], p_ref[:, 2 * HW + d:2 * HW + 2 * d]
        ya, _, _ = _gated_norm(oa, za, ga_ref[...])
        yb, _, _ = _gated_norm(ob, zb, gb_ref[...])
        ya2 = _bnn(ya, wa_ref[...])
        yb2 = _bnn(yb, wb_ref[...])
        sa, sb = _sig(gta), _sig(gtb)
        mixed = sa * ya2 + sb * yb2
        dmixed = _bnt(dh, wo_ref[...])
        dwo_ref[...] += _btn(mixed, dh)
        dya2 = dmixed * sa
        dyb2 = dmixed * sb
        dwa_ref[...] += _btn(ya, dya2)
        dwb_ref[...] += _btn(yb, dyb2)
        doa, dza, dga = _gated_norm_bwd(_bnt(dya2, wa_ref[...]), oa, za, ga_ref[...])
        dob, dzb, dgb = _gated_norm_bwd(_bnt(dyb2, wb_ref[...]), ob, zb, gb_ref[...])
        dga_ref[...] += dga
        dgb_ref[...] += dgb
        doa_ref[...] = doa
        dob_ref[...] = dob
        dt = dp_ref.dtype
        dp_ref[:, 0:HW] = dza.astype(dt)
        dp_ref[:, HW:2 * HW] = dzb.astype(dt)
        dp_ref[:, 2 * HW:2 * HW + d] = (dmixed * ya2 * sa * (1.0 - sa)).astype(dt)
        dp_ref[:, 2 * HW + d:2 * HW + 2 * d] = (dmixed * yb2 * sb * (1.0 - sb)).astype(dt)

    full = lambda shape: pl.BlockSpec(shape, lambda i: (0, 0))
    rows = lambda w: pl.BlockSpec((tm, w), lambda i: (i, 0))
    return pl.pallas_call(
        body, grid=(n // tm,),
        in_specs=[rows(d), rows(HW), rows(HW), rows(wm), full((1, HD)), full((1, HD)), full((HW, d)), full((HW, d)), full((d, d))],
        out_specs=[rows(wm), rows(HW), rows(HW), full((HW, d)), full((HW, d)), full((d, d)), full((1, HD)), full((1, HD))],
        out_shape=[SDS((n, lay.pw), BF16), SDS((n, HW), F32), SDS((n, HW), F32), SDS((HW, d), F32), SDS((HW, d), F32),
                   SDS((d, d), F32), SDS((1, HD), F32), SDS((1, HD), F32)],
        compiler_params=_params(1), name="merge_bwd")(dhn, oa, ob, proj, ga, gb, wa, wb, wo)


def _loss_head(h, target, fw, nb, nc):
    n, d = h.shape

    def body(h_ref, t_ref, fw_ref, lp_ref, dh_ref, dfw_ref):
        b, c = pl.program_id(0), pl.program_id(1)

        @pl.when((b == 0) & (c == 0))
        def _():
            dfw_ref[...] = jnp.zeros_like(dfw_ref)

        @pl.when(c == 0)
        def _():
            dh_ref[...] = jnp.zeros_like(dh_ref)
            lp_ref[...] = jnp.zeros_like(lp_ref)

        @pl.when(c > 0)
        def _():
            x = h_ref[...]
            r = lax.rsqrt(jnp.mean(x * x, axis=-1, keepdims=True) + EPS)
            xh = x * r
            err = xh * fw_ref[...] - t_ref[...]
            lp_ref[...] = jnp.zeros_like(lp_ref) + 0.5 * jnp.sum(_rs(err * err), axis=0, keepdims=True) / d
            dy = err / d
            dfw_ref[...] += jnp.sum(dy * xh, axis=0, keepdims=True)
            dxh = dy * fw_ref[...]
            dh_ref[...] = r * (dxh - xh * jnp.mean(dxh * xh, axis=-1, keepdims=True))

    return pl.pallas_call(
        body, grid=(nb, nc),
        in_specs=[pl.BlockSpec((CH, d), lambda b, c: (b * nc + c, 0)),
                  pl.BlockSpec((CH, d), lambda b, c: (b * (nc - 1) + jnp.maximum(c - 1, 0), 0)),
                  pl.BlockSpec((1, d), lambda b, c: (0, 0))],
        out_specs=[pl.BlockSpec((8, HD), lambda b, c: (b * nc + c, 0)), pl.BlockSpec((CH, d), lambda b, c: (b * nc + c, 0)),
                   pl.BlockSpec((1, d), lambda b, c: (0, 0))],
        out_shape=[SDS((nb * nc * 8, HD), F32), SDS((n, d), F32), SDS((1, d), F32)],
        compiler_params=_params(2), name="loss_head")(h, target, fw)


def _lb_fwd(lb):
    def body(x_ref, o_ref):
        x = x_ref[...]
        mx = jnp.max(x, axis=0, keepdims=True)
        e = jnp.exp(x - mx)
        sm = e / jnp.sum(e, axis=0, keepdims=True)
        run = jnp.zeros((1, HW), F32)
        for l in range(DEPTH):
            run = run + sm[l:l + 1, :]
            o_ref[l:l + 1, :] = run - sm[0:1, :]

    return pl.pallas_call(body, out_shape=SDS(lb.shape, F32), name="lb_fwd")(lb)


def _lb_bwd(lb, dlb_all):
    def body(x_ref, d_ref, o_ref):
        x = x_ref[...]
        dl = d_ref[...]
        mx = jnp.max(x, axis=0, keepdims=True)
        e = jnp.exp(x - mx)
        sm = e / jnp.sum(e, axis=0, keepdims=True)
        tot = jnp.sum(dl, axis=0, keepdims=True)
        dsm = []
        run = tot
        for l in range(DEPTH):
            dsm.append(run - (tot if l == 0 else 0.0))
            run = run - dl[l:l + 1, :]
        inner = sum(sm[l:l + 1, :] * dsm[l] for l in range(DEPTH))
        for l in range(DEPTH):
            o_ref[l:l + 1, :] = sm[l:l + 1, :] * (dsm[l] - inner)

    return pl.pallas_call(body, out_shape=SDS(lb.shape, F32), name="lb_bwd")(lb, dlb_all)


def _adamw(g, w, m, v):
    r, c = g.shape
    tr = _tile(r, 264)
    c1 = 1.0 / (1.0 - ADAM_B1 ** ADAM_STEP)
    c2 = 1.0 / (1.0 - ADAM_B2 ** ADAM_STEP)

    def body(g_ref, w_ref, m_ref, v_ref, d_ref, mo_ref, vo_ref):
        gg = g_ref[...]
        mn = ADAM_B1 * m_ref[...] + (1.0 - ADAM_B1) * gg
        vn = ADAM_B2 * v_ref[...] + (1.0 - ADAM_B2) * gg * gg
        d_ref[...] = -ADAM_LR * ((mn * c1) / (jnp.sqrt(vn * c2) + ADAM_EPS) + ADAM_WD * w_ref[...])
        mo_ref[...] = mn
        vo_ref[...] = vn

    spec = pl.BlockSpec((tr, c), lambda i: (i, 0))
    return pl.pallas_call(body, grid=(r // tr,), in_specs=[spec] * 4, out_specs=[spec] * 3, out_shape=[SDS(g.shape, F32)] * 3,
                          compiler_params=_params(1), name="adamw")(g, w, m, v)


def _tile16(n, target):
    return _tile(n // 2, target // 2) * 2 if n % 16 == 0 else _tile(n, target)


def _add_cores(g, got, core):
    k, r, c = got.shape
    tr = _tile16(r, 264)

    def body(c_ref, a_ref, b_ref, o_ref):
        o_ref[...] = (a_ref[...] + b_ref[...].astype(F32)).astype(o_ref.dtype)

    spec = pl.BlockSpec((None, tr, c), lambda s, i, cr: (s, i, 0))
    return pl.pallas_call(
        body, grid_spec=pltpu.PrefetchScalarGridSpec(
            num_scalar_prefetch=1, grid=(k, r // tr),
            in_specs=[pl.BlockSpec((None, None, tr, c), lambda s, i, cr: (cr[0], s, i, 0)), spec], out_specs=spec),
        out_shape=SDS(got.shape, got.dtype), compiler_params=_params(2), name="add_cores")(core, g, got)


def _sum_chips(parts, own, place):
    k, r, c = parts.shape
    tr = _tile16(r, 264)

    def body(p_ref, *refs):
        part_refs, own_ref, o_ref = refs[:k], refs[k], refs[k + 1]
        mine = own_ref[...].astype(F32)
        acc = None
        for s in range(k):
            term = jnp.where(p_ref[0] == s, mine, part_refs[s][...].astype(F32))
            acc = term if acc is None else acc + term
        o_ref[...] = acc

    slots = jnp.stack([jnp.where(place[0] == s, (s + 1) % k, s) for s in range(k)]).astype(jnp.int32)
    other = lambda s: pl.BlockSpec((None, tr, c), lambda i, p: (p[2 + s], i, 0))
    return pl.pallas_call(
        body, grid_spec=pltpu.PrefetchScalarGridSpec(
            num_scalar_prefetch=1, grid=(r // tr,),
            in_specs=[other(s) for s in range(k)] + [pl.BlockSpec((None, tr, c), lambda i, p: (p[0], i, 0))],
            out_specs=pl.BlockSpec((None, tr, c), lambda i, p: (p[1], i, 0))),
        out_shape=SDS((2, r, c), F32), compiler_params=_params(1), name="sum_chips")(
            jnp.concatenate([place, slots]), *([parts] * k), own)


def _meta_grad(dh, nb, nc):
    d = dh.shape[1]

    def body(x_ref, o_ref):
        @pl.when(pl.program_id(0) == 0)
        def _():
            o_ref[...] = jnp.zeros_like(o_ref)

        o_ref[...] += x_ref[PAD:CH, :]

    return pl.pallas_call(body, grid=(nb,), in_specs=[pl.BlockSpec((CH, d), lambda b: (b * nc, 0))],
                          out_specs=pl.BlockSpec((N_META, d), lambda b: (0, 0)), out_shape=SDS((N_META, d), F32),
                          compiler_params=_params(1), name="meta_grad")(dh)


ANY = pl.BlockSpec(memory_space=pl.ANY)


def _place():
    x, y, c = lax.axis_index("x"), lax.axis_index("y"), lax.axis_index("c")
    chips = [(1 - x, y), (x, 1 - y), (1 - x, 1 - y)]
    return x, y, c, chips


def _remote(src, dst, send_sems, recv_sems, k, to):
    return pltpu.make_async_remote_copy(src_ref=src, dst_ref=dst, send_sem=send_sems.at[k], recv_sem=recv_sems.at[k],
                                        device_id=to, device_id_type=MESH)


def _gather_weights(pbs, ps):
    nt = len(pbs)

    def body(*refs):
        pb_refs, ps_ref, gb_refs, gs_ref = refs[:nt], refs[nt], refs[nt + 1:2 * nt + 1], refs[2 * nt + 1]
        send_sems, recv_sems, local_sems = refs[2 * nt + 2:]
        x, y, c, chips = _place()
        s = 2 * x + y
        sib = (x, y, 1 - c)
        l1 = pltpu.make_async_copy(ps_ref, gs_ref.at[s], local_sems.at[0])
        l1.start()
        sends = []
        for k, (px, py) in enumerate(chips):
            for t in range(nt):
                sends.append(_remote(pb_refs[t].at[c], gb_refs[t].at[s, c], send_sems, recv_sems, 6 * t + k, (px, py, c)))
            sends.append(_remote(ps_ref, gs_ref.at[s], send_sems, recv_sems, 6 * nt + k, (px, py, c)))
        for cp in sends:
            cp.start()
        for k, (px, py) in enumerate(chips):
            sk = 2 * px + py
            for t in range(nt):
                _remote(pb_refs[t].at[c], gb_refs[t].at[sk, c], send_sems, recv_sems, 6 * t + k, sib).wait_recv()
                fwd = _remote(gb_refs[t].at[sk, c], gb_refs[t].at[sk, c], send_sems, recv_sems, 6 * t + 3 + k, sib)
                fwd.start()
                sends.append(fwd)
        for k, (px, py) in enumerate(chips):
            sk = 2 * px + py
            for t in range(nt):
                _remote(pb_refs[t].at[c], gb_refs[t].at[sk, 1 - c], send_sems, recv_sems, 6 * t + 3 + k, sib).wait_recv()
            _remote(ps_ref, gs_ref.at[sk], send_sems, recv_sems, 6 * nt + k, sib).wait_recv()
        for cp in sends:
            cp.wait_send()
        l1.wait()

    nsem = 6 * nt + 3
    out = pl.pallas_call(
        body, in_specs=[ANY] * (nt + 1), out_specs=[ANY] * (nt + 1),
        out_shape=[SDS((4,) + pb.shape, pb.dtype) for pb in pbs] + [SDS((4,) + ps.shape, ps.dtype)],
        scratch_shapes=[pltpu.SemaphoreType.DMA((nsem,)), pltpu.SemaphoreType.DMA((nsem,)), pltpu.SemaphoreType.DMA((1,))],
        name="gather_weights")(*pbs, ps)
    return out[:nt], out[nt]


def _contain(wpad, shift):
    r, cw = wpad.shape
    tr = _tile16(r, 256)

    def body(n_ref, x_ref, o_ref):
        o_ref[...] = pltpu.roll(x_ref[...], n_ref[0], axis=1).astype(o_ref.dtype)

    spec = pl.BlockSpec((tr, cw), lambda i, n: (i, 0))
    return pl.pallas_call(
        body, grid_spec=pltpu.PrefetchScalarGridSpec(num_scalar_prefetch=1, grid=(r // tr,), in_specs=[spec], out_specs=spec),
        out_shape=SDS((r, cw), BF16), compiler_params=_params(1), name="contain")(shift, wpad)


def _place_own(gb, pb, chip):
    _, _, r, c = gb.shape
    tr = _tile16(r, 1100)

    def body(s_ref, p_ref, g_in, o_ref):
        o_ref[...] = p_ref[...]

    return pl.pallas_call(
        body, grid_spec=pltpu.PrefetchScalarGridSpec(
            num_scalar_prefetch=1, grid=(2, r // tr),
            in_specs=[pl.BlockSpec((None, tr, c), lambda h, i, s: (h, i, 0)), ANY],
            out_specs=pl.BlockSpec((None, None, tr, c), lambda h, i, s: (s[0], h, i, 0))),
        out_shape=SDS(gb.shape, gb.dtype), input_output_aliases={2: 0}, compiler_params=_params(2),
        name="place_own")(chip, pb, gb)


def _sem_scratch(n_remote, n_local):
    return [pltpu.SemaphoreType.DMA((n_remote,)), pltpu.SemaphoreType.DMA((n_remote,)), pltpu.SemaphoreType.DMA((n_local,))]


def _swap_halves(sends):
    nt = len(sends)

    def body(*refs):
        s_refs, got_refs = refs[:nt], refs[nt:2 * nt]
        send_sems, recv_sems = refs[2 * nt:]
        x, y, c, _ = _place()
        sib = (x, y, 1 - c)
        remote = [_remote(s_refs[t].at[1 - c, s], got_refs[t].at[s], send_sems, recv_sems, 4 * t + s, sib)
                  for t in range(nt) for s in range(4)]
        for cp in remote:
            cp.start()
        for cp in remote:
            cp.wait()

    return pl.pallas_call(
        body, in_specs=[ANY] * nt, out_specs=[ANY] * nt, out_shape=[SDS(g.shape[1:], g.dtype) for g in sends],
        scratch_shapes=[pltpu.SemaphoreType.DMA((4 * nt,)), pltpu.SemaphoreType.DMA((4 * nt,))], name="swap_halves")(*sends)


def _scatter_chip_sums(parts):
    nt = len(parts)

    def body(*refs):
        a_refs, r_refs = refs[:nt], refs[nt:2 * nt]
        send_sems, recv_sems = refs[2 * nt:]
        x, y, c, chips = _place()
        s = 2 * x + y
        sends = [_remote(a_refs[t].at[2 * px + py], r_refs[t].at[s], send_sems, recv_sems, 3 * t + k, (px, py, c))
                 for t in range(nt) for k, (px, py) in enumerate(chips)]
        for cp in sends:
            cp.start()
        for t in range(nt):
            for k, (px, py) in enumerate(chips):
                _remote(a_refs[t].at[s], r_refs[t].at[2 * px + py], send_sems, recv_sems, 3 * t + k, (px, py, c)).wait_recv()
        for cp in sends:
            cp.wait_send()

    return pl.pallas_call(
        body, in_specs=[ANY] * nt, out_specs=[ANY] * nt, out_shape=[SDS(a.shape, a.dtype) for a in parts],
        scratch_shapes=[pltpu.SemaphoreType.DMA((3 * nt,)), pltpu.SemaphoreType.DMA((3 * nt,))],
        name="scatter_chip_sums")(*parts)


def _join_halves(fs):
    nt = len(fs)

    def body(*refs):
        f_refs = refs[nt:2 * nt]
        send_sems, recv_sems = refs[2 * nt:]
        x, y, c, _ = _place()
        sib = (x, y, 1 - c)
        sends = [_remote(f_refs[t].at[c], f_refs[t].at[c], send_sems, recv_sems, t, sib) for t in range(nt)]
        for cp in sends:
            cp.start()
        for t in range(nt):
            _remote(f_refs[t].at[c], f_refs[t].at[1 - c], send_sems, recv_sems, t, sib).wait_recv()
        for cp in sends:
            cp.wait_send()

    return pl.pallas_call(
        body, in_specs=[ANY] * nt, out_specs=[ANY] * nt, out_shape=[SDS(f.shape, f.dtype) for f in fs],
        input_output_aliases={t: t for t in range(nt)},
        scratch_shapes=[pltpu.SemaphoreType.DMA((nt,)), pltpu.SemaphoreType.DMA((nt,))], name="join_halves")(*fs)


def _uncontain(cont, n_head, width):
    r, cw = cont.shape
    tr = _tile(r, 256)

    def body(n_ref, x_ref, o_ref):
        o_ref[...] = pltpu.roll(x_ref[...], n_ref[0], axis=1)[:, :width]

    return pl.pallas_call(
        body, grid_spec=pltpu.PrefetchScalarGridSpec(
            num_scalar_prefetch=1, grid=(r // tr,), in_specs=[pl.BlockSpec((tr, cw), lambda i, n: (i, 0))],
            out_specs=pl.BlockSpec((tr, width), lambda i, n: (i, 0))),
        out_shape=SDS((r, width), F32), compiler_params=_params(1), name="uncontain")(n_head, cont)


WEIGHTS = ("meta_tokens", "norm_w", "w_in", "conv_w", "a_log", "dt_bias", "gnorm_a", "gnorm_b", "hgrn_lower_bounds",
           "w_branch_a", "w_branch_b", "w_out", "final_norm_w")
SHARD_AXIS = {"meta_tokens": 1, "w_in": 2, "conv_w": 2, "w_branch_a": 2, "w_branch_b": 2, "w_out": 1}
FLAT_C = 1024


def _flat(parts, rows, cols=FLAT_C):
    v = jnp.concatenate([p.reshape(-1) for p in parts])
    return jnp.pad(v, (0, rows * cols - v.shape[0])).reshape(rows, cols)


def _local_step(x, target, w, lay):
    nb, seq, d = x.shape
    tp = CH + seq
    nc = tp // CH
    n = nb * tp
    e_mat, s_mat = _gate_consts()
    lb_all = _lb_fwd(w["hgrn_lower_bounds"])
    h = jnp.concatenate([jnp.zeros((nb, PAD, d), F32), jnp.broadcast_to(w["meta_tokens"][None], (nb, N_META, d)), x],
                        axis=1).reshape(n, d)
    rep = lambda a: jnp.repeat(a, HD)[None, :]
    saved = []
    for l in range(DEPTH):
        nw = w["norm_w"][l][None, :]
        proj, xn = _norm_proj_fwd(h, nw, w["w_in"][l])
        qkv = _gdn_prep_fwd(proj, w["conv_w"][l], lay, nb, tp)
        alog, dtb = rep(w["a_log"][l]), rep(w["dt_bias"][l])
        oa, sa = _gdn_fwd(qkv, proj, e_mat, alog, dtb, lay, nb, nc)
        lbl = lb_all[l][None, :]
        ob, sb = _hgrn_fwd(proj, lbl, lay, nb, nc)
        ga, gb = w["gnorm_a"][l][None, :], w["gnorm_b"][l][None, :]
        hn = _merge_fwd(h, oa, ob, proj, ga, gb, w["w_branch_a"][l], w["w_branch_b"][l], w["w_out"][l], lay)
        saved.append((h, nw, proj, qkv, alog, dtb, oa, sa, lbl, ob, sb, ga, gb, xn))
        h = hn
    lp, dh, dfw = _loss_head(h, target.reshape(nb * seq, d), w["final_norm_w"][None, :], nb, nc)
    loss = jnp.sum(lp[::8, 0])
    g = {n_: [None] * DEPTH for n_ in WEIGHTS}
    dlb_all = [None] * DEPTH
    for l in reversed(range(DEPTH)):
        h, nw, proj, qkv, alog, dtb, oa, sa, lbl, ob, sb, ga, gb, xn = saved[l]
        dproj, doa, dob, dwa, dwb, dwo, dga, dgb = _merge_bwd(dh, oa, ob, proj, ga, gb, w["w_branch_a"][l],
                                                             w["w_branch_b"][l], w["w_out"][l], lay, tp)
        dproj, acc_b = _hgrn_bwd(proj, lbl, sb, dob, dproj, lay, nb, nc)
        dqkv, dproj, acc_a = _gdn_bwd(qkv, proj, e_mat, s_mat, alog, dtb, sa, doa, dproj, lay, nb, nc)
        dproj, dconv = _gdn_prep_bwd(proj, w["conv_w"][l], dqkv, dproj, lay, nb, tp)
        dh, dnw = _proj_bwd_dx(dproj, w["w_in"][l], h, nw, dh, tp)
        g["w_in"][l] = _proj_bwd_dw(dproj, xn, tp)
        g["norm_w"][l] = dnw[0]
        g["conv_w"][l] = dconv
        g["a_log"][l] = acc_a[0, ::HD]
        g["dt_bias"][l] = acc_a[1, ::HD]
        g["gnorm_a"][l], g["gnorm_b"][l] = dga[0], dgb[0]
        g["w_branch_a"][l], g["w_branch_b"][l], g["w_out"][l] = dwa, dwb, dwo
        dlb_all[l] = acc_b[0]
    grads = {n_: jnp.stack(v) for n_, v in g.items() if v[0] is not None}
    grads["hgrn_lower_bounds"] = _lb_bwd(w["hgrn_lower_bounds"], jnp.stack(dlb_all))
    grads["final_norm_w"] = dfw[0]
    grads["meta_tokens"] = _meta_grad(dh, nb, nc)
    grad_x = dh.reshape(nb, tp, d)[:, CH:, :]
    return loss, grad_x, grads


def kernel(x, meta_tokens, norm_w, w_in, conv_w, a_log, dt_bias, gnorm_a, gnorm_b, hgrn_lower_bounds, w_branch_a, w_branch_b, w_out, final_norm_w, loss_target, m_meta_tokens, m_norm_w, m_w_in, m_conv_w, m_a_log, m_dt_bias, m_gnorm_a, m_gnorm_b, m_hgrn_lower_bounds, m_w_branch_a, m_w_branch_b, m_w_out, m_final_norm_w, v_meta_tokens, v_norm_w, v_w_in, v_conv_w, v_a_log, v_dt_bias, v_gnorm_a, v_gnorm_b, v_hgrn_lower_bounds, v_w_branch_a, v_w_branch_b, v_w_out, v_final_norm_w):
    wl = dict(meta_tokens=meta_tokens, norm_w=norm_w, w_in=w_in, conv_w=conv_w, a_log=a_log, dt_bias=dt_bias, gnorm_a=gnorm_a,
              gnorm_b=gnorm_b, hgrn_lower_bounds=hgrn_lower_bounds, w_branch_a=w_branch_a, w_branch_b=w_branch_b, w_out=w_out,
              final_norm_w=final_norm_w)
    ml = dict(zip(WEIGHTS, (m_meta_tokens, m_norm_w, m_w_in, m_conv_w, m_a_log, m_dt_bias, m_gnorm_a, m_gnorm_b,
                            m_hgrn_lower_bounds, m_w_branch_a, m_w_branch_b, m_w_out, m_final_norm_w)))
    vl = dict(zip(WEIGHTS, (v_meta_tokens, v_norm_w, v_w_in, v_conv_w, v_a_log, v_dt_bias, v_gnorm_a, v_gnorm_b,
                            v_hgrn_lower_bounds, v_w_branch_a, v_w_branch_b, v_w_out, v_final_norm_w)))
    d = x.shape[2]
    lay = _Layout(d)
    nchip = 4

    big = ("w_in", "w_branch_a", "w_branch_b", "w_out")
    small = ("conv_w", "meta_tokens")
    table, heads, cw = lay.pieces(nchip)
    sw = wl["w_in"].shape[2]
    chip_id = (2 * lax.axis_index("x") + lax.axis_index("y")).astype(jnp.int32)
    n_head = sum(jnp.where(chip_id == s, heads[s], 0) for s in range(nchip)).astype(jnp.int32)
    w_pad = jnp.pad(wl["w_in"], ((0, 0), (0, 0), (0, cw - sw))).reshape(DEPTH * d, cw)
    shift = jnp.where(n_head == 0, 0, cw - n_head).astype(jnp.int32).reshape(1)
    pbs = [_contain(w_pad, shift).reshape(DEPTH, d, cw)] + [wl[n].astype(BF16) for n in big[1:]]
    nsmall = sum(int(np.prod(wl[n].shape)) for n in small)
    rs = -(-nsmall // (HD * 8)) * 8
    ps = jnp.pad(jnp.concatenate([wl[n].reshape(-1) for n in small]), (0, rs * HD - nsmall)).reshape(rs, HD)
    gbig, gsmall = _gather_weights(pbs, ps)
    gbig = [_place_own(g, p, chip_id.reshape(1)) for g, p in zip(gbig, pbs)]
    gsmall = gsmall.reshape(nchip, -1)

    wf = dict(wl)
    wf["w_in"] = lay.from_containers([gbig[0][s] for s in range(nchip)])
    for i, n in enumerate(big[1:], start=1):
        wf[n] = jnp.concatenate([gbig[i][s] for s in range(nchip)], axis=SHARD_AXIS[n])
    o = 0
    for n in small:
        sz = int(np.prod(wl[n].shape))
        a = gsmall[:, o:o + sz].reshape((nchip,) + wl[n].shape)
        wf[n] = jnp.concatenate([a[s] for s in range(nchip)], axis=SHARD_AXIS[n])
        o += sz

    loss_part, grad_x, gfull = _local_step(x, loss_target, wf, lay)
    loss = lax.psum(loss_part, ("x", "y", "c"))

    sw = wl["w_in"].shape[2]
    conts, heads = lay.containers(gfull["w_in"], nchip)
    dd = wl["w_branch_a"].shape[2]
    rows_o = wl["w_out"].shape[1]
    by_dest = lambda g, n: [lax.slice_in_dim(g, s * wl[n].shape[SHARD_AXIS[n]], (s + 1) * wl[n].shape[SHARD_AXIS[n]],
                                            axis=SHARD_AXIS[n]) if n in SHARD_AXIS else g for s in range(nchip)]
    small_names = tuple(n for n in WEIGHTS if n not in big)
    nsm = sum(int(np.prod(wl[n].shape)) for n in small_names)
    rsm = -(-nsm // (2 * HD * 8)) * 8
    pack_small = lambda parts: _flat(parts, 2 * rsm, HD).reshape(2, rsm, HD)
    small_by_dest = [by_dest(gfull[n], n) for n in small_names]
    gs = [jnp.stack(conts, axis=1),
          jnp.stack(by_dest(gfull["w_branch_a"], "w_branch_a"), axis=1),
          jnp.stack(by_dest(gfull["w_branch_b"], "w_branch_b"), axis=1),
          gfull["w_out"].reshape(DEPTH, nchip, rows_o, d),
          jnp.stack([pack_small([p[s] for p in small_by_dest]) for s in range(nchip)], axis=1)]
    gs = [g.reshape((2, nchip, -1, g.shape[-1])) for g in gs]
    my_chip = (2 * lax.axis_index("x") + lax.axis_index("y")).astype(jnp.int32)
    my_core = lax.axis_index("c").astype(jnp.int32)
    got = _swap_halves([g.astype(BF16) for g in gs[:4]] + gs[4:])
    chip_sums = [_add_cores(g, b, my_core.reshape(1)) for g, b in zip(gs, got)]
    by_chip = _scatter_chip_sums(chip_sums)
    place = jnp.stack([my_chip, my_core])
    full = _join_halves([_sum_chips(p, a, place) for p, a in zip(by_chip, chip_sums)])
    n_head = sum(jnp.where(my_chip == s, heads[s], 0) for s in range(nchip)).astype(jnp.int32).reshape(1)
    g_w_in = _uncontain(full[0].reshape(DEPTH * d, -1), n_head, sw)
    g2 = {"w_in": g_w_in, "w_branch_a": full[1].reshape(-1, dd), "w_branch_b": full[2].reshape(-1, dd),
          "w_out": full[3].reshape(-1, d), "small": full[4].reshape(2 * rsm, HD)}

    def two_d(src, n):
        if n == "small":
            return _flat([src[k] for k in small_names], 2 * rsm, HD)
        return src[n].reshape(g2[n].shape)

    outs = {}
    for n in big + ("small",):
        delta, mnew, vnew = _adamw(g2[n], two_d(wl, n), two_d(ml, n), two_d(vl, n))
        outs[n] = (g2[n], delta, mnew, vnew)
    res = [{}, {}, {}, {}]
    for i in range(4):
        for n in big:
            res[i][n] = outs[n][i].reshape(wl[n].shape)
        v, o = outs["small"][i].reshape(-1), 0
        for n in small_names:
            sz = int(np.prod(wl[n].shape))
            res[i][n] = v[o:o + sz].reshape(wl[n].shape)
            o += sz
    return (loss, grad_x, *[res[0][n] for n in WEIGHTS], *[res[1][n] for n in WEIGHTS], *[res[2][n] for n in WEIGHTS],
            *[res[3][n] for n in WEIGHTS])
```

```python
import functools

import numpy as np
import jax
import jax.numpy as jnp
from jax import lax
from jax.experimental import pallas as pl
from jax.experimental.pallas import tpu as pltpu

F32 = jnp.float32
BF16 = jnp.bfloat16
HI = lax.Precision.HIGHEST
SDS = jax.ShapeDtypeStruct

NH = 4
HD = 128
HW = NH * HD
N_META = 16
CH = 64
SUB = 16
PAD = CH - N_META
EPS = 1e-6
Q_SCALE = HD ** -0.5
DEPTH = 2
CONV_K = 4
VMEM_LIMIT = 56 * 1024 * 1024
ADAM_LR, ADAM_B1, ADAM_B2, ADAM_EPS, ADAM_WD, ADAM_STEP = 0.001, 0.9, 0.999, 1e-08, 0.01, 10
MESH = pl.DeviceIdType.MESH


def _nn(a, b):
    return jnp.dot(a, b, precision=HI, preferred_element_type=F32)


def _nt(a, b):
    return lax.dot_general(a, b, (((1,), (1,)), ((), ())), precision=HI, preferred_element_type=F32)


def _tn(a, b):
    return _nn(a.T, b)


def _scan_rows(x, group, reverse=False):
    n = x.shape[0]
    pos = lax.bitwise_and(_iota2(x.shape, 0), group - 1)
    s = 1
    while s < group:
        if reverse:
            x = x + jnp.where(pos < group - s, pltpu.roll(x, n - s, axis=0), 0.0)
        else:
            x = x + jnp.where(pos >= s, pltpu.roll(x, s, axis=0), 0.0)
        s *= 2
    return x


def _bnn(a, b):
    return jnp.dot(a.astype(BF16), b.astype(BF16), preferred_element_type=F32)


def _bnt(a, b):
    return lax.dot_general(a.astype(BF16), b.astype(BF16), (((1,), (1,)), ((), ())), preferred_element_type=F32)


def _btn(a, b):
    return lax.dot_general(a.astype(BF16), b.astype(BF16), (((0,), (0,)), ((), ())), preferred_element_type=F32)


def _hi_lo(x):
    hi = x.astype(jnp.bfloat16)
    return hi, (x - hi.astype(F32)).astype(jnp.bfloat16)


def _dot3(dims):
    def f(a, b):
        ah, al = _hi_lo(a)
        bh, bl = _hi_lo(b)
        d = lambda p, q: lax.dot_general(p, q, (dims, ((), ())), preferred_element_type=F32)
        return d(ah, bh) + (d(ah, bl) + d(al, bh))
    return f


_rnn, _rnt, _rtn = _dot3(((1,), (0,))), _dot3(((1,), (1,))), _dot3(((0,), (0,)))
_hnn, _hnt, _htn = _bnn, _bnt, _btn


def _rr(x):
    return x


def _sig(x):
    return jax.nn.sigmoid(x)


def _silu(x):
    return x * _sig(x)


def _dsilu(x):
    s = _sig(x)
    return s * (1.0 + x * (1.0 - s))


def _softplus(x):
    return jnp.maximum(x, 0.0) + jnp.log(1.0 + jnp.exp(-jnp.abs(x)))


def _logsig(x):
    return jnp.minimum(x, 0.0) - jnp.log(1.0 + jnp.exp(-jnp.abs(x)))


def _rs(x):
    return jnp.sum(x, axis=-1, keepdims=True)


def _params(n_axes):
    return pltpu.CompilerParams(dimension_semantics=("arbitrary",) * n_axes, vmem_limit_bytes=VMEM_LIMIT)


def _tile(n, target):
    best = 8
    for t in range(8, target + 1, 8):
        if n % t == 0:
            best = t
    return best


def _ctile(pw, most=7):
    return HD * max(k for k in range(1, most + 1) if (pw // HD) % k == 0)


def _iota2(shape, axis):
    return lax.broadcasted_iota(jnp.int32, shape, axis)


class _Layout:
    def __init__(self, d):
        self.d = d
        self.wm = 2 * HW + 2 * d
        self.c_qkv = self.wm
        self.c_b = self.wm + 3 * HW
        self.c_ba = self.wm + 6 * HW
        self.pw = self.c_ba + HD
        assert self.c_b % (3 * HW) == 0
        o = 0
        segs = {}
        for name, w in (("a_q", HW), ("a_k", HW), ("a_v", HW), ("ba", 2 * NH), ("a_z", HW), ("b_q", HW), ("b_f", HW),
                        ("b_i", HW), ("b_g", HW), ("gate_a", d), ("gate_b", d)):
            segs[name] = (o, o + w)
            o += w
        self.segs = segs
        self.width = o
        self.order = ("a_z", "b_g", "gate_a", "gate_b", "a_q", "a_k", "a_v", "b_q", "b_f", "b_i", "ba")

    def to_kernel(self, w):
        parts = [w[..., self.segs[n][0]:self.segs[n][1]] for n in self.order]
        parts.append(jnp.zeros(w.shape[:-1] + (HD - 2 * NH,), w.dtype))
        return jnp.concatenate(parts, axis=-1)

    def containers(self, g, nchip):
        table, heads, cw = self.pieces(nchip)
        out = []
        for s in range(nchip):
            parts, at = [], 0
            for kcol, w, ccol in sorted(table[s], key=lambda p: p[2]):
                if ccol > at:
                    parts.append(jnp.zeros(g.shape[:-1] + (ccol - at,), g.dtype))
                parts.append(g[..., kcol:kcol + w])
                at = ccol + w
            if at < cw:
                parts.append(jnp.zeros(g.shape[:-1] + (cw - at,), g.dtype))
            out.append(jnp.concatenate(parts, axis=-1))
        return out, heads

    def pieces(self, nchip):
        off, where = 0, {}
        for n in self.order:
            where[n] = off
            off += self.segs[n][1] - self.segs[n][0]
        names = sorted(self.segs, key=lambda n: self.segs[n][0])
        sw = self.width // nchip
        cw = -(-sw // HD) * HD
        table, heads = [], []
        for s in range(nchip):
            lo, hi = s * sw, (s + 1) * sw
            pieces = []
            for n in names:
                a, b = max(lo, self.segs[n][0]), min(hi, self.segs[n][1])
                if a < b:
                    pieces.append((where[n] + a - self.segs[n][0], b - a))
            start, width = pieces[0]
            n_head = min((-start) % HD, width)
            body = ([(start + n_head, width - n_head)] if width > n_head else []) + pieces[1:]
            rows, at = [], 0
            for c, w in body:
                rows.append((c, w, at))
                at += w
            if n_head:
                rows.append((start, n_head, cw - n_head))
            table.append(rows)
            heads.append(n_head)
        return table, heads, cw

    def from_containers(self, conts):
        table, _, _ = self.pieces(len(conts))
        cut = sorted((kcol, w, s, ccol) for s, rows in enumerate(table) for kcol, w, ccol in rows)
        parts, at = [], 0
        for kcol, w, s, ccol in cut:
            assert kcol == at, (kcol, at)
            parts.append(conts[s][..., ccol:ccol + w])
            at = kcol + w
        parts.append(jnp.zeros(conts[0].shape[:-1] + (self.pw - at,), conts[0].dtype))
        return jnp.concatenate(parts, axis=-1)

    def from_kernel(self, g):
        off, where = 0, {}
        for n in self.order:
            w = self.segs[n][1] - self.segs[n][0]
            where[n] = (off, off + w)
            off += w
        names = sorted(self.segs, key=lambda n: self.segs[n][0])
        return jnp.concatenate([g[..., where[n][0]:where[n][1]] for n in names], axis=-1)


def _norm_proj_fwd(h, nw, wp):
    n, d = h.shape
    pw = wp.shape[1]
    tm, tn = _tile16(n, 768), _ctile(pw)

    def body(h_ref, nw_ref, w_ref, o_ref, xn_ref):
        @pl.when(pl.program_id(1) == 0)
        def _():
            x = h_ref[...]
            r = lax.rsqrt(jnp.mean(x * x, axis=-1, keepdims=True) + EPS)
            xn_ref[...] = (x * r * nw_ref[...]).astype(BF16)

        o_ref[...] = jnp.dot(xn_ref[...], w_ref[...], preferred_element_type=F32)

    return pl.pallas_call(
        body, grid=(n // tm, pw // tn),
        in_specs=[pl.BlockSpec((tm, d), lambda i, j: (i, 0)), pl.BlockSpec((1, d), lambda i, j: (0, 0)),
                  pl.BlockSpec((d, tn), lambda i, j: (0, j))],
        out_specs=[pl.BlockSpec((tm, tn), lambda i, j: (i, j)), pl.BlockSpec((tm, d), lambda i, j: (i, 0))],
        out_shape=[SDS((n, pw), F32), SDS((n, d), BF16)], compiler_params=_params(2), name="norm_proj_fwd")(h, nw, wp)


def _row_valid(tm, tp, base):
    row = base + _iota2((tm, 1), 0)
    return lax.rem(row, tp) >= PAD


def _proj_bwd_dx(dproj, wp, h, nw, dhn, tp):
    n, d = h.shape
    pw = wp.shape[1]
    tm, tk = _tile16(n, 768), _ctile(pw)
    nk = pw // tk

    def body(dp_ref, w_ref, h_ref, nw_ref, dhn_ref, dh_ref, dnw_ref, acc_ref):
        i, k = pl.program_id(0), pl.program_id(1)

        @pl.when(k == 0)
        def _():
            acc_ref[...] = jnp.zeros_like(acc_ref)

        @pl.when((i == 0) & (k == 0))
        def _():
            dnw_ref[...] = jnp.zeros_like(dnw_ref)

        valid = _row_valid(tm, tp, i * tm)
        dp = jnp.where(valid, dp_ref[...], 0.0)
        acc_ref[...] += _bnt(dp, w_ref[...])

        @pl.when(k == nk - 1)
        def _():
            x = h_ref[...]
            r = lax.rsqrt(jnp.mean(x * x, axis=-1, keepdims=True) + EPS)
            xh = x * r
            dxn = acc_ref[...]
            dnw_ref[...] += jnp.sum(dxn * xh, axis=0, keepdims=True)
            dxh = dxn * nw_ref[...]
            dh_ref[...] = dhn_ref[...] + r * (dxh - xh * jnp.mean(dxh * xh, axis=-1, keepdims=True))

    return pl.pallas_call(
        body, grid=(n // tm, nk),
        in_specs=[pl.BlockSpec((tm, tk), lambda i, k: (i, k)), pl.BlockSpec((d, tk), lambda i, k: (0, k)),
                  pl.BlockSpec((tm, d), lambda i, k: (i, 0)), pl.BlockSpec((1, d), lambda i, k: (0, 0)),
                  pl.BlockSpec((tm, d), lambda i, k: (i, 0))],
        out_specs=[pl.BlockSpec((tm, d), lambda i, k: (i, 0)), pl.BlockSpec((1, d), lambda i, k: (0, 0))],
        out_shape=[SDS((n, d), F32), SDS((1, d), F32)],
        scratch_shapes=[pltpu.VMEM((tm, d), F32)], compiler_params=_params(2), name="proj_bwd_dx")(dproj, wp, h, nw, dhn)


def _proj_bwd_dw(dproj, xn, tp):
    n, d = xn.shape
    pw = dproj.shape[1]
    tm, tn = _tile16(n, 768), _ctile(pw)

    def body(dp_ref, xn_ref, dw_ref):
        i = pl.program_id(1)

        @pl.when(i == 0)
        def _():
            dw_ref[...] = jnp.zeros_like(dw_ref)

        dp = jnp.where(_row_valid(tm, tp, i * tm), dp_ref[...], 0.0)
        dw_ref[...] += _btn(xn_ref[...], dp)

    return pl.pallas_call(
        body, grid=(pw // tn, n // tm),
        in_specs=[pl.BlockSpec((tm, tn), lambda j, i: (i, j)), pl.BlockSpec((tm, d), lambda j, i: (i, 0))],
        out_specs=pl.BlockSpec((d, tn), lambda j, i: (0, j)), out_shape=SDS((d, pw), F32),
        compiler_params=_params(2), name="proj_bwd_dw")(dproj, xn)


def _conv_silu(x, w, row):
    c = x * w[CONV_K - 1:CONV_K, :]
    for k in range(1, CONV_K):
        c = c + jnp.where(row >= k, pltpu.roll(x, k, axis=0), 0.0) * w[CONV_K - 1 - k:CONV_K - k, :]
    return c


def _gdn_prep_fwd(proj, conv_w, lay, nb, tp):
    n = proj.shape[0]
    nblk = 3 * NH
    cb = lay.c_qkv // HD

    def body(p_ref, w_ref, o_ref):
        j = pl.program_id(1)
        x = p_ref[...]
        row = _iota2(x.shape, 0)
        c = _conv_silu(x, w_ref[...], row)
        s = _silu(c)
        r = lax.rsqrt(_rs(s * s) + EPS)
        scale = jnp.where(j < NH, Q_SCALE, 1.0)
        y = jnp.where(j < 2 * NH, s * r * scale, s)
        o_ref[...] = jnp.where(row >= PAD, y, 0.0)

    return pl.pallas_call(
        body, grid=(nb, nblk),
        in_specs=[pl.BlockSpec((tp, HD), lambda b, j: (b, cb + j)), pl.BlockSpec((CONV_K, HD), lambda b, j: (0, j))],
        out_specs=pl.BlockSpec((tp, HD), lambda b, j: (b, j)), out_shape=SDS((n, nblk * HD), F32),
        compiler_params=_params(2), name="gdn_prep_fwd")(proj, conv_w)


def _gdn_prep_bwd(proj, conv_w, dqkv, dproj, lay, nb, tp):
    nblk = 3 * NH
    cb = lay.c_qkv // HD

    def body(p_ref, w_ref, dy_ref, dp_in, dp_ref, dw_ref):
        j, b = pl.program_id(0), pl.program_id(1)
        x = p_ref[...]
        w = w_ref[...]
        row = _iota2(x.shape, 0)
        c = _conv_silu(x, w, row)
        s = _silu(c)
        dy = jnp.where(row >= PAD, dy_ref[...], 0.0)
        r = lax.rsqrt(_rs(s * s) + EPS)
        nh = s * r
        scale = jnp.where(j < NH, Q_SCALE, 1.0)
        ds_n = scale * r * (dy - nh * _rs(dy * nh))
        ds = jnp.where(j < 2 * NH, ds_n, dy)
        dc = ds * _dsilu(c)
        dx = dc * w[CONV_K - 1:CONV_K, :]
        dws = [jnp.sum(dc * x, axis=0, keepdims=True)]
        for k in range(1, CONV_K):
            dx = dx + jnp.where(row < tp - k, pltpu.roll(dc, tp - k, axis=0), 0.0) * w[CONV_K - 1 - k:CONV_K - k, :]
            xs = jnp.where(row >= k, pltpu.roll(x, k, axis=0), 0.0)
            dws.append(jnp.sum(dc * xs, axis=0, keepdims=True))
        dp_ref[...] = dx.astype(dp_ref.dtype)
        r4 = _iota2((CONV_K, HD), 0)
        dw = jnp.zeros((CONV_K, HD), F32)
        for k in range(CONV_K):
            dw = dw + jnp.where(r4 == CONV_K - 1 - k, dws[k], 0.0)

        @pl.when(b == 0)
        def _():
            dw_ref[...] = dw

        @pl.when(b > 0)
        def _():
            dw_ref[...] += dw

    return pl.pallas_call(
        body, grid=(nblk, nb),
        in_specs=[pl.BlockSpec((tp, HD), lambda j, b: (b, cb + j)), pl.BlockSpec((CONV_K, HD), lambda j, b: (0, j)),
                  pl.BlockSpec((tp, HD), lambda j, b: (b, j)), pl.BlockSpec(memory_space=pl.ANY)],
        out_specs=[pl.BlockSpec((tp, HD), lambda j, b: (b, cb + j)), pl.BlockSpec((CONV_K, HD), lambda j, b: (0, j))],
        out_shape=[SDS(dproj.shape, dproj.dtype), SDS((CONV_K, nblk * HD), F32)],
        input_output_aliases={3: 0}, compiler_params=_params(2), name="gdn_prep_bwd")(proj, conv_w, dqkv, dproj)


def _gate_consts():
    e = np.zeros((HD, 2 * HW), np.float32)
    s = np.zeros((2 * HW, HD), np.float32)
    for h in range(NH):
        e[h, h * HD:(h + 1) * HD] = 1.0
        e[NH + h, HW + h * HD:HW + (h + 1) * HD] = 1.0
        s[h * HD, h] = 1.0
        s[HW + h * HD, NH + h] = 1.0
    return jnp.asarray(e), jnp.asarray(s)


def _gdn_tri():
    i, j = _iota2((CH, CH), 0), _iota2((CH, CH), 1)
    return i >= j, i > j


def _each(fn, *lists):
    return [fn(*xs) for xs in zip(*lists)]


def _tri_inv(a_list, eye):
    p = [-a for a in a_list]
    t = [eye + x for x in p]
    for _ in range(5):
        p = _each(_rnn, p, p)
        tp_ = _each(_rnn, t, p)
        t = _each(lambda x, y: x + y, t, tp_)
    return t


def _gdn_chunks(args, solved=None):
    causal, strict = _gdn_tri()
    eye = jnp.where(_iota2((CH, CH), 0) == _iota2((CH, CH), 1), 1.0, 0.0)
    q, k, v, beta, g, s0 = (list(t) for t in zip(*args))
    gc = [_scan_rows(x, CH) for x in g]
    dm = [jnp.where(causal, jnp.exp(jnp.where(causal, x[:, :CH] - x[:, :CH].T, 0.0)), 0.0) for x in gc]
    ds = [jnp.where(strict, x, 0.0) for x in dm]
    kb = _each(lambda x, y: x * y, k, beta)
    kk = _each(_rnt, kb, k)
    a = _each(lambda x, y: x * y, kk, ds)
    eg = [jnp.exp(x) for x in gc]
    rw = _each(lambda x, y: x * y, kb, eg)
    if solved is None:
        tinv = _tri_inv(a, eye)
        rv = _each(lambda x, y: x * y, v, beta)
        u = _each(_rnn, tinv, rv)
        w = _each(_rnn, tinv, rw)
    else:
        tinv, u, w = (list(t) for t in zip(*solved))
    ws = _each(_rnn, w, s0)
    vn = _each(lambda x, y: x - y, u, ws)
    qk = _each(_rnt, q, k)
    p = _each(lambda x, y: x * y, qk, dm)
    qg = _each(lambda x, y: x * y, q, eg)
    out = []
    for i in range(len(args)):
        gl = gc[i][CH - 1:CH, :]
        ek = jnp.exp(gl - gc[i])
        out.append(dict(gc=gc[i], dm=dm[i], ds=ds[i], kb=kb[i], a=a[i], tinv=tinv[i], eg=eg[i], rw=rw[i], u=u[i], w=w[i],
                        vn=vn[i], p=p[i], qg=qg[i], egl=jnp.exp(gl), ek=ek, kd=k[i] * ek))
    return out


def _gdn_gates(ba, e, alog, dtb):
    raw = _nn(ba, e)
    beta = _sig(raw[:, :HW])
    za = raw[:, HW:] + dtb
    g = -jnp.exp(alog) * _softplus(za)
    return beta, g, za


def _seqs_per_step(nb):
    return 4 if nb % 4 == 0 else (2 if nb % 2 == 0 else 1)


def _gdn_fwd(qkv, proj, e_mat, alog, dtb, lay, nb, nc):
    n = qkv.shape[0]
    tp = n // nb
    cba = lay.c_ba // HD
    gb = _seqs_per_step(nb)

    def body(x_ref, ba_ref, e_ref, al_ref, dt_ref, o_ref, so_ref, sv_ref, s_ref):
        @pl.when(pl.program_id(1) == 0)
        def _():
            s_ref[...] = jnp.zeros_like(s_ref)

        args = []
        for j in range(gb):
            beta, g, _ = _gdn_gates(ba_ref[j], e_ref[...], al_ref[...], dt_ref[...])
            for h in range(NH):
                hs = slice(h * HD, (h + 1) * HD)
                args.append((x_ref[j, :, hs], x_ref[j, :, HW + h * HD:HW + (h + 1) * HD],
                             x_ref[j, :, 2 * HW + h * HD:2 * HW + (h + 1) * HD], beta[:, hs], g[:, hs], s_ref[j, h]))
        cs = _gdn_chunks(args)
        s0s = [a[5] for a in args]
        o1 = _each(lambda c, s0: _rnn(c["qg"], s0), cs, s0s)
        o2 = [_rnn(c["p"], c["vn"]) for c in cs]
        upd = [_rtn(c["kd"], c["vn"]) for c in cs]
        res = [(o1[i] + o2[i], s0s[i] * cs[i]["egl"] + upd[i]) for i in range(len(cs))]
        zero = jnp.zeros((CH, HD - CH), F32)
        for j in range(gb):
            for h in range(NH):
                c = cs[j * NH + h]
                so_ref[j, h] = args[j * NH + h][5]
                sv_ref[j, h] = jnp.concatenate([c["u"], c["w"], c["tinv"], zero], axis=-1)
                s_ref[j, h] = res[j * NH + h][1]
            o_ref[j] = jnp.concatenate([res[j * NH + h][0] for h in range(NH)], axis=-1)

    o, st, sv = pl.pallas_call(
        body, grid=(nb // gb, nc),
        in_specs=[pl.BlockSpec((gb, CH, 3 * HW), lambda b, c: (b, c, 0)), pl.BlockSpec((gb, CH, HD), lambda b, c: (b, c, cba)),
                  pl.BlockSpec((HD, 2 * HW), lambda b, c: (0, 0)), pl.BlockSpec((1, HW), lambda b, c: (0, 0)),
                  pl.BlockSpec((1, HW), lambda b, c: (0, 0))],
        out_specs=[pl.BlockSpec((gb, CH, HW), lambda b, c: (b, c, 0)),
                   pl.BlockSpec((gb, None, NH, HD, HD), lambda b, c: (b, c, 0, 0, 0)),
                   pl.BlockSpec((gb, None, NH, CH, 3 * HD), lambda b, c: (b, c, 0, 0, 0))],
        out_shape=[SDS((nb, tp, HW), F32), SDS((nb, nc, NH, HD, HD), F32), SDS((nb, nc, NH, CH, 3 * HD), F32)],
        scratch_shapes=[pltpu.VMEM((gb, NH, HD, HD), F32)], compiler_params=_params(2), name="gdn_fwd")(
            qkv.reshape(nb, tp, 3 * HW), proj.reshape(nb, tp, -1), e_mat, alog, dtb)
    return o.reshape(n, HW), st, sv


def _gdn_bwd(qkv, proj, e_mat, s_mat, alog, dtb, states, solved, do, dproj, lay, nb, nc):
    n = qkv.shape[0]
    tp = n // nb
    cba = lay.c_ba // HD
    gb = _seqs_per_step(nb)

    def body(x_ref, ba_ref, e_ref, sm_ref, al_ref, dt_ref, st_ref, sv_ref, do_ref, dp_in, dx_ref, dba_ref, acc_ref, ds_ref):
        ci = pl.program_id(1)

        @pl.when(ci == 0)
        def _():
            ds_ref[...] = jnp.zeros_like(ds_ref)

        @pl.when((ci == 0) & (pl.program_id(0) == 0))
        def _():
            acc_ref[...] = jnp.zeros_like(acc_ref)

        causal, strict = _gdn_tri()
        alog = al_ref[...]
        row = _iota2((CH, 1), 0)
        valid = (row >= PAD) | (ci < nc - 1)
        last = row == CH - 1
        gates = [_gdn_gates(ba_ref[j], e_ref[...], alog, dt_ref[...]) for j in range(gb)]
        args, do, ds1, solved = [], [], [], []
        for j in range(gb):
            beta, g, _ = gates[j]
            for h in range(NH):
                hs = slice(h * HD, (h + 1) * HD)
                args.append((x_ref[j, :, hs], x_ref[j, :, HW + h * HD:HW + (h + 1) * HD],
                             x_ref[j, :, 2 * HW + h * HD:2 * HW + (h + 1) * HD], beta[:, hs], g[:, hs], st_ref[j, h]))
                do.append(do_ref[j, :, hs])
                ds1.append(ds_ref[j, h])
                solved.append((sv_ref[j, h, :, 2 * HD:2 * HD + CH], sv_ref[j, h, :, 0:HD], sv_ref[j, h, :, HD:2 * HD]))
        q, k, v, bh, _, s0 = (list(t) for t in zip(*args))
        cs = _gdn_chunks(args, solved)
        get = lambda name: [c[name] for c in cs]
        mul = lambda x, y: x * y
        add = lambda x, y: x + y
        dvn = _each(add, _each(_rtn, get("p"), do), _each(_rnn, get("kd"), ds1))
        dqg = _each(_rnt, do, s0)
        dp = [jnp.where(causal, x, 0.0) for x in _each(_rnt, do, get("vn"))]
        dkd = _each(_rnt, get("vn"), ds1)
        dw = [-x for x in _each(_rnt, dvn, s0)]
        ds_a = _each(_rtn, get("qg"), do)
        ds_b = _each(_rtn, get("w"), dvn)
        ds_new = [ds_a[i] - ds_b[i] + ds1[i] * cs[i]["egl"] for i in range(len(cs))]
        drv = _each(_rtn, get("tinv"), dvn)
        drw = _each(_rtn, get("tinv"), dw)
        da_1 = _each(_rnt, drv, get("u"))
        da_2 = _each(_rnt, drw, get("w"))
        da = [jnp.where(strict, -(x + y), 0.0) for x, y in zip(da_1, da_2)]
        m = [da[i] * cs[i]["a"] + dp[i] * cs[i]["p"] for i in range(len(cs))]
        dkk = _each(mul, da, get("ds"))
        dqk = _each(mul, dp, get("dm"))
        dq = _each(add, _each(_rnn, dqk, k), _each(mul, dqg, get("eg")))
        dkb = _each(add, _each(_rnn, dkk, k), _each(mul, drw, get("eg")))
        dk_1 = _each(_rtn, dqk, q)
        dk_2 = _each(_rtn, dkk, get("kb"))
        dk = [dk_1[i] + dk_2[i] + dkd[i] * cs[i]["ek"] + dkb[i] * bh[i] for i in range(len(cs))]
        dv = _each(mul, drv, bh)
        dbeta, dg = [], []
        for i, c in enumerate(cs):
            dbeta.append(_rs(drv[i] * v[i]) + _rs(dkb[i] * k[i]) + jnp.zeros((CH, HD), F32))
            t_kd = _rs(dkd[i] * c["kd"])
            dgc = _rs(m[i]) - _rs(m[i].T) + _rs(dqg[i] * c["qg"]) + _rs(drw[i] * c["rw"]) - t_kd
            tail = jnp.sum(t_kd, axis=0, keepdims=True) + c["egl"] * jnp.sum(_rs(s0[i] * ds1[i]), axis=0, keepdims=True)
            dgc = dgc + jnp.where(last, tail, 0.0)
            dg.append(_scan_rows(dgc + jnp.zeros((CH, HD), F32), CH, reverse=True))
        r8 = _iota2((8, HW), 0)
        upd = jnp.zeros((8, HW), F32)
        for j in range(gb):
            sl = slice(j * NH, (j + 1) * NH)
            beta, g, za = gates[j]
            for h in range(NH):
                ds_ref[j, h] = ds_new[j * NH + h]
            dx_ref[j] = jnp.concatenate(dq[sl] + dk[sl] + dv[sl], axis=-1)
            dbeta_j = jnp.where(valid, jnp.concatenate(dbeta[sl], axis=-1), 0.0)
            dg_j = jnp.where(valid, jnp.concatenate(dg[sl], axis=-1), 0.0)
            draw_b = dbeta_j * beta * (1.0 - beta)
            draw_a = dg_j * (-jnp.exp(alog)) * _sig(za)
            dba_ref[j] = _nn(jnp.concatenate([draw_b, draw_a], axis=-1), sm_ref[...]).astype(dba_ref.dtype)
            upd = upd + jnp.where(r8 == 0, jnp.sum(dg_j * g, axis=0, keepdims=True), 0.0) + jnp.where(
                r8 == 1, jnp.sum(draw_a, axis=0, keepdims=True), 0.0)
        acc_ref[...] += upd

    rc = lambda c: nc - 1 - c
    dqkv, dproj3, acc = pl.pallas_call(
        body, grid=(nb // gb, nc),
        in_specs=[pl.BlockSpec((gb, CH, 3 * HW), lambda b, c: (b, rc(c), 0)), pl.BlockSpec((gb, CH, HD), lambda b, c: (b, rc(c), cba)),
                  pl.BlockSpec((HD, 2 * HW), lambda b, c: (0, 0)), pl.BlockSpec((2 * HW, HD), lambda b, c: (0, 0)),
                  pl.BlockSpec((1, HW), lambda b, c: (0, 0)), pl.BlockSpec((1, HW), lambda b, c: (0, 0)),
                  pl.BlockSpec((gb, None, NH, HD, HD), lambda b, c: (b, rc(c), 0, 0, 0)),
                  pl.BlockSpec((gb, None, NH, CH, 3 * HD), lambda b, c: (b, rc(c), 0, 0, 0)),
                  pl.BlockSpec((gb, CH, HW), lambda b, c: (b, rc(c), 0)), pl.BlockSpec(memory_space=pl.ANY)],
        out_specs=[pl.BlockSpec((gb, CH, 3 * HW), lambda b, c: (b, rc(c), 0)), pl.BlockSpec((gb, CH, HD), lambda b, c: (b, rc(c), cba)),
                   pl.BlockSpec((8, HW), lambda b, c: (0, 0))],
        out_shape=[SDS((nb, tp, 3 * HW), F32), SDS((nb, tp, dproj.shape[1]), dproj.dtype), SDS((8, HW), F32)],
        input_output_aliases={9: 1},
        scratch_shapes=[pltpu.VMEM((gb, NH, HD, HD), F32)], compiler_params=_params(2), name="gdn_bwd")(
            qkv.reshape(nb, tp, 3 * HW), proj.reshape(nb, tp, -1), e_mat, s_mat, alog, dtb, states, solved, do.reshape(nb, tp, HW),
            dproj.reshape(nb, tp, -1))
    return dqkv.reshape(n, 3 * HW), dproj3.reshape(dproj.shape), acc


def _hgrn_inputs(zq, zf, lb):
    sg = _sig(zf)
    sgn = _sig(-zf)
    pos = lb > 0.0
    lbp = jnp.where(pos, lb, 0.0)
    fpos = lbp + (1.0 - lbp) * sg
    lf = jnp.where(pos, jnp.log(jnp.where(pos, fpos, 1.0)), _logsig(zf))
    k = (1.0 - lbp) * sgn
    q = _silu(zq) * Q_SCALE
    return q, k, lf, sg, sgn, pos, lbp, fpos


def _hgrn_consts():
    i3, j3 = _iota2((SUB, SUB, HD), 0), _iota2((SUB, SUB, HD), 1)
    return i3 >= j3


def _sum_j(x):
    return jnp.sum(x.reshape(SUB, SUB, HD), axis=1)


def _sum_i(x):
    return jnp.sum(x.reshape(SUB, SUB, HD), axis=0)


def _pairs(a, b):
    return (a[:, None, :] * b[None, :, :]).reshape(SUB * SUB, HD)


def _hgrn_sub(q, k, v, bc, st, consts):
    mask3 = consts
    bl = bc[SUB - 1:SUB, :]
    p3 = jnp.where(mask3, jnp.exp(jnp.where(mask3, bc[:, None, :] - bc[None, :, :], 0.0)), 0.0).reshape(SUB * SUB, HD)
    x = _pairs(q, k) * p3
    srep = _rs(x)
    vt = jnp.broadcast_to(v[None, :, :], (SUB, SUB, HD)).reshape(SUB * SUB, HD)
    eb = jnp.exp(bc)
    qe = q * eb
    o = _hnt(qe, st) + _sum_j(_rr(srep) * _rr(vt))
    ek = jnp.exp(bl - bc)
    kd = k * ek
    ebl = jnp.exp(bl)
    st1 = st * ebl + _htn(v, kd)
    return o, st1, dict(bc=bc, p3=p3, srep=srep, vt=vt, eb=eb, qe=qe, ek=ek, kd=kd, ebl=ebl)


def _hgrn_fwd(proj, lb, lay, nb, nc):
    n = proj.shape[0]
    cbb = lay.c_b // (3 * HW)

    def body(z_ref, lb_ref, o_ref, so_ref, s_ref):
        @pl.when(pl.program_id(1) == 0)
        def _():
            s_ref[...] = jnp.zeros_like(s_ref)

        consts = _hgrn_consts()
        outs = []
        for h in range(NH):
            hs = slice(h * HD, (h + 1) * HD)
            q, k, lf = _hgrn_inputs(z_ref[:, hs], z_ref[:, HW + h * HD:HW + (h + 1) * HD], lb_ref[:, hs])[:3]
            v = z_ref[:, 2 * HW + h * HD:2 * HW + (h + 1) * HD]
            st = s_ref[h]
            so_ref[h] = st
            bc = _scan_rows(lf, SUB)
            oh = []
            for s in range(CH // SUB):
                rs = slice(s * SUB, (s + 1) * SUB)
                o, st, _ = _hgrn_sub(q[rs], k[rs], v[rs], bc[rs], st, consts)
                oh.append(o)
            s_ref[h] = st
            outs.append(jnp.concatenate(oh, axis=0))
        o_ref[...] = jnp.concatenate(outs, axis=-1)

    return pl.pallas_call(
        body, grid=(nb, nc),
        in_specs=[pl.BlockSpec((CH, 3 * HW), lambda b, c: (b * nc + c, cbb)), pl.BlockSpec((1, HW), lambda b, c: (0, 0))],
        out_specs=[pl.BlockSpec((CH, HW), lambda b, c: (b * nc + c, 0)),
                   pl.BlockSpec((None, None, NH, HD, HD), lambda b, c: (b, c, 0, 0, 0))],
        out_shape=[SDS((n, HW), F32), SDS((nb, nc, NH, HD, HD), F32)],
        scratch_shapes=[pltpu.VMEM((NH, HD, HD), F32)], compiler_params=_params(2), name="hgrn_fwd")(proj, lb)


def _hgrn_bwd(proj, lb, states, do, dproj, lay, nb, nc):
    cbb = lay.c_b // (3 * HW)
    nsub = CH // SUB

    def rev(b, c):
        return b * nc + (nc - 1 - c)

    def body(z_ref, lb_ref, st_ref, do_ref, dp_in, dz_ref, acc_ref, ds_ref):
        ci = pl.program_id(1)

        @pl.when(ci == 0)
        def _():
            ds_ref[...] = jnp.zeros_like(ds_ref)

        @pl.when((ci == 0) & (pl.program_id(0) == 0))
        def _():
            acc_ref[...] = jnp.zeros_like(acc_ref)

        consts = _hgrn_consts()
        row = _iota2((CH, 1), 0)
        valid = (row >= PAD) | (ci < nc - 1)
        lastrow = _iota2((SUB, 1), 0) == SUB - 1
        dzq, dzf, dzi, dlbs = [], [], [], []
        for h in range(NH):
            hs = slice(h * HD, (h + 1) * HD)
            zq, zf = z_ref[:, hs], z_ref[:, HW + h * HD:HW + (h + 1) * HD]
            q, k, lf, sg, sgn, pos, lbp, fpos = _hgrn_inputs(zq, zf, lb_ref[:, hs])
            v = z_ref[:, 2 * HW + h * HD:2 * HW + (h + 1) * HD]
            doh = do_ref[:, hs]
            sts, fw = [st_ref[h]], []
            bc = _scan_rows(lf, SUB)
            for s in range(nsub):
                rs = slice(s * SUB, (s + 1) * SUB)
                _, st1, c = _hgrn_sub(q[rs], k[rs], v[rs], bc[rs], sts[-1], consts)
                sts.append(st1)
                fw.append(c)
            dst = ds_ref[h]
            dq_l, dk_l, dv_l, dlf_l = [None] * nsub, [None] * nsub, [None] * nsub, [None] * nsub
            for s in reversed(range(nsub)):
                rs = slice(s * SUB, (s + 1) * SUB)
                c, st = fw[s], sts[s]
                qs, ks, vs, dos = q[rs], k[rs], v[rs], doh[rs]
                dqe = _hnn(dos, st)
                dkd = _hnn(vs, dst)
                dsrep = _rs(_pairs(_rr(dos), _rr(vs)))
                w = dsrep * c["p3"]
                kt = jnp.broadcast_to(ks[None, :, :], (SUB, SUB, HD)).reshape(SUB * SUB, HD)
                qt = jnp.broadcast_to(qs[:, None, :], (SUB, SUB, HD)).reshape(SUB * SUB, HD)
                dq_i = _sum_j(w * kt)
                dk_i = _sum_i(w * qt)
                dot = jnp.broadcast_to(_rr(dos)[:, None, :], (SUB, SUB, HD)).reshape(SUB * SUB, HD)
                dvv = _sum_i(_rr(c["srep"]) * dot) + _hnt(c["kd"], dst)
                t_kd = dkd * c["kd"]
                dbc = dqe * c["qe"] - t_kd + qs * dq_i - ks * dk_i
                tail = jnp.sum(t_kd, axis=0, keepdims=True) + c["ebl"] * jnp.sum(st * dst, axis=0, keepdims=True)
                dbc = dbc + jnp.where(lastrow, tail, 0.0)
                dlf_l[s] = dbc
                dq_l[s] = dq_i + dqe * c["eb"]
                dk_l[s] = dk_i + dkd * c["ek"]
                dv_l[s] = dvv
                dst = _htn(dos, c["qe"]) + dst * c["ebl"]
            ds_ref[h] = dst
            dq, dk, dv, dbc = (jnp.concatenate(t, axis=0) for t in (dq_l, dk_l, dv_l, dlf_l))
            dlf = _scan_rows(dbc, SUB, reverse=True)
            dlft = dlf - dk * (1.0 - k)
            dlf_dz = jnp.where(pos, (1.0 - lbp) * sg * sgn / jnp.where(pos, fpos, 1.0), sgn)
            dlf_dlb = jnp.where(pos, sgn / jnp.where(pos, fpos, 1.0), 0.0)
            dzq.append(dq * Q_SCALE * _dsilu(zq))
            dzf.append(dlft * dlf_dz)
            dzi.append(dv)
            dlbs.append(jnp.sum(jnp.where(valid, dlft * dlf_dlb, 0.0), axis=0, keepdims=True))
        dz_ref[...] = jnp.concatenate(dzq + dzf + dzi, axis=-1).astype(dz_ref.dtype)
        acc_ref[...] += jnp.where(_iota2((8, HW), 0) == 0, jnp.concatenate(dlbs, axis=-1), 0.0)

    return pl.pallas_call(
        body, grid=(nb, nc),
        in_specs=[pl.BlockSpec((CH, 3 * HW), lambda b, c: (rev(b, c), cbb)), pl.BlockSpec((1, HW), lambda b, c: (0, 0)),
                  pl.BlockSpec((None, None, NH, HD, HD), lambda b, c: (b, nc - 1 - c, 0, 0, 0)),
                  pl.BlockSpec((CH, HW), lambda b, c: (rev(b, c), 0)), pl.BlockSpec(memory_space=pl.ANY)],
        out_specs=[pl.BlockSpec((CH, 3 * HW), lambda b, c: (rev(b, c), cbb)), pl.BlockSpec((8, HW), lambda b, c: (0, 0))],
        out_shape=[SDS(dproj.shape, dproj.dtype), SDS((8, HW), F32)],
        input_output_aliases={4: 0},
        scratch_shapes=[pltpu.VMEM((NH, HD, HD), F32)], compiler_params=_params(2), name="hgrn_bwd")(proj, lb, states, do, dproj)


def _gated_norm(o, z, gamma):
    ys, ns, rs = [], [], []
    for h in range(NH):
        hs = slice(h * HD, (h + 1) * HD)
        oh = o[:, hs]
        r = lax.rsqrt(jnp.mean(oh * oh, axis=-1, keepdims=True) + EPS)
        nh = oh * r
        ys.append(nh * gamma * _silu(z[:, hs]))
        ns.append(nh)
        rs.append(r)
    return jnp.concatenate(ys, axis=-1), ns, rs


def _merge_fwd(h, oa, ob, proj, ga, gb, wa, wb, wo, lay):
    n, d = h.shape
    tm = _tile(n, 384)
    wm = lay.wm

    def body(h_ref, oa_ref, ob_ref, p_ref, ga_ref, gb_ref, wa_ref, wb_ref, wo_ref, out_ref):
        ya, _, _ = _gated_norm(oa_ref[...], p_ref[:, 0:HW], ga_ref[...])
        yb, _, _ = _gated_norm(ob_ref[...], p_ref[:, HW:2 * HW], gb_ref[...])
        ya2 = _bnn(ya, wa_ref[...])
        yb2 = _bnn(yb, wb_ref[...])
        mixed = _sig(p_ref[:, 2 * HW:2 * HW + d]) * ya2 + _sig(p_ref[:, 2 * HW + d:2 * HW + 2 * d]) * yb2
        out_ref[...] = h_ref[...] + _bnn(mixed, wo_ref[...])

    full = lambda shape: pl.BlockSpec(shape, lambda i: (0, 0))
    return pl.pallas_call(
        body, grid=(n // tm,),
        in_specs=[pl.BlockSpec((tm, d), lambda i: (i, 0)), pl.BlockSpec((tm, HW), lambda i: (i, 0)),
                  pl.BlockSpec((tm, HW), lambda i: (i, 0)), pl.BlockSpec((tm, wm), lambda i: (i, 0)),
                  full((1, HD)), full((1, HD)), full((HW, d)), full((HW, d)), full((d, d))],
        out_specs=pl.BlockSpec((tm, d), lambda i: (i, 0)), out_shape=SDS((n, d), F32),
        compiler_params=_params(1), name="merge_fwd")(h, oa, ob, proj, ga, gb, wa, wb, wo)


def _gated_norm_bwd(dy, o, z, gamma):
    dos, dzs = [], []
    dgam = jnp.zeros((1, HD), F32)
    for h in range(NH):
        hs = slice(h * HD, (h + 1) * HD)
        oh, zh, dyh = o[:, hs], z[:, hs], dy[:, hs]
        r = lax.rsqrt(jnp.mean(oh * oh, axis=-1, keepdims=True) + EPS)
        nh = oh * r
        dzs.append(dyh * nh * gamma * _dsilu(zh))
        dng = dyh * _silu(zh)
        dgam = dgam + jnp.sum(dng * nh, axis=0, keepdims=True)
        dn = dng * gamma
        dos.append(r * (dn - nh * jnp.mean(dn * nh, axis=-1, keepdims=True)))
    return jnp.concatenate(dos, axis=-1), jnp.concatenate(dzs, axis=-1), dgam


def _merge_bwd(dhn, oa, ob, proj, ga, gb, wa, wb, wo, lay, tp):
    n, d = dhn.shape
    tm = _tile(n, 256)
    wm = lay.wm

    def body(dh_ref, oa_ref, ob_ref, p_ref, ga_ref, gb_ref, wa_ref, wb_ref, wo_ref,
             dp_ref, doa_ref, dob_ref, dwa_ref, dwb_ref, dwo_ref, dga_ref, dgb_ref):
        i = pl.program_id(0)

        @pl.when(i == 0)
        def _():
            for r in (dwa_ref, dwb_ref, dwo_ref, dga_ref, dgb_ref):
                r[...] = jnp.zeros_like(r)

        dh = jnp.where(_row_valid(tm, tp, i * tm), dh_ref[...], 0.0)
        oa, ob = oa_ref[...], ob_ref[...]
        za, zb = p_ref[:, 0:HW], p_ref[:, HW:2 * HW]
        gta, gtb = p_ref[:, 2 * HW:2 * HW + d], p_ref[:, 2 * HW + d:2 * HW + 2 * d]
        ya, _, _ = _gated_norm(oa, za, ga_ref[...])
        yb, _, _ = _gated_norm(ob, zb, gb_ref[...])
        ya2 = _bnn(ya, wa_ref[...])
        yb2 = _bnn(yb, wb_ref[...])
        sa, sb = _sig(gta), _sig(gtb)
        mixed = sa * ya2 + sb * yb2
        dmixed = _bnt(dh, wo_ref[...])
        dwo_ref[...] += _btn(mixed, dh)
        dya2 = dmixed * sa
        dyb2 = dmixed * sb
        dwa_ref[...] += _btn(ya, dya2)
        dwb_ref[...] += _btn(yb, dyb2)
        doa, dza, dga = _gated_norm_bwd(_bnt(dya2, wa_ref[...]), oa, za, ga_ref[...])
        dob, dzb, dgb = _gated_norm_bwd(_bnt(dyb2, wb_ref[...]), ob, zb, gb_ref[...])
        dga_ref[...] += dga
        dgb_ref[...] += dgb
        doa_ref[...] = doa
        dob_ref[...] = dob
        dt = dp_ref.dtype
        dp_ref[:, 0:HW] = dza.astype(dt)
        dp_ref[:, HW:2 * HW] = dzb.astype(dt)
        dp_ref[:, 2 * HW:2 * HW + d] = (dmixed * ya2 * sa * (1.0 - sa)).astype(dt)
        dp_ref[:, 2 * HW + d:2 * HW + 2 * d] = (dmixed * yb2 * sb * (1.0 - sb)).astype(dt)

    full = lambda shape: pl.BlockSpec(shape, lambda i: (0, 0))
    rows = lambda w: pl.BlockSpec((tm, w), lambda i: (i, 0))
    return pl.pallas_call(
        body, grid=(n // tm,),
        in_specs=[rows(d), rows(HW), rows(HW), rows(wm), full((1, HD)), full((1, HD)), full((HW, d)), full((HW, d)), full((d, d))],
        out_specs=[rows(wm), rows(HW), rows(HW), full((HW, d)), full((HW, d)), full((d, d)), full((1, HD)), full((1, HD))],
        out_shape=[SDS((n, lay.pw), BF16), SDS((n, HW), F32), SDS((n, HW), F32), SDS((HW, d), F32), SDS((HW, d), F32),
                   SDS((d, d), F32), SDS((1, HD), F32), SDS((1, HD), F32)],
        compiler_params=_params(1), name="merge_bwd")(dhn, oa, ob, proj, ga, gb, wa, wb, wo)


def _loss_head(h, target, fw, nb, nc):
    n, d = h.shape

    def body(h_ref, t_ref, fw_ref, lp_ref, dh_ref, dfw_ref):
        b, c = pl.program_id(0), pl.program_id(1)

        @pl.when((b == 0) & (c == 0))
        def _():
            dfw_ref[...] = jnp.zeros_like(dfw_ref)

        @pl.when(c == 0)
        def _():
            dh_ref[...] = jnp.zeros_like(dh_ref)
            lp_ref[...] = jnp.zeros_like(lp_ref)

        @pl.when(c > 0)
        def _():
            x = h_ref[...]
            r = lax.rsqrt(jnp.mean(x * x, axis=-1, keepdims=True) + EPS)
            xh = x * r
            err = xh * fw_ref[...] - t_ref[...]
            lp_ref[...] = jnp.zeros_like(lp_ref) + 0.5 * jnp.sum(_rs(err * err), axis=0, keepdims=True) / d
            dy = err / d
            dfw_ref[...] += jnp.sum(dy * xh, axis=0, keepdims=True)
            dxh = dy * fw_ref[...]
            dh_ref[...] = r * (dxh - xh * jnp.mean(dxh * xh, axis=-1, keepdims=True))

    return pl.pallas_call(
        body, grid=(nb, nc),
        in_specs=[pl.BlockSpec((CH, d), lambda b, c: (b * nc + c, 0)),
                  pl.BlockSpec((CH, d), lambda b, c: (b * (nc - 1) + jnp.maximum(c - 1, 0), 0)),
                  pl.BlockSpec((1, d), lambda b, c: (0, 0))],
        out_specs=[pl.BlockSpec((8, HD), lambda b, c: (b * nc + c, 0)), pl.BlockSpec((CH, d), lambda b, c: (b * nc + c, 0)),
                   pl.BlockSpec((1, d), lambda b, c: (0, 0))],
        out_shape=[SDS((nb * nc * 8, HD), F32), SDS((n, d), F32), SDS((1, d), F32)],
        compiler_params=_params(2), name="loss_head")(h, target, fw)


def _lb_fwd(lb):
    def body(x_ref, o_ref):
        x = x_ref[...]
        mx = jnp.max(x, axis=0, keepdims=True)
        e = jnp.exp(x - mx)
        sm = e / jnp.sum(e, axis=0, keepdims=True)
        run = jnp.zeros((1, HW), F32)
        for l in range(DEPTH):
            run = run + sm[l:l + 1, :]
            o_ref[l:l + 1, :] = run - sm[0:1, :]

    return pl.pallas_call(body, out_shape=SDS(lb.shape, F32), name="lb_fwd")(lb)


def _lb_bwd(lb, dlb_all):
    def body(x_ref, d_ref, o_ref):
        x = x_ref[...]
        dl = d_ref[...]
        mx = jnp.max(x, axis=0, keepdims=True)
        e = jnp.exp(x - mx)
        sm = e / jnp.sum(e, axis=0, keepdims=True)
        tot = jnp.sum(dl, axis=0, keepdims=True)
        dsm = []
        run = tot
        for l in range(DEPTH):
            dsm.append(run - (tot if l == 0 else 0.0))
            run = run - dl[l:l + 1, :]
        inner = sum(sm[l:l + 1, :] * dsm[l] for l in range(DEPTH))
        for l in range(DEPTH):
            o_ref[l:l + 1, :] = sm[l:l + 1, :] * (dsm[l] - inner)

    return pl.pallas_call(body, out_shape=SDS(lb.shape, F32), name="lb_bwd")(lb, dlb_all)


def _adamw(g, w, m, v):
    r, c = g.shape
    tr = _tile(r, 264)
    c1 = 1.0 / (1.0 - ADAM_B1 ** ADAM_STEP)
    c2 = 1.0 / (1.0 - ADAM_B2 ** ADAM_STEP)

    def body(g_ref, w_ref, m_ref, v_ref, d_ref, mo_ref, vo_ref):
        gg = g_ref[...]
        mn = ADAM_B1 * m_ref[...] + (1.0 - ADAM_B1) * gg
        vn = ADAM_B2 * v_ref[...] + (1.0 - ADAM_B2) * gg * gg
        d_ref[...] = -ADAM_LR * ((mn * c1) / (jnp.sqrt(vn * c2) + ADAM_EPS) + ADAM_WD * w_ref[...])
        mo_ref[...] = mn
        vo_ref[...] = vn

    spec = pl.BlockSpec((tr, c), lambda i: (i, 0))
    return pl.pallas_call(body, grid=(r // tr,), in_specs=[spec] * 4, out_specs=[spec] * 3, out_shape=[SDS(g.shape, F32)] * 3,
                          compiler_params=_params(1), name="adamw")(g, w, m, v)


def _tile16(n, target):
    return _tile(n // 2, target // 2) * 2 if n % 16 == 0 else _tile(n, target)


def _add_cores(g, got, core):
    k, r, c = got.shape
    tr = _tile16(r, 264)

    def body(c_ref, a_ref, b_ref, o_ref):
        o_ref[...] = (a_ref[...] + b_ref[...].astype(F32)).astype(o_ref.dtype)

    spec = pl.BlockSpec((None, tr, c), lambda s, i, cr: (s, i, 0))
    return pl.pallas_call(
        body, grid_spec=pltpu.PrefetchScalarGridSpec(
            num_scalar_prefetch=1, grid=(k, r // tr),
            in_specs=[pl.BlockSpec((None, None, tr, c), lambda s, i, cr: (cr[0], s, i, 0)), spec], out_specs=spec),
        out_shape=SDS(got.shape, got.dtype), compiler_params=_params(2), name="add_cores")(core, g, got)


def _sum_chips(parts, own, place):
    k, r, c = parts.shape
    tr = _tile16(r, 264)

    def body(p_ref, *refs):
        part_refs, own_ref, o_ref = refs[:k], refs[k], refs[k + 1]
        mine = own_ref[...].astype(F32)
        acc = None
        for s in range(k):
            term = jnp.where(p_ref[0] == s, mine, part_refs[s][...].astype(F32))
            acc = term if acc is None else acc + term
        o_ref[...] = acc

    slots = jnp.stack([jnp.where(place[0] == s, (s + 1) % k, s) for s in range(k)]).astype(jnp.int32)
    other = lambda s: pl.BlockSpec((None, tr, c), lambda i, p: (p[2 + s], i, 0))
    return pl.pallas_call(
        body, grid_spec=pltpu.PrefetchScalarGridSpec(
            num_scalar_prefetch=1, grid=(r // tr,),
            in_specs=[other(s) for s in range(k)] + [pl.BlockSpec((None, tr, c), lambda i, p: (p[0], i, 0))],
            out_specs=pl.BlockSpec((None, tr, c), lambda i, p: (p[1], i, 0))),
        out_shape=SDS((2, r, c), F32), compiler_params=_params(1), name="sum_chips")(
            jnp.concatenate([place, slots]), *([parts] * k), own)


def _meta_grad(dh, nb, nc):
    d = dh.shape[1]

    def body(x_ref, o_ref):
        @pl.when(pl.program_id(0) == 0)
        def _():
            o_ref[...] = jnp.zeros_like(o_ref)

        o_ref[...] += x_ref[PAD:CH, :]

    return pl.pallas_call(body, grid=(nb,), in_specs=[pl.BlockSpec((CH, d), lambda b: (b * nc, 0))],
                          out_specs=pl.BlockSpec((N_META, d), lambda b: (0, 0)), out_shape=SDS((N_META, d), F32),
                          compiler_params=_params(1), name="meta_grad")(dh)


ANY = pl.BlockSpec(memory_space=pl.ANY)


def _place():
    x, y, c = lax.axis_index("x"), lax.axis_index("y"), lax.axis_index("c")
    chips = [(1 - x, y), (x, 1 - y), (1 - x, 1 - y)]
    return x, y, c, chips


def _remote(src, dst, send_sems, recv_sems, k, to):
    return pltpu.make_async_remote_copy(src_ref=src, dst_ref=dst, send_sem=send_sems.at[k], recv_sem=recv_sems.at[k],
                                        device_id=to, device_id_type=MESH)


def _gather_weights(pbs, ps):
    nt = len(pbs)

    def body(*refs):
        pb_refs, ps_ref, gb_refs, gs_ref = refs[:nt], refs[nt], refs[nt + 1:2 * nt + 1], refs[2 * nt + 1]
        send_sems, recv_sems, local_sems = refs[2 * nt + 2:]
        x, y, c, chips = _place()
        s = 2 * x + y
        sib = (x, y, 1 - c)
        l1 = pltpu.make_async_copy(ps_ref, gs_ref.at[s], local_sems.at[0])
        l1.start()
        sends = []
        for k, (px, py) in enumerate(chips):
            for t in range(nt):
                sends.append(_remote(pb_refs[t].at[c], gb_refs[t].at[s, c], send_sems, recv_sems, 6 * t + k, (px, py, c)))
            sends.append(_remote(ps_ref, gs_ref.at[s], send_sems, recv_sems, 6 * nt + k, (px, py, c)))
        for cp in sends:
            cp.start()
        for k, (px, py) in enumerate(chips):
            sk = 2 * px + py
            for t in range(nt):
                _remote(pb_refs[t].at[c], gb_refs[t].at[sk, c], send_sems, recv_sems, 6 * t + k, sib).wait_recv()
                fwd = _remote(gb_refs[t].at[sk, c], gb_refs[t].at[sk, c], send_sems, recv_sems, 6 * t + 3 + k, sib)
                fwd.start()
                sends.append(fwd)
        for k, (px, py) in enumerate(chips):
            sk = 2 * px + py
            for t in range(nt):
                _remote(pb_refs[t].at[c], gb_refs[t].at[sk, 1 - c], send_sems, recv_sems, 6 * t + 3 + k, sib).wait_recv()
            _remote(ps_ref, gs_ref.at[sk], send_sems, recv_sems, 6 * nt + k, sib).wait_recv()
        for cp in sends:
            cp.wait_send()
        l1.wait()

    nsem = 6 * nt + 3
    out = pl.pallas_call(
        body, in_specs=[ANY] * (nt + 1), out_specs=[ANY] * (nt + 1),
        out_shape=[SDS((4,) + pb.shape, pb.dtype) for pb in pbs] + [SDS((4,) + ps.shape, ps.dtype)],
        scratch_shapes=[pltpu.SemaphoreType.DMA((nsem,)), pltpu.SemaphoreType.DMA((nsem,)), pltpu.SemaphoreType.DMA((1,))],
        name="gather_weights")(*pbs, ps)
    return out[:nt], out[nt]


def _contain(wpad, shift):
    r, cw = wpad.shape
    tr = _tile16(r, 256)

    def body(n_ref, x_ref, o_ref):
        o_ref[...] = pltpu.roll(x_ref[...], n_ref[0], axis=1).astype(o_ref.dtype)

    spec = pl.BlockSpec((tr, cw), lambda i, n: (i, 0))
    return pl.pallas_call(
        body, grid_spec=pltpu.PrefetchScalarGridSpec(num_scalar_prefetch=1, grid=(r // tr,), in_specs=[spec], out_specs=spec),
        out_shape=SDS((r, cw), BF16), compiler_params=_params(1), name="contain")(shift, wpad)


def _place_own(gb, pb, chip):
    _, _, r, c = gb.shape
    tr = _tile16(r, 1100)

    def body(s_ref, p_ref, g_in, o_ref):
        o_ref[...] = p_ref[...]

    return pl.pallas_call(
        body, grid_spec=pltpu.PrefetchScalarGridSpec(
            num_scalar_prefetch=1, grid=(2, r // tr),
            in_specs=[pl.BlockSpec((None, tr, c), lambda h, i, s: (h, i, 0)), ANY],
            out_specs=pl.BlockSpec((None, None, tr, c), lambda h, i, s: (s[0], h, i, 0))),
        out_shape=SDS(gb.shape, gb.dtype), input_output_aliases={2: 0}, compiler_params=_params(2),
        name="place_own")(chip, pb, gb)


def _sem_scratch(n_remote, n_local):
    return [pltpu.SemaphoreType.DMA((n_remote,)), pltpu.SemaphoreType.DMA((n_remote,)), pltpu.SemaphoreType.DMA((n_local,))]


def _swap_halves(sends):
    nt = len(sends)

    def body(*refs):
        s_refs, got_refs = refs[:nt], refs[nt:2 * nt]
        send_sems, recv_sems = refs[2 * nt:]
        x, y, c, _ = _place()
        sib = (x, y, 1 - c)
        remote = [_remote(s_refs[t].at[1 - c, s], got_refs[t].at[s], send_sems, recv_sems, 4 * t + s, sib)
                  for t in range(nt) for s in range(4)]
        for cp in remote:
            cp.start()
        for cp in remote:
            cp.wait()

    return pl.pallas_call(
        body, in_specs=[ANY] * nt, out_specs=[ANY] * nt, out_shape=[SDS(g.shape[1:], g.dtype) for g in sends],
        scratch_shapes=[pltpu.SemaphoreType.DMA((4 * nt,)), pltpu.SemaphoreType.DMA((4 * nt,))], name="swap_halves")(*sends)


def _scatter_chip_sums(parts):
    nt = len(parts)

    def body(*refs):
        a_refs, r_refs = refs[:nt], refs[nt:2 * nt]
        send_sems, recv_sems = refs[2 * nt:]
        x, y, c, chips = _place()
        s = 2 * x + y
        sends = [_remote(a_refs[t].at[2 * px + py], r_refs[t].at[s], send_sems, recv_sems, 3 * t + k, (px, py, c))
                 for t in range(nt) for k, (px, py) in enumerate(chips)]
        for cp in sends:
            cp.start()
        for t in range(nt):
            for k, (px, py) in enumerate(chips):
                _remote(a_refs[t].at[s], r_refs[t].at[2 * px + py], send_sems, recv_sems, 3 * t + k, (px, py, c)).wait_recv()
        for cp in sends:
            cp.wait_send()

    return pl.pallas_call(
        body, in_specs=[ANY] * nt, out_specs=[ANY] * nt, out_shape=[SDS(a.shape, a.dtype) for a in parts],
        scratch_shapes=[pltpu.SemaphoreType.DMA((3 * nt,)), pltpu.SemaphoreType.DMA((3 * nt,))],
        name="scatter_chip_sums")(*parts)


def _join_halves(fs):
    nt = len(fs)

    def body(*refs):
        f_refs = refs[nt:2 * nt]
        send_sems, recv_sems = refs[2 * nt:]
        x, y, c, _ = _place()
        sib = (x, y, 1 - c)
        sends = [_remote(f_refs[t].at[c], f_refs[t].at[c], send_sems, recv_sems, t, sib) for t in range(nt)]
        for cp in sends:
            cp.start()
        for t in range(nt):
            _remote(f_refs[t].at[c], f_refs[t].at[1 - c], send_sems, recv_sems, t, sib).wait_recv()
        for cp in sends:
            cp.wait_send()

    return pl.pallas_call(
        body, in_specs=[ANY] * nt, out_specs=[ANY] * nt, out_shape=[SDS(f.shape, f.dtype) for f in fs],
        input_output_aliases={t: t for t in range(nt)},
        scratch_shapes=[pltpu.SemaphoreType.DMA((nt,)), pltpu.SemaphoreType.DMA((nt,))], name="join_halves")(*fs)


def _uncontain(cont, n_head, width):
    r, cw = cont.shape
    tr = _tile(r, 256)

    def body(n_ref, x_ref, o_ref):
        o_ref[...] = pltpu.roll(x_ref[...], n_ref[0], axis=1)[:, :width]

    return pl.pallas_call(
        body, grid_spec=pltpu.PrefetchScalarGridSpec(
            num_scalar_prefetch=1, grid=(r // tr,), in_specs=[pl.BlockSpec((tr, cw), lambda i, n: (i, 0))],
            out_specs=pl.BlockSpec((tr, width), lambda i, n: (i, 0))),
        out_shape=SDS((r, width), F32), compiler_params=_params(1), name="uncontain")(n_head, cont)


WEIGHTS = ("meta_tokens", "norm_w", "w_in", "conv_w", "a_log", "dt_bias", "gnorm_a", "gnorm_b", "hgrn_lower_bounds",
           "w_branch_a", "w_branch_b", "w_out", "final_norm_w")
SHARD_AXIS = {"meta_tokens": 1, "w_in": 2, "conv_w": 2, "w_branch_a": 2, "w_branch_b": 2, "w_out": 1}
FLAT_C = 1024


def _flat(parts, rows, cols=FLAT_C):
    v = jnp.concatenate([p.reshape(-1) for p in parts])
    return jnp.pad(v, (0, rows * cols - v.shape[0])).reshape(rows, cols)


def _local_step(x, target, w, lay):
    nb, seq, d = x.shape
    tp = CH + seq
    nc = tp // CH
    n = nb * tp
    e_mat, s_mat = _gate_consts()
    lb_all = _lb_fwd(w["hgrn_lower_bounds"])
    h = jnp.concatenate([jnp.zeros((nb, PAD, d), F32), jnp.broadcast_to(w["meta_tokens"][None], (nb, N_META, d)), x],
                        axis=1).reshape(n, d)
    rep = lambda a: jnp.repeat(a, HD)[None, :]
    saved = []
    for l in range(DEPTH):
        nw = w["norm_w"][l][None, :]
        proj, xn = _norm_proj_fwd(h, nw, w["w_in"][l])
        qkv = _gdn_prep_fwd(proj, w["conv_w"][l], lay, nb, tp)
        alog, dtb = rep(w["a_log"][l]), rep(w["dt_bias"][l])
        oa, sa, sva = _gdn_fwd(qkv, proj, e_mat, alog, dtb, lay, nb, nc)
        lbl = lb_all[l][None, :]
        ob, sb = _hgrn_fwd(proj, lbl, lay, nb, nc)
        ga, gb = w["gnorm_a"][l][None, :], w["gnorm_b"][l][None, :]
        hn = _merge_fwd(h, oa, ob, proj, ga, gb, w["w_branch_a"][l], w["w_branch_b"][l], w["w_out"][l], lay)
        saved.append((h, nw, proj, qkv, alog, dtb, oa, sa, lbl, ob, sb, ga, gb, xn, sva))
        h = hn
    lp, dh, dfw = _loss_head(h, target.reshape(nb * seq, d), w["final_norm_w"][None, :], nb, nc)
    loss = jnp.sum(lp[::8, 0])
    g = {n_: [None] * DEPTH for n_ in WEIGHTS}
    dlb_all = [None] * DEPTH
    for l in reversed(range(DEPTH)):
        h, nw, proj, qkv, alog, dtb, oa, sa, lbl, ob, sb, ga, gb, xn, sva = saved[l]
        dproj, doa, dob, dwa, dwb, dwo, dga, dgb = _merge_bwd(dh, oa, ob, proj, ga, gb, w["w_branch_a"][l],
                                                             w["w_branch_b"][l], w["w_out"][l], lay, tp)
        dproj, acc_b = _hgrn_bwd(proj, lbl, sb, dob, dproj, lay, nb, nc)
        dqkv, dproj, acc_a = _gdn_bwd(qkv, proj, e_mat, s_mat, alog, dtb, sa, sva, doa, dproj, lay, nb, nc)
        dproj, dconv = _gdn_prep_bwd(proj, w["conv_w"][l], dqkv, dproj, lay, nb, tp)
        dh, dnw = _proj_bwd_dx(dproj, w["w_in"][l], h, nw, dh, tp)
        g["w_in"][l] = _proj_bwd_dw(dproj, xn, tp)
        g["norm_w"][l] = dnw[0]
        g["conv_w"][l] = dconv
        g["a_log"][l] = acc_a[0, ::HD]
        g["dt_bias"][l] = acc_a[1, ::HD]
        g["gnorm_a"][l], g["gnorm_b"][l] = dga[0], dgb[0]
        g["w_branch_a"][l], g["w_branch_b"][l], g["w_out"][l] = dwa, dwb, dwo
        dlb_all[l] = acc_b[0]
    grads = {n_: jnp.stack(v) for n_, v in g.items() if v[0] is not None}
    grads["hgrn_lower_bounds"] = _lb_bwd(w["hgrn_lower_bounds"], jnp.stack(dlb_all))
    grads["final_norm_w"] = dfw[0]
    grads["meta_tokens"] = _meta_grad(dh, nb, nc)
    grad_x = dh.reshape(nb, tp, d)[:, CH:, :]
    return loss, grad_x, grads


def kernel(x, meta_tokens, norm_w, w_in, conv_w, a_log, dt_bias, gnorm_a, gnorm_b, hgrn_lower_bounds, w_branch_a, w_branch_b, w_out, final_norm_w, loss_target, m_meta_tokens, m_norm_w, m_w_in, m_conv_w, m_a_log, m_dt_bias, m_gnorm_a, m_gnorm_b, m_hgrn_lower_bounds, m_w_branch_a, m_w_branch_b, m_w_out, m_final_norm_w, v_meta_tokens, v_norm_w, v_w_in, v_conv_w, v_a_log, v_dt_bias, v_gnorm_a, v_gnorm_b, v_hgrn_lower_bounds, v_w_branch_a, v_w_branch_b, v_w_out, v_final_norm_w):
    wl = dict(meta_tokens=meta_tokens, norm_w=norm_w, w_in=w_in, conv_w=conv_w, a_log=a_log, dt_bias=dt_bias, gnorm_a=gnorm_a,
              gnorm_b=gnorm_b, hgrn_lower_bounds=hgrn_lower_bounds, w_branch_a=w_branch_a, w_branch_b=w_branch_b, w_out=w_out,
              final_norm_w=final_norm_w)
    ml = dict(zip(WEIGHTS, (m_meta_tokens, m_norm_w, m_w_in, m_conv_w, m_a_log, m_dt_bias, m_gnorm_a, m_gnorm_b,
                            m_hgrn_lower_bounds, m_w_branch_a, m_w_branch_b, m_w_out, m_final_norm_w)))
    vl = dict(zip(WEIGHTS, (v_meta_tokens, v_norm_w, v_w_in, v_conv_w, v_a_log, v_dt_bias, v_gnorm_a, v_gnorm_b,
                            v_hgrn_lower_bounds, v_w_branch_a, v_w_branch_b, v_w_out, v_final_norm_w)))
    d = x.shape[2]
    lay = _Layout(d)
    nchip = 4

    big = ("w_in", "w_branch_a", "w_branch_b", "w_out")
    small = ("conv_w", "meta_tokens")
    table, heads, cw = lay.pieces(nchip)
    sw = wl["w_in"].shape[2]
    chip_id = (2 * lax.axis_index("x") + lax.axis_index("y")).astype(jnp.int32)
    n_head = sum(jnp.where(chip_id == s, heads[s], 0) for s in range(nchip)).astype(jnp.int32)
    w_pad = jnp.pad(wl["w_in"], ((0, 0), (0, 0), (0, cw - sw))).reshape(DEPTH * d, cw)
    shift = jnp.where(n_head == 0, 0, cw - n_head).astype(jnp.int32).reshape(1)
    pbs = [_contain(w_pad, shift).reshape(DEPTH, d, cw)] + [wl[n].astype(BF16) for n in big[1:]]
    nsmall = sum(int(np.prod(wl[n].shape)) for n in small)
    rs = -(-nsmall // (HD * 8)) * 8
    ps = jnp.pad(jnp.concatenate([wl[n].reshape(-1) for n in small]), (0, rs * HD - nsmall)).reshape(rs, HD)
    gbig, gsmall = _gather_weights(pbs, ps)
    gbig = [_place_own(g, p, chip_id.reshape(1)) for g, p in zip(gbig, pbs)]
    gsmall = gsmall.reshape(nchip, -1)

    wf = dict(wl)
    wf["w_in"] = lay.from_containers([gbig[0][s] for s in range(nchip)])
    for i, n in enumerate(big[1:], start=1):
        wf[n] = jnp.concatenate([gbig[i][s] for s in range(nchip)], axis=SHARD_AXIS[n])
    o = 0
    for n in small:
        sz = int(np.prod(wl[n].shape))
        a = gsmall[:, o:o + sz].reshape((nchip,) + wl[n].shape)
        wf[n] = jnp.concatenate([a[s] for s in range(nchip)], axis=SHARD_AXIS[n])
        o += sz

    loss_part, grad_x, gfull = _local_step(x, loss_target, wf, lay)
    loss = lax.psum(loss_part, ("x", "y", "c"))

    sw = wl["w_in"].shape[2]
    conts, heads = lay.containers(gfull["w_in"], nchip)
    dd = wl["w_branch_a"].shape[2]
    rows_o = wl["w_out"].shape[1]
    by_dest = lambda g, n: [lax.slice_in_dim(g, s * wl[n].shape[SHARD_AXIS[n]], (s + 1) * wl[n].shape[SHARD_AXIS[n]],
                                            axis=SHARD_AXIS[n]) if n in SHARD_AXIS else g for s in range(nchip)]
    small_names = tuple(n for n in WEIGHTS if n not in big)
    nsm = sum(int(np.prod(wl[n].shape)) for n in small_names)
    rsm = -(-nsm // (2 * HD * 8)) * 8
    pack_small = lambda parts: _flat(parts, 2 * rsm, HD).reshape(2, rsm, HD)
    small_by_dest = [by_dest(gfull[n], n) for n in small_names]
    gs = [jnp.stack(conts, axis=1),
          jnp.stack(by_dest(gfull["w_branch_a"], "w_branch_a"), axis=1),
          jnp.stack(by_dest(gfull["w_branch_b"], "w_branch_b"), axis=1),
          gfull["w_out"].reshape(DEPTH, nchip, rows_o, d),
          jnp.stack([pack_small([p[s] for p in small_by_dest]) for s in range(nchip)], axis=1)]
    gs = [g.reshape((2, nchip, -1, g.shape[-1])) for g in gs]
    my_chip = (2 * lax.axis_index("x") + lax.axis_index("y")).astype(jnp.int32)
    my_core = lax.axis_index("c").astype(jnp.int32)
    got = _swap_halves([g.astype(BF16) for g in gs[:4]] + gs[4:])
    chip_sums = [_add_cores(g, b, my_core.reshape(1)) for g, b in zip(gs, got)]
    by_chip = _scatter_chip_sums(chip_sums)
    place = jnp.stack([my_chip, my_core])
    full = _join_halves([_sum_chips(p, a, place) for p, a in zip(by_chip, chip_sums)])
    n_head = sum(jnp.where(my_chip == s, heads[s], 0) for s in range(nchip)).astype(jnp.int32).reshape(1)
    g_w_in = _uncontain(full[0].reshape(DEPTH * d, -1), n_head, sw)
    g2 = {"w_in": g_w_in, "w_branch_a": full[1].reshape(-1, dd), "w_branch_b": full[2].reshape(-1, dd),
          "w_out": full[3].reshape(-1, d), "small": full[4].reshape(2 * rsm, HD)}

    def two_d(src, n):
        if n == "small":
            return _flat([src[k] for k in small_names], 2 * rsm, HD)
        return src[n].reshape(g2[n].shape)

    outs = {}
    for n in big + ("small",):
        delta, mnew, vnew = _adamw(g2[n], two_d(wl, n), two_d(ml, n), two_d(vl, n))
        outs[n] = (g2[n], delta, mnew, vnew)
    res = [{}, {}, {}, {}]
    for i in range(4):
        for n in big:
            res[i][n] = outs[n][i].reshape(wl[n].shape)
        v, o = outs["small"][i].reshape(-1), 0
        for n in small_names:
            sz = int(np.prod(wl[n].shape))
            res[i][n] = v[o:o + sz].reshape(wl[n].shape)
            o += sz
    return (loss, grad_x, *[res[0][n] for n in WEIGHTS], *[res[1][n] for n in WEIGHTS], *[res[2][n] for n in WEIGHTS],
            *[res[3][n] for n in WEIGHTS])
```

```python
import functools

import numpy as np
import jax
import jax.numpy as jnp
from jax import lax
from jax.experimental import pallas as pl
from jax.experimental.pallas import tpu as pltpu

F32 = jnp.float32
BF16 = jnp.bfloat16
HI = lax.Precision.HIGHEST
SDS = jax.ShapeDtypeStruct

NH = 4
HD = 128
HW = NH * HD
N_META = 16
CH = 64
SUB = 16
PAD = CH - N_META
EPS = 1e-6
Q_SCALE = HD ** -0.5
DEPTH = 2
CONV_K = 4
VMEM_LIMIT = 56 * 1024 * 1024
ADAM_LR, ADAM_B1, ADAM_B2, ADAM_EPS, ADAM_WD, ADAM_STEP = 0.001, 0.9, 0.999, 1e-08, 0.01, 10
MESH = pl.DeviceIdType.MESH


def _nn(a, b):
    return jnp.dot(a, b, precision=HI, preferred_element_type=F32)


def _nt(a, b):
    return lax.dot_general(a, b, (((1,), (1,)), ((), ())), precision=HI, preferred_element_type=F32)


def _tn(a, b):
    return _nn(a.T, b)


def _scan_rows(x, group, reverse=False):
    n = x.shape[0]
    pos = lax.bitwise_and(_iota2(x.shape, 0), group - 1)
    s = 1
    while s < group:
        if reverse:
            x = x + jnp.where(pos < group - s, pltpu.roll(x, n - s, axis=0), 0.0)
        else:
            x = x + jnp.where(pos >= s, pltpu.roll(x, s, axis=0), 0.0)
        s *= 2
    return x


def _bnn(a, b):
    return jnp.dot(a.astype(BF16), b.astype(BF16), preferred_element_type=F32)


def _bnt(a, b):
    return lax.dot_general(a.astype(BF16), b.astype(BF16), (((1,), (1,)), ((), ())), preferred_element_type=F32)


def _btn(a, b):
    return lax.dot_general(a.astype(BF16), b.astype(BF16), (((0,), (0,)), ((), ())), preferred_element_type=F32)


def _hi_lo(x):
    hi = x.astype(jnp.bfloat16)
    return hi, (x - hi.astype(F32)).astype(jnp.bfloat16)


def _dot3(dims):
    def f(a, b):
        ah, al = _hi_lo(a)
        bh, bl = _hi_lo(b)
        d = lambda p, q: lax.dot_general(p, q, (dims, ((), ())), preferred_element_type=F32)
        return d(ah, bh) + (d(ah, bl) + d(al, bh))
    return f


_rnn, _rnt, _rtn = _dot3(((1,), (0,))), _dot3(((1,), (1,))), _dot3(((0,), (0,)))
_enn, _ent, _etn = _bnn, _bnt, _btn
_hnn, _hnt, _htn = _bnn, _bnt, _btn


def _rr(x):
    return x


def _sig(x):
    return jax.nn.sigmoid(x)


def _silu(x):
    return x * _sig(x)


def _dsilu(x):
    s = _sig(x)
    return s * (1.0 + x * (1.0 - s))


def _softplus(x):
    return jnp.maximum(x, 0.0) + jnp.log(1.0 + jnp.exp(-jnp.abs(x)))


def _logsig(x):
    return jnp.minimum(x, 0.0) - jnp.log(1.0 + jnp.exp(-jnp.abs(x)))


def _rs(x):
    return jnp.sum(x, axis=-1, keepdims=True)


def _params(n_axes):
    return pltpu.CompilerParams(dimension_semantics=("arbitrary",) * n_axes, vmem_limit_bytes=VMEM_LIMIT)


def _tile(n, target):
    best = 8
    for t in range(8, target + 1, 8):
        if n % t == 0:
            best = t
    return best


def _ctile(pw, most=7):
    return HD * max(k for k in range(1, most + 1) if (pw // HD) % k == 0)


def _iota2(shape, axis):
    return lax.broadcasted_iota(jnp.int32, shape, axis)


class _Layout:
    def __init__(self, d):
        self.d = d
        self.wm = 2 * HW + 2 * d
        self.c_qkv = self.wm
        self.c_b = self.wm + 3 * HW
        self.c_ba = self.wm + 6 * HW
        self.pw = self.c_ba + HD
        assert self.c_b % (3 * HW) == 0
        o = 0
        segs = {}
        for name, w in (("a_q", HW), ("a_k", HW), ("a_v", HW), ("ba", 2 * NH), ("a_z", HW), ("b_q", HW), ("b_f", HW),
                        ("b_i", HW), ("b_g", HW), ("gate_a", d), ("gate_b", d)):
            segs[name] = (o, o + w)
            o += w
        self.segs = segs
        self.width = o
        self.order = ("a_z", "b_g", "gate_a", "gate_b", "a_q", "a_k", "a_v", "b_q", "b_f", "b_i", "ba")

    def to_kernel(self, w):
        parts = [w[..., self.segs[n][0]:self.segs[n][1]] for n in self.order]
        parts.append(jnp.zeros(w.shape[:-1] + (HD - 2 * NH,), w.dtype))
        return jnp.concatenate(parts, axis=-1)

    def containers(self, g, nchip):
        table, heads, cw = self.pieces(nchip)
        out = []
        for s in range(nchip):
            parts, at = [], 0
            for kcol, w, ccol in sorted(table[s], key=lambda p: p[2]):
                if ccol > at:
                    parts.append(jnp.zeros(g.shape[:-1] + (ccol - at,), g.dtype))
                parts.append(g[..., kcol:kcol + w])
                at = ccol + w
            if at < cw:
                parts.append(jnp.zeros(g.shape[:-1] + (cw - at,), g.dtype))
            out.append(jnp.concatenate(parts, axis=-1))
        return out, heads

    def pieces(self, nchip):
        off, where = 0, {}
        for n in self.order:
            where[n] = off
            off += self.segs[n][1] - self.segs[n][0]
        names = sorted(self.segs, key=lambda n: self.segs[n][0])
        sw = self.width // nchip
        cw = -(-sw // HD) * HD
        table, heads = [], []
        for s in range(nchip):
            lo, hi = s * sw, (s + 1) * sw
            pieces = []
            for n in names:
                a, b = max(lo, self.segs[n][0]), min(hi, self.segs[n][1])
                if a < b:
                    pieces.append((where[n] + a - self.segs[n][0], b - a))
            start, width = pieces[0]
            n_head = min((-start) % HD, width)
            body = ([(start + n_head, width - n_head)] if width > n_head else []) + pieces[1:]
            rows, at = [], 0
            for c, w in body:
                rows.append((c, w, at))
                at += w
            if n_head:
                rows.append((start, n_head, cw - n_head))
            table.append(rows)
            heads.append(n_head)
        return table, heads, cw

    def from_containers(self, conts):
        table, _, _ = self.pieces(len(conts))
        cut = sorted((kcol, w, s, ccol) for s, rows in enumerate(table) for kcol, w, ccol in rows)
        parts, at = [], 0
        for kcol, w, s, ccol in cut:
            assert kcol == at, (kcol, at)
            parts.append(conts[s][..., ccol:ccol + w])
            at = kcol + w
        parts.append(jnp.zeros(conts[0].shape[:-1] + (self.pw - at,), conts[0].dtype))
        return jnp.concatenate(parts, axis=-1)

    def from_kernel(self, g):
        off, where = 0, {}
        for n in self.order:
            w = self.segs[n][1] - self.segs[n][0]
            where[n] = (off, off + w)
            off += w
        names = sorted(self.segs, key=lambda n: self.segs[n][0])
        return jnp.concatenate([g[..., where[n][0]:where[n][1]] for n in names], axis=-1)


def _norm_proj_fwd(h, nw, wp):
    n, d = h.shape
    pw = wp.shape[1]
    tm, tn = _tile16(n, 768), _ctile(pw)

    def body(h_ref, nw_ref, w_ref, o_ref, xn_ref):
        @pl.when(pl.program_id(1) == 0)
        def _():
            x = h_ref[...]
            r = lax.rsqrt(jnp.mean(x * x, axis=-1, keepdims=True) + EPS)
            xn_ref[...] = (x * r * nw_ref[...]).astype(BF16)

        o_ref[...] = jnp.dot(xn_ref[...], w_ref[...], preferred_element_type=F32)

    return pl.pallas_call(
        body, grid=(n // tm, pw // tn),
        in_specs=[pl.BlockSpec((tm, d), lambda i, j: (i, 0)), pl.BlockSpec((1, d), lambda i, j: (0, 0)),
                  pl.BlockSpec((d, tn), lambda i, j: (0, j))],
        out_specs=[pl.BlockSpec((tm, tn), lambda i, j: (i, j)), pl.BlockSpec((tm, d), lambda i, j: (i, 0))],
        out_shape=[SDS((n, pw), F32), SDS((n, d), BF16)], compiler_params=_params(2), name="norm_proj_fwd")(h, nw, wp)


def _row_valid(tm, tp, base):
    row = base + _iota2((tm, 1), 0)
    return lax.rem(row, tp) >= PAD


def _proj_bwd_dx(dproj, wp, h, nw, dhn, tp):
    n, d = h.shape
    pw = wp.shape[1]
    tm, tk = _tile16(n, 768), _ctile(pw)
    nk = pw // tk

    def body(dp_ref, w_ref, h_ref, nw_ref, dhn_ref, dh_ref, dnw_ref, acc_ref):
        i, k = pl.program_id(0), pl.program_id(1)

        @pl.when(k == 0)
        def _():
            acc_ref[...] = jnp.zeros_like(acc_ref)

        @pl.when((i == 0) & (k == 0))
        def _():
            dnw_ref[...] = jnp.zeros_like(dnw_ref)

        valid = _row_valid(tm, tp, i * tm)
        dp = jnp.where(valid, dp_ref[...], 0.0)
        acc_ref[...] += _bnt(dp, w_ref[...])

        @pl.when(k == nk - 1)
        def _():
            x = h_ref[...]
            r = lax.rsqrt(jnp.mean(x * x, axis=-1, keepdims=True) + EPS)
            xh = x * r
            dxn = acc_ref[...]
            dnw_ref[...] += jnp.sum(dxn * xh, axis=0, keepdims=True)
            dxh = dxn * nw_ref[...]
            dh_ref[...] = dhn_ref[...] + r * (dxh - xh * jnp.mean(dxh * xh, axis=-1, keepdims=True))

    return pl.pallas_call(
        body, grid=(n // tm, nk),
        in_specs=[pl.BlockSpec((tm, tk), lambda i, k: (i, k)), pl.BlockSpec((d, tk), lambda i, k: (0, k)),
                  pl.BlockSpec((tm, d), lambda i, k: (i, 0)), pl.BlockSpec((1, d), lambda i, k: (0, 0)),
                  pl.BlockSpec((tm, d), lambda i, k: (i, 0))],
        out_specs=[pl.BlockSpec((tm, d), lambda i, k: (i, 0)), pl.BlockSpec((1, d), lambda i, k: (0, 0))],
        out_shape=[SDS((n, d), F32), SDS((1, d), F32)],
        scratch_shapes=[pltpu.VMEM((tm, d), F32)], compiler_params=_params(2), name="proj_bwd_dx")(dproj, wp, h, nw, dhn)


def _proj_bwd_dw(dproj, xn, tp):
    n, d = xn.shape
    pw = dproj.shape[1]
    tm, tn = _tile16(n, 768), _ctile(pw)

    def body(dp_ref, xn_ref, dw_ref):
        i = pl.program_id(1)

        @pl.when(i == 0)
        def _():
            dw_ref[...] = jnp.zeros_like(dw_ref)

        dp = jnp.where(_row_valid(tm, tp, i * tm), dp_ref[...], 0.0)
        dw_ref[...] += _btn(xn_ref[...], dp)

    return pl.pallas_call(
        body, grid=(pw // tn, n // tm),
        in_specs=[pl.BlockSpec((tm, tn), lambda j, i: (i, j)), pl.BlockSpec((tm, d), lambda j, i: (i, 0))],
        out_specs=pl.BlockSpec((d, tn), lambda j, i: (0, j)), out_shape=SDS((d, pw), F32),
        compiler_params=_params(2), name="proj_bwd_dw")(dproj, xn)


def _conv_silu(x, w, row):
    c = x * w[CONV_K - 1:CONV_K, :]
    for k in range(1, CONV_K):
        c = c + jnp.where(row >= k, pltpu.roll(x, k, axis=0), 0.0) * w[CONV_K - 1 - k:CONV_K - k, :]
    return c


def _gdn_prep_fwd(proj, conv_w, lay, nb, tp):
    n = proj.shape[0]
    nblk = 3 * NH
    cb = lay.c_qkv // HD

    def body(p_ref, w_ref, o_ref):
        j = pl.program_id(1)
        x = p_ref[...]
        row = _iota2(x.shape, 0)
        c = _conv_silu(x, w_ref[...], row)
        s = _silu(c)
        r = lax.rsqrt(_rs(s * s) + EPS)
        scale = jnp.where(j < NH, Q_SCALE, 1.0)
        y = jnp.where(j < 2 * NH, s * r * scale, s)
        o_ref[...] = jnp.where(row >= PAD, y, 0.0)

    return pl.pallas_call(
        body, grid=(nb, nblk),
        in_specs=[pl.BlockSpec((tp, HD), lambda b, j: (b, cb + j)), pl.BlockSpec((CONV_K, HD), lambda b, j: (0, j))],
        out_specs=pl.BlockSpec((tp, HD), lambda b, j: (b, j)), out_shape=SDS((n, nblk * HD), F32),
        compiler_params=_params(2), name="gdn_prep_fwd")(proj, conv_w)


def _gdn_prep_bwd(proj, conv_w, dqkv, dproj, lay, nb, tp):
    nblk = 3 * NH
    cb = lay.c_qkv // HD

    def body(p_ref, w_ref, dy_ref, dp_in, dp_ref, dw_ref):
        j, b = pl.program_id(0), pl.program_id(1)
        x = p_ref[...]
        w = w_ref[...]
        row = _iota2(x.shape, 0)
        c = _conv_silu(x, w, row)
        s = _silu(c)
        dy = jnp.where(row >= PAD, dy_ref[...], 0.0)
        r = lax.rsqrt(_rs(s * s) + EPS)
        nh = s * r
        scale = jnp.where(j < NH, Q_SCALE, 1.0)
        ds_n = scale * r * (dy - nh * _rs(dy * nh))
        ds = jnp.where(j < 2 * NH, ds_n, dy)
        dc = ds * _dsilu(c)
        dx = dc * w[CONV_K - 1:CONV_K, :]
        dws = [jnp.sum(dc * x, axis=0, keepdims=True)]
        for k in range(1, CONV_K):
            dx = dx + jnp.where(row < tp - k, pltpu.roll(dc, tp - k, axis=0), 0.0) * w[CONV_K - 1 - k:CONV_K - k, :]
            xs = jnp.where(row >= k, pltpu.roll(x, k, axis=0), 0.0)
            dws.append(jnp.sum(dc * xs, axis=0, keepdims=True))
        dp_ref[...] = dx.astype(dp_ref.dtype)
        r4 = _iota2((CONV_K, HD), 0)
        dw = jnp.zeros((CONV_K, HD), F32)
        for k in range(CONV_K):
            dw = dw + jnp.where(r4 == CONV_K - 1 - k, dws[k], 0.0)

        @pl.when(b == 0)
        def _():
            dw_ref[...] = dw

        @pl.when(b > 0)
        def _():
            dw_ref[...] += dw

    return pl.pallas_call(
        body, grid=(nblk, nb),
        in_specs=[pl.BlockSpec((tp, HD), lambda j, b: (b, cb + j)), pl.BlockSpec((CONV_K, HD), lambda j, b: (0, j)),
                  pl.BlockSpec((tp, HD), lambda j, b: (b, j)), pl.BlockSpec(memory_space=pl.ANY)],
        out_specs=[pl.BlockSpec((tp, HD), lambda j, b: (b, cb + j)), pl.BlockSpec((CONV_K, HD), lambda j, b: (0, j))],
        out_shape=[SDS(dproj.shape, dproj.dtype), SDS((CONV_K, nblk * HD), F32)],
        input_output_aliases={3: 0}, compiler_params=_params(2), name="gdn_prep_bwd")(proj, conv_w, dqkv, dproj)


def _gate_consts():
    e = np.zeros((HD, 2 * HW), np.float32)
    s = np.zeros((2 * HW, HD), np.float32)
    for h in range(NH):
        e[h, h * HD:(h + 1) * HD] = 1.0
        e[NH + h, HW + h * HD:HW + (h + 1) * HD] = 1.0
        s[h * HD, h] = 1.0
        s[HW + h * HD, NH + h] = 1.0
    return jnp.asarray(e), jnp.asarray(s)


def _gdn_tri():
    i, j = _iota2((CH, CH), 0), _iota2((CH, CH), 1)
    return i >= j, i > j


def _each(fn, *lists):
    return [fn(*xs) for xs in zip(*lists)]


def _tri_inv(a_list, eye):
    p = [-a for a in a_list]
    t = [eye + x for x in p]
    for _ in range(5):
        p = _each(_rnn, p, p)
        tp_ = _each(_rnn, t, p)
        t = _each(lambda x, y: x + y, t, tp_)
    return t


def _gdn_chunks(args, solved=None):
    causal, strict = _gdn_tri()
    eye = jnp.where(_iota2((CH, CH), 0) == _iota2((CH, CH), 1), 1.0, 0.0)
    q, k, v, beta, g, s0 = (list(t) for t in zip(*args))
    gc = [_scan_rows(x, CH) for x in g]
    dm = [jnp.where(causal, jnp.exp(jnp.where(causal, x[:, :CH] - x[:, :CH].T, 0.0)), 0.0) for x in gc]
    ds = [jnp.where(strict, x, 0.0) for x in dm]
    kb = _each(lambda x, y: x * y, k, beta)
    kk = _each(_ent, kb, k)
    a = _each(lambda x, y: x * y, kk, ds)
    eg = [jnp.exp(x) for x in gc]
    rw = _each(lambda x, y: x * y, kb, eg)
    if solved is None:
        tinv = _tri_inv(a, eye)
        rv = _each(lambda x, y: x * y, v, beta)
        u = _each(_rnn, tinv, rv)
        w = _each(_rnn, tinv, rw)
    else:
        tinv, u, w = (list(t) for t in zip(*solved))
    ws = _each(_enn, w, s0)
    vn = _each(lambda x, y: x - y, u, ws)
    qk = _each(_ent, q, k)
    p = _each(lambda x, y: x * y, qk, dm)
    qg = _each(lambda x, y: x * y, q, eg)
    out = []
    for i in range(len(args)):
        gl = gc[i][CH - 1:CH, :]
        ek = jnp.exp(gl - gc[i])
        out.append(dict(gc=gc[i], dm=dm[i], ds=ds[i], kb=kb[i], a=a[i], tinv=tinv[i], eg=eg[i], rw=rw[i], u=u[i], w=w[i],
                        vn=vn[i], p=p[i], qg=qg[i], egl=jnp.exp(gl), ek=ek, kd=k[i] * ek))
    return out


def _gdn_gates(ba, e, alog, dtb):
    raw = _nn(ba, e)
    beta = _sig(raw[:, :HW])
    za = raw[:, HW:] + dtb
    g = -jnp.exp(alog) * _softplus(za)
    return beta, g, za


def _seqs_per_step(nb):
    return 4 if nb % 4 == 0 else (2 if nb % 2 == 0 else 1)


def _gdn_fwd(qkv, proj, e_mat, alog, dtb, lay, nb, nc):
    n = qkv.shape[0]
    tp = n // nb
    cba = lay.c_ba // HD
    gb = _seqs_per_step(nb)

    def body(x_ref, ba_ref, e_ref, al_ref, dt_ref, o_ref, so_ref, sv_ref, s_ref):
        @pl.when(pl.program_id(1) == 0)
        def _():
            s_ref[...] = jnp.zeros_like(s_ref)

        args = []
        for j in range(gb):
            beta, g, _ = _gdn_gates(ba_ref[j], e_ref[...], al_ref[...], dt_ref[...])
            for h in range(NH):
                hs = slice(h * HD, (h + 1) * HD)
                args.append((x_ref[j, :, hs], x_ref[j, :, HW + h * HD:HW + (h + 1) * HD],
                             x_ref[j, :, 2 * HW + h * HD:2 * HW + (h + 1) * HD], beta[:, hs], g[:, hs], s_ref[j, h]))
        cs = _gdn_chunks(args)
        s0s = [a[5] for a in args]
        o1 = _each(lambda c, s0: _enn(c["qg"], s0), cs, s0s)
        o2 = [_enn(c["p"], c["vn"]) for c in cs]
        upd = [_etn(c["kd"], c["vn"]) for c in cs]
        res = [(o1[i] + o2[i], s0s[i] * cs[i]["egl"] + upd[i]) for i in range(len(cs))]
        zero = jnp.zeros((CH, HD - CH), F32)
        for j in range(gb):
            for h in range(NH):
                c = cs[j * NH + h]
                so_ref[j, h] = args[j * NH + h][5]
                sv_ref[j, h] = jnp.concatenate([c["u"], c["w"], c["tinv"], zero], axis=-1)
                s_ref[j, h] = res[j * NH + h][1]
            o_ref[j] = jnp.concatenate([res[j * NH + h][0] for h in range(NH)], axis=-1)

    o, st, sv = pl.pallas_call(
        body, grid=(nb // gb, nc),
        in_specs=[pl.BlockSpec((gb, CH, 3 * HW), lambda b, c: (b, c, 0)), pl.BlockSpec((gb, CH, HD), lambda b, c: (b, c, cba)),
                  pl.BlockSpec((HD, 2 * HW), lambda b, c: (0, 0)), pl.BlockSpec((1, HW), lambda b, c: (0, 0)),
                  pl.BlockSpec((1, HW), lambda b, c: (0, 0))],
        out_specs=[pl.BlockSpec((gb, CH, HW), lambda b, c: (b, c, 0)),
                   pl.BlockSpec((gb, None, NH, HD, HD), lambda b, c: (b, c, 0, 0, 0)),
                   pl.BlockSpec((gb, None, NH, CH, 3 * HD), lambda b, c: (b, c, 0, 0, 0))],
        out_shape=[SDS((nb, tp, HW), F32), SDS((nb, nc, NH, HD, HD), F32), SDS((nb, nc, NH, CH, 3 * HD), F32)],
        scratch_shapes=[pltpu.VMEM((gb, NH, HD, HD), F32)], compiler_params=_params(2), name="gdn_fwd")(
            qkv.reshape(nb, tp, 3 * HW), proj.reshape(nb, tp, -1), e_mat, alog, dtb)
    return o.reshape(n, HW), st, sv


def _gdn_bwd(qkv, proj, e_mat, s_mat, alog, dtb, states, solved, do, dproj, lay, nb, nc):
    n = qkv.shape[0]
    tp = n // nb
    cba = lay.c_ba // HD
    gb = _seqs_per_step(nb)

    def body(x_ref, ba_ref, e_ref, sm_ref, al_ref, dt_ref, st_ref, sv_ref, do_ref, dp_in, dx_ref, dba_ref, acc_ref, ds_ref):
        ci = pl.program_id(1)

        @pl.when(ci == 0)
        def _():
            ds_ref[...] = jnp.zeros_like(ds_ref)

        @pl.when((ci == 0) & (pl.program_id(0) == 0))
        def _():
            acc_ref[...] = jnp.zeros_like(acc_ref)

        causal, strict = _gdn_tri()
        alog = al_ref[...]
        row = _iota2((CH, 1), 0)
        valid = (row >= PAD) | (ci < nc - 1)
        last = row == CH - 1
        gates = [_gdn_gates(ba_ref[j], e_ref[...], alog, dt_ref[...]) for j in range(gb)]
        args, do, ds1, solved = [], [], [], []
        for j in range(gb):
            beta, g, _ = gates[j]
            for h in range(NH):
                hs = slice(h * HD, (h + 1) * HD)
                args.append((x_ref[j, :, hs], x_ref[j, :, HW + h * HD:HW + (h + 1) * HD],
                             x_ref[j, :, 2 * HW + h * HD:2 * HW + (h + 1) * HD], beta[:, hs], g[:, hs], st_ref[j, h]))
                do.append(do_ref[j, :, hs])
                ds1.append(ds_ref[j, h])
                solved.append((sv_ref[j, h, :, 2 * HD:2 * HD + CH], sv_ref[j, h, :, 0:HD], sv_ref[j, h, :, HD:2 * HD]))
        q, k, v, bh, _, s0 = (list(t) for t in zip(*args))
        cs = _gdn_chunks(args, solved)
        get = lambda name: [c[name] for c in cs]
        mul = lambda x, y: x * y
        add = lambda x, y: x + y
        dvn = _each(add, _each(_etn, get("p"), do), _each(_enn, get("kd"), ds1))
        dqg = _each(_ent, do, s0)
        dp = [jnp.where(causal, x, 0.0) for x in _each(_ent, do, get("vn"))]
        dkd = _each(_ent, get("vn"), ds1)
        dw = [-x for x in _each(_ent, dvn, s0)]
        ds_a = _each(_etn, get("qg"), do)
        ds_b = _each(_etn, get("w"), dvn)
        ds_new = [ds_a[i] - ds_b[i] + ds1[i] * cs[i]["egl"] for i in range(len(cs))]
        drv = _each(_rtn, get("tinv"), dvn)
        drw = _each(_rtn, get("tinv"), dw)
        da_1 = _each(_rnt, drv, get("u"))
        da_2 = _each(_rnt, drw, get("w"))
        da = [jnp.where(strict, -(x + y), 0.0) for x, y in zip(da_1, da_2)]
        m = [da[i] * cs[i]["a"] + dp[i] * cs[i]["p"] for i in range(len(cs))]
        dkk = _each(mul, da, get("ds"))
        dqk = _each(mul, dp, get("dm"))
        dq = _each(add, _each(_enn, dqk, k), _each(mul, dqg, get("eg")))
        dkb = _each(add, _each(_enn, dkk, k), _each(mul, drw, get("eg")))
        dk_1 = _each(_etn, dqk, q)
        dk_2 = _each(_etn, dkk, get("kb"))
        dk = [dk_1[i] + dk_2[i] + dkd[i] * cs[i]["ek"] + dkb[i] * bh[i] for i in range(len(cs))]
        dv = _each(mul, drv, bh)
        dbeta, dg = [], []
        for i, c in enumerate(cs):
            dbeta.append(_rs(drv[i] * v[i]) + _rs(dkb[i] * k[i]) + jnp.zeros((CH, HD), F32))
            t_kd = _rs(dkd[i] * c["kd"])
            dgc = _rs(m[i]) - _rs(m[i].T) + _rs(dqg[i] * c["qg"]) + _rs(drw[i] * c["rw"]) - t_kd
            tail = jnp.sum(t_kd, axis=0, keepdims=True) + c["egl"] * jnp.sum(_rs(s0[i] * ds1[i]), axis=0, keepdims=True)
            dgc = dgc + jnp.where(last, tail, 0.0)
            dg.append(_scan_rows(dgc + jnp.zeros((CH, HD), F32), CH, reverse=True))
        r8 = _iota2((8, HW), 0)
        upd = jnp.zeros((8, HW), F32)
        for j in range(gb):
            sl = slice(j * NH, (j + 1) * NH)
            beta, g, za = gates[j]
            for h in range(NH):
                ds_ref[j, h] = ds_new[j * NH + h]
            dx_ref[j] = jnp.concatenate(dq[sl] + dk[sl] + dv[sl], axis=-1)
            dbeta_j = jnp.where(valid, jnp.concatenate(dbeta[sl], axis=-1), 0.0)
            dg_j = jnp.where(valid, jnp.concatenate(dg[sl], axis=-1), 0.0)
            draw_b = dbeta_j * beta * (1.0 - beta)
            draw_a = dg_j * (-jnp.exp(alog)) * _sig(za)
            dba_ref[j] = _nn(jnp.concatenate([draw_b, draw_a], axis=-1), sm_ref[...]).astype(dba_ref.dtype)
            upd = upd + jnp.where(r8 == 0, jnp.sum(dg_j * g, axis=0, keepdims=True), 0.0) + jnp.where(
                r8 == 1, jnp.sum(draw_a, axis=0, keepdims=True), 0.0)
        acc_ref[...] += upd

    rc = lambda c: nc - 1 - c
    dqkv, dproj3, acc = pl.pallas_call(
        body, grid=(nb // gb, nc),
        in_specs=[pl.BlockSpec((gb, CH, 3 * HW), lambda b, c: (b, rc(c), 0)), pl.BlockSpec((gb, CH, HD), lambda b, c: (b, rc(c), cba)),
                  pl.BlockSpec((HD, 2 * HW), lambda b, c: (0, 0)), pl.BlockSpec((2 * HW, HD), lambda b, c: (0, 0)),
                  pl.BlockSpec((1, HW), lambda b, c: (0, 0)), pl.BlockSpec((1, HW), lambda b, c: (0, 0)),
                  pl.BlockSpec((gb, None, NH, HD, HD), lambda b, c: (b, rc(c), 0, 0, 0)),
                  pl.BlockSpec((gb, None, NH, CH, 3 * HD), lambda b, c: (b, rc(c), 0, 0, 0)),
                  pl.BlockSpec((gb, CH, HW), lambda b, c: (b, rc(c), 0)), pl.BlockSpec(memory_space=pl.ANY)],
        out_specs=[pl.BlockSpec((gb, CH, 3 * HW), lambda b, c: (b, rc(c), 0)), pl.BlockSpec((gb, CH, HD), lambda b, c: (b, rc(c), cba)),
                   pl.BlockSpec((8, HW), lambda b, c: (0, 0))],
        out_shape=[SDS((nb, tp, 3 * HW), F32), SDS((nb, tp, dproj.shape[1]), dproj.dtype), SDS((8, HW), F32)],
        input_output_aliases={9: 1},
        scratch_shapes=[pltpu.VMEM((gb, NH, HD, HD), F32)], compiler_params=_params(2), name="gdn_bwd")(
            qkv.reshape(nb, tp, 3 * HW), proj.reshape(nb, tp, -1), e_mat, s_mat, alog, dtb, states, solved, do.reshape(nb, tp, HW),
            dproj.reshape(nb, tp, -1))
    return dqkv.reshape(n, 3 * HW), dproj3.reshape(dproj.shape), acc


def _hgrn_inputs(zq, zf, lb):
    sg = _sig(zf)
    sgn = _sig(-zf)
    pos = lb > 0.0
    lbp = jnp.where(pos, lb, 0.0)
    fpos = lbp + (1.0 - lbp) * sg
    lf = jnp.where(pos, jnp.log(jnp.where(pos, fpos, 1.0)), _logsig(zf))
    k = (1.0 - lbp) * sgn
    q = _silu(zq) * Q_SCALE
    return q, k, lf, sg, sgn, pos, lbp, fpos


def _hgrn_consts():
    i3, j3 = _iota2((SUB, SUB, HD), 0), _iota2((SUB, SUB, HD), 1)
    return i3 >= j3


def _sum_j(x):
    return jnp.sum(x.reshape(SUB, SUB, HD), axis=1)


def _sum_i(x):
    return jnp.sum(x.reshape(SUB, SUB, HD), axis=0)


def _pairs(a, b):
    return (a[:, None, :] * b[None, :, :]).reshape(SUB * SUB, HD)


def _hgrn_sub(q, k, v, bc, st, consts):
    mask3 = consts
    bl = bc[SUB - 1:SUB, :]
    p3 = jnp.where(mask3, jnp.exp(jnp.where(mask3, bc[:, None, :] - bc[None, :, :], 0.0)), 0.0).reshape(SUB * SUB, HD)
    x = _pairs(q, k) * p3
    srep = _rs(x)
    vt = jnp.broadcast_to(v[None, :, :], (SUB, SUB, HD)).reshape(SUB * SUB, HD)
    eb = jnp.exp(bc)
    qe = q * eb
    o = _hnt(qe, st) + _sum_j(_rr(srep) * _rr(vt))
    ek = jnp.exp(bl - bc)
    kd = k * ek
    ebl = jnp.exp(bl)
    st1 = st * ebl + _htn(v, kd)
    return o, st1, dict(bc=bc, p3=p3, srep=srep, vt=vt, eb=eb, qe=qe, ek=ek, kd=kd, ebl=ebl)


def _hgrn_fwd(proj, lb, lay, nb, nc):
    n = proj.shape[0]
    cbb = lay.c_b // (3 * HW)

    def body(z_ref, lb_ref, o_ref, so_ref, s_ref):
        @pl.when(pl.program_id(1) == 0)
        def _():
            s_ref[...] = jnp.zeros_like(s_ref)

        consts = _hgrn_consts()
        outs = []
        for h in range(NH):
            hs = slice(h * HD, (h + 1) * HD)
            q, k, lf = _hgrn_inputs(z_ref[:, hs], z_ref[:, HW + h * HD:HW + (h + 1) * HD], lb_ref[:, hs])[:3]
            v = z_ref[:, 2 * HW + h * HD:2 * HW + (h + 1) * HD]
            st = s_ref[h]
            so_ref[h] = st
            bc = _scan_rows(lf, SUB)
            oh = []
            for s in range(CH // SUB):
                rs = slice(s * SUB, (s + 1) * SUB)
                o, st, _ = _hgrn_sub(q[rs], k[rs], v[rs], bc[rs], st, consts)
                oh.append(o)
            s_ref[h] = st
            outs.append(jnp.concatenate(oh, axis=0))
        o_ref[...] = jnp.concatenate(outs, axis=-1)

    return pl.pallas_call(
        body, grid=(nb, nc),
        in_specs=[pl.BlockSpec((CH, 3 * HW), lambda b, c: (b * nc + c, cbb)), pl.BlockSpec((1, HW), lambda b, c: (0, 0))],
        out_specs=[pl.BlockSpec((CH, HW), lambda b, c: (b * nc + c, 0)),
                   pl.BlockSpec((None, None, NH, HD, HD), lambda b, c: (b, c, 0, 0, 0))],
        out_shape=[SDS((n, HW), F32), SDS((nb, nc, NH, HD, HD), F32)],
        scratch_shapes=[pltpu.VMEM((NH, HD, HD), F32)], compiler_params=_params(2), name="hgrn_fwd")(proj, lb)


def _hgrn_bwd(proj, lb, states, do, dproj, lay, nb, nc):
    cbb = lay.c_b // (3 * HW)
    nsub = CH // SUB

    def rev(b, c):
        return b * nc + (nc - 1 - c)

    def body(z_ref, lb_ref, st_ref, do_ref, dp_in, dz_ref, acc_ref, ds_ref):
        ci = pl.program_id(1)

        @pl.when(ci == 0)
        def _():
            ds_ref[...] = jnp.zeros_like(ds_ref)

        @pl.when((ci == 0) & (pl.program_id(0) == 0))
        def _():
            acc_ref[...] = jnp.zeros_like(acc_ref)

        consts = _hgrn_consts()
        row = _iota2((CH, 1), 0)
        valid = (row >= PAD) | (ci < nc - 1)
        lastrow = _iota2((SUB, 1), 0) == SUB - 1
        dzq, dzf, dzi, dlbs = [], [], [], []
        for h in range(NH):
            hs = slice(h * HD, (h + 1) * HD)
            zq, zf = z_ref[:, hs], z_ref[:, HW + h * HD:HW + (h + 1) * HD]
            q, k, lf, sg, sgn, pos, lbp, fpos = _hgrn_inputs(zq, zf, lb_ref[:, hs])
            v = z_ref[:, 2 * HW + h * HD:2 * HW + (h + 1) * HD]
            doh = do_ref[:, hs]
            sts, fw = [st_ref[h]], []
            bc = _scan_rows(lf, SUB)
            for s in range(nsub):
                rs = slice(s * SUB, (s + 1) * SUB)
                _, st1, c = _hgrn_sub(q[rs], k[rs], v[rs], bc[rs], sts[-1], consts)
                sts.append(st1)
                fw.append(c)
            dst = ds_ref[h]
            dq_l, dk_l, dv_l, dlf_l = [None] * nsub, [None] * nsub, [None] * nsub, [None] * nsub
            for s in reversed(range(nsub)):
                rs = slice(s * SUB, (s + 1) * SUB)
                c, st = fw[s], sts[s]
                qs, ks, vs, dos = q[rs], k[rs], v[rs], doh[rs]
                dqe = _hnn(dos, st)
                dkd = _hnn(vs, dst)
                dsrep = _rs(_pairs(_rr(dos), _rr(vs)))
                w = dsrep * c["p3"]
                kt = jnp.broadcast_to(ks[None, :, :], (SUB, SUB, HD)).reshape(SUB * SUB, HD)
                qt = jnp.broadcast_to(qs[:, None, :], (SUB, SUB, HD)).reshape(SUB * SUB, HD)
                dq_i = _sum_j(w * kt)
                dk_i = _sum_i(w * qt)
                dot = jnp.broadcast_to(_rr(dos)[:, None, :], (SUB, SUB, HD)).reshape(SUB * SUB, HD)
                dvv = _sum_i(_rr(c["srep"]) * dot) + _hnt(c["kd"], dst)
                t_kd = dkd * c["kd"]
                dbc = dqe * c["qe"] - t_kd + qs * dq_i - ks * dk_i
                tail = jnp.sum(t_kd, axis=0, keepdims=True) + c["ebl"] * jnp.sum(st * dst, axis=0, keepdims=True)
                dbc = dbc + jnp.where(lastrow, tail, 0.0)
                dlf_l[s] = dbc
                dq_l[s] = dq_i + dqe * c["eb"]
                dk_l[s] = dk_i + dkd * c["ek"]
                dv_l[s] = dvv
                dst = _htn(dos, c["qe"]) + dst * c["ebl"]
            ds_ref[h] = dst
            dq, dk, dv, dbc = (jnp.concatenate(t, axis=0) for t in (dq_l, dk_l, dv_l, dlf_l))
            dlf = _scan_rows(dbc, SUB, reverse=True)
            dlft = dlf - dk * (1.0 - k)
            dlf_dz = jnp.where(pos, (1.0 - lbp) * sg * sgn / jnp.where(pos, fpos, 1.0), sgn)
            dlf_dlb = jnp.where(pos, sgn / jnp.where(pos, fpos, 1.0), 0.0)
            dzq.append(dq * Q_SCALE * _dsilu(zq))
            dzf.append(dlft * dlf_dz)
            dzi.append(dv)
            dlbs.append(jnp.sum(jnp.where(valid, dlft * dlf_dlb, 0.0), axis=0, keepdims=True))
        dz_ref[...] = jnp.concatenate(dzq + dzf + dzi, axis=-1).astype(dz_ref.dtype)
        acc_ref[...] += jnp.where(_iota2((8, HW), 0) == 0, jnp.concatenate(dlbs, axis=-1), 0.0)

    return pl.pallas_call(
        body, grid=(nb, nc),
        in_specs=[pl.BlockSpec((CH, 3 * HW), lambda b, c: (rev(b, c), cbb)), pl.BlockSpec((1, HW), lambda b, c: (0, 0)),
                  pl.BlockSpec((None, None, NH, HD, HD), lambda b, c: (b, nc - 1 - c, 0, 0, 0)),
                  pl.BlockSpec((CH, HW), lambda b, c: (rev(b, c), 0)), pl.BlockSpec(memory_space=pl.ANY)],
        out_specs=[pl.BlockSpec((CH, 3 * HW), lambda b, c: (rev(b, c), cbb)), pl.BlockSpec((8, HW), lambda b, c: (0, 0))],
        out_shape=[SDS(dproj.shape, dproj.dtype), SDS((8, HW), F32)],
        input_output_aliases={4: 0},
        scratch_shapes=[pltpu.VMEM((NH, HD, HD), F32)], compiler_params=_params(2), name="hgrn_bwd")(proj, lb, states, do, dproj)


def _gated_norm(o, z, gamma):
    ys, ns, rs = [], [], []
    for h in range(NH):
        hs = slice(h * HD, (h + 1) * HD)
        oh = o[:, hs]
        r = lax.rsqrt(jnp.mean(oh * oh, axis=-1, keepdims=True) + EPS)
        nh = oh * r
        ys.append(nh * gamma * _silu(z[:, hs]))
        ns.append(nh)
        rs.append(r)
    return jnp.concatenate(ys, axis=-1), ns, rs


def _merge_fwd(h, oa, ob, proj, ga, gb, wa, wb, wo, lay):
    n, d = h.shape
    tm = _tile(n, 384)
    wm = lay.wm

    def body(h_ref, oa_ref, ob_ref, p_ref, ga_ref, gb_ref, wa_ref, wb_ref, wo_ref, out_ref):
        ya, _, _ = _gated_norm(oa_ref[...], p_ref[:, 0:HW], ga_ref[...])
        yb, _, _ = _gated_norm(ob_ref[...], p_ref[:, HW:2 * HW], gb_ref[...])
        ya2 = _bnn(ya, wa_ref[...])
        yb2 = _bnn(yb, wb_ref[...])
        mixed = _sig(p_ref[:, 2 * HW:2 * HW + d]) * ya2 + _sig(p_ref[:, 2 * HW + d:2 * HW + 2 * d]) * yb2
        out_ref[...] = h_ref[...] + _bnn(mixed, wo_ref[...])

    full = lambda shape: pl.BlockSpec(shape, lambda i: (0, 0))
    return pl.pallas_call(
        body, grid=(n // tm,),
        in_specs=[pl.BlockSpec((tm, d), lambda i: (i, 0)), pl.BlockSpec((tm, HW), lambda i: (i, 0)),
                  pl.BlockSpec((tm, HW), lambda i: (i, 0)), pl.BlockSpec((tm, wm), lambda i: (i, 0)),
                  full((1, HD)), full((1, HD)), full((HW, d)), full((HW, d)), full((d, d))],
        out_specs=pl.BlockSpec((tm, d), lambda i: (i, 0)), out_shape=SDS((n, d), F32),
        compiler_params=_params(1), name="merge_fwd")(h, oa, ob, proj, ga, gb, wa, wb, wo)


def _gated_norm_bwd(dy, o, z, gamma):
    dos, dzs = [], []
    dgam = jnp.zeros((1, HD), F32)
    for h in range(NH):
        hs = slice(h * HD, (h + 1) * HD)
        oh, zh, dyh = o[:, hs], z[:, hs], dy[:, hs]
        r = lax.rsqrt(jnp.mean(oh * oh, axis=-1, keepdims=True) + EPS)
        nh = oh * r
        dzs.append(dyh * nh * gamma * _dsilu(zh))
        dng = dyh * _silu(zh)
        dgam = dgam + jnp.sum(dng * nh, axis=0, keepdims=True)
        dn = dng * gamma
        dos.append(r * (dn - nh * jnp.mean(dn * nh, axis=-1, keepdims=True)))
    return jnp.concatenate(dos, axis=-1), jnp.concatenate(dzs, axis=-1), dgam


def _merge_bwd(dhn, oa, ob, proj, ga, gb, wa, wb, wo, lay, tp):
    n, d = dhn.shape
    tm = _tile(n, 256)
    wm = lay.wm

    def body(dh_ref, oa_ref, ob_ref, p_ref, ga_ref, gb_ref, wa_ref, wb_ref, wo_ref,
             dp_ref, doa_ref, dob_ref, dwa_ref, dwb_ref, dwo_ref, dga_ref, dgb_ref):
        i = pl.program_id(0)

        @pl.when(i == 0)
        def _():
            for r in (dwa_ref, dwb_ref, dwo_ref, dga_ref, dgb_ref):
                r[...] = jnp.zeros_like(r)

        dh = jnp.where(_row_valid(tm, tp, i * tm), dh_ref[...], 0.0)
        oa, ob = oa_ref[...], ob_ref[...]
        za, zb = p_ref[:, 0:HW], p_ref[:, HW:2 * HW]
        gta, gtb = p_ref[:, 2 * HW:2 * HW + d], p_ref[:, 2 * HW + d:2 * HW + 2 * d]
        ya, _, _ = _gated_norm(oa, za, ga_ref[...])
        yb, _, _ = _gated_norm(ob, zb, gb_ref[...])
        ya2 = _bnn(ya, wa_ref[...])
        yb2 = _bnn(yb, wb_ref[...])
        sa, sb = _sig(gta), _sig(gtb)
        mixed = sa * ya2 + sb * yb2
        dmixed = _bnt(dh, wo_ref[...])
        dwo_ref[...] += _btn(mixed, dh)
        dya2 = dmixed * sa
        dyb2 = dmixed * sb
        dwa_ref[...] += _btn(ya, dya2)
        dwb_ref[...] += _btn(yb, dyb2)
        doa, dza, dga = _gated_norm_bwd(_bnt(dya2, wa_ref[...]), oa, za, ga_ref[...])
        dob, dzb, dgb = _gated_norm_bwd(_bnt(dyb2, wb_ref[...]), ob, zb, gb_ref[...])
        dga_ref[...] += dga
        dgb_ref[...] += dgb
        doa_ref[...] = doa
        dob_ref[...] = dob
        dt = dp_ref.dtype
        dp_ref[:, 0:HW] = dza.astype(dt)
        dp_ref[:, HW:2 * HW] = dzb.astype(dt)
        dp_ref[:, 2 * HW:2 * HW + d] = (dmixed * ya2 * sa * (1.0 - sa)).astype(dt)
        dp_ref[:, 2 * HW + d:2 * HW + 2 * d] = (dmixed * yb2 * sb * (1.0 - sb)).astype(dt)

    full = lambda shape: pl.BlockSpec(shape, lambda i: (0, 0))
    rows = lambda w: pl.BlockSpec((tm, w), lambda i: (i, 0))
    return pl.pallas_call(
        body, grid=(n // tm,),
        in_specs=[rows(d), rows(HW), rows(HW), rows(wm), full((1, HD)), full((1, HD)), full((HW, d)), full((HW, d)), full((d, d))],
        out_specs=[rows(wm), rows(HW), rows(HW), full((HW, d)), full((HW, d)), full((d, d)), full((1, HD)), full((1, HD))],
        out_shape=[SDS((n, lay.pw), BF16), SDS((n, HW), F32), SDS((n, HW), F32), SDS((HW, d), F32), SDS((HW, d), F32),
                   SDS((d, d), F32), SDS((1, HD), F32), SDS((1, HD), F32)],
        compiler_params=_params(1), name="merge_bwd")(dhn, oa, ob, proj, ga, gb, wa, wb, wo)


def _loss_head(h, target, fw, nb, nc):
    n, d = h.shape

    def body(h_ref, t_ref, fw_ref, lp_ref, dh_ref, dfw_ref):
        b, c = pl.program_id(0), pl.program_id(1)

        @pl.when((b == 0) & (c == 0))
        def _():
            dfw_ref[...] = jnp.zeros_like(dfw_ref)

        @pl.when(c == 0)
        def _():
            dh_ref[...] = jnp.zeros_like(dh_ref)
            lp_ref[...] = jnp.zeros_like(lp_ref)

        @pl.when(c > 0)
        def _():
            x = h_ref[...]
            r = lax.rsqrt(jnp.mean(x * x, axis=-1, keepdims=True) + EPS)
            xh = x * r
            err = xh * fw_ref[...] - t_ref[...]
            lp_ref[...] = jnp.zeros_like(lp_ref) + 0.5 * jnp.sum(_rs(err * err), axis=0, keepdims=True) / d
            dy = err / d
            dfw_ref[...] += jnp.sum(dy * xh, axis=0, keepdims=True)
            dxh = dy * fw_ref[...]
            dh_ref[...] = r * (dxh - xh * jnp.mean(dxh * xh, axis=-1, keepdims=True))

    return pl.pallas_call(
        body, grid=(nb, nc),
        in_specs=[pl.BlockSpec((CH, d), lambda b, c: (b * nc + c, 0)),
                  pl.BlockSpec((CH, d), lambda b, c: (b * (nc - 1) + jnp.maximum(c - 1, 0), 0)),
                  pl.BlockSpec((1, d), lambda b, c: (0, 0))],
        out_specs=[pl.BlockSpec((8, HD), lambda b, c: (b * nc + c, 0)), pl.BlockSpec((CH, d), lambda b, c: (b * nc + c, 0)),
                   pl.BlockSpec((1, d), lambda b, c: (0, 0))],
        out_shape=[SDS((nb * nc * 8, HD), F32), SDS((n, d), F32), SDS((1, d), F32)],
        compiler_params=_params(2), name="loss_head")(h, target, fw)


def _lb_fwd(lb):
    def body(x_ref, o_ref):
        x = x_ref[...]
        mx = jnp.max(x, axis=0, keepdims=True)
        e = jnp.exp(x - mx)
        sm = e / jnp.sum(e, axis=0, keepdims=True)
        run = jnp.zeros((1, HW), F32)
        for l in range(DEPTH):
            run = run + sm[l:l + 1, :]
            o_ref[l:l + 1, :] = run - sm[0:1, :]

    return pl.pallas_call(body, out_shape=SDS(lb.shape, F32), name="lb_fwd")(lb)


def _lb_bwd(lb, dlb_all):
    def body(x_ref, d_ref, o_ref):
        x = x_ref[...]
        dl = d_ref[...]
        mx = jnp.max(x, axis=0, keepdims=True)
        e = jnp.exp(x - mx)
        sm = e / jnp.sum(e, axis=0, keepdims=True)
        tot = jnp.sum(dl, axis=0, keepdims=True)
        dsm = []
        run = tot
        for l in range(DEPTH):
            dsm.append(run - (tot if l == 0 else 0.0))
            run = run - dl[l:l + 1, :]
        inner = sum(sm[l:l + 1, :] * dsm[l] for l in range(DEPTH))
        for l in range(DEPTH):
            o_ref[l:l + 1, :] = sm[l:l + 1, :] * (dsm[l] - inner)

    return pl.pallas_call(body, out_shape=SDS(lb.shape, F32), name="lb_bwd")(lb, dlb_all)


def _adamw(g, w, m, v):
    r, c = g.shape
    tr = _tile(r, 264)
    c1 = 1.0 / (1.0 - ADAM_B1 ** ADAM_STEP)
    c2 = 1.0 / (1.0 - ADAM_B2 ** ADAM_STEP)

    def body(g_ref, w_ref, m_ref, v_ref, d_ref, mo_ref, vo_ref):
        gg = g_ref[...]
        mn = ADAM_B1 * m_ref[...] + (1.0 - ADAM_B1) * gg
        vn = ADAM_B2 * v_ref[...] + (1.0 - ADAM_B2) * gg * gg
        d_ref[...] = -ADAM_LR * ((mn * c1) / (jnp.sqrt(vn * c2) + ADAM_EPS) + ADAM_WD * w_ref[...])
        mo_ref[...] = mn
        vo_ref[...] = vn

    spec = pl.BlockSpec((tr, c), lambda i: (i, 0))
    return pl.pallas_call(body, grid=(r // tr,), in_specs=[spec] * 4, out_specs=[spec] * 3, out_shape=[SDS(g.shape, F32)] * 3,
                          compiler_params=_params(1), name="adamw")(g, w, m, v)


def _tile16(n, target):
    return _tile(n // 2, target // 2) * 2 if n % 16 == 0 else _tile(n, target)


def _add_cores(g, got, core):
    k, r, c = got.shape
    tr = _tile16(r, 264)

    def body(c_ref, a_ref, b_ref, o_ref):
        o_ref[...] = (a_ref[...] + b_ref[...].astype(F32)).astype(o_ref.dtype)

    spec = pl.BlockSpec((None, tr, c), lambda s, i, cr: (s, i, 0))
    return pl.pallas_call(
        body, grid_spec=pltpu.PrefetchScalarGridSpec(
            num_scalar_prefetch=1, grid=(k, r // tr),
            in_specs=[pl.BlockSpec((None, None, tr, c), lambda s, i, cr: (cr[0], s, i, 0)), spec], out_specs=spec),
        out_shape=SDS(got.shape, got.dtype), compiler_params=_params(2), name="add_cores")(core, g, got)


def _sum_chips(parts, own, place):
    k, r, c = parts.shape
    tr = _tile16(r, 264)

    def body(p_ref, *refs):
        part_refs, own_ref, o_ref = refs[:k], refs[k], refs[k + 1]
        mine = own_ref[...].astype(F32)
        acc = None
        for s in range(k):
            term = jnp.where(p_ref[0] == s, mine, part_refs[s][...].astype(F32))
            acc = term if acc is None else acc + term
        o_ref[...] = acc

    slots = jnp.stack([jnp.where(place[0] == s, (s + 1) % k, s) for s in range(k)]).astype(jnp.int32)
    other = lambda s: pl.BlockSpec((None, tr, c), lambda i, p: (p[2 + s], i, 0))
    return pl.pallas_call(
        body, grid_spec=pltpu.PrefetchScalarGridSpec(
            num_scalar_prefetch=1, grid=(r // tr,),
            in_specs=[other(s) for s in range(k)] + [pl.BlockSpec((None, tr, c), lambda i, p: (p[0], i, 0))],
            out_specs=pl.BlockSpec((None, tr, c), lambda i, p: (p[1], i, 0))),
        out_shape=SDS((2, r, c), F32), compiler_params=_params(1), name="sum_chips")(
            jnp.concatenate([place, slots]), *([parts] * k), own)


def _meta_grad(dh, nb, nc):
    d = dh.shape[1]

    def body(x_ref, o_ref):
        @pl.when(pl.program_id(0) == 0)
        def _():
            o_ref[...] = jnp.zeros_like(o_ref)

        o_ref[...] += x_ref[PAD:CH, :]

    return pl.pallas_call(body, grid=(nb,), in_specs=[pl.BlockSpec((CH, d), lambda b: (b * nc, 0))],
                          out_specs=pl.BlockSpec((N_META, d), lambda b: (0, 0)), out_shape=SDS((N_META, d), F32),
                          compiler_params=_params(1), name="meta_grad")(dh)


ANY = pl.BlockSpec(memory_space=pl.ANY)


def _place():
    x, y, c = lax.axis_index("x"), lax.axis_index("y"), lax.axis_index("c")
    chips = [(1 - x, y), (x, 1 - y), (1 - x, 1 - y)]
    return x, y, c, chips


def _remote(src, dst, send_sems, recv_sems, k, to):
    return pltpu.make_async_remote_copy(src_ref=src, dst_ref=dst, send_sem=send_sems.at[k], recv_sem=recv_sems.at[k],
                                        device_id=to, device_id_type=MESH)


def _gather_weights(pbs, ps):
    nt = len(pbs)

    def body(*refs):
        pb_refs, ps_ref, gb_refs, gs_ref = refs[:nt], refs[nt], refs[nt + 1:2 * nt + 1], refs[2 * nt + 1]
        send_sems, recv_sems, local_sems = refs[2 * nt + 2:]
        x, y, c, chips = _place()
        s = 2 * x + y
        sib = (x, y, 1 - c)
        l1 = pltpu.make_async_copy(ps_ref, gs_ref.at[s], local_sems.at[0])
        l1.start()
        sends = []
        for k, (px, py) in enumerate(chips):
            for t in range(nt):
                sends.append(_remote(pb_refs[t].at[c], gb_refs[t].at[s, c], send_sems, recv_sems, 6 * t + k, (px, py, c)))
            sends.append(_remote(ps_ref, gs_ref.at[s], send_sems, recv_sems, 6 * nt + k, (px, py, c)))
        for cp in sends:
            cp.start()
        for k, (px, py) in enumerate(chips):
            sk = 2 * px + py
            for t in range(nt):
                _remote(pb_refs[t].at[c], gb_refs[t].at[sk, c], send_sems, recv_sems, 6 * t + k, sib).wait_recv()
                fwd = _remote(gb_refs[t].at[sk, c], gb_refs[t].at[sk, c], send_sems, recv_sems, 6 * t + 3 + k, sib)
                fwd.start()
                sends.append(fwd)
        for k, (px, py) in enumerate(chips):
            sk = 2 * px + py
            for t in range(nt):
                _remote(pb_refs[t].at[c], gb_refs[t].at[sk, 1 - c], send_sems, recv_sems, 6 * t + 3 + k, sib).wait_recv()
            _remote(ps_ref, gs_ref.at[sk], send_sems, recv_sems, 6 * nt + k, sib).wait_recv()
        for cp in sends:
            cp.wait_send()
        l1.wait()

    nsem = 6 * nt + 3
    out = pl.pallas_call(
        body, in_specs=[ANY] * (nt + 1), out_specs=[ANY] * (nt + 1),
        out_shape=[SDS((4,) + pb.shape, pb.dtype) for pb in pbs] + [SDS((4,) + ps.shape, ps.dtype)],
        scratch_shapes=[pltpu.SemaphoreType.DMA((nsem,)), pltpu.SemaphoreType.DMA((nsem,)), pltpu.SemaphoreType.DMA((1,))],
        name="gather_weights")(*pbs, ps)
    return out[:nt], out[nt]


def _contain(wpad, shift):
    r, cw = wpad.shape
    tr = _tile16(r, 256)

    def body(n_ref, x_ref, o_ref):
        o_ref[...] = pltpu.roll(x_ref[...], n_ref[0], axis=1).astype(o_ref.dtype)

    spec = pl.BlockSpec((tr, cw), lambda i, n: (i, 0))
    return pl.pallas_call(
        body, grid_spec=pltpu.PrefetchScalarGridSpec(num_scalar_prefetch=1, grid=(r // tr,), in_specs=[spec], out_specs=spec),
        out_shape=SDS((r, cw), BF16), compiler_params=_params(1), name="contain")(shift, wpad)


def _place_own(gb, pb, chip):
    _, _, r, c = gb.shape
    tr = _tile16(r, 1100)

    def body(s_ref, p_ref, g_in, o_ref):
        o_ref[...] = p_ref[...]

    return pl.pallas_call(
        body, grid_spec=pltpu.PrefetchScalarGridSpec(
            num_scalar_prefetch=1, grid=(2, r // tr),
            in_specs=[pl.BlockSpec((None, tr, c), lambda h, i, s: (h, i, 0)), ANY],
            out_specs=pl.BlockSpec((None, None, tr, c), lambda h, i, s: (s[0], h, i, 0))),
        out_shape=SDS(gb.shape, gb.dtype), input_output_aliases={2: 0}, compiler_params=_params(2),
        name="place_own")(chip, pb, gb)


def _sem_scratch(n_remote, n_local):
    return [pltpu.SemaphoreType.DMA((n_remote,)), pltpu.SemaphoreType.DMA((n_remote,)), pltpu.SemaphoreType.DMA((n_local,))]


def _swap_halves(sends):
    nt = len(sends)

    def body(*refs):
        s_refs, got_refs = refs[:nt], refs[nt:2 * nt]
        send_sems, recv_sems = refs[2 * nt:]
        x, y, c, _ = _place()
        sib = (x, y, 1 - c)
        remote = [_remote(s_refs[t].at[1 - c, s], got_refs[t].at[s], send_sems, recv_sems, 4 * t + s, sib)
                  for t in range(nt) for s in range(4)]
        for cp in remote:
            cp.start()
        for cp in remote:
            cp.wait()

    return pl.pallas_call(
        body, in_specs=[ANY] * nt, out_specs=[ANY] * nt, out_shape=[SDS(g.shape[1:], g.dtype) for g in sends],
        scratch_shapes=[pltpu.SemaphoreType.DMA((4 * nt,)), pltpu.SemaphoreType.DMA((4 * nt,))], name="swap_halves")(*sends)


def _scatter_chip_sums(parts):
    nt = len(parts)

    def body(*refs):
        a_refs, r_refs = refs[:nt], refs[nt:2 * nt]
        send_sems, recv_sems = refs[2 * nt:]
        x, y, c, chips = _place()
        s = 2 * x + y
        sends = [_remote(a_refs[t].at[2 * px + py], r_refs[t].at[s], send_sems, recv_sems, 3 * t + k, (px, py, c))
                 for t in range(nt) for k, (px, py) in enumerate(chips)]
        for cp in sends:
            cp.start()
        for t in range(nt):
            for k, (px, py) in enumerate(chips):
                _remote(a_refs[t].at[s], r_refs[t].at[2 * px + py], send_sems, recv_sems, 3 * t + k, (px, py, c)).wait_recv()
        for cp in sends:
            cp.wait_send()

    return pl.pallas_call(
        body, in_specs=[ANY] * nt, out_specs=[ANY] * nt, out_shape=[SDS(a.shape, a.dtype) for a in parts],
        scratch_shapes=[pltpu.SemaphoreType.DMA((3 * nt,)), pltpu.SemaphoreType.DMA((3 * nt,))],
        name="scatter_chip_sums")(*parts)


def _join_halves(fs):
    nt = len(fs)

    def body(*refs):
        f_refs = refs[nt:2 * nt]
        send_sems, recv_sems = refs[2 * nt:]
        x, y, c, _ = _place()
        sib = (x, y, 1 - c)
        sends = [_remote(f_refs[t].at[c], f_refs[t].at[c], send_sems, recv_sems, t, sib) for t in range(nt)]
        for cp in sends:
            cp.start()
        for t in range(nt):
            _remote(f_refs[t].at[c], f_refs[t].at[1 - c], send_sems, recv_sems, t, sib).wait_recv()
        for cp in sends:
            cp.wait_send()

    return pl.pallas_call(
        body, in_specs=[ANY] * nt, out_specs=[ANY] * nt, out_shape=[SDS(f.shape, f.dtype) for f in fs],
        input_output_aliases={t: t for t in range(nt)},
        scratch_shapes=[pltpu.SemaphoreType.DMA((nt,)), pltpu.SemaphoreType.DMA((nt,))], name="join_halves")(*fs)


def _uncontain(cont, n_head, width):
    r, cw = cont.shape
    tr = _tile(r, 256)

    def body(n_ref, x_ref, o_ref):
        o_ref[...] = pltpu.roll(x_ref[...], n_ref[0], axis=1)[:, :width]

    return pl.pallas_call(
        body, grid_spec=pltpu.PrefetchScalarGridSpec(
            num_scalar_prefetch=1, grid=(r // tr,), in_specs=[pl.BlockSpec((tr, cw), lambda i, n: (i, 0))],
            out_specs=pl.BlockSpec((tr, width), lambda i, n: (i, 0))),
        out_shape=SDS((r, width), F32), compiler_params=_params(1), name="uncontain")(n_head, cont)


WEIGHTS = ("meta_tokens", "norm_w", "w_in", "conv_w", "a_log", "dt_bias", "gnorm_a", "gnorm_b", "hgrn_lower_bounds",
           "w_branch_a", "w_branch_b", "w_out", "final_norm_w")
SHARD_AXIS = {"meta_tokens": 1, "w_in": 2, "conv_w": 2, "w_branch_a": 2, "w_branch_b": 2, "w_out": 1}
FLAT_C = 1024


def _flat(parts, rows, cols=FLAT_C):
    v = jnp.concatenate([p.reshape(-1) for p in parts])
    return jnp.pad(v, (0, rows * cols - v.shape[0])).reshape(rows, cols)


def _local_step(x, target, w, lay):
    nb, seq, d = x.shape
    tp = CH + seq
    nc = tp // CH
    n = nb * tp
    e_mat, s_mat = _gate_consts()
    lb_all = _lb_fwd(w["hgrn_lower_bounds"])
    h = jnp.concatenate([jnp.zeros((nb, PAD, d), F32), jnp.broadcast_to(w["meta_tokens"][None], (nb, N_META, d)), x],
                        axis=1).reshape(n, d)
    rep = lambda a: jnp.repeat(a, HD)[None, :]
    saved = []
    for l in range(DEPTH):
        nw = w["norm_w"][l][None, :]
        proj, xn = _norm_proj_fwd(h, nw, w["w_in"][l])
        qkv = _gdn_prep_fwd(proj, w["conv_w"][l], lay, nb, tp)
        alog, dtb = rep(w["a_log"][l]), rep(w["dt_bias"][l])
        oa, sa, sva = _gdn_fwd(qkv, proj, e_mat, alog, dtb, lay, nb, nc)
        lbl = lb_all[l][None, :]
        ob, sb = _hgrn_fwd(proj, lbl, lay, nb, nc)
        ga, gb = w["gnorm_a"][l][None, :], w["gnorm_b"][l][None, :]
        hn = _merge_fwd(h, oa, ob, proj, ga, gb, w["w_branch_a"][l], w["w_branch_b"][l], w["w_out"][l], lay)
        saved.append((h, nw, proj, qkv, alog, dtb, oa, sa, lbl, ob, sb, ga, gb, xn, sva))
        h = hn
    lp, dh, dfw = _loss_head(h, target.reshape(nb * seq, d), w["final_norm_w"][None, :], nb, nc)
    loss = jnp.sum(lp[::8, 0])
    g = {n_: [None] * DEPTH for n_ in WEIGHTS}
    dlb_all = [None] * DEPTH
    for l in reversed(range(DEPTH)):
        h, nw, proj, qkv, alog, dtb, oa, sa, lbl, ob, sb, ga, gb, xn, sva = saved[l]
        dproj, doa, dob, dwa, dwb, dwo, dga, dgb = _merge_bwd(dh, oa, ob, proj, ga, gb, w["w_branch_a"][l],
                                                             w["w_branch_b"][l], w["w_out"][l], lay, tp)
        dproj, acc_b = _hgrn_bwd(proj, lbl, sb, dob, dproj, lay, nb, nc)
        dqkv, dproj, acc_a = _gdn_bwd(qkv, proj, e_mat, s_mat, alog, dtb, sa, sva, doa, dproj, lay, nb, nc)
        dproj, dconv = _gdn_prep_bwd(proj, w["conv_w"][l], dqkv, dproj, lay, nb, tp)
        dh, dnw = _proj_bwd_dx(dproj, w["w_in"][l], h, nw, dh, tp)
        g["w_in"][l] = _proj_bwd_dw(dproj, xn, tp)
        g["norm_w"][l] = dnw[0]
        g["conv_w"][l] = dconv
        g["a_log"][l] = acc_a[0, ::HD]
        g["dt_bias"][l] = acc_a[1, ::HD]
        g["gnorm_a"][l], g["gnorm_b"][l] = dga[0], dgb[0]
        g["w_branch_a"][l], g["w_branch_b"][l], g["w_out"][l] = dwa, dwb, dwo
        dlb_all[l] = acc_b[0]
    grads = {n_: jnp.stack(v) for n_, v in g.items() if v[0] is not None}
    grads["hgrn_lower_bounds"] = _lb_bwd(w["hgrn_lower_bounds"], jnp.stack(dlb_all))
    grads["final_norm_w"] = dfw[0]
    grads["meta_tokens"] = _meta_grad(dh, nb, nc)
    grad_x = dh.reshape(nb, tp, d)[:, CH:, :]
    return loss, grad_x, grads


def kernel(x, meta_tokens, norm_w, w_in, conv_w, a_log, dt_bias, gnorm_a, gnorm_b, hgrn_lower_bounds, w_branch_a, w_branch_b, w_out, final_norm_w, loss_target, m_meta_tokens, m_norm_w, m_w_in, m_conv_w, m_a_log, m_dt_bias, m_gnorm_a, m_gnorm_b, m_hgrn_lower_bounds, m_w_branch_a, m_w_branch_b, m_w_out, m_final_norm_w, v_meta_tokens, v_norm_w, v_w_in, v_conv_w, v_a_log, v_dt_bias, v_gnorm_a, v_gnorm_b, v_hgrn_lower_bounds, v_w_branch_a, v_w_branch_b, v_w_out, v_final_norm_w):
    wl = dict(meta_tokens=meta_tokens, norm_w=norm_w, w_in=w_in, conv_w=conv_w, a_log=a_log, dt_bias=dt_bias, gnorm_a=gnorm_a,
              gnorm_b=gnorm_b, hgrn_lower_bounds=hgrn_lower_bounds, w_branch_a=w_branch_a, w_branch_b=w_branch_b, w_out=w_out,
              final_norm_w=final_norm_w)
    ml = dict(zip(WEIGHTS, (m_meta_tokens, m_norm_w, m_w_in, m_conv_w, m_a_log, m_dt_bias, m_gnorm_a, m_gnorm_b,
                            m_hgrn_lower_bounds, m_w_branch_a, m_w_branch_b, m_w_out, m_final_norm_w)))
    vl = dict(zip(WEIGHTS, (v_meta_tokens, v_norm_w, v_w_in, v_conv_w, v_a_log, v_dt_bias, v_gnorm_a, v_gnorm_b,
                            v_hgrn_lower_bounds, v_w_branch_a, v_w_branch_b, v_w_out, v_final_norm_w)))
    d = x.shape[2]
    lay = _Layout(d)
    nchip = 4

    big = ("w_in", "w_branch_a", "w_branch_b", "w_out")
    small = ("conv_w", "meta_tokens")
    table, heads, cw = lay.pieces(nchip)
    sw = wl["w_in"].shape[2]
    chip_id = (2 * lax.axis_index("x") + lax.axis_index("y")).astype(jnp.int32)
    n_head = sum(jnp.where(chip_id == s, heads[s], 0) for s in range(nchip)).astype(jnp.int32)
    w_pad = jnp.pad(wl["w_in"], ((0, 0), (0, 0), (0, cw - sw))).reshape(DEPTH * d, cw)
    shift = jnp.where(n_head == 0, 0, cw - n_head).astype(jnp.int32).reshape(1)
    pbs = [_contain(w_pad, shift).reshape(DEPTH, d, cw)] + [wl[n].astype(BF16) for n in big[1:]]
    nsmall = sum(int(np.prod(wl[n].shape)) for n in small)
    rs = -(-nsmall // (HD * 8)) * 8
    ps = jnp.pad(jnp.concatenate([wl[n].reshape(-1) for n in small]), (0, rs * HD - nsmall)).reshape(rs, HD)
    gbig, gsmall = _gather_weights(pbs, ps)
    gbig = [_place_own(g, p, chip_id.reshape(1)) for g, p in zip(gbig, pbs)]
    gsmall = gsmall.reshape(nchip, -1)

    wf = dict(wl)
    wf["w_in"] = lay.from_containers([gbig[0][s] for s in range(nchip)])
    for i, n in enumerate(big[1:], start=1):
        wf[n] = jnp.concatenate([gbig[i][s] for s in range(nchip)], axis=SHARD_AXIS[n])
    o = 0
    for n in small:
        sz = int(np.prod(wl[n].shape))
        a = gsmall[:, o:o + sz].reshape((nchip,) + wl[n].shape)
        wf[n] = jnp.concatenate([a[s] for s in range(nchip)], axis=SHARD_AXIS[n])
        o += sz

    loss_part, grad_x, gfull = _local_step(x, loss_target, wf, lay)
    loss = lax.psum(loss_part, ("x", "y", "c"))

    sw = wl["w_in"].shape[2]
    conts, heads = lay.containers(gfull["w_in"], nchip)
    dd = wl["w_branch_a"].shape[2]
    rows_o = wl["w_out"].shape[1]
    by_dest = lambda g, n: [lax.slice_in_dim(g, s * wl[n].shape[SHARD_AXIS[n]], (s + 1) * wl[n].shape[SHARD_AXIS[n]],
                                            axis=SHARD_AXIS[n]) if n in SHARD_AXIS else g for s in range(nchip)]
    small_names = tuple(n for n in WEIGHTS if n not in big)
    nsm = sum(int(np.prod(wl[n].shape)) for n in small_names)
    rsm = -(-nsm // (2 * HD * 8)) * 8
    pack_small = lambda parts: _flat(parts, 2 * rsm, HD).reshape(2, rsm, HD)
    small_by_dest = [by_dest(gfull[n], n) for n in small_names]
    gs = [jnp.stack(conts, axis=1),
          jnp.stack(by_dest(gfull["w_branch_a"], "w_branch_a"), axis=1),
          jnp.stack(by_dest(gfull["w_branch_b"], "w_branch_b"), axis=1),
          gfull["w_out"].reshape(DEPTH, nchip, rows_o, d),
          jnp.stack([pack_small([p[s] for p in small_by_dest]) for s in range(nchip)], axis=1)]
    gs = [g.reshape((2, nchip, -1, g.shape[-1])) for g in gs]
    my_chip = (2 * lax.axis_index("x") + lax.axis_index("y")).astype(jnp.int32)
    my_core = lax.axis_index("c").astype(jnp.int32)
    got = _swap_halves([g.astype(BF16) for g in gs[:4]] + gs[4:])
    chip_sums = [_add_cores(g, b, my_core.reshape(1)) for g, b in zip(gs, got)]
    by_chip = _scatter_chip_sums(chip_sums)
    place = jnp.stack([my_chip, my_core])
    full = _join_halves([_sum_chips(p, a, place) for p, a in zip(by_chip, chip_sums)])
    n_head = sum(jnp.where(my_chip == s, heads[s], 0) for s in range(nchip)).astype(jnp.int32).reshape(1)
    g_w_in = _uncontain(full[0].reshape(DEPTH * d, -1), n_head, sw)
    g2 = {"w_in": g_w_in, "w_branch_a": full[1].reshape(-1, dd), "w_branch_b": full[2].reshape(-1, dd),
          "w_out": full[3].reshape(-1, d), "small": full[4].reshape(2 * rsm, HD)}

    def two_d(src, n):
        if n == "small":
            return _flat([src[k] for k in small_names], 2 * rsm, HD)
        return src[n].reshape(g2[n].shape)

    outs = {}
    for n in big + ("small",):
        delta, mnew, vnew = _adamw(g2[n], two_d(wl, n), two_d(ml, n), two_d(vl, n))
        outs[n] = (g2[n], delta, mnew, vnew)
    res = [{}, {}, {}, {}]
    for i in range(4):
        for n in big:
            res[i][n] = outs[n][i].reshape(wl[n].shape)
        v, o = outs["small"][i].reshape(-1), 0
        for n in small_names:
            sz = int(np.prod(wl[n].shape))
            res[i][n] = v[o:o + sz].reshape(wl[n].shape)
            o += sz
    return (loss, grad_x, *[res[0][n] for n in WEIGHTS], *[res[1][n] for n in WEIGHTS], *[res[2][n] for n in WEIGHTS],
            *[res[3][n] for n in WEIGHTS])
```

```python
import functools

import numpy as np
import jax
import jax.numpy as jnp
from jax import lax
from jax.experimental import pallas as pl
from jax.experimental.pallas import tpu as pltpu

F32 = jnp.float32
BF16 = jnp.bfloat16
HI = lax.Precision.HIGHEST
SDS = jax.ShapeDtypeStruct

NH = 4
HD = 128
HW = NH * HD
N_META = 16
CH = 64
SUB = 16
PAD = CH - N_META
EPS = 1e-6
Q_SCALE = HD ** -0.5
DEPTH = 2
CONV_K = 4
VMEM_LIMIT = 56 * 1024 * 1024
ADAM_LR, ADAM_B1, ADAM_B2, ADAM_EPS, ADAM_WD, ADAM_STEP = 0.001, 0.9, 0.999, 1e-08, 0.01, 10
MESH = pl.DeviceIdType.MESH


def _nn(a, b):
    return jnp.dot(a, b, precision=HI, preferred_element_type=F32)


def _nt(a, b):
    return lax.dot_general(a, b, (((1,), (1,)), ((), ())), precision=HI, preferred_element_type=F32)


def _tn(a, b):
    return _nn(a.T, b)


def _scan_rows(x, group, reverse=False):
    n = x.shape[0]
    pos = lax.bitwise_and(_iota2(x.shape, 0), group - 1)
    s = 1
    while s < group:
        if reverse:
            x = x + jnp.where(pos < group - s, pltpu.roll(x, n - s, axis=0), 0.0)
        else:
            x = x + jnp.where(pos >= s, pltpu.roll(x, s, axis=0), 0.0)
        s *= 2
    return x


def _bnn(a, b):
    return jnp.dot(a.astype(BF16), b.astype(BF16), preferred_element_type=F32)


def _bnt(a, b):
    return lax.dot_general(a.astype(BF16), b.astype(BF16), (((1,), (1,)), ((), ())), preferred_element_type=F32)


def _btn(a, b):
    return lax.dot_general(a.astype(BF16), b.astype(BF16), (((0,), (0,)), ((), ())), preferred_element_type=F32)


def _hi_lo(x):
    hi = x.astype(jnp.bfloat16)
    return hi, (x - hi.astype(F32)).astype(jnp.bfloat16)


def _dot3(dims):
    def f(a, b):
        ah, al = _hi_lo(a)
        bh, bl = _hi_lo(b)
        d = lambda p, q: lax.dot_general(p, q, (dims, ((), ())), preferred_element_type=F32)
        return d(ah, bh) + (d(ah, bl) + d(al, bh))
    return f


_rnn, _rnt, _rtn = _dot3(((1,), (0,))), _dot3(((1,), (1,))), _dot3(((0,), (0,)))
_enn, _ent, _etn = _bnn, _bnt, _btn
_hnn, _hnt, _htn = _bnn, _bnt, _btn


def _rr(x):
    return x


def _sig(x):
    return jax.nn.sigmoid(x)


def _silu(x):
    return x * _sig(x)


def _dsilu(x):
    s = _sig(x)
    return s * (1.0 + x * (1.0 - s))


def _softplus(x):
    return jnp.maximum(x, 0.0) + jnp.log(1.0 + jnp.exp(-jnp.abs(x)))


def _logsig(x):
    return jnp.minimum(x, 0.0) - jnp.log(1.0 + jnp.exp(-jnp.abs(x)))


def _rs(x):
    return jnp.sum(x, axis=-1, keepdims=True)


def _params(n_axes):
    return pltpu.CompilerParams(dimension_semantics=("arbitrary",) * n_axes, vmem_limit_bytes=VMEM_LIMIT)


def _tile(n, target, mult=8):
    best = mult
    for t in range(mult, target + 1, mult):
        if n % t == 0:
            best = t
    assert n % best == 0, (n, mult)
    return best


def _ctile(pw, most=7):
    return HD * max(k for k in range(1, most + 1) if (pw // HD) % k == 0)


def _iota2(shape, axis):
    return lax.broadcasted_iota(jnp.int32, shape, axis)


class _Layout:
    def __init__(self, d):
        self.d = d
        self.wm = 2 * HW + 2 * d
        self.c_qkv = self.wm
        self.c_b = self.wm + 3 * HW
        self.c_ba = self.wm + 6 * HW
        self.pw = self.c_ba + HD
        assert self.c_b % (3 * HW) == 0
        o = 0
        segs = {}
        for name, w in (("a_q", HW), ("a_k", HW), ("a_v", HW), ("ba", 2 * NH), ("a_z", HW), ("b_q", HW), ("b_f", HW),
                        ("b_i", HW), ("b_g", HW), ("gate_a", d), ("gate_b", d)):
            segs[name] = (o, o + w)
            o += w
        self.segs = segs
        self.width = o
        self.order = ("a_z", "b_g", "gate_a", "gate_b", "a_q", "a_k", "a_v", "b_q", "b_f", "b_i", "ba")

    def to_kernel(self, w):
        parts = [w[..., self.segs[n][0]:self.segs[n][1]] for n in self.order]
        parts.append(jnp.zeros(w.shape[:-1] + (HD - 2 * NH,), w.dtype))
        return jnp.concatenate(parts, axis=-1)

    def containers(self, g, nchip):
        table, heads, cw = self.pieces(nchip)
        out = []
        for s in range(nchip):
            parts, at = [], 0
            for kcol, w, ccol in sorted(table[s], key=lambda p: p[2]):
                if ccol > at:
                    parts.append(jnp.zeros(g.shape[:-1] + (ccol - at,), g.dtype))
                parts.append(g[..., kcol:kcol + w])
                at = ccol + w
            if at < cw:
                parts.append(jnp.zeros(g.shape[:-1] + (cw - at,), g.dtype))
            out.append(jnp.concatenate(parts, axis=-1))
        return out, heads

    def pieces(self, nchip):
        off, where = 0, {}
        for n in self.order:
            where[n] = off
            off += self.segs[n][1] - self.segs[n][0]
        names = sorted(self.segs, key=lambda n: self.segs[n][0])
        sw = self.width // nchip
        cw = -(-sw // HD) * HD
        table, heads = [], []
        for s in range(nchip):
            lo, hi = s * sw, (s + 1) * sw
            pieces = []
            for n in names:
                a, b = max(lo, self.segs[n][0]), min(hi, self.segs[n][1])
                if a < b:
                    pieces.append((where[n] + a - self.segs[n][0], b - a))
            start, width = pieces[0]
            n_head = min((-start) % HD, width)
            body = ([(start + n_head, width - n_head)] if width > n_head else []) + pieces[1:]
            rows, at = [], 0
            for c, w in body:
                rows.append((c, w, at))
                at += w
            if n_head:
                rows.append((start, n_head, cw - n_head))
            table.append(rows)
            heads.append(n_head)
        return table, heads, cw

    def from_containers(self, conts):
        table, _, _ = self.pieces(len(conts))
        cut = sorted((kcol, w, s, ccol) for s, rows in enumerate(table) for kcol, w, ccol in rows)
        parts, at = [], 0
        for kcol, w, s, ccol in cut:
            assert kcol == at, (kcol, at)
            parts.append(conts[s][..., ccol:ccol + w])
            at = kcol + w
        parts.append(jnp.zeros(conts[0].shape[:-1] + (self.pw - at,), conts[0].dtype))
        return jnp.concatenate(parts, axis=-1)

    def from_kernel(self, g):
        off, where = 0, {}
        for n in self.order:
            w = self.segs[n][1] - self.segs[n][0]
            where[n] = (off, off + w)
            off += w
        names = sorted(self.segs, key=lambda n: self.segs[n][0])
        return jnp.concatenate([g[..., where[n][0]:where[n][1]] for n in names], axis=-1)


def _norm_proj_fwd(h, nw, wp):
    n, d = h.shape
    pw = wp.shape[1]
    tm, tn = _tile(n, 1408, HD), _ctile(pw)

    def body(h_ref, nw_ref, w_ref, o_ref, xt_ref, xn_ref):
        @pl.when(pl.program_id(1) == 0)
        def _():
            x = h_ref[...]
            r = lax.rsqrt(jnp.mean(x * x, axis=-1, keepdims=True) + EPS)
            xn = (x * r * nw_ref[...]).astype(BF16)
            xn_ref[...] = xn
            xt_ref[...] = xn.T

        o_ref[...] = jnp.dot(xn_ref[...], w_ref[...], preferred_element_type=F32)

    return pl.pallas_call(
        body, grid=(n // tm, pw // tn),
        in_specs=[pl.BlockSpec((tm, d), lambda i, j: (i, 0)), pl.BlockSpec((1, d), lambda i, j: (0, 0)),
                  pl.BlockSpec((d, tn), lambda i, j: (0, j))],
        out_specs=[pl.BlockSpec((tm, tn), lambda i, j: (i, j)), pl.BlockSpec((d, tm), lambda i, j: (0, i))],
        out_shape=[SDS((n, pw), F32), SDS((d, n), BF16)], scratch_shapes=[pltpu.VMEM((tm, d), BF16)],
        compiler_params=_params(2), name="norm_proj_fwd")(h, nw, wp)


def _row_valid(tm, tp, base):
    row = base + _iota2((tm, 1), 0)
    return lax.rem(row, tp) >= PAD


def _proj_bwd_dx(dproj, wpt, h, nw, dhn, tp):
    n, d = h.shape
    pw = wpt.shape[0]
    tm, tk = _tile16(n, 768), _ctile(pw)
    nk = pw // tk

    def body(dp_ref, w_ref, h_ref, nw_ref, dhn_ref, dh_ref, dnw_ref, acc_ref):
        i, k = pl.program_id(0), pl.program_id(1)

        @pl.when(k == 0)
        def _():
            acc_ref[...] = jnp.zeros_like(acc_ref)

        @pl.when((i == 0) & (k == 0))
        def _():
            dnw_ref[...] = jnp.zeros_like(dnw_ref)

        valid = _row_valid(tm, tp, i * tm)
        dp = jnp.where(valid, dp_ref[...], 0.0)
        acc_ref[...] += jnp.dot(dp.astype(BF16), w_ref[...], preferred_element_type=F32)

        @pl.when(k == nk - 1)
        def _():
            x = h_ref[...]
            r = lax.rsqrt(jnp.mean(x * x, axis=-1, keepdims=True) + EPS)
            xh = x * r
            dxn = acc_ref[...]
            dnw_ref[...] += jnp.sum(dxn * xh, axis=0, keepdims=True)
            dxh = dxn * nw_ref[...]
            dh_ref[...] = dhn_ref[...] + r * (dxh - xh * jnp.mean(dxh * xh, axis=-1, keepdims=True))

    return pl.pallas_call(
        body, grid=(n // tm, nk),
        in_specs=[pl.BlockSpec((tm, tk), lambda i, k: (i, k)), pl.BlockSpec((tk, d), lambda i, k: (k, 0)),
                  pl.BlockSpec((tm, d), lambda i, k: (i, 0)), pl.BlockSpec((1, d), lambda i, k: (0, 0)),
                  pl.BlockSpec((tm, d), lambda i, k: (i, 0))],
        out_specs=[pl.BlockSpec((tm, d), lambda i, k: (i, 0)), pl.BlockSpec((1, d), lambda i, k: (0, 0))],
        out_shape=[SDS((n, d), F32), SDS((1, d), F32)],
        scratch_shapes=[pltpu.VMEM((tm, d), F32)], compiler_params=_params(2), name="proj_bwd_dx")(dproj, wpt, h, nw, dhn)


def _proj_bwd_dw(dproj, xt, tp):
    d, n = xt.shape
    pw = dproj.shape[1]
    tm, tn = _tile(n, 1408, HD), _ctile(pw)

    def body(dp_ref, xt_ref, dw_ref):
        i = pl.program_id(1)

        @pl.when(i == 0)
        def _():
            dw_ref[...] = jnp.zeros_like(dw_ref)

        dp = jnp.where(_row_valid(tm, tp, i * tm), dp_ref[...], 0.0)
        dw_ref[...] += jnp.dot(xt_ref[...], dp.astype(BF16), preferred_element_type=F32)

    return pl.pallas_call(
        body, grid=(pw // tn, n // tm),
        in_specs=[pl.BlockSpec((tm, tn), lambda j, i: (i, j)), pl.BlockSpec((d, tm), lambda j, i: (0, i))],
        out_specs=pl.BlockSpec((d, tn), lambda j, i: (0, j)), out_shape=SDS((d, pw), F32),
        compiler_params=_params(2), name="proj_bwd_dw")(dproj, xt)


def _conv_silu(x, w, row):
    c = x * w[CONV_K - 1:CONV_K, :]
    for k in range(1, CONV_K):
        c = c + jnp.where(row >= k, pltpu.roll(x, k, axis=0), 0.0) * w[CONV_K - 1 - k:CONV_K - k, :]
    return c


def _gdn_prep_fwd(proj, conv_w, lay, nb, tp):
    n = proj.shape[0]
    nblk = 3 * NH
    cb = lay.c_qkv // HD

    def body(p_ref, w_ref, o_ref):
        j = pl.program_id(1)
        x = p_ref[...]
        row = _iota2(x.shape, 0)
        c = _conv_silu(x, w_ref[...], row)
        s = _silu(c)
        r = lax.rsqrt(_rs(s * s) + EPS)
        scale = jnp.where(j < NH, Q_SCALE, 1.0)
        y = jnp.where(j < 2 * NH, s * r * scale, s)
        o_ref[...] = jnp.where(row >= PAD, y, 0.0)

    return pl.pallas_call(
        body, grid=(nb, nblk),
        in_specs=[pl.BlockSpec((tp, HD), lambda b, j: (b, cb + j)), pl.BlockSpec((CONV_K, HD), lambda b, j: (0, j))],
        out_specs=pl.BlockSpec((tp, HD), lambda b, j: (b, j)), out_shape=SDS((n, nblk * HD), F32),
        compiler_params=_params(2), name="gdn_prep_fwd")(proj, conv_w)


def _gdn_prep_bwd(proj, conv_w, dqkv, dproj, lay, nb, tp):
    nblk = 3 * NH
    cb = lay.c_qkv // HD

    def body(p_ref, w_ref, dy_ref, dp_in, dp_ref, dw_ref):
        j, b = pl.program_id(0), pl.program_id(1)
        x = p_ref[...]
        w = w_ref[...]
        row = _iota2(x.shape, 0)
        c = _conv_silu(x, w, row)
        s = _silu(c)
        dy = jnp.where(row >= PAD, dy_ref[...], 0.0)
        r = lax.rsqrt(_rs(s * s) + EPS)
        nh = s * r
        scale = jnp.where(j < NH, Q_SCALE, 1.0)
        ds_n = scale * r * (dy - nh * _rs(dy * nh))
        ds = jnp.where(j < 2 * NH, ds_n, dy)
        dc = ds * _dsilu(c)
        dx = dc * w[CONV_K - 1:CONV_K, :]
        dws = [jnp.sum(dc * x, axis=0, keepdims=True)]
        for k in range(1, CONV_K):
            dx = dx + jnp.where(row < tp - k, pltpu.roll(dc, tp - k, axis=0), 0.0) * w[CONV_K - 1 - k:CONV_K - k, :]
            xs = jnp.where(row >= k, pltpu.roll(x, k, axis=0), 0.0)
            dws.append(jnp.sum(dc * xs, axis=0, keepdims=True))
        dp_ref[...] = dx.astype(dp_ref.dtype)
        r4 = _iota2((CONV_K, HD), 0)
        dw = jnp.zeros((CONV_K, HD), F32)
        for k in range(CONV_K):
            dw = dw + jnp.where(r4 == CONV_K - 1 - k, dws[k], 0.0)

        @pl.when(b == 0)
        def _():
            dw_ref[...] = dw

        @pl.when(b > 0)
        def _():
            dw_ref[...] += dw

    return pl.pallas_call(
        body, grid=(nblk, nb),
        in_specs=[pl.BlockSpec((tp, HD), lambda j, b: (b, cb + j)), pl.BlockSpec((CONV_K, HD), lambda j, b: (0, j)),
                  pl.BlockSpec((tp, HD), lambda j, b: (b, j)), pl.BlockSpec(memory_space=pl.ANY)],
        out_specs=[pl.BlockSpec((tp, HD), lambda j, b: (b, cb + j)), pl.BlockSpec((CONV_K, HD), lambda j, b: (0, j))],
        out_shape=[SDS(dproj.shape, dproj.dtype), SDS((CONV_K, nblk * HD), F32)],
        input_output_aliases={3: 0}, compiler_params=_params(2), name="gdn_prep_bwd")(proj, conv_w, dqkv, dproj)


def _gate_consts():
    e = np.zeros((HD, 2 * HW), np.float32)
    s = np.zeros((2 * HW, HD), np.float32)
    for h in range(NH):
        e[h, h * HD:(h + 1) * HD] = 1.0
        e[NH + h, HW + h * HD:HW + (h + 1) * HD] = 1.0
        s[h * HD, h] = 1.0
        s[HW + h * HD, NH + h] = 1.0
    return jnp.asarray(e), jnp.asarray(s)


def _gdn_tri():
    i, j = _iota2((CH, CH), 0), _iota2((CH, CH), 1)
    return i >= j, i > j


def _each(fn, *lists):
    return [fn(*xs) for xs in zip(*lists)]


def _tri_inv(a_list, eye):
    p = [-a for a in a_list]
    t = [eye + x for x in p]
    for _ in range(5):
        p = _each(_rnn, p, p)
        tp_ = _each(_rnn, t, p)
        t = _each(lambda x, y: x + y, t, tp_)
    return t


def _gdn_chunks(args, solved=None):
    causal, strict = _gdn_tri()
    eye = jnp.where(_iota2((CH, CH), 0) == _iota2((CH, CH), 1), 1.0, 0.0)
    q, k, v, beta, g, s0 = (list(t) for t in zip(*args))
    gc = [_scan_rows(x, CH) for x in g]
    dm = [jnp.where(causal, jnp.exp(jnp.where(causal, x[:, :CH] - x[:, :CH].T, 0.0)), 0.0) for x in gc]
    ds = [jnp.where(strict, x, 0.0) for x in dm]
    kb = _each(lambda x, y: x * y, k, beta)
    kk = _each(_ent, kb, k)
    a = _each(lambda x, y: x * y, kk, ds)
    eg = [jnp.exp(x) for x in gc]
    rw = _each(lambda x, y: x * y, kb, eg)
    if solved is None:
        tinv = _tri_inv(a, eye)
        rv = _each(lambda x, y: x * y, v, beta)
        u = _each(_rnn, tinv, rv)
        w = _each(_rnn, tinv, rw)
    else:
        tinv, u, w = (list(t) for t in zip(*solved))
    ws = _each(_enn, w, s0)
    vn = _each(lambda x, y: x - y, u, ws)
    qk = _each(_ent, q, k)
    p = _each(lambda x, y: x * y, qk, dm)
    qg = _each(lambda x, y: x * y, q, eg)
    out = []
    for i in range(len(args)):
        gl = gc[i][CH - 1:CH, :]
        ek = jnp.exp(gl - gc[i])
        out.append(dict(gc=gc[i], dm=dm[i], ds=ds[i], kb=kb[i], a=a[i], tinv=tinv[i], eg=eg[i], rw=rw[i], u=u[i], w=w[i],
                        vn=vn[i], p=p[i], qg=qg[i], egl=jnp.exp(gl), ek=ek, kd=k[i] * ek))
    return out


def _gdn_gates(ba, e, alog, dtb):
    raw = _nn(ba, e)
    beta = _sig(raw[:, :HW])
    za = raw[:, HW:] + dtb
    g = -jnp.exp(alog) * _softplus(za)
    return beta, g, za


def _seqs_per_step(nb):
    return 4 if nb % 4 == 0 else (2 if nb % 2 == 0 else 1)


def _gdn_fwd(qkv, proj, e_mat, alog, dtb, lay, nb, nc):
    n = qkv.shape[0]
    tp = n // nb
    cba = lay.c_ba // HD
    gb = _seqs_per_step(nb)

    def body(x_ref, ba_ref, e_ref, al_ref, dt_ref, o_ref, so_ref, sv_ref, s_ref):
        @pl.when(pl.program_id(1) == 0)
        def _():
            s_ref[...] = jnp.zeros_like(s_ref)

        args = []
        for j in range(gb):
            beta, g, _ = _gdn_gates(ba_ref[j], e_ref[...], al_ref[...], dt_ref[...])
            for h in range(NH):
                hs = slice(h * HD, (h + 1) * HD)
                args.append((x_ref[j, :, hs], x_ref[j, :, HW + h * HD:HW + (h + 1) * HD],
                             x_ref[j, :, 2 * HW + h * HD:2 * HW + (h + 1) * HD], beta[:, hs], g[:, hs], s_ref[j, h]))
        cs = _gdn_chunks(args)
        s0s = [a[5] for a in args]
        o1 = _each(lambda c, s0: _enn(c["qg"], s0), cs, s0s)
        o2 = [_enn(c["p"], c["vn"]) for c in cs]
        upd = [_etn(c["kd"], c["vn"]) for c in cs]
        res = [(o1[i] + o2[i], s0s[i] * cs[i]["egl"] + upd[i]) for i in range(len(cs))]
        zero = jnp.zeros((CH, HD - CH), F32)
        for j in range(gb):
            for h in range(NH):
                c = cs[j * NH + h]
                so_ref[j, h] = args[j * NH + h][5]
                sv_ref[j, h] = jnp.concatenate([c["u"], c["w"], c["tinv"], zero], axis=-1)
                s_ref[j, h] = res[j * NH + h][1]
            o_ref[j] = jnp.concatenate([res[j * NH + h][0] for h in range(NH)], axis=-1)

    o, st, sv = pl.pallas_call(
        body, grid=(nb // gb, nc),
        in_specs=[pl.BlockSpec((gb, CH, 3 * HW), lambda b, c: (b, c, 0)), pl.BlockSpec((gb, CH, HD), lambda b, c: (b, c, cba)),
                  pl.BlockSpec((HD, 2 * HW), lambda b, c: (0, 0)), pl.BlockSpec((1, HW), lambda b, c: (0, 0)),
                  pl.BlockSpec((1, HW), lambda b, c: (0, 0))],
        out_specs=[pl.BlockSpec((gb, CH, HW), lambda b, c: (b, c, 0)),
                   pl.BlockSpec((gb, None, NH, HD, HD), lambda b, c: (b, c, 0, 0, 0)),
                   pl.BlockSpec((gb, None, NH, CH, 3 * HD), lambda b, c: (b, c, 0, 0, 0))],
        out_shape=[SDS((nb, tp, HW), F32), SDS((nb, nc, NH, HD, HD), F32), SDS((nb, nc, NH, CH, 3 * HD), F32)],
        scratch_shapes=[pltpu.VMEM((gb, NH, HD, HD), F32)], compiler_params=_params(2), name="gdn_fwd")(
            qkv.reshape(nb, tp, 3 * HW), proj.reshape(nb, tp, -1), e_mat, alog, dtb)
    return o.reshape(n, HW), st, sv


def _gdn_bwd(qkv, proj, e_mat, s_mat, alog, dtb, states, solved, do, dproj, lay, nb, nc):
    n = qkv.shape[0]
    tp = n // nb
    cba = lay.c_ba // HD
    gb = _seqs_per_step(nb)

    def body(x_ref, ba_ref, e_ref, sm_ref, al_ref, dt_ref, st_ref, sv_ref, do_ref, dp_in, dx_ref, dba_ref, acc_ref, ds_ref):
        ci = pl.program_id(1)

        @pl.when(ci == 0)
        def _():
            ds_ref[...] = jnp.zeros_like(ds_ref)

        @pl.when((ci == 0) & (pl.program_id(0) == 0))
        def _():
            acc_ref[...] = jnp.zeros_like(acc_ref)

        causal, strict = _gdn_tri()
        alog = al_ref[...]
        row = _iota2((CH, 1), 0)
        valid = (row >= PAD) | (ci < nc - 1)
        last = row == CH - 1
        gates = [_gdn_gates(ba_ref[j], e_ref[...], alog, dt_ref[...]) for j in range(gb)]
        args, do, ds1, solved = [], [], [], []
        for j in range(gb):
            beta, g, _ = gates[j]
            for h in range(NH):
                hs = slice(h * HD, (h + 1) * HD)
                args.append((x_ref[j, :, hs], x_ref[j, :, HW + h * HD:HW + (h + 1) * HD],
                             x_ref[j, :, 2 * HW + h * HD:2 * HW + (h + 1) * HD], beta[:, hs], g[:, hs], st_ref[j, h]))
                do.append(do_ref[j, :, hs])
                ds1.append(ds_ref[j, h])
                solved.append((sv_ref[j, h, :, 2 * HD:2 * HD + CH], sv_ref[j, h, :, 0:HD], sv_ref[j, h, :, HD:2 * HD]))
        q, k, v, bh, _, s0 = (list(t) for t in zip(*args))
        cs = _gdn_chunks(args, solved)
        get = lambda name: [c[name] for c in cs]
        mul = lambda x, y: x * y
        add = lambda x, y: x + y
        dvn = _each(add, _each(_etn, get("p"), do), _each(_enn, get("kd"), ds1))
        dqg = _each(_ent, do, s0)
        dp = [jnp.where(causal, x, 0.0) for x in _each(_ent, do, get("vn"))]
        dkd = _each(_ent, get("vn"), ds1)
        dw = [-x for x in _each(_ent, dvn, s0)]
        ds_a = _each(_etn, get("qg"), do)
        ds_b = _each(_etn, get("w"), dvn)
        ds_new = [ds_a[i] - ds_b[i] + ds1[i] * cs[i]["egl"] for i in range(len(cs))]
        drv = _each(_rtn, get("tinv"), dvn)
        drw = _each(_rtn, get("tinv"), dw)
        da_1 = _each(_rnt, drv, get("u"))
        da_2 = _each(_rnt, drw, get("w"))
        da = [jnp.where(strict, -(x + y), 0.0) for x, y in zip(da_1, da_2)]
        m = [da[i] * cs[i]["a"] + dp[i] * cs[i]["p"] for i in range(len(cs))]
        dkk = _each(mul, da, get("ds"))
        dqk = _each(mul, dp, get("dm"))
        dq = _each(add, _each(_enn, dqk, k), _each(mul, dqg, get("eg")))
        dkb = _each(add, _each(_enn, dkk, k), _each(mul, drw, get("eg")))
        dk_1 = _each(_etn, dqk, q)
        dk_2 = _each(_etn, dkk, get("kb"))
        dk = [dk_1[i] + dk_2[i] + dkd[i] * cs[i]["ek"] + dkb[i] * bh[i] for i in range(len(cs))]
        dv = _each(mul, drv, bh)
        dbeta, dg = [], []
        for i, c in enumerate(cs):
            dbeta.append(_rs(drv[i] * v[i]) + _rs(dkb[i] * k[i]) + jnp.zeros((CH, HD), F32))
            t_kd = _rs(dkd[i] * c["kd"])
            dgc = _rs(m[i]) - _rs(m[i].T) + _rs(dqg[i] * c["qg"]) + _rs(drw[i] * c["rw"]) - t_kd
            tail = jnp.sum(t_kd, axis=0, keepdims=True) + c["egl"] * jnp.sum(_rs(s0[i] * ds1[i]), axis=0, keepdims=True)
            dgc = dgc + jnp.where(last, tail, 0.0)
            dg.append(_scan_rows(dgc + jnp.zeros((CH, HD), F32), CH, reverse=True))
        r8 = _iota2((8, HW), 0)
        upd = jnp.zeros((8, HW), F32)
        for j in range(gb):
            sl = slice(j * NH, (j + 1) * NH)
            beta, g, za = gates[j]
            for h in range(NH):
                ds_ref[j, h] = ds_new[j * NH + h]
            dx_ref[j] = jnp.concatenate(dq[sl] + dk[sl] + dv[sl], axis=-1)
            dbeta_j = jnp.where(valid, jnp.concatenate(dbeta[sl], axis=-1), 0.0)
            dg_j = jnp.where(valid, jnp.concatenate(dg[sl], axis=-1), 0.0)
            draw_b = dbeta_j * beta * (1.0 - beta)
            draw_a = dg_j * (-jnp.exp(alog)) * _sig(za)
            dba_ref[j] = _nn(jnp.concatenate([draw_b, draw_a], axis=-1), sm_ref[...]).astype(dba_ref.dtype)
            upd = upd + jnp.where(r8 == 0, jnp.sum(dg_j * g, axis=0, keepdims=True), 0.0) + jnp.where(
                r8 == 1, jnp.sum(draw_a, axis=0, keepdims=True), 0.0)
        acc_ref[...] += upd

    rc = lambda c: nc - 1 - c
    dqkv, dproj3, acc = pl.pallas_call(
        body, grid=(nb // gb, nc),
        in_specs=[pl.BlockSpec((gb, CH, 3 * HW), lambda b, c: (b, rc(c), 0)), pl.BlockSpec((gb, CH, HD), lambda b, c: (b, rc(c), cba)),
                  pl.BlockSpec((HD, 2 * HW), lambda b, c: (0, 0)), pl.BlockSpec((2 * HW, HD), lambda b, c: (0, 0)),
                  pl.BlockSpec((1, HW), lambda b, c: (0, 0)), pl.BlockSpec((1, HW), lambda b, c: (0, 0)),
                  pl.BlockSpec((gb, None, NH, HD, HD), lambda b, c: (b, rc(c), 0, 0, 0)),
                  pl.BlockSpec((gb, None, NH, CH, 3 * HD), lambda b, c: (b, rc(c), 0, 0, 0)),
                  pl.BlockSpec((gb, CH, HW), lambda b, c: (b, rc(c), 0)), pl.BlockSpec(memory_space=pl.ANY)],
        out_specs=[pl.BlockSpec((gb, CH, 3 * HW), lambda b, c: (b, rc(c), 0)), pl.BlockSpec((gb, CH, HD), lambda b, c: (b, rc(c), cba)),
                   pl.BlockSpec((8, HW), lambda b, c: (0, 0))],
        out_shape=[SDS((nb, tp, 3 * HW), F32), SDS((nb, tp, dproj.shape[1]), dproj.dtype), SDS((8, HW), F32)],
        input_output_aliases={9: 1},
        scratch_shapes=[pltpu.VMEM((gb, NH, HD, HD), F32)], compiler_params=_params(2), name="gdn_bwd")(
            qkv.reshape(nb, tp, 3 * HW), proj.reshape(nb, tp, -1), e_mat, s_mat, alog, dtb, states, solved, do.reshape(nb, tp, HW),
            dproj.reshape(nb, tp, -1))
    return dqkv.reshape(n, 3 * HW), dproj3.reshape(dproj.shape), acc


def _hgrn_inputs(zq, zf, lb):
    sg = _sig(zf)
    sgn = _sig(-zf)
    pos = lb > 0.0
    lbp = jnp.where(pos, lb, 0.0)
    fpos = lbp + (1.0 - lbp) * sg
    lf = jnp.where(pos, jnp.log(jnp.where(pos, fpos, 1.0)), _logsig(zf))
    k = (1.0 - lbp) * sgn
    q = _silu(zq) * Q_SCALE
    return q, k, lf, sg, sgn, pos, lbp, fpos


def _hgrn_consts():
    i3, j3 = _iota2((SUB, SUB, HD), 0), _iota2((SUB, SUB, HD), 1)
    return i3 >= j3


def _sum_j(x):
    return jnp.sum(x.reshape(SUB, SUB, HD), axis=1)


def _sum_i(x):
    return jnp.sum(x.reshape(SUB, SUB, HD), axis=0)


def _pairs(a, b):
    return (a[:, None, :] * b[None, :, :]).reshape(SUB * SUB, HD)


def _hgrn_sub(q, k, v, bc, st, consts):
    mask3 = consts
    bl = bc[SUB - 1:SUB, :]
    p3 = jnp.where(mask3, jnp.exp(jnp.where(mask3, bc[:, None, :] - bc[None, :, :], 0.0)), 0.0).reshape(SUB * SUB, HD)
    x = _pairs(q, k) * p3
    srep = _rs(x)
    vt = jnp.broadcast_to(v[None, :, :], (SUB, SUB, HD)).reshape(SUB * SUB, HD)
    eb = jnp.exp(bc)
    qe = q * eb
    o = _hnt(qe, st) + _sum_j(_rr(srep) * _rr(vt))
    ek = jnp.exp(bl - bc)
    kd = k * ek
    ebl = jnp.exp(bl)
    st1 = st * ebl + _htn(v, kd)
    return o, st1, dict(bc=bc, p3=p3, srep=srep, vt=vt, eb=eb, qe=qe, ek=ek, kd=kd, ebl=ebl)


def _hgrn_fwd(proj, lb, lay, nb, nc):
    n = proj.shape[0]
    cbb = lay.c_b // (3 * HW)

    def body(z_ref, lb_ref, o_ref, so_ref, s_ref):
        @pl.when(pl.program_id(1) == 0)
        def _():
            s_ref[...] = jnp.zeros_like(s_ref)

        consts = _hgrn_consts()
        outs = []
        for h in range(NH):
            hs = slice(h * HD, (h + 1) * HD)
            q, k, lf = _hgrn_inputs(z_ref[:, hs], z_ref[:, HW + h * HD:HW + (h + 1) * HD], lb_ref[:, hs])[:3]
            v = z_ref[:, 2 * HW + h * HD:2 * HW + (h + 1) * HD]
            st = s_ref[h]
            so_ref[h] = st
            bc = _scan_rows(lf, SUB)
            oh = []
            for s in range(CH // SUB):
                rs = slice(s * SUB, (s + 1) * SUB)
                o, st, _ = _hgrn_sub(q[rs], k[rs], v[rs], bc[rs], st, consts)
                oh.append(o)
            s_ref[h] = st
            outs.append(jnp.concatenate(oh, axis=0))
        o_ref[...] = jnp.concatenate(outs, axis=-1)

    return pl.pallas_call(
        body, grid=(nb, nc),
        in_specs=[pl.BlockSpec((CH, 3 * HW), lambda b, c: (b * nc + c, cbb)), pl.BlockSpec((1, HW), lambda b, c: (0, 0))],
        out_specs=[pl.BlockSpec((CH, HW), lambda b, c: (b * nc + c, 0)),
                   pl.BlockSpec((None, None, NH, HD, HD), lambda b, c: (b, c, 0, 0, 0))],
        out_shape=[SDS((n, HW), F32), SDS((nb, nc, NH, HD, HD), F32)],
        scratch_shapes=[pltpu.VMEM((NH, HD, HD), F32)], compiler_params=_params(2), name="hgrn_fwd")(proj, lb)


def _hgrn_bwd(proj, lb, states, do, dproj, lay, nb, nc):
    cbb = lay.c_b // (3 * HW)
    nsub = CH // SUB

    def rev(b, c):
        return b * nc + (nc - 1 - c)

    def body(z_ref, lb_ref, st_ref, do_ref, dp_in, dz_ref, acc_ref, ds_ref):
        ci = pl.program_id(1)

        @pl.when(ci == 0)
        def _():
            ds_ref[...] = jnp.zeros_like(ds_ref)

        @pl.when((ci == 0) & (pl.program_id(0) == 0))
        def _():
            acc_ref[...] = jnp.zeros_like(acc_ref)

        consts = _hgrn_consts()
        row = _iota2((CH, 1), 0)
        valid = (row >= PAD) | (ci < nc - 1)
        lastrow = _iota2((SUB, 1), 0) == SUB - 1
        dzq, dzf, dzi, dlbs = [], [], [], []
        for h in range(NH):
            hs = slice(h * HD, (h + 1) * HD)
            zq, zf = z_ref[:, hs], z_ref[:, HW + h * HD:HW + (h + 1) * HD]
            q, k, lf, sg, sgn, pos, lbp, fpos = _hgrn_inputs(zq, zf, lb_ref[:, hs])
            v = z_ref[:, 2 * HW + h * HD:2 * HW + (h + 1) * HD]
            doh = do_ref[:, hs]
            sts, fw = [st_ref[h]], []
            bc = _scan_rows(lf, SUB)
            for s in range(nsub):
                rs = slice(s * SUB, (s + 1) * SUB)
                _, st1, c = _hgrn_sub(q[rs], k[rs], v[rs], bc[rs], sts[-1], consts)
                sts.append(st1)
                fw.append(c)
            dst = ds_ref[h]
            dq_l, dk_l, dv_l, dlf_l = [None] * nsub, [None] * nsub, [None] * nsub, [None] * nsub
            for s in reversed(range(nsub)):
                rs = slice(s * SUB, (s + 1) * SUB)
                c, st = fw[s], sts[s]
                qs, ks, vs, dos = q[rs], k[rs], v[rs], doh[rs]
                dqe = _hnn(dos, st)
                dkd = _hnn(vs, dst)
                dsrep = _rs(_pairs(_rr(dos), _rr(vs)))
                w = dsrep * c["p3"]
                kt = jnp.broadcast_to(ks[None, :, :], (SUB, SUB, HD)).reshape(SUB * SUB, HD)
                qt = jnp.broadcast_to(qs[:, None, :], (SUB, SUB, HD)).reshape(SUB * SUB, HD)
                dq_i = _sum_j(w * kt)
                dk_i = _sum_i(w * qt)
                dot = jnp.broadcast_to(_rr(dos)[:, None, :], (SUB, SUB, HD)).reshape(SUB * SUB, HD)
                dvv = _sum_i(_rr(c["srep"]) * dot) + _hnt(c["kd"], dst)
                t_kd = dkd * c["kd"]
                dbc = dqe * c["qe"] - t_kd + qs * dq_i - ks * dk_i
                tail = jnp.sum(t_kd, axis=0, keepdims=True) + c["ebl"] * jnp.sum(st * dst, axis=0, keepdims=True)
                dbc = dbc + jnp.where(lastrow, tail, 0.0)
                dlf_l[s] = dbc
                dq_l[s] = dq_i + dqe * c["eb"]
                dk_l[s] = dk_i + dkd * c["ek"]
                dv_l[s] = dvv
                dst = _htn(dos, c["qe"]) + dst * c["ebl"]
            ds_ref[h] = dst
            dq, dk, dv, dbc = (jnp.concatenate(t, axis=0) for t in (dq_l, dk_l, dv_l, dlf_l))
            dlf = _scan_rows(dbc, SUB, reverse=True)
            dlft = dlf - dk * (1.0 - k)
            dlf_dz = jnp.where(pos, (1.0 - lbp) * sg * sgn / jnp.where(pos, fpos, 1.0), sgn)
            dlf_dlb = jnp.where(pos, sgn / jnp.where(pos, fpos, 1.0), 0.0)
            dzq.append(dq * Q_SCALE * _dsilu(zq))
            dzf.append(dlft * dlf_dz)
            dzi.append(dv)
            dlbs.append(jnp.sum(jnp.where(valid, dlft * dlf_dlb, 0.0), axis=0, keepdims=True))
        dz_ref[...] = jnp.concatenate(dzq + dzf + dzi, axis=-1).astype(dz_ref.dtype)
        acc_ref[...] += jnp.where(_iota2((8, HW), 0) == 0, jnp.concatenate(dlbs, axis=-1), 0.0)

    return pl.pallas_call(
        body, grid=(nb, nc),
        in_specs=[pl.BlockSpec((CH, 3 * HW), lambda b, c: (rev(b, c), cbb)), pl.BlockSpec((1, HW), lambda b, c: (0, 0)),
                  pl.BlockSpec((None, None, NH, HD, HD), lambda b, c: (b, nc - 1 - c, 0, 0, 0)),
                  pl.BlockSpec((CH, HW), lambda b, c: (rev(b, c), 0)), pl.BlockSpec(memory_space=pl.ANY)],
        out_specs=[pl.BlockSpec((CH, 3 * HW), lambda b, c: (rev(b, c), cbb)), pl.BlockSpec((8, HW), lambda b, c: (0, 0))],
        out_shape=[SDS(dproj.shape, dproj.dtype), SDS((8, HW), F32)],
        input_output_aliases={4: 0},
        scratch_shapes=[pltpu.VMEM((NH, HD, HD), F32)], compiler_params=_params(2), name="hgrn_bwd")(proj, lb, states, do, dproj)


def _gated_norm(o, z, gamma):
    ys, ns, rs = [], [], []
    for h in range(NH):
        hs = slice(h * HD, (h + 1) * HD)
        oh = o[:, hs]
        r = lax.rsqrt(jnp.mean(oh * oh, axis=-1, keepdims=True) + EPS)
        nh = oh * r
        ys.append(nh * gamma * _silu(z[:, hs]))
        ns.append(nh)
        rs.append(r)
    return jnp.concatenate(ys, axis=-1), ns, rs


def _merge_fwd(h, oa, ob, proj, ga, gb, wa, wb, wo, lay):
    n, d = h.shape
    tm = _tile(n, 384)
    wm = lay.wm

    def body(h_ref, oa_ref, ob_ref, p_ref, ga_ref, gb_ref, wa_ref, wb_ref, wo_ref, out_ref):
        ya, _, _ = _gated_norm(oa_ref[...], p_ref[:, 0:HW], ga_ref[...])
        yb, _, _ = _gated_norm(ob_ref[...], p_ref[:, HW:2 * HW], gb_ref[...])
        ya2 = _bnn(ya, wa_ref[...])
        yb2 = _bnn(yb, wb_ref[...])
        mixed = _sig(p_ref[:, 2 * HW:2 * HW + d]) * ya2 + _sig(p_ref[:, 2 * HW + d:2 * HW + 2 * d]) * yb2
        out_ref[...] = h_ref[...] + _bnn(mixed, wo_ref[...])

    full = lambda shape: pl.BlockSpec(shape, lambda i: (0, 0))
    return pl.pallas_call(
        body, grid=(n // tm,),
        in_specs=[pl.BlockSpec((tm, d), lambda i: (i, 0)), pl.BlockSpec((tm, HW), lambda i: (i, 0)),
                  pl.BlockSpec((tm, HW), lambda i: (i, 0)), pl.BlockSpec((tm, wm), lambda i: (i, 0)),
                  full((1, HD)), full((1, HD)), full((HW, d)), full((HW, d)), full((d, d))],
        out_specs=pl.BlockSpec((tm, d), lambda i: (i, 0)), out_shape=SDS((n, d), F32),
        compiler_params=_params(1), name="merge_fwd")(h, oa, ob, proj, ga, gb, wa, wb, wo)


def _gated_norm_bwd(dy, o, z, gamma):
    dos, dzs = [], []
    dgam = jnp.zeros((1, HD), F32)
    for h in range(NH):
        hs = slice(h * HD, (h + 1) * HD)
        oh, zh, dyh = o[:, hs], z[:, hs], dy[:, hs]
        r = lax.rsqrt(jnp.mean(oh * oh, axis=-1, keepdims=True) + EPS)
        nh = oh * r
        dzs.append(dyh * nh * gamma * _dsilu(zh))
        dng = dyh * _silu(zh)
        dgam = dgam + jnp.sum(dng * nh, axis=0, keepdims=True)
        dn = dng * gamma
        dos.append(r * (dn - nh * jnp.mean(dn * nh, axis=-1, keepdims=True)))
    return jnp.concatenate(dos, axis=-1), jnp.concatenate(dzs, axis=-1), dgam


def _merge_bwd(dhn, oa, ob, proj, ga, gb, wa, wb, wo, lay, tp):
    n, d = dhn.shape
    tm = _tile(n, 256)
    wm = lay.wm

    def body(dh_ref, oa_ref, ob_ref, p_ref, ga_ref, gb_ref, wa_ref, wb_ref, wo_ref,
             dp_ref, doa_ref, dob_ref, dwa_ref, dwb_ref, dwo_ref, dga_ref, dgb_ref):
        i = pl.program_id(0)

        @pl.when(i == 0)
        def _():
            for r in (dwa_ref, dwb_ref, dwo_ref, dga_ref, dgb_ref):
                r[...] = jnp.zeros_like(r)

        dh = jnp.where(_row_valid(tm, tp, i * tm), dh_ref[...], 0.0)
        oa, ob = oa_ref[...], ob_ref[...]
        za, zb = p_ref[:, 0:HW], p_ref[:, HW:2 * HW]
        gta, gtb = p_ref[:, 2 * HW:2 * HW + d], p_ref[:, 2 * HW + d:2 * HW + 2 * d]
        ya, _, _ = _gated_norm(oa, za, ga_ref[...])
        yb, _, _ = _gated_norm(ob, zb, gb_ref[...])
        ya2 = _bnn(ya, wa_ref[...])
        yb2 = _bnn(yb, wb_ref[...])
        sa, sb = _sig(gta), _sig(gtb)
        mixed = sa * ya2 + sb * yb2
        dmixed = _bnt(dh, wo_ref[...])
        dwo_ref[...] += _btn(mixed, dh)
        dya2 = dmixed * sa
        dyb2 = dmixed * sb
        dwa_ref[...] += _btn(ya, dya2)
        dwb_ref[...] += _btn(yb, dyb2)
        doa, dza, dga = _gated_norm_bwd(_bnt(dya2, wa_ref[...]), oa, za, ga_ref[...])
        dob, dzb, dgb = _gated_norm_bwd(_bnt(dyb2, wb_ref[...]), ob, zb, gb_ref[...])
        dga_ref[...] += dga
        dgb_ref[...] += dgb
        doa_ref[...] = doa
        dob_ref[...] = dob
        dt = dp_ref.dtype
        dp_ref[:, 0:HW] = dza.astype(dt)
        dp_ref[:, HW:2 * HW] = dzb.astype(dt)
        dp_ref[:, 2 * HW:2 * HW + d] = (dmixed * ya2 * sa * (1.0 - sa)).astype(dt)
        dp_ref[:, 2 * HW + d:2 * HW + 2 * d] = (dmixed * yb2 * sb * (1.0 - sb)).astype(dt)

    full = lambda shape: pl.BlockSpec(shape, lambda i: (0, 0))
    rows = lambda w: pl.BlockSpec((tm, w), lambda i: (i, 0))
    return pl.pallas_call(
        body, grid=(n // tm,),
        in_specs=[rows(d), rows(HW), rows(HW), rows(wm), full((1, HD)), full((1, HD)), full((HW, d)), full((HW, d)), full((d, d))],
        out_specs=[rows(wm), rows(HW), rows(HW), full((HW, d)), full((HW, d)), full((d, d)), full((1, HD)), full((1, HD))],
        out_shape=[SDS((n, lay.pw), BF16), SDS((n, HW), F32), SDS((n, HW), F32), SDS((HW, d), F32), SDS((HW, d), F32),
                   SDS((d, d), F32), SDS((1, HD), F32), SDS((1, HD), F32)],
        compiler_params=_params(1), name="merge_bwd")(dhn, oa, ob, proj, ga, gb, wa, wb, wo)


def _loss_head(h, target, fw, nb, nc):
    n, d = h.shape

    def body(h_ref, t_ref, fw_ref, lp_ref, dh_ref, dfw_ref):
        b, c = pl.program_id(0), pl.program_id(1)

        @pl.when((b == 0) & (c == 0))
        def _():
            dfw_ref[...] = jnp.zeros_like(dfw_ref)

        @pl.when(c == 0)
        def _():
            dh_ref[...] = jnp.zeros_like(dh_ref)
            lp_ref[...] = jnp.zeros_like(lp_ref)

        @pl.when(c > 0)
        def _():
            x = h_ref[...]
            r = lax.rsqrt(jnp.mean(x * x, axis=-1, keepdims=True) + EPS)
            xh = x * r
            err = xh * fw_ref[...] - t_ref[...]
            lp_ref[...] = jnp.zeros_like(lp_ref) + 0.5 * jnp.sum(_rs(err * err), axis=0, keepdims=True) / d
            dy = err / d
            dfw_ref[...] += jnp.sum(dy * xh, axis=0, keepdims=True)
            dxh = dy * fw_ref[...]
            dh_ref[...] = r * (dxh - xh * jnp.mean(dxh * xh, axis=-1, keepdims=True))

    return pl.pallas_call(
        body, grid=(nb, nc),
        in_specs=[pl.BlockSpec((CH, d), lambda b, c: (b * nc + c, 0)),
                  pl.BlockSpec((CH, d), lambda b, c: (b * (nc - 1) + jnp.maximum(c - 1, 0), 0)),
                  pl.BlockSpec((1, d), lambda b, c: (0, 0))],
        out_specs=[pl.BlockSpec((8, HD), lambda b, c: (b * nc + c, 0)), pl.BlockSpec((CH, d), lambda b, c: (b * nc + c, 0)),
                   pl.BlockSpec((1, d), lambda b, c: (0, 0))],
        out_shape=[SDS((nb * nc * 8, HD), F32), SDS((n, d), F32), SDS((1, d), F32)],
        compiler_params=_params(2), name="loss_head")(h, target, fw)


def _lb_fwd(lb):
    def body(x_ref, o_ref):
        x = x_ref[...]
        mx = jnp.max(x, axis=0, keepdims=True)
        e = jnp.exp(x - mx)
        sm = e / jnp.sum(e, axis=0, keepdims=True)
        run = jnp.zeros((1, HW), F32)
        for l in range(DEPTH):
            run = run + sm[l:l + 1, :]
            o_ref[l:l + 1, :] = run - sm[0:1, :]

    return pl.pallas_call(body, out_shape=SDS(lb.shape, F32), name="lb_fwd")(lb)


def _lb_bwd(lb, dlb_all):
    def body(x_ref, d_ref, o_ref):
        x = x_ref[...]
        dl = d_ref[...]
        mx = jnp.max(x, axis=0, keepdims=True)
        e = jnp.exp(x - mx)
        sm = e / jnp.sum(e, axis=0, keepdims=True)
        tot = jnp.sum(dl, axis=0, keepdims=True)
        dsm = []
        run = tot
        for l in range(DEPTH):
            dsm.append(run - (tot if l == 0 else 0.0))
            run = run - dl[l:l + 1, :]
        inner = sum(sm[l:l + 1, :] * dsm[l] for l in range(DEPTH))
        for l in range(DEPTH):
            o_ref[l:l + 1, :] = sm[l:l + 1, :] * (dsm[l] - inner)

    return pl.pallas_call(body, out_shape=SDS(lb.shape, F32), name="lb_bwd")(lb, dlb_all)


def _adamw(g, w, m, v):
    r, c = g.shape
    tr = _tile(r, 264)
    c1 = 1.0 / (1.0 - ADAM_B1 ** ADAM_STEP)
    c2 = 1.0 / (1.0 - ADAM_B2 ** ADAM_STEP)

    def body(g_ref, w_ref, m_ref, v_ref, d_ref, mo_ref, vo_ref):
        gg = g_ref[...]
        mn = ADAM_B1 * m_ref[...] + (1.0 - ADAM_B1) * gg
        vn = ADAM_B2 * v_ref[...] + (1.0 - ADAM_B2) * gg * gg
        d_ref[...] = -ADAM_LR * ((mn * c1) / (jnp.sqrt(vn * c2) + ADAM_EPS) + ADAM_WD * w_ref[...])
        mo_ref[...] = mn
        vo_ref[...] = vn

    spec = pl.BlockSpec((tr, c), lambda i: (i, 0))
    return pl.pallas_call(body, grid=(r // tr,), in_specs=[spec] * 4, out_specs=[spec] * 3, out_shape=[SDS(g.shape, F32)] * 3,
                          compiler_params=_params(1), name="adamw")(g, w, m, v)


def _tile16(n, target):
    return _tile(n // 2, target // 2) * 2 if n % 16 == 0 else _tile(n, target)


def _add_cores(g, got, core):
    k, r, c = got.shape
    tr = _tile16(r, 264)

    def body(c_ref, a_ref, b_ref, o_ref):
        o_ref[...] = (a_ref[...] + b_ref[...].astype(F32)).astype(o_ref.dtype)

    spec = pl.BlockSpec((None, tr, c), lambda s, i, cr: (s, i, 0))
    return pl.pallas_call(
        body, grid_spec=pltpu.PrefetchScalarGridSpec(
            num_scalar_prefetch=1, grid=(k, r // tr),
            in_specs=[pl.BlockSpec((None, None, tr, c), lambda s, i, cr: (cr[0], s, i, 0)), spec], out_specs=spec),
        out_shape=SDS(got.shape, got.dtype), compiler_params=_params(2), name="add_cores")(core, g, got)


def _sum_chips(parts, own, place):
    k, r, c = parts.shape
    tr = _tile16(r, 264)

    def body(p_ref, *refs):
        part_refs, own_ref, o_ref = refs[:k], refs[k], refs[k + 1]
        mine = own_ref[...].astype(F32)
        acc = None
        for s in range(k):
            term = jnp.where(p_ref[0] == s, mine, part_refs[s][...].astype(F32))
            acc = term if acc is None else acc + term
        o_ref[...] = acc

    slots = jnp.stack([jnp.where(place[0] == s, (s + 1) % k, s) for s in range(k)]).astype(jnp.int32)
    other = lambda s: pl.BlockSpec((None, tr, c), lambda i, p: (p[2 + s], i, 0))
    return pl.pallas_call(
        body, grid_spec=pltpu.PrefetchScalarGridSpec(
            num_scalar_prefetch=1, grid=(r // tr,),
            in_specs=[other(s) for s in range(k)] + [pl.BlockSpec((None, tr, c), lambda i, p: (p[0], i, 0))],
            out_specs=pl.BlockSpec((None, tr, c), lambda i, p: (p[1], i, 0))),
        out_shape=SDS((2, r, c), F32), compiler_params=_params(1), name="sum_chips")(
            jnp.concatenate([place, slots]), *([parts] * k), own)


def _meta_grad(dh, nb, nc):
    d = dh.shape[1]

    def body(x_ref, o_ref):
        @pl.when(pl.program_id(0) == 0)
        def _():
            o_ref[...] = jnp.zeros_like(o_ref)

        o_ref[...] += x_ref[PAD:CH, :]

    return pl.pallas_call(body, grid=(nb,), in_specs=[pl.BlockSpec((CH, d), lambda b: (b * nc, 0))],
                          out_specs=pl.BlockSpec((N_META, d), lambda b: (0, 0)), out_shape=SDS((N_META, d), F32),
                          compiler_params=_params(1), name="meta_grad")(dh)


ANY = pl.BlockSpec(memory_space=pl.ANY)


def _place():
    x, y, c = lax.axis_index("x"), lax.axis_index("y"), lax.axis_index("c")
    chips = [(1 - x, y), (x, 1 - y), (1 - x, 1 - y)]
    return x, y, c, chips


def _remote(src, dst, send_sems, recv_sems, k, to):
    return pltpu.make_async_remote_copy(src_ref=src, dst_ref=dst, send_sem=send_sems.at[k], recv_sem=recv_sems.at[k],
                                        device_id=to, device_id_type=MESH)


def _gather_weights(pbs, ps):
    nt = len(pbs)

    def body(*refs):
        pb_refs, ps_ref, gb_refs, gs_ref = refs[:nt], refs[nt], refs[nt + 1:2 * nt + 1], refs[2 * nt + 1]
        send_sems, recv_sems, local_sems = refs[2 * nt + 2:]
        x, y, c, chips = _place()
        s = 2 * x + y
        sib = (x, y, 1 - c)
        l1 = pltpu.make_async_copy(ps_ref, gs_ref.at[s], local_sems.at[0])
        l1.start()
        sends = []
        for k, (px, py) in enumerate(chips):
            for t in range(nt):
                sends.append(_remote(pb_refs[t].at[c], gb_refs[t].at[s, c], send_sems, recv_sems, 6 * t + k, (px, py, c)))
            sends.append(_remote(ps_ref, gs_ref.at[s], send_sems, recv_sems, 6 * nt + k, (px, py, c)))
        for cp in sends:
            cp.start()
        for k, (px, py) in enumerate(chips):
            sk = 2 * px + py
            for t in range(nt):
                _remote(pb_refs[t].at[c], gb_refs[t].at[sk, c], send_sems, recv_sems, 6 * t + k, sib).wait_recv()
                fwd = _remote(gb_refs[t].at[sk, c], gb_refs[t].at[sk, c], send_sems, recv_sems, 6 * t + 3 + k, sib)
                fwd.start()
                sends.append(fwd)
        for k, (px, py) in enumerate(chips):
            sk = 2 * px + py
            for t in range(nt):
                _remote(pb_refs[t].at[c], gb_refs[t].at[sk, 1 - c], send_sems, recv_sems, 6 * t + 3 + k, sib).wait_recv()
            _remote(ps_ref, gs_ref.at[sk], send_sems, recv_sems, 6 * nt + k, sib).wait_recv()
        for cp in sends:
            cp.wait_send()
        l1.wait()

    nsem = 6 * nt + 3
    out = pl.pallas_call(
        body, in_specs=[ANY] * (nt + 1), out_specs=[ANY] * (nt + 1),
        out_shape=[SDS((4,) + pb.shape, pb.dtype) for pb in pbs] + [SDS((4,) + ps.shape, ps.dtype)],
        scratch_shapes=[pltpu.SemaphoreType.DMA((nsem,)), pltpu.SemaphoreType.DMA((nsem,)), pltpu.SemaphoreType.DMA((1,))],
        name="gather_weights")(*pbs, ps)
    return out[:nt], out[nt]


def _contain(wpad, shift):
    r, cw = wpad.shape
    tr = _tile16(r, 256)

    def body(n_ref, x_ref, o_ref):
        o_ref[...] = pltpu.roll(x_ref[...], n_ref[0], axis=1).astype(o_ref.dtype)

    spec = pl.BlockSpec((tr, cw), lambda i, n: (i, 0))
    return pl.pallas_call(
        body, grid_spec=pltpu.PrefetchScalarGridSpec(num_scalar_prefetch=1, grid=(r // tr,), in_specs=[spec], out_specs=spec),
        out_shape=SDS((r, cw), BF16), compiler_params=_params(1), name="contain")(shift, wpad)


def _place_own(gb, pb, chip):
    _, _, r, c = gb.shape
    tr = _tile16(r, 1100)

    def body(s_ref, p_ref, g_in, o_ref):
        o_ref[...] = p_ref[...]

    return pl.pallas_call(
        body, grid_spec=pltpu.PrefetchScalarGridSpec(
            num_scalar_prefetch=1, grid=(2, r // tr),
            in_specs=[pl.BlockSpec((None, tr, c), lambda h, i, s: (h, i, 0)), ANY],
            out_specs=pl.BlockSpec((None, None, tr, c), lambda h, i, s: (s[0], h, i, 0))),
        out_shape=SDS(gb.shape, gb.dtype), input_output_aliases={2: 0}, compiler_params=_params(2),
        name="place_own")(chip, pb, gb)


def _sem_scratch(n_remote, n_local):
    return [pltpu.SemaphoreType.DMA((n_remote,)), pltpu.SemaphoreType.DMA((n_remote,)), pltpu.SemaphoreType.DMA((n_local,))]


def _swap_halves(sends):
    nt = len(sends)

    def body(*refs):
        s_refs, got_refs = refs[:nt], refs[nt:2 * nt]
        send_sems, recv_sems = refs[2 * nt:]
        x, y, c, _ = _place()
        sib = (x, y, 1 - c)
        remote = [_remote(s_refs[t].at[1 - c, s], got_refs[t].at[s], send_sems, recv_sems, 4 * t + s, sib)
                  for t in range(nt) for s in range(4)]
        for cp in remote:
            cp.start()
        for cp in remote:
            cp.wait()

    return pl.pallas_call(
        body, in_specs=[ANY] * nt, out_specs=[ANY] * nt, out_shape=[SDS(g.shape[1:], g.dtype) for g in sends],
        scratch_shapes=[pltpu.SemaphoreType.DMA((4 * nt,)), pltpu.SemaphoreType.DMA((4 * nt,))], name="swap_halves")(*sends)


def _scatter_chip_sums(parts):
    nt = len(parts)

    def body(*refs):
        a_refs, r_refs = refs[:nt], refs[nt:2 * nt]
        send_sems, recv_sems = refs[2 * nt:]
        x, y, c, chips = _place()
        s = 2 * x + y
        sends = [_remote(a_refs[t].at[2 * px + py], r_refs[t].at[s], send_sems, recv_sems, 3 * t + k, (px, py, c))
                 for t in range(nt) for k, (px, py) in enumerate(chips)]
        for cp in sends:
            cp.start()
        for t in range(nt):
            for k, (px, py) in enumerate(chips):
                _remote(a_refs[t].at[s], r_refs[t].at[2 * px + py], send_sems, recv_sems, 3 * t + k, (px, py, c)).wait_recv()
        for cp in sends:
            cp.wait_send()

    return pl.pallas_call(
        body, in_specs=[ANY] * nt, out_specs=[ANY] * nt, out_shape=[SDS(a.shape, a.dtype) for a in parts],
        scratch_shapes=[pltpu.SemaphoreType.DMA((3 * nt,)), pltpu.SemaphoreType.DMA((3 * nt,))],
        name="scatter_chip_sums")(*parts)


def _join_halves(fs):
    nt = len(fs)

    def body(*refs):
        f_refs = refs[nt:2 * nt]
        send_sems, recv_sems = refs[2 * nt:]
        x, y, c, _ = _place()
        sib = (x, y, 1 - c)
        sends = [_remote(f_refs[t].at[c], f_refs[t].at[c], send_sems, recv_sems, t, sib) for t in range(nt)]
        for cp in sends:
            cp.start()
        for t in range(nt):
            _remote(f_refs[t].at[c], f_refs[t].at[1 - c], send_sems, recv_sems, t, sib).wait_recv()
        for cp in sends:
            cp.wait_send()

    return pl.pallas_call(
        body, in_specs=[ANY] * nt, out_specs=[ANY] * nt, out_shape=[SDS(f.shape, f.dtype) for f in fs],
        input_output_aliases={t: t for t in range(nt)},
        scratch_shapes=[pltpu.SemaphoreType.DMA((nt,)), pltpu.SemaphoreType.DMA((nt,))], name="join_halves")(*fs)


def _uncontain(cont, n_head, width):
    r, cw = cont.shape
    tr = _tile(r, 256)

    def body(n_ref, x_ref, o_ref):
        o_ref[...] = pltpu.roll(x_ref[...], n_ref[0], axis=1)[:, :width]

    return pl.pallas_call(
        body, grid_spec=pltpu.PrefetchScalarGridSpec(
            num_scalar_prefetch=1, grid=(r // tr,), in_specs=[pl.BlockSpec((tr, cw), lambda i, n: (i, 0))],
            out_specs=pl.BlockSpec((tr, width), lambda i, n: (i, 0))),
        out_shape=SDS((r, width), F32), compiler_params=_params(1), name="uncontain")(n_head, cont)


WEIGHTS = ("meta_tokens", "norm_w", "w_in", "conv_w", "a_log", "dt_bias", "gnorm_a", "gnorm_b", "hgrn_lower_bounds",
           "w_branch_a", "w_branch_b", "w_out", "final_norm_w")
SHARD_AXIS = {"meta_tokens": 1, "w_in": 2, "conv_w": 2, "w_branch_a": 2, "w_branch_b": 2, "w_out": 1}
FLAT_C = 1024


def _flat(parts, rows, cols=FLAT_C):
    v = jnp.concatenate([p.reshape(-1) for p in parts])
    return jnp.pad(v, (0, rows * cols - v.shape[0])).reshape(rows, cols)


def _local_step(x, target, w, lay):
    nb, seq, d = x.shape
    tp = CH + seq
    nc = tp // CH
    n = nb * tp
    e_mat, s_mat = _gate_consts()
    lb_all = _lb_fwd(w["hgrn_lower_bounds"])
    h = jnp.concatenate([jnp.zeros((nb, PAD, d), F32), jnp.broadcast_to(w["meta_tokens"][None], (nb, N_META, d)), x],
                        axis=1).reshape(n, d)
    rep = lambda a: jnp.repeat(a, HD)[None, :]
    saved = []
    for l in range(DEPTH):
        nw = w["norm_w"][l][None, :]
        proj, xn = _norm_proj_fwd(h, nw, w["w_in"][l])
        qkv = _gdn_prep_fwd(proj, w["conv_w"][l], lay, nb, tp)
        alog, dtb = rep(w["a_log"][l]), rep(w["dt_bias"][l])
        oa, sa, sva = _gdn_fwd(qkv, proj, e_mat, alog, dtb, lay, nb, nc)
        lbl = lb_all[l][None, :]
        ob, sb = _hgrn_fwd(proj, lbl, lay, nb, nc)
        ga, gb = w["gnorm_a"][l][None, :], w["gnorm_b"][l][None, :]
        hn = _merge_fwd(h, oa, ob, proj, ga, gb, w["w_branch_a"][l], w["w_branch_b"][l], w["w_out"][l], lay)
        saved.append((h, nw, proj, qkv, alog, dtb, oa, sa, lbl, ob, sb, ga, gb, xn, sva))
        h = hn
    lp, dh, dfw = _loss_head(h, target.reshape(nb * seq, d), w["final_norm_w"][None, :], nb, nc)
    loss = jnp.sum(lp[::8, 0])
    g = {n_: [None] * DEPTH for n_ in WEIGHTS}
    dlb_all = [None] * DEPTH
    for l in reversed(range(DEPTH)):
        h, nw, proj, qkv, alog, dtb, oa, sa, lbl, ob, sb, ga, gb, xn, sva = saved[l]
        dproj, doa, dob, dwa, dwb, dwo, dga, dgb = _merge_bwd(dh, oa, ob, proj, ga, gb, w["w_branch_a"][l],
                                                             w["w_branch_b"][l], w["w_out"][l], lay, tp)
        dproj, acc_b = _hgrn_bwd(proj, lbl, sb, dob, dproj, lay, nb, nc)
        dqkv, dproj, acc_a = _gdn_bwd(qkv, proj, e_mat, s_mat, alog, dtb, sa, sva, doa, dproj, lay, nb, nc)
        dproj, dconv = _gdn_prep_bwd(proj, w["conv_w"][l], dqkv, dproj, lay, nb, tp)
        dh, dnw = _proj_bwd_dx(dproj, w["w_in"][l].T, h, nw, dh, tp)
        g["w_in"][l] = _proj_bwd_dw(dproj, xn, tp)
        g["norm_w"][l] = dnw[0]
        g["conv_w"][l] = dconv
        g["a_log"][l] = acc_a[0, ::HD]
        g["dt_bias"][l] = acc_a[1, ::HD]
        g["gnorm_a"][l], g["gnorm_b"][l] = dga[0], dgb[0]
        g["w_branch_a"][l], g["w_branch_b"][l], g["w_out"][l] = dwa, dwb, dwo
        dlb_all[l] = acc_b[0]
    grads = {n_: jnp.stack(v) for n_, v in g.items() if v[0] is not None}
    grads["hgrn_lower_bounds"] = _lb_bwd(w["hgrn_lower_bounds"], jnp.stack(dlb_all))
    grads["final_norm_w"] = dfw[0]
    grads["meta_tokens"] = _meta_grad(dh, nb, nc)
    grad_x = dh.reshape(nb, tp, d)[:, CH:, :]
    return loss, grad_x, grads


def kernel(x, meta_tokens, norm_w, w_in, conv_w, a_log, dt_bias, gnorm_a, gnorm_b, hgrn_lower_bounds, w_branch_a, w_branch_b, w_out, final_norm_w, loss_target, m_meta_tokens, m_norm_w, m_w_in, m_conv_w, m_a_log, m_dt_bias, m_gnorm_a, m_gnorm_b, m_hgrn_lower_bounds, m_w_branch_a, m_w_branch_b, m_w_out, m_final_norm_w, v_meta_tokens, v_norm_w, v_w_in, v_conv_w, v_a_log, v_dt_bias, v_gnorm_a, v_gnorm_b, v_hgrn_lower_bounds, v_w_branch_a, v_w_branch_b, v_w_out, v_final_norm_w):
    wl = dict(meta_tokens=meta_tokens, norm_w=norm_w, w_in=w_in, conv_w=conv_w, a_log=a_log, dt_bias=dt_bias, gnorm_a=gnorm_a,
              gnorm_b=gnorm_b, hgrn_lower_bounds=hgrn_lower_bounds, w_branch_a=w_branch_a, w_branch_b=w_branch_b, w_out=w_out,
              final_norm_w=final_norm_w)
    ml = dict(zip(WEIGHTS, (m_meta_tokens, m_norm_w, m_w_in, m_conv_w, m_a_log, m_dt_bias, m_gnorm_a, m_gnorm_b,
                            m_hgrn_lower_bounds, m_w_branch_a, m_w_branch_b, m_w_out, m_final_norm_w)))
    vl = dict(zip(WEIGHTS, (v_meta_tokens, v_norm_w, v_w_in, v_conv_w, v_a_log, v_dt_bias, v_gnorm_a, v_gnorm_b,
                            v_hgrn_lower_bounds, v_w_branch_a, v_w_branch_b, v_w_out, v_final_norm_w)))
    d = x.shape[2]
    lay = _Layout(d)
    nchip = 4

    big = ("w_in", "w_branch_a", "w_branch_b", "w_out")
    small = ("conv_w", "meta_tokens")
    table, heads, cw = lay.pieces(nchip)
    sw = wl["w_in"].shape[2]
    chip_id = (2 * lax.axis_index("x") + lax.axis_index("y")).astype(jnp.int32)
    n_head = sum(jnp.where(chip_id == s, heads[s], 0) for s in range(nchip)).astype(jnp.int32)
    w_pad = jnp.pad(wl["w_in"], ((0, 0), (0, 0), (0, cw - sw))).reshape(DEPTH * d, cw)
    shift = jnp.where(n_head == 0, 0, cw - n_head).astype(jnp.int32).reshape(1)
    pbs = [_contain(w_pad, shift).reshape(DEPTH, d, cw)] + [wl[n].astype(BF16) for n in big[1:]]
    nsmall = sum(int(np.prod(wl[n].shape)) for n in small)
    rs = -(-nsmall // (HD * 8)) * 8
    ps = jnp.pad(jnp.concatenate([wl[n].reshape(-1) for n in small]), (0, rs * HD - nsmall)).reshape(rs, HD)
    gbig, gsmall = _gather_weights(pbs, ps)
    gbig = [_place_own(g, p, chip_id.reshape(1)) for g, p in zip(gbig, pbs)]
    gsmall = gsmall.reshape(nchip, -1)

    wf = dict(wl)
    wf["w_in"] = lay.from_containers([gbig[0][s] for s in range(nchip)])
    for i, n in enumerate(big[1:], start=1):
        wf[n] = jnp.concatenate([gbig[i][s] for s in range(nchip)], axis=SHARD_AXIS[n])
    o = 0
    for n in small:
        sz = int(np.prod(wl[n].shape))
        a = gsmall[:, o:o + sz].reshape((nchip,) + wl[n].shape)
        wf[n] = jnp.concatenate([a[s] for s in range(nchip)], axis=SHARD_AXIS[n])
        o += sz

    loss_part, grad_x, gfull = _local_step(x, loss_target, wf, lay)
    loss = lax.psum(loss_part, ("x", "y", "c"))

    sw = wl["w_in"].shape[2]
    conts, heads = lay.containers(gfull["w_in"], nchip)
    dd = wl["w_branch_a"].shape[2]
    rows_o = wl["w_out"].shape[1]
    by_dest = lambda g, n: [lax.slice_in_dim(g, s * wl[n].shape[SHARD_AXIS[n]], (s + 1) * wl[n].shape[SHARD_AXIS[n]],
                                            axis=SHARD_AXIS[n]) if n in SHARD_AXIS else g for s in range(nchip)]
    small_names = tuple(n for n in WEIGHTS if n not in big)
    nsm = sum(int(np.prod(wl[n].shape)) for n in small_names)
    rsm = -(-nsm // (2 * HD * 8)) * 8
    pack_small = lambda parts: _flat(parts, 2 * rsm, HD).reshape(2, rsm, HD)
    small_by_dest = [by_dest(gfull[n], n) for n in small_names]
    gs = [jnp.stack(conts, axis=1),
          jnp.stack(by_dest(gfull["w_branch_a"], "w_branch_a"), axis=1),
          jnp.stack(by_dest(gfull["w_branch_b"], "w_branch_b"), axis=1),
          gfull["w_out"].reshape(DEPTH, nchip, rows_o, d),
          jnp.stack([pack_small([p[s] for p in small_by_dest]) for s in range(nchip)], axis=1)]
    gs = [g.reshape((2, nchip, -1, g.shape[-1])) for g in gs]
    my_chip = (2 * lax.axis_index("x") + lax.axis_index("y")).astype(jnp.int32)
    my_core = lax.axis_index("c").astype(jnp.int32)
    got = _swap_halves([g.astype(BF16) for g in gs[:4]] + gs[4:])
    chip_sums = [_add_cores(g, b, my_core.reshape(1)) for g, b in zip(gs, got)]
    by_chip = _scatter_chip_sums(chip_sums)
    place = jnp.stack([my_chip, my_core])
    full = _join_halves([_sum_chips(p, a, place) for p, a in zip(by_chip, chip_sums)])
    n_head = sum(jnp.where(my_chip == s, heads[s], 0) for s in range(nchip)).astype(jnp.int32).reshape(1)
    g_w_in = _uncontain(full[0].reshape(DEPTH * d, -1), n_head, sw)
    g2 = {"w_in": g_w_in, "w_branch_a": full[1].reshape(-1, dd), "w_branch_b": full[2].reshape(-1, dd),
          "w_out": full[3].reshape(-1, d), "small": full[4].reshape(2 * rsm, HD)}

    def two_d(src, n):
        if n == "small":
            return _flat([src[k] for k in small_names], 2 * rsm, HD)
        return src[n].reshape(g2[n].shape)

    outs = {}
    for n in big + ("small",):
        delta, mnew, vnew = _adamw(g2[n], two_d(wl, n), two_d(ml, n), two_d(vl, n))
        outs[n] = (g2[n], delta, mnew, vnew)
    res = [{}, {}, {}, {}]
    for i in range(4):
        for n in big:
            res[i][n] = outs[n][i].reshape(wl[n].shape)
        v, o = outs["small"][i].reshape(-1), 0
        for n in small_names:
            sz = int(np.prod(wl[n].shape))
            res[i][n] = v[o:o + sz].reshape(wl[n].shape)
            o += sz
    return (loss, grad_x, *[res[0][n] for n in WEIGHTS], *[res[1][n] for n in WEIGHTS], *[res[2][n] for n in WEIGHTS],
            *[res[3][n] for n in WEIGHTS])
```

```python
import functools

import numpy as np
import jax
import jax.numpy as jnp
from jax import lax
from jax.experimental import pallas as pl
from jax.experimental.pallas import tpu as pltpu

F32 = jnp.float32
BF16 = jnp.bfloat16
HI = lax.Precision.HIGHEST
SDS = jax.ShapeDtypeStruct

NH = 4
HD = 128
HW = NH * HD
N_META = 16
CH = 64
SUB = 16
PAD = CH - N_META
EPS = 1e-6
Q_SCALE = HD ** -0.5
DEPTH = 2
CONV_K = 4
VMEM_LIMIT = 56 * 1024 * 1024
ADAM_LR, ADAM_B1, ADAM_B2, ADAM_EPS, ADAM_WD, ADAM_STEP = 0.001, 0.9, 0.999, 1e-08, 0.01, 10
MESH = pl.DeviceIdType.MESH


def _nn(a, b):
    return jnp.dot(a, b, precision=HI, preferred_element_type=F32)


def _nt(a, b):
    return lax.dot_general(a, b, (((1,), (1,)), ((), ())), precision=HI, preferred_element_type=F32)


def _tn(a, b):
    return _nn(a.T, b)


def _scan_rows(x, group, reverse=False):
    n = x.shape[0]
    pos = lax.bitwise_and(_iota2(x.shape, 0), group - 1)
    s = 1
    while s < group:
        if reverse:
            x = x + jnp.where(pos < group - s, pltpu.roll(x, n - s, axis=0), 0.0)
        else:
            x = x + jnp.where(pos >= s, pltpu.roll(x, s, axis=0), 0.0)
        s *= 2
    return x


def _bnn(a, b):
    return jnp.dot(a.astype(BF16), b.astype(BF16), preferred_element_type=F32)


def _bnt(a, b):
    return lax.dot_general(a.astype(BF16), b.astype(BF16), (((1,), (1,)), ((), ())), preferred_element_type=F32)


def _btn(a, b):
    return lax.dot_general(a.astype(BF16), b.astype(BF16), (((0,), (0,)), ((), ())), preferred_element_type=F32)


def _hi_lo(x):
    hi = x.astype(jnp.bfloat16)
    return hi, (x - hi.astype(F32)).astype(jnp.bfloat16)


def _dot3(dims):
    def f(a, b):
        ah, al = _hi_lo(a)
        bh, bl = _hi_lo(b)
        d = lambda p, q: lax.dot_general(p, q, (dims, ((), ())), preferred_element_type=F32)
        return d(ah, bh) + (d(ah, bl) + d(al, bh))
    return f


_rnn, _rnt, _rtn = _dot3(((1,), (0,))), _dot3(((1,), (1,))), _dot3(((0,), (0,)))
_enn, _ent, _etn = _bnn, _bnt, _btn
_hnn, _hnt, _htn = _bnn, _bnt, _btn


def _rr(x):
    return x


def _sig(x):
    return jax.nn.sigmoid(x)


def _silu(x):
    return x * _sig(x)


def _dsilu(x):
    s = _sig(x)
    return s * (1.0 + x * (1.0 - s))


def _softplus(x):
    return jnp.maximum(x, 0.0) + jnp.log(1.0 + jnp.exp(-jnp.abs(x)))


def _logsig(x):
    return jnp.minimum(x, 0.0) - jnp.log(1.0 + jnp.exp(-jnp.abs(x)))


def _rs(x):
    return jnp.sum(x, axis=-1, keepdims=True)


def _params(n_axes):
    return pltpu.CompilerParams(dimension_semantics=("arbitrary",) * n_axes, vmem_limit_bytes=VMEM_LIMIT)


def _tile(n, target, mult=8):
    best = mult
    for t in range(mult, target + 1, mult):
        if n % t == 0:
            best = t
    assert n % best == 0, (n, mult)
    return best


def _ctile(pw, most=7):
    return HD * max(k for k in range(1, most + 1) if (pw // HD) % k == 0)


def _iota2(shape, axis):
    return lax.broadcasted_iota(jnp.int32, shape, axis)


class _Layout:
    def __init__(self, d):
        self.d = d
        self.wm = 2 * HW + 2 * d
        self.c_qkv = self.wm
        self.c_b = self.wm + 3 * HW
        self.c_ba = self.wm + 6 * HW
        self.pw = self.c_ba + HD
        assert self.c_b % (3 * HW) == 0
        o = 0
        segs = {}
        for name, w in (("a_q", HW), ("a_k", HW), ("a_v", HW), ("ba", 2 * NH), ("a_z", HW), ("b_q", HW), ("b_f", HW),
                        ("b_i", HW), ("b_g", HW), ("gate_a", d), ("gate_b", d)):
            segs[name] = (o, o + w)
            o += w
        self.segs = segs
        self.width = o
        self.order = ("a_z", "b_g", "gate_a", "gate_b", "a_q", "a_k", "a_v", "b_q", "b_f", "b_i", "ba")

    def to_kernel(self, w):
        parts = [w[..., self.segs[n][0]:self.segs[n][1]] for n in self.order]
        parts.append(jnp.zeros(w.shape[:-1] + (HD - 2 * NH,), w.dtype))
        return jnp.concatenate(parts, axis=-1)

    def containers(self, g, nchip):
        table, heads, cw = self.pieces(nchip)
        out = []
        for s in range(nchip):
            parts, at = [], 0
            for kcol, w, ccol in sorted(table[s], key=lambda p: p[2]):
                if ccol > at:
                    parts.append(jnp.zeros(g.shape[:-1] + (ccol - at,), g.dtype))
                parts.append(g[..., kcol:kcol + w])
                at = ccol + w
            if at < cw:
                parts.append(jnp.zeros(g.shape[:-1] + (cw - at,), g.dtype))
            out.append(jnp.concatenate(parts, axis=-1))
        return out, heads

    def pieces(self, nchip):
        off, where = 0, {}
        for n in self.order:
            where[n] = off
            off += self.segs[n][1] - self.segs[n][0]
        names = sorted(self.segs, key=lambda n: self.segs[n][0])
        sw = self.width // nchip
        cw = -(-sw // HD) * HD
        table, heads = [], []
        for s in range(nchip):
            lo, hi = s * sw, (s + 1) * sw
            pieces = []
            for n in names:
                a, b = max(lo, self.segs[n][0]), min(hi, self.segs[n][1])
                if a < b:
                    pieces.append((where[n] + a - self.segs[n][0], b - a))
            start, width = pieces[0]
            n_head = min((-start) % HD, width)
            body = ([(start + n_head, width - n_head)] if width > n_head else []) + pieces[1:]
            rows, at = [], 0
            for c, w in body:
                rows.append((c, w, at))
                at += w
            if n_head:
                rows.append((start, n_head, cw - n_head))
            table.append(rows)
            heads.append(n_head)
        return table, heads, cw

    def from_containers(self, conts):
        table, _, _ = self.pieces(len(conts))
        cut = sorted((kcol, w, s, ccol) for s, rows in enumerate(table) for kcol, w, ccol in rows)
        parts, at = [], 0
        for kcol, w, s, ccol in cut:
            assert kcol == at, (kcol, at)
            parts.append(conts[s][..., ccol:ccol + w])
            at = kcol + w
        parts.append(jnp.zeros(conts[0].shape[:-1] + (self.pw - at,), conts[0].dtype))
        return jnp.concatenate(parts, axis=-1)

    def from_kernel(self, g):
        off, where = 0, {}
        for n in self.order:
            w = self.segs[n][1] - self.segs[n][0]
            where[n] = (off, off + w)
            off += w
        names = sorted(self.segs, key=lambda n: self.segs[n][0])
        return jnp.concatenate([g[..., where[n][0]:where[n][1]] for n in names], axis=-1)


def _norm_proj_fwd(h, nw, wp):
    n, d = h.shape
    pw = wp.shape[1]
    tm, tn = _tile(n, 1408, HD), _ctile(pw)

    def body(h_ref, nw_ref, w_ref, o_ref, xt_ref, xn_ref):
        @pl.when(pl.program_id(1) == 0)
        def _():
            x = h_ref[...]
            r = lax.rsqrt(jnp.mean(x * x, axis=-1, keepdims=True) + EPS)
            xn = (x * r * nw_ref[...]).astype(BF16)
            xn_ref[...] = xn
            xt_ref[...] = xn.T

        o_ref[...] = jnp.dot(xn_ref[...], w_ref[...], preferred_element_type=F32)

    return pl.pallas_call(
        body, grid=(n // tm, pw // tn),
        in_specs=[pl.BlockSpec((tm, d), lambda i, j: (i, 0)), pl.BlockSpec((1, d), lambda i, j: (0, 0)),
                  pl.BlockSpec((d, tn), lambda i, j: (0, j))],
        out_specs=[pl.BlockSpec((tm, tn), lambda i, j: (i, j)), pl.BlockSpec((d, tm), lambda i, j: (0, i))],
        out_shape=[SDS((n, pw), F32), SDS((d, n), BF16)], scratch_shapes=[pltpu.VMEM((tm, d), BF16)],
        compiler_params=_params(2), name="norm_proj_fwd")(h, nw, wp)


def _row_valid(tm, tp, base):
    row = base + _iota2((tm, 1), 0)
    return lax.rem(row, tp) >= PAD


def _proj_bwd_dx(dproj, wp, h, nw, dhn, tp):
    n, d = h.shape
    pw = wp.shape[1]
    tm, tk = _tile16(n, 768), _ctile(pw)
    nk = pw // tk

    def body(dp_ref, w_ref, h_ref, nw_ref, dhn_ref, dh_ref, dnw_ref, acc_ref):
        i, k = pl.program_id(0), pl.program_id(1)

        @pl.when(k == 0)
        def _():
            acc_ref[...] = jnp.zeros_like(acc_ref)

        @pl.when((i == 0) & (k == 0))
        def _():
            dnw_ref[...] = jnp.zeros_like(dnw_ref)

        valid = _row_valid(tm, tp, i * tm)
        dp = jnp.where(valid, dp_ref[...], 0.0)
        acc_ref[...] += _bnt(dp, w_ref[...])

        @pl.when(k == nk - 1)
        def _():
            x = h_ref[...]
            r = lax.rsqrt(jnp.mean(x * x, axis=-1, keepdims=True) + EPS)
            xh = x * r
            dxn = acc_ref[...]
            dnw_ref[...] += jnp.sum(dxn * xh, axis=0, keepdims=True)
            dxh = dxn * nw_ref[...]
            dh_ref[...] = dhn_ref[...] + r * (dxh - xh * jnp.mean(dxh * xh, axis=-1, keepdims=True))

    return pl.pallas_call(
        body, grid=(n // tm, nk),
        in_specs=[pl.BlockSpec((tm, tk), lambda i, k: (i, k)), pl.BlockSpec((d, tk), lambda i, k: (0, k)),
                  pl.BlockSpec((tm, d), lambda i, k: (i, 0)), pl.BlockSpec((1, d), lambda i, k: (0, 0)),
                  pl.BlockSpec((tm, d), lambda i, k: (i, 0))],
        out_specs=[pl.BlockSpec((tm, d), lambda i, k: (i, 0)), pl.BlockSpec((1, d), lambda i, k: (0, 0))],
        out_shape=[SDS((n, d), F32), SDS((1, d), F32)],
        scratch_shapes=[pltpu.VMEM((tm, d), F32)], compiler_params=_params(2), name="proj_bwd_dx")(dproj, wp, h, nw, dhn)


def _proj_bwd_dw(dproj, xt, tp):
    d, n = xt.shape
    pw = dproj.shape[1]
    tm, tn = _tile(n, 1408, HD), _ctile(pw)

    def body(dp_ref, xt_ref, dw_ref):
        i = pl.program_id(1)

        @pl.when(i == 0)
        def _():
            dw_ref[...] = jnp.zeros_like(dw_ref)

        dp = jnp.where(_row_valid(tm, tp, i * tm), dp_ref[...], 0.0)
        dw_ref[...] += jnp.dot(xt_ref[...], dp.astype(BF16), preferred_element_type=F32)

    return pl.pallas_call(
        body, grid=(pw // tn, n // tm),
        in_specs=[pl.BlockSpec((tm, tn), lambda j, i: (i, j)), pl.BlockSpec((d, tm), lambda j, i: (0, i))],
        out_specs=pl.BlockSpec((d, tn), lambda j, i: (0, j)), out_shape=SDS((d, pw), F32),
        compiler_params=_params(2), name="proj_bwd_dw")(dproj, xt)


def _conv_silu(x, w, row):
    c = x * w[CONV_K - 1:CONV_K, :]
    for k in range(1, CONV_K):
        c = c + jnp.where(row >= k, pltpu.roll(x, k, axis=0), 0.0) * w[CONV_K - 1 - k:CONV_K - k, :]
    return c


def _gdn_prep_fwd(proj, conv_w, lay, nb, tp):
    n = proj.shape[0]
    nblk = 3 * NH
    cb = lay.c_qkv // HD

    def body(p_ref, w_ref, o_ref):
        j = pl.program_id(1)
        x = p_ref[...]
        row = _iota2(x.shape, 0)
        c = _conv_silu(x, w_ref[...], row)
        s = _silu(c)
        r = lax.rsqrt(_rs(s * s) + EPS)
        scale = jnp.where(j < NH, Q_SCALE, 1.0)
        y = jnp.where(j < 2 * NH, s * r * scale, s)
        o_ref[...] = jnp.where(row >= PAD, y, 0.0)

    return pl.pallas_call(
        body, grid=(nb, nblk),
        in_specs=[pl.BlockSpec((tp, HD), lambda b, j: (b, cb + j)), pl.BlockSpec((CONV_K, HD), lambda b, j: (0, j))],
        out_specs=pl.BlockSpec((tp, HD), lambda b, j: (b, j)), out_shape=SDS((n, nblk * HD), F32),
        compiler_params=_params(2), name="gdn_prep_fwd")(proj, conv_w)


def _gdn_prep_bwd(proj, conv_w, dqkv, dproj, lay, nb, tp):
    nblk = 3 * NH
    cb = lay.c_qkv // HD

    def body(p_ref, w_ref, dy_ref, dp_in, dp_ref, dw_ref):
        j, b = pl.program_id(0), pl.program_id(1)
        x = p_ref[...]
        w = w_ref[...]
        row = _iota2(x.shape, 0)
        c = _conv_silu(x, w, row)
        s = _silu(c)
        dy = jnp.where(row >= PAD, dy_ref[...], 0.0)
        r = lax.rsqrt(_rs(s * s) + EPS)
        nh = s * r
        scale = jnp.where(j < NH, Q_SCALE, 1.0)
        ds_n = scale * r * (dy - nh * _rs(dy * nh))
        ds = jnp.where(j < 2 * NH, ds_n, dy)
        dc = ds * _dsilu(c)
        dx = dc * w[CONV_K - 1:CONV_K, :]
        dws = [jnp.sum(dc * x, axis=0, keepdims=True)]
        for k in range(1, CONV_K):
            dx = dx + jnp.where(row < tp - k, pltpu.roll(dc, tp - k, axis=0), 0.0) * w[CONV_K - 1 - k:CONV_K - k, :]
            xs = jnp.where(row >= k, pltpu.roll(x, k, axis=0), 0.0)
            dws.append(jnp.sum(dc * xs, axis=0, keepdims=True))
        dp_ref[...] = dx.astype(dp_ref.dtype)
        r4 = _iota2((CONV_K, HD), 0)
        dw = jnp.zeros((CONV_K, HD), F32)
        for k in range(CONV_K):
            dw = dw + jnp.where(r4 == CONV_K - 1 - k, dws[k], 0.0)

        @pl.when(b == 0)
        def _():
            dw_ref[...] = dw

        @pl.when(b > 0)
        def _():
            dw_ref[...] += dw

    return pl.pallas_call(
        body, grid=(nblk, nb),
        in_specs=[pl.BlockSpec((tp, HD), lambda j, b: (b, cb + j)), pl.BlockSpec((CONV_K, HD), lambda j, b: (0, j)),
                  pl.BlockSpec((tp, HD), lambda j, b: (b, j)), pl.BlockSpec(memory_space=pl.ANY)],
        out_specs=[pl.BlockSpec((tp, HD), lambda j, b: (b, cb + j)), pl.BlockSpec((CONV_K, HD), lambda j, b: (0, j))],
        out_shape=[SDS(dproj.shape, dproj.dtype), SDS((CONV_K, nblk * HD), F32)],
        input_output_aliases={3: 0}, compiler_params=_params(2), name="gdn_prep_bwd")(proj, conv_w, dqkv, dproj)


def _gate_consts():
    e = np.zeros((HD, 2 * HW), np.float32)
    s = np.zeros((2 * HW, HD), np.float32)
    for h in range(NH):
        e[h, h * HD:(h + 1) * HD] = 1.0
        e[NH + h, HW + h * HD:HW + (h + 1) * HD] = 1.0
        s[h * HD, h] = 1.0
        s[HW + h * HD, NH + h] = 1.0
    return jnp.asarray(e), jnp.asarray(s)


def _gdn_tri():
    i, j = _iota2((CH, CH), 0), _iota2((CH, CH), 1)
    return i >= j, i > j


def _each(fn, *lists):
    return [fn(*xs) for xs in zip(*lists)]


def _tri_inv(a_list, eye):
    p = [-a for a in a_list]
    t = [eye + x for x in p]
    for _ in range(5):
        p = _each(_rnn, p, p)
        tp_ = _each(_rnn, t, p)
        t = _each(lambda x, y: x + y, t, tp_)
    return t


def _gdn_chunks(args, solved=None):
    causal, strict = _gdn_tri()
    eye = jnp.where(_iota2((CH, CH), 0) == _iota2((CH, CH), 1), 1.0, 0.0)
    q, k, v, beta, g, s0 = (list(t) for t in zip(*args))
    gc = [_scan_rows(x, CH) for x in g]
    dm = [jnp.where(causal, jnp.exp(jnp.where(causal, x[:, :CH] - x[:, :CH].T, 0.0)), 0.0) for x in gc]
    ds = [jnp.where(strict, x, 0.0) for x in dm]
    kb = _each(lambda x, y: x * y, k, beta)
    kk = _each(_ent, kb, k)
    a = _each(lambda x, y: x * y, kk, ds)
    eg = [jnp.exp(x) for x in gc]
    rw = _each(lambda x, y: x * y, kb, eg)
    if solved is None:
        tinv = _tri_inv(a, eye)
        rv = _each(lambda x, y: x * y, v, beta)
        u = _each(_rnn, tinv, rv)
        w = _each(_rnn, tinv, rw)
    else:
        tinv, u, w = (list(t) for t in zip(*solved))
    ws = _each(_enn, w, s0)
    vn = _each(lambda x, y: x - y, u, ws)
    qk = _each(_ent, q, k)
    p = _each(lambda x, y: x * y, qk, dm)
    qg = _each(lambda x, y: x * y, q, eg)
    out = []
    for i in range(len(args)):
        gl = gc[i][CH - 1:CH, :]
        ek = jnp.exp(gl - gc[i])
        out.append(dict(gc=gc[i], dm=dm[i], ds=ds[i], kb=kb[i], a=a[i], tinv=tinv[i], eg=eg[i], rw=rw[i], u=u[i], w=w[i],
                        vn=vn[i], p=p[i], qg=qg[i], egl=jnp.exp(gl), ek=ek, kd=k[i] * ek))
    return out


def _gdn_gates(ba, e, alog, dtb):
    raw = _nn(ba, e)
    beta = _sig(raw[:, :HW])
    za = raw[:, HW:] + dtb
    g = -jnp.exp(alog) * _softplus(za)
    return beta, g, za


def _seqs_per_step(nb):
    return 4 if nb % 4 == 0 else (2 if nb % 2 == 0 else 1)


def _gdn_fwd(qkv, proj, e_mat, alog, dtb, lay, nb, nc):
    n = qkv.shape[0]
    tp = n // nb
    cba = lay.c_ba // HD
    gb = _seqs_per_step(nb)

    def body(x_ref, ba_ref, e_ref, al_ref, dt_ref, o_ref, so_ref, sv_ref, s_ref):
        @pl.when(pl.program_id(1) == 0)
        def _():
            s_ref[...] = jnp.zeros_like(s_ref)

        args = []
        for j in range(gb):
            beta, g, _ = _gdn_gates(ba_ref[j], e_ref[...], al_ref[...], dt_ref[...])
            for h in range(NH):
                hs = slice(h * HD, (h + 1) * HD)
                args.append((x_ref[j, :, hs], x_ref[j, :, HW + h * HD:HW + (h + 1) * HD],
                             x_ref[j, :, 2 * HW + h * HD:2 * HW + (h + 1) * HD], beta[:, hs], g[:, hs], s_ref[j, h]))
        cs = _gdn_chunks(args)
        s0s = [a[5] for a in args]
        o1 = _each(lambda c, s0: _enn(c["qg"], s0), cs, s0s)
        o2 = [_enn(c["p"], c["vn"]) for c in cs]
        upd = [_etn(c["kd"], c["vn"]) for c in cs]
        res = [(o1[i] + o2[i], s0s[i] * cs[i]["egl"] + upd[i]) for i in range(len(cs))]
        zero = jnp.zeros((CH, HD - CH), F32)
        for j in range(gb):
            for h in range(NH):
                c = cs[j * NH + h]
                so_ref[j, h] = args[j * NH + h][5]
                sv_ref[j, h] = jnp.concatenate([c["u"], c["w"], c["tinv"], zero], axis=-1)
                s_ref[j, h] = res[j * NH + h][1]
            o_ref[j] = jnp.concatenate([res[j * NH + h][0] for h in range(NH)], axis=-1)

    o, st, sv = pl.pallas_call(
        body, grid=(nb // gb, nc),
        in_specs=[pl.BlockSpec((gb, CH, 3 * HW), lambda b, c: (b, c, 0)), pl.BlockSpec((gb, CH, HD), lambda b, c: (b, c, cba)),
                  pl.BlockSpec((HD, 2 * HW), lambda b, c: (0, 0)), pl.BlockSpec((1, HW), lambda b, c: (0, 0)),
                  pl.BlockSpec((1, HW), lambda b, c: (0, 0))],
        out_specs=[pl.BlockSpec((gb, CH, HW), lambda b, c: (b, c, 0)),
                   pl.BlockSpec((gb, None, NH, HD, HD), lambda b, c: (b, c, 0, 0, 0)),
                   pl.BlockSpec((gb, None, NH, CH, 3 * HD), lambda b, c: (b, c, 0, 0, 0))],
        out_shape=[SDS((nb, tp, HW), F32), SDS((nb, nc, NH, HD, HD), F32), SDS((nb, nc, NH, CH, 3 * HD), F32)],
        scratch_shapes=[pltpu.VMEM((gb, NH, HD, HD), F32)], compiler_params=_params(2), name="gdn_fwd")(
            qkv.reshape(nb, tp, 3 * HW), proj.reshape(nb, tp, -1), e_mat, alog, dtb)
    return o.reshape(n, HW), st, sv


def _gdn_bwd(qkv, proj, e_mat, s_mat, alog, dtb, states, solved, do, dproj, lay, nb, nc):
    n = qkv.shape[0]
    tp = n // nb
    cba = lay.c_ba // HD
    gb = _seqs_per_step(nb)

    def body(x_ref, ba_ref, e_ref, sm_ref, al_ref, dt_ref, st_ref, sv_ref, do_ref, dp_in, dx_ref, dba_ref, acc_ref, ds_ref):
        ci = pl.program_id(1)

        @pl.when(ci == 0)
        def _():
            ds_ref[...] = jnp.zeros_like(ds_ref)

        @pl.when((ci == 0) & (pl.program_id(0) == 0))
        def _():
            acc_ref[...] = jnp.zeros_like(acc_ref)

        causal, strict = _gdn_tri()
        alog = al_ref[...]
        row = _iota2((CH, 1), 0)
        valid = (row >= PAD) | (ci < nc - 1)
        last = row == CH - 1
        gates = [_gdn_gates(ba_ref[j], e_ref[...], alog, dt_ref[...]) for j in range(gb)]
        args, do, ds1, solved = [], [], [], []
        for j in range(gb):
            beta, g, _ = gates[j]
            for h in range(NH):
                hs = slice(h * HD, (h + 1) * HD)
                args.append((x_ref[j, :, hs], x_ref[j, :, HW + h * HD:HW + (h + 1) * HD],
                             x_ref[j, :, 2 * HW + h * HD:2 * HW + (h + 1) * HD], beta[:, hs], g[:, hs], st_ref[j, h]))
                do.append(do_ref[j, :, hs])
                ds1.append(ds_ref[j, h])
                solved.append((sv_ref[j, h, :, 2 * HD:2 * HD + CH], sv_ref[j, h, :, 0:HD], sv_ref[j, h, :, HD:2 * HD]))
        q, k, v, bh, _, s0 = (list(t) for t in zip(*args))
        cs = _gdn_chunks(args, solved)
        get = lambda name: [c[name] for c in cs]
        mul = lambda x, y: x * y
        add = lambda x, y: x + y
        dvn = _each(add, _each(_etn, get("p"), do), _each(_enn, get("kd"), ds1))
        dqg = _each(_ent, do, s0)
        dp = [jnp.where(causal, x, 0.0) for x in _each(_ent, do, get("vn"))]
        dkd = _each(_ent, get("vn"), ds1)
        dw = [-x for x in _each(_ent, dvn, s0)]
        ds_a = _each(_etn, get("qg"), do)
        ds_b = _each(_etn, get("w"), dvn)
        ds_new = [ds_a[i] - ds_b[i] + ds1[i] * cs[i]["egl"] for i in range(len(cs))]
        drv = _each(_rtn, get("tinv"), dvn)
        drw = _each(_rtn, get("tinv"), dw)
        da_1 = _each(_rnt, drv, get("u"))
        da_2 = _each(_rnt, drw, get("w"))
        da = [jnp.where(strict, -(x + y), 0.0) for x, y in zip(da_1, da_2)]
        m = [da[i] * cs[i]["a"] + dp[i] * cs[i]["p"] for i in range(len(cs))]
        dkk = _each(mul, da, get("ds"))
        dqk = _each(mul, dp, get("dm"))
        dq = _each(add, _each(_enn, dqk, k), _each(mul, dqg, get("eg")))
        dkb = _each(add, _each(_enn, dkk, k), _each(mul, drw, get("eg")))
        dk_1 = _each(_etn, dqk, q)
        dk_2 = _each(_etn, dkk, get("kb"))
        dk = [dk_1[i] + dk_2[i] + dkd[i] * cs[i]["ek"] + dkb[i] * bh[i] for i in range(len(cs))]
        dv = _each(mul, drv, bh)
        dbeta, dg = [], []
        for i, c in enumerate(cs):
            dbeta.append(_rs(drv[i] * v[i]) + _rs(dkb[i] * k[i]) + jnp.zeros((CH, HD), F32))
            t_kd = _rs(dkd[i] * c["kd"])
            dgc = _rs(m[i]) - _rs(m[i].T) + _rs(dqg[i] * c["qg"]) + _rs(drw[i] * c["rw"]) - t_kd
            tail = jnp.sum(t_kd, axis=0, keepdims=True) + c["egl"] * jnp.sum(_rs(s0[i] * ds1[i]), axis=0, keepdims=True)
            dgc = dgc + jnp.where(last, tail, 0.0)
            dg.append(_scan_rows(dgc + jnp.zeros((CH, HD), F32), CH, reverse=True))
        r8 = _iota2((8, HW), 0)
        upd = jnp.zeros((8, HW), F32)
        for j in range(gb):
            sl = slice(j * NH, (j + 1) * NH)
            beta, g, za = gates[j]
            for h in range(NH):
                ds_ref[j, h] = ds_new[j * NH + h]
            dx_ref[j] = jnp.concatenate(dq[sl] + dk[sl] + dv[sl], axis=-1)
            dbeta_j = jnp.where(valid, jnp.concatenate(dbeta[sl], axis=-1), 0.0)
            dg_j = jnp.where(valid, jnp.concatenate(dg[sl], axis=-1), 0.0)
            draw_b = dbeta_j * beta * (1.0 - beta)
            draw_a = dg_j * (-jnp.exp(alog)) * _sig(za)
            dba_ref[j] = _nn(jnp.concatenate([draw_b, draw_a], axis=-1), sm_ref[...]).astype(dba_ref.dtype)
            upd = upd + jnp.where(r8 == 0, jnp.sum(dg_j * g, axis=0, keepdims=True), 0.0) + jnp.where(
                r8 == 1, jnp.sum(draw_a, axis=0, keepdims=True), 0.0)
        acc_ref[...] += upd

    rc = lambda c: nc - 1 - c
    dqkv, dproj3, acc = pl.pallas_call(
        body, grid=(nb // gb, nc),
        in_specs=[pl.BlockSpec((gb, CH, 3 * HW), lambda b, c: (b, rc(c), 0)), pl.BlockSpec((gb, CH, HD), lambda b, c: (b, rc(c), cba)),
                  pl.BlockSpec((HD, 2 * HW), lambda b, c: (0, 0)), pl.BlockSpec((2 * HW, HD), lambda b, c: (0, 0)),
                  pl.BlockSpec((1, HW), lambda b, c: (0, 0)), pl.BlockSpec((1, HW), lambda b, c: (0, 0)),
                  pl.BlockSpec((gb, None, NH, HD, HD), lambda b, c: (b, rc(c), 0, 0, 0)),
                  pl.BlockSpec((gb, None, NH, CH, 3 * HD), lambda b, c: (b, rc(c), 0, 0, 0)),
                  pl.BlockSpec((gb, CH, HW), lambda b, c: (b, rc(c), 0)), pl.BlockSpec(memory_space=pl.ANY)],
        out_specs=[pl.BlockSpec((gb, CH, 3 * HW), lambda b, c: (b, rc(c), 0)), pl.BlockSpec((gb, CH, HD), lambda b, c: (b, rc(c), cba)),
                   pl.BlockSpec((8, HW), lambda b, c: (0, 0))],
        out_shape=[SDS((nb, tp, 3 * HW), F32), SDS((nb, tp, dproj.shape[1]), dproj.dtype), SDS((8, HW), F32)],
        input_output_aliases={9: 1},
        scratch_shapes=[pltpu.VMEM((gb, NH, HD, HD), F32)], compiler_params=_params(2), name="gdn_bwd")(
            qkv.reshape(nb, tp, 3 * HW), proj.reshape(nb, tp, -1), e_mat, s_mat, alog, dtb, states, solved, do.reshape(nb, tp, HW),
            dproj.reshape(nb, tp, -1))
    return dqkv.reshape(n, 3 * HW), dproj3.reshape(dproj.shape), acc


def _hgrn_inputs(zq, zf, lb):
    sg = _sig(zf)
    sgn = _sig(-zf)
    pos = lb > 0.0
    lbp = jnp.where(pos, lb, 0.0)
    fpos = lbp + (1.0 - lbp) * sg
    lf = jnp.where(pos, jnp.log(jnp.where(pos, fpos, 1.0)), _logsig(zf))
    k = (1.0 - lbp) * sgn
    q = _silu(zq) * Q_SCALE
    return q, k, lf, sg, sgn, pos, lbp, fpos


def _hgrn_consts():
    i3, j3 = _iota2((SUB, SUB, HD), 0), _iota2((SUB, SUB, HD), 1)
    return i3 >= j3


def _sum_j(x):
    return jnp.sum(x.reshape(SUB, SUB, HD), axis=1)


def _sum_i(x):
    return jnp.sum(x.reshape(SUB, SUB, HD), axis=0)


def _pairs(a, b):
    return (a[:, None, :] * b[None, :, :]).reshape(SUB * SUB, HD)


def _hgrn_sub(q, k, v, bc, st, consts):
    mask3 = consts
    bl = bc[SUB - 1:SUB, :]
    p3 = jnp.where(mask3, jnp.exp(jnp.where(mask3, bc[:, None, :] - bc[None, :, :], 0.0)), 0.0).reshape(SUB * SUB, HD)
    x = _pairs(q, k) * p3
    srep = _rs(x)
    vt = jnp.broadcast_to(v[None, :, :], (SUB, SUB, HD)).reshape(SUB * SUB, HD)
    eb = jnp.exp(bc)
    qe = q * eb
    o = _hnt(qe, st) + _sum_j(_rr(srep) * _rr(vt))
    ek = jnp.exp(bl - bc)
    kd = k * ek
    ebl = jnp.exp(bl)
    st1 = st * ebl + _htn(v, kd)
    return o, st1, dict(bc=bc, p3=p3, srep=srep, vt=vt, eb=eb, qe=qe, ek=ek, kd=kd, ebl=ebl)


def _hgrn_fwd(proj, lb, lay, nb, nc):
    n = proj.shape[0]
    cbb = lay.c_b // (3 * HW)

    def body(z_ref, lb_ref, o_ref, so_ref, s_ref):
        @pl.when(pl.program_id(1) == 0)
        def _():
            s_ref[...] = jnp.zeros_like(s_ref)

        consts = _hgrn_consts()
        outs = []
        for h in range(NH):
            hs = slice(h * HD, (h + 1) * HD)
            q, k, lf = _hgrn_inputs(z_ref[:, hs], z_ref[:, HW + h * HD:HW + (h + 1) * HD], lb_ref[:, hs])[:3]
            v = z_ref[:, 2 * HW + h * HD:2 * HW + (h + 1) * HD]
            st = s_ref[h]
            so_ref[h] = st
            bc = _scan_rows(lf, SUB)
            oh = []
            for s in range(CH // SUB):
                rs = slice(s * SUB, (s + 1) * SUB)
                o, st, _ = _hgrn_sub(q[rs], k[rs], v[rs], bc[rs], st, consts)
                oh.append(o)
            s_ref[h] = st
            outs.append(jnp.concatenate(oh, axis=0))
        o_ref[...] = jnp.concatenate(outs, axis=-1)

    return pl.pallas_call(
        body, grid=(nb, nc),
        in_specs=[pl.BlockSpec((CH, 3 * HW), lambda b, c: (b * nc + c, cbb)), pl.BlockSpec((1, HW), lambda b, c: (0, 0))],
        out_specs=[pl.BlockSpec((CH, HW), lambda b, c: (b * nc + c, 0)),
                   pl.BlockSpec((None, None, NH, HD, HD), lambda b, c: (b, c, 0, 0, 0))],
        out_shape=[SDS((n, HW), F32), SDS((nb, nc, NH, HD, HD), F32)],
        scratch_shapes=[pltpu.VMEM((NH, HD, HD), F32)], compiler_params=_params(2), name="hgrn_fwd")(proj, lb)


def _hgrn_bwd(proj, lb, states, do, dproj, lay, nb, nc):
    cbb = lay.c_b // (3 * HW)
    nsub = CH // SUB

    def rev(b, c):
        return b * nc + (nc - 1 - c)

    def body(z_ref, lb_ref, st_ref, do_ref, dp_in, dz_ref, acc_ref, ds_ref):
        ci = pl.program_id(1)

        @pl.when(ci == 0)
        def _():
            ds_ref[...] = jnp.zeros_like(ds_ref)

        @pl.when((ci == 0) & (pl.program_id(0) == 0))
        def _():
            acc_ref[...] = jnp.zeros_like(acc_ref)

        consts = _hgrn_consts()
        row = _iota2((CH, 1), 0)
        valid = (row >= PAD) | (ci < nc - 1)
        lastrow = _iota2((SUB, 1), 0) == SUB - 1
        dzq, dzf, dzi, dlbs = [], [], [], []
        for h in range(NH):
            hs = slice(h * HD, (h + 1) * HD)
            zq, zf = z_ref[:, hs], z_ref[:, HW + h * HD:HW + (h + 1) * HD]
            q, k, lf, sg, sgn, pos, lbp, fpos = _hgrn_inputs(zq, zf, lb_ref[:, hs])
            v = z_ref[:, 2 * HW + h * HD:2 * HW + (h + 1) * HD]
            doh = do_ref[:, hs]
            sts, fw = [st_ref[h]], []
            bc = _scan_rows(lf, SUB)
            for s in range(nsub):
                rs = slice(s * SUB, (s + 1) * SUB)
                _, st1, c = _hgrn_sub(q[rs], k[rs], v[rs], bc[rs], sts[-1], consts)
                sts.append(st1)
                fw.append(c)
            dst = ds_ref[h]
            dq_l, dk_l, dv_l, dlf_l = [None] * nsub, [None] * nsub, [None] * nsub, [None] * nsub
            for s in reversed(range(nsub)):
                rs = slice(s * SUB, (s + 1) * SUB)
                c, st = fw[s], sts[s]
                qs, ks, vs, dos = q[rs], k[rs], v[rs], doh[rs]
                dqe = _hnn(dos, st)
                dkd = _hnn(vs, dst)
                dsrep = _rs(_pairs(_rr(dos), _rr(vs)))
                w = dsrep * c["p3"]
                kt = jnp.broadcast_to(ks[None, :, :], (SUB, SUB, HD)).reshape(SUB * SUB, HD)
                qt = jnp.broadcast_to(qs[:, None, :], (SUB, SUB, HD)).reshape(SUB * SUB, HD)
                dq_i = _sum_j(w * kt)
                dk_i = _sum_i(w * qt)
                dot = jnp.broadcast_to(_rr(dos)[:, None, :], (SUB, SUB, HD)).reshape(SUB * SUB, HD)
                dvv = _sum_i(_rr(c["srep"]) * dot) + _hnt(c["kd"], dst)
                t_kd = dkd * c["kd"]
                dbc = dqe * c["qe"] - t_kd + qs * dq_i - ks * dk_i
                tail = jnp.sum(t_kd, axis=0, keepdims=True) + c["ebl"] * jnp.sum(st * dst, axis=0, keepdims=True)
                dbc = dbc + jnp.where(lastrow, tail, 0.0)
                dlf_l[s] = dbc
                dq_l[s] = dq_i + dqe * c["eb"]
                dk_l[s] = dk_i + dkd * c["ek"]
                dv_l[s] = dvv
                dst = _htn(dos, c["qe"]) + dst * c["ebl"]
            ds_ref[h] = dst
            dq, dk, dv, dbc = (jnp.concatenate(t, axis=0) for t in (dq_l, dk_l, dv_l, dlf_l))
            dlf = _scan_rows(dbc, SUB, reverse=True)
            dlft = dlf - dk * (1.0 - k)
            dlf_dz = jnp.where(pos, (1.0 - lbp) * sg * sgn / jnp.where(pos, fpos, 1.0), sgn)
            dlf_dlb = jnp.where(pos, sgn / jnp.where(pos, fpos, 1.0), 0.0)
            dzq.append(dq * Q_SCALE * _dsilu(zq))
            dzf.append(dlft * dlf_dz)
            dzi.append(dv)
            dlbs.append(jnp.sum(jnp.where(valid, dlft * dlf_dlb, 0.0), axis=0, keepdims=True))
        dz_ref[...] = jnp.concatenate(dzq + dzf + dzi, axis=-1).astype(dz_ref.dtype)
        acc_ref[...] += jnp.where(_iota2((8, HW), 0) == 0, jnp.concatenate(dlbs, axis=-1), 0.0)

    return pl.pallas_call(
        body, grid=(nb, nc),
        in_specs=[pl.BlockSpec((CH, 3 * HW), lambda b, c: (rev(b, c), cbb)), pl.BlockSpec((1, HW), lambda b, c: (0, 0)),
                  pl.BlockSpec((None, None, NH, HD, HD), lambda b, c: (b, nc - 1 - c, 0, 0, 0)),
                  pl.BlockSpec((CH, HW), lambda b, c: (rev(b, c), 0)), pl.BlockSpec(memory_space=pl.ANY)],
        out_specs=[pl.BlockSpec((CH, 3 * HW), lambda b, c: (rev(b, c), cbb)), pl.BlockSpec((8, HW), lambda b, c: (0, 0))],
        out_shape=[SDS(dproj.shape, dproj.dtype), SDS((8, HW), F32)],
        input_output_aliases={4: 0},
        scratch_shapes=[pltpu.VMEM((NH, HD, HD), F32)], compiler_params=_params(2), name="hgrn_bwd")(proj, lb, states, do, dproj)


def _gated_norm(o, z, gamma):
    ys, ns, rs = [], [], []
    for h in range(NH):
        hs = slice(h * HD, (h + 1) * HD)
        oh = o[:, hs]
        r = lax.rsqrt(jnp.mean(oh * oh, axis=-1, keepdims=True) + EPS)
        nh = oh * r
        ys.append(nh * gamma * _silu(z[:, hs]))
        ns.append(nh)
        rs.append(r)
    return jnp.concatenate(ys, axis=-1), ns, rs


def _merge_fwd(h, oa, ob, proj, ga, gb, wa, wb, wo, lay):
    n, d = h.shape
    tm = _tile(n, 384)
    wm = lay.wm

    def body(h_ref, oa_ref, ob_ref, p_ref, ga_ref, gb_ref, wa_ref, wb_ref, wo_ref, out_ref):
        ya, _, _ = _gated_norm(oa_ref[...], p_ref[:, 0:HW], ga_ref[...])
        yb, _, _ = _gated_norm(ob_ref[...], p_ref[:, HW:2 * HW], gb_ref[...])
        ya2 = _bnn(ya, wa_ref[...])
        yb2 = _bnn(yb, wb_ref[...])
        mixed = _sig(p_ref[:, 2 * HW:2 * HW + d]) * ya2 + _sig(p_ref[:, 2 * HW + d:2 * HW + 2 * d]) * yb2
        out_ref[...] = h_ref[...] + _bnn(mixed, wo_ref[...])

    full = lambda shape: pl.BlockSpec(shape, lambda i: (0, 0))
    return pl.pallas_call(
        body, grid=(n // tm,),
        in_specs=[pl.BlockSpec((tm, d), lambda i: (i, 0)), pl.BlockSpec((tm, HW), lambda i: (i, 0)),
                  pl.BlockSpec((tm, HW), lambda i: (i, 0)), pl.BlockSpec((tm, wm), lambda i: (i, 0)),
                  full((1, HD)), full((1, HD)), full((HW, d)), full((HW, d)), full((d, d))],
        out_specs=pl.BlockSpec((tm, d), lambda i: (i, 0)), out_shape=SDS((n, d), F32),
        compiler_params=_params(1), name="merge_fwd")(h, oa, ob, proj, ga, gb, wa, wb, wo)


def _gated_norm_bwd(dy, o, z, gamma):
    dos, dzs = [], []
    dgam = jnp.zeros((1, HD), F32)
    for h in range(NH):
        hs = slice(h * HD, (h + 1) * HD)
        oh, zh, dyh = o[:, hs], z[:, hs], dy[:, hs]
        r = lax.rsqrt(jnp.mean(oh * oh, axis=-1, keepdims=True) + EPS)
        nh = oh * r
        dzs.append(dyh * nh * gamma * _dsilu(zh))
        dng = dyh * _silu(zh)
        dgam = dgam + jnp.sum(dng * nh, axis=0, keepdims=True)
        dn = dng * gamma
        dos.append(r * (dn - nh * jnp.mean(dn * nh, axis=-1, keepdims=True)))
    return jnp.concatenate(dos, axis=-1), jnp.concatenate(dzs, axis=-1), dgam


def _merge_bwd(dhn, oa, ob, proj, ga, gb, wa, wb, wo, lay, tp):
    n, d = dhn.shape
    tm = _tile(n, 256)
    wm = lay.wm

    def body(dh_ref, oa_ref, ob_ref, p_ref, ga_ref, gb_ref, wa_ref, wb_ref, wo_ref,
             dp_ref, doa_ref, dob_ref, dwa_ref, dwb_ref, dwo_ref, dga_ref, dgb_ref):
        i = pl.program_id(0)

        @pl.when(i == 0)
        def _():
            for r in (dwa_ref, dwb_ref, dwo_ref, dga_ref, dgb_ref):
                r[...] = jnp.zeros_like(r)

        dh = jnp.where(_row_valid(tm, tp, i * tm), dh_ref[...], 0.0)
        oa, ob = oa_ref[...], ob_ref[...]
        za, zb = p_ref[:, 0:HW], p_ref[:, HW:2 * HW]
        gta, gtb = p_ref[:, 2 * HW:2 * HW + d], p_ref[:, 2 * HW + d:2 * HW + 2 * d]
        ya, _, _ = _gated_norm(oa, za, ga_ref[...])
        yb, _, _ = _gated_norm(ob, zb, gb_ref[...])
        ya2 = _bnn(ya, wa_ref[...])
        yb2 = _bnn(yb, wb_ref[...])
        sa, sb = _sig(gta), _sig(gtb)
        mixed = sa * ya2 + sb * yb2
        dmixed = _bnt(dh, wo_ref[...])
        dwo_ref[...] += _btn(mixed, dh)
        dya2 = dmixed * sa
        dyb2 = dmixed * sb
        dwa_ref[...] += _btn(ya, dya2)
        dwb_ref[...] += _btn(yb, dyb2)
        doa, dza, dga = _gated_norm_bwd(_bnt(dya2, wa_ref[...]), oa, za, ga_ref[...])
        dob, dzb, dgb = _gated_norm_bwd(_bnt(dyb2, wb_ref[...]), ob, zb, gb_ref[...])
        dga_ref[...] += dga
        dgb_ref[...] += dgb
        doa_ref[...] = doa
        dob_ref[...] = dob
        dt = dp_ref.dtype
        dp_ref[:, 0:HW] = dza.astype(dt)
        dp_ref[:, HW:2 * HW] = dzb.astype(dt)
        dp_ref[:, 2 * HW:2 * HW + d] = (dmixed * ya2 * sa * (1.0 - sa)).astype(dt)
        dp_ref[:, 2 * HW + d:2 * HW + 2 * d] = (dmixed * yb2 * sb * (1.0 - sb)).astype(dt)

    full = lambda shape: pl.BlockSpec(shape, lambda i: (0, 0))
    rows = lambda w: pl.BlockSpec((tm, w), lambda i: (i, 0))
    return pl.pallas_call(
        body, grid=(n // tm,),
        in_specs=[rows(d), rows(HW), rows(HW), rows(wm), full((1, HD)), full((1, HD)), full((HW, d)), full((HW, d)), full((d, d))],
        out_specs=[rows(wm), rows(HW), rows(HW), full((HW, d)), full((HW, d)), full((d, d)), full((1, HD)), full((1, HD))],
        out_shape=[SDS((n, lay.pw), BF16), SDS((n, HW), F32), SDS((n, HW), F32), SDS((HW, d), F32), SDS((HW, d), F32),
                   SDS((d, d), F32), SDS((1, HD), F32), SDS((1, HD), F32)],
        compiler_params=_params(1), name="merge_bwd")(dhn, oa, ob, proj, ga, gb, wa, wb, wo)


def _loss_head(h, target, fw, nb, nc):
    n, d = h.shape

    def body(h_ref, t_ref, fw_ref, lp_ref, dh_ref, dfw_ref):
        b, c = pl.program_id(0), pl.program_id(1)

        @pl.when((b == 0) & (c == 0))
        def _():
            dfw_ref[...] = jnp.zeros_like(dfw_ref)

        @pl.when(c == 0)
        def _():
            dh_ref[...] = jnp.zeros_like(dh_ref)
            lp_ref[...] = jnp.zeros_like(lp_ref)

        @pl.when(c > 0)
        def _():
            x = h_ref[...]
            r = lax.rsqrt(jnp.mean(x * x, axis=-1, keepdims=True) + EPS)
            xh = x * r
            err = xh * fw_ref[...] - t_ref[...]
            lp_ref[...] = jnp.zeros_like(lp_ref) + 0.5 * jnp.sum(_rs(err * err), axis=0, keepdims=True) / d
            dy = err / d
            dfw_ref[...] += jnp.sum(dy * xh, axis=0, keepdims=True)
            dxh = dy * fw_ref[...]
            dh_ref[...] = r * (dxh - xh * jnp.mean(dxh * xh, axis=-1, keepdims=True))

    return pl.pallas_call(
        body, grid=(nb, nc),
        in_specs=[pl.BlockSpec((CH, d), lambda b, c: (b * nc + c, 0)),
                  pl.BlockSpec((CH, d), lambda b, c: (b * (nc - 1) + jnp.maximum(c - 1, 0), 0)),
                  pl.BlockSpec((1, d), lambda b, c: (0, 0))],
        out_specs=[pl.BlockSpec((8, HD), lambda b, c: (b * nc + c, 0)), pl.BlockSpec((CH, d), lambda b, c: (b * nc + c, 0)),
                   pl.BlockSpec((1, d), lambda b, c: (0, 0))],
        out_shape=[SDS((nb * nc * 8, HD), F32), SDS((n, d), F32), SDS((1, d), F32)],
        compiler_params=_params(2), name="loss_head")(h, target, fw)


def _lb_fwd(lb):
    def body(x_ref, o_ref):
        x = x_ref[...]
        mx = jnp.max(x, axis=0, keepdims=True)
        e = jnp.exp(x - mx)
        sm = e / jnp.sum(e, axis=0, keepdims=True)
        run = jnp.zeros((1, HW), F32)
        for l in range(DEPTH):
            run = run + sm[l:l + 1, :]
            o_ref[l:l + 1, :] = run - sm[0:1, :]

    return pl.pallas_call(body, out_shape=SDS(lb.shape, F32), name="lb_fwd")(lb)


def _lb_bwd(lb, dlb_all):
    def body(x_ref, d_ref, o_ref):
        x = x_ref[...]
        dl = d_ref[...]
        mx = jnp.max(x, axis=0, keepdims=True)
        e = jnp.exp(x - mx)
        sm = e / jnp.sum(e, axis=0, keepdims=True)
        tot = jnp.sum(dl, axis=0, keepdims=True)
        dsm = []
        run = tot
        for l in range(DEPTH):
            dsm.append(run - (tot if l == 0 else 0.0))
            run = run - dl[l:l + 1, :]
        inner = sum(sm[l:l + 1, :] * dsm[l] for l in range(DEPTH))
        for l in range(DEPTH):
            o_ref[l:l + 1, :] = sm[l:l + 1, :] * (dsm[l] - inner)

    return pl.pallas_call(body, out_shape=SDS(lb.shape, F32), name="lb_bwd")(lb, dlb_all)


def _adamw(g, w, m, v):
    r, c = g.shape
    tr = _tile(r, 264)
    c1 = 1.0 / (1.0 - ADAM_B1 ** ADAM_STEP)
    c2 = 1.0 / (1.0 - ADAM_B2 ** ADAM_STEP)

    def body(g_ref, w_ref, m_ref, v_ref, d_ref, mo_ref, vo_ref):
        gg = g_ref[...]
        mn = ADAM_B1 * m_ref[...] + (1.0 - ADAM_B1) * gg
        vn = ADAM_B2 * v_ref[...] + (1.0 - ADAM_B2) * gg * gg
        d_ref[...] = -ADAM_LR * ((mn * c1) / (jnp.sqrt(vn * c2) + ADAM_EPS) + ADAM_WD * w_ref[...])
        mo_ref[...] = mn
        vo_ref[...] = vn

    spec = pl.BlockSpec((tr, c), lambda i: (i, 0))
    return pl.pallas_call(body, grid=(r // tr,), in_specs=[spec] * 4, out_specs=[spec] * 3, out_shape=[SDS(g.shape, F32)] * 3,
                          compiler_params=_params(1), name="adamw")(g, w, m, v)


def _tile16(n, target):
    return _tile(n // 2, target // 2) * 2 if n % 16 == 0 else _tile(n, target)


def _add_cores(g, got, core):
    k, r, c = got.shape
    tr = _tile16(r, 264)

    def body(c_ref, a_ref, b_ref, o_ref):
        o_ref[...] = (a_ref[...] + b_ref[...].astype(F32)).astype(o_ref.dtype)

    spec = pl.BlockSpec((None, tr, c), lambda s, i, cr: (s, i, 0))
    return pl.pallas_call(
        body, grid_spec=pltpu.PrefetchScalarGridSpec(
            num_scalar_prefetch=1, grid=(k, r // tr),
            in_specs=[pl.BlockSpec((None, None, tr, c), lambda s, i, cr: (cr[0], s, i, 0)), spec], out_specs=spec),
        out_shape=SDS(got.shape, got.dtype), compiler_params=_params(2), name="add_cores")(core, g, got)


def _sum_chips(parts, own, place):
    k, r, c = parts.shape
    tr = _tile16(r, 264)

    def body(p_ref, *refs):
        part_refs, own_ref, o_ref = refs[:k], refs[k], refs[k + 1]
        mine = own_ref[...].astype(F32)
        acc = None
        for s in range(k):
            term = jnp.where(p_ref[0] == s, mine, part_refs[s][...].astype(F32))
            acc = term if acc is None else acc + term
        o_ref[...] = acc

    slots = jnp.stack([jnp.where(place[0] == s, (s + 1) % k, s) for s in range(k)]).astype(jnp.int32)
    other = lambda s: pl.BlockSpec((None, tr, c), lambda i, p: (p[2 + s], i, 0))
    return pl.pallas_call(
        body, grid_spec=pltpu.PrefetchScalarGridSpec(
            num_scalar_prefetch=1, grid=(r // tr,),
            in_specs=[other(s) for s in range(k)] + [pl.BlockSpec((None, tr, c), lambda i, p: (p[0], i, 0))],
            out_specs=pl.BlockSpec((None, tr, c), lambda i, p: (p[1], i, 0))),
        out_shape=SDS((2, r, c), F32), compiler_params=_params(1), name="sum_chips")(
            jnp.concatenate([place, slots]), *([parts] * k), own)


def _meta_grad(dh, nb, nc):
    d = dh.shape[1]

    def body(x_ref, o_ref):
        @pl.when(pl.program_id(0) == 0)
        def _():
            o_ref[...] = jnp.zeros_like(o_ref)

        o_ref[...] += x_ref[PAD:CH, :]

    return pl.pallas_call(body, grid=(nb,), in_specs=[pl.BlockSpec((CH, d), lambda b: (b * nc, 0))],
                          out_specs=pl.BlockSpec((N_META, d), lambda b: (0, 0)), out_shape=SDS((N_META, d), F32),
                          compiler_params=_params(1), name="meta_grad")(dh)


ANY = pl.BlockSpec(memory_space=pl.ANY)


def _place():
    x, y, c = lax.axis_index("x"), lax.axis_index("y"), lax.axis_index("c")
    chips = [(1 - x, y), (x, 1 - y), (1 - x, 1 - y)]
    return x, y, c, chips


def _remote(src, dst, send_sems, recv_sems, k, to):
    return pltpu.make_async_remote_copy(src_ref=src, dst_ref=dst, send_sem=send_sems.at[k], recv_sem=recv_sems.at[k],
                                        device_id=to, device_id_type=MESH)


def _gather_weights(pbs, ps):
    nt = len(pbs)

    def body(*refs):
        pb_refs, ps_ref, gb_refs, gs_ref = refs[:nt], refs[nt], refs[nt + 1:2 * nt + 1], refs[2 * nt + 1]
        send_sems, recv_sems, local_sems = refs[2 * nt + 2:]
        x, y, c, chips = _place()
        s = 2 * x + y
        sib = (x, y, 1 - c)
        l1 = pltpu.make_async_copy(ps_ref, gs_ref.at[s], local_sems.at[0])
        l1.start()
        sends = []
        for k, (px, py) in enumerate(chips):
            for t in range(nt):
                sends.append(_remote(pb_refs[t].at[c], gb_refs[t].at[s, c], send_sems, recv_sems, 6 * t + k, (px, py, c)))
            sends.append(_remote(ps_ref, gs_ref.at[s], send_sems, recv_sems, 6 * nt + k, (px, py, c)))
        for cp in sends:
            cp.start()
        for k, (px, py) in enumerate(chips):
            sk = 2 * px + py
            for t in range(nt):
                _remote(pb_refs[t].at[c], gb_refs[t].at[sk, c], send_sems, recv_sems, 6 * t + k, sib).wait_recv()
                fwd = _remote(gb_refs[t].at[sk, c], gb_refs[t].at[sk, c], send_sems, recv_sems, 6 * t + 3 + k, sib)
                fwd.start()
                sends.append(fwd)
        for k, (px, py) in enumerate(chips):
            sk = 2 * px + py
            for t in range(nt):
                _remote(pb_refs[t].at[c], gb_refs[t].at[sk, 1 - c], send_sems, recv_sems, 6 * t + 3 + k, sib).wait_recv()
            _remote(ps_ref, gs_ref.at[sk], send_sems, recv_sems, 6 * nt + k, sib).wait_recv()
        for cp in sends:
            cp.wait_send()
        l1.wait()

    nsem = 6 * nt + 3
    out = pl.pallas_call(
        body, in_specs=[ANY] * (nt + 1), out_specs=[ANY] * (nt + 1),
        out_shape=[SDS((4,) + pb.shape, pb.dtype) for pb in pbs] + [SDS((4,) + ps.shape, ps.dtype)],
        scratch_shapes=[pltpu.SemaphoreType.DMA((nsem,)), pltpu.SemaphoreType.DMA((nsem,)), pltpu.SemaphoreType.DMA((1,))],
        name="gather_weights")(*pbs, ps)
    return out[:nt], out[nt]


def _contain(wpad, shift):
    r, cw = wpad.shape
    tr = _tile16(r, 256)

    def body(n_ref, x_ref, o_ref):
        o_ref[...] = pltpu.roll(x_ref[...], n_ref[0], axis=1).astype(o_ref.dtype)

    spec = pl.BlockSpec((tr, cw), lambda i, n: (i, 0))
    return pl.pallas_call(
        body, grid_spec=pltpu.PrefetchScalarGridSpec(num_scalar_prefetch=1, grid=(r // tr,), in_specs=[spec], out_specs=spec),
        out_shape=SDS((r, cw), BF16), compiler_params=_params(1), name="contain")(shift, wpad)


def _place_own(gb, pb, chip):
    _, _, r, c = gb.shape
    tr = _tile16(r, 1100)

    def body(s_ref, p_ref, g_in, o_ref):
        o_ref[...] = p_ref[...]

    return pl.pallas_call(
        body, grid_spec=pltpu.PrefetchScalarGridSpec(
            num_scalar_prefetch=1, grid=(2, r // tr),
            in_specs=[pl.BlockSpec((None, tr, c), lambda h, i, s: (h, i, 0)), ANY],
            out_specs=pl.BlockSpec((None, None, tr, c), lambda h, i, s: (s[0], h, i, 0))),
        out_shape=SDS(gb.shape, gb.dtype), input_output_aliases={2: 0}, compiler_params=_params(2),
        name="place_own")(chip, pb, gb)


def _sem_scratch(n_remote, n_local):
    return [pltpu.SemaphoreType.DMA((n_remote,)), pltpu.SemaphoreType.DMA((n_remote,)), pltpu.SemaphoreType.DMA((n_local,))]


def _swap_halves(sends):
    nt = len(sends)

    def body(*refs):
        s_refs, got_refs = refs[:nt], refs[nt:2 * nt]
        send_sems, recv_sems = refs[2 * nt:]
        x, y, c, _ = _place()
        sib = (x, y, 1 - c)
        remote = [_remote(s_refs[t].at[1 - c, s], got_refs[t].at[s], send_sems, recv_sems, 4 * t + s, sib)
                  for t in range(nt) for s in range(4)]
        for cp in remote:
            cp.start()
        for cp in remote:
            cp.wait()

    return pl.pallas_call(
        body, in_specs=[ANY] * nt, out_specs=[ANY] * nt, out_shape=[SDS(g.shape[1:], g.dtype) for g in sends],
        scratch_shapes=[pltpu.SemaphoreType.DMA((4 * nt,)), pltpu.SemaphoreType.DMA((4 * nt,))], name="swap_halves")(*sends)


def _scatter_chip_sums(parts):
    nt = len(parts)

    def body(*refs):
        a_refs, r_refs = refs[:nt], refs[nt:2 * nt]
        send_sems, recv_sems = refs[2 * nt:]
        x, y, c, chips = _place()
        s = 2 * x + y
        sends = [_remote(a_refs[t].at[2 * px + py], r_refs[t].at[s], send_sems, recv_sems, 3 * t + k, (px, py, c))
                 for t in range(nt) for k, (px, py) in enumerate(chips)]
        for cp in sends:
            cp.start()
        for t in range(nt):
            for k, (px, py) in enumerate(chips):
                _remote(a_refs[t].at[s], r_refs[t].at[2 * px + py], send_sems, recv_sems, 3 * t + k, (px, py, c)).wait_recv()
        for cp in sends:
            cp.wait_send()

    return pl.pallas_call(
        body, in_specs=[ANY] * nt, out_specs=[ANY] * nt, out_shape=[SDS(a.shape, a.dtype) for a in parts],
        scratch_shapes=[pltpu.SemaphoreType.DMA((3 * nt,)), pltpu.SemaphoreType.DMA((3 * nt,))],
        name="scatter_chip_sums")(*parts)


def _join_halves(fs):
    nt = len(fs)

    def body(*refs):
        f_refs = refs[nt:2 * nt]
        send_sems, recv_sems = refs[2 * nt:]
        x, y, c, _ = _place()
        sib = (x, y, 1 - c)
        sends = [_remote(f_refs[t].at[c], f_refs[t].at[c], send_sems, recv_sems, t, sib) for t in range(nt)]
        for cp in sends:
            cp.start()
        for t in range(nt):
            _remote(f_refs[t].at[c], f_refs[t].at[1 - c], send_sems, recv_sems, t, sib).wait_recv()
        for cp in sends:
            cp.wait_send()

    return pl.pallas_call(
        body, in_specs=[ANY] * nt, out_specs=[ANY] * nt, out_shape=[SDS(f.shape, f.dtype) for f in fs],
        input_output_aliases={t: t for t in range(nt)},
        scratch_shapes=[pltpu.SemaphoreType.DMA((nt,)), pltpu.SemaphoreType.DMA((nt,))], name="join_halves")(*fs)


def _uncontain(cont, n_head, width):
    r, cw = cont.shape
    tr = _tile(r, 256)

    def body(n_ref, x_ref, o_ref):
        o_ref[...] = pltpu.roll(x_ref[...], n_ref[0], axis=1)[:, :width]

    return pl.pallas_call(
        body, grid_spec=pltpu.PrefetchScalarGridSpec(
            num_scalar_prefetch=1, grid=(r // tr,), in_specs=[pl.BlockSpec((tr, cw), lambda i, n: (i, 0))],
            out_specs=pl.BlockSpec((tr, width), lambda i, n: (i, 0))),
        out_shape=SDS((r, width), F32), compiler_params=_params(1), name="uncontain")(n_head, cont)


WEIGHTS = ("meta_tokens", "norm_w", "w_in", "conv_w", "a_log", "dt_bias", "gnorm_a", "gnorm_b", "hgrn_lower_bounds",
           "w_branch_a", "w_branch_b", "w_out", "final_norm_w")
SHARD_AXIS = {"meta_tokens": 1, "w_in": 2, "conv_w": 2, "w_branch_a": 2, "w_branch_b": 2, "w_out": 1}
FLAT_C = 1024


def _flat(parts, rows, cols=FLAT_C):
    v = jnp.concatenate([p.reshape(-1) for p in parts])
    return jnp.pad(v, (0, rows * cols - v.shape[0])).reshape(rows, cols)


def _local_step(x, target, w, lay):
    nb, seq, d = x.shape
    tp = CH + seq
    nc = tp // CH
    n = nb * tp
    e_mat, s_mat = _gate_consts()
    lb_all = _lb_fwd(w["hgrn_lower_bounds"])
    h = jnp.concatenate([jnp.zeros((nb, PAD, d), F32), jnp.broadcast_to(w["meta_tokens"][None], (nb, N_META, d)), x],
                        axis=1).reshape(n, d)
    rep = lambda a: jnp.repeat(a, HD)[None, :]
    saved = []
    for l in range(DEPTH):
        nw = w["norm_w"][l][None, :]
        proj, xn = _norm_proj_fwd(h, nw, w["w_in"][l])
        qkv = _gdn_prep_fwd(proj, w["conv_w"][l], lay, nb, tp)
        alog, dtb = rep(w["a_log"][l]), rep(w["dt_bias"][l])
        oa, sa, sva = _gdn_fwd(qkv, proj, e_mat, alog, dtb, lay, nb, nc)
        lbl = lb_all[l][None, :]
        ob, sb = _hgrn_fwd(proj, lbl, lay, nb, nc)
        ga, gb = w["gnorm_a"][l][None, :], w["gnorm_b"][l][None, :]
        hn = _merge_fwd(h, oa, ob, proj, ga, gb, w["w_branch_a"][l], w["w_branch_b"][l], w["w_out"][l], lay)
        saved.append((h, nw, proj, qkv, alog, dtb, oa, sa, lbl, ob, sb, ga, gb, xn, sva))
        h = hn
    lp, dh, dfw = _loss_head(h, target.reshape(nb * seq, d), w["final_norm_w"][None, :], nb, nc)
    loss = jnp.sum(lp[::8, 0])
    g = {n_: [None] * DEPTH for n_ in WEIGHTS}
    dlb_all = [None] * DEPTH
    for l in reversed(range(DEPTH)):
        h, nw, proj, qkv, alog, dtb, oa, sa, lbl, ob, sb, ga, gb, xn, sva = saved[l]
        dproj, doa, dob, dwa, dwb, dwo, dga, dgb = _merge_bwd(dh, oa, ob, proj, ga, gb, w["w_branch_a"][l],
                                                             w["w_branch_b"][l], w["w_out"][l], lay, tp)
        dproj, acc_b = _hgrn_bwd(proj, lbl, sb, dob, dproj, lay, nb, nc)
        dqkv, dproj, acc_a = _gdn_bwd(qkv, proj, e_mat, s_mat, alog, dtb, sa, sva, doa, dproj, lay, nb, nc)
        dproj, dconv = _gdn_prep_bwd(proj, w["conv_w"][l], dqkv, dproj, lay, nb, tp)
        dh, dnw = _proj_bwd_dx(dproj, w["w_in"][l], h, nw, dh, tp)
        g["w_in"][l] = _proj_bwd_dw(dproj, xn, tp)
        g["norm_w"][l] = dnw[0]
        g["conv_w"][l] = dconv
        g["a_log"][l] = acc_a[0, ::HD]
        g["dt_bias"][l] = acc_a[1, ::HD]
        g["gnorm_a"][l], g["gnorm_b"][l] = dga[0], dgb[0]
        g["w_branch_a"][l], g["w_branch_b"][l], g["w_out"][l] = dwa, dwb, dwo
        dlb_all[l] = acc_b[0]
    grads = {n_: jnp.stack(v) for n_, v in g.items() if v[0] is not None}
    grads["hgrn_lower_bounds"] = _lb_bwd(w["hgrn_lower_bounds"], jnp.stack(dlb_all))
    grads["final_norm_w"] = dfw[0]
    grads["meta_tokens"] = _meta_grad(dh, nb, nc)
    grad_x = dh.reshape(nb, tp, d)[:, CH:, :]
    return loss, grad_x, grads


def kernel(x, meta_tokens, norm_w, w_in, conv_w, a_log, dt_bias, gnorm_a, gnorm_b, hgrn_lower_bounds, w_branch_a, w_branch_b, w_out, final_norm_w, loss_target, m_meta_tokens, m_norm_w, m_w_in, m_conv_w, m_a_log, m_dt_bias, m_gnorm_a, m_gnorm_b, m_hgrn_lower_bounds, m_w_branch_a, m_w_branch_b, m_w_out, m_final_norm_w, v_meta_tokens, v_norm_w, v_w_in, v_conv_w, v_a_log, v_dt_bias, v_gnorm_a, v_gnorm_b, v_hgrn_lower_bounds, v_w_branch_a, v_w_branch_b, v_w_out, v_final_norm_w):
    wl = dict(meta_tokens=meta_tokens, norm_w=norm_w, w_in=w_in, conv_w=conv_w, a_log=a_log, dt_bias=dt_bias, gnorm_a=gnorm_a,
              gnorm_b=gnorm_b, hgrn_lower_bounds=hgrn_lower_bounds, w_branch_a=w_branch_a, w_branch_b=w_branch_b, w_out=w_out,
              final_norm_w=final_norm_w)
    ml = dict(zip(WEIGHTS, (m_meta_tokens, m_norm_w, m_w_in, m_conv_w, m_a_log, m_dt_bias, m_gnorm_a, m_gnorm_b,
                            m_hgrn_lower_bounds, m_w_branch_a, m_w_branch_b, m_w_out, m_final_norm_w)))
    vl = dict(zip(WEIGHTS, (v_meta_tokens, v_norm_w, v_w_in, v_conv_w, v_a_log, v_dt_bias, v_gnorm_a, v_gnorm_b,
                            v_hgrn_lower_bounds, v_w_branch_a, v_w_branch_b, v_w_out, v_final_norm_w)))
    d = x.shape[2]
    lay = _Layout(d)
    nchip = 4

    big = ("w_in", "w_branch_a", "w_branch_b", "w_out")
    small = ("conv_w", "meta_tokens")
    table, heads, cw = lay.pieces(nchip)
    sw = wl["w_in"].shape[2]
    chip_id = (2 * lax.axis_index("x") + lax.axis_index("y")).astype(jnp.int32)
    n_head = sum(jnp.where(chip_id == s, heads[s], 0) for s in range(nchip)).astype(jnp.int32)
    w_pad = jnp.pad(wl["w_in"], ((0, 0), (0, 0), (0, cw - sw))).reshape(DEPTH * d, cw)
    shift = jnp.where(n_head == 0, 0, cw - n_head).astype(jnp.int32).reshape(1)
    pbs = [_contain(w_pad, shift).reshape(DEPTH, d, cw)] + [wl[n].astype(BF16) for n in big[1:]]
    nsmall = sum(int(np.prod(wl[n].shape)) for n in small)
    rs = -(-nsmall // (HD * 8)) * 8
    ps = jnp.pad(jnp.concatenate([wl[n].reshape(-1) for n in small]), (0, rs * HD - nsmall)).reshape(rs, HD)
    gbig, gsmall = _gather_weights(pbs, ps)
    gbig = [_place_own(g, p, chip_id.reshape(1)) for g, p in zip(gbig, pbs)]
    gsmall = gsmall.reshape(nchip, -1)

    wf = dict(wl)
    wf["w_in"] = lay.from_containers([gbig[0][s] for s in range(nchip)])
    for i, n in enumerate(big[1:], start=1):
        wf[n] = jnp.concatenate([gbig[i][s] for s in range(nchip)], axis=SHARD_AXIS[n])
    o = 0
    for n in small:
        sz = int(np.prod(wl[n].shape))
        a = gsmall[:, o:o + sz].reshape((nchip,) + wl[n].shape)
        wf[n] = jnp.concatenate([a[s] for s in range(nchip)], axis=SHARD_AXIS[n])
        o += sz

    loss_part, grad_x, gfull = _local_step(x, loss_target, wf, lay)
    loss = lax.psum(loss_part, ("x", "y", "c"))

    sw = wl["w_in"].shape[2]
    conts, heads = lay.containers(gfull["w_in"], nchip)
    dd = wl["w_branch_a"].shape[2]
    rows_o = wl["w_out"].shape[1]
    by_dest = lambda g, n: [lax.slice_in_dim(g, s * wl[n].shape[SHARD_AXIS[n]], (s + 1) * wl[n].shape[SHARD_AXIS[n]],
                                            axis=SHARD_AXIS[n]) if n in SHARD_AXIS else g for s in range(nchip)]
    small_names = tuple(n for n in WEIGHTS if n not in big)
    nsm = sum(int(np.prod(wl[n].shape)) for n in small_names)
    rsm = -(-nsm // (2 * HD * 8)) * 8
    pack_small = lambda parts: _flat(parts, 2 * rsm, HD).reshape(2, rsm, HD)
    small_by_dest = [by_dest(gfull[n], n) for n in small_names]
    gs = [jnp.stack(conts, axis=1),
          jnp.stack(by_dest(gfull["w_branch_a"], "w_branch_a"), axis=1),
          jnp.stack(by_dest(gfull["w_branch_b"], "w_branch_b"), axis=1),
          gfull["w_out"].reshape(DEPTH, nchip, rows_o, d),
          jnp.stack([pack_small([p[s] for p in small_by_dest]) for s in range(nchip)], axis=1)]
    gs = [g.reshape((2, nchip, -1, g.shape[-1])) for g in gs]
    my_chip = (2 * lax.axis_index("x") + lax.axis_index("y")).astype(jnp.int32)
    my_core = lax.axis_index("c").astype(jnp.int32)
    got = _swap_halves([g.astype(BF16) for g in gs[:4]] + gs[4:])
    chip_sums = [_add_cores(g, b, my_core.reshape(1)) for g, b in zip(gs, got)]
    by_chip = _scatter_chip_sums(chip_sums)
    place = jnp.stack([my_chip, my_core])
    full = _join_halves([_sum_chips(p, a, place) for p, a in zip(by_chip, chip_sums)])
    n_head = sum(jnp.where(my_chip == s, heads[s], 0) for s in range(nchip)).astype(jnp.int32).reshape(1)
    g_w_in = _uncontain(full[0].reshape(DEPTH * d, -1), n_head, sw)
    g2 = {"w_in": g_w_in, "w_branch_a": full[1].reshape(-1, dd), "w_branch_b": full[2].reshape(-1, dd),
          "w_out": full[3].reshape(-1, d), "small": full[4].reshape(2 * rsm, HD)}

    def two_d(src, n):
        if n == "small":
            return _flat([src[k] for k in small_names], 2 * rsm, HD)
        return src[n].reshape(g2[n].shape)

    outs = {}
    for n in big + ("small",):
        delta, mnew, vnew = _adamw(g2[n], two_d(wl, n), two_d(ml, n), two_d(vl, n))
        outs[n] = (g2[n], delta, mnew, vnew)
    res = [{}, {}, {}, {}]
    for i in range(4):
        for n in big:
            res[i][n] = outs[n][i].reshape(wl[n].shape)
        v, o = outs["small"][i].reshape(-1), 0
        for n in small_names:
            sz = int(np.prod(wl[n].shape))
            res[i][n] = v[o:o + sz].reshape(wl[n].shape)
            o += sz
    return (loss, grad_x, *[res[0][n] for n in WEIGHTS], *[res[1][n] for n in WEIGHTS], *[res[2][n] for n in WEIGHTS],
            *[res[3][n] for n in WEIGHTS])
```

```python
import functools

import numpy as np
import jax
import jax.numpy as jnp
from jax import lax
from jax.experimental import pallas as pl
from jax.experimental.pallas import tpu as pltpu

F32 = jnp.float32
BF16 = jnp.bfloat16
HI = lax.Precision.HIGHEST
SDS = jax.ShapeDtypeStruct

NH = 4
HD = 128
HW = NH * HD
N_META = 16
CH = 64
SUB = 16
PAD = CH - N_META
EPS = 1e-6
Q_SCALE = HD ** -0.5
DEPTH = 2
CONV_K = 4
VMEM_LIMIT = 56 * 1024 * 1024
ADAM_LR, ADAM_B1, ADAM_B2, ADAM_EPS, ADAM_WD, ADAM_STEP = 0.001, 0.9, 0.999, 1e-08, 0.01, 10
MESH = pl.DeviceIdType.MESH


def _nn(a, b):
    return jnp.dot(a, b, precision=HI, preferred_element_type=F32)


def _nt(a, b):
    return lax.dot_general(a, b, (((1,), (1,)), ((), ())), precision=HI, preferred_element_type=F32)


def _tn(a, b):
    return _nn(a.T, b)


def _scan_rows(x, group, reverse=False):
    n = x.shape[0]
    pos = lax.bitwise_and(_iota2(x.shape, 0), group - 1)
    s = 1
    while s < group:
        if reverse:
            x = x + jnp.where(pos < group - s, pltpu.roll(x, n - s, axis=0), 0.0)
        else:
            x = x + jnp.where(pos >= s, pltpu.roll(x, s, axis=0), 0.0)
        s *= 2
    return x


def _bnn(a, b):
    return jnp.dot(a.astype(BF16), b.astype(BF16), preferred_element_type=F32)


def _bnt(a, b):
    return lax.dot_general(a.astype(BF16), b.astype(BF16), (((1,), (1,)), ((), ())), preferred_element_type=F32)


def _btn(a, b):
    return lax.dot_general(a.astype(BF16), b.astype(BF16), (((0,), (0,)), ((), ())), preferred_element_type=F32)


def _hi_lo(x):
    hi = x.astype(jnp.bfloat16)
    return hi, (x - hi.astype(F32)).astype(jnp.bfloat16)


def _dot3(dims):
    def f(a, b):
        ah, al = _hi_lo(a)
        bh, bl = _hi_lo(b)
        d = lambda p, q: lax.dot_general(p, q, (dims, ((), ())), preferred_element_type=F32)
        return d(ah, bh) + (d(ah, bl) + d(al, bh))
    return f


_rnn, _rnt, _rtn = _dot3(((1,), (0,))), _dot3(((1,), (1,))), _dot3(((0,), (0,)))
_enn, _ent, _etn = _bnn, _bnt, _btn
_hnn, _hnt, _htn = _bnn, _bnt, _btn


def _rr(x):
    return x


def _sig(x):
    return jax.nn.sigmoid(x)


def _silu(x):
    return x * _sig(x)


def _dsilu(x):
    s = _sig(x)
    return s * (1.0 + x * (1.0 - s))


def _softplus(x):
    return jnp.maximum(x, 0.0) + jnp.log(1.0 + jnp.exp(-jnp.abs(x)))


def _logsig(x):
    return jnp.minimum(x, 0.0) - jnp.log(1.0 + jnp.exp(-jnp.abs(x)))


def _rs(x):
    return jnp.sum(x, axis=-1, keepdims=True)


def _params(n_axes):
    return pltpu.CompilerParams(dimension_semantics=("arbitrary",) * n_axes, vmem_limit_bytes=VMEM_LIMIT)


def _tile(n, target, mult=8):
    best = mult
    for t in range(mult, target + 1, mult):
        if n % t == 0:
            best = t
    assert n % best == 0, (n, mult)
    return best


def _ctile(pw, most=7):
    return HD * max(k for k in range(1, most + 1) if (pw // HD) % k == 0)


def _iota2(shape, axis):
    return lax.broadcasted_iota(jnp.int32, shape, axis)


class _Layout:
    def __init__(self, d):
        self.d = d
        self.wm = 2 * HW + 2 * d
        self.c_qkv = self.wm
        self.c_b = self.wm + 3 * HW
        self.c_ba = self.wm + 6 * HW
        self.pw = self.c_ba + HD
        assert self.c_b % (3 * HW) == 0
        o = 0
        segs = {}
        for name, w in (("a_q", HW), ("a_k", HW), ("a_v", HW), ("ba", 2 * NH), ("a_z", HW), ("b_q", HW), ("b_f", HW),
                        ("b_i", HW), ("b_g", HW), ("gate_a", d), ("gate_b", d)):
            segs[name] = (o, o + w)
            o += w
        self.segs = segs
        self.width = o
        self.order = ("a_z", "b_g", "gate_a", "gate_b", "a_q", "a_k", "a_v", "b_q", "b_f", "b_i", "ba")

    def to_kernel(self, w):
        parts = [w[..., self.segs[n][0]:self.segs[n][1]] for n in self.order]
        parts.append(jnp.zeros(w.shape[:-1] + (HD - 2 * NH,), w.dtype))
        return jnp.concatenate(parts, axis=-1)

    def containers(self, g, nchip):
        table, heads, cw = self.pieces(nchip)
        out = []
        for s in range(nchip):
            parts, at = [], 0
            for kcol, w, ccol in sorted(table[s], key=lambda p: p[2]):
                if ccol > at:
                    parts.append(jnp.zeros(g.shape[:-1] + (ccol - at,), g.dtype))
                parts.append(g[..., kcol:kcol + w])
                at = ccol + w
            if at < cw:
                parts.append(jnp.zeros(g.shape[:-1] + (cw - at,), g.dtype))
            out.append(jnp.concatenate(parts, axis=-1))
        return out, heads

    def pieces(self, nchip):
        off, where = 0, {}
        for n in self.order:
            where[n] = off
            off += self.segs[n][1] - self.segs[n][0]
        names = sorted(self.segs, key=lambda n: self.segs[n][0])
        sw = self.width // nchip
        cw = -(-sw // HD) * HD
        table, heads = [], []
        for s in range(nchip):
            lo, hi = s * sw, (s + 1) * sw
            pieces = []
            for n in names:
                a, b = max(lo, self.segs[n][0]), min(hi, self.segs[n][1])
                if a < b:
                    pieces.append((where[n] + a - self.segs[n][0], b - a))
            start, width = pieces[0]
            n_head = min((-start) % HD, width)
            body = ([(start + n_head, width - n_head)] if width > n_head else []) + pieces[1:]
            rows, at = [], 0
            for c, w in body:
                rows.append((c, w, at))
                at += w
            if n_head:
                rows.append((start, n_head, cw - n_head))
            table.append(rows)
            heads.append(n_head)
        return table, heads, cw

    def from_containers(self, conts):
        table, _, _ = self.pieces(len(conts))
        cut = sorted((kcol, w, s, ccol) for s, rows in enumerate(table) for kcol, w, ccol in rows)
        parts, at = [], 0
        for kcol, w, s, ccol in cut:
            assert kcol == at, (kcol, at)
            parts.append(conts[s][..., ccol:ccol + w])
            at = kcol + w
        parts.append(jnp.zeros(conts[0].shape[:-1] + (self.pw - at,), conts[0].dtype))
        return jnp.concatenate(parts, axis=-1)

    def from_kernel(self, g):
        off, where = 0, {}
        for n in self.order:
            w = self.segs[n][1] - self.segs[n][0]
            where[n] = (off, off + w)
            off += w
        names = sorted(self.segs, key=lambda n: self.segs[n][0])
        return jnp.concatenate([g[..., where[n][0]:where[n][1]] for n in names], axis=-1)


def _norm_proj_fwd(h, nw, wp):
    n, d = h.shape
    pw = wp.shape[1]
    tm, tn = _tile(n, 1408, HD), _ctile(pw)

    def body(h_ref, nw_ref, w_ref, o_ref, xt_ref, xn_ref):
        @pl.when(pl.program_id(1) == 0)
        def _():
            x = h_ref[...]
            r = lax.rsqrt(jnp.mean(x * x, axis=-1, keepdims=True) + EPS)
            xn = (x * r * nw_ref[...]).astype(BF16)
            xn_ref[...] = xn
            xt_ref[...] = xn.T

        o_ref[...] = jnp.dot(xn_ref[...], w_ref[...], preferred_element_type=F32)

    return pl.pallas_call(
        body, grid=(n // tm, pw // tn),
        in_specs=[pl.BlockSpec((tm, d), lambda i, j: (i, 0)), pl.BlockSpec((1, d), lambda i, j: (0, 0)),
                  pl.BlockSpec((d, tn), lambda i, j: (0, j))],
        out_specs=[pl.BlockSpec((tm, tn), lambda i, j: (i, j)), pl.BlockSpec((d, tm), lambda i, j: (0, i))],
        out_shape=[SDS((n, pw), F32), SDS((d, n), BF16)], scratch_shapes=[pltpu.VMEM((tm, d), BF16)],
        compiler_params=_params(2), name="norm_proj_fwd")(h, nw, wp)


def _row_valid(tm, tp, base):
    row = base + _iota2((tm, 1), 0)
    return lax.rem(row, tp) >= PAD


def _proj_bwd_dx(dproj, wp, h, nw, dhn, tp):
    n, d = h.shape
    pw = wp.shape[1]
    tm, tk = _tile16(n, 768), _ctile(pw)
    nk = pw // tk

    def body(dp_ref, w_ref, h_ref, nw_ref, dhn_ref, dh_ref, dnw_ref, acc_ref):
        i, k = pl.program_id(0), pl.program_id(1)

        @pl.when(k == 0)
        def _():
            acc_ref[...] = jnp.zeros_like(acc_ref)

        @pl.when((i == 0) & (k == 0))
        def _():
            dnw_ref[...] = jnp.zeros_like(dnw_ref)

        valid = _row_valid(tm, tp, i * tm)
        dp = jnp.where(valid, dp_ref[...], 0.0)
        acc_ref[...] += _bnt(dp, w_ref[...])

        @pl.when(k == nk - 1)
        def _():
            x = h_ref[...]
            r = lax.rsqrt(jnp.mean(x * x, axis=-1, keepdims=True) + EPS)
            xh = x * r
            dxn = acc_ref[...]
            dnw_ref[...] += jnp.sum(dxn * xh, axis=0, keepdims=True)
            dxh = dxn * nw_ref[...]
            dh_ref[...] = dhn_ref[...] + r * (dxh - xh * jnp.mean(dxh * xh, axis=-1, keepdims=True))

    return pl.pallas_call(
        body, grid=(n // tm, nk),
        in_specs=[pl.BlockSpec((tm, tk), lambda i, k: (i, k)), pl.BlockSpec((d, tk), lambda i, k: (0, k)),
                  pl.BlockSpec((tm, d), lambda i, k: (i, 0)), pl.BlockSpec((1, d), lambda i, k: (0, 0)),
                  pl.BlockSpec((tm, d), lambda i, k: (i, 0))],
        out_specs=[pl.BlockSpec((tm, d), lambda i, k: (i, 0)), pl.BlockSpec((1, d), lambda i, k: (0, 0))],
        out_shape=[SDS((n, d), F32), SDS((1, d), F32)],
        scratch_shapes=[pltpu.VMEM((tm, d), F32)], compiler_params=_params(2), name="proj_bwd_dx")(dproj, wp, h, nw, dhn)


def _proj_bwd_dw(dproj, xt, tp):
    d, n = xt.shape
    pw = dproj.shape[1]
    tm, tn = _tile(n, 1408, HD), _ctile(pw)

    def body(dp_ref, xt_ref, dw_ref):
        i = pl.program_id(1)

        @pl.when(i == 0)
        def _():
            dw_ref[...] = jnp.zeros_like(dw_ref)

        dp = jnp.where(_row_valid(tm, tp, i * tm), dp_ref[...], 0.0)
        dw_ref[...] += jnp.dot(xt_ref[...], dp.astype(BF16), preferred_element_type=F32)

    return pl.pallas_call(
        body, grid=(pw // tn, n // tm),
        in_specs=[pl.BlockSpec((tm, tn), lambda j, i: (i, j)), pl.BlockSpec((d, tm), lambda j, i: (0, i))],
        out_specs=pl.BlockSpec((d, tn), lambda j, i: (0, j)), out_shape=SDS((d, pw), F32),
        compiler_params=_params(2), name="proj_bwd_dw")(dproj, xt)


def _conv_silu(x, w, row):
    c = x * w[CONV_K - 1:CONV_K, :]
    for k in range(1, CONV_K):
        c = c + jnp.where(row >= k, pltpu.roll(x, k, axis=0), 0.0) * w[CONV_K - 1 - k:CONV_K - k, :]
    return c


def _gdn_prep_fwd(proj, conv_w, lay, nb, tp):
    n = proj.shape[0]
    nblk = 3 * NH
    cb = lay.c_qkv // HD

    def body(p_ref, w_ref, o_ref):
        j = pl.program_id(1)
        x = p_ref[...]
        row = _iota2(x.shape, 0)
        c = _conv_silu(x, w_ref[...], row)
        s = _silu(c)
        r = lax.rsqrt(_rs(s * s) + EPS)
        scale = jnp.where(j < NH, Q_SCALE, 1.0)
        y = jnp.where(j < 2 * NH, s * r * scale, s)
        o_ref[...] = jnp.where(row >= PAD, y, 0.0)

    return pl.pallas_call(
        body, grid=(nb, nblk),
        in_specs=[pl.BlockSpec((tp, HD), lambda b, j: (b, cb + j)), pl.BlockSpec((CONV_K, HD), lambda b, j: (0, j))],
        out_specs=pl.BlockSpec((tp, HD), lambda b, j: (b, j)), out_shape=SDS((n, nblk * HD), F32),
        compiler_params=_params(2), name="gdn_prep_fwd")(proj, conv_w)


def _gdn_prep_bwd(proj, conv_w, dqkv, dproj, lay, nb, tp):
    nblk = 3 * NH
    cb = lay.c_qkv // HD

    def body(p_ref, w_ref, dy_ref, dp_in, dp_ref, dw_ref):
        j, b = pl.program_id(0), pl.program_id(1)
        x = p_ref[...]
        w = w_ref[...]
        row = _iota2(x.shape, 0)
        c = _conv_silu(x, w, row)
        s = _silu(c)
        dy = jnp.where(row >= PAD, dy_ref[...], 0.0)
        r = lax.rsqrt(_rs(s * s) + EPS)
        nh = s * r
        scale = jnp.where(j < NH, Q_SCALE, 1.0)
        ds_n = scale * r * (dy - nh * _rs(dy * nh))
        ds = jnp.where(j < 2 * NH, ds_n, dy)
        dc = ds * _dsilu(c)
        dx = dc * w[CONV_K - 1:CONV_K, :]
        dws = [jnp.sum(dc * x, axis=0, keepdims=True)]
        for k in range(1, CONV_K):
            dx = dx + jnp.where(row < tp - k, pltpu.roll(dc, tp - k, axis=0), 0.0) * w[CONV_K - 1 - k:CONV_K - k, :]
            xs = jnp.where(row >= k, pltpu.roll(x, k, axis=0), 0.0)
            dws.append(jnp.sum(dc * xs, axis=0, keepdims=True))
        dp_ref[...] = dx.astype(dp_ref.dtype)
        r4 = _iota2((CONV_K, HD), 0)
        dw = jnp.zeros((CONV_K, HD), F32)
        for k in range(CONV_K):
            dw = dw + jnp.where(r4 == CONV_K - 1 - k, dws[k], 0.0)

        @pl.when(b == 0)
        def _():
            dw_ref[...] = dw

        @pl.when(b > 0)
        def _():
            dw_ref[...] += dw

    return pl.pallas_call(
        body, grid=(nblk, nb),
        in_specs=[pl.BlockSpec((tp, HD), lambda j, b: (b, cb + j)), pl.BlockSpec((CONV_K, HD), lambda j, b: (0, j)),
                  pl.BlockSpec((tp, HD), lambda j, b: (b, j)), pl.BlockSpec(memory_space=pl.ANY)],
        out_specs=[pl.BlockSpec((tp, HD), lambda j, b: (b, cb + j)), pl.BlockSpec((CONV_K, HD), lambda j, b: (0, j))],
        out_shape=[SDS(dproj.shape, dproj.dtype), SDS((CONV_K, nblk * HD), F32)],
        input_output_aliases={3: 0}, compiler_params=_params(2), name="gdn_prep_bwd")(proj, conv_w, dqkv, dproj)


def _gate_consts():
    e = np.zeros((HD, 2 * HW), np.float32)
    s = np.zeros((2 * HW, HD), np.float32)
    for h in range(NH):
        e[h, h * HD:(h + 1) * HD] = 1.0
        e[NH + h, HW + h * HD:HW + (h + 1) * HD] = 1.0
        s[h * HD, h] = 1.0
        s[HW + h * HD, NH + h] = 1.0
    return jnp.asarray(e), jnp.asarray(s)


def _gdn_tri():
    i, j = _iota2((CH, CH), 0), _iota2((CH, CH), 1)
    return i >= j, i > j


def _each(fn, *lists):
    return [fn(*xs) for xs in zip(*lists)]


def _tri_inv(a_list, eye):
    p = [-a for a in a_list]
    t = [eye + x for x in p]
    for _ in range(5):
        p = _each(_rnn, p, p)
        tp_ = _each(_rnn, t, p)
        t = _each(lambda x, y: x + y, t, tp_)
    return t


def _gdn_chunks(args, solved=None):
    causal, strict = _gdn_tri()
    eye = jnp.where(_iota2((CH, CH), 0) == _iota2((CH, CH), 1), 1.0, 0.0)
    q, k, v, beta, g, s0 = (list(t) for t in zip(*args))
    gc = [_scan_rows(x, CH) for x in g]
    dm = [jnp.where(causal, jnp.exp(jnp.where(causal, x[:, :CH] - x[:, :CH].T, 0.0)), 0.0) for x in gc]
    ds = [jnp.where(strict, x, 0.0) for x in dm]
    kb = _each(lambda x, y: x * y, k, beta)
    kk = _each(_ent, kb, k)
    a = _each(lambda x, y: x * y, kk, ds)
    eg = [jnp.exp(x) for x in gc]
    rw = _each(lambda x, y: x * y, kb, eg)
    if solved is None:
        tinv = _tri_inv(a, eye)
        rv = _each(lambda x, y: x * y, v, beta)
        u = _each(_rnn, tinv, rv)
        w = _each(_rnn, tinv, rw)
    else:
        tinv, u, w = (list(t) for t in zip(*solved))
    ws = _each(_enn, w, s0)
    vn = _each(lambda x, y: x - y, u, ws)
    qk = _each(_ent, q, k)
    p = _each(lambda x, y: x * y, qk, dm)
    qg = _each(lambda x, y: x * y, q, eg)
    out = []
    for i in range(len(args)):
        gl = gc[i][CH - 1:CH, :]
        ek = jnp.exp(gl - gc[i])
        out.append(dict(gc=gc[i], dm=dm[i], ds=ds[i], kb=kb[i], a=a[i], tinv=tinv[i], eg=eg[i], rw=rw[i], u=u[i], w=w[i],
                        vn=vn[i], p=p[i], qg=qg[i], egl=jnp.exp(gl), ek=ek, kd=k[i] * ek))
    return out


def _gdn_gates(ba, e, alog, dtb):
    raw = _nn(ba, e)
    beta = _sig(raw[:, :HW])
    za = raw[:, HW:] + dtb
    g = -jnp.exp(alog) * _softplus(za)
    return beta, g, za


def _seqs_per_step(nb):
    return 4 if nb % 4 == 0 else (2 if nb % 2 == 0 else 1)


def _gdn_fwd(qkv, proj, e_mat, alog, dtb, lay, nb, nc):
    n = qkv.shape[0]
    tp = n // nb
    cba = lay.c_ba // HD
    gb = _seqs_per_step(nb)

    def body(x_ref, ba_ref, e_ref, al_ref, dt_ref, o_ref, so_ref, sv_ref, s_ref):
        @pl.when(pl.program_id(1) == 0)
        def _():
            s_ref[...] = jnp.zeros_like(s_ref)

        args = []
        for j in range(gb):
            beta, g, _ = _gdn_gates(ba_ref[j], e_ref[...], al_ref[...], dt_ref[...])
            for h in range(NH):
                hs = slice(h * HD, (h + 1) * HD)
                args.append((x_ref[j, :, hs], x_ref[j, :, HW + h * HD:HW + (h + 1) * HD],
                             x_ref[j, :, 2 * HW + h * HD:2 * HW + (h + 1) * HD], beta[:, hs], g[:, hs], s_ref[j, h]))
        cs = _gdn_chunks(args)
        s0s = [a[5] for a in args]
        o1 = _each(lambda c, s0: _enn(c["qg"], s0), cs, s0s)
        o2 = [_enn(c["p"], c["vn"]) for c in cs]
        upd = [_etn(c["kd"], c["vn"]) for c in cs]
        res = [(o1[i] + o2[i], s0s[i] * cs[i]["egl"] + upd[i]) for i in range(len(cs))]
        zero = jnp.zeros((CH, HD - CH), F32)
        for j in range(gb):
            for h in range(NH):
                c = cs[j * NH + h]
                so_ref[j, h] = args[j * NH + h][5]
                sv_ref[j, h] = jnp.concatenate([c["u"], c["w"], c["tinv"], zero], axis=-1)
                s_ref[j, h] = res[j * NH + h][1]
            o_ref[j] = jnp.concatenate([res[j * NH + h][0] for h in range(NH)], axis=-1)

    o, st, sv = pl.pallas_call(
        body, grid=(nb // gb, nc),
        in_specs=[pl.BlockSpec((gb, CH, 3 * HW), lambda b, c: (b, c, 0)), pl.BlockSpec((gb, CH, HD), lambda b, c: (b, c, cba)),
                  pl.BlockSpec((HD, 2 * HW), lambda b, c: (0, 0)), pl.BlockSpec((1, HW), lambda b, c: (0, 0)),
                  pl.BlockSpec((1, HW), lambda b, c: (0, 0))],
        out_specs=[pl.BlockSpec((gb, CH, HW), lambda b, c: (b, c, 0)),
                   pl.BlockSpec((gb, None, NH, HD, HD), lambda b, c: (b, c, 0, 0, 0)),
                   pl.BlockSpec((gb, None, NH, CH, 3 * HD), lambda b, c: (b, c, 0, 0, 0))],
        out_shape=[SDS((nb, tp, HW), F32), SDS((nb, nc, NH, HD, HD), F32), SDS((nb, nc, NH, CH, 3 * HD), F32)],
        scratch_shapes=[pltpu.VMEM((gb, NH, HD, HD), F32)], compiler_params=_params(2), name="gdn_fwd")(
            qkv.reshape(nb, tp, 3 * HW), proj.reshape(nb, tp, -1), e_mat, alog, dtb)
    return o.reshape(n, HW), st, sv


def _gdn_bwd(qkv, proj, e_mat, s_mat, alog, dtb, states, solved, do, dproj, lay, nb, nc):
    n = qkv.shape[0]
    tp = n // nb
    cba = lay.c_ba // HD
    gb = _seqs_per_step(nb)

    def body(x_ref, ba_ref, e_ref, sm_ref, al_ref, dt_ref, st_ref, sv_ref, do_ref, dp_in, dx_ref, dba_ref, acc_ref, ds_ref):
        ci = pl.program_id(1)

        @pl.when(ci == 0)
        def _():
            ds_ref[...] = jnp.zeros_like(ds_ref)

        @pl.when((ci == 0) & (pl.program_id(0) == 0))
        def _():
            acc_ref[...] = jnp.zeros_like(acc_ref)

        causal, strict = _gdn_tri()
        alog = al_ref[...]
        row = _iota2((CH, 1), 0)
        valid = (row >= PAD) | (ci < nc - 1)
        last = row == CH - 1
        gates = [_gdn_gates(ba_ref[j], e_ref[...], alog, dt_ref[...]) for j in range(gb)]
        args, do, ds1, solved = [], [], [], []
        for j in range(gb):
            beta, g, _ = gates[j]
            for h in range(NH):
                hs = slice(h * HD, (h + 1) * HD)
                args.append((x_ref[j, :, hs], x_ref[j, :, HW + h * HD:HW + (h + 1) * HD],
                             x_ref[j, :, 2 * HW + h * HD:2 * HW + (h + 1) * HD], beta[:, hs], g[:, hs], st_ref[j, h]))
                do.append(do_ref[j, :, hs])
                ds1.append(ds_ref[j, h])
                solved.append((sv_ref[j, h, :, 2 * HD:2 * HD + CH], sv_ref[j, h, :, 0:HD], sv_ref[j, h, :, HD:2 * HD]))
        q, k, v, bh, _, s0 = (list(t) for t in zip(*args))
        cs = _gdn_chunks(args, solved)
        get = lambda name: [c[name] for c in cs]
        mul = lambda x, y: x * y
        add = lambda x, y: x + y
        dvn = _each(add, _each(_etn, get("p"), do), _each(_enn, get("kd"), ds1))
        dqg = _each(_ent, do, s0)
        dp = [jnp.where(causal, x, 0.0) for x in _each(_ent, do, get("vn"))]
        dkd = _each(_ent, get("vn"), ds1)
        dw = [-x for x in _each(_ent, dvn, s0)]
        ds_a = _each(_etn, get("qg"), do)
        ds_b = _each(_etn, get("w"), dvn)
        ds_new = [ds_a[i] - ds_b[i] + ds1[i] * cs[i]["egl"] for i in range(len(cs))]
        drv = _each(_rtn, get("tinv"), dvn)
        drw = _each(_rtn, get("tinv"), dw)
        da_1 = _each(_rnt, drv, get("u"))
        da_2 = _each(_rnt, drw, get("w"))
        da = [jnp.where(strict, -(x + y), 0.0) for x, y in zip(da_1, da_2)]
        m = [da[i] * cs[i]["a"] + dp[i] * cs[i]["p"] for i in range(len(cs))]
        dkk = _each(mul, da, get("ds"))
        dqk = _each(mul, dp, get("dm"))
        dq = _each(add, _each(_enn, dqk, k), _each(mul, dqg, get("eg")))
        dkb = _each(add, _each(_enn, dkk, k), _each(mul, drw, get("eg")))
        dk_1 = _each(_etn, dqk, q)
        dk_2 = _each(_etn, dkk, get("kb"))
        dk = [dk_1[i] + dk_2[i] + dkd[i] * cs[i]["ek"] + dkb[i] * bh[i] for i in range(len(cs))]
        dv = _each(mul, drv, bh)
        dbeta, dg = [], []
        for i, c in enumerate(cs):
            dbeta.append(_rs(drv[i] * v[i]) + _rs(dkb[i] * k[i]) + jnp.zeros((CH, HD), F32))
            t_kd = _rs(dkd[i] * c["kd"])
            dgc = _rs(m[i]) - _rs(m[i].T) + _rs(dqg[i] * c["qg"]) + _rs(drw[i] * c["rw"]) - t_kd
            tail = jnp.sum(t_kd, axis=0, keepdims=True) + c["egl"] * jnp.sum(_rs(s0[i] * ds1[i]), axis=0, keepdims=True)
            dgc = dgc + jnp.where(last, tail, 0.0)
            dg.append(_scan_rows(dgc + jnp.zeros((CH, HD), F32), CH, reverse=True))
        r8 = _iota2((8, HW), 0)
        upd = jnp.zeros((8, HW), F32)
        for j in range(gb):
            sl = slice(j * NH, (j + 1) * NH)
            beta, g, za = gates[j]
            for h in range(NH):
                ds_ref[j, h] = ds_new[j * NH + h]
            dx_ref[j] = jnp.concatenate(dq[sl] + dk[sl] + dv[sl], axis=-1)
            dbeta_j = jnp.where(valid, jnp.concatenate(dbeta[sl], axis=-1), 0.0)
            dg_j = jnp.where(valid, jnp.concatenate(dg[sl], axis=-1), 0.0)
            draw_b = dbeta_j * beta * (1.0 - beta)
            draw_a = dg_j * (-jnp.exp(alog)) * _sig(za)
            dba_ref[j] = _nn(jnp.concatenate([draw_b, draw_a], axis=-1), sm_ref[...]).astype(dba_ref.dtype)
            upd = upd + jnp.where(r8 == 0, jnp.sum(dg_j * g, axis=0, keepdims=True), 0.0) + jnp.where(
                r8 == 1, jnp.sum(draw_a, axis=0, keepdims=True), 0.0)
        acc_ref[...] += upd

    rc = lambda c: nc - 1 - c
    dqkv, dproj3, acc = pl.pallas_call(
        body, grid=(nb // gb, nc),
        in_specs=[pl.BlockSpec((gb, CH, 3 * HW), lambda b, c: (b, rc(c), 0)), pl.BlockSpec((gb, CH, HD), lambda b, c: (b, rc(c), cba)),
                  pl.BlockSpec((HD, 2 * HW), lambda b, c: (0, 0)), pl.BlockSpec((2 * HW, HD), lambda b, c: (0, 0)),
                  pl.BlockSpec((1, HW), lambda b, c: (0, 0)), pl.BlockSpec((1, HW), lambda b, c: (0, 0)),
                  pl.BlockSpec((gb, None, NH, HD, HD), lambda b, c: (b, rc(c), 0, 0, 0)),
                  pl.BlockSpec((gb, None, NH, CH, 3 * HD), lambda b, c: (b, rc(c), 0, 0, 0)),
                  pl.BlockSpec((gb, CH, HW), lambda b, c: (b, rc(c), 0)), pl.BlockSpec(memory_space=pl.ANY)],
        out_specs=[pl.BlockSpec((gb, CH, 3 * HW), lambda b, c: (b, rc(c), 0)), pl.BlockSpec((gb, CH, HD), lambda b, c: (b, rc(c), cba)),
                   pl.BlockSpec((8, HW), lambda b, c: (0, 0))],
        out_shape=[SDS((nb, tp, 3 * HW), F32), SDS((nb, tp, dproj.shape[1]), dproj.dtype), SDS((8, HW), F32)],
        input_output_aliases={9: 1},
        scratch_shapes=[pltpu.VMEM((gb, NH, HD, HD), F32)], compiler_params=_params(2), name="gdn_bwd")(
            qkv.reshape(nb, tp, 3 * HW), proj.reshape(nb, tp, -1), e_mat, s_mat, alog, dtb, states, solved, do.reshape(nb, tp, HW),
            dproj.reshape(nb, tp, -1))
    return dqkv.reshape(n, 3 * HW), dproj3.reshape(dproj.shape), acc


def _hgrn_inputs(zq, zf, lb):
    sg = _sig(zf)
    sgn = _sig(-zf)
    pos = lb > 0.0
    lbp = jnp.where(pos, lb, 0.0)
    fpos = lbp + (1.0 - lbp) * sg
    lf = jnp.where(pos, jnp.log(jnp.where(pos, fpos, 1.0)), _logsig(zf))
    k = (1.0 - lbp) * sgn
    q = _silu(zq) * Q_SCALE
    return q, k, lf, sg, sgn, pos, lbp, fpos


def _hgrn_consts():
    i3, j3 = _iota2((SUB, SUB, HD), 0), _iota2((SUB, SUB, HD), 1)
    return i3 >= j3


def _sum_j(x):
    return jnp.sum(x.reshape(SUB, SUB, HD), axis=1)


def _sum_i(x):
    return jnp.sum(x.reshape(SUB, SUB, HD), axis=0)


def _pairs(a, b):
    return (a[:, None, :] * b[None, :, :]).reshape(SUB * SUB, HD)


def _hgrn_sub(q, k, v, bc, st, consts):
    mask3 = consts
    bl = bc[SUB - 1:SUB, :]
    p3 = jnp.where(mask3, jnp.exp(jnp.where(mask3, bc[:, None, :] - bc[None, :, :], 0.0)), 0.0).reshape(SUB * SUB, HD)
    x = _pairs(q, k) * p3
    srep = _rs(x)
    vt = jnp.broadcast_to(v[None, :, :], (SUB, SUB, HD)).reshape(SUB * SUB, HD)
    eb = jnp.exp(bc)
    qe = q * eb
    o = _hnt(qe, st) + _sum_j(_rr(srep) * _rr(vt))
    ek = jnp.exp(bl - bc)
    kd = k * ek
    ebl = jnp.exp(bl)
    st1 = st * ebl + _htn(v, kd)
    return o, st1, dict(bc=bc, p3=p3, srep=srep, vt=vt, eb=eb, qe=qe, ek=ek, kd=kd, ebl=ebl)


def _hgrn_fwd(proj, lb, lay, nb, nc):
    n = proj.shape[0]
    tp = n // nb
    cbb = lay.c_b // (3 * HW)
    gb = _seqs_per_step(nb)

    def body(z_ref, lb_ref, o_ref, so_ref, s_ref):
        @pl.when(pl.program_id(1) == 0)
        def _():
            s_ref[...] = jnp.zeros_like(s_ref)

        consts = _hgrn_consts()
        for j in range(gb):
            outs = []
            for h in range(NH):
                hs = slice(h * HD, (h + 1) * HD)
                q, k, lf = _hgrn_inputs(z_ref[j, :, hs], z_ref[j, :, HW + h * HD:HW + (h + 1) * HD], lb_ref[:, hs])[:3]
                v = z_ref[j, :, 2 * HW + h * HD:2 * HW + (h + 1) * HD]
                st = s_ref[j, h]
                so_ref[j, h] = st
                bc = _scan_rows(lf, SUB)
                oh = []
                for s in range(CH // SUB):
                    rs = slice(s * SUB, (s + 1) * SUB)
                    o, st, _ = _hgrn_sub(q[rs], k[rs], v[rs], bc[rs], st, consts)
                    oh.append(o)
                s_ref[j, h] = st
                outs.append(jnp.concatenate(oh, axis=0))
            o_ref[j] = jnp.concatenate(outs, axis=-1)

    o, st = pl.pallas_call(
        body, grid=(nb // gb, nc),
        in_specs=[pl.BlockSpec((gb, CH, 3 * HW), lambda b, c: (b, c, cbb)), pl.BlockSpec((1, HW), lambda b, c: (0, 0))],
        out_specs=[pl.BlockSpec((gb, CH, HW), lambda b, c: (b, c, 0)),
                   pl.BlockSpec((gb, None, NH, HD, HD), lambda b, c: (b, c, 0, 0, 0))],
        out_shape=[SDS((nb, tp, HW), F32), SDS((nb, nc, NH, HD, HD), F32)],
        scratch_shapes=[pltpu.VMEM((gb, NH, HD, HD), F32)], compiler_params=_params(2), name="hgrn_fwd")(
            proj.reshape(nb, tp, -1), lb)
    return o.reshape(n, HW), st


def _hgrn_bwd(proj, lb, states, do, dproj, lay, nb, nc):
    n = proj.shape[0]
    tp = n // nb
    cbb = lay.c_b // (3 * HW)
    nsub = CH // SUB
    gb = _seqs_per_step(nb)

    def body(z_ref, lb_ref, st_ref, do_ref, dp_in, dz_ref, acc_ref, ds_ref):
        ci = pl.program_id(1)

        @pl.when(ci == 0)
        def _():
            ds_ref[...] = jnp.zeros_like(ds_ref)

        @pl.when((ci == 0) & (pl.program_id(0) == 0))
        def _():
            acc_ref[...] = jnp.zeros_like(acc_ref)

        upd = jnp.zeros((8, HW), F32)
        for j in range(gb):
            upd = upd + one_seq(j, ci, z_ref, lb_ref, st_ref, do_ref, dz_ref, ds_ref)
        acc_ref[...] += upd

    def one_seq(j, ci, z_ref, lb_ref, st_ref, do_ref, dz_ref, ds_ref):
        consts = _hgrn_consts()
        row = _iota2((CH, 1), 0)
        valid = (row >= PAD) | (ci < nc - 1)
        lastrow = _iota2((SUB, 1), 0) == SUB - 1
        dzq, dzf, dzi, dlbs = [], [], [], []
        for h in range(NH):
            hs = slice(h * HD, (h + 1) * HD)
            zq, zf = z_ref[j, :, hs], z_ref[j, :, HW + h * HD:HW + (h + 1) * HD]
            q, k, lf, sg, sgn, pos, lbp, fpos = _hgrn_inputs(zq, zf, lb_ref[:, hs])
            v = z_ref[j, :, 2 * HW + h * HD:2 * HW + (h + 1) * HD]
            doh = do_ref[j, :, hs]
            sts, fw = [st_ref[j, h]], []
            bc = _scan_rows(lf, SUB)
            for s in range(nsub):
                rs = slice(s * SUB, (s + 1) * SUB)
                _, st1, c = _hgrn_sub(q[rs], k[rs], v[rs], bc[rs], sts[-1], consts)
                sts.append(st1)
                fw.append(c)
            dst = ds_ref[j, h]
            dq_l, dk_l, dv_l, dlf_l = [None] * nsub, [None] * nsub, [None] * nsub, [None] * nsub
            for s in reversed(range(nsub)):
                rs = slice(s * SUB, (s + 1) * SUB)
                c, st = fw[s], sts[s]
                qs, ks, vs, dos = q[rs], k[rs], v[rs], doh[rs]
                dqe = _hnn(dos, st)
                dkd = _hnn(vs, dst)
                dsrep = _rs(_pairs(_rr(dos), _rr(vs)))
                w = dsrep * c["p3"]
                kt = jnp.broadcast_to(ks[None, :, :], (SUB, SUB, HD)).reshape(SUB * SUB, HD)
                qt = jnp.broadcast_to(qs[:, None, :], (SUB, SUB, HD)).reshape(SUB * SUB, HD)
                dq_i = _sum_j(w * kt)
                dk_i = _sum_i(w * qt)
                dot = jnp.broadcast_to(_rr(dos)[:, None, :], (SUB, SUB, HD)).reshape(SUB * SUB, HD)
                dvv = _sum_i(_rr(c["srep"]) * dot) + _hnt(c["kd"], dst)
                t_kd = dkd * c["kd"]
                dbc = dqe * c["qe"] - t_kd + qs * dq_i - ks * dk_i
                tail = jnp.sum(t_kd, axis=0, keepdims=True) + c["ebl"] * jnp.sum(st * dst, axis=0, keepdims=True)
                dbc = dbc + jnp.where(lastrow, tail, 0.0)
                dlf_l[s] = dbc
                dq_l[s] = dq_i + dqe * c["eb"]
                dk_l[s] = dk_i + dkd * c["ek"]
                dv_l[s] = dvv
                dst = _htn(dos, c["qe"]) + dst * c["ebl"]
            ds_ref[j, h] = dst
            dq, dk, dv, dbc = (jnp.concatenate(t, axis=0) for t in (dq_l, dk_l, dv_l, dlf_l))
            dlf = _scan_rows(dbc, SUB, reverse=True)
            dlft = dlf - dk * (1.0 - k)
            dlf_dz = jnp.where(pos, (1.0 - lbp) * sg * sgn / jnp.where(pos, fpos, 1.0), sgn)
            dlf_dlb = jnp.where(pos, sgn / jnp.where(pos, fpos, 1.0), 0.0)
            dzq.append(dq * Q_SCALE * _dsilu(zq))
            dzf.append(dlft * dlf_dz)
            dzi.append(dv)
            dlbs.append(jnp.sum(jnp.where(valid, dlft * dlf_dlb, 0.0), axis=0, keepdims=True))
        dz_ref[j] = jnp.concatenate(dzq + dzf + dzi, axis=-1).astype(dz_ref.dtype)
        return jnp.where(_iota2((8, HW), 0) == 0, jnp.concatenate(dlbs, axis=-1), 0.0)

    rc = lambda c: nc - 1 - c
    dproj3, acc = pl.pallas_call(
        body, grid=(nb // gb, nc),
        in_specs=[pl.BlockSpec((gb, CH, 3 * HW), lambda b, c: (b, rc(c), cbb)), pl.BlockSpec((1, HW), lambda b, c: (0, 0)),
                  pl.BlockSpec((gb, None, NH, HD, HD), lambda b, c: (b, rc(c), 0, 0, 0)),
                  pl.BlockSpec((gb, CH, HW), lambda b, c: (b, rc(c), 0)), pl.BlockSpec(memory_space=pl.ANY)],
        out_specs=[pl.BlockSpec((gb, CH, 3 * HW), lambda b, c: (b, rc(c), cbb)), pl.BlockSpec((8, HW), lambda b, c: (0, 0))],
        out_shape=[SDS((nb, tp, dproj.shape[1]), dproj.dtype), SDS((8, HW), F32)],
        input_output_aliases={4: 0},
        scratch_shapes=[pltpu.VMEM((gb, NH, HD, HD), F32)], compiler_params=_params(2), name="hgrn_bwd")(
            proj.reshape(nb, tp, -1), lb, states, do.reshape(nb, tp, HW), dproj.reshape(nb, tp, -1))
    return dproj3.reshape(dproj.shape), acc


def _gated_norm(o, z, gamma):
    ys, ns, rs = [], [], []
    for h in range(NH):
        hs = slice(h * HD, (h + 1) * HD)
        oh = o[:, hs]
        r = lax.rsqrt(jnp.mean(oh * oh, axis=-1, keepdims=True) + EPS)
        nh = oh * r
        ys.append(nh * gamma * _silu(z[:, hs]))
        ns.append(nh)
        rs.append(r)
    return jnp.concatenate(ys, axis=-1), ns, rs


def _merge_fwd(h, oa, ob, proj, ga, gb, wa, wb, wo, lay):
    n, d = h.shape
    tm = _tile(n, 384)
    wm = lay.wm

    def body(h_ref, oa_ref, ob_ref, p_ref, ga_ref, gb_ref, wa_ref, wb_ref, wo_ref, out_ref):
        ya, _, _ = _gated_norm(oa_ref[...], p_ref[:, 0:HW], ga_ref[...])
        yb, _, _ = _gated_norm(ob_ref[...], p_ref[:, HW:2 * HW], gb_ref[...])
        ya2 = _bnn(ya, wa_ref[...])
        yb2 = _bnn(yb, wb_ref[...])
        mixed = _sig(p_ref[:, 2 * HW:2 * HW + d]) * ya2 + _sig(p_ref[:, 2 * HW + d:2 * HW + 2 * d]) * yb2
        out_ref[...] = h_ref[...] + _bnn(mixed, wo_ref[...])

    full = lambda shape: pl.BlockSpec(shape, lambda i: (0, 0))
    return pl.pallas_call(
        body, grid=(n // tm,),
        in_specs=[pl.BlockSpec((tm, d), lambda i: (i, 0)), pl.BlockSpec((tm, HW), lambda i: (i, 0)),
                  pl.BlockSpec((tm, HW), lambda i: (i, 0)), pl.BlockSpec((tm, wm), lambda i: (i, 0)),
                  full((1, HD)), full((1, HD)), full((HW, d)), full((HW, d)), full((d, d))],
        out_specs=pl.BlockSpec((tm, d), lambda i: (i, 0)), out_shape=SDS((n, d), F32),
        compiler_params=_params(1), name="merge_fwd")(h, oa, ob, proj, ga, gb, wa, wb, wo)


def _gated_norm_bwd(dy, o, z, gamma):
    dos, dzs = [], []
    dgam = jnp.zeros((1, HD), F32)
    for h in range(NH):
        hs = slice(h * HD, (h + 1) * HD)
        oh, zh, dyh = o[:, hs], z[:, hs], dy[:, hs]
        r = lax.rsqrt(jnp.mean(oh * oh, axis=-1, keepdims=True) + EPS)
        nh = oh * r
        dzs.append(dyh * nh * gamma * _dsilu(zh))
        dng = dyh * _silu(zh)
        dgam = dgam + jnp.sum(dng * nh, axis=0, keepdims=True)
        dn = dng * gamma
        dos.append(r * (dn - nh * jnp.mean(dn * nh, axis=-1, keepdims=True)))
    return jnp.concatenate(dos, axis=-1), jnp.concatenate(dzs, axis=-1), dgam


def _merge_bwd(dhn, oa, ob, proj, ga, gb, wa, wb, wo, lay, tp):
    n, d = dhn.shape
    tm = _tile(n, 256)
    wm = lay.wm

    def body(dh_ref, oa_ref, ob_ref, p_ref, ga_ref, gb_ref, wa_ref, wb_ref, wo_ref,
             dp_ref, doa_ref, dob_ref, dwa_ref, dwb_ref, dwo_ref, dga_ref, dgb_ref):
        i = pl.program_id(0)

        @pl.when(i == 0)
        def _():
            for r in (dwa_ref, dwb_ref, dwo_ref, dga_ref, dgb_ref):
                r[...] = jnp.zeros_like(r)

        dh = jnp.where(_row_valid(tm, tp, i * tm), dh_ref[...], 0.0)
        oa, ob = oa_ref[...], ob_ref[...]
        za, zb = p_ref[:, 0:HW], p_ref[:, HW:2 * HW]
        gta, gtb = p_ref[:, 2 * HW:2 * HW + d], p_ref[:, 2 * HW + d:2 * HW + 2 * d]
        ya, _, _ = _gated_norm(oa, za, ga_ref[...])
        yb, _, _ = _gated_norm(ob, zb, gb_ref[...])
        ya2 = _bnn(ya, wa_ref[...])
        yb2 = _bnn(yb, wb_ref[...])
        sa, sb = _sig(gta), _sig(gtb)
        mixed = sa * ya2 + sb * yb2
        dmixed = _bnt(dh, wo_ref[...])
        dwo_ref[...] += _btn(mixed, dh)
        dya2 = dmixed * sa
        dyb2 = dmixed * sb
        dwa_ref[...] += _btn(ya, dya2)
        dwb_ref[...] += _btn(yb, dyb2)
        doa, dza, dga = _gated_norm_bwd(_bnt(dya2, wa_ref[...]), oa, za, ga_ref[...])
        dob, dzb, dgb = _gated_norm_bwd(_bnt(dyb2, wb_ref[...]), ob, zb, gb_ref[...])
        dga_ref[...] += dga
        dgb_ref[...] += dgb
        doa_ref[...] = doa
        dob_ref[...] = dob
        dt = dp_ref.dtype
        dp_ref[:, 0:HW] = dza.astype(dt)
        dp_ref[:, HW:2 * HW] = dzb.astype(dt)
        dp_ref[:, 2 * HW:2 * HW + d] = (dmixed * ya2 * sa * (1.0 - sa)).astype(dt)
        dp_ref[:, 2 * HW + d:2 * HW + 2 * d] = (dmixed * yb2 * sb * (1.0 - sb)).astype(dt)

    full = lambda shape: pl.BlockSpec(shape, lambda i: (0, 0))
    rows = lambda w: pl.BlockSpec((tm, w), lambda i: (i, 0))
    return pl.pallas_call(
        body, grid=(n // tm,),
        in_specs=[rows(d), rows(HW), rows(HW), rows(wm), full((1, HD)), full((1, HD)), full((HW, d)), full((HW, d)), full((d, d))],
        out_specs=[rows(wm), rows(HW), rows(HW), full((HW, d)), full((HW, d)), full((d, d)), full((1, HD)), full((1, HD))],
        out_shape=[SDS((n, lay.pw), BF16), SDS((n, HW), F32), SDS((n, HW), F32), SDS((HW, d), F32), SDS((HW, d), F32),
                   SDS((d, d), F32), SDS((1, HD), F32), SDS((1, HD), F32)],
        compiler_params=_params(1), name="merge_bwd")(dhn, oa, ob, proj, ga, gb, wa, wb, wo)


def _loss_head(h, target, fw, nb, nc):
    n, d = h.shape

    def body(h_ref, t_ref, fw_ref, lp_ref, dh_ref, dfw_ref):
        b, c = pl.program_id(0), pl.program_id(1)

        @pl.when((b == 0) & (c == 0))
        def _():
            dfw_ref[...] = jnp.zeros_like(dfw_ref)

        @pl.when(c == 0)
        def _():
            dh_ref[...] = jnp.zeros_like(dh_ref)
            lp_ref[...] = jnp.zeros_like(lp_ref)

        @pl.when(c > 0)
        def _():
            x = h_ref[...]
            r = lax.rsqrt(jnp.mean(x * x, axis=-1, keepdims=True) + EPS)
            xh = x * r
            err = xh * fw_ref[...] - t_ref[...]
            lp_ref[...] = jnp.zeros_like(lp_ref) + 0.5 * jnp.sum(_rs(err * err), axis=0, keepdims=True) / d
            dy = err / d
            dfw_ref[...] += jnp.sum(dy * xh, axis=0, keepdims=True)
            dxh = dy * fw_ref[...]
            dh_ref[...] = r * (dxh - xh * jnp.mean(dxh * xh, axis=-1, keepdims=True))

    return pl.pallas_call(
        body, grid=(nb, nc),
        in_specs=[pl.BlockSpec((CH, d), lambda b, c: (b * nc + c, 0)),
                  pl.BlockSpec((CH, d), lambda b, c: (b * (nc - 1) + jnp.maximum(c - 1, 0), 0)),
                  pl.BlockSpec((1, d), lambda b, c: (0, 0))],
        out_specs=[pl.BlockSpec((8, HD), lambda b, c: (b * nc + c, 0)), pl.BlockSpec((CH, d), lambda b, c: (b * nc + c, 0)),
                   pl.BlockSpec((1, d), lambda b, c: (0, 0))],
        out_shape=[SDS((nb * nc * 8, HD), F32), SDS((n, d), F32), SDS((1, d), F32)],
        compiler_params=_params(2), name="loss_head")(h, target, fw)


def _lb_fwd(lb):
    def body(x_ref, o_ref):
        x = x_ref[...]
        mx = jnp.max(x, axis=0, keepdims=True)
        e = jnp.exp(x - mx)
        sm = e / jnp.sum(e, axis=0, keepdims=True)
        run = jnp.zeros((1, HW), F32)
        for l in range(DEPTH):
            run = run + sm[l:l + 1, :]
            o_ref[l:l + 1, :] = run - sm[0:1, :]

    return pl.pallas_call(body, out_shape=SDS(lb.shape, F32), name="lb_fwd")(lb)


def _lb_bwd(lb, dlb_all):
    def body(x_ref, d_ref, o_ref):
        x = x_ref[...]
        dl = d_ref[...]
        mx = jnp.max(x, axis=0, keepdims=True)
        e = jnp.exp(x - mx)
        sm = e / jnp.sum(e, axis=0, keepdims=True)
        tot = jnp.sum(dl, axis=0, keepdims=True)
        dsm = []
        run = tot
        for l in range(DEPTH):
            dsm.append(run - (tot if l == 0 else 0.0))
            run = run - dl[l:l + 1, :]
        inner = sum(sm[l:l + 1, :] * dsm[l] for l in range(DEPTH))
        for l in range(DEPTH):
            o_ref[l:l + 1, :] = sm[l:l + 1, :] * (dsm[l] - inner)

    return pl.pallas_call(body, out_shape=SDS(lb.shape, F32), name="lb_bwd")(lb, dlb_all)


def _adamw(g, w, m, v):
    r, c = g.shape
    tr = _tile(r, 264)
    c1 = 1.0 / (1.0 - ADAM_B1 ** ADAM_STEP)
    c2 = 1.0 / (1.0 - ADAM_B2 ** ADAM_STEP)

    def body(g_ref, w_ref, m_ref, v_ref, d_ref, mo_ref, vo_ref):
        gg = g_ref[...]
        mn = ADAM_B1 * m_ref[...] + (1.0 - ADAM_B1) * gg
        vn = ADAM_B2 * v_ref[...] + (1.0 - ADAM_B2) * gg * gg
        d_ref[...] = -ADAM_LR * ((mn * c1) / (jnp.sqrt(vn * c2) + ADAM_EPS) + ADAM_WD * w_ref[...])
        mo_ref[...] = mn
        vo_ref[...] = vn

    spec = pl.BlockSpec((tr, c), lambda i: (i, 0))
    return pl.pallas_call(body, grid=(r // tr,), in_specs=[spec] * 4, out_specs=[spec] * 3, out_shape=[SDS(g.shape, F32)] * 3,
                          compiler_params=_params(1), name="adamw")(g, w, m, v)


def _tile16(n, target):
    return _tile(n // 2, target // 2) * 2 if n % 16 == 0 else _tile(n, target)


def _add_cores(g, got, core):
    k, r, c = got.shape
    tr = _tile16(r, 264)

    def body(c_ref, a_ref, b_ref, o_ref):
        o_ref[...] = (a_ref[...] + b_ref[...].astype(F32)).astype(o_ref.dtype)

    spec = pl.BlockSpec((None, tr, c), lambda s, i, cr: (s, i, 0))
    return pl.pallas_call(
        body, grid_spec=pltpu.PrefetchScalarGridSpec(
            num_scalar_prefetch=1, grid=(k, r // tr),
            in_specs=[pl.BlockSpec((None, None, tr, c), lambda s, i, cr: (cr[0], s, i, 0)), spec], out_specs=spec),
        out_shape=SDS(got.shape, got.dtype), compiler_params=_params(2), name="add_cores")(core, g, got)


def _sum_chips(parts, own, place):
    k, r, c = parts.shape
    tr = _tile16(r, 264)

    def body(p_ref, *refs):
        part_refs, own_ref, o_ref = refs[:k], refs[k], refs[k + 1]
        mine = own_ref[...].astype(F32)
        acc = None
        for s in range(k):
            term = jnp.where(p_ref[0] == s, mine, part_refs[s][...].astype(F32))
            acc = term if acc is None else acc + term
        o_ref[...] = acc

    slots = jnp.stack([jnp.where(place[0] == s, (s + 1) % k, s) for s in range(k)]).astype(jnp.int32)
    other = lambda s: pl.BlockSpec((None, tr, c), lambda i, p: (p[2 + s], i, 0))
    return pl.pallas_call(
        body, grid_spec=pltpu.PrefetchScalarGridSpec(
            num_scalar_prefetch=1, grid=(r // tr,),
            in_specs=[other(s) for s in range(k)] + [pl.BlockSpec((None, tr, c), lambda i, p: (p[0], i, 0))],
            out_specs=pl.BlockSpec((None, tr, c), lambda i, p: (p[1], i, 0))),
        out_shape=SDS((2, r, c), F32), compiler_params=_params(1), name="sum_chips")(
            jnp.concatenate([place, slots]), *([parts] * k), own)


def _meta_grad(dh, nb, nc):
    d = dh.shape[1]

    def body(x_ref, o_ref):
        @pl.when(pl.program_id(0) == 0)
        def _():
            o_ref[...] = jnp.zeros_like(o_ref)

        o_ref[...] += x_ref[PAD:CH, :]

    return pl.pallas_call(body, grid=(nb,), in_specs=[pl.BlockSpec((CH, d), lambda b: (b * nc, 0))],
                          out_specs=pl.BlockSpec((N_META, d), lambda b: (0, 0)), out_shape=SDS((N_META, d), F32),
                          compiler_params=_params(1), name="meta_grad")(dh)


ANY = pl.BlockSpec(memory_space=pl.ANY)


def _place():
    x, y, c = lax.axis_index("x"), lax.axis_index("y"), lax.axis_index("c")
    chips = [(1 - x, y), (x, 1 - y), (1 - x, 1 - y)]
    return x, y, c, chips


def _remote(src, dst, send_sems, recv_sems, k, to):
    return pltpu.make_async_remote_copy(src_ref=src, dst_ref=dst, send_sem=send_sems.at[k], recv_sem=recv_sems.at[k],
                                        device_id=to, device_id_type=MESH)


def _gather_weights(pbs, ps):
    nt = len(pbs)

    def body(*refs):
        pb_refs, ps_ref, gb_refs, gs_ref = refs[:nt], refs[nt], refs[nt + 1:2 * nt + 1], refs[2 * nt + 1]
        send_sems, recv_sems, local_sems = refs[2 * nt + 2:]
        x, y, c, chips = _place()
        s = 2 * x + y
        sib = (x, y, 1 - c)
        l1 = pltpu.make_async_copy(ps_ref, gs_ref.at[s], local_sems.at[0])
        l1.start()
        sends = []
        for k, (px, py) in enumerate(chips):
            for t in range(nt):
                sends.append(_remote(pb_refs[t].at[c], gb_refs[t].at[s, c], send_sems, recv_sems, 6 * t + k, (px, py, c)))
            sends.append(_remote(ps_ref, gs_ref.at[s], send_sems, recv_sems, 6 * nt + k, (px, py, c)))
        for cp in sends:
            cp.start()
        for k, (px, py) in enumerate(chips):
            sk = 2 * px + py
            for t in range(nt):
                _remote(pb_refs[t].at[c], gb_refs[t].at[sk, c], send_sems, recv_sems, 6 * t + k, sib).wait_recv()
                fwd = _remote(gb_refs[t].at[sk, c], gb_refs[t].at[sk, c], send_sems, recv_sems, 6 * t + 3 + k, sib)
                fwd.start()
                sends.append(fwd)
        for k, (px, py) in enumerate(chips):
            sk = 2 * px + py
            for t in range(nt):
                _remote(pb_refs[t].at[c], gb_refs[t].at[sk, 1 - c], send_sems, recv_sems, 6 * t + 3 + k, sib).wait_recv()
            _remote(ps_ref, gs_ref.at[sk], send_sems, recv_sems, 6 * nt + k, sib).wait_recv()
        for cp in sends:
            cp.wait_send()
        l1.wait()

    nsem = 6 * nt + 3
    out = pl.pallas_call(
        body, in_specs=[ANY] * (nt + 1), out_specs=[ANY] * (nt + 1),
        out_shape=[SDS((4,) + pb.shape, pb.dtype) for pb in pbs] + [SDS((4,) + ps.shape, ps.dtype)],
        scratch_shapes=[pltpu.SemaphoreType.DMA((nsem,)), pltpu.SemaphoreType.DMA((nsem,)), pltpu.SemaphoreType.DMA((1,))],
        name="gather_weights")(*pbs, ps)
    return out[:nt], out[nt]


def _contain(wpad, shift):
    r, cw = wpad.shape
    tr = _tile16(r, 256)

    def body(n_ref, x_ref, o_ref):
        o_ref[...] = pltpu.roll(x_ref[...], n_ref[0], axis=1).astype(o_ref.dtype)

    spec = pl.BlockSpec((tr, cw), lambda i, n: (i, 0))
    return pl.pallas_call(
        body, grid_spec=pltpu.PrefetchScalarGridSpec(num_scalar_prefetch=1, grid=(r // tr,), in_specs=[spec], out_specs=spec),
        out_shape=SDS((r, cw), BF16), compiler_params=_params(1), name="contain")(shift, wpad)


def _place_own(gb, pb, chip):
    _, _, r, c = gb.shape
    tr = _tile16(r, 1100)

    def body(s_ref, p_ref, g_in, o_ref):
        o_ref[...] = p_ref[...]

    return pl.pallas_call(
        body, grid_spec=pltpu.PrefetchScalarGridSpec(
            num_scalar_prefetch=1, grid=(2, r // tr),
            in_specs=[pl.BlockSpec((None, tr, c), lambda h, i, s: (h, i, 0)), ANY],
            out_specs=pl.BlockSpec((None, None, tr, c), lambda h, i, s: (s[0], h, i, 0))),
        out_shape=SDS(gb.shape, gb.dtype), input_output_aliases={2: 0}, compiler_params=_params(2),
        name="place_own")(chip, pb, gb)


def _sem_scratch(n_remote, n_local):
    return [pltpu.SemaphoreType.DMA((n_remote,)), pltpu.SemaphoreType.DMA((n_remote,)), pltpu.SemaphoreType.DMA((n_local,))]


def _swap_halves(sends):
    nt = len(sends)

    def body(*refs):
        s_refs, got_refs = refs[:nt], refs[nt:2 * nt]
        send_sems, recv_sems = refs[2 * nt:]
        x, y, c, _ = _place()
        sib = (x, y, 1 - c)
        remote = [_remote(s_refs[t].at[1 - c, s], got_refs[t].at[s], send_sems, recv_sems, 4 * t + s, sib)
                  for t in range(nt) for s in range(4)]
        for cp in remote:
            cp.start()
        for cp in remote:
            cp.wait()

    return pl.pallas_call(
        body, in_specs=[ANY] * nt, out_specs=[ANY] * nt, out_shape=[SDS(g.shape[1:], g.dtype) for g in sends],
        scratch_shapes=[pltpu.SemaphoreType.DMA((4 * nt,)), pltpu.SemaphoreType.DMA((4 * nt,))], name="swap_halves")(*sends)


def _scatter_chip_sums(parts):
    nt = len(parts)

    def body(*refs):
        a_refs, r_refs = refs[:nt], refs[nt:2 * nt]
        send_sems, recv_sems = refs[2 * nt:]
        x, y, c, chips = _place()
        s = 2 * x + y
        sends = [_remote(a_refs[t].at[2 * px + py], r_refs[t].at[s], send_sems, recv_sems, 3 * t + k, (px, py, c))
                 for t in range(nt) for k, (px, py) in enumerate(chips)]
        for cp in sends:
            cp.start()
        for t in range(nt):
            for k, (px, py) in enumerate(chips):
                _remote(a_refs[t].at[s], r_refs[t].at[2 * px + py], send_sems, recv_sems, 3 * t + k, (px, py, c)).wait_recv()
        for cp in sends:
            cp.wait_send()

    return pl.pallas_call(
        body, in_specs=[ANY] * nt, out_specs=[ANY] * nt, out_shape=[SDS(a.shape, a.dtype) for a in parts],
        scratch_shapes=[pltpu.SemaphoreType.DMA((3 * nt,)), pltpu.SemaphoreType.DMA((3 * nt,))],
        name="scatter_chip_sums")(*parts)


def _join_halves(fs):
    nt = len(fs)

    def body(*refs):
        f_refs = refs[nt:2 * nt]
        send_sems, recv_sems = refs[2 * nt:]
        x, y, c, _ = _place()
        sib = (x, y, 1 - c)
        sends = [_remote(f_refs[t].at[c], f_refs[t].at[c], send_sems, recv_sems, t, sib) for t in range(nt)]
        for cp in sends:
            cp.start()
        for t in range(nt):
            _remote(f_refs[t].at[c], f_refs[t].at[1 - c], send_sems, recv_sems, t, sib).wait_recv()
        for cp in sends:
            cp.wait_send()

    return pl.pallas_call(
        body, in_specs=[ANY] * nt, out_specs=[ANY] * nt, out_shape=[SDS(f.shape, f.dtype) for f in fs],
        input_output_aliases={t: t for t in range(nt)},
        scratch_shapes=[pltpu.SemaphoreType.DMA((nt,)), pltpu.SemaphoreType.DMA((nt,))], name="join_halves")(*fs)


def _uncontain(cont, n_head, width):
    r, cw = cont.shape
    tr = _tile(r, 256)

    def body(n_ref, x_ref, o_ref):
        o_ref[...] = pltpu.roll(x_ref[...], n_ref[0], axis=1)[:, :width]

    return pl.pallas_call(
        body, grid_spec=pltpu.PrefetchScalarGridSpec(
            num_scalar_prefetch=1, grid=(r // tr,), in_specs=[pl.BlockSpec((tr, cw), lambda i, n: (i, 0))],
            out_specs=pl.BlockSpec((tr, width), lambda i, n: (i, 0))),
        out_shape=SDS((r, width), F32), compiler_params=_params(1), name="uncontain")(n_head, cont)


WEIGHTS = ("meta_tokens", "norm_w", "w_in", "conv_w", "a_log", "dt_bias", "gnorm_a", "gnorm_b", "hgrn_lower_bounds",
           "w_branch_a", "w_branch_b", "w_out", "final_norm_w")
SHARD_AXIS = {"meta_tokens": 1, "w_in": 2, "conv_w": 2, "w_branch_a": 2, "w_branch_b": 2, "w_out": 1}
FLAT_C = 1024


def _flat(parts, rows, cols=FLAT_C):
    v = jnp.concatenate([p.reshape(-1) for p in parts])
    return jnp.pad(v, (0, rows * cols - v.shape[0])).reshape(rows, cols)


def _local_step(x, target, w, lay):
    nb, seq, d = x.shape
    tp = CH + seq
    nc = tp // CH
    n = nb * tp
    e_mat, s_mat = _gate_consts()
    lb_all = _lb_fwd(w["hgrn_lower_bounds"])
    h = jnp.concatenate([jnp.zeros((nb, PAD, d), F32), jnp.broadcast_to(w["meta_tokens"][None], (nb, N_META, d)), x],
                        axis=1).reshape(n, d)
    rep = lambda a: jnp.repeat(a, HD)[None, :]
    saved = []
    for l in range(DEPTH):
        nw = w["norm_w"][l][None, :]
        proj, xn = _norm_proj_fwd(h, nw, w["w_in"][l])
        qkv = _gdn_prep_fwd(proj, w["conv_w"][l], lay, nb, tp)
        alog, dtb = rep(w["a_log"][l]), rep(w["dt_bias"][l])
        oa, sa, sva = _gdn_fwd(qkv, proj, e_mat, alog, dtb, lay, nb, nc)
        lbl = lb_all[l][None, :]
        ob, sb = _hgrn_fwd(proj, lbl, lay, nb, nc)
        ga, gb = w["gnorm_a"][l][None, :], w["gnorm_b"][l][None, :]
        hn = _merge_fwd(h, oa, ob, proj, ga, gb, w["w_branch_a"][l], w["w_branch_b"][l], w["w_out"][l], lay)
        saved.append((h, nw, proj, qkv, alog, dtb, oa, sa, lbl, ob, sb, ga, gb, xn, sva))
        h = hn
    lp, dh, dfw = _loss_head(h, target.reshape(nb * seq, d), w["final_norm_w"][None, :], nb, nc)
    loss = jnp.sum(lp[::8, 0])
    g = {n_: [None] * DEPTH for n_ in WEIGHTS}
    dlb_all = [None] * DEPTH
    for l in reversed(range(DEPTH)):
        h, nw, proj, qkv, alog, dtb, oa, sa, lbl, ob, sb, ga, gb, xn, sva = saved[l]
        dproj, doa, dob, dwa, dwb, dwo, dga, dgb = _merge_bwd(dh, oa, ob, proj, ga, gb, w["w_branch_a"][l],
                                                             w["w_branch_b"][l], w["w_out"][l], lay, tp)
        dproj, acc_b = _hgrn_bwd(proj, lbl, sb, dob, dproj, lay, nb, nc)
        dqkv, dproj, acc_a = _gdn_bwd(qkv, proj, e_mat, s_mat, alog, dtb, sa, sva, doa, dproj, lay, nb, nc)
        dproj, dconv = _gdn_prep_bwd(proj, w["conv_w"][l], dqkv, dproj, lay, nb, tp)
        dh, dnw = _proj_bwd_dx(dproj, w["w_in"][l], h, nw, dh, tp)
        g["w_in"][l] = _proj_bwd_dw(dproj, xn, tp)
        g["norm_w"][l] = dnw[0]
        g["conv_w"][l] = dconv
        g["a_log"][l] = acc_a[0, ::HD]
        g["dt_bias"][l] = acc_a[1, ::HD]
        g["gnorm_a"][l], g["gnorm_b"][l] = dga[0], dgb[0]
        g["w_branch_a"][l], g["w_branch_b"][l], g["w_out"][l] = dwa, dwb, dwo
        dlb_all[l] = acc_b[0]
    grads = {n_: jnp.stack(v) for n_, v in g.items() if v[0] is not None}
    grads["hgrn_lower_bounds"] = _lb_bwd(w["hgrn_lower_bounds"], jnp.stack(dlb_all))
    grads["final_norm_w"] = dfw[0]
    grads["meta_tokens"] = _meta_grad(dh, nb, nc)
    grad_x = dh.reshape(nb, tp, d)[:, CH:, :]
    return loss, grad_x, grads


def kernel(x, meta_tokens, norm_w, w_in, conv_w, a_log, dt_bias, gnorm_a, gnorm_b, hgrn_lower_bounds, w_branch_a, w_branch_b, w_out, final_norm_w, loss_target, m_meta_tokens, m_norm_w, m_w_in, m_conv_w, m_a_log, m_dt_bias, m_gnorm_a, m_gnorm_b, m_hgrn_lower_bounds, m_w_branch_a, m_w_branch_b, m_w_out, m_final_norm_w, v_meta_tokens, v_norm_w, v_w_in, v_conv_w, v_a_log, v_dt_bias, v_gnorm_a, v_gnorm_b, v_hgrn_lower_bounds, v_w_branch_a, v_w_branch_b, v_w_out, v_final_norm_w):
    wl = dict(meta_tokens=meta_tokens, norm_w=norm_w, w_in=w_in, conv_w=conv_w, a_log=a_log, dt_bias=dt_bias, gnorm_a=gnorm_a,
              gnorm_b=gnorm_b, hgrn_lower_bounds=hgrn_lower_bounds, w_branch_a=w_branch_a, w_branch_b=w_branch_b, w_out=w_out,
              final_norm_w=final_norm_w)
    ml = dict(zip(WEIGHTS, (m_meta_tokens, m_norm_w, m_w_in, m_conv_w, m_a_log, m_dt_bias, m_gnorm_a, m_gnorm_b,
                            m_hgrn_lower_bounds, m_w_branch_a, m_w_branch_b, m_w_out, m_final_norm_w)))
    vl = dict(zip(WEIGHTS, (v_meta_tokens, v_norm_w, v_w_in, v_conv_w, v_a_log, v_dt_bias, v_gnorm_a, v_gnorm_b,
                            v_hgrn_lower_bounds, v_w_branch_a, v_w_branch_b, v_w_out, v_final_norm_w)))
    d = x.shape[2]
    lay = _Layout(d)
    nchip = 4

    big = ("w_in", "w_branch_a", "w_branch_b", "w_out")
    small = ("conv_w", "meta_tokens")
    table, heads, cw = lay.pieces(nchip)
    sw = wl["w_in"].shape[2]
    chip_id = (2 * lax.axis_index("x") + lax.axis_index("y")).astype(jnp.int32)
    n_head = sum(jnp.where(chip_id == s, heads[s], 0) for s in range(nchip)).astype(jnp.int32)
    w_pad = jnp.pad(wl["w_in"], ((0, 0), (0, 0), (0, cw - sw))).reshape(DEPTH * d, cw)
    shift = jnp.where(n_head == 0, 0, cw - n_head).astype(jnp.int32).reshape(1)
    pbs = [_contain(w_pad, shift).reshape(DEPTH, d, cw)] + [wl[n].astype(BF16) for n in big[1:]]
    nsmall = sum(int(np.prod(wl[n].shape)) for n in small)
    rs = -(-nsmall // (HD * 8)) * 8
    ps = jnp.pad(jnp.concatenate([wl[n].reshape(-1) for n in small]), (0, rs * HD - nsmall)).reshape(rs, HD)
    gbig, gsmall = _gather_weights(pbs, ps)
    gbig = [_place_own(g, p, chip_id.reshape(1)) for g, p in zip(gbig, pbs)]
    gsmall = gsmall.reshape(nchip, -1)

    wf = dict(wl)
    wf["w_in"] = lay.from_containers([gbig[0][s] for s in range(nchip)])
    for i, n in enumerate(big[1:], start=1):
        wf[n] = jnp.concatenate([gbig[i][s] for s in range(nchip)], axis=SHARD_AXIS[n])
    o = 0
    for n in small:
        sz = int(np.prod(wl[n].shape))
        a = gsmall[:, o:o + sz].reshape((nchip,) + wl[n].shape)
        wf[n] = jnp.concatenate([a[s] for s in range(nchip)], axis=SHARD_AXIS[n])
        o += sz

    loss_part, grad_x, gfull = _local_step(x, loss_target, wf, lay)
    loss = lax.psum(loss_part, ("x", "y", "c"))

    sw = wl["w_in"].shape[2]
    conts, heads = lay.containers(gfull["w_in"], nchip)
    dd = wl["w_branch_a"].shape[2]
    rows_o = wl["w_out"].shape[1]
    by_dest = lambda g, n: [lax.slice_in_dim(g, s * wl[n].shape[SHARD_AXIS[n]], (s + 1) * wl[n].shape[SHARD_AXIS[n]],
                                            axis=SHARD_AXIS[n]) if n in SHARD_AXIS else g for s in range(nchip)]
    small_names = tuple(n for n in WEIGHTS if n not in big)
    nsm = sum(int(np.prod(wl[n].shape)) for n in small_names)
    rsm = -(-nsm // (2 * HD * 8)) * 8
    pack_small = lambda parts: _flat(parts, 2 * rsm, HD).reshape(2, rsm, HD)
    small_by_dest = [by_dest(gfull[n], n) for n in small_names]
    gs = [jnp.stack(conts, axis=1),
          jnp.stack(by_dest(gfull["w_branch_a"], "w_branch_a"), axis=1),
          jnp.stack(by_dest(gfull["w_branch_b"], "w_branch_b"), axis=1),
          gfull["w_out"].reshape(DEPTH, nchip, rows_o, d),
          jnp.stack([pack_small([p[s] for p in small_by_dest]) for s in range(nchip)], axis=1)]
    gs = [g.reshape((2, nchip, -1, g.shape[-1])) for g in gs]
    my_chip = (2 * lax.axis_index("x") + lax.axis_index("y")).astype(jnp.int32)
    my_core = lax.axis_index("c").astype(jnp.int32)
    got = _swap_halves([g.astype(BF16) for g in gs[:4]] + gs[4:])
    chip_sums = [_add_cores(g, b, my_core.reshape(1)) for g, b in zip(gs, got)]
    by_chip = _scatter_chip_sums(chip_sums)
    place = jnp.stack([my_chip, my_core])
    full = _join_halves([_sum_chips(p, a, place) for p, a in zip(by_chip, chip_sums)])
    n_head = sum(jnp.where(my_chip == s, heads[s], 0) for s in range(nchip)).astype(jnp.int32).reshape(1)
    g_w_in = _uncontain(full[0].reshape(DEPTH * d, -1), n_head, sw)
    g2 = {"w_in": g_w_in, "w_branch_a": full[1].reshape(-1, dd), "w_branch_b": full[2].reshape(-1, dd),
          "w_out": full[3].reshape(-1, d), "small": full[4].reshape(2 * rsm, HD)}

    def two_d(src, n):
        if n == "small":
            return _flat([src[k] for k in small_names], 2 * rsm, HD)
        return src[n].reshape(g2[n].shape)

    outs = {}
    for n in big + ("small",):
        delta, mnew, vnew = _adamw(g2[n], two_d(wl, n), two_d(ml, n), two_d(vl, n))
        outs[n] = (g2[n], delta, mnew, vnew)
    res = [{}, {}, {}, {}]
    for i in range(4):
        for n in big:
            res[i][n] = outs[n][i].reshape(wl[n].shape)
        v, o = outs["small"][i].reshape(-1), 0
        for n in small_names:
            sz = int(np.prod(wl[n].shape))
            res[i][n] = v[o:o + sz].reshape(wl[n].shape)
            o += sz
    return (loss, grad_x, *[res[0][n] for n in WEIGHTS], *[res[1][n] for n in WEIGHTS], *[res[2][n] for n in WEIGHTS],
            *[res[3][n] for n in WEIGHTS])
```

```python
import functools

import numpy as np
import jax
import jax.numpy as jnp
from jax import lax
from jax.experimental import pallas as pl
from jax.experimental.pallas import tpu as pltpu

F32 = jnp.float32
BF16 = jnp.bfloat16
HI = lax.Precision.HIGHEST
SDS = jax.ShapeDtypeStruct

NH = 4
HD = 128
HW = NH * HD
N_META = 16
CH = 64
SUB = 16
PAD = CH - N_META
EPS = 1e-6
Q_SCALE = HD ** -0.5
DEPTH = 2
CONV_K = 4
VMEM_LIMIT = 56 * 1024 * 1024
ADAM_LR, ADAM_B1, ADAM_B2, ADAM_EPS, ADAM_WD, ADAM_STEP = 0.001, 0.9, 0.999, 1e-08, 0.01, 10
MESH = pl.DeviceIdType.MESH


def _nn(a, b):
    return jnp.dot(a, b, precision=HI, preferred_element_type=F32)


def _nt(a, b):
    return lax.dot_general(a, b, (((1,), (1,)), ((), ())), precision=HI, preferred_element_type=F32)


def _tn(a, b):
    return _nn(a.T, b)


def _scan_rows(x, group, reverse=False):
    n = x.shape[0]
    pos = lax.bitwise_and(_iota2(x.shape, 0), group - 1)
    s = 1
    while s < group:
        if reverse:
            x = x + jnp.where(pos < group - s, pltpu.roll(x, n - s, axis=0), 0.0)
        else:
            x = x + jnp.where(pos >= s, pltpu.roll(x, s, axis=0), 0.0)
        s *= 2
    return x


def _bnn(a, b):
    return jnp.dot(a.astype(BF16), b.astype(BF16), preferred_element_type=F32)


def _bnt(a, b):
    return lax.dot_general(a.astype(BF16), b.astype(BF16), (((1,), (1,)), ((), ())), preferred_element_type=F32)


def _btn(a, b):
    return lax.dot_general(a.astype(BF16), b.astype(BF16), (((0,), (0,)), ((), ())), preferred_element_type=F32)


def _hi_lo(x):
    hi = x.astype(jnp.bfloat16)
    return hi, (x - hi.astype(F32)).astype(jnp.bfloat16)


def _dot3(dims):
    def f(a, b):
        ah, al = _hi_lo(a)
        bh, bl = _hi_lo(b)
        d = lambda p, q: lax.dot_general(p, q, (dims, ((), ())), preferred_element_type=F32)
        return d(ah, bh) + (d(ah, bl) + d(al, bh))
    return f


_rnn, _rnt, _rtn = _dot3(((1,), (0,))), _dot3(((1,), (1,))), _dot3(((0,), (0,)))
_enn, _ent, _etn = _bnn, _bnt, _btn
_hnn, _hnt, _htn = _bnn, _bnt, _btn


def _rr(x):
    return x


def _sig(x):
    return jax.nn.sigmoid(x)


def _silu(x):
    return x * _sig(x)


def _dsilu(x):
    s = _sig(x)
    return s * (1.0 + x * (1.0 - s))


def _softplus(x):
    return jnp.maximum(x, 0.0) + jnp.log(1.0 + jnp.exp(-jnp.abs(x)))


def _logsig(x):
    return jnp.minimum(x, 0.0) - jnp.log(1.0 + jnp.exp(-jnp.abs(x)))


def _rs(x):
    return jnp.sum(x, axis=-1, keepdims=True)


def _params(n_axes):
    return pltpu.CompilerParams(dimension_semantics=("arbitrary",) * n_axes, vmem_limit_bytes=VMEM_LIMIT)


def _tile(n, target, mult=8):
    best = mult
    for t in range(mult, target + 1, mult):
        if n % t == 0:
            best = t
    assert n % best == 0, (n, mult)
    return best


def _ctile(pw, most=7):
    return HD * max(k for k in range(1, most + 1) if (pw // HD) % k == 0)


def _iota2(shape, axis):
    return lax.broadcasted_iota(jnp.int32, shape, axis)


class _Layout:
    def __init__(self, d):
        self.d = d
        self.wm = 2 * HW + 2 * d
        self.c_qkv = self.wm
        self.c_b = self.wm + 3 * HW
        self.c_ba = self.wm + 6 * HW
        self.pw = self.c_ba + HD
        assert self.c_b % (3 * HW) == 0
        o = 0
        segs = {}
        for name, w in (("a_q", HW), ("a_k", HW), ("a_v", HW), ("ba", 2 * NH), ("a_z", HW), ("b_q", HW), ("b_f", HW),
                        ("b_i", HW), ("b_g", HW), ("gate_a", d), ("gate_b", d)):
            segs[name] = (o, o + w)
            o += w
        self.segs = segs
        self.width = o
        self.order = ("a_z", "b_g", "gate_a", "gate_b", "a_q", "a_k", "a_v", "b_q", "b_f", "b_i", "ba")

    def to_kernel(self, w):
        parts = [w[..., self.segs[n][0]:self.segs[n][1]] for n in self.order]
        parts.append(jnp.zeros(w.shape[:-1] + (HD - 2 * NH,), w.dtype))
        return jnp.concatenate(parts, axis=-1)

    def containers(self, g, nchip):
        table, heads, cw = self.pieces(nchip)
        out = []
        for s in range(nchip):
            parts, at = [], 0
            for kcol, w, ccol in sorted(table[s], key=lambda p: p[2]):
                if ccol > at:
                    parts.append(jnp.zeros(g.shape[:-1] + (ccol - at,), g.dtype))
                parts.append(g[..., kcol:kcol + w])
                at = ccol + w
            if at < cw:
                parts.append(jnp.zeros(g.shape[:-1] + (cw - at,), g.dtype))
            out.append(jnp.concatenate(parts, axis=-1))
        return out, heads

    def pieces(self, nchip):
        off, where = 0, {}
        for n in self.order:
            where[n] = off
            off += self.segs[n][1] - self.segs[n][0]
        names = sorted(self.segs, key=lambda n: self.segs[n][0])
        sw = self.width // nchip
        cw = -(-sw // HD) * HD
        table, heads = [], []
        for s in range(nchip):
            lo, hi = s * sw, (s + 1) * sw
            pieces = []
            for n in names:
                a, b = max(lo, self.segs[n][0]), min(hi, self.segs[n][1])
                if a < b:
                    pieces.append((where[n] + a - self.segs[n][0], b - a))
            start, width = pieces[0]
            n_head = min((-start) % HD, width)
            body = ([(start + n_head, width - n_head)] if width > n_head else []) + pieces[1:]
            rows, at = [], 0
            for c, w in body:
                rows.append((c, w, at))
                at += w
            if n_head:
                rows.append((start, n_head, cw - n_head))
            table.append(rows)
            heads.append(n_head)
        return table, heads, cw

    def from_containers(self, conts):
        table, _, _ = self.pieces(len(conts))
        cut = sorted((kcol, w, s, ccol) for s, rows in enumerate(table) for kcol, w, ccol in rows)
        parts, at = [], 0
        for kcol, w, s, ccol in cut:
            assert kcol == at, (kcol, at)
            parts.append(conts[s][..., ccol:ccol + w])
            at = kcol + w
        parts.append(jnp.zeros(conts[0].shape[:-1] + (self.pw - at,), conts[0].dtype))
        return jnp.concatenate(parts, axis=-1)

    def from_kernel(self, g):
        off, where = 0, {}
        for n in self.order:
            w = self.segs[n][1] - self.segs[n][0]
            where[n] = (off, off + w)
            off += w
        names = sorted(self.segs, key=lambda n: self.segs[n][0])
        return jnp.concatenate([g[..., where[n][0]:where[n][1]] for n in names], axis=-1)


def _norm_proj_fwd(h, nw, wp):
    n, d = h.shape
    pw = wp.shape[1]
    tm, tn = _tile(n, 1408, HD), _ctile(pw)

    def body(h_ref, nw_ref, w_ref, o_ref, xt_ref, xn_ref):
        @pl.when(pl.program_id(1) == 0)
        def _():
            x = h_ref[...]
            r = lax.rsqrt(jnp.mean(x * x, axis=-1, keepdims=True) + EPS)
            xn = (x * r * nw_ref[...]).astype(BF16)
            xn_ref[...] = xn
            xt_ref[...] = xn.T

        o_ref[...] = jnp.dot(xn_ref[...], w_ref[...], preferred_element_type=F32)

    return pl.pallas_call(
        body, grid=(n // tm, pw // tn),
        in_specs=[pl.BlockSpec((tm, d), lambda i, j: (i, 0)), pl.BlockSpec((1, d), lambda i, j: (0, 0)),
                  pl.BlockSpec((d, tn), lambda i, j: (0, j))],
        out_specs=[pl.BlockSpec((tm, tn), lambda i, j: (i, j)), pl.BlockSpec((d, tm), lambda i, j: (0, i))],
        out_shape=[SDS((n, pw), F32), SDS((d, n), BF16)], scratch_shapes=[pltpu.VMEM((tm, d), BF16)],
        compiler_params=_params(2), name="norm_proj_fwd")(h, nw, wp)


def _row_valid(tm, tp, base):
    row = base + _iota2((tm, 1), 0)
    return lax.rem(row, tp) >= PAD


def _proj_bwd_dx(dproj, wp, h, nw, dhn, tp):
    n, d = h.shape
    pw = wp.shape[1]
    tm, tk = _tile16(n, 1056), _ctile(pw)
    nk = pw // tk

    def body(dp_ref, w_ref, h_ref, nw_ref, dhn_ref, dh_ref, dnw_ref, acc_ref):
        i, k = pl.program_id(0), pl.program_id(1)

        @pl.when(k == 0)
        def _():
            acc_ref[...] = jnp.zeros_like(acc_ref)

        @pl.when((i == 0) & (k == 0))
        def _():
            dnw_ref[...] = jnp.zeros_like(dnw_ref)

        valid = _row_valid(tm, tp, i * tm)
        dp = jnp.where(valid, dp_ref[...], 0.0)
        acc_ref[...] += _bnt(dp, w_ref[...])

        @pl.when(k == nk - 1)
        def _():
            x = h_ref[...]
            r = lax.rsqrt(jnp.mean(x * x, axis=-1, keepdims=True) + EPS)
            xh = x * r
            dxn = acc_ref[...]
            dnw_ref[...] += jnp.sum(dxn * xh, axis=0, keepdims=True)
            dxh = dxn * nw_ref[...]
            dh_ref[...] = dhn_ref[...] + r * (dxh - xh * jnp.mean(dxh * xh, axis=-1, keepdims=True))

    return pl.pallas_call(
        body, grid=(n // tm, nk),
        in_specs=[pl.BlockSpec((tm, tk), lambda i, k: (i, k)), pl.BlockSpec((d, tk), lambda i, k: (0, k)),
                  pl.BlockSpec((tm, d), lambda i, k: (i, 0)), pl.BlockSpec((1, d), lambda i, k: (0, 0)),
                  pl.BlockSpec((tm, d), lambda i, k: (i, 0))],
        out_specs=[pl.BlockSpec((tm, d), lambda i, k: (i, 0)), pl.BlockSpec((1, d), lambda i, k: (0, 0))],
        out_shape=[SDS((n, d), F32), SDS((1, d), F32)],
        scratch_shapes=[pltpu.VMEM((tm, d), F32)], compiler_params=_params(2), name="proj_bwd_dx")(dproj, wp, h, nw, dhn)


def _proj_bwd_dw(dproj, xt, tp):
    d, n = xt.shape
    pw = dproj.shape[1]
    tm, tn = _tile(n, 1408, HD), _ctile(pw)

    def body(dp_ref, xt_ref, dw_ref):
        i = pl.program_id(1)

        @pl.when(i == 0)
        def _():
            dw_ref[...] = jnp.zeros_like(dw_ref)

        dp = jnp.where(_row_valid(tm, tp, i * tm), dp_ref[...], 0.0)
        dw_ref[...] += jnp.dot(xt_ref[...], dp.astype(BF16), preferred_element_type=F32)

    return pl.pallas_call(
        body, grid=(pw // tn, n // tm),
        in_specs=[pl.BlockSpec((tm, tn), lambda j, i: (i, j)), pl.BlockSpec((d, tm), lambda j, i: (0, i))],
        out_specs=pl.BlockSpec((d, tn), lambda j, i: (0, j)), out_shape=SDS((d, pw), F32),
        compiler_params=_params(2), name="proj_bwd_dw")(dproj, xt)


def _conv_silu(x, w, row):
    c = x * w[CONV_K - 1:CONV_K, :]
    for k in range(1, CONV_K):
        c = c + jnp.where(row >= k, pltpu.roll(x, k, axis=0), 0.0) * w[CONV_K - 1 - k:CONV_K - k, :]
    return c


def _gdn_prep_fwd(proj, conv_w, lay, nb, tp):
    n = proj.shape[0]
    nblk = 3 * NH
    cb = lay.c_qkv // HD

    def body(p_ref, w_ref, o_ref):
        j = pl.program_id(1)
        x = p_ref[...]
        row = _iota2(x.shape, 0)
        c = _conv_silu(x, w_ref[...], row)
        s = _silu(c)
        r = lax.rsqrt(_rs(s * s) + EPS)
        scale = jnp.where(j < NH, Q_SCALE, 1.0)
        y = jnp.where(j < 2 * NH, s * r * scale, s)
        o_ref[...] = jnp.where(row >= PAD, y, 0.0)

    return pl.pallas_call(
        body, grid=(nb, nblk),
        in_specs=[pl.BlockSpec((tp, HD), lambda b, j: (b, cb + j)), pl.BlockSpec((CONV_K, HD), lambda b, j: (0, j))],
        out_specs=pl.BlockSpec((tp, HD), lambda b, j: (b, j)), out_shape=SDS((n, nblk * HD), F32),
        compiler_params=_params(2), name="gdn_prep_fwd")(proj, conv_w)


def _gdn_prep_bwd(proj, conv_w, dqkv, dproj, lay, nb, tp):
    nblk = 3 * NH
    cb = lay.c_qkv // HD

    def body(p_ref, w_ref, dy_ref, dp_in, dp_ref, dw_ref):
        j, b = pl.program_id(0), pl.program_id(1)
        x = p_ref[...]
        w = w_ref[...]
        row = _iota2(x.shape, 0)
        c = _conv_silu(x, w, row)
        s = _silu(c)
        dy = jnp.where(row >= PAD, dy_ref[...], 0.0)
        r = lax.rsqrt(_rs(s * s) + EPS)
        nh = s * r
        scale = jnp.where(j < NH, Q_SCALE, 1.0)
        ds_n = scale * r * (dy - nh * _rs(dy * nh))
        ds = jnp.where(j < 2 * NH, ds_n, dy)
        dc = ds * _dsilu(c)
        dx = dc * w[CONV_K - 1:CONV_K, :]
        dws = [jnp.sum(dc * x, axis=0, keepdims=True)]
        for k in range(1, CONV_K):
            dx = dx + jnp.where(row < tp - k, pltpu.roll(dc, tp - k, axis=0), 0.0) * w[CONV_K - 1 - k:CONV_K - k, :]
            xs = jnp.where(row >= k, pltpu.roll(x, k, axis=0), 0.0)
            dws.append(jnp.sum(dc * xs, axis=0, keepdims=True))
        dp_ref[...] = dx.astype(dp_ref.dtype)
        r4 = _iota2((CONV_K, HD), 0)
        dw = jnp.zeros((CONV_K, HD), F32)
        for k in range(CONV_K):
            dw = dw + jnp.where(r4 == CONV_K - 1 - k, dws[k], 0.0)

        @pl.when(b == 0)
        def _():
            dw_ref[...] = dw

        @pl.when(b > 0)
        def _():
            dw_ref[...] += dw

    return pl.pallas_call(
        body, grid=(nblk, nb),
        in_specs=[pl.BlockSpec((tp, HD), lambda j, b: (b, cb + j)), pl.BlockSpec((CONV_K, HD), lambda j, b: (0, j)),
                  pl.BlockSpec((tp, HD), lambda j, b: (b, j)), pl.BlockSpec(memory_space=pl.ANY)],
        out_specs=[pl.BlockSpec((tp, HD), lambda j, b: (b, cb + j)), pl.BlockSpec((CONV_K, HD), lambda j, b: (0, j))],
        out_shape=[SDS(dproj.shape, dproj.dtype), SDS((CONV_K, nblk * HD), F32)],
        input_output_aliases={3: 0}, compiler_params=_params(2), name="gdn_prep_bwd")(proj, conv_w, dqkv, dproj)


def _gate_consts():
    e = np.zeros((HD, 2 * HW), np.float32)
    s = np.zeros((2 * HW, HD), np.float32)
    for h in range(NH):
        e[h, h * HD:(h + 1) * HD] = 1.0
        e[NH + h, HW + h * HD:HW + (h + 1) * HD] = 1.0
        s[h * HD, h] = 1.0
        s[HW + h * HD, NH + h] = 1.0
    return jnp.asarray(e), jnp.asarray(s)


def _gdn_tri():
    i, j = _iota2((CH, CH), 0), _iota2((CH, CH), 1)
    return i >= j, i > j


def _each(fn, *lists):
    return [fn(*xs) for xs in zip(*lists)]


def _tri_inv(a_list, eye):
    p = [-a for a in a_list]
    t = [eye + x for x in p]
    for _ in range(5):
        p = _each(_rnn, p, p)
        tp_ = _each(_rnn, t, p)
        t = _each(lambda x, y: x + y, t, tp_)
    return t


def _gdn_chunks(args, solved=None):
    causal, strict = _gdn_tri()
    eye = jnp.where(_iota2((CH, CH), 0) == _iota2((CH, CH), 1), 1.0, 0.0)
    q, k, v, beta, g, s0 = (list(t) for t in zip(*args))
    gc = [_scan_rows(x, CH) for x in g]
    dm = [jnp.where(causal, jnp.exp(jnp.where(causal, x[:, :CH] - x[:, :CH].T, 0.0)), 0.0) for x in gc]
    ds = [jnp.where(strict, x, 0.0) for x in dm]
    kb = _each(lambda x, y: x * y, k, beta)
    kk = _each(_ent, kb, k)
    a = _each(lambda x, y: x * y, kk, ds)
    eg = [jnp.exp(x) for x in gc]
    rw = _each(lambda x, y: x * y, kb, eg)
    if solved is None:
        tinv = _tri_inv(a, eye)
        rv = _each(lambda x, y: x * y, v, beta)
        u = _each(_rnn, tinv, rv)
        w = _each(_rnn, tinv, rw)
    else:
        tinv, u, w = (list(t) for t in zip(*solved))
    ws = _each(_enn, w, s0)
    vn = _each(lambda x, y: x - y, u, ws)
    qk = _each(_ent, q, k)
    p = _each(lambda x, y: x * y, qk, dm)
    qg = _each(lambda x, y: x * y, q, eg)
    out = []
    for i in range(len(args)):
        gl = gc[i][CH - 1:CH, :]
        ek = jnp.exp(gl - gc[i])
        out.append(dict(gc=gc[i], dm=dm[i], ds=ds[i], kb=kb[i], a=a[i], tinv=tinv[i], eg=eg[i], rw=rw[i], u=u[i], w=w[i],
                        vn=vn[i], p=p[i], qg=qg[i], egl=jnp.exp(gl), ek=ek, kd=k[i] * ek))
    return out


def _gdn_gates(ba, e, alog, dtb):
    raw = _nn(ba, e)
    beta = _sig(raw[:, :HW])
    za = raw[:, HW:] + dtb
    g = -jnp.exp(alog) * _softplus(za)
    return beta, g, za


def _seqs_per_step(nb):
    return 4 if nb % 4 == 0 else (2 if nb % 2 == 0 else 1)


def _gdn_fwd(qkv, proj, e_mat, alog, dtb, lay, nb, nc):
    n = qkv.shape[0]
    tp = n // nb
    cba = lay.c_ba // HD
    gb = _seqs_per_step(nb)

    def body(x_ref, ba_ref, e_ref, al_ref, dt_ref, o_ref, so_ref, sv_ref, s_ref):
        @pl.when(pl.program_id(1) == 0)
        def _():
            s_ref[...] = jnp.zeros_like(s_ref)

        args = []
        for j in range(gb):
            beta, g, _ = _gdn_gates(ba_ref[j], e_ref[...], al_ref[...], dt_ref[...])
            for h in range(NH):
                hs = slice(h * HD, (h + 1) * HD)
                args.append((x_ref[j, :, hs], x_ref[j, :, HW + h * HD:HW + (h + 1) * HD],
                             x_ref[j, :, 2 * HW + h * HD:2 * HW + (h + 1) * HD], beta[:, hs], g[:, hs], s_ref[j, h]))
        cs = _gdn_chunks(args)
        s0s = [a[5] for a in args]
        o1 = _each(lambda c, s0: _enn(c["qg"], s0), cs, s0s)
        o2 = [_enn(c["p"], c["vn"]) for c in cs]
        upd = [_etn(c["kd"], c["vn"]) for c in cs]
        res = [(o1[i] + o2[i], s0s[i] * cs[i]["egl"] + upd[i]) for i in range(len(cs))]
        zero = jnp.zeros((CH, HD - CH), F32)
        for j in range(gb):
            for h in range(NH):
                c = cs[j * NH + h]
                so_ref[j, h] = args[j * NH + h][5]
                sv_ref[j, h] = jnp.concatenate([c["u"], c["w"], c["tinv"], zero], axis=-1)
                s_ref[j, h] = res[j * NH + h][1]
            o_ref[j] = jnp.concatenate([res[j * NH + h][0] for h in range(NH)], axis=-1)

    o, st, sv = pl.pallas_call(
        body, grid=(nb // gb, nc),
        in_specs=[pl.BlockSpec((gb, CH, 3 * HW), lambda b, c: (b, c, 0)), pl.BlockSpec((gb, CH, HD), lambda b, c: (b, c, cba)),
                  pl.BlockSpec((HD, 2 * HW), lambda b, c: (0, 0)), pl.BlockSpec((1, HW), lambda b, c: (0, 0)),
                  pl.BlockSpec((1, HW), lambda b, c: (0, 0))],
        out_specs=[pl.BlockSpec((gb, CH, HW), lambda b, c: (b, c, 0)),
                   pl.BlockSpec((gb, None, NH, HD, HD), lambda b, c: (b, c, 0, 0, 0)),
                   pl.BlockSpec((gb, None, NH, CH, 3 * HD), lambda b, c: (b, c, 0, 0, 0))],
        out_shape=[SDS((nb, tp, HW), F32), SDS((nb, nc, NH, HD, HD), F32), SDS((nb, nc, NH, CH, 3 * HD), F32)],
        scratch_shapes=[pltpu.VMEM((gb, NH, HD, HD), F32)], compiler_params=_params(2), name="gdn_fwd")(
            qkv.reshape(nb, tp, 3 * HW), proj.reshape(nb, tp, -1), e_mat, alog, dtb)
    return o.reshape(n, HW), st, sv


def _gdn_bwd(qkv, proj, e_mat, s_mat, alog, dtb, states, solved, do, dproj, lay, nb, nc):
    n = qkv.shape[0]
    tp = n // nb
    cba = lay.c_ba // HD
    gb = _seqs_per_step(nb)

    def body(x_ref, ba_ref, e_ref, sm_ref, al_ref, dt_ref, st_ref, sv_ref, do_ref, dp_in, dx_ref, dba_ref, acc_ref, ds_ref):
        ci = pl.program_id(1)

        @pl.when(ci == 0)
        def _():
            ds_ref[...] = jnp.zeros_like(ds_ref)

        @pl.when((ci == 0) & (pl.program_id(0) == 0))
        def _():
            acc_ref[...] = jnp.zeros_like(acc_ref)

        causal, strict = _gdn_tri()
        alog = al_ref[...]
        row = _iota2((CH, 1), 0)
        valid = (row >= PAD) | (ci < nc - 1)
        last = row == CH - 1
        gates = [_gdn_gates(ba_ref[j], e_ref[...], alog, dt_ref[...]) for j in range(gb)]
        args, do, ds1, solved = [], [], [], []
        for j in range(gb):
            beta, g, _ = gates[j]
            for h in range(NH):
                hs = slice(h * HD, (h + 1) * HD)
                args.append((x_ref[j, :, hs], x_ref[j, :, HW + h * HD:HW + (h + 1) * HD],
                             x_ref[j, :, 2 * HW + h * HD:2 * HW + (h + 1) * HD], beta[:, hs], g[:, hs], st_ref[j, h]))
                do.append(do_ref[j, :, hs])
                ds1.append(ds_ref[j, h])
                solved.append((sv_ref[j, h, :, 2 * HD:2 * HD + CH], sv_ref[j, h, :, 0:HD], sv_ref[j, h, :, HD:2 * HD]))
        q, k, v, bh, _, s0 = (list(t) for t in zip(*args))
        cs = _gdn_chunks(args, solved)
        get = lambda name: [c[name] for c in cs]
        mul = lambda x, y: x * y
        add = lambda x, y: x + y
        dvn = _each(add, _each(_etn, get("p"), do), _each(_enn, get("kd"), ds1))
        dqg = _each(_ent, do, s0)
        dp = [jnp.where(causal, x, 0.0) for x in _each(_ent, do, get("vn"))]
        dkd = _each(_ent, get("vn"), ds1)
        dw = [-x for x in _each(_ent, dvn, s0)]
        ds_a = _each(_etn, get("qg"), do)
        ds_b = _each(_etn, get("w"), dvn)
        ds_new = [ds_a[i] - ds_b[i] + ds1[i] * cs[i]["egl"] for i in range(len(cs))]
        drv = _each(_rtn, get("tinv"), dvn)
        drw = _each(_rtn, get("tinv"), dw)
        da_1 = _each(_rnt, drv, get("u"))
        da_2 = _each(_rnt, drw, get("w"))
        da = [jnp.where(strict, -(x + y), 0.0) for x, y in zip(da_1, da_2)]
        m = [da[i] * cs[i]["a"] + dp[i] * cs[i]["p"] for i in range(len(cs))]
        dkk = _each(mul, da, get("ds"))
        dqk = _each(mul, dp, get("dm"))
        dq = _each(add, _each(_enn, dqk, k), _each(mul, dqg, get("eg")))
        dkb = _each(add, _each(_enn, dkk, k), _each(mul, drw, get("eg")))
        dk_1 = _each(_etn, dqk, q)
        dk_2 = _each(_etn, dkk, get("kb"))
        dk = [dk_1[i] + dk_2[i] + dkd[i] * cs[i]["ek"] + dkb[i] * bh[i] for i in range(len(cs))]
        dv = _each(mul, drv, bh)
        dbeta, dg = [], []
        for i, c in enumerate(cs):
            dbeta.append(_rs(drv[i] * v[i]) + _rs(dkb[i] * k[i]) + jnp.zeros((CH, HD), F32))
            t_kd = _rs(dkd[i] * c["kd"])
            dgc = _rs(m[i]) - _rs(m[i].T) + _rs(dqg[i] * c["qg"]) + _rs(drw[i] * c["rw"]) - t_kd
            tail = jnp.sum(t_kd, axis=0, keepdims=True) + c["egl"] * jnp.sum(_rs(s0[i] * ds1[i]), axis=0, keepdims=True)
            dgc = dgc + jnp.where(last, tail, 0.0)
            dg.append(_scan_rows(dgc + jnp.zeros((CH, HD), F32), CH, reverse=True))
        r8 = _iota2((8, HW), 0)
        upd = jnp.zeros((8, HW), F32)
        for j in range(gb):
            sl = slice(j * NH, (j + 1) * NH)
            beta, g, za = gates[j]
            for h in range(NH):
                ds_ref[j, h] = ds_new[j * NH + h]
            dx_ref[j] = jnp.concatenate(dq[sl] + dk[sl] + dv[sl], axis=-1)
            dbeta_j = jnp.where(valid, jnp.concatenate(dbeta[sl], axis=-1), 0.0)
            dg_j = jnp.where(valid, jnp.concatenate(dg[sl], axis=-1), 0.0)
            draw_b = dbeta_j * beta * (1.0 - beta)
            draw_a = dg_j * (-jnp.exp(alog)) * _sig(za)
            dba_ref[j] = _nn(jnp.concatenate([draw_b, draw_a], axis=-1), sm_ref[...]).astype(dba_ref.dtype)
            upd = upd + jnp.where(r8 == 0, jnp.sum(dg_j * g, axis=0, keepdims=True), 0.0) + jnp.where(
                r8 == 1, jnp.sum(draw_a, axis=0, keepdims=True), 0.0)
        acc_ref[...] += upd

    rc = lambda c: nc - 1 - c
    dqkv, dproj3, acc = pl.pallas_call(
        body, grid=(nb // gb, nc),
        in_specs=[pl.BlockSpec((gb, CH, 3 * HW), lambda b, c: (b, rc(c), 0)), pl.BlockSpec((gb, CH, HD), lambda b, c: (b, rc(c), cba)),
                  pl.BlockSpec((HD, 2 * HW), lambda b, c: (0, 0)), pl.BlockSpec((2 * HW, HD), lambda b, c: (0, 0)),
                  pl.BlockSpec((1, HW), lambda b, c: (0, 0)), pl.BlockSpec((1, HW), lambda b, c: (0, 0)),
                  pl.BlockSpec((gb, None, NH, HD, HD), lambda b, c: (b, rc(c), 0, 0, 0)),
                  pl.BlockSpec((gb, None, NH, CH, 3 * HD), lambda b, c: (b, rc(c), 0, 0, 0)),
                  pl.BlockSpec((gb, CH, HW), lambda b, c: (b, rc(c), 0)), pl.BlockSpec(memory_space=pl.ANY)],
        out_specs=[pl.BlockSpec((gb, CH, 3 * HW), lambda b, c: (b, rc(c), 0)), pl.BlockSpec((gb, CH, HD), lambda b, c: (b, rc(c), cba)),
                   pl.BlockSpec((8, HW), lambda b, c: (0, 0))],
        out_shape=[SDS((nb, tp, 3 * HW), F32), SDS((nb, tp, dproj.shape[1]), dproj.dtype), SDS((8, HW), F32)],
        input_output_aliases={9: 1},
        scratch_shapes=[pltpu.VMEM((gb, NH, HD, HD), F32)], compiler_params=_params(2), name="gdn_bwd")(
            qkv.reshape(nb, tp, 3 * HW), proj.reshape(nb, tp, -1), e_mat, s_mat, alog, dtb, states, solved, do.reshape(nb, tp, HW),
            dproj.reshape(nb, tp, -1))
    return dqkv.reshape(n, 3 * HW), dproj3.reshape(dproj.shape), acc


def _hgrn_inputs(zq, zf, lb):
    sg = _sig(zf)
    sgn = _sig(-zf)
    pos = lb > 0.0
    lbp = jnp.where(pos, lb, 0.0)
    fpos = lbp + (1.0 - lbp) * sg
    lf = jnp.where(pos, jnp.log(jnp.where(pos, fpos, 1.0)), _logsig(zf))
    k = (1.0 - lbp) * sgn
    q = _silu(zq) * Q_SCALE
    return q, k, lf, sg, sgn, pos, lbp, fpos


def _hgrn_consts():
    i3, j3 = _iota2((SUB, SUB, HD), 0), _iota2((SUB, SUB, HD), 1)
    return i3 >= j3


def _sum_j(x):
    return jnp.sum(x.reshape(SUB, SUB, HD), axis=1)


def _sum_i(x):
    return jnp.sum(x.reshape(SUB, SUB, HD), axis=0)


def _pairs(a, b):
    return (a[:, None, :] * b[None, :, :]).reshape(SUB * SUB, HD)


def _hgrn_sub(q, k, v, bc, st, consts):
    mask3 = consts
    bl = bc[SUB - 1:SUB, :]
    p3 = jnp.where(mask3, jnp.exp(jnp.where(mask3, bc[:, None, :] - bc[None, :, :], 0.0)), 0.0).reshape(SUB * SUB, HD)
    x = _pairs(q, k) * p3
    srep = _rs(x)
    vt = jnp.broadcast_to(v[None, :, :], (SUB, SUB, HD)).reshape(SUB * SUB, HD)
    eb = jnp.exp(bc)
    qe = q * eb
    o = _hnt(qe, st) + _sum_j(_rr(srep) * _rr(vt))
    ek = jnp.exp(bl - bc)
    kd = k * ek
    ebl = jnp.exp(bl)
    st1 = st * ebl + _htn(v, kd)
    return o, st1, dict(bc=bc, p3=p3, srep=srep, vt=vt, eb=eb, qe=qe, ek=ek, kd=kd, ebl=ebl)


def _hgrn_fwd(proj, lb, lay, nb, nc):
    n = proj.shape[0]
    tp = n // nb
    cbb = lay.c_b // (3 * HW)
    gb = _seqs_per_step(nb)

    def body(z_ref, lb_ref, o_ref, so_ref, s_ref):
        @pl.when(pl.program_id(1) == 0)
        def _():
            s_ref[...] = jnp.zeros_like(s_ref)

        consts = _hgrn_consts()
        for j in range(gb):
            outs = []
            for h in range(NH):
                hs = slice(h * HD, (h + 1) * HD)
                q, k, lf = _hgrn_inputs(z_ref[j, :, hs], z_ref[j, :, HW + h * HD:HW + (h + 1) * HD], lb_ref[:, hs])[:3]
                v = z_ref[j, :, 2 * HW + h * HD:2 * HW + (h + 1) * HD]
                st = s_ref[j, h]
                so_ref[j, h] = st
                bc = _scan_rows(lf, SUB)
                oh = []
                for s in range(CH // SUB):
                    rs = slice(s * SUB, (s + 1) * SUB)
                    o, st, _ = _hgrn_sub(q[rs], k[rs], v[rs], bc[rs], st, consts)
                    oh.append(o)
                s_ref[j, h] = st
                outs.append(jnp.concatenate(oh, axis=0))
            o_ref[j] = jnp.concatenate(outs, axis=-1)

    o, st = pl.pallas_call(
        body, grid=(nb // gb, nc),
        in_specs=[pl.BlockSpec((gb, CH, 3 * HW), lambda b, c: (b, c, cbb)), pl.BlockSpec((1, HW), lambda b, c: (0, 0))],
        out_specs=[pl.BlockSpec((gb, CH, HW), lambda b, c: (b, c, 0)),
                   pl.BlockSpec((gb, None, NH, HD, HD), lambda b, c: (b, c, 0, 0, 0))],
        out_shape=[SDS((nb, tp, HW), F32), SDS((nb, nc, NH, HD, HD), F32)],
        scratch_shapes=[pltpu.VMEM((gb, NH, HD, HD), F32)], compiler_params=_params(2), name="hgrn_fwd")(
            proj.reshape(nb, tp, -1), lb)
    return o.reshape(n, HW), st


def _hgrn_bwd(proj, lb, states, do, dproj, lay, nb, nc):
    n = proj.shape[0]
    tp = n // nb
    cbb = lay.c_b // (3 * HW)
    nsub = CH // SUB
    gb = _seqs_per_step(nb)

    def body(z_ref, lb_ref, st_ref, do_ref, dp_in, dz_ref, acc_ref, ds_ref):
        ci = pl.program_id(1)

        @pl.when(ci == 0)
        def _():
            ds_ref[...] = jnp.zeros_like(ds_ref)

        @pl.when((ci == 0) & (pl.program_id(0) == 0))
        def _():
            acc_ref[...] = jnp.zeros_like(acc_ref)

        upd = jnp.zeros((8, HW), F32)
        for j in range(gb):
            upd = upd + one_seq(j, ci, z_ref, lb_ref, st_ref, do_ref, dz_ref, ds_ref)
        acc_ref[...] += upd

    def one_seq(j, ci, z_ref, lb_ref, st_ref, do_ref, dz_ref, ds_ref):
        consts = _hgrn_consts()
        row = _iota2((CH, 1), 0)
        valid = (row >= PAD) | (ci < nc - 1)
        lastrow = _iota2((SUB, 1), 0) == SUB - 1
        dzq, dzf, dzi, dlbs = [], [], [], []
        for h in range(NH):
            hs = slice(h * HD, (h + 1) * HD)
            zq, zf = z_ref[j, :, hs], z_ref[j, :, HW + h * HD:HW + (h + 1) * HD]
            q, k, lf, sg, sgn, pos, lbp, fpos = _hgrn_inputs(zq, zf, lb_ref[:, hs])
            v = z_ref[j, :, 2 * HW + h * HD:2 * HW + (h + 1) * HD]
            doh = do_ref[j, :, hs]
            sts, fw = [st_ref[j, h]], []
            bc = _scan_rows(lf, SUB)
            for s in range(nsub):
                rs = slice(s * SUB, (s + 1) * SUB)
                _, st1, c = _hgrn_sub(q[rs], k[rs], v[rs], bc[rs], sts[-1], consts)
                sts.append(st1)
                fw.append(c)
            dst = ds_ref[j, h]
            dq_l, dk_l, dv_l, dlf_l = [None] * nsub, [None] * nsub, [None] * nsub, [None] * nsub
            for s in reversed(range(nsub)):
                rs = slice(s * SUB, (s + 1) * SUB)
                c, st = fw[s], sts[s]
                qs, ks, vs, dos = q[rs], k[rs], v[rs], doh[rs]
                dqe = _hnn(dos, st)
                dkd = _hnn(vs, dst)
                dsrep = _rs(_pairs(_rr(dos), _rr(vs)))
                w = dsrep * c["p3"]
                kt = jnp.broadcast_to(ks[None, :, :], (SUB, SUB, HD)).reshape(SUB * SUB, HD)
                qt = jnp.broadcast_to(qs[:, None, :], (SUB, SUB, HD)).reshape(SUB * SUB, HD)
                dq_i = _sum_j(w * kt)
                dk_i = _sum_i(w * qt)
                dot = jnp.broadcast_to(_rr(dos)[:, None, :], (SUB, SUB, HD)).reshape(SUB * SUB, HD)
                dvv = _sum_i(_rr(c["srep"]) * dot) + _hnt(c["kd"], dst)
                t_kd = dkd * c["kd"]
                dbc = dqe * c["qe"] - t_kd + qs * dq_i - ks * dk_i
                tail = jnp.sum(t_kd, axis=0, keepdims=True) + c["ebl"] * jnp.sum(st * dst, axis=0, keepdims=True)
                dbc = dbc + jnp.where(lastrow, tail, 0.0)
                dlf_l[s] = dbc
                dq_l[s] = dq_i + dqe * c["eb"]
                dk_l[s] = dk_i + dkd * c["ek"]
                dv_l[s] = dvv
                dst = _htn(dos, c["qe"]) + dst * c["ebl"]
            ds_ref[j, h] = dst
            dq, dk, dv, dbc = (jnp.concatenate(t, axis=0) for t in (dq_l, dk_l, dv_l, dlf_l))
            dlf = _scan_rows(dbc, SUB, reverse=True)
            dlft = dlf - dk * (1.0 - k)
            dlf_dz = jnp.where(pos, (1.0 - lbp) * sg * sgn / jnp.where(pos, fpos, 1.0), sgn)
            dlf_dlb = jnp.where(pos, sgn / jnp.where(pos, fpos, 1.0), 0.0)
            dzq.append(dq * Q_SCALE * _dsilu(zq))
            dzf.append(dlft * dlf_dz)
            dzi.append(dv)
            dlbs.append(jnp.sum(jnp.where(valid, dlft * dlf_dlb, 0.0), axis=0, keepdims=True))
        dz_ref[j] = jnp.concatenate(dzq + dzf + dzi, axis=-1).astype(dz_ref.dtype)
        return jnp.where(_iota2((8, HW), 0) == 0, jnp.concatenate(dlbs, axis=-1), 0.0)

    rc = lambda c: nc - 1 - c
    dproj3, acc = pl.pallas_call(
        body, grid=(nb // gb, nc),
        in_specs=[pl.BlockSpec((gb, CH, 3 * HW), lambda b, c: (b, rc(c), cbb)), pl.BlockSpec((1, HW), lambda b, c: (0, 0)),
                  pl.BlockSpec((gb, None, NH, HD, HD), lambda b, c: (b, rc(c), 0, 0, 0)),
                  pl.BlockSpec((gb, CH, HW), lambda b, c: (b, rc(c), 0)), pl.BlockSpec(memory_space=pl.ANY)],
        out_specs=[pl.BlockSpec((gb, CH, 3 * HW), lambda b, c: (b, rc(c), cbb)), pl.BlockSpec((8, HW), lambda b, c: (0, 0))],
        out_shape=[SDS((nb, tp, dproj.shape[1]), dproj.dtype), SDS((8, HW), F32)],
        input_output_aliases={4: 0},
        scratch_shapes=[pltpu.VMEM((gb, NH, HD, HD), F32)], compiler_params=_params(2), name="hgrn_bwd")(
            proj.reshape(nb, tp, -1), lb, states, do.reshape(nb, tp, HW), dproj.reshape(nb, tp, -1))
    return dproj3.reshape(dproj.shape), acc


def _gated_norm(o, z, gamma):
    ys, ns, rs = [], [], []
    for h in range(NH):
        hs = slice(h * HD, (h + 1) * HD)
        oh = o[:, hs]
        r = lax.rsqrt(jnp.mean(oh * oh, axis=-1, keepdims=True) + EPS)
        nh = oh * r
        ys.append(nh * gamma * _silu(z[:, hs]))
        ns.append(nh)
        rs.append(r)
    return jnp.concatenate(ys, axis=-1), ns, rs


def _merge_fwd(h, oa, ob, proj, ga, gb, wa, wb, wo, lay):
    n, d = h.shape
    tm = _tile(n, 384)
    wm = lay.wm

    def body(h_ref, oa_ref, ob_ref, p_ref, ga_ref, gb_ref, wa_ref, wb_ref, wo_ref, out_ref):
        ya, _, _ = _gated_norm(oa_ref[...], p_ref[:, 0:HW], ga_ref[...])
        yb, _, _ = _gated_norm(ob_ref[...], p_ref[:, HW:2 * HW], gb_ref[...])
        ya2 = _bnn(ya, wa_ref[...])
        yb2 = _bnn(yb, wb_ref[...])
        mixed = _sig(p_ref[:, 2 * HW:2 * HW + d]) * ya2 + _sig(p_ref[:, 2 * HW + d:2 * HW + 2 * d]) * yb2
        out_ref[...] = h_ref[...] + _bnn(mixed, wo_ref[...])

    full = lambda shape: pl.BlockSpec(shape, lambda i: (0, 0))
    return pl.pallas_call(
        body, grid=(n // tm,),
        in_specs=[pl.BlockSpec((tm, d), lambda i: (i, 0)), pl.BlockSpec((tm, HW), lambda i: (i, 0)),
                  pl.BlockSpec((tm, HW), lambda i: (i, 0)), pl.BlockSpec((tm, wm), lambda i: (i, 0)),
                  full((1, HD)), full((1, HD)), full((HW, d)), full((HW, d)), full((d, d))],
        out_specs=pl.BlockSpec((tm, d), lambda i: (i, 0)), out_shape=SDS((n, d), F32),
        compiler_params=_params(1), name="merge_fwd")(h, oa, ob, proj, ga, gb, wa, wb, wo)


def _gated_norm_bwd(dy, o, z, gamma):
    dos, dzs = [], []
    dgam = jnp.zeros((1, HD), F32)
    for h in range(NH):
        hs = slice(h * HD, (h + 1) * HD)
        oh, zh, dyh = o[:, hs], z[:, hs], dy[:, hs]
        r = lax.rsqrt(jnp.mean(oh * oh, axis=-1, keepdims=True) + EPS)
        nh = oh * r
        dzs.append(dyh * nh * gamma * _dsilu(zh))
        dng = dyh * _silu(zh)
        dgam = dgam + jnp.sum(dng * nh, axis=0, keepdims=True)
        dn = dng * gamma
        dos.append(r * (dn - nh * jnp.mean(dn * nh, axis=-1, keepdims=True)))
    return jnp.concatenate(dos, axis=-1), jnp.concatenate(dzs, axis=-1), dgam


def _merge_bwd(dhn, oa, ob, proj, ga, gb, wa, wb, wo, lay, tp):
    n, d = dhn.shape
    tm = _tile(n, 256)
    wm = lay.wm

    def body(dh_ref, oa_ref, ob_ref, p_ref, ga_ref, gb_ref, wa_ref, wb_ref, wo_ref,
             dp_ref, doa_ref, dob_ref, dwa_ref, dwb_ref, dwo_ref, dga_ref, dgb_ref):
        i = pl.program_id(0)

        @pl.when(i == 0)
        def _():
            for r in (dwa_ref, dwb_ref, dwo_ref, dga_ref, dgb_ref):
                r[...] = jnp.zeros_like(r)

        dh = jnp.where(_row_valid(tm, tp, i * tm), dh_ref[...], 0.0)
        oa, ob = oa_ref[...], ob_ref[...]
        za, zb = p_ref[:, 0:HW], p_ref[:, HW:2 * HW]
        gta, gtb = p_ref[:, 2 * HW:2 * HW + d], p_ref[:, 2 * HW + d:2 * HW + 2 * d]
        ya, _, _ = _gated_norm(oa, za, ga_ref[...])
        yb, _, _ = _gated_norm(ob, zb, gb_ref[...])
        ya2 = _bnn(ya, wa_ref[...])
        yb2 = _bnn(yb, wb_ref[...])
        sa, sb = _sig(gta), _sig(gtb)
        mixed = sa * ya2 + sb * yb2
        dmixed = _bnt(dh, wo_ref[...])
        dwo_ref[...] += _btn(mixed, dh)
        dya2 = dmixed * sa
        dyb2 = dmixed * sb
        dwa_ref[...] += _btn(ya, dya2)
        dwb_ref[...] += _btn(yb, dyb2)
        doa, dza, dga = _gated_norm_bwd(_bnt(dya2, wa_ref[...]), oa, za, ga_ref[...])
        dob, dzb, dgb = _gated_norm_bwd(_bnt(dyb2, wb_ref[...]), ob, zb, gb_ref[...])
        dga_ref[...] += dga
        dgb_ref[...] += dgb
        doa_ref[...] = doa
        dob_ref[...] = dob
        dt = dp_ref.dtype
        dp_ref[:, 0:HW] = dza.astype(dt)
        dp_ref[:, HW:2 * HW] = dzb.astype(dt)
        dp_ref[:, 2 * HW:2 * HW + d] = (dmixed * ya2 * sa * (1.0 - sa)).astype(dt)
        dp_ref[:, 2 * HW + d:2 * HW + 2 * d] = (dmixed * yb2 * sb * (1.0 - sb)).astype(dt)

    full = lambda shape: pl.BlockSpec(shape, lambda i: (0, 0))
    rows = lambda w: pl.BlockSpec((tm, w), lambda i: (i, 0))
    return pl.pallas_call(
        body, grid=(n // tm,),
        in_specs=[rows(d), rows(HW), rows(HW), rows(wm), full((1, HD)), full((1, HD)), full((HW, d)), full((HW, d)), full((d, d))],
        out_specs=[rows(wm), rows(HW), rows(HW), full((HW, d)), full((HW, d)), full((d, d)), full((1, HD)), full((1, HD))],
        out_shape=[SDS((n, lay.pw), BF16), SDS((n, HW), F32), SDS((n, HW), F32), SDS((HW, d), F32), SDS((HW, d), F32),
                   SDS((d, d), F32), SDS((1, HD), F32), SDS((1, HD), F32)],
        compiler_params=_params(1), name="merge_bwd")(dhn, oa, ob, proj, ga, gb, wa, wb, wo)


def _loss_head(h, target, fw, nb, tp):
    n, d = h.shape
    tr = _tile(tp, 768)
    nr = tp // tr

    def body(h_ref, t_ref, fw_ref, lp_ref, dh_ref, dfw_ref):
        b, i = pl.program_id(0), pl.program_id(1)

        @pl.when((b == 0) & (i == 0))
        def _():
            dfw_ref[...] = jnp.zeros_like(dfw_ref)

        x = h_ref[...]
        r = lax.rsqrt(jnp.mean(x * x, axis=-1, keepdims=True) + EPS)
        xh = x * r
        live = i * tr + _iota2((tr, 1), 0) >= CH
        err = jnp.where(live, xh * fw_ref[...] - t_ref[...], 0.0)
        lp_ref[...] = jnp.zeros_like(lp_ref) + 0.5 * jnp.sum(_rs(err * err), axis=0, keepdims=True) / d
        dy = err / d
        dfw_ref[...] += jnp.sum(dy * xh, axis=0, keepdims=True)
        dxh = dy * fw_ref[...]
        dh_ref[...] = r * (dxh - xh * jnp.mean(dxh * xh, axis=-1, keepdims=True))

    rows = pl.BlockSpec((tr, d), lambda b, i: (b * nr + i, 0))
    return pl.pallas_call(
        body, grid=(nb, nr), in_specs=[rows, rows, pl.BlockSpec((1, d), lambda b, i: (0, 0))],
        out_specs=[pl.BlockSpec((8, HD), lambda b, i: (b * nr + i, 0)), rows, pl.BlockSpec((1, d), lambda b, i: (0, 0))],
        out_shape=[SDS((nb * nr * 8, HD), F32), SDS((n, d), F32), SDS((1, d), F32)],
        compiler_params=_params(2), name="loss_head")(h, target, fw)


def _lb_fwd(lb):
    def body(x_ref, o_ref):
        x = x_ref[...]
        mx = jnp.max(x, axis=0, keepdims=True)
        e = jnp.exp(x - mx)
        sm = e / jnp.sum(e, axis=0, keepdims=True)
        run = jnp.zeros((1, HW), F32)
        for l in range(DEPTH):
            run = run + sm[l:l + 1, :]
            o_ref[l:l + 1, :] = run - sm[0:1, :]

    return pl.pallas_call(body, out_shape=SDS(lb.shape, F32), name="lb_fwd")(lb)


def _lb_bwd(lb, dlb_all):
    def body(x_ref, d_ref, o_ref):
        x = x_ref[...]
        dl = d_ref[...]
        mx = jnp.max(x, axis=0, keepdims=True)
        e = jnp.exp(x - mx)
        sm = e / jnp.sum(e, axis=0, keepdims=True)
        tot = jnp.sum(dl, axis=0, keepdims=True)
        dsm = []
        run = tot
        for l in range(DEPTH):
            dsm.append(run - (tot if l == 0 else 0.0))
            run = run - dl[l:l + 1, :]
        inner = sum(sm[l:l + 1, :] * dsm[l] for l in range(DEPTH))
        for l in range(DEPTH):
            o_ref[l:l + 1, :] = sm[l:l + 1, :] * (dsm[l] - inner)

    return pl.pallas_call(body, out_shape=SDS(lb.shape, F32), name="lb_bwd")(lb, dlb_all)


def _adamw(g, w, m, v):
    r, c = g.shape
    tr = _tile(r, 264)
    c1 = 1.0 / (1.0 - ADAM_B1 ** ADAM_STEP)
    c2 = 1.0 / (1.0 - ADAM_B2 ** ADAM_STEP)

    def body(g_ref, w_ref, m_ref, v_ref, d_ref, mo_ref, vo_ref):
        gg = g_ref[...]
        mn = ADAM_B1 * m_ref[...] + (1.0 - ADAM_B1) * gg
        vn = ADAM_B2 * v_ref[...] + (1.0 - ADAM_B2) * gg * gg
        d_ref[...] = -ADAM_LR * ((mn * c1) / (jnp.sqrt(vn * c2) + ADAM_EPS) + ADAM_WD * w_ref[...])
        mo_ref[...] = mn
        vo_ref[...] = vn

    spec = pl.BlockSpec((tr, c), lambda i: (i, 0))
    return pl.pallas_call(body, grid=(r // tr,), in_specs=[spec] * 4, out_specs=[spec] * 3, out_shape=[SDS(g.shape, F32)] * 3,
                          compiler_params=_params(1), name="adamw")(g, w, m, v)


def _tile16(n, target):
    return _tile(n // 2, target // 2) * 2 if n % 16 == 0 else _tile(n, target)


def _add_cores(g, got, core):
    k, r, c = got.shape
    tr = _tile16(r, 264)

    def body(c_ref, a_ref, b_ref, o_ref):
        o_ref[...] = (a_ref[...] + b_ref[...].astype(F32)).astype(o_ref.dtype)

    spec = pl.BlockSpec((None, tr, c), lambda s, i, cr: (s, i, 0))
    return pl.pallas_call(
        body, grid_spec=pltpu.PrefetchScalarGridSpec(
            num_scalar_prefetch=1, grid=(k, r // tr),
            in_specs=[pl.BlockSpec((None, None, tr, c), lambda s, i, cr: (cr[0], s, i, 0)), spec], out_specs=spec),
        out_shape=SDS(got.shape, got.dtype), compiler_params=_params(2), name="add_cores")(core, g, got)


def _sum_chips(parts, own, place):
    k, r, c = parts.shape
    tr = _tile16(r, 264)

    def body(p_ref, *refs):
        part_refs, own_ref, o_ref = refs[:k], refs[k], refs[k + 1]
        mine = own_ref[...].astype(F32)
        acc = None
        for s in range(k):
            term = jnp.where(p_ref[0] == s, mine, part_refs[s][...].astype(F32))
            acc = term if acc is None else acc + term
        o_ref[...] = acc

    slots = jnp.stack([jnp.where(place[0] == s, (s + 1) % k, s) for s in range(k)]).astype(jnp.int32)
    other = lambda s: pl.BlockSpec((None, tr, c), lambda i, p: (p[2 + s], i, 0))
    return pl.pallas_call(
        body, grid_spec=pltpu.PrefetchScalarGridSpec(
            num_scalar_prefetch=1, grid=(r // tr,),
            in_specs=[other(s) for s in range(k)] + [pl.BlockSpec((None, tr, c), lambda i, p: (p[0], i, 0))],
            out_specs=pl.BlockSpec((None, tr, c), lambda i, p: (p[1], i, 0))),
        out_shape=SDS((2, r, c), F32), compiler_params=_params(1), name="sum_chips")(
            jnp.concatenate([place, slots]), *([parts] * k), own)


def _meta_grad(dh, nb, nc):
    d = dh.shape[1]

    def body(x_ref, o_ref):
        @pl.when(pl.program_id(0) == 0)
        def _():
            o_ref[...] = jnp.zeros_like(o_ref)

        o_ref[...] += x_ref[PAD:CH, :]

    return pl.pallas_call(body, grid=(nb,), in_specs=[pl.BlockSpec((CH, d), lambda b: (b * nc, 0))],
                          out_specs=pl.BlockSpec((N_META, d), lambda b: (0, 0)), out_shape=SDS((N_META, d), F32),
                          compiler_params=_params(1), name="meta_grad")(dh)


ANY = pl.BlockSpec(memory_space=pl.ANY)


def _place():
    x, y, c = lax.axis_index("x"), lax.axis_index("y"), lax.axis_index("c")
    chips = [(1 - x, y), (x, 1 - y), (1 - x, 1 - y)]
    return x, y, c, chips


def _remote(src, dst, send_sems, recv_sems, k, to):
    return pltpu.make_async_remote_copy(src_ref=src, dst_ref=dst, send_sem=send_sems.at[k], recv_sem=recv_sems.at[k],
                                        device_id=to, device_id_type=MESH)


def _gather_weights(pbs, ps):
    nt = len(pbs)

    def body(*refs):
        pb_refs, ps_ref, gb_refs, gs_ref = refs[:nt], refs[nt], refs[nt + 1:2 * nt + 1], refs[2 * nt + 1]
        send_sems, recv_sems, local_sems = refs[2 * nt + 2:]
        x, y, c, chips = _place()
        s = 2 * x + y
        sib = (x, y, 1 - c)
        l1 = pltpu.make_async_copy(ps_ref, gs_ref.at[s], local_sems.at[0])
        l1.start()
        sends = []
        for k, (px, py) in enumerate(chips):
            for t in range(nt):
                sends.append(_remote(pb_refs[t].at[c], gb_refs[t].at[s, c], send_sems, recv_sems, 6 * t + k, (px, py, c)))
            sends.append(_remote(ps_ref, gs_ref.at[s], send_sems, recv_sems, 6 * nt + k, (px, py, c)))
        for cp in sends:
            cp.start()
        for k, (px, py) in enumerate(chips):
            sk = 2 * px + py
            for t in range(nt):
                _remote(pb_refs[t].at[c], gb_refs[t].at[sk, c], send_sems, recv_sems, 6 * t + k, sib).wait_recv()
                fwd = _remote(gb_refs[t].at[sk, c], gb_refs[t].at[sk, c], send_sems, recv_sems, 6 * t + 3 + k, sib)
                fwd.start()
                sends.append(fwd)
        for k, (px, py) in enumerate(chips):
            sk = 2 * px + py
            for t in range(nt):
                _remote(pb_refs[t].at[c], gb_refs[t].at[sk, 1 - c], send_sems, recv_sems, 6 * t + 3 + k, sib).wait_recv()
            _remote(ps_ref, gs_ref.at[sk], send_sems, recv_sems, 6 * nt + k, sib).wait_recv()
        for cp in sends:
            cp.wait_send()
        l1.wait()

    nsem = 6 * nt + 3
    out = pl.pallas_call(
        body, in_specs=[ANY] * (nt + 1), out_specs=[ANY] * (nt + 1),
        out_shape=[SDS((4,) + pb.shape, pb.dtype) for pb in pbs] + [SDS((4,) + ps.shape, ps.dtype)],
        scratch_shapes=[pltpu.SemaphoreType.DMA((nsem,)), pltpu.SemaphoreType.DMA((nsem,)), pltpu.SemaphoreType.DMA((1,))],
        name="gather_weights")(*pbs, ps)
    return out[:nt], out[nt]


def _contain(wpad, shift):
    r, cw = wpad.shape
    tr = _tile16(r, 256)

    def body(n_ref, x_ref, o_ref):
        o_ref[...] = pltpu.roll(x_ref[...], n_ref[0], axis=1).astype(o_ref.dtype)

    spec = pl.BlockSpec((tr, cw), lambda i, n: (i, 0))
    return pl.pallas_call(
        body, grid_spec=pltpu.PrefetchScalarGridSpec(num_scalar_prefetch=1, grid=(r // tr,), in_specs=[spec], out_specs=spec),
        out_shape=SDS((r, cw), BF16), compiler_params=_params(1), name="contain")(shift, wpad)


def _place_own(gb, pb, chip):
    _, _, r, c = gb.shape
    tr = _tile16(r, 1100)

    def body(s_ref, p_ref, g_in, o_ref):
        o_ref[...] = p_ref[...]

    return pl.pallas_call(
        body, grid_spec=pltpu.PrefetchScalarGridSpec(
            num_scalar_prefetch=1, grid=(2, r // tr),
            in_specs=[pl.BlockSpec((None, tr, c), lambda h, i, s: (h, i, 0)), ANY],
            out_specs=pl.BlockSpec((None, None, tr, c), lambda h, i, s: (s[0], h, i, 0))),
        out_shape=SDS(gb.shape, gb.dtype), input_output_aliases={2: 0}, compiler_params=_params(2),
        name="place_own")(chip, pb, gb)


def _sem_scratch(n_remote, n_local):
    return [pltpu.SemaphoreType.DMA((n_remote,)), pltpu.SemaphoreType.DMA((n_remote,)), pltpu.SemaphoreType.DMA((n_local,))]


def _swap_halves(sends):
    nt = len(sends)

    def body(*refs):
        s_refs, got_refs = refs[:nt], refs[nt:2 * nt]
        send_sems, recv_sems = refs[2 * nt:]
        x, y, c, _ = _place()
        sib = (x, y, 1 - c)
        remote = [_remote(s_refs[t].at[1 - c, s], got_refs[t].at[s], send_sems, recv_sems, 4 * t + s, sib)
                  for t in range(nt) for s in range(4)]
        for cp in remote:
            cp.start()
        for cp in remote:
            cp.wait()

    return pl.pallas_call(
        body, in_specs=[ANY] * nt, out_specs=[ANY] * nt, out_shape=[SDS(g.shape[1:], g.dtype) for g in sends],
        scratch_shapes=[pltpu.SemaphoreType.DMA((4 * nt,)), pltpu.SemaphoreType.DMA((4 * nt,))], name="swap_halves")(*sends)


def _scatter_chip_sums(parts):
    nt = len(parts)

    def body(*refs):
        a_refs, r_refs = refs[:nt], refs[nt:2 * nt]
        send_sems, recv_sems = refs[2 * nt:]
        x, y, c, chips = _place()
        s = 2 * x + y
        sends = [_remote(a_refs[t].at[2 * px + py], r_refs[t].at[s], send_sems, recv_sems, 3 * t + k, (px, py, c))
                 for t in range(nt) for k, (px, py) in enumerate(chips)]
        for cp in sends:
            cp.start()
        for t in range(nt):
            for k, (px, py) in enumerate(chips):
                _remote(a_refs[t].at[s], r_refs[t].at[2 * px + py], send_sems, recv_sems, 3 * t + k, (px, py, c)).wait_recv()
        for cp in sends:
            cp.wait_send()

    return pl.pallas_call(
        body, in_specs=[ANY] * nt, out_specs=[ANY] * nt, out_shape=[SDS(a.shape, a.dtype) for a in parts],
        scratch_shapes=[pltpu.SemaphoreType.DMA((3 * nt,)), pltpu.SemaphoreType.DMA((3 * nt,))],
        name="scatter_chip_sums")(*parts)


def _join_halves(fs):
    nt = len(fs)

    def body(*refs):
        f_refs = refs[nt:2 * nt]
        send_sems, recv_sems = refs[2 * nt:]
        x, y, c, _ = _place()
        sib = (x, y, 1 - c)
        sends = [_remote(f_refs[t].at[c], f_refs[t].at[c], send_sems, recv_sems, t, sib) for t in range(nt)]
        for cp in sends:
            cp.start()
        for t in range(nt):
            _remote(f_refs[t].at[c], f_refs[t].at[1 - c], send_sems, recv_sems, t, sib).wait_recv()
        for cp in sends:
            cp.wait_send()

    return pl.pallas_call(
        body, in_specs=[ANY] * nt, out_specs=[ANY] * nt, out_shape=[SDS(f.shape, f.dtype) for f in fs],
        input_output_aliases={t: t for t in range(nt)},
        scratch_shapes=[pltpu.SemaphoreType.DMA((nt,)), pltpu.SemaphoreType.DMA((nt,))], name="join_halves")(*fs)


def _uncontain(cont, n_head, width):
    r, cw = cont.shape
    tr = _tile(r, 256)

    def body(n_ref, x_ref, o_ref):
        o_ref[...] = pltpu.roll(x_ref[...], n_ref[0], axis=1)[:, :width]

    return pl.pallas_call(
        body, grid_spec=pltpu.PrefetchScalarGridSpec(
            num_scalar_prefetch=1, grid=(r // tr,), in_specs=[pl.BlockSpec((tr, cw), lambda i, n: (i, 0))],
            out_specs=pl.BlockSpec((tr, width), lambda i, n: (i, 0))),
        out_shape=SDS((r, width), F32), compiler_params=_params(1), name="uncontain")(n_head, cont)


WEIGHTS = ("meta_tokens", "norm_w", "w_in", "conv_w", "a_log", "dt_bias", "gnorm_a", "gnorm_b", "hgrn_lower_bounds",
           "w_branch_a", "w_branch_b", "w_out", "final_norm_w")
SHARD_AXIS = {"meta_tokens": 1, "w_in": 2, "conv_w": 2, "w_branch_a": 2, "w_branch_b": 2, "w_out": 1}
FLAT_C = 1024


def _flat(parts, rows, cols=FLAT_C):
    v = jnp.concatenate([p.reshape(-1) for p in parts])
    return jnp.pad(v, (0, rows * cols - v.shape[0])).reshape(rows, cols)


def _local_step(x, target, w, lay):
    nb, seq, d = x.shape
    tp = CH + seq
    nc = tp // CH
    n = nb * tp
    e_mat, s_mat = _gate_consts()
    lb_all = _lb_fwd(w["hgrn_lower_bounds"])
    h = jnp.concatenate([jnp.zeros((nb, PAD, d), F32), jnp.broadcast_to(w["meta_tokens"][None], (nb, N_META, d)), x],
                        axis=1).reshape(n, d)
    rep = lambda a: jnp.repeat(a, HD)[None, :]
    saved = []
    for l in range(DEPTH):
        nw = w["norm_w"][l][None, :]
        proj, xn = _norm_proj_fwd(h, nw, w["w_in"][l])
        qkv = _gdn_prep_fwd(proj, w["conv_w"][l], lay, nb, tp)
        alog, dtb = rep(w["a_log"][l]), rep(w["dt_bias"][l])
        oa, sa, sva = _gdn_fwd(qkv, proj, e_mat, alog, dtb, lay, nb, nc)
        lbl = lb_all[l][None, :]
        ob, sb = _hgrn_fwd(proj, lbl, lay, nb, nc)
        ga, gb = w["gnorm_a"][l][None, :], w["gnorm_b"][l][None, :]
        hn = _merge_fwd(h, oa, ob, proj, ga, gb, w["w_branch_a"][l], w["w_branch_b"][l], w["w_out"][l], lay)
        saved.append((h, nw, proj, qkv, alog, dtb, oa, sa, lbl, ob, sb, ga, gb, xn, sva))
        h = hn
    target_p = jnp.pad(target, ((0, 0), (CH, 0), (0, 0))).reshape(n, d)
    lp, dh, dfw = _loss_head(h, target_p, w["final_norm_w"][None, :], nb, tp)
    loss = jnp.sum(lp[::8, 0])
    g = {n_: [None] * DEPTH for n_ in WEIGHTS}
    dlb_all = [None] * DEPTH
    for l in reversed(range(DEPTH)):
        h, nw, proj, qkv, alog, dtb, oa, sa, lbl, ob, sb, ga, gb, xn, sva = saved[l]
        dproj, doa, dob, dwa, dwb, dwo, dga, dgb = _merge_bwd(dh, oa, ob, proj, ga, gb, w["w_branch_a"][l],
                                                             w["w_branch_b"][l], w["w_out"][l], lay, tp)
        dproj, acc_b = _hgrn_bwd(proj, lbl, sb, dob, dproj, lay, nb, nc)
        dqkv, dproj, acc_a = _gdn_bwd(qkv, proj, e_mat, s_mat, alog, dtb, sa, sva, doa, dproj, lay, nb, nc)
        dproj, dconv = _gdn_prep_bwd(proj, w["conv_w"][l], dqkv, dproj, lay, nb, tp)
        dh, dnw = _proj_bwd_dx(dproj, w["w_in"][l], h, nw, dh, tp)
        g["w_in"][l] = _proj_bwd_dw(dproj, xn, tp)
        g["norm_w"][l] = dnw[0]
        g["conv_w"][l] = dconv
        g["a_log"][l] = acc_a[0, ::HD]
        g["dt_bias"][l] = acc_a[1, ::HD]
        g["gnorm_a"][l], g["gnorm_b"][l] = dga[0], dgb[0]
        g["w_branch_a"][l], g["w_branch_b"][l], g["w_out"][l] = dwa, dwb, dwo
        dlb_all[l] = acc_b[0]
    grads = {n_: jnp.stack(v) for n_, v in g.items() if v[0] is not None}
    grads["hgrn_lower_bounds"] = _lb_bwd(w["hgrn_lower_bounds"], jnp.stack(dlb_all))
    grads["final_norm_w"] = dfw[0]
    grads["meta_tokens"] = _meta_grad(dh, nb, nc)
    grad_x = dh.reshape(nb, tp, d)[:, CH:, :]
    return loss, grad_x, grads


def kernel(x, meta_tokens, norm_w, w_in, conv_w, a_log, dt_bias, gnorm_a, gnorm_b, hgrn_lower_bounds, w_branch_a, w_branch_b, w_out, final_norm_w, loss_target, m_meta_tokens, m_norm_w, m_w_in, m_conv_w, m_a_log, m_dt_bias, m_gnorm_a, m_gnorm_b, m_hgrn_lower_bounds, m_w_branch_a, m_w_branch_b, m_w_out, m_final_norm_w, v_meta_tokens, v_norm_w, v_w_in, v_conv_w, v_a_log, v_dt_bias, v_gnorm_a, v_gnorm_b, v_hgrn_lower_bounds, v_w_branch_a, v_w_branch_b, v_w_out, v_final_norm_w):
    wl = dict(meta_tokens=meta_tokens, norm_w=norm_w, w_in=w_in, conv_w=conv_w, a_log=a_log, dt_bias=dt_bias, gnorm_a=gnorm_a,
              gnorm_b=gnorm_b, hgrn_lower_bounds=hgrn_lower_bounds, w_branch_a=w_branch_a, w_branch_b=w_branch_b, w_out=w_out,
              final_norm_w=final_norm_w)
    ml = dict(zip(WEIGHTS, (m_meta_tokens, m_norm_w, m_w_in, m_conv_w, m_a_log, m_dt_bias, m_gnorm_a, m_gnorm_b,
                            m_hgrn_lower_bounds, m_w_branch_a, m_w_branch_b, m_w_out, m_final_norm_w)))
    vl = dict(zip(WEIGHTS, (v_meta_tokens, v_norm_w, v_w_in, v_conv_w, v_a_log, v_dt_bias, v_gnorm_a, v_gnorm_b,
                            v_hgrn_lower_bounds, v_w_branch_a, v_w_branch_b, v_w_out, v_final_norm_w)))
    d = x.shape[2]
    lay = _Layout(d)
    nchip = 4

    big = ("w_in", "w_branch_a", "w_branch_b", "w_out")
    small = ("conv_w", "meta_tokens")
    table, heads, cw = lay.pieces(nchip)
    sw = wl["w_in"].shape[2]
    chip_id = (2 * lax.axis_index("x") + lax.axis_index("y")).astype(jnp.int32)
    n_head = sum(jnp.where(chip_id == s, heads[s], 0) for s in range(nchip)).astype(jnp.int32)
    w_pad = jnp.pad(wl["w_in"], ((0, 0), (0, 0), (0, cw - sw))).reshape(DEPTH * d, cw)
    shift = jnp.where(n_head == 0, 0, cw - n_head).astype(jnp.int32).reshape(1)
    pbs = [_contain(w_pad, shift).reshape(DEPTH, d, cw)] + [wl[n].astype(BF16) for n in big[1:]]
    nsmall = sum(int(np.prod(wl[n].shape)) for n in small)
    rs = -(-nsmall // (HD * 8)) * 8
    ps = jnp.pad(jnp.concatenate([wl[n].reshape(-1) for n in small]), (0, rs * HD - nsmall)).reshape(rs, HD)
    gbig, gsmall = _gather_weights(pbs, ps)
    gbig = [_place_own(g, p, chip_id.reshape(1)) for g, p in zip(gbig, pbs)]
    gsmall = gsmall.reshape(nchip, -1)

    wf = dict(wl)
    wf["w_in"] = lay.from_containers([gbig[0][s] for s in range(nchip)])
    for i, n in enumerate(big[1:], start=1):
        wf[n] = jnp.concatenate([gbig[i][s] for s in range(nchip)], axis=SHARD_AXIS[n])
    o = 0
    for n in small:
        sz = int(np.prod(wl[n].shape))
        a = gsmall[:, o:o + sz].reshape((nchip,) + wl[n].shape)
        wf[n] = jnp.concatenate([a[s] for s in range(nchip)], axis=SHARD_AXIS[n])
        o += sz

    loss_part, grad_x, gfull = _local_step(x, loss_target, wf, lay)
    loss = lax.psum(loss_part, ("x", "y", "c"))

    sw = wl["w_in"].shape[2]
    conts, heads = lay.containers(gfull["w_in"], nchip)
    dd = wl["w_branch_a"].shape[2]
    rows_o = wl["w_out"].shape[1]
    by_dest = lambda g, n: [lax.slice_in_dim(g, s * wl[n].shape[SHARD_AXIS[n]], (s + 1) * wl[n].shape[SHARD_AXIS[n]],
                                            axis=SHARD_AXIS[n]) if n in SHARD_AXIS else g for s in range(nchip)]
    small_names = tuple(n for n in WEIGHTS if n not in big)
    nsm = sum(int(np.prod(wl[n].shape)) for n in small_names)
    rsm = -(-nsm // (2 * HD * 8)) * 8
    pack_small = lambda parts: _flat(parts, 2 * rsm, HD).reshape(2, rsm, HD)
    small_by_dest = [by_dest(gfull[n], n) for n in small_names]
    gs = [jnp.stack(conts, axis=1),
          jnp.stack(by_dest(gfull["w_branch_a"], "w_branch_a"), axis=1),
          jnp.stack(by_dest(gfull["w_branch_b"], "w_branch_b"), axis=1),
          gfull["w_out"].reshape(DEPTH, nchip, rows_o, d),
          jnp.stack([pack_small([p[s] for p in small_by_dest]) for s in range(nchip)], axis=1)]
    gs = [g.reshape((2, nchip, -1, g.shape[-1])) for g in gs]
    my_chip = (2 * lax.axis_index("x") + lax.axis_index("y")).astype(jnp.int32)
    my_core = lax.axis_index("c").astype(jnp.int32)
    got = _swap_halves([g.astype(BF16) for g in gs[:4]] + gs[4:])
    chip_sums = [_add_cores(g, b, my_core.reshape(1)) for g, b in zip(gs, got)]
    by_chip = _scatter_chip_sums(chip_sums)
    place = jnp.stack([my_chip, my_core])
    full = _join_halves([_sum_chips(p, a, place) for p, a in zip(by_chip, chip_sums)])
    n_head = sum(jnp.where(my_chip == s, heads[s], 0) for s in range(nchip)).astype(jnp.int32).reshape(1)
    g_w_in = _uncontain(full[0].reshape(DEPTH * d, -1), n_head, sw)
    g2 = {"w_in": g_w_in, "w_branch_a": full[1].reshape(-1, dd), "w_branch_b": full[2].reshape(-1, dd),
          "w_out": full[3].reshape(-1, d), "small": full[4].reshape(2 * rsm, HD)}

    def two_d(src, n):
        if n == "small":
            return _flat([src[k] for k in small_names], 2 * rsm, HD)
        return src[n].reshape(g2[n].shape)

    outs = {}
    for n in big + ("small",):
        delta, mnew, vnew = _adamw(g2[n], two_d(wl, n), two_d(ml, n), two_d(vl, n))
        outs[n] = (g2[n], delta, mnew, vnew)
    res = [{}, {}, {}, {}]
    for i in range(4):
        for n in big:
            res[i][n] = outs[n][i].reshape(wl[n].shape)
        v, o = outs["small"][i].reshape(-1), 0
        for n in small_names:
            sz = int(np.prod(wl[n].shape))
            res[i][n] = v[o:o + sz].reshape(wl[n].shape)
            o += sz
    return (loss, grad_x, *[res[0][n] for n in WEIGHTS], *[res[1][n] for n in WEIGHTS], *[res[2][n] for n in WEIGHTS],
            *[res[3][n] for n in WEIGHTS])
```

```python
import functools

import numpy as np
import jax
import jax.numpy as jnp
from jax import lax
from jax.experimental import pallas as pl
from jax.experimental.pallas import tpu as pltpu

F32 = jnp.float32
BF16 = jnp.bfloat16
HI = lax.Precision.HIGHEST
SDS = jax.ShapeDtypeStruct

NH = 4
HD = 128
HW = NH * HD
N_META = 16
CH = 64
SUB = 16
PAD = CH - N_META
EPS = 1e-6
Q_SCALE = HD ** -0.5
DEPTH = 2
CONV_K = 4
VMEM_LIMIT = 56 * 1024 * 1024
ADAM_LR, ADAM_B1, ADAM_B2, ADAM_EPS, ADAM_WD, ADAM_STEP = 0.001, 0.9, 0.999, 1e-08, 0.01, 10
MESH = pl.DeviceIdType.MESH


def _nn(a, b):
    return jnp.dot(a, b, precision=HI, preferred_element_type=F32)


def _nt(a, b):
    return lax.dot_general(a, b, (((1,), (1,)), ((), ())), precision=HI, preferred_element_type=F32)


def _tn(a, b):
    return _nn(a.T, b)


def _scan_rows(x, group, reverse=False):
    n = x.shape[0]
    pos = lax.bitwise_and(_iota2(x.shape, 0), group - 1)
    s = 1
    while s < group:
        if reverse:
            x = x + jnp.where(pos < group - s, pltpu.roll(x, n - s, axis=0), 0.0)
        else:
            x = x + jnp.where(pos >= s, pltpu.roll(x, s, axis=0), 0.0)
        s *= 2
    return x


def _bnn(a, b):
    return jnp.dot(a.astype(BF16), b.astype(BF16), preferred_element_type=F32)


def _bnt(a, b):
    return lax.dot_general(a.astype(BF16), b.astype(BF16), (((1,), (1,)), ((), ())), preferred_element_type=F32)


def _btn(a, b):
    return lax.dot_general(a.astype(BF16), b.astype(BF16), (((0,), (0,)), ((), ())), preferred_element_type=F32)


def _hi_lo(x):
    hi = x.astype(jnp.bfloat16)
    return hi, (x - hi.astype(F32)).astype(jnp.bfloat16)


def _dot3(dims):
    def f(a, b):
        ah, al = _hi_lo(a)
        bh, bl = _hi_lo(b)
        d = lambda p, q: lax.dot_general(p, q, (dims, ((), ())), preferred_element_type=F32)
        return d(ah, bh) + (d(ah, bl) + d(al, bh))
    return f


_rnn, _rnt, _rtn = _dot3(((1,), (0,))), _dot3(((1,), (1,))), _dot3(((0,), (0,)))
_enn, _ent, _etn = _bnn, _bnt, _btn
_hnn, _hnt, _htn = _bnn, _bnt, _btn


def _rr(x):
    return x


def _sig(x):
    return jax.nn.sigmoid(x)


def _silu(x):
    return x * _sig(x)


def _dsilu(x):
    s = _sig(x)
    return s * (1.0 + x * (1.0 - s))


def _softplus(x):
    return jnp.maximum(x, 0.0) + jnp.log(1.0 + jnp.exp(-jnp.abs(x)))


def _logsig(x):
    return jnp.minimum(x, 0.0) - jnp.log(1.0 + jnp.exp(-jnp.abs(x)))


def _rs(x):
    return jnp.sum(x, axis=-1, keepdims=True)


def _params(n_axes):
    return pltpu.CompilerParams(dimension_semantics=("arbitrary",) * n_axes, vmem_limit_bytes=VMEM_LIMIT)


def _tile(n, target, mult=8):
    best = mult
    for t in range(mult, target + 1, mult):
        if n % t == 0:
            best = t
    assert n % best == 0, (n, mult)
    return best


def _ctile(pw, most=7):
    return HD * max(k for k in range(1, most + 1) if (pw // HD) % k == 0)


def _iota2(shape, axis):
    return lax.broadcasted_iota(jnp.int32, shape, axis)


class _Layout:
    def __init__(self, d):
        self.d = d
        self.wm = 2 * HW + 2 * d
        self.c_qkv = self.wm
        self.c_b = self.wm + 3 * HW
        self.c_ba = self.wm + 6 * HW
        self.pw = self.c_ba + HD
        assert self.c_b % (3 * HW) == 0
        o = 0
        segs = {}
        for name, w in (("a_q", HW), ("a_k", HW), ("a_v", HW), ("ba", 2 * NH), ("a_z", HW), ("b_q", HW), ("b_f", HW),
                        ("b_i", HW), ("b_g", HW), ("gate_a", d), ("gate_b", d)):
            segs[name] = (o, o + w)
            o += w
        self.segs = segs
        self.width = o
        self.order = ("a_z", "b_g", "gate_a", "gate_b", "a_q", "a_k", "a_v", "b_q", "b_f", "b_i", "ba")

    def to_kernel(self, w):
        parts = [w[..., self.segs[n][0]:self.segs[n][1]] for n in self.order]
        parts.append(jnp.zeros(w.shape[:-1] + (HD - 2 * NH,), w.dtype))
        return jnp.concatenate(parts, axis=-1)

    def containers(self, g, nchip):
        table, heads, cw = self.pieces(nchip)
        out = []
        for s in range(nchip):
            parts, at = [], 0
            for kcol, w, ccol in sorted(table[s], key=lambda p: p[2]):
                if ccol > at:
                    parts.append(jnp.zeros(g.shape[:-1] + (ccol - at,), g.dtype))
                parts.append(g[..., kcol:kcol + w])
                at = ccol + w
            if at < cw:
                parts.append(jnp.zeros(g.shape[:-1] + (cw - at,), g.dtype))
            out.append(jnp.concatenate(parts, axis=-1))
        return out, heads

    def pieces(self, nchip):
        off, where = 0, {}
        for n in self.order:
            where[n] = off
            off += self.segs[n][1] - self.segs[n][0]
        names = sorted(self.segs, key=lambda n: self.segs[n][0])
        sw = self.width // nchip
        cw = -(-sw // HD) * HD
        table, heads = [], []
        for s in range(nchip):
            lo, hi = s * sw, (s + 1) * sw
            pieces = []
            for n in names:
                a, b = max(lo, self.segs[n][0]), min(hi, self.segs[n][1])
                if a < b:
                    pieces.append((where[n] + a - self.segs[n][0], b - a))
            start, width = pieces[0]
            n_head = min((-start) % HD, width)
            body = ([(start + n_head, width - n_head)] if width > n_head else []) + pieces[1:]
            rows, at = [], 0
            for c, w in body:
                rows.append((c, w, at))
                at += w
            if n_head:
                rows.append((start, n_head, cw - n_head))
            table.append(rows)
            heads.append(n_head)
        return table, heads, cw

    def from_containers(self, conts):
        table, _, _ = self.pieces(len(conts))
        cut = sorted((kcol, w, s, ccol) for s, rows in enumerate(table) for kcol, w, ccol in rows)
        parts, at = [], 0
        for kcol, w, s, ccol in cut:
            assert kcol == at, (kcol, at)
            parts.append(conts[s][..., ccol:ccol + w])
            at = kcol + w
        parts.append(jnp.zeros(conts[0].shape[:-1] + (self.pw - at,), conts[0].dtype))
        return jnp.concatenate(parts, axis=-1)

    def from_kernel(self, g):
        off, where = 0, {}
        for n in self.order:
            w = self.segs[n][1] - self.segs[n][0]
            where[n] = (off, off + w)
            off += w
        names = sorted(self.segs, key=lambda n: self.segs[n][0])
        return jnp.concatenate([g[..., where[n][0]:where[n][1]] for n in names], axis=-1)


def _norm_proj_fwd(h, nw, wp):
    n, d = h.shape
    pw = wp.shape[1]
    tm, tn = _tile(n, 1408, HD), _ctile(pw)

    def body(h_ref, nw_ref, w_ref, o_ref, xt_ref, xn_ref):
        @pl.when(pl.program_id(1) == 0)
        def _():
            x = h_ref[...]
            r = lax.rsqrt(jnp.mean(x * x, axis=-1, keepdims=True) + EPS)
            xn = (x * r * nw_ref[...]).astype(BF16)
            xn_ref[...] = xn
            xt_ref[...] = xn.T

        o_ref[...] = jnp.dot(xn_ref[...], w_ref[...], preferred_element_type=F32)

    return pl.pallas_call(
        body, grid=(n // tm, pw // tn),
        in_specs=[pl.BlockSpec((tm, d), lambda i, j: (i, 0)), pl.BlockSpec((1, d), lambda i, j: (0, 0)),
                  pl.BlockSpec((d, tn), lambda i, j: (0, j))],
        out_specs=[pl.BlockSpec((tm, tn), lambda i, j: (i, j)), pl.BlockSpec((d, tm), lambda i, j: (0, i))],
        out_shape=[SDS((n, pw), F32), SDS((d, n), BF16)], scratch_shapes=[pltpu.VMEM((tm, d), BF16)],
        compiler_params=_params(2), name="norm_proj_fwd")(h, nw, wp)


def _row_valid(tm, tp, base):
    row = base + _iota2((tm, 1), 0)
    return lax.rem(row, tp) >= PAD


def _proj_bwd_dx(dproj, wp, h, nw, dhn, tp):
    n, d = h.shape
    pw = wp.shape[1]
    tm, tk = _tile16(n, 1056), _ctile(pw)
    nk = pw // tk

    def body(dp_ref, w_ref, h_ref, nw_ref, dhn_ref, dh_ref, dnw_ref, acc_ref):
        i, k = pl.program_id(0), pl.program_id(1)

        @pl.when(k == 0)
        def _():
            acc_ref[...] = jnp.zeros_like(acc_ref)

        @pl.when((i == 0) & (k == 0))
        def _():
            dnw_ref[...] = jnp.zeros_like(dnw_ref)

        valid = _row_valid(tm, tp, i * tm)
        dp = jnp.where(valid, dp_ref[...], 0.0)
        acc_ref[...] += _bnt(dp, w_ref[...])

        @pl.when(k == nk - 1)
        def _():
            x = h_ref[...]
            r = lax.rsqrt(jnp.mean(x * x, axis=-1, keepdims=True) + EPS)
            xh = x * r
            dxn = acc_ref[...]
            dnw_ref[...] += jnp.sum(dxn * xh, axis=0, keepdims=True)
            dxh = dxn * nw_ref[...]
            dh_ref[...] = dhn_ref[...] + r * (dxh - xh * jnp.mean(dxh * xh, axis=-1, keepdims=True))

    return pl.pallas_call(
        body, grid=(n // tm, nk),
        in_specs=[pl.BlockSpec((tm, tk), lambda i, k: (i, k)), pl.BlockSpec((d, tk), lambda i, k: (0, k)),
                  pl.BlockSpec((tm, d), lambda i, k: (i, 0)), pl.BlockSpec((1, d), lambda i, k: (0, 0)),
                  pl.BlockSpec((tm, d), lambda i, k: (i, 0))],
        out_specs=[pl.BlockSpec((tm, d), lambda i, k: (i, 0)), pl.BlockSpec((1, d), lambda i, k: (0, 0))],
        out_shape=[SDS((n, d), F32), SDS((1, d), F32)],
        scratch_shapes=[pltpu.VMEM((tm, d), F32)], compiler_params=_params(2), name="proj_bwd_dx")(dproj, wp, h, nw, dhn)


def _proj_bwd_dw(dproj, xt, tp):
    d, n = xt.shape
    pw = dproj.shape[1]
    tm, tn = _tile(n, 1408, HD), _ctile(pw)

    def body(dp_ref, xt_ref, dw_ref):
        i = pl.program_id(1)

        @pl.when(i == 0)
        def _():
            dw_ref[...] = jnp.zeros_like(dw_ref)

        dp = jnp.where(_row_valid(tm, tp, i * tm), dp_ref[...], 0.0)
        dw_ref[...] += jnp.dot(xt_ref[...], dp.astype(BF16), preferred_element_type=F32)

    return pl.pallas_call(
        body, grid=(pw // tn, n // tm),
        in_specs=[pl.BlockSpec((tm, tn), lambda j, i: (i, j)), pl.BlockSpec((d, tm), lambda j, i: (0, i))],
        out_specs=pl.BlockSpec((d, tn), lambda j, i: (0, j)), out_shape=SDS((d, pw), F32),
        compiler_params=_params(2), name="proj_bwd_dw")(dproj, xt)


def _conv_silu(x, w, row):
    c = x * w[CONV_K - 1:CONV_K, :]
    for k in range(1, CONV_K):
        c = c + jnp.where(row >= k, pltpu.roll(x, k, axis=0), 0.0) * w[CONV_K - 1 - k:CONV_K - k, :]
    return c


def _gdn_prep_fwd(proj, conv_w, lay, nb, tp):
    n = proj.shape[0]
    nblk = 3 * NH
    cb = lay.c_qkv // HD

    def body(p_ref, w_ref, o_ref):
        j = pl.program_id(1)
        x = p_ref[...]
        row = _iota2(x.shape, 0)
        c = _conv_silu(x, w_ref[...], row)
        s = _silu(c)
        r = lax.rsqrt(_rs(s * s) + EPS)
        scale = jnp.where(j < NH, Q_SCALE, 1.0)
        y = jnp.where(j < 2 * NH, s * r * scale, s)
        o_ref[...] = jnp.where(row >= PAD, y, 0.0)

    return pl.pallas_call(
        body, grid=(nb, nblk),
        in_specs=[pl.BlockSpec((tp, HD), lambda b, j: (b, cb + j)), pl.BlockSpec((CONV_K, HD), lambda b, j: (0, j))],
        out_specs=pl.BlockSpec((tp, HD), lambda b, j: (b, j)), out_shape=SDS((n, nblk * HD), F32),
        compiler_params=_params(2), name="gdn_prep_fwd")(proj, conv_w)


def _gdn_prep_bwd(proj, conv_w, dqkv, dproj, lay, nb, tp):
    nblk = 3 * NH
    cb = lay.c_qkv // HD

    def body(p_ref, w_ref, dy_ref, dp_in, dp_ref, dw_ref):
        j, b = pl.program_id(0), pl.program_id(1)
        x = p_ref[...]
        w = w_ref[...]
        row = _iota2(x.shape, 0)
        c = _conv_silu(x, w, row)
        s = _silu(c)
        dy = jnp.where(row >= PAD, dy_ref[...], 0.0)
        r = lax.rsqrt(_rs(s * s) + EPS)
        nh = s * r
        scale = jnp.where(j < NH, Q_SCALE, 1.0)
        ds_n = scale * r * (dy - nh * _rs(dy * nh))
        ds = jnp.where(j < 2 * NH, ds_n, dy)
        dc = ds * _dsilu(c)
        dx = dc * w[CONV_K - 1:CONV_K, :]
        dws = [jnp.sum(dc * x, axis=0, keepdims=True)]
        for k in range(1, CONV_K):
            dx = dx + jnp.where(row < tp - k, pltpu.roll(dc, tp - k, axis=0), 0.0) * w[CONV_K - 1 - k:CONV_K - k, :]
            xs = jnp.where(row >= k, pltpu.roll(x, k, axis=0), 0.0)
            dws.append(jnp.sum(dc * xs, axis=0, keepdims=True))
        dp_ref[...] = dx.astype(dp_ref.dtype)
        r4 = _iota2((CONV_K, HD), 0)
        dw = jnp.zeros((CONV_K, HD), F32)
        for k in range(CONV_K):
            dw = dw + jnp.where(r4 == CONV_K - 1 - k, dws[k], 0.0)

        @pl.when(b == 0)
        def _():
            dw_ref[...] = dw

        @pl.when(b > 0)
        def _():
            dw_ref[...] += dw

    return pl.pallas_call(
        body, grid=(nblk, nb),
        in_specs=[pl.BlockSpec((tp, HD), lambda j, b: (b, cb + j)), pl.BlockSpec((CONV_K, HD), lambda j, b: (0, j)),
                  pl.BlockSpec((tp, HD), lambda j, b: (b, j)), pl.BlockSpec(memory_space=pl.ANY)],
        out_specs=[pl.BlockSpec((tp, HD), lambda j, b: (b, cb + j)), pl.BlockSpec((CONV_K, HD), lambda j, b: (0, j))],
        out_shape=[SDS(dproj.shape, dproj.dtype), SDS((CONV_K, nblk * HD), F32)],
        input_output_aliases={3: 0}, compiler_params=_params(2), name="gdn_prep_bwd")(proj, conv_w, dqkv, dproj)


def _gate_consts():
    e = np.zeros((HD, 2 * HW), np.float32)
    s = np.zeros((2 * HW, HD), np.float32)
    for h in range(NH):
        e[h, h * HD:(h + 1) * HD] = 1.0
        e[NH + h, HW + h * HD:HW + (h + 1) * HD] = 1.0
        s[h * HD, h] = 1.0
        s[HW + h * HD, NH + h] = 1.0
    return jnp.asarray(e), jnp.asarray(s)


def _gdn_tri():
    i, j = _iota2((CH, CH), 0), _iota2((CH, CH), 1)
    return i >= j, i > j


def _each(fn, *lists):
    return [fn(*xs) for xs in zip(*lists)]


def _tri_inv(a_list, eye):
    p = [-a for a in a_list]
    t = [eye + x for x in p]
    for _ in range(5):
        p = _each(_rnn, p, p)
        tp_ = _each(_rnn, t, p)
        t = _each(lambda x, y: x + y, t, tp_)
    return t


def _gdn_chunks(args, solved=None):
    causal, strict = _gdn_tri()
    eye = jnp.where(_iota2((CH, CH), 0) == _iota2((CH, CH), 1), 1.0, 0.0)
    q, k, v, beta, g, s0 = (list(t) for t in zip(*args))
    gc = [_scan_rows(x, CH) for x in g]
    dm = [jnp.where(causal, jnp.exp(jnp.where(causal, x[:, :CH] - x[:, :CH].T, 0.0)), 0.0) for x in gc]
    ds = [jnp.where(strict, x, 0.0) for x in dm]
    kb = _each(lambda x, y: x * y, k, beta)
    kk = _each(_ent, kb, k)
    a = _each(lambda x, y: x * y, kk, ds)
    eg = [jnp.exp(x) for x in gc]
    rw = _each(lambda x, y: x * y, kb, eg)
    if solved is None:
        tinv = _tri_inv(a, eye)
        rv = _each(lambda x, y: x * y, v, beta)
        u = _each(_rnn, tinv, rv)
        w = _each(_rnn, tinv, rw)
    else:
        tinv, u, w = (list(t) for t in zip(*solved))
    ws = _each(_enn, w, s0)
    vn = _each(lambda x, y: x - y, u, ws)
    qk = _each(_ent, q, k)
    p = _each(lambda x, y: x * y, qk, dm)
    qg = _each(lambda x, y: x * y, q, eg)
    out = []
    for i in range(len(args)):
        gl = gc[i][CH - 1:CH, :]
        ek = jnp.exp(gl - gc[i])
        out.append(dict(gc=gc[i], dm=dm[i], ds=ds[i], kb=kb[i], a=a[i], tinv=tinv[i], eg=eg[i], rw=rw[i], u=u[i], w=w[i],
                        vn=vn[i], p=p[i], qg=qg[i], egl=jnp.exp(gl), ek=ek, kd=k[i] * ek))
    return out


def _gdn_gates(ba, e, alog, dtb):
    raw = _nn(ba, e)
    beta = _sig(raw[:, :HW])
    za = raw[:, HW:] + dtb
    g = -jnp.exp(alog) * _softplus(za)
    return beta, g, za


def _seqs_per_step(nb):
    return 4 if nb % 4 == 0 else (2 if nb % 2 == 0 else 1)


def _gdn_fwd(qkv, proj, e_mat, alog, dtb, lay, nb, nc):
    n = qkv.shape[0]
    tp = n // nb
    cba = lay.c_ba // HD
    gb = _seqs_per_step(nb)

    def body(x_ref, ba_ref, e_ref, al_ref, dt_ref, o_ref, so_ref, sv_ref, s_ref):
        @pl.when(pl.program_id(1) == 0)
        def _():
            s_ref[...] = jnp.zeros_like(s_ref)

        args = []
        for j in range(gb):
            beta, g, _ = _gdn_gates(ba_ref[j], e_ref[...], al_ref[...], dt_ref[...])
            for h in range(NH):
                hs = slice(h * HD, (h + 1) * HD)
                args.append((x_ref[j, :, hs], x_ref[j, :, HW + h * HD:HW + (h + 1) * HD],
                             x_ref[j, :, 2 * HW + h * HD:2 * HW + (h + 1) * HD], beta[:, hs], g[:, hs], s_ref[j, h]))
        cs = _gdn_chunks(args)
        s0s = [a[5] for a in args]
        o1 = _each(lambda c, s0: _enn(c["qg"], s0), cs, s0s)
        o2 = [_enn(c["p"], c["vn"]) for c in cs]
        upd = [_etn(c["kd"], c["vn"]) for c in cs]
        res = [(o1[i] + o2[i], s0s[i] * cs[i]["egl"] + upd[i]) for i in range(len(cs))]
        zero = jnp.zeros((CH, HD - CH), F32)
        for j in range(gb):
            for h in range(NH):
                c = cs[j * NH + h]
                so_ref[j, h] = args[j * NH + h][5]
                sv_ref[j, h] = jnp.concatenate([c["u"], c["w"], c["tinv"], zero], axis=-1)
                s_ref[j, h] = res[j * NH + h][1]
            o_ref[j] = jnp.concatenate([res[j * NH + h][0] for h in range(NH)], axis=-1)

    o, st, sv = pl.pallas_call(
        body, grid=(nb // gb, nc),
        in_specs=[pl.BlockSpec((gb, CH, 3 * HW), lambda b, c: (b, c, 0)), pl.BlockSpec((gb, CH, HD), lambda b, c: (b, c, cba)),
                  pl.BlockSpec((HD, 2 * HW), lambda b, c: (0, 0)), pl.BlockSpec((1, HW), lambda b, c: (0, 0)),
                  pl.BlockSpec((1, HW), lambda b, c: (0, 0))],
        out_specs=[pl.BlockSpec((gb, CH, HW), lambda b, c: (b, c, 0)),
                   pl.BlockSpec((gb, None, NH, HD, HD), lambda b, c: (b, c, 0, 0, 0)),
                   pl.BlockSpec((gb, None, NH, CH, 3 * HD), lambda b, c: (b, c, 0, 0, 0))],
        out_shape=[SDS((nb, tp, HW), F32), SDS((nb, nc, NH, HD, HD), F32), SDS((nb, nc, NH, CH, 3 * HD), F32)],
        scratch_shapes=[pltpu.VMEM((gb, NH, HD, HD), F32)], compiler_params=_params(2), name="gdn_fwd")(
            qkv.reshape(nb, tp, 3 * HW), proj.reshape(nb, tp, -1), e_mat, alog, dtb)
    return o.reshape(n, HW), st, sv


def _gdn_bwd(qkv, proj, e_mat, s_mat, alog, dtb, states, solved, do, dproj, lay, nb, nc):
    n = qkv.shape[0]
    tp = n // nb
    cba = lay.c_ba // HD
    gb = _seqs_per_step(nb)

    def body(x_ref, ba_ref, e_ref, sm_ref, al_ref, dt_ref, st_ref, sv_ref, do_ref, dp_in, dx_ref, dba_ref, acc_ref, ds_ref):
        ci = pl.program_id(1)

        @pl.when(ci == 0)
        def _():
            ds_ref[...] = jnp.zeros_like(ds_ref)

        @pl.when((ci == 0) & (pl.program_id(0) == 0))
        def _():
            acc_ref[...] = jnp.zeros_like(acc_ref)

        causal, strict = _gdn_tri()
        alog = al_ref[...]
        row = _iota2((CH, 1), 0)
        valid = (row >= PAD) | (ci < nc - 1)
        last = row == CH - 1
        gates = [_gdn_gates(ba_ref[j], e_ref[...], alog, dt_ref[...]) for j in range(gb)]
        args, do, ds1, solved = [], [], [], []
        for j in range(gb):
            beta, g, _ = gates[j]
            for h in range(NH):
                hs = slice(h * HD, (h + 1) * HD)
                args.append((x_ref[j, :, hs], x_ref[j, :, HW + h * HD:HW + (h + 1) * HD],
                             x_ref[j, :, 2 * HW + h * HD:2 * HW + (h + 1) * HD], beta[:, hs], g[:, hs], st_ref[j, h]))
                do.append(do_ref[j, :, hs])
                ds1.append(ds_ref[j, h])
                solved.append((sv_ref[j, h, :, 2 * HD:2 * HD + CH], sv_ref[j, h, :, 0:HD], sv_ref[j, h, :, HD:2 * HD]))
        q, k, v, bh, _, s0 = (list(t) for t in zip(*args))
        cs = _gdn_chunks(args, solved)
        get = lambda name: [c[name] for c in cs]
        mul = lambda x, y: x * y
        add = lambda x, y: x + y
        dvn = _each(add, _each(_etn, get("p"), do), _each(_enn, get("kd"), ds1))
        dqg = _each(_ent, do, s0)
        dp = [jnp.where(causal, x, 0.0) for x in _each(_ent, do, get("vn"))]
        dkd = _each(_ent, get("vn"), ds1)
        dw = [-x for x in _each(_ent, dvn, s0)]
        ds_a = _each(_etn, get("qg"), do)
        ds_b = _each(_etn, get("w"), dvn)
        ds_new = [ds_a[i] - ds_b[i] + ds1[i] * cs[i]["egl"] for i in range(len(cs))]
        drv = _each(_rtn, get("tinv"), dvn)
        drw = _each(_rtn, get("tinv"), dw)
        da_1 = _each(_rnt, drv, get("u"))
        da_2 = _each(_rnt, drw, get("w"))
        da = [jnp.where(strict, -(x + y), 0.0) for x, y in zip(da_1, da_2)]
        m = [da[i] * cs[i]["a"] + dp[i] * cs[i]["p"] for i in range(len(cs))]
        dkk = _each(mul, da, get("ds"))
        dqk = _each(mul, dp, get("dm"))
        dq = _each(add, _each(_enn, dqk, k), _each(mul, dqg, get("eg")))
        dkb = _each(add, _each(_enn, dkk, k), _each(mul, drw, get("eg")))
        dk_1 = _each(_etn, dqk, q)
        dk_2 = _each(_etn, dkk, get("kb"))
        dk = [dk_1[i] + dk_2[i] + dkd[i] * cs[i]["ek"] + dkb[i] * bh[i] for i in range(len(cs))]
        dv = _each(mul, drv, bh)
        dbeta, dg = [], []
        for i, c in enumerate(cs):
            dbeta.append(_rs(drv[i] * v[i]) + _rs(dkb[i] * k[i]) + jnp.zeros((CH, HD), F32))
            t_kd = _rs(dkd[i] * c["kd"])
            dgc = _rs(m[i]) - _rs(m[i].T) + _rs(dqg[i] * c["qg"]) + _rs(drw[i] * c["rw"]) - t_kd
            tail = jnp.sum(t_kd, axis=0, keepdims=True) + c["egl"] * jnp.sum(_rs(s0[i] * ds1[i]), axis=0, keepdims=True)
            dgc = dgc + jnp.where(last, tail, 0.0)
            dg.append(_scan_rows(dgc + jnp.zeros((CH, HD), F32), CH, reverse=True))
        r8 = _iota2((8, HW), 0)
        upd = jnp.zeros((8, HW), F32)
        for j in range(gb):
            sl = slice(j * NH, (j + 1) * NH)
            beta, g, za = gates[j]
            for h in range(NH):
                ds_ref[j, h] = ds_new[j * NH + h]
            dx_ref[j] = jnp.concatenate(dq[sl] + dk[sl] + dv[sl], axis=-1)
            dbeta_j = jnp.where(valid, jnp.concatenate(dbeta[sl], axis=-1), 0.0)
            dg_j = jnp.where(valid, jnp.concatenate(dg[sl], axis=-1), 0.0)
            draw_b = dbeta_j * beta * (1.0 - beta)
            draw_a = dg_j * (-jnp.exp(alog)) * _sig(za)
            dba_ref[j] = _nn(jnp.concatenate([draw_b, draw_a], axis=-1), sm_ref[...]).astype(dba_ref.dtype)
            upd = upd + jnp.where(r8 == 0, jnp.sum(dg_j * g, axis=0, keepdims=True), 0.0) + jnp.where(
                r8 == 1, jnp.sum(draw_a, axis=0, keepdims=True), 0.0)
        acc_ref[...] += upd

    rc = lambda c: nc - 1 - c
    dqkv, dproj3, acc = pl.pallas_call(
        body, grid=(nb // gb, nc),
        in_specs=[pl.BlockSpec((gb, CH, 3 * HW), lambda b, c: (b, rc(c), 0)), pl.BlockSpec((gb, CH, HD), lambda b, c: (b, rc(c), cba)),
                  pl.BlockSpec((HD, 2 * HW), lambda b, c: (0, 0)), pl.BlockSpec((2 * HW, HD), lambda b, c: (0, 0)),
                  pl.BlockSpec((1, HW), lambda b, c: (0, 0)), pl.BlockSpec((1, HW), lambda b, c: (0, 0)),
                  pl.BlockSpec((gb, None, NH, HD, HD), lambda b, c: (b, rc(c), 0, 0, 0)),
                  pl.BlockSpec((gb, None, NH, CH, 3 * HD), lambda b, c: (b, rc(c), 0, 0, 0)),
                  pl.BlockSpec((gb, CH, HW), lambda b, c: (b, rc(c), 0)), pl.BlockSpec(memory_space=pl.ANY)],
        out_specs=[pl.BlockSpec((gb, CH, 3 * HW), lambda b, c: (b, rc(c), 0)), pl.BlockSpec((gb, CH, HD), lambda b, c: (b, rc(c), cba)),
                   pl.BlockSpec((8, HW), lambda b, c: (0, 0))],
        out_shape=[SDS((nb, tp, 3 * HW), F32), SDS((nb, tp, dproj.shape[1]), dproj.dtype), SDS((8, HW), F32)],
        input_output_aliases={9: 1},
        scratch_shapes=[pltpu.VMEM((gb, NH, HD, HD), F32)], compiler_params=_params(2), name="gdn_bwd")(
            qkv.reshape(nb, tp, 3 * HW), proj.reshape(nb, tp, -1), e_mat, s_mat, alog, dtb, states, solved, do.reshape(nb, tp, HW),
            dproj.reshape(nb, tp, -1))
    return dqkv.reshape(n, 3 * HW), dproj3.reshape(dproj.shape), acc


def _hgrn_inputs(zq, zf, lb):
    sg = _sig(zf)
    sgn = _sig(-zf)
    pos = lb > 0.0
    lbp = jnp.where(pos, lb, 0.0)
    fpos = lbp + (1.0 - lbp) * sg
    lf = jnp.where(pos, jnp.log(jnp.where(pos, fpos, 1.0)), _logsig(zf))
    k = (1.0 - lbp) * sgn
    q = _silu(zq) * Q_SCALE
    return q, k, lf, sg, sgn, pos, lbp, fpos


def _hgrn_consts():
    i3, j3 = _iota2((SUB, SUB, HD), 0), _iota2((SUB, SUB, HD), 1)
    return i3 >= j3


def _sum_j(x):
    return jnp.sum(x.reshape(SUB, SUB, HD), axis=1)


def _sum_i(x):
    return jnp.sum(x.reshape(SUB, SUB, HD), axis=0)


def _pairs(a, b):
    return (a[:, None, :] * b[None, :, :]).reshape(SUB * SUB, HD)


def _hgrn_sub(q, k, v, bc, st, consts):
    mask3 = consts
    bl = bc[SUB - 1:SUB, :]
    p3 = jnp.where(mask3, jnp.exp(jnp.where(mask3, bc[:, None, :] - bc[None, :, :], 0.0)), 0.0).reshape(SUB * SUB, HD)
    x = _pairs(q, k) * p3
    srep = _rs(x)
    vt = jnp.broadcast_to(v[None, :, :], (SUB, SUB, HD)).reshape(SUB * SUB, HD)
    eb = jnp.exp(bc)
    qe = q * eb
    o = _hnt(qe, st) + _sum_j(_rr(srep) * _rr(vt))
    ek = jnp.exp(bl - bc)
    kd = k * ek
    ebl = jnp.exp(bl)
    st1 = st * ebl + _htn(v, kd)
    return o, st1, dict(bc=bc, p3=p3, srep=srep, vt=vt, eb=eb, qe=qe, ek=ek, kd=kd, ebl=ebl)


def _hgrn_fwd(proj, lb, lay, nb, nc):
    n = proj.shape[0]
    tp = n // nb
    cbb = lay.c_b // (3 * HW)
    gb = _seqs_per_step(nb)

    def body(z_ref, lb_ref, o_ref, so_ref, s_ref):
        @pl.when(pl.program_id(1) == 0)
        def _():
            s_ref[...] = jnp.zeros_like(s_ref)

        consts = _hgrn_consts()
        for j in range(gb):
            outs = []
            for h in range(NH):
                hs = slice(h * HD, (h + 1) * HD)
                q, k, lf = _hgrn_inputs(z_ref[j, :, hs], z_ref[j, :, HW + h * HD:HW + (h + 1) * HD], lb_ref[:, hs])[:3]
                v = z_ref[j, :, 2 * HW + h * HD:2 * HW + (h + 1) * HD]
                st = s_ref[j, h]
                so_ref[j, h] = st
                bc = _scan_rows(lf, SUB)
                oh = []
                for s in range(CH // SUB):
                    rs = slice(s * SUB, (s + 1) * SUB)
                    o, st, _ = _hgrn_sub(q[rs], k[rs], v[rs], bc[rs], st, consts)
                    oh.append(o)
                s_ref[j, h] = st
                outs.append(jnp.concatenate(oh, axis=0))
            o_ref[j] = jnp.concatenate(outs, axis=-1)

    o, st = pl.pallas_call(
        body, grid=(nb // gb, nc),
        in_specs=[pl.BlockSpec((gb, CH, 3 * HW), lambda b, c: (b, c, cbb)), pl.BlockSpec((1, HW), lambda b, c: (0, 0))],
        out_specs=[pl.BlockSpec((gb, CH, HW), lambda b, c: (b, c, 0)),
                   pl.BlockSpec((gb, None, NH, HD, HD), lambda b, c: (b, c, 0, 0, 0))],
        out_shape=[SDS((nb, tp, HW), F32), SDS((nb, nc, NH, HD, HD), F32)],
        scratch_shapes=[pltpu.VMEM((gb, NH, HD, HD), F32)], compiler_params=_params(2), name="hgrn_fwd")(
            proj.reshape(nb, tp, -1), lb)
    return o.reshape(n, HW), st


def _hgrn_bwd(proj, lb, states, do, dproj, lay, nb, nc):
    n = proj.shape[0]
    tp = n // nb
    cbb = lay.c_b // (3 * HW)
    nsub = CH // SUB
    gb = _seqs_per_step(nb)

    def body(z_ref, lb_ref, st_ref, do_ref, dp_in, dz_ref, acc_ref, ds_ref):
        ci = pl.program_id(1)

        @pl.when(ci == 0)
        def _():
            ds_ref[...] = jnp.zeros_like(ds_ref)

        @pl.when((ci == 0) & (pl.program_id(0) == 0))
        def _():
            acc_ref[...] = jnp.zeros_like(acc_ref)

        upd = jnp.zeros((8, HW), F32)
        for j in range(gb):
            upd = upd + one_seq(j, ci, z_ref, lb_ref, st_ref, do_ref, dz_ref, ds_ref)
        acc_ref[...] += upd

    def one_seq(j, ci, z_ref, lb_ref, st_ref, do_ref, dz_ref, ds_ref):
        consts = _hgrn_consts()
        row = _iota2((CH, 1), 0)
        valid = (row >= PAD) | (ci < nc - 1)
        lastrow = _iota2((SUB, 1), 0) == SUB - 1
        dzq, dzf, dzi, dlbs = [], [], [], []
        for h in range(NH):
            hs = slice(h * HD, (h + 1) * HD)
            zq, zf = z_ref[j, :, hs], z_ref[j, :, HW + h * HD:HW + (h + 1) * HD]
            q, k, lf, sg, sgn, pos, lbp, fpos = _hgrn_inputs(zq, zf, lb_ref[:, hs])
            v = z_ref[j, :, 2 * HW + h * HD:2 * HW + (h + 1) * HD]
            doh = do_ref[j, :, hs]
            sts, fw = [st_ref[j, h]], []
            bc = _scan_rows(lf, SUB)
            for s in range(nsub):
                rs = slice(s * SUB, (s + 1) * SUB)
                _, st1, c = _hgrn_sub(q[rs], k[rs], v[rs], bc[rs], sts[-1], consts)
                sts.append(st1)
                fw.append(c)
            dst = ds_ref[j, h]
            dq_l, dk_l, dv_l, dlf_l = [None] * nsub, [None] * nsub, [None] * nsub, [None] * nsub
            for s in reversed(range(nsub)):
                rs = slice(s * SUB, (s + 1) * SUB)
                c, st = fw[s], sts[s]
                qs, ks, vs, dos = q[rs], k[rs], v[rs], doh[rs]
                dqe = _hnn(dos, st)
                dkd = _hnn(vs, dst)
                dsrep = _rs(_pairs(_rr(dos), _rr(vs)))
                w = dsrep * c["p3"]
                kt = jnp.broadcast_to(ks[None, :, :], (SUB, SUB, HD)).reshape(SUB * SUB, HD)
                qt = jnp.broadcast_to(qs[:, None, :], (SUB, SUB, HD)).reshape(SUB * SUB, HD)
                dq_i = _sum_j(w * kt)
                dk_i = _sum_i(w * qt)
                dot = jnp.broadcast_to(_rr(dos)[:, None, :], (SUB, SUB, HD)).reshape(SUB * SUB, HD)
                dvv = _sum_i(_rr(c["srep"]) * dot) + _hnt(c["kd"], dst)
                t_kd = dkd * c["kd"]
                dbc = dqe * c["qe"] - t_kd + qs * dq_i - ks * dk_i
                tail = jnp.sum(t_kd, axis=0, keepdims=True) + c["ebl"] * jnp.sum(st * dst, axis=0, keepdims=True)
                dbc = dbc + jnp.where(lastrow, tail, 0.0)
                dlf_l[s] = dbc
                dq_l[s] = dq_i + dqe * c["eb"]
                dk_l[s] = dk_i + dkd * c["ek"]
                dv_l[s] = dvv
                dst = _htn(dos, c["qe"]) + dst * c["ebl"]
            ds_ref[j, h] = dst
            dq, dk, dv, dbc = (jnp.concatenate(t, axis=0) for t in (dq_l, dk_l, dv_l, dlf_l))
            dlf = _scan_rows(dbc, SUB, reverse=True)
            dlft = dlf - dk * (1.0 - k)
            dlf_dz = jnp.where(pos, (1.0 - lbp) * sg * sgn / jnp.where(pos, fpos, 1.0), sgn)
            dlf_dlb = jnp.where(pos, sgn / jnp.where(pos, fpos, 1.0), 0.0)
            dzq.append(dq * Q_SCALE * _dsilu(zq))
            dzf.append(dlft * dlf_dz)
            dzi.append(dv)
            dlbs.append(jnp.sum(jnp.where(valid, dlft * dlf_dlb, 0.0), axis=0, keepdims=True))
        dz_ref[j] = jnp.concatenate(dzq + dzf + dzi, axis=-1).astype(dz_ref.dtype)
        return jnp.where(_iota2((8, HW), 0) == 0, jnp.concatenate(dlbs, axis=-1), 0.0)

    rc = lambda c: nc - 1 - c
    dproj3, acc = pl.pallas_call(
        body, grid=(nb // gb, nc),
        in_specs=[pl.BlockSpec((gb, CH, 3 * HW), lambda b, c: (b, rc(c), cbb)), pl.BlockSpec((1, HW), lambda b, c: (0, 0)),
                  pl.BlockSpec((gb, None, NH, HD, HD), lambda b, c: (b, rc(c), 0, 0, 0)),
                  pl.BlockSpec((gb, CH, HW), lambda b, c: (b, rc(c), 0)), pl.BlockSpec(memory_space=pl.ANY)],
        out_specs=[pl.BlockSpec((gb, CH, 3 * HW), lambda b, c: (b, rc(c), cbb)), pl.BlockSpec((8, HW), lambda b, c: (0, 0))],
        out_shape=[SDS((nb, tp, dproj.shape[1]), dproj.dtype), SDS((8, HW), F32)],
        input_output_aliases={4: 0},
        scratch_shapes=[pltpu.VMEM((gb, NH, HD, HD), F32)], compiler_params=_params(2), name="hgrn_bwd")(
            proj.reshape(nb, tp, -1), lb, states, do.reshape(nb, tp, HW), dproj.reshape(nb, tp, -1))
    return dproj3.reshape(dproj.shape), acc


def _gated_norm(o, z, gamma):
    ys, ns, rs = [], [], []
    for h in range(NH):
        hs = slice(h * HD, (h + 1) * HD)
        oh = o[:, hs]
        r = lax.rsqrt(jnp.mean(oh * oh, axis=-1, keepdims=True) + EPS)
        nh = oh * r
        ys.append(nh * gamma * _silu(z[:, hs]))
        ns.append(nh)
        rs.append(r)
    return jnp.concatenate(ys, axis=-1), ns, rs


def _merge_fwd(h, oa, ob, proj, ga, gb, wa, wb, wo, lay):
    n, d = h.shape
    tm = _tile(n, 384)
    wm = lay.wm

    def body(h_ref, oa_ref, ob_ref, p_ref, ga_ref, gb_ref, wa_ref, wb_ref, wo_ref, out_ref):
        ya, _, _ = _gated_norm(oa_ref[...], p_ref[:, 0:HW], ga_ref[...])
        yb, _, _ = _gated_norm(ob_ref[...], p_ref[:, HW:2 * HW], gb_ref[...])
        ya2 = _bnn(ya, wa_ref[...])
        yb2 = _bnn(yb, wb_ref[...])
        mixed = _sig(p_ref[:, 2 * HW:2 * HW + d]) * ya2 + _sig(p_ref[:, 2 * HW + d:2 * HW + 2 * d]) * yb2
        out_ref[...] = h_ref[...] + _bnn(mixed, wo_ref[...])

    full = lambda shape: pl.BlockSpec(shape, lambda i: (0, 0))
    return pl.pallas_call(
        body, grid=(n // tm,),
        in_specs=[pl.BlockSpec((tm, d), lambda i: (i, 0)), pl.BlockSpec((tm, HW), lambda i: (i, 0)),
                  pl.BlockSpec((tm, HW), lambda i: (i, 0)), pl.BlockSpec((tm, wm), lambda i: (i, 0)),
                  full((1, HD)), full((1, HD)), full((HW, d)), full((HW, d)), full((d, d))],
        out_specs=pl.BlockSpec((tm, d), lambda i: (i, 0)), out_shape=SDS((n, d), F32),
        compiler_params=_params(1), name="merge_fwd")(h, oa, ob, proj, ga, gb, wa, wb, wo)


def _gated_norm_bwd(dy, o, z, gamma):
    dos, dzs = [], []
    dgam = jnp.zeros((1, HD), F32)
    for h in range(NH):
        hs = slice(h * HD, (h + 1) * HD)
        oh, zh, dyh = o[:, hs], z[:, hs], dy[:, hs]
        r = lax.rsqrt(jnp.mean(oh * oh, axis=-1, keepdims=True) + EPS)
        nh = oh * r
        dzs.append(dyh * nh * gamma * _dsilu(zh))
        dng = dyh * _silu(zh)
        dgam = dgam + jnp.sum(dng * nh, axis=0, keepdims=True)
        dn = dng * gamma
        dos.append(r * (dn - nh * jnp.mean(dn * nh, axis=-1, keepdims=True)))
    return jnp.concatenate(dos, axis=-1), jnp.concatenate(dzs, axis=-1), dgam


def _merge_bwd(dhn, oa, ob, proj, ga, gb, wa, wb, wo, lay, tp):
    n, d = dhn.shape
    tm = _tile(n, 256)
    wm = lay.wm

    def body(dh_ref, oa_ref, ob_ref, p_ref, ga_ref, gb_ref, wa_ref, wb_ref, wo_ref,
             dp_ref, doa_ref, dob_ref, dwa_ref, dwb_ref, dwo_ref, dga_ref, dgb_ref):
        i = pl.program_id(0)

        @pl.when(i == 0)
        def _():
            for r in (dwa_ref, dwb_ref, dwo_ref, dga_ref, dgb_ref):
                r[...] = jnp.zeros_like(r)

        dh = jnp.where(_row_valid(tm, tp, i * tm), dh_ref[...], 0.0)
        oa, ob = oa_ref[...], ob_ref[...]
        za, zb = p_ref[:, 0:HW], p_ref[:, HW:2 * HW]
        gta, gtb = p_ref[:, 2 * HW:2 * HW + d], p_ref[:, 2 * HW + d:2 * HW + 2 * d]
        ya, _, _ = _gated_norm(oa, za, ga_ref[...])
        yb, _, _ = _gated_norm(ob, zb, gb_ref[...])
        ya2 = _bnn(ya, wa_ref[...])
        yb2 = _bnn(yb, wb_ref[...])
        sa, sb = _sig(gta), _sig(gtb)
        mixed = sa * ya2 + sb * yb2
        dmixed = _bnt(dh, wo_ref[...])
        dwo_ref[...] += _btn(mixed, dh)
        dya2 = dmixed * sa
        dyb2 = dmixed * sb
        dwa_ref[...] += _btn(ya, dya2)
        dwb_ref[...] += _btn(yb, dyb2)
        doa, dza, dga = _gated_norm_bwd(_bnt(dya2, wa_ref[...]), oa, za, ga_ref[...])
        dob, dzb, dgb = _gated_norm_bwd(_bnt(dyb2, wb_ref[...]), ob, zb, gb_ref[...])
        dga_ref[...] += dga
        dgb_ref[...] += dgb
        doa_ref[...] = doa
        dob_ref[...] = dob
        dt = dp_ref.dtype
        dp_ref[:, 0:HW] = dza.astype(dt)
        dp_ref[:, HW:2 * HW] = dzb.astype(dt)
        dp_ref[:, 2 * HW:2 * HW + d] = (dmixed * ya2 * sa * (1.0 - sa)).astype(dt)
        dp_ref[:, 2 * HW + d:2 * HW + 2 * d] = (dmixed * yb2 * sb * (1.0 - sb)).astype(dt)

    full = lambda shape: pl.BlockSpec(shape, lambda i: (0, 0))
    rows = lambda w: pl.BlockSpec((tm, w), lambda i: (i, 0))
    return pl.pallas_call(
        body, grid=(n // tm,),
        in_specs=[rows(d), rows(HW), rows(HW), rows(wm), full((1, HD)), full((1, HD)), full((HW, d)), full((HW, d)), full((d, d))],
        out_specs=[rows(wm), rows(HW), rows(HW), full((HW, d)), full((HW, d)), full((d, d)), full((1, HD)), full((1, HD))],
        out_shape=[SDS((n, lay.pw), BF16), SDS((n, HW), F32), SDS((n, HW), F32), SDS((HW, d), F32), SDS((HW, d), F32),
                   SDS((d, d), F32), SDS((1, HD), F32), SDS((1, HD), F32)],
        compiler_params=_params(1), name="merge_bwd")(dhn, oa, ob, proj, ga, gb, wa, wb, wo)


def _loss_head(h, target, fw, nb, tp):
    n, d = h.shape
    tr = _tile(tp, 768)
    nr = tp // tr

    def body(h_ref, t_ref, fw_ref, lp_ref, dh_ref, dfw_ref):
        b, i = pl.program_id(0), pl.program_id(1)

        @pl.when((b == 0) & (i == 0))
        def _():
            dfw_ref[...] = jnp.zeros_like(dfw_ref)

        x = h_ref[...]
        r = lax.rsqrt(jnp.mean(x * x, axis=-1, keepdims=True) + EPS)
        xh = x * r
        live = i * tr + _iota2((tr, 1), 0) >= CH
        err = jnp.where(live, xh * fw_ref[...] - t_ref[...], 0.0)
        lp_ref[...] = jnp.zeros_like(lp_ref) + 0.5 * jnp.sum(_rs(err * err), axis=0, keepdims=True) / d
        dy = err / d
        dfw_ref[...] += jnp.sum(dy * xh, axis=0, keepdims=True)
        dxh = dy * fw_ref[...]
        dh_ref[...] = r * (dxh - xh * jnp.mean(dxh * xh, axis=-1, keepdims=True))

    rows = pl.BlockSpec((tr, d), lambda b, i: (b * nr + i, 0))
    return pl.pallas_call(
        body, grid=(nb, nr), in_specs=[rows, rows, pl.BlockSpec((1, d), lambda b, i: (0, 0))],
        out_specs=[pl.BlockSpec((8, HD), lambda b, i: (b * nr + i, 0)), rows, pl.BlockSpec((1, d), lambda b, i: (0, 0))],
        out_shape=[SDS((nb * nr * 8, HD), F32), SDS((n, d), F32), SDS((1, d), F32)],
        compiler_params=_params(2), name="loss_head")(h, target, fw)


def _lb_fwd(lb):
    def body(x_ref, o_ref):
        x = x_ref[...]
        mx = jnp.max(x, axis=0, keepdims=True)
        e = jnp.exp(x - mx)
        sm = e / jnp.sum(e, axis=0, keepdims=True)
        run = jnp.zeros((1, HW), F32)
        for l in range(DEPTH):
            run = run + sm[l:l + 1, :]
            o_ref[l:l + 1, :] = run - sm[0:1, :]

    return pl.pallas_call(body, out_shape=SDS(lb.shape, F32), name="lb_fwd")(lb)


def _lb_bwd(lb, dlb_all):
    def body(x_ref, d_ref, o_ref):
        x = x_ref[...]
        dl = d_ref[...]
        mx = jnp.max(x, axis=0, keepdims=True)
        e = jnp.exp(x - mx)
        sm = e / jnp.sum(e, axis=0, keepdims=True)
        tot = jnp.sum(dl, axis=0, keepdims=True)
        dsm = []
        run = tot
        for l in range(DEPTH):
            dsm.append(run - (tot if l == 0 else 0.0))
            run = run - dl[l:l + 1, :]
        inner = sum(sm[l:l + 1, :] * dsm[l] for l in range(DEPTH))
        for l in range(DEPTH):
            o_ref[l:l + 1, :] = sm[l:l + 1, :] * (dsm[l] - inner)

    return pl.pallas_call(body, out_shape=SDS(lb.shape, F32), name="lb_bwd")(lb, dlb_all)


def _adamw(g, w, m, v):
    r, c = g.shape
    tr = _tile(r, 264)
    c1 = 1.0 / (1.0 - ADAM_B1 ** ADAM_STEP)
    c2 = 1.0 / (1.0 - ADAM_B2 ** ADAM_STEP)

    def body(g_ref, w_ref, m_ref, v_ref, d_ref, mo_ref, vo_ref):
        gg = g_ref[...]
        mn = ADAM_B1 * m_ref[...] + (1.0 - ADAM_B1) * gg
        vn = ADAM_B2 * v_ref[...] + (1.0 - ADAM_B2) * gg * gg
        d_ref[...] = -ADAM_LR * ((mn * c1) / (jnp.sqrt(vn * c2) + ADAM_EPS) + ADAM_WD * w_ref[...])
        mo_ref[...] = mn
        vo_ref[...] = vn

    spec = pl.BlockSpec((tr, c), lambda i: (i, 0))
    return pl.pallas_call(body, grid=(r // tr,), in_specs=[spec] * 4, out_specs=[spec] * 3, out_shape=[SDS(g.shape, F32)] * 3,
                          compiler_params=_params(1), name="adamw")(g, w, m, v)


def _tile16(n, target):
    return _tile(n // 2, target // 2) * 2 if n % 16 == 0 else _tile(n, target)


def _add_cores(g, got, core):
    k, r, c = got.shape
    tr = _tile16(r, 264)

    def body(c_ref, a_ref, b_ref, o_ref):
        o_ref[...] = (a_ref[...] + b_ref[...].astype(F32)).astype(o_ref.dtype)

    spec = pl.BlockSpec((None, tr, c), lambda s, i, cr: (s, i, 0))
    return pl.pallas_call(
        body, grid_spec=pltpu.PrefetchScalarGridSpec(
            num_scalar_prefetch=1, grid=(k, r // tr),
            in_specs=[pl.BlockSpec((None, None, tr, c), lambda s, i, cr: (cr[0], s, i, 0)), spec], out_specs=spec),
        out_shape=SDS(got.shape, got.dtype), compiler_params=_params(2), name="add_cores")(core, g, got)


def _sum_chips(parts, own, chip, core):
    k, r, c = parts.shape
    tr = _tile16(r, 264)

    def body(chip_ref, core_ref, *refs):
        part_refs, own_ref, o_ref = refs[:k], refs[k], refs[k + 1]
        mine = own_ref[...].astype(F32)
        acc = None
        for s in range(k):
            term = jnp.where(chip_ref[0] == s, mine, part_refs[s][...].astype(F32))
            acc = term if acc is None else acc + term
        o_ref[...] = acc

    def other(s):
        return pl.BlockSpec((None, tr, c), lambda i, ch, co: (jnp.where(ch[0] == s, (s + 1) % k, s), i, 0))

    return pl.pallas_call(
        body, grid_spec=pltpu.PrefetchScalarGridSpec(
            num_scalar_prefetch=2, grid=(r // tr,),
            in_specs=[other(s) for s in range(k)] + [pl.BlockSpec((None, tr, c), lambda i, ch, co: (ch[0], i, 0))],
            out_specs=pl.BlockSpec((None, tr, c), lambda i, ch, co: (co[0], i, 0))),
        out_shape=SDS((2, r, c), F32), compiler_params=_params(1), name="sum_chips")(chip, core, *([parts] * k), own)


def _meta_grad(dh, nb, nc):
    d = dh.shape[1]

    def body(x_ref, o_ref):
        @pl.when(pl.program_id(0) == 0)
        def _():
            o_ref[...] = jnp.zeros_like(o_ref)

        o_ref[...] += x_ref[PAD:CH, :]

    return pl.pallas_call(body, grid=(nb,), in_specs=[pl.BlockSpec((CH, d), lambda b: (b * nc, 0))],
                          out_specs=pl.BlockSpec((N_META, d), lambda b: (0, 0)), out_shape=SDS((N_META, d), F32),
                          compiler_params=_params(1), name="meta_grad")(dh)


ANY = pl.BlockSpec(memory_space=pl.ANY)


def _place():
    x, y, c = lax.axis_index("x"), lax.axis_index("y"), lax.axis_index("c")
    chips = [(1 - x, y), (x, 1 - y), (1 - x, 1 - y)]
    return x, y, c, chips


def _remote(src, dst, send_sems, recv_sems, k, to):
    return pltpu.make_async_remote_copy(src_ref=src, dst_ref=dst, send_sem=send_sems.at[k], recv_sem=recv_sems.at[k],
                                        device_id=to, device_id_type=MESH)


def _gather_weights(pbs, ps):
    nt = len(pbs)

    def body(*refs):
        pb_refs, ps_ref, gb_refs, gs_ref = refs[:nt], refs[nt], refs[nt + 1:2 * nt + 1], refs[2 * nt + 1]
        send_sems, recv_sems, local_sems = refs[2 * nt + 2:]
        x, y, c, chips = _place()
        s = 2 * x + y
        sib = (x, y, 1 - c)
        l1 = pltpu.make_async_copy(ps_ref, gs_ref.at[s], local_sems.at[0])
        l1.start()
        sends = []
        for k, (px, py) in enumerate(chips):
            for t in range(nt):
                sends.append(_remote(pb_refs[t].at[c], gb_refs[t].at[s, c], send_sems, recv_sems, 6 * t + k, (px, py, c)))
            sends.append(_remote(ps_ref, gs_ref.at[s], send_sems, recv_sems, 6 * nt + k, (px, py, c)))
        for cp in sends:
            cp.start()
        for k, (px, py) in enumerate(chips):
            sk = 2 * px + py
            for t in range(nt):
                _remote(pb_refs[t].at[c], gb_refs[t].at[sk, c], send_sems, recv_sems, 6 * t + k, sib).wait_recv()
                fwd = _remote(gb_refs[t].at[sk, c], gb_refs[t].at[sk, c], send_sems, recv_sems, 6 * t + 3 + k, sib)
                fwd.start()
                sends.append(fwd)
        for k, (px, py) in enumerate(chips):
            sk = 2 * px + py
            for t in range(nt):
                _remote(pb_refs[t].at[c], gb_refs[t].at[sk, 1 - c], send_sems, recv_sems, 6 * t + 3 + k, sib).wait_recv()
            _remote(ps_ref, gs_ref.at[sk], send_sems, recv_sems, 6 * nt + k, sib).wait_recv()
        for cp in sends:
            cp.wait_send()
        l1.wait()

    nsem = 6 * nt + 3
    out = pl.pallas_call(
        body, in_specs=[ANY] * (nt + 1), out_specs=[ANY] * (nt + 1),
        out_shape=[SDS((4,) + pb.shape, pb.dtype) for pb in pbs] + [SDS((4,) + ps.shape, ps.dtype)],
        scratch_shapes=[pltpu.SemaphoreType.DMA((nsem,)), pltpu.SemaphoreType.DMA((nsem,)), pltpu.SemaphoreType.DMA((1,))],
        name="gather_weights")(*pbs, ps)
    return out[:nt], out[nt]


def _contain(wpad, shift):
    r, cw = wpad.shape
    tr = _tile16(r, 256)

    def body(n_ref, x_ref, o_ref):
        o_ref[...] = pltpu.roll(x_ref[...], n_ref[0], axis=1).astype(o_ref.dtype)

    spec = pl.BlockSpec((tr, cw), lambda i, n: (i, 0))
    return pl.pallas_call(
        body, grid_spec=pltpu.PrefetchScalarGridSpec(num_scalar_prefetch=1, grid=(r // tr,), in_specs=[spec], out_specs=spec),
        out_shape=SDS((r, cw), BF16), compiler_params=_params(1), name="contain")(shift, wpad)


def _place_own(gb, pb, chip):
    _, _, r, c = gb.shape
    tr = _tile16(r, 1100)

    def body(s_ref, p_ref, g_in, o_ref):
        o_ref[...] = p_ref[...]

    return pl.pallas_call(
        body, grid_spec=pltpu.PrefetchScalarGridSpec(
            num_scalar_prefetch=1, grid=(2, r // tr),
            in_specs=[pl.BlockSpec((None, tr, c), lambda h, i, s: (h, i, 0)), ANY],
            out_specs=pl.BlockSpec((None, None, tr, c), lambda h, i, s: (s[0], h, i, 0))),
        out_shape=SDS(gb.shape, gb.dtype), input_output_aliases={2: 0}, compiler_params=_params(2),
        name="place_own")(chip, pb, gb)


def _sem_scratch(n_remote, n_local):
    return [pltpu.SemaphoreType.DMA((n_remote,)), pltpu.SemaphoreType.DMA((n_remote,)), pltpu.SemaphoreType.DMA((n_local,))]


def _swap_halves(sends):
    nt = len(sends)

    def body(*refs):
        s_refs, got_refs = refs[:nt], refs[nt:2 * nt]
        send_sems, recv_sems = refs[2 * nt:]
        x, y, c, _ = _place()
        sib = (x, y, 1 - c)
        remote = [_remote(s_refs[t].at[1 - c, s], got_refs[t].at[s], send_sems, recv_sems, 4 * t + s, sib)
                  for t in range(nt) for s in range(4)]
        for cp in remote:
            cp.start()
        for cp in remote:
            cp.wait()

    return pl.pallas_call(
        body, in_specs=[ANY] * nt, out_specs=[ANY] * nt, out_shape=[SDS(g.shape[1:], g.dtype) for g in sends],
        scratch_shapes=[pltpu.SemaphoreType.DMA((4 * nt,)), pltpu.SemaphoreType.DMA((4 * nt,))], name="swap_halves")(*sends)


def _scatter_chip_sums(parts):
    nt = len(parts)

    def body(*refs):
        a_refs, r_refs = refs[:nt], refs[nt:2 * nt]
        send_sems, recv_sems = refs[2 * nt:]
        x, y, c, chips = _place()
        s = 2 * x + y
        sends = [_remote(a_refs[t].at[2 * px + py], r_refs[t].at[s], send_sems, recv_sems, 3 * t + k, (px, py, c))
                 for t in range(nt) for k, (px, py) in enumerate(chips)]
        for cp in sends:
            cp.start()
        for t in range(nt):
            for k, (px, py) in enumerate(chips):
                _remote(a_refs[t].at[s], r_refs[t].at[2 * px + py], send_sems, recv_sems, 3 * t + k, (px, py, c)).wait_recv()
        for cp in sends:
            cp.wait_send()

    return pl.pallas_call(
        body, in_specs=[ANY] * nt, out_specs=[ANY] * nt, out_shape=[SDS(a.shape, a.dtype) for a in parts],
        scratch_shapes=[pltpu.SemaphoreType.DMA((3 * nt,)), pltpu.SemaphoreType.DMA((3 * nt,))],
        name="scatter_chip_sums")(*parts)


def _join_halves(fs):
    nt = len(fs)

    def body(*refs):
        f_refs = refs[nt:2 * nt]
        send_sems, recv_sems = refs[2 * nt:]
        x, y, c, _ = _place()
        sib = (x, y, 1 - c)
        sends = [_remote(f_refs[t].at[c], f_refs[t].at[c], send_sems, recv_sems, t, sib) for t in range(nt)]
        for cp in sends:
            cp.start()
        for t in range(nt):
            _remote(f_refs[t].at[c], f_refs[t].at[1 - c], send_sems, recv_sems, t, sib).wait_recv()
        for cp in sends:
            cp.wait_send()

    return pl.pallas_call(
        body, in_specs=[ANY] * nt, out_specs=[ANY] * nt, out_shape=[SDS(f.shape, f.dtype) for f in fs],
        input_output_aliases={t: t for t in range(nt)},
        scratch_shapes=[pltpu.SemaphoreType.DMA((nt,)), pltpu.SemaphoreType.DMA((nt,))], name="join_halves")(*fs)


def _uncontain(cont, n_head, width):
    r, cw = cont.shape
    tr = _tile(r, 256)

    def body(n_ref, x_ref, o_ref):
        o_ref[...] = pltpu.roll(x_ref[...], n_ref[0], axis=1)[:, :width]

    return pl.pallas_call(
        body, grid_spec=pltpu.PrefetchScalarGridSpec(
            num_scalar_prefetch=1, grid=(r // tr,), in_specs=[pl.BlockSpec((tr, cw), lambda i, n: (i, 0))],
            out_specs=pl.BlockSpec((tr, width), lambda i, n: (i, 0))),
        out_shape=SDS((r, width), F32), compiler_params=_params(1), name="uncontain")(n_head, cont)


WEIGHTS = ("meta_tokens", "norm_w", "w_in", "conv_w", "a_log", "dt_bias", "gnorm_a", "gnorm_b", "hgrn_lower_bounds",
           "w_branch_a", "w_branch_b", "w_out", "final_norm_w")
SHARD_AXIS = {"meta_tokens": 1, "w_in": 2, "conv_w": 2, "w_branch_a": 2, "w_branch_b": 2, "w_out": 1}
FLAT_C = 1024


def _flat(parts, rows, cols=FLAT_C):
    v = jnp.concatenate([p.reshape(-1) for p in parts])
    return jnp.pad(v, (0, rows * cols - v.shape[0])).reshape(rows, cols)


def _local_step(x, target, w, lay):
    nb, seq, d = x.shape
    tp = CH + seq
    nc = tp // CH
    n = nb * tp
    e_mat, s_mat = _gate_consts()
    lb_all = _lb_fwd(w["hgrn_lower_bounds"])
    h = jnp.concatenate([jnp.zeros((nb, PAD, d), F32), jnp.broadcast_to(w["meta_tokens"][None], (nb, N_META, d)), x],
                        axis=1).reshape(n, d)
    rep = lambda a: jnp.repeat(a, HD)[None, :]
    saved = []
    for l in range(DEPTH):
        nw = w["norm_w"][l][None, :]
        proj, xn = _norm_proj_fwd(h, nw, w["w_in"][l])
        qkv = _gdn_prep_fwd(proj, w["conv_w"][l], lay, nb, tp)
        alog, dtb = rep(w["a_log"][l]), rep(w["dt_bias"][l])
        oa, sa, sva = _gdn_fwd(qkv, proj, e_mat, alog, dtb, lay, nb, nc)
        lbl = lb_all[l][None, :]
        ob, sb = _hgrn_fwd(proj, lbl, lay, nb, nc)
        ga, gb = w["gnorm_a"][l][None, :], w["gnorm_b"][l][None, :]
        hn = _merge_fwd(h, oa, ob, proj, ga, gb, w["w_branch_a"][l], w["w_branch_b"][l], w["w_out"][l], lay)
        saved.append((h, nw, proj, qkv, alog, dtb, oa, sa, lbl, ob, sb, ga, gb, xn, sva))
        h = hn
    target_p = jnp.pad(target, ((0, 0), (CH, 0), (0, 0))).reshape(n, d)
    lp, dh, dfw = _loss_head(h, target_p, w["final_norm_w"][None, :], nb, tp)
    loss = jnp.sum(lp[::8, 0])
    g = {n_: [None] * DEPTH for n_ in WEIGHTS}
    dlb_all = [None] * DEPTH
    for l in reversed(range(DEPTH)):
        h, nw, proj, qkv, alog, dtb, oa, sa, lbl, ob, sb, ga, gb, xn, sva = saved[l]
        dproj, doa, dob, dwa, dwb, dwo, dga, dgb = _merge_bwd(dh, oa, ob, proj, ga, gb, w["w_branch_a"][l],
                                                             w["w_branch_b"][l], w["w_out"][l], lay, tp)
        dproj, acc_b = _hgrn_bwd(proj, lbl, sb, dob, dproj, lay, nb, nc)
        dqkv, dproj, acc_a = _gdn_bwd(qkv, proj, e_mat, s_mat, alog, dtb, sa, sva, doa, dproj, lay, nb, nc)
        dproj, dconv = _gdn_prep_bwd(proj, w["conv_w"][l], dqkv, dproj, lay, nb, tp)
        dh, dnw = _proj_bwd_dx(dproj, w["w_in"][l], h, nw, dh, tp)
        g["w_in"][l] = _proj_bwd_dw(dproj, xn, tp)
        g["norm_w"][l] = dnw[0]
        g["conv_w"][l] = dconv
        g["a_log"][l] = acc_a[0, ::HD]
        g["dt_bias"][l] = acc_a[1, ::HD]
        g["gnorm_a"][l], g["gnorm_b"][l] = dga[0], dgb[0]
        g["w_branch_a"][l], g["w_branch_b"][l], g["w_out"][l] = dwa, dwb, dwo
        dlb_all[l] = acc_b[0]
    grads = {n_: jnp.stack(v) for n_, v in g.items() if v[0] is not None}
    grads["hgrn_lower_bounds"] = _lb_bwd(w["hgrn_lower_bounds"], jnp.stack(dlb_all))
    grads["final_norm_w"] = dfw[0]
    grads["meta_tokens"] = _meta_grad(dh, nb, nc)
    grad_x = dh.reshape(nb, tp, d)[:, CH:, :]
    return loss, grad_x, grads


def kernel(x, meta_tokens, norm_w, w_in, conv_w, a_log, dt_bias, gnorm_a, gnorm_b, hgrn_lower_bounds, w_branch_a, w_branch_b, w_out, final_norm_w, loss_target, m_meta_tokens, m_norm_w, m_w_in, m_conv_w, m_a_log, m_dt_bias, m_gnorm_a, m_gnorm_b, m_hgrn_lower_bounds, m_w_branch_a, m_w_branch_b, m_w_out, m_final_norm_w, v_meta_tokens, v_norm_w, v_w_in, v_conv_w, v_a_log, v_dt_bias, v_gnorm_a, v_gnorm_b, v_hgrn_lower_bounds, v_w_branch_a, v_w_branch_b, v_w_out, v_final_norm_w):
    wl = dict(meta_tokens=meta_tokens, norm_w=norm_w, w_in=w_in, conv_w=conv_w, a_log=a_log, dt_bias=dt_bias, gnorm_a=gnorm_a,
              gnorm_b=gnorm_b, hgrn_lower_bounds=hgrn_lower_bounds, w_branch_a=w_branch_a, w_branch_b=w_branch_b, w_out=w_out,
              final_norm_w=final_norm_w)
    ml = dict(zip(WEIGHTS, (m_meta_tokens, m_norm_w, m_w_in, m_conv_w, m_a_log, m_dt_bias, m_gnorm_a, m_gnorm_b,
                            m_hgrn_lower_bounds, m_w_branch_a, m_w_branch_b, m_w_out, m_final_norm_w)))
    vl = dict(zip(WEIGHTS, (v_meta_tokens, v_norm_w, v_w_in, v_conv_w, v_a_log, v_dt_bias, v_gnorm_a, v_gnorm_b,
                            v_hgrn_lower_bounds, v_w_branch_a, v_w_branch_b, v_w_out, v_final_norm_w)))
    d = x.shape[2]
    lay = _Layout(d)
    nchip = 4

    big = ("w_in", "w_branch_a", "w_branch_b", "w_out")
    small = ("conv_w", "meta_tokens")
    table, heads, cw = lay.pieces(nchip)
    sw = wl["w_in"].shape[2]
    chip_id = (2 * lax.axis_index("x") + lax.axis_index("y")).astype(jnp.int32)
    n_head = sum(jnp.where(chip_id == s, heads[s], 0) for s in range(nchip)).astype(jnp.int32)
    w_pad = jnp.pad(wl["w_in"], ((0, 0), (0, 0), (0, cw - sw))).reshape(DEPTH * d, cw)
    shift = jnp.where(n_head == 0, 0, cw - n_head).astype(jnp.int32).reshape(1)
    pbs = [_contain(w_pad, shift).reshape(DEPTH, d, cw)] + [wl[n].astype(BF16) for n in big[1:]]
    nsmall = sum(int(np.prod(wl[n].shape)) for n in small)
    rs = -(-nsmall // (HD * 8)) * 8
    ps = jnp.pad(jnp.concatenate([wl[n].reshape(-1) for n in small]), (0, rs * HD - nsmall)).reshape(rs, HD)
    gbig, gsmall = _gather_weights(pbs, ps)
    gbig = [_place_own(g, p, chip_id.reshape(1)) for g, p in zip(gbig, pbs)]
    gsmall = gsmall.reshape(nchip, -1)

    wf = dict(wl)
    wf["w_in"] = lay.from_containers([gbig[0][s] for s in range(nchip)])
    for i, n in enumerate(big[1:], start=1):
        wf[n] = jnp.concatenate([gbig[i][s] for s in range(nchip)], axis=SHARD_AXIS[n])
    o = 0
    for n in small:
        sz = int(np.prod(wl[n].shape))
        a = gsmall[:, o:o + sz].reshape((nchip,) + wl[n].shape)
        wf[n] = jnp.concatenate([a[s] for s in range(nchip)], axis=SHARD_AXIS[n])
        o += sz

    loss_part, grad_x, gfull = _local_step(x, loss_target, wf, lay)
    loss = lax.psum(loss_part, ("x", "y", "c"))

    sw = wl["w_in"].shape[2]
    conts, heads = lay.containers(gfull["w_in"], nchip)
    dd = wl["w_branch_a"].shape[2]
    rows_o = wl["w_out"].shape[1]
    by_dest = lambda g, n: [lax.slice_in_dim(g, s * wl[n].shape[SHARD_AXIS[n]], (s + 1) * wl[n].shape[SHARD_AXIS[n]],
                                            axis=SHARD_AXIS[n]) if n in SHARD_AXIS else g for s in range(nchip)]
    small_names = tuple(n for n in WEIGHTS if n not in big)
    nsm = sum(int(np.prod(wl[n].shape)) for n in small_names)
    rsm = -(-nsm // (2 * HD * 8)) * 8
    pack_small = lambda parts: _flat(parts, 2 * rsm, HD).reshape(2, rsm, HD)
    small_by_dest = [by_dest(gfull[n], n) for n in small_names]
    gs = [jnp.stack(conts, axis=1),
          jnp.stack(by_dest(gfull["w_branch_a"], "w_branch_a"), axis=1),
          jnp.stack(by_dest(gfull["w_branch_b"], "w_branch_b"), axis=1),
          gfull["w_out"].reshape(DEPTH, nchip, rows_o, d),
          jnp.stack([pack_small([p[s] for p in small_by_dest]) for s in range(nchip)], axis=1)]
    gs = [g.reshape((2, nchip, -1, g.shape[-1])) for g in gs]
    my_chip = (2 * lax.axis_index("x") + lax.axis_index("y")).astype(jnp.int32)
    my_core = lax.axis_index("c").astype(jnp.int32)
    got = _swap_halves([g.astype(BF16) for g in gs[:4]] + gs[4:])
    chip_sums = [_add_cores(g, b, my_core.reshape(1)) for g, b in zip(gs, got)]
    by_chip = _scatter_chip_sums(chip_sums)
    full = _join_halves([_sum_chips(p, a, my_chip.reshape(1), my_core.reshape(1)) for p, a in zip(by_chip, chip_sums)])
    n_head = sum(jnp.where(my_chip == s, heads[s], 0) for s in range(nchip)).astype(jnp.int32).reshape(1)
    g_w_in = _uncontain(full[0].reshape(DEPTH * d, -1), n_head, sw)
    g2 = {"w_in": g_w_in, "w_branch_a": full[1].reshape(-1, dd), "w_branch_b": full[2].reshape(-1, dd),
          "w_out": full[3].reshape(-1, d), "small": full[4].reshape(2 * rsm, HD)}

    def two_d(src, n):
        if n == "small":
            return _flat([src[k] for k in small_names], 2 * rsm, HD)
        return src[n].reshape(g2[n].shape)

    outs = {}
    for n in big + ("small",):
        delta, mnew, vnew = _adamw(g2[n], two_d(wl, n), two_d(ml, n), two_d(vl, n))
        outs[n] = (g2[n], delta, mnew, vnew)
    res = [{}, {}, {}, {}]
    for i in range(4):
        for n in big:
            res[i][n] = outs[n][i].reshape(wl[n].shape)
        v, o = outs["small"][i].reshape(-1), 0
        for n in small_names:
            sz = int(np.prod(wl[n].shape))
            res[i][n] = v[o:o + sz].reshape(wl[n].shape)
            o += sz
    return (loss, grad_x, *[res[0][n] for n in WEIGHTS], *[res[1][n] for n in WEIGHTS], *[res[2][n] for n in WEIGHTS],
            *[res[3][n] for n in WEIGHTS])
```

```python
import functools

import numpy as np
import jax
import jax.numpy as jnp
from jax import lax
from jax.experimental import pallas as pl
from jax.experimental.pallas import tpu as pltpu

F32 = jnp.float32
BF16 = jnp.bfloat16
HI = lax.Precision.HIGHEST
SDS = jax.ShapeDtypeStruct

NH = 4
HD = 128
HW = NH * HD
N_META = 16
CH = 64
SUB = 16
PAD = CH - N_META
EPS = 1e-6
Q_SCALE = HD ** -0.5
DEPTH = 2
CONV_K = 4
VMEM_LIMIT = 56 * 1024 * 1024
ADAM_LR, ADAM_B1, ADAM_B2, ADAM_EPS, ADAM_WD, ADAM_STEP = 0.001, 0.9, 0.999, 1e-08, 0.01, 10
MESH = pl.DeviceIdType.MESH


def _nn(a, b):
    return jnp.dot(a, b, precision=HI, preferred_element_type=F32)


def _nt(a, b):
    return lax.dot_general(a, b, (((1,), (1,)), ((), ())), precision=HI, preferred_element_type=F32)


def _tn(a, b):
    return _nn(a.T, b)


def _scan_rows(x, group, reverse=False):
    n = x.shape[0]
    pos = lax.bitwise_and(_iota2(x.shape, 0), group - 1)
    s = 1
    while s < group:
        if reverse:
            x = x + jnp.where(pos < group - s, pltpu.roll(x, n - s, axis=0), 0.0)
        else:
            x = x + jnp.where(pos >= s, pltpu.roll(x, s, axis=0), 0.0)
        s *= 2
    return x


def _bnn(a, b):
    return jnp.dot(a.astype(BF16), b.astype(BF16), preferred_element_type=F32)


def _bnt(a, b):
    return lax.dot_general(a.astype(BF16), b.astype(BF16), (((1,), (1,)), ((), ())), preferred_element_type=F32)


def _btn(a, b):
    return lax.dot_general(a.astype(BF16), b.astype(BF16), (((0,), (0,)), ((), ())), preferred_element_type=F32)


def _hi_lo(x):
    hi = x.astype(jnp.bfloat16)
    return hi, (x - hi.astype(F32)).astype(jnp.bfloat16)


def _dot3(dims):
    def f(a, b):
        ah, al = _hi_lo(a)
        bh, bl = _hi_lo(b)
        d = lambda p, q: lax.dot_general(p, q, (dims, ((), ())), preferred_element_type=F32)
        return d(ah, bh) + (d(ah, bl) + d(al, bh))
    return f


_rnn, _rnt, _rtn = _dot3(((1,), (0,))), _dot3(((1,), (1,))), _dot3(((0,), (0,)))
_enn, _ent, _etn = _bnn, _bnt, _btn
_hnn, _hnt, _htn = _bnn, _bnt, _btn


def _rr(x):
    return x


def _sig(x):
    return jax.nn.sigmoid(x)


def _silu(x):
    return x * _sig(x)


def _dsilu(x):
    s = _sig(x)
    return s * (1.0 + x * (1.0 - s))


def _softplus(x):
    return jnp.maximum(x, 0.0) + jnp.log(1.0 + jnp.exp(-jnp.abs(x)))


def _logsig(x):
    return jnp.minimum(x, 0.0) - jnp.log(1.0 + jnp.exp(-jnp.abs(x)))


def _rs(x):
    return jnp.sum(x, axis=-1, keepdims=True)


def _params(n_axes):
    return pltpu.CompilerParams(dimension_semantics=("arbitrary",) * n_axes, vmem_limit_bytes=VMEM_LIMIT)


def _tile(n, target, mult=8):
    best = mult
    for t in range(mult, target + 1, mult):
        if n % t == 0:
            best = t
    assert n % best == 0, (n, mult)
    return best


def _ctile(pw, most=7):
    return HD * max(k for k in range(1, most + 1) if (pw // HD) % k == 0)


def _iota2(shape, axis):
    return lax.broadcasted_iota(jnp.int32, shape, axis)


class _Layout:
    def __init__(self, d):
        self.d = d
        self.wm = 2 * HW + 2 * d
        self.c_qkv = self.wm
        self.c_b = self.wm + 3 * HW
        self.c_ba = self.wm + 6 * HW
        self.pw = self.c_ba + HD
        assert self.c_b % (3 * HW) == 0
        o = 0
        segs = {}
        for name, w in (("a_q", HW), ("a_k", HW), ("a_v", HW), ("ba", 2 * NH), ("a_z", HW), ("b_q", HW), ("b_f", HW),
                        ("b_i", HW), ("b_g", HW), ("gate_a", d), ("gate_b", d)):
            segs[name] = (o, o + w)
            o += w
        self.segs = segs
        self.width = o
        self.order = ("a_z", "b_g", "gate_a", "gate_b", "a_q", "a_k", "a_v", "b_q", "b_f", "b_i", "ba")

    def to_kernel(self, w):
        parts = [w[..., self.segs[n][0]:self.segs[n][1]] for n in self.order]
        parts.append(jnp.zeros(w.shape[:-1] + (HD - 2 * NH,), w.dtype))
        return jnp.concatenate(parts, axis=-1)

    def containers(self, g, nchip):
        table, heads, cw = self.pieces(nchip)
        out = []
        for s in range(nchip):
            parts, at = [], 0
            for kcol, w, ccol in sorted(table[s], key=lambda p: p[2]):
                if ccol > at:
                    parts.append(jnp.zeros(g.shape[:-1] + (ccol - at,), g.dtype))
                parts.append(g[..., kcol:kcol + w])
                at = ccol + w
            if at < cw:
                parts.append(jnp.zeros(g.shape[:-1] + (cw - at,), g.dtype))
            out.append(jnp.concatenate(parts, axis=-1))
        return out, heads

    def pieces(self, nchip):
        off, where = 0, {}
        for n in self.order:
            where[n] = off
            off += self.segs[n][1] - self.segs[n][0]
        names = sorted(self.segs, key=lambda n: self.segs[n][0])
        sw = self.width // nchip
        cw = -(-sw // HD) * HD
        table, heads = [], []
        for s in range(nchip):
            lo, hi = s * sw, (s + 1) * sw
            pieces = []
            for n in names:
                a, b = max(lo, self.segs[n][0]), min(hi, self.segs[n][1])
                if a < b:
                    pieces.append((where[n] + a - self.segs[n][0], b - a))
            start, width = pieces[0]
            n_head = min((-start) % HD, width)
            body = ([(start + n_head, width - n_head)] if width > n_head else []) + pieces[1:]
            rows, at = [], 0
            for c, w in body:
                rows.append((c, w, at))
                at += w
            if n_head:
                rows.append((start, n_head, cw - n_head))
            table.append(rows)
            heads.append(n_head)
        return table, heads, cw

    def from_containers(self, conts):
        table, _, _ = self.pieces(len(conts))
        cut = sorted((kcol, w, s, ccol) for s, rows in enumerate(table) for kcol, w, ccol in rows)
        parts, at = [], 0
        for kcol, w, s, ccol in cut:
            assert kcol == at, (kcol, at)
            parts.append(conts[s][..., ccol:ccol + w])
            at = kcol + w
        parts.append(jnp.zeros(conts[0].shape[:-1] + (self.pw - at,), conts[0].dtype))
        return jnp.concatenate(parts, axis=-1)

    def from_kernel(self, g):
        off, where = 0, {}
        for n in self.order:
            w = self.segs[n][1] - self.segs[n][0]
            where[n] = (off, off + w)
            off += w
        names = sorted(self.segs, key=lambda n: self.segs[n][0])
        return jnp.concatenate([g[..., where[n][0]:where[n][1]] for n in names], axis=-1)


def _norm_proj_fwd(h, nw, wp):
    n, d = h.shape
    pw = wp.shape[1]
    tm, tn = _tile(n, 1408, HD), _ctile(pw)

    def body(h_ref, nw_ref, w_ref, o_ref, xt_ref, xn_ref):
        @pl.when(pl.program_id(1) == 0)
        def _():
            x = h_ref[...]
            r = lax.rsqrt(jnp.mean(x * x, axis=-1, keepdims=True) + EPS)
            xn = (x * r * nw_ref[...]).astype(BF16)
            xn_ref[...] = xn
            xt_ref[...] = xn.T

        o_ref[...] = jnp.dot(xn_ref[...], w_ref[...], preferred_element_type=F32)

    return pl.pallas_call(
        body, grid=(n // tm, pw // tn),
        in_specs=[pl.BlockSpec((tm, d), lambda i, j: (i, 0)), pl.BlockSpec((1, d), lambda i, j: (0, 0)),
                  pl.BlockSpec((d, tn), lambda i, j: (0, j))],
        out_specs=[pl.BlockSpec((tm, tn), lambda i, j: (i, j)), pl.BlockSpec((d, tm), lambda i, j: (0, i))],
        out_shape=[SDS((n, pw), F32), SDS((d, n), BF16)], scratch_shapes=[pltpu.VMEM((tm, d), BF16)],
        compiler_params=_params(2), name="norm_proj_fwd")(h, nw, wp)


def _row_valid(tm, tp, base):
    row = base + _iota2((tm, 1), 0)
    return lax.rem(row, tp) >= PAD


def _proj_bwd_dx(dproj, wp, h, nw, dhn, tp):
    n, d = h.shape
    pw = wp.shape[1]
    tm, tk = _tile16(n, 1056), _ctile(pw)
    nk = pw // tk

    def body(dp_ref, w_ref, h_ref, nw_ref, dhn_ref, dh_ref, dnw_ref, acc_ref):
        i, k = pl.program_id(0), pl.program_id(1)

        @pl.when(k == 0)
        def _():
            acc_ref[...] = jnp.zeros_like(acc_ref)

        @pl.when((i == 0) & (k == 0))
        def _():
            dnw_ref[...] = jnp.zeros_like(dnw_ref)

        valid = _row_valid(tm, tp, i * tm)
        dp = jnp.where(valid, dp_ref[...], 0.0)
        acc_ref[...] += _bnt(dp, w_ref[...])

        @pl.when(k == nk - 1)
        def _():
            x = h_ref[...]
            r = lax.rsqrt(jnp.mean(x * x, axis=-1, keepdims=True) + EPS)
            xh = x * r
            dxn = acc_ref[...]
            dnw_ref[...] += jnp.sum(dxn * xh, axis=0, keepdims=True)
            dxh = dxn * nw_ref[...]
            dh_ref[...] = dhn_ref[...] + r * (dxh - xh * jnp.mean(dxh * xh, axis=-1, keepdims=True))

    return pl.pallas_call(
        body, grid=(n // tm, nk),
        in_specs=[pl.BlockSpec((tm, tk), lambda i, k: (i, k)), pl.BlockSpec((d, tk), lambda i, k: (0, k)),
                  pl.BlockSpec((tm, d), lambda i, k: (i, 0)), pl.BlockSpec((1, d), lambda i, k: (0, 0)),
                  pl.BlockSpec((tm, d), lambda i, k: (i, 0))],
        out_specs=[pl.BlockSpec((tm, d), lambda i, k: (i, 0)), pl.BlockSpec((1, d), lambda i, k: (0, 0))],
        out_shape=[SDS((n, d), F32), SDS((1, d), F32)],
        scratch_shapes=[pltpu.VMEM((tm, d), F32)], compiler_params=_params(2), name="proj_bwd_dx")(dproj, wp, h, nw, dhn)


def _proj_bwd_dw(dproj, xt, tp):
    d, n = xt.shape
    pw = dproj.shape[1]
    tm, tn = _tile(n, 1408, HD), _ctile(pw)

    def body(dp_ref, xt_ref, dw_ref):
        i = pl.program_id(1)

        @pl.when(i == 0)
        def _():
            dw_ref[...] = jnp.zeros_like(dw_ref)

        dp = jnp.where(_row_valid(tm, tp, i * tm), dp_ref[...], 0.0)
        dw_ref[...] += jnp.dot(xt_ref[...], dp.astype(BF16), preferred_element_type=F32)

    return pl.pallas_call(
        body, grid=(pw // tn, n // tm),
        in_specs=[pl.BlockSpec((tm, tn), lambda j, i: (i, j)), pl.BlockSpec((d, tm), lambda j, i: (0, i))],
        out_specs=pl.BlockSpec((d, tn), lambda j, i: (0, j)), out_shape=SDS((d, pw), F32),
        compiler_params=_params(2), name="proj_bwd_dw")(dproj, xt)


def _conv_silu(x, w, row):
    c = x * w[CONV_K - 1:CONV_K, :]
    for k in range(1, CONV_K):
        c = c + jnp.where(row >= k, pltpu.roll(x, k, axis=0), 0.0) * w[CONV_K - 1 - k:CONV_K - k, :]
    return c


def _gdn_prep_fwd(proj, conv_w, lay, nb, tp):
    n = proj.shape[0]
    nblk = 3 * NH
    cb = lay.c_qkv // HD

    def body(p_ref, w_ref, o_ref):
        j = pl.program_id(1)
        x = p_ref[...]
        row = _iota2(x.shape, 0)
        c = _conv_silu(x, w_ref[...], row)
        s = _silu(c)
        r = lax.rsqrt(_rs(s * s) + EPS)
        scale = jnp.where(j < NH, Q_SCALE, 1.0)
        y = jnp.where(j < 2 * NH, s * r * scale, s)
        o_ref[...] = jnp.where(row >= PAD, y, 0.0)

    return pl.pallas_call(
        body, grid=(nb, nblk),
        in_specs=[pl.BlockSpec((tp, HD), lambda b, j: (b, cb + j)), pl.BlockSpec((CONV_K, HD), lambda b, j: (0, j))],
        out_specs=pl.BlockSpec((tp, HD), lambda b, j: (b, j)), out_shape=SDS((n, nblk * HD), F32),
        compiler_params=_params(2), name="gdn_prep_fwd")(proj, conv_w)


def _gdn_prep_bwd(proj, conv_w, dqkv, dproj, lay, nb, tp):
    nblk = 3 * NH
    cb = lay.c_qkv // HD

    def body(p_ref, w_ref, dy_ref, dp_in, dp_ref, dw_ref):
        j, b = pl.program_id(0), pl.program_id(1)
        x = p_ref[...]
        w = w_ref[...]
        row = _iota2(x.shape, 0)
        c = _conv_silu(x, w, row)
        s = _silu(c)
        dy = jnp.where(row >= PAD, dy_ref[...], 0.0)
        r = lax.rsqrt(_rs(s * s) + EPS)
        nh = s * r
        scale = jnp.where(j < NH, Q_SCALE, 1.0)
        ds_n = scale * r * (dy - nh * _rs(dy * nh))
        ds = jnp.where(j < 2 * NH, ds_n, dy)
        dc = ds * _dsilu(c)
        dx = dc * w[CONV_K - 1:CONV_K, :]
        dws = [jnp.sum(dc * x, axis=0, keepdims=True)]
        for k in range(1, CONV_K):
            dx = dx + jnp.where(row < tp - k, pltpu.roll(dc, tp - k, axis=0), 0.0) * w[CONV_K - 1 - k:CONV_K - k, :]
            xs = jnp.where(row >= k, pltpu.roll(x, k, axis=0), 0.0)
            dws.append(jnp.sum(dc * xs, axis=0, keepdims=True))
        dp_ref[...] = dx.astype(dp_ref.dtype)
        r4 = _iota2((CONV_K, HD), 0)
        dw = jnp.zeros((CONV_K, HD), F32)
        for k in range(CONV_K):
            dw = dw + jnp.where(r4 == CONV_K - 1 - k, dws[k], 0.0)

        @pl.when(b == 0)
        def _():
            dw_ref[...] = dw

        @pl.when(b > 0)
        def _():
            dw_ref[...] += dw

    return pl.pallas_call(
        body, grid=(nblk, nb),
        in_specs=[pl.BlockSpec((tp, HD), lambda j, b: (b, cb + j)), pl.BlockSpec((CONV_K, HD), lambda j, b: (0, j)),
                  pl.BlockSpec((tp, HD), lambda j, b: (b, j)), pl.BlockSpec(memory_space=pl.ANY)],
        out_specs=[pl.BlockSpec((tp, HD), lambda j, b: (b, cb + j)), pl.BlockSpec((CONV_K, HD), lambda j, b: (0, j))],
        out_shape=[SDS(dproj.shape, dproj.dtype), SDS((CONV_K, nblk * HD), F32)],
        input_output_aliases={3: 0}, compiler_params=_params(2), name="gdn_prep_bwd")(proj, conv_w, dqkv, dproj)


def _gate_consts():
    e = np.zeros((HD, 2 * HW), np.float32)
    s = np.zeros((2 * HW, HD), np.float32)
    for h in range(NH):
        e[h, h * HD:(h + 1) * HD] = 1.0
        e[NH + h, HW + h * HD:HW + (h + 1) * HD] = 1.0
        s[h * HD, h] = 1.0
        s[HW + h * HD, NH + h] = 1.0
    return jnp.asarray(e), jnp.asarray(s)


def _gdn_tri():
    i, j = _iota2((CH, CH), 0), _iota2((CH, CH), 1)
    return i >= j, i > j


def _each(fn, *lists):
    return [fn(*xs) for xs in zip(*lists)]


def _tri_inv(a_list, eye):
    p = [-a for a in a_list]
    t = [eye + x for x in p]
    for _ in range(5):
        p = _each(_rnn, p, p)
        tp_ = _each(_rnn, t, p)
        t = _each(lambda x, y: x + y, t, tp_)
    return t


def _gdn_chunks(args, solved=None):
    causal, strict = _gdn_tri()
    eye = jnp.where(_iota2((CH, CH), 0) == _iota2((CH, CH), 1), 1.0, 0.0)
    q, k, v, beta, g, s0 = (list(t) for t in zip(*args))
    gc = [_scan_rows(x, CH) for x in g]
    dm = [jnp.where(causal, jnp.exp(jnp.where(causal, x[:, :CH] - x[:, :CH].T, 0.0)), 0.0) for x in gc]
    ds = [jnp.where(strict, x, 0.0) for x in dm]
    kb = _each(lambda x, y: x * y, k, beta)
    kk = _each(_ent, kb, k)
    a = _each(lambda x, y: x * y, kk, ds)
    eg = [jnp.exp(x) for x in gc]
    rw = _each(lambda x, y: x * y, kb, eg)
    if solved is None:
        tinv = _tri_inv(a, eye)
        rv = _each(lambda x, y: x * y, v, beta)
        u = _each(_rnn, tinv, rv)
        w = _each(_rnn, tinv, rw)
    else:
        tinv, u, w = (list(t) for t in zip(*solved))
    ws = _each(_enn, w, s0)
    vn = _each(lambda x, y: x - y, u, ws)
    qk = _each(_ent, q, k)
    p = _each(lambda x, y: x * y, qk, dm)
    qg = _each(lambda x, y: x * y, q, eg)
    out = []
    for i in range(len(args)):
        gl = gc[i][CH - 1:CH, :]
        ek = jnp.exp(gl - gc[i])
        out.append(dict(gc=gc[i], dm=dm[i], ds=ds[i], kb=kb[i], a=a[i], tinv=tinv[i], eg=eg[i], rw=rw[i], u=u[i], w=w[i],
                        vn=vn[i], p=p[i], qg=qg[i], egl=jnp.exp(gl), ek=ek, kd=k[i] * ek))
    return out


def _gdn_gates(ba, e, alog, dtb):
    raw = _nn(ba, e)
    beta = _sig(raw[:, :HW])
    za = raw[:, HW:] + dtb
    g = -jnp.exp(alog) * _softplus(za)
    return beta, g, za


def _seqs_per_step(nb):
    return 4 if nb % 4 == 0 else (2 if nb % 2 == 0 else 1)


def _gdn_fwd(qkv, proj, e_mat, alog, dtb, lay, nb, nc):
    n = qkv.shape[0]
    tp = n // nb
    cba = lay.c_ba // HD
    gb = _seqs_per_step(nb)

    def body(x_ref, ba_ref, e_ref, al_ref, dt_ref, o_ref, so_ref, sv_ref, s_ref):
        @pl.when(pl.program_id(1) == 0)
        def _():
            s_ref[...] = jnp.zeros_like(s_ref)

        args = []
        for j in range(gb):
            beta, g, _ = _gdn_gates(ba_ref[j], e_ref[...], al_ref[...], dt_ref[...])
            for h in range(NH):
                hs = slice(h * HD, (h + 1) * HD)
                args.append((x_ref[j, :, hs], x_ref[j, :, HW + h * HD:HW + (h + 1) * HD],
                             x_ref[j, :, 2 * HW + h * HD:2 * HW + (h + 1) * HD], beta[:, hs], g[:, hs], s_ref[j, h]))
        cs = _gdn_chunks(args)
        s0s = [a[5] for a in args]
        o1 = _each(lambda c, s0: _enn(c["qg"], s0), cs, s0s)
        o2 = [_enn(c["p"], c["vn"]) for c in cs]
        upd = [_etn(c["kd"], c["vn"]) for c in cs]
        res = [(o1[i] + o2[i], s0s[i] * cs[i]["egl"] + upd[i]) for i in range(len(cs))]
        zero = jnp.zeros((CH, HD - CH), F32)
        for j in range(gb):
            for h in range(NH):
                c = cs[j * NH + h]
                so_ref[j, h] = args[j * NH + h][5]
                sv_ref[j, h] = jnp.concatenate([c["u"], c["w"], c["tinv"], zero], axis=-1)
                s_ref[j, h] = res[j * NH + h][1]
            o_ref[j] = jnp.concatenate([res[j * NH + h][0] for h in range(NH)], axis=-1)

    o, st, sv = pl.pallas_call(
        body, grid=(nb // gb, nc),
        in_specs=[pl.BlockSpec((gb, CH, 3 * HW), lambda b, c: (b, c, 0)), pl.BlockSpec((gb, CH, HD), lambda b, c: (b, c, cba)),
                  pl.BlockSpec((HD, 2 * HW), lambda b, c: (0, 0)), pl.BlockSpec((1, HW), lambda b, c: (0, 0)),
                  pl.BlockSpec((1, HW), lambda b, c: (0, 0))],
        out_specs=[pl.BlockSpec((gb, CH, HW), lambda b, c: (b, c, 0)),
                   pl.BlockSpec((gb, None, NH, HD, HD), lambda b, c: (b, c, 0, 0, 0)),
                   pl.BlockSpec((gb, None, NH, CH, 3 * HD), lambda b, c: (b, c, 0, 0, 0))],
        out_shape=[SDS((nb, tp, HW), F32), SDS((nb, nc, NH, HD, HD), F32), SDS((nb, nc, NH, CH, 3 * HD), F32)],
        scratch_shapes=[pltpu.VMEM((gb, NH, HD, HD), F32)], compiler_params=_params(2), name="gdn_fwd")(
            qkv.reshape(nb, tp, 3 * HW), proj.reshape(nb, tp, -1), e_mat, alog, dtb)
    return o.reshape(n, HW), st, sv


def _gdn_bwd(qkv, proj, e_mat, s_mat, alog, dtb, states, solved, do, dproj, lay, nb, nc):
    n = qkv.shape[0]
    tp = n // nb
    cba = lay.c_ba // HD
    gb = _seqs_per_step(nb)

    def body(x_ref, ba_ref, e_ref, sm_ref, al_ref, dt_ref, st_ref, sv_ref, do_ref, dp_in, dx_ref, dba_ref, acc_ref, ds_ref):
        ci = pl.program_id(1)

        @pl.when(ci == 0)
        def _():
            ds_ref[...] = jnp.zeros_like(ds_ref)

        @pl.when((ci == 0) & (pl.program_id(0) == 0))
        def _():
            acc_ref[...] = jnp.zeros_like(acc_ref)

        causal, strict = _gdn_tri()
        alog = al_ref[...]
        row = _iota2((CH, 1), 0)
        valid = (row >= PAD) | (ci < nc - 1)
        last = row == CH - 1
        gates = [_gdn_gates(ba_ref[j], e_ref[...], alog, dt_ref[...]) for j in range(gb)]
        args, do, ds1, solved = [], [], [], []
        for j in range(gb):
            beta, g, _ = gates[j]
            for h in range(NH):
                hs = slice(h * HD, (h + 1) * HD)
                args.append((x_ref[j, :, hs], x_ref[j, :, HW + h * HD:HW + (h + 1) * HD],
                             x_ref[j, :, 2 * HW + h * HD:2 * HW + (h + 1) * HD], beta[:, hs], g[:, hs], st_ref[j, h]))
                do.append(do_ref[j, :, hs])
                ds1.append(ds_ref[j, h])
                solved.append((sv_ref[j, h, :, 2 * HD:2 * HD + CH], sv_ref[j, h, :, 0:HD], sv_ref[j, h, :, HD:2 * HD]))
        q, k, v, bh, _, s0 = (list(t) for t in zip(*args))
        cs = _gdn_chunks(args, solved)
        get = lambda name: [c[name] for c in cs]
        mul = lambda x, y: x * y
        add = lambda x, y: x + y
        dvn = _each(add, _each(_etn, get("p"), do), _each(_enn, get("kd"), ds1))
        dqg = _each(_ent, do, s0)
        dp = [jnp.where(causal, x, 0.0) for x in _each(_ent, do, get("vn"))]
        dkd = _each(_ent, get("vn"), ds1)
        dw = [-x for x in _each(_ent, dvn, s0)]
        ds_a = _each(_etn, get("qg"), do)
        ds_b = _each(_etn, get("w"), dvn)
        ds_new = [ds_a[i] - ds_b[i] + ds1[i] * cs[i]["egl"] for i in range(len(cs))]
        drvw = _each(_rtn, get("tinv"), [jnp.concatenate([x, y], axis=-1) for x, y in zip(dvn, dw)])
        drv = [x[:, :HD] for x in drvw]
        drw = [x[:, HD:] for x in drvw]
        uw = [sv_ref[j, h, :, 0:2 * HD] for j in range(gb) for h in range(NH)]
        da = [jnp.where(strict, -x, 0.0) for x in _each(_rnt, drvw, uw)]
        m = [da[i] * cs[i]["a"] + dp[i] * cs[i]["p"] for i in range(len(cs))]
        dkk = _each(mul, da, get("ds"))
        dqk = _each(mul, dp, get("dm"))
        dq = _each(add, _each(_enn, dqk, k), _each(mul, dqg, get("eg")))
        dkb = _each(add, _each(_enn, dkk, k), _each(mul, drw, get("eg")))
        dk_1 = _each(_etn, dqk, q)
        dk_2 = _each(_etn, dkk, get("kb"))
        dk = [dk_1[i] + dk_2[i] + dkd[i] * cs[i]["ek"] + dkb[i] * bh[i] for i in range(len(cs))]
        dv = _each(mul, drv, bh)
        dbeta, dg = [], []
        for i, c in enumerate(cs):
            dbeta.append(_rs(drv[i] * v[i]) + _rs(dkb[i] * k[i]) + jnp.zeros((CH, HD), F32))
            t_kd = _rs(dkd[i] * c["kd"])
            dgc = _rs(m[i]) - _rs(m[i].T) + _rs(dqg[i] * c["qg"]) + _rs(drw[i] * c["rw"]) - t_kd
            tail = jnp.sum(t_kd, axis=0, keepdims=True) + c["egl"] * jnp.sum(_rs(s0[i] * ds1[i]), axis=0, keepdims=True)
            dgc = dgc + jnp.where(last, tail, 0.0)
            dg.append(_scan_rows(dgc + jnp.zeros((CH, HD), F32), CH, reverse=True))
        r8 = _iota2((8, HW), 0)
        upd = jnp.zeros((8, HW), F32)
        for j in range(gb):
            sl = slice(j * NH, (j + 1) * NH)
            beta, g, za = gates[j]
            for h in range(NH):
                ds_ref[j, h] = ds_new[j * NH + h]
            dx_ref[j] = jnp.concatenate(dq[sl] + dk[sl] + dv[sl], axis=-1)
            dbeta_j = jnp.where(valid, jnp.concatenate(dbeta[sl], axis=-1), 0.0)
            dg_j = jnp.where(valid, jnp.concatenate(dg[sl], axis=-1), 0.0)
            draw_b = dbeta_j * beta * (1.0 - beta)
            draw_a = dg_j * (-jnp.exp(alog)) * _sig(za)
            dba_ref[j] = _nn(jnp.concatenate([draw_b, draw_a], axis=-1), sm_ref[...]).astype(dba_ref.dtype)
            upd = upd + jnp.where(r8 == 0, jnp.sum(dg_j * g, axis=0, keepdims=True), 0.0) + jnp.where(
                r8 == 1, jnp.sum(draw_a, axis=0, keepdims=True), 0.0)
        acc_ref[...] += upd

    rc = lambda c: nc - 1 - c
    dqkv, dproj3, acc = pl.pallas_call(
        body, grid=(nb // gb, nc),
        in_specs=[pl.BlockSpec((gb, CH, 3 * HW), lambda b, c: (b, rc(c), 0)), pl.BlockSpec((gb, CH, HD), lambda b, c: (b, rc(c), cba)),
                  pl.BlockSpec((HD, 2 * HW), lambda b, c: (0, 0)), pl.BlockSpec((2 * HW, HD), lambda b, c: (0, 0)),
                  pl.BlockSpec((1, HW), lambda b, c: (0, 0)), pl.BlockSpec((1, HW), lambda b, c: (0, 0)),
                  pl.BlockSpec((gb, None, NH, HD, HD), lambda b, c: (b, rc(c), 0, 0, 0)),
                  pl.BlockSpec((gb, None, NH, CH, 3 * HD), lambda b, c: (b, rc(c), 0, 0, 0)),
                  pl.BlockSpec((gb, CH, HW), lambda b, c: (b, rc(c), 0)), pl.BlockSpec(memory_space=pl.ANY)],
        out_specs=[pl.BlockSpec((gb, CH, 3 * HW), lambda b, c: (b, rc(c), 0)), pl.BlockSpec((gb, CH, HD), lambda b, c: (b, rc(c), cba)),
                   pl.BlockSpec((8, HW), lambda b, c: (0, 0))],
        out_shape=[SDS((nb, tp, 3 * HW), F32), SDS((nb, tp, dproj.shape[1]), dproj.dtype), SDS((8, HW), F32)],
        input_output_aliases={9: 1},
        scratch_shapes=[pltpu.VMEM((gb, NH, HD, HD), F32)], compiler_params=_params(2), name="gdn_bwd")(
            qkv.reshape(nb, tp, 3 * HW), proj.reshape(nb, tp, -1), e_mat, s_mat, alog, dtb, states, solved, do.reshape(nb, tp, HW),
            dproj.reshape(nb, tp, -1))
    return dqkv.reshape(n, 3 * HW), dproj3.reshape(dproj.shape), acc


def _hgrn_inputs(zq, zf, lb):
    sg = _sig(zf)
    sgn = _sig(-zf)
    pos = lb > 0.0
    lbp = jnp.where(pos, lb, 0.0)
    fpos = lbp + (1.0 - lbp) * sg
    lf = jnp.where(pos, jnp.log(jnp.where(pos, fpos, 1.0)), _logsig(zf))
    k = (1.0 - lbp) * sgn
    q = _silu(zq) * Q_SCALE
    return q, k, lf, sg, sgn, pos, lbp, fpos


def _hgrn_consts():
    i3, j3 = _iota2((SUB, SUB, HD), 0), _iota2((SUB, SUB, HD), 1)
    return i3 >= j3


def _sum_j(x):
    return jnp.sum(x.reshape(SUB, SUB, HD), axis=1)


def _sum_i(x):
    return jnp.sum(x.reshape(SUB, SUB, HD), axis=0)


def _pairs(a, b):
    return (a[:, None, :] * b[None, :, :]).reshape(SUB * SUB, HD)


def _hgrn_sub(q, k, v, bc, st, consts):
    mask3 = consts
    bl = bc[SUB - 1:SUB, :]
    p3 = jnp.where(mask3, jnp.exp(jnp.where(mask3, bc[:, None, :] - bc[None, :, :], 0.0)), 0.0).reshape(SUB * SUB, HD)
    x = _pairs(q, k) * p3
    srep = _rs(x)
    vt = jnp.broadcast_to(v[None, :, :], (SUB, SUB, HD)).reshape(SUB * SUB, HD)
    eb = jnp.exp(bc)
    qe = q * eb
    o = _hnt(qe, st) + _sum_j(_rr(srep) * _rr(vt))
    ek = jnp.exp(bl - bc)
    kd = k * ek
    ebl = jnp.exp(bl)
    st1 = st * ebl + _htn(v, kd)
    return o, st1, dict(bc=bc, p3=p3, srep=srep, vt=vt, eb=eb, qe=qe, ek=ek, kd=kd, ebl=ebl)


def _hgrn_fwd(proj, lb, lay, nb, nc):
    n = proj.shape[0]
    tp = n // nb
    cbb = lay.c_b // (3 * HW)
    gb = _seqs_per_step(nb)

    def body(z_ref, lb_ref, o_ref, so_ref, s_ref):
        @pl.when(pl.program_id(1) == 0)
        def _():
            s_ref[...] = jnp.zeros_like(s_ref)

        consts = _hgrn_consts()
        for j in range(gb):
            outs = []
            for h in range(NH):
                hs = slice(h * HD, (h + 1) * HD)
                q, k, lf = _hgrn_inputs(z_ref[j, :, hs], z_ref[j, :, HW + h * HD:HW + (h + 1) * HD], lb_ref[:, hs])[:3]
                v = z_ref[j, :, 2 * HW + h * HD:2 * HW + (h + 1) * HD]
                st = s_ref[j, h]
                so_ref[j, h] = st
                bc = _scan_rows(lf, SUB)
                oh = []
                for s in range(CH // SUB):
                    rs = slice(s * SUB, (s + 1) * SUB)
                    o, st, _ = _hgrn_sub(q[rs], k[rs], v[rs], bc[rs], st, consts)
                    oh.append(o)
                s_ref[j, h] = st
                outs.append(jnp.concatenate(oh, axis=0))
            o_ref[j] = jnp.concatenate(outs, axis=-1)

    o, st = pl.pallas_call(
        body, grid=(nb // gb, nc),
        in_specs=[pl.BlockSpec((gb, CH, 3 * HW), lambda b, c: (b, c, cbb)), pl.BlockSpec((1, HW), lambda b, c: (0, 0))],
        out_specs=[pl.BlockSpec((gb, CH, HW), lambda b, c: (b, c, 0)),
                   pl.BlockSpec((gb, None, NH, HD, HD), lambda b, c: (b, c, 0, 0, 0))],
        out_shape=[SDS((nb, tp, HW), F32), SDS((nb, nc, NH, HD, HD), F32)],
        scratch_shapes=[pltpu.VMEM((gb, NH, HD, HD), F32)], compiler_params=_params(2), name="hgrn_fwd")(
            proj.reshape(nb, tp, -1), lb)
    return o.reshape(n, HW), st


def _hgrn_bwd(proj, lb, states, do, dproj, lay, nb, nc):
    n = proj.shape[0]
    tp = n // nb
    cbb = lay.c_b // (3 * HW)
    nsub = CH // SUB
    gb = _seqs_per_step(nb)

    def body(z_ref, lb_ref, st_ref, do_ref, dp_in, dz_ref, acc_ref, ds_ref):
        ci = pl.program_id(1)

        @pl.when(ci == 0)
        def _():
            ds_ref[...] = jnp.zeros_like(ds_ref)

        @pl.when((ci == 0) & (pl.program_id(0) == 0))
        def _():
            acc_ref[...] = jnp.zeros_like(acc_ref)

        upd = jnp.zeros((8, HW), F32)
        for j in range(gb):
            upd = upd + one_seq(j, ci, z_ref, lb_ref, st_ref, do_ref, dz_ref, ds_ref)
        acc_ref[...] += upd

    def one_seq(j, ci, z_ref, lb_ref, st_ref, do_ref, dz_ref, ds_ref):
        consts = _hgrn_consts()
        row = _iota2((CH, 1), 0)
        valid = (row >= PAD) | (ci < nc - 1)
        lastrow = _iota2((SUB, 1), 0) == SUB - 1
        dzq, dzf, dzi, dlbs = [], [], [], []
        for h in range(NH):
            hs = slice(h * HD, (h + 1) * HD)
            zq, zf = z_ref[j, :, hs], z_ref[j, :, HW + h * HD:HW + (h + 1) * HD]
            q, k, lf, sg, sgn, pos, lbp, fpos = _hgrn_inputs(zq, zf, lb_ref[:, hs])
            v = z_ref[j, :, 2 * HW + h * HD:2 * HW + (h + 1) * HD]
            doh = do_ref[j, :, hs]
            sts, fw = [st_ref[j, h]], []
            bc = _scan_rows(lf, SUB)
            for s in range(nsub):
                rs = slice(s * SUB, (s + 1) * SUB)
                _, st1, c = _hgrn_sub(q[rs], k[rs], v[rs], bc[rs], sts[-1], consts)
                sts.append(st1)
                fw.append(c)
            dst = ds_ref[j, h]
            dq_l, dk_l, dv_l, dlf_l = [None] * nsub, [None] * nsub, [None] * nsub, [None] * nsub
            for s in reversed(range(nsub)):
                rs = slice(s * SUB, (s + 1) * SUB)
                c, st = fw[s], sts[s]
                qs, ks, vs, dos = q[rs], k[rs], v[rs], doh[rs]
                dqe = _hnn(dos, st)
                dkd = _hnn(vs, dst)
                dsrep = _rs(_pairs(_rr(dos), _rr(vs)))
                w = dsrep * c["p3"]
                kt = jnp.broadcast_to(ks[None, :, :], (SUB, SUB, HD)).reshape(SUB * SUB, HD)
                qt = jnp.broadcast_to(qs[:, None, :], (SUB, SUB, HD)).reshape(SUB * SUB, HD)
                dq_i = _sum_j(w * kt)
                dk_i = _sum_i(w * qt)
                dot = jnp.broadcast_to(_rr(dos)[:, None, :], (SUB, SUB, HD)).reshape(SUB * SUB, HD)
                dvv = _sum_i(_rr(c["srep"]) * dot) + _hnt(c["kd"], dst)
                t_kd = dkd * c["kd"]
                dbc = dqe * c["qe"] - t_kd + qs * dq_i - ks * dk_i
                tail = jnp.sum(t_kd, axis=0, keepdims=True) + c["ebl"] * jnp.sum(st * dst, axis=0, keepdims=True)
                dbc = dbc + jnp.where(lastrow, tail, 0.0)
                dlf_l[s] = dbc
                dq_l[s] = dq_i + dqe * c["eb"]
                dk_l[s] = dk_i + dkd * c["ek"]
                dv_l[s] = dvv
                dst = _htn(dos, c["qe"]) + dst * c["ebl"]
            ds_ref[j, h] = dst
            dq, dk, dv, dbc = (jnp.concatenate(t, axis=0) for t in (dq_l, dk_l, dv_l, dlf_l))
            dlf = _scan_rows(dbc, SUB, reverse=True)
            dlft = dlf - dk * (1.0 - k)
            dlf_dz = jnp.where(pos, (1.0 - lbp) * sg * sgn / jnp.where(pos, fpos, 1.0), sgn)
            dlf_dlb = jnp.where(pos, sgn / jnp.where(pos, fpos, 1.0), 0.0)
            dzq.append(dq * Q_SCALE * _dsilu(zq))
            dzf.append(dlft * dlf_dz)
            dzi.append(dv)
            dlbs.append(jnp.sum(jnp.where(valid, dlft * dlf_dlb, 0.0), axis=0, keepdims=True))
        dz_ref[j] = jnp.concatenate(dzq + dzf + dzi, axis=-1).astype(dz_ref.dtype)
        return jnp.where(_iota2((8, HW), 0) == 0, jnp.concatenate(dlbs, axis=-1), 0.0)

    rc = lambda c: nc - 1 - c
    dproj3, acc = pl.pallas_call(
        body, grid=(nb // gb, nc),
        in_specs=[pl.BlockSpec((gb, CH, 3 * HW), lambda b, c: (b, rc(c), cbb)), pl.BlockSpec((1, HW), lambda b, c: (0, 0)),
                  pl.BlockSpec((gb, None, NH, HD, HD), lambda b, c: (b, rc(c), 0, 0, 0)),
                  pl.BlockSpec((gb, CH, HW), lambda b, c: (b, rc(c), 0)), pl.BlockSpec(memory_space=pl.ANY)],
        out_specs=[pl.BlockSpec((gb, CH, 3 * HW), lambda b, c: (b, rc(c), cbb)), pl.BlockSpec((8, HW), lambda b, c: (0, 0))],
        out_shape=[SDS((nb, tp, dproj.shape[1]), dproj.dtype), SDS((8, HW), F32)],
        input_output_aliases={4: 0},
        scratch_shapes=[pltpu.VMEM((gb, NH, HD, HD), F32)], compiler_params=_params(2), name="hgrn_bwd")(
            proj.reshape(nb, tp, -1), lb, states, do.reshape(nb, tp, HW), dproj.reshape(nb, tp, -1))
    return dproj3.reshape(dproj.shape), acc


def _gated_norm(o, z, gamma):
    ys, ns, rs = [], [], []
    for h in range(NH):
        hs = slice(h * HD, (h + 1) * HD)
        oh = o[:, hs]
        r = lax.rsqrt(jnp.mean(oh * oh, axis=-1, keepdims=True) + EPS)
        nh = oh * r
        ys.append(nh * gamma * _silu(z[:, hs]))
        ns.append(nh)
        rs.append(r)
    return jnp.concatenate(ys, axis=-1), ns, rs


def _merge_fwd(h, oa, ob, proj, ga, gb, wa, wb, wo, lay):
    n, d = h.shape
    tm = _tile(n, 384)
    wm = lay.wm

    def body(h_ref, oa_ref, ob_ref, p_ref, ga_ref, gb_ref, wa_ref, wb_ref, wo_ref, out_ref):
        ya, _, _ = _gated_norm(oa_ref[...], p_ref[:, 0:HW], ga_ref[...])
        yb, _, _ = _gated_norm(ob_ref[...], p_ref[:, HW:2 * HW], gb_ref[...])
        ya2 = _bnn(ya, wa_ref[...])
        yb2 = _bnn(yb, wb_ref[...])
        mixed = _sig(p_ref[:, 2 * HW:2 * HW + d]) * ya2 + _sig(p_ref[:, 2 * HW + d:2 * HW + 2 * d]) * yb2
        out_ref[...] = h_ref[...] + _bnn(mixed, wo_ref[...])

    full = lambda shape: pl.BlockSpec(shape, lambda i: (0, 0))
    return pl.pallas_call(
        body, grid=(n // tm,),
        in_specs=[pl.BlockSpec((tm, d), lambda i: (i, 0)), pl.BlockSpec((tm, HW), lambda i: (i, 0)),
                  pl.BlockSpec((tm, HW), lambda i: (i, 0)), pl.BlockSpec((tm, wm), lambda i: (i, 0)),
                  full((1, HD)), full((1, HD)), full((HW, d)), full((HW, d)), full((d, d))],
        out_specs=pl.BlockSpec((tm, d), lambda i: (i, 0)), out_shape=SDS((n, d), F32),
        compiler_params=_params(1), name="merge_fwd")(h, oa, ob, proj, ga, gb, wa, wb, wo)


def _gated_norm_bwd(dy, o, z, gamma):
    dos, dzs = [], []
    dgam = jnp.zeros((1, HD), F32)
    for h in range(NH):
        hs = slice(h * HD, (h + 1) * HD)
        oh, zh, dyh = o[:, hs], z[:, hs], dy[:, hs]
        r = lax.rsqrt(jnp.mean(oh * oh, axis=-1, keepdims=True) + EPS)
        nh = oh * r
        dzs.append(dyh * nh * gamma * _dsilu(zh))
        dng = dyh * _silu(zh)
        dgam = dgam + jnp.sum(dng * nh, axis=0, keepdims=True)
        dn = dng * gamma
        dos.append(r * (dn - nh * jnp.mean(dn * nh, axis=-1, keepdims=True)))
    return jnp.concatenate(dos, axis=-1), jnp.concatenate(dzs, axis=-1), dgam


def _merge_bwd(dhn, oa, ob, proj, ga, gb, wa, wb, wo, lay, tp):
    n, d = dhn.shape
    tm = _tile(n, 256)
    wm = lay.wm

    def body(dh_ref, oa_ref, ob_ref, p_ref, ga_ref, gb_ref, wa_ref, wb_ref, wo_ref,
             dp_ref, doa_ref, dob_ref, dwa_ref, dwb_ref, dwo_ref, dga_ref, dgb_ref):
        i = pl.program_id(0)

        @pl.when(i == 0)
        def _():
            for r in (dwa_ref, dwb_ref, dwo_ref, dga_ref, dgb_ref):
                r[...] = jnp.zeros_like(r)

        dh = jnp.where(_row_valid(tm, tp, i * tm), dh_ref[...], 0.0)
        oa, ob = oa_ref[...], ob_ref[...]
        za, zb = p_ref[:, 0:HW], p_ref[:, HW:2 * HW]
        gta, gtb = p_ref[:, 2 * HW:2 * HW + d], p_ref[:, 2 * HW + d:2 * HW + 2 * d]
        ya, _, _ = _gated_norm(oa, za, ga_ref[...])
        yb, _, _ = _gated_norm(ob, zb, gb_ref[...])
        ya2 = _bnn(ya, wa_ref[...])
        yb2 = _bnn(yb, wb_ref[...])
        sa, sb = _sig(gta), _sig(gtb)
        mixed = sa * ya2 + sb * yb2
        dmixed = _bnt(dh, wo_ref[...])
        dwo_ref[...] += _btn(mixed, dh)
        dya2 = dmixed * sa
        dyb2 = dmixed * sb
        dwa_ref[...] += _btn(ya, dya2)
        dwb_ref[...] += _btn(yb, dyb2)
        doa, dza, dga = _gated_norm_bwd(_bnt(dya2, wa_ref[...]), oa, za, ga_ref[...])
        dob, dzb, dgb = _gated_norm_bwd(_bnt(dyb2, wb_ref[...]), ob, zb, gb_ref[...])
        dga_ref[...] += dga
        dgb_ref[...] += dgb
        doa_ref[...] = doa
        dob_ref[...] = dob
        dt = dp_ref.dtype
        dp_ref[:, 0:HW] = dza.astype(dt)
        dp_ref[:, HW:2 * HW] = dzb.astype(dt)
        dp_ref[:, 2 * HW:2 * HW + d] = (dmixed * ya2 * sa * (1.0 - sa)).astype(dt)
        dp_ref[:, 2 * HW + d:2 * HW + 2 * d] = (dmixed * yb2 * sb * (1.0 - sb)).astype(dt)

    full = lambda shape: pl.BlockSpec(shape, lambda i: (0, 0))
    rows = lambda w: pl.BlockSpec((tm, w), lambda i: (i, 0))
    return pl.pallas_call(
        body, grid=(n // tm,),
        in_specs=[rows(d), rows(HW), rows(HW), rows(wm), full((1, HD)), full((1, HD)), full((HW, d)), full((HW, d)), full((d, d))],
        out_specs=[rows(wm), rows(HW), rows(HW), full((HW, d)), full((HW, d)), full((d, d)), full((1, HD)), full((1, HD))],
        out_shape=[SDS((n, lay.pw), BF16), SDS((n, HW), F32), SDS((n, HW), F32), SDS((HW, d), F32), SDS((HW, d), F32),
                   SDS((d, d), F32), SDS((1, HD), F32), SDS((1, HD), F32)],
        compiler_params=_params(1), name="merge_bwd")(dhn, oa, ob, proj, ga, gb, wa, wb, wo)


def _loss_head(h, target, fw, nb, tp):
    n, d = h.shape
    tr = _tile(tp, 768)
    nr = tp // tr

    def body(h_ref, t_ref, fw_ref, lp_ref, dh_ref, dfw_ref):
        b, i = pl.program_id(0), pl.program_id(1)

        @pl.when((b == 0) & (i == 0))
        def _():
            dfw_ref[...] = jnp.zeros_like(dfw_ref)

        x = h_ref[...]
        r = lax.rsqrt(jnp.mean(x * x, axis=-1, keepdims=True) + EPS)
        xh = x * r
        live = i * tr + _iota2((tr, 1), 0) >= CH
        err = jnp.where(live, xh * fw_ref[...] - t_ref[...], 0.0)
        lp_ref[...] = jnp.zeros_like(lp_ref) + 0.5 * jnp.sum(_rs(err * err), axis=0, keepdims=True) / d
        dy = err / d
        dfw_ref[...] += jnp.sum(dy * xh, axis=0, keepdims=True)
        dxh = dy * fw_ref[...]
        dh_ref[...] = r * (dxh - xh * jnp.mean(dxh * xh, axis=-1, keepdims=True))

    rows = pl.BlockSpec((tr, d), lambda b, i: (b * nr + i, 0))
    return pl.pallas_call(
        body, grid=(nb, nr), in_specs=[rows, rows, pl.BlockSpec((1, d), lambda b, i: (0, 0))],
        out_specs=[pl.BlockSpec((8, HD), lambda b, i: (b * nr + i, 0)), rows, pl.BlockSpec((1, d), lambda b, i: (0, 0))],
        out_shape=[SDS((nb * nr * 8, HD), F32), SDS((n, d), F32), SDS((1, d), F32)],
        compiler_params=_params(2), name="loss_head")(h, target, fw)


def _lb_fwd(lb):
    def body(x_ref, o_ref):
        x = x_ref[...]
        mx = jnp.max(x, axis=0, keepdims=True)
        e = jnp.exp(x - mx)
        sm = e / jnp.sum(e, axis=0, keepdims=True)
        run = jnp.zeros((1, HW), F32)
        for l in range(DEPTH):
            run = run + sm[l:l + 1, :]
            o_ref[l:l + 1, :] = run - sm[0:1, :]

    return pl.pallas_call(body, out_shape=SDS(lb.shape, F32), name="lb_fwd")(lb)


def _lb_bwd(lb, dlb_all):
    def body(x_ref, d_ref, o_ref):
        x = x_ref[...]
        dl = d_ref[...]
        mx = jnp.max(x, axis=0, keepdims=True)
        e = jnp.exp(x - mx)
        sm = e / jnp.sum(e, axis=0, keepdims=True)
        tot = jnp.sum(dl, axis=0, keepdims=True)
        dsm = []
        run = tot
        for l in range(DEPTH):
            dsm.append(run - (tot if l == 0 else 0.0))
            run = run - dl[l:l + 1, :]
        inner = sum(sm[l:l + 1, :] * dsm[l] for l in range(DEPTH))
        for l in range(DEPTH):
            o_ref[l:l + 1, :] = sm[l:l + 1, :] * (dsm[l] - inner)

    return pl.pallas_call(body, out_shape=SDS(lb.shape, F32), name="lb_bwd")(lb, dlb_all)


def _adamw(g, w, m, v):
    r, c = g.shape
    tr = _tile(r, 264)
    c1 = 1.0 / (1.0 - ADAM_B1 ** ADAM_STEP)
    c2 = 1.0 / (1.0 - ADAM_B2 ** ADAM_STEP)

    def body(g_ref, w_ref, m_ref, v_ref, d_ref, mo_ref, vo_ref):
        gg = g_ref[...]
        mn = ADAM_B1 * m_ref[...] + (1.0 - ADAM_B1) * gg
        vn = ADAM_B2 * v_ref[...] + (1.0 - ADAM_B2) * gg * gg
        d_ref[...] = -ADAM_LR * ((mn * c1) / (jnp.sqrt(vn * c2) + ADAM_EPS) + ADAM_WD * w_ref[...])
        mo_ref[...] = mn
        vo_ref[...] = vn

    spec = pl.BlockSpec((tr, c), lambda i: (i, 0))
    return pl.pallas_call(body, grid=(r // tr,), in_specs=[spec] * 4, out_specs=[spec] * 3, out_shape=[SDS(g.shape, F32)] * 3,
                          compiler_params=_params(1), name="adamw")(g, w, m, v)


def _tile16(n, target):
    return _tile(n // 2, target // 2) * 2 if n % 16 == 0 else _tile(n, target)


def _add_cores(g, got, core):
    k, r, c = got.shape
    tr = _tile16(r, 264)

    def body(c_ref, a_ref, b_ref, o_ref):
        o_ref[...] = (a_ref[...] + b_ref[...].astype(F32)).astype(o_ref.dtype)

    spec = pl.BlockSpec((None, tr, c), lambda s, i, cr: (s, i, 0))
    return pl.pallas_call(
        body, grid_spec=pltpu.PrefetchScalarGridSpec(
            num_scalar_prefetch=1, grid=(k, r // tr),
            in_specs=[pl.BlockSpec((None, None, tr, c), lambda s, i, cr: (cr[0], s, i, 0)), spec], out_specs=spec),
        out_shape=SDS(got.shape, got.dtype), compiler_params=_params(2), name="add_cores")(core, g, got)


def _sum_chips(parts, own, chip, core):
    k, r, c = parts.shape
    tr = _tile16(r, 264)

    def body(chip_ref, core_ref, *refs):
        part_refs, own_ref, o_ref = refs[:k], refs[k], refs[k + 1]
        mine = own_ref[...].astype(F32)
        acc = None
        for s in range(k):
            term = jnp.where(chip_ref[0] == s, mine, part_refs[s][...].astype(F32))
            acc = term if acc is None else acc + term
        o_ref[...] = acc

    def other(s):
        return pl.BlockSpec((None, tr, c), lambda i, ch, co: (jnp.where(ch[0] == s, (s + 1) % k, s), i, 0))

    return pl.pallas_call(
        body, grid_spec=pltpu.PrefetchScalarGridSpec(
            num_scalar_prefetch=2, grid=(r // tr,),
            in_specs=[other(s) for s in range(k)] + [pl.BlockSpec((None, tr, c), lambda i, ch, co: (ch[0], i, 0))],
            out_specs=pl.BlockSpec((None, tr, c), lambda i, ch, co: (co[0], i, 0))),
        out_shape=SDS((2, r, c), F32), compiler_params=_params(1), name="sum_chips")(chip, core, *([parts] * k), own)


def _meta_grad(dh, nb, nc):
    d = dh.shape[1]

    def body(x_ref, o_ref):
        @pl.when(pl.program_id(0) == 0)
        def _():
            o_ref[...] = jnp.zeros_like(o_ref)

        o_ref[...] += x_ref[PAD:CH, :]

    return pl.pallas_call(body, grid=(nb,), in_specs=[pl.BlockSpec((CH, d), lambda b: (b * nc, 0))],
                          out_specs=pl.BlockSpec((N_META, d), lambda b: (0, 0)), out_shape=SDS((N_META, d), F32),
                          compiler_params=_params(1), name="meta_grad")(dh)


ANY = pl.BlockSpec(memory_space=pl.ANY)


def _place():
    x, y, c = lax.axis_index("x"), lax.axis_index("y"), lax.axis_index("c")
    chips = [(1 - x, y), (x, 1 - y), (1 - x, 1 - y)]
    return x, y, c, chips


def _remote(src, dst, send_sems, recv_sems, k, to):
    return pltpu.make_async_remote_copy(src_ref=src, dst_ref=dst, send_sem=send_sems.at[k], recv_sem=recv_sems.at[k],
                                        device_id=to, device_id_type=MESH)


def _gather_weights(pbs, ps):
    nt = len(pbs)

    def body(*refs):
        pb_refs, ps_ref, gb_refs, gs_ref = refs[:nt], refs[nt], refs[nt + 1:2 * nt + 1], refs[2 * nt + 1]
        send_sems, recv_sems, local_sems = refs[2 * nt + 2:]
        x, y, c, chips = _place()
        s = 2 * x + y
        sib = (x, y, 1 - c)
        l1 = pltpu.make_async_copy(ps_ref, gs_ref.at[s], local_sems.at[0])
        l1.start()
        sends = []
        for k, (px, py) in enumerate(chips):
            for t in range(nt):
                sends.append(_remote(pb_refs[t].at[c], gb_refs[t].at[s, c], send_sems, recv_sems, 6 * t + k, (px, py, c)))
            sends.append(_remote(ps_ref, gs_ref.at[s], send_sems, recv_sems, 6 * nt + k, (px, py, c)))
        for cp in sends:
            cp.start()
        for k, (px, py) in enumerate(chips):
            sk = 2 * px + py
            for t in range(nt):
                _remote(pb_refs[t].at[c], gb_refs[t].at[sk, c], send_sems, recv_sems, 6 * t + k, sib).wait_recv()
                fwd = _remote(gb_refs[t].at[sk, c], gb_refs[t].at[sk, c], send_sems, recv_sems, 6 * t + 3 + k, sib)
                fwd.start()
                sends.append(fwd)
        for k, (px, py) in enumerate(chips):
            sk = 2 * px + py
            for t in range(nt):
                _remote(pb_refs[t].at[c], gb_refs[t].at[sk, 1 - c], send_sems, recv_sems, 6 * t + 3 + k, sib).wait_recv()
            _remote(ps_ref, gs_ref.at[sk], send_sems, recv_sems, 6 * nt + k, sib).wait_recv()
        for cp in sends:
            cp.wait_send()
        l1.wait()

    nsem = 6 * nt + 3
    out = pl.pallas_call(
        body, in_specs=[ANY] * (nt + 1), out_specs=[ANY] * (nt + 1),
        out_shape=[SDS((4,) + pb.shape, pb.dtype) for pb in pbs] + [SDS((4,) + ps.shape, ps.dtype)],
        scratch_shapes=[pltpu.SemaphoreType.DMA((nsem,)), pltpu.SemaphoreType.DMA((nsem,)), pltpu.SemaphoreType.DMA((1,))],
        name="gather_weights")(*pbs, ps)
    return out[:nt], out[nt]


def _contain(wpad, shift):
    r, cw = wpad.shape
    tr = _tile16(r, 256)

    def body(n_ref, x_ref, o_ref):
        o_ref[...] = pltpu.roll(x_ref[...], n_ref[0], axis=1).astype(o_ref.dtype)

    spec = pl.BlockSpec((tr, cw), lambda i, n: (i, 0))
    return pl.pallas_call(
        body, grid_spec=pltpu.PrefetchScalarGridSpec(num_scalar_prefetch=1, grid=(r // tr,), in_specs=[spec], out_specs=spec),
        out_shape=SDS((r, cw), BF16), compiler_params=_params(1), name="contain")(shift, wpad)


def _place_own(gb, pb, chip):
    _, _, r, c = gb.shape
    tr = _tile16(r, 1100)

    def body(s_ref, p_ref, g_in, o_ref):
        o_ref[...] = p_ref[...]

    return pl.pallas_call(
        body, grid_spec=pltpu.PrefetchScalarGridSpec(
            num_scalar_prefetch=1, grid=(2, r // tr),
            in_specs=[pl.BlockSpec((None, tr, c), lambda h, i, s: (h, i, 0)), ANY],
            out_specs=pl.BlockSpec((None, None, tr, c), lambda h, i, s: (s[0], h, i, 0))),
        out_shape=SDS(gb.shape, gb.dtype), input_output_aliases={2: 0}, compiler_params=_params(2),
        name="place_own")(chip, pb, gb)


def _sem_scratch(n_remote, n_local):
    return [pltpu.SemaphoreType.DMA((n_remote,)), pltpu.SemaphoreType.DMA((n_remote,)), pltpu.SemaphoreType.DMA((n_local,))]


def _swap_halves(sends):
    nt = len(sends)

    def body(*refs):
        s_refs, got_refs = refs[:nt], refs[nt:2 * nt]
        send_sems, recv_sems = refs[2 * nt:]
        x, y, c, _ = _place()
        sib = (x, y, 1 - c)
        remote = [_remote(s_refs[t].at[1 - c, s], got_refs[t].at[s], send_sems, recv_sems, 4 * t + s, sib)
                  for t in range(nt) for s in range(4)]
        for cp in remote:
            cp.start()
        for cp in remote:
            cp.wait()

    return pl.pallas_call(
        body, in_specs=[ANY] * nt, out_specs=[ANY] * nt, out_shape=[SDS(g.shape[1:], g.dtype) for g in sends],
        scratch_shapes=[pltpu.SemaphoreType.DMA((4 * nt,)), pltpu.SemaphoreType.DMA((4 * nt,))], name="swap_halves")(*sends)


def _scatter_chip_sums(parts):
    nt = len(parts)

    def body(*refs):
        a_refs, r_refs = refs[:nt], refs[nt:2 * nt]
        send_sems, recv_sems = refs[2 * nt:]
        x, y, c, chips = _place()
        s = 2 * x + y
        sends = [_remote(a_refs[t].at[2 * px + py], r_refs[t].at[s], send_sems, recv_sems, 3 * t + k, (px, py, c))
                 for t in range(nt) for k, (px, py) in enumerate(chips)]
        for cp in sends:
            cp.start()
        for t in range(nt):
            for k, (px, py) in enumerate(chips):
                _remote(a_refs[t].at[s], r_refs[t].at[2 * px + py], send_sems, recv_sems, 3 * t + k, (px, py, c)).wait_recv()
        for cp in sends:
            cp.wait_send()

    return pl.pallas_call(
        body, in_specs=[ANY] * nt, out_specs=[ANY] * nt, out_shape=[SDS(a.shape, a.dtype) for a in parts],
        scratch_shapes=[pltpu.SemaphoreType.DMA((3 * nt,)), pltpu.SemaphoreType.DMA((3 * nt,))],
        name="scatter_chip_sums")(*parts)


def _join_halves(fs):
    nt = len(fs)

    def body(*refs):
        f_refs = refs[nt:2 * nt]
        send_sems, recv_sems = refs[2 * nt:]
        x, y, c, _ = _place()
        sib = (x, y, 1 - c)
        sends = [_remote(f_refs[t].at[c], f_refs[t].at[c], send_sems, recv_sems, t, sib) for t in range(nt)]
        for cp in sends:
            cp.start()
        for t in range(nt):
            _remote(f_refs[t].at[c], f_refs[t].at[1 - c], send_sems, recv_sems, t, sib).wait_recv()
        for cp in sends:
            cp.wait_send()

    return pl.pallas_call(
        body, in_specs=[ANY] * nt, out_specs=[ANY] * nt, out_shape=[SDS(f.shape, f.dtype) for f in fs],
        input_output_aliases={t: t for t in range(nt)},
        scratch_shapes=[pltpu.SemaphoreType.DMA((nt,)), pltpu.SemaphoreType.DMA((nt,))], name="join_halves")(*fs)


def _uncontain(cont, n_head, width):
    r, cw = cont.shape
    tr = _tile(r, 256)

    def body(n_ref, x_ref, o_ref):
        o_ref[...] = pltpu.roll(x_ref[...], n_ref[0], axis=1)[:, :width]

    return pl.pallas_call(
        body, grid_spec=pltpu.PrefetchScalarGridSpec(
            num_scalar_prefetch=1, grid=(r // tr,), in_specs=[pl.BlockSpec((tr, cw), lambda i, n: (i, 0))],
            out_specs=pl.BlockSpec((tr, width), lambda i, n: (i, 0))),
        out_shape=SDS((r, width), F32), compiler_params=_params(1), name="uncontain")(n_head, cont)


WEIGHTS = ("meta_tokens", "norm_w", "w_in", "conv_w", "a_log", "dt_bias", "gnorm_a", "gnorm_b", "hgrn_lower_bounds",
           "w_branch_a", "w_branch_b", "w_out", "final_norm_w")
SHARD_AXIS = {"meta_tokens": 1, "w_in": 2, "conv_w": 2, "w_branch_a": 2, "w_branch_b": 2, "w_out": 1}
FLAT_C = 1024


def _flat(parts, rows, cols=FLAT_C):
    v = jnp.concatenate([p.reshape(-1) for p in parts])
    return jnp.pad(v, (0, rows * cols - v.shape[0])).reshape(rows, cols)


def _local_step(x, target, w, lay):
    nb, seq, d = x.shape
    tp = CH + seq
    nc = tp // CH
    n = nb * tp
    e_mat, s_mat = _gate_consts()
    lb_all = _lb_fwd(w["hgrn_lower_bounds"])
    h = jnp.concatenate([jnp.zeros((nb, PAD, d), F32), jnp.broadcast_to(w["meta_tokens"][None], (nb, N_META, d)), x],
                        axis=1).reshape(n, d)
    rep = lambda a: jnp.repeat(a, HD)[None, :]
    saved = []
    for l in range(DEPTH):
        nw = w["norm_w"][l][None, :]
        proj, xn = _norm_proj_fwd(h, nw, w["w_in"][l])
        qkv = _gdn_prep_fwd(proj, w["conv_w"][l], lay, nb, tp)
        alog, dtb = rep(w["a_log"][l]), rep(w["dt_bias"][l])
        oa, sa, sva = _gdn_fwd(qkv, proj, e_mat, alog, dtb, lay, nb, nc)
        lbl = lb_all[l][None, :]
        ob, sb = _hgrn_fwd(proj, lbl, lay, nb, nc)
        ga, gb = w["gnorm_a"][l][None, :], w["gnorm_b"][l][None, :]
        hn = _merge_fwd(h, oa, ob, proj, ga, gb, w["w_branch_a"][l], w["w_branch_b"][l], w["w_out"][l], lay)
        saved.append((h, nw, proj, qkv, alog, dtb, oa, sa, lbl, ob, sb, ga, gb, xn, sva))
        h = hn
    target_p = jnp.pad(target, ((0, 0), (CH, 0), (0, 0))).reshape(n, d)
    lp, dh, dfw = _loss_head(h, target_p, w["final_norm_w"][None, :], nb, tp)
    loss = jnp.sum(lp[::8, 0])
    g = {n_: [None] * DEPTH for n_ in WEIGHTS}
    dlb_all = [None] * DEPTH
    for l in reversed(range(DEPTH)):
        h, nw, proj, qkv, alog, dtb, oa, sa, lbl, ob, sb, ga, gb, xn, sva = saved[l]
        dproj, doa, dob, dwa, dwb, dwo, dga, dgb = _merge_bwd(dh, oa, ob, proj, ga, gb, w["w_branch_a"][l],
                                                             w["w_branch_b"][l], w["w_out"][l], lay, tp)
        dproj, acc_b = _hgrn_bwd(proj, lbl, sb, dob, dproj, lay, nb, nc)
        dqkv, dproj, acc_a = _gdn_bwd(qkv, proj, e_mat, s_mat, alog, dtb, sa, sva, doa, dproj, lay, nb, nc)
        dproj, dconv = _gdn_prep_bwd(proj, w["conv_w"][l], dqkv, dproj, lay, nb, tp)
        dh, dnw = _proj_bwd_dx(dproj, w["w_in"][l], h, nw, dh, tp)
        g["w_in"][l] = _proj_bwd_dw(dproj, xn, tp)
        g["norm_w"][l] = dnw[0]
        g["conv_w"][l] = dconv
        g["a_log"][l] = acc_a[0, ::HD]
        g["dt_bias"][l] = acc_a[1, ::HD]
        g["gnorm_a"][l], g["gnorm_b"][l] = dga[0], dgb[0]
        g["w_branch_a"][l], g["w_branch_b"][l], g["w_out"][l] = dwa, dwb, dwo
        dlb_all[l] = acc_b[0]
    grads = {n_: jnp.stack(v) for n_, v in g.items() if v[0] is not None}
    grads["hgrn_lower_bounds"] = _lb_bwd(w["hgrn_lower_bounds"], jnp.stack(dlb_all))
    grads["final_norm_w"] = dfw[0]
    grads["meta_tokens"] = _meta_grad(dh, nb, nc)
    grad_x = dh.reshape(nb, tp, d)[:, CH:, :]
    return loss, grad_x, grads


def kernel(x, meta_tokens, norm_w, w_in, conv_w, a_log, dt_bias, gnorm_a, gnorm_b, hgrn_lower_bounds, w_branch_a, w_branch_b, w_out, final_norm_w, loss_target, m_meta_tokens, m_norm_w, m_w_in, m_conv_w, m_a_log, m_dt_bias, m_gnorm_a, m_gnorm_b, m_hgrn_lower_bounds, m_w_branch_a, m_w_branch_b, m_w_out, m_final_norm_w, v_meta_tokens, v_norm_w, v_w_in, v_conv_w, v_a_log, v_dt_bias, v_gnorm_a, v_gnorm_b, v_hgrn_lower_bounds, v_w_branch_a, v_w_branch_b, v_w_out, v_final_norm_w):
    wl = dict(meta_tokens=meta_tokens, norm_w=norm_w, w_in=w_in, conv_w=conv_w, a_log=a_log, dt_bias=dt_bias, gnorm_a=gnorm_a,
              gnorm_b=gnorm_b, hgrn_lower_bounds=hgrn_lower_bounds, w_branch_a=w_branch_a, w_branch_b=w_branch_b, w_out=w_out,
              final_norm_w=final_norm_w)
    ml = dict(zip(WEIGHTS, (m_meta_tokens, m_norm_w, m_w_in, m_conv_w, m_a_log, m_dt_bias, m_gnorm_a, m_gnorm_b,
                            m_hgrn_lower_bounds, m_w_branch_a, m_w_branch_b, m_w_out, m_final_norm_w)))
    vl = dict(zip(WEIGHTS, (v_meta_tokens, v_norm_w, v_w_in, v_conv_w, v_a_log, v_dt_bias, v_gnorm_a, v_gnorm_b,
                            v_hgrn_lower_bounds, v_w_branch_a, v_w_branch_b, v_w_out, v_final_norm_w)))
    d = x.shape[2]
    lay = _Layout(d)
    nchip = 4

    big = ("w_in", "w_branch_a", "w_branch_b", "w_out")
    small = ("conv_w", "meta_tokens")
    table, heads, cw = lay.pieces(nchip)
    sw = wl["w_in"].shape[2]
    chip_id = (2 * lax.axis_index("x") + lax.axis_index("y")).astype(jnp.int32)
    n_head = sum(jnp.where(chip_id == s, heads[s], 0) for s in range(nchip)).astype(jnp.int32)
    w_pad = jnp.pad(wl["w_in"], ((0, 0), (0, 0), (0, cw - sw))).reshape(DEPTH * d, cw)
    shift = jnp.where(n_head == 0, 0, cw - n_head).astype(jnp.int32).reshape(1)
    pbs = [_contain(w_pad, shift).reshape(DEPTH, d, cw)] + [wl[n].astype(BF16) for n in big[1:]]
    nsmall = sum(int(np.prod(wl[n].shape)) for n in small)
    rs = -(-nsmall // (HD * 8)) * 8
    ps = jnp.pad(jnp.concatenate([wl[n].reshape(-1) for n in small]), (0, rs * HD - nsmall)).reshape(rs, HD)
    gbig, gsmall = _gather_weights(pbs, ps)
    gbig = [_place_own(g, p, chip_id.reshape(1)) for g, p in zip(gbig, pbs)]
    gsmall = gsmall.reshape(nchip, -1)

    wf = dict(wl)
    wf["w_in"] = lay.from_containers([gbig[0][s] for s in range(nchip)])
    for i, n in enumerate(big[1:], start=1):
        wf[n] = jnp.concatenate([gbig[i][s] for s in range(nchip)], axis=SHARD_AXIS[n])
    o = 0
    for n in small:
        sz = int(np.prod(wl[n].shape))
        a = gsmall[:, o:o + sz].reshape((nchip,) + wl[n].shape)
        wf[n] = jnp.concatenate([a[s] for s in range(nchip)], axis=SHARD_AXIS[n])
        o += sz

    loss_part, grad_x, gfull = _local_step(x, loss_target, wf, lay)
    loss = lax.psum(loss_part, ("x", "y", "c"))

    sw = wl["w_in"].shape[2]
    conts, heads = lay.containers(gfull["w_in"], nchip)
    dd = wl["w_branch_a"].shape[2]
    rows_o = wl["w_out"].shape[1]
    by_dest = lambda g, n: [lax.slice_in_dim(g, s * wl[n].shape[SHARD_AXIS[n]], (s + 1) * wl[n].shape[SHARD_AXIS[n]],
                                            axis=SHARD_AXIS[n]) if n in SHARD_AXIS else g for s in range(nchip)]
    small_names = tuple(n for n in WEIGHTS if n not in big)
    nsm = sum(int(np.prod(wl[n].shape)) for n in small_names)
    rsm = -(-nsm // (2 * HD * 8)) * 8
    pack_small = lambda parts: _flat(parts, 2 * rsm, HD).reshape(2, rsm, HD)
    small_by_dest = [by_dest(gfull[n], n) for n in small_names]
    gs = [jnp.stack(conts, axis=1),
          jnp.stack(by_dest(gfull["w_branch_a"], "w_branch_a"), axis=1),
          jnp.stack(by_dest(gfull["w_branch_b"], "w_branch_b"), axis=1),
          gfull["w_out"].reshape(DEPTH, nchip, rows_o, d),
          jnp.stack([pack_small([p[s] for p in small_by_dest]) for s in range(nchip)], axis=1)]
    gs = [g.reshape((2, nchip, -1, g.shape[-1])) for g in gs]
    my_chip = (2 * lax.axis_index("x") + lax.axis_index("y")).astype(jnp.int32)
    my_core = lax.axis_index("c").astype(jnp.int32)
    got = _swap_halves([g.astype(BF16) for g in gs[:4]] + gs[4:])
    chip_sums = [_add_cores(g, b, my_core.reshape(1)) for g, b in zip(gs, got)]
    by_chip = _scatter_chip_sums(chip_sums)
    full = _join_halves([_sum_chips(p, a, my_chip.reshape(1), my_core.reshape(1)) for p, a in zip(by_chip, chip_sums)])
    n_head = sum(jnp.where(my_chip == s, heads[s], 0) for s in range(nchip)).astype(jnp.int32).reshape(1)
    g_w_in = _uncontain(full[0].reshape(DEPTH * d, -1), n_head, sw)
    g2 = {"w_in": g_w_in, "w_branch_a": full[1].reshape(-1, dd), "w_branch_b": full[2].reshape(-1, dd),
          "w_out": full[3].reshape(-1, d), "small": full[4].reshape(2 * rsm, HD)}

    def two_d(src, n):
        if n == "small":
            return _flat([src[k] for k in small_names], 2 * rsm, HD)
        return src[n].reshape(g2[n].shape)

    outs = {}
    for n in big + ("small",):
        delta, mnew, vnew = _adamw(g2[n], two_d(wl, n), two_d(ml, n), two_d(vl, n))
        outs[n] = (g2[n], delta, mnew, vnew)
    res = [{}, {}, {}, {}]
    for i in range(4):
        for n in big:
            res[i][n] = outs[n][i].reshape(wl[n].shape)
        v, o = outs["small"][i].reshape(-1), 0
        for n in small_names:
            sz = int(np.prod(wl[n].shape))
            res[i][n] = v[o:o + sz].reshape(wl[n].shape)
            o += sz
    return (loss, grad_x, *[res[0][n] for n in WEIGHTS], *[res[1][n] for n in WEIGHTS], *[res[2][n] for n in WEIGHTS],
            *[res[3][n] for n in WEIGHTS])
```

```python
import functools

import numpy as np
import jax
import jax.numpy as jnp
from jax import lax
from jax.experimental import pallas as pl
from jax.experimental.pallas import tpu as pltpu

F32 = jnp.float32
BF16 = jnp.bfloat16
HI = lax.Precision.HIGHEST
SDS = jax.ShapeDtypeStruct

NH = 4
HD = 128
HW = NH * HD
N_META = 16
CH = 64
SUB = 16
PAD = CH - N_META
EPS = 1e-6
Q_SCALE = HD ** -0.5
DEPTH = 2
CONV_K = 4
VMEM_LIMIT = 56 * 1024 * 1024
ADAM_LR, ADAM_B1, ADAM_B2, ADAM_EPS, ADAM_WD, ADAM_STEP = 0.001, 0.9, 0.999, 1e-08, 0.01, 10
MESH = pl.DeviceIdType.MESH


def _nn(a, b):
    return jnp.dot(a, b, precision=HI, preferred_element_type=F32)


def _nt(a, b):
    return lax.dot_general(a, b, (((1,), (1,)), ((), ())), precision=HI, preferred_element_type=F32)


def _tn(a, b):
    return _nn(a.T, b)


def _scan_rows(x, group, reverse=False):
    n = x.shape[0]
    pos = lax.bitwise_and(_iota2(x.shape, 0), group - 1)
    s = 1
    while s < group:
        if reverse:
            x = x + jnp.where(pos < group - s, pltpu.roll(x, n - s, axis=0), 0.0)
        else:
            x = x + jnp.where(pos >= s, pltpu.roll(x, s, axis=0), 0.0)
        s *= 2
    return x


def _bnn(a, b):
    return jnp.dot(a.astype(BF16), b.astype(BF16), preferred_element_type=F32)


def _bnt(a, b):
    return lax.dot_general(a.astype(BF16), b.astype(BF16), (((1,), (1,)), ((), ())), preferred_element_type=F32)


def _btn(a, b):
    return lax.dot_general(a.astype(BF16), b.astype(BF16), (((0,), (0,)), ((), ())), preferred_element_type=F32)


def _hi_lo(x):
    hi = x.astype(jnp.bfloat16)
    return hi, (x - hi.astype(F32)).astype(jnp.bfloat16)


def _dot3(dims):
    def f(a, b):
        ah, al = _hi_lo(a)
        bh, bl = _hi_lo(b)
        d = lambda p, q: lax.dot_general(p, q, (dims, ((), ())), preferred_element_type=F32)
        return d(ah, bh) + (d(ah, bl) + d(al, bh))
    return f


_rnn, _rnt, _rtn = _dot3(((1,), (0,))), _dot3(((1,), (1,))), _dot3(((0,), (0,)))
_enn, _ent, _etn = _bnn, _bnt, _btn
_hnn, _hnt, _htn = _bnn, _bnt, _btn


def _rr(x):
    return x


def _sig(x):
    return jax.nn.sigmoid(x)


def _silu(x):
    return x * _sig(x)


def _dsilu(x):
    s = _sig(x)
    return s * (1.0 + x * (1.0 - s))


def _softplus(x):
    return jnp.maximum(x, 0.0) + jnp.log(1.0 + jnp.exp(-jnp.abs(x)))


def _logsig(x):
    return jnp.minimum(x, 0.0) - jnp.log(1.0 + jnp.exp(-jnp.abs(x)))


def _rs(x):
    return jnp.sum(x, axis=-1, keepdims=True)


def _params(n_axes):
    return pltpu.CompilerParams(dimension_semantics=("arbitrary",) * n_axes, vmem_limit_bytes=VMEM_LIMIT)


def _tile(n, target, mult=8):
    best = mult
    for t in range(mult, target + 1, mult):
        if n % t == 0:
            best = t
    assert n % best == 0, (n, mult)
    return best


def _ctile(pw, most=7):
    return HD * max(k for k in range(1, most + 1) if (pw // HD) % k == 0)


def _iota2(shape, axis):
    return lax.broadcasted_iota(jnp.int32, shape, axis)


class _Layout:
    def __init__(self, d):
        self.d = d
        self.wm = 2 * HW + 2 * d
        self.c_qkv = self.wm
        self.c_b = self.wm + 3 * HW
        self.c_ba = self.wm + 6 * HW
        self.pw = self.c_ba + HD
        assert self.c_b % (3 * HW) == 0
        o = 0
        segs = {}
        for name, w in (("a_q", HW), ("a_k", HW), ("a_v", HW), ("ba", 2 * NH), ("a_z", HW), ("b_q", HW), ("b_f", HW),
                        ("b_i", HW), ("b_g", HW), ("gate_a", d), ("gate_b", d)):
            segs[name] = (o, o + w)
            o += w
        self.segs = segs
        self.width = o
        self.order = ("a_z", "b_g", "gate_a", "gate_b", "a_q", "a_k", "a_v", "b_q", "b_f", "b_i", "ba")

    def to_kernel(self, w):
        parts = [w[..., self.segs[n][0]:self.segs[n][1]] for n in self.order]
        parts.append(jnp.zeros(w.shape[:-1] + (HD - 2 * NH,), w.dtype))
        return jnp.concatenate(parts, axis=-1)

    def containers(self, g, nchip):
        table, heads, cw = self.pieces(nchip)
        out = []
        for s in range(nchip):
            parts, at = [], 0
            for kcol, w, ccol in sorted(table[s], key=lambda p: p[2]):
                if ccol > at:
                    parts.append(jnp.zeros(g.shape[:-1] + (ccol - at,), g.dtype))
                parts.append(g[..., kcol:kcol + w])
                at = ccol + w
            if at < cw:
                parts.append(jnp.zeros(g.shape[:-1] + (cw - at,), g.dtype))
            out.append(jnp.concatenate(parts, axis=-1))
        return out, heads

    def pieces(self, nchip):
        off, where = 0, {}
        for n in self.order:
            where[n] = off
            off += self.segs[n][1] - self.segs[n][0]
        names = sorted(self.segs, key=lambda n: self.segs[n][0])
        sw = self.width // nchip
        cw = -(-sw // HD) * HD
        table, heads = [], []
        for s in range(nchip):
            lo, hi = s * sw, (s + 1) * sw
            pieces = []
            for n in names:
                a, b = max(lo, self.segs[n][0]), min(hi, self.segs[n][1])
                if a < b:
                    pieces.append((where[n] + a - self.segs[n][0], b - a))
            start, width = pieces[0]
            n_head = min((-start) % HD, width)
            body = ([(start + n_head, width - n_head)] if width > n_head else []) + pieces[1:]
            rows, at = [], 0
            for c, w in body:
                rows.append((c, w, at))
                at += w
            if n_head:
                rows.append((start, n_head, cw - n_head))
            table.append(rows)
            heads.append(n_head)
        return table, heads, cw

    def from_containers(self, conts):
        table, _, _ = self.pieces(len(conts))
        cut = sorted((kcol, w, s, ccol) for s, rows in enumerate(table) for kcol, w, ccol in rows)
        parts, at = [], 0
        for kcol, w, s, ccol in cut:
            assert kcol == at, (kcol, at)
            parts.append(conts[s][..., ccol:ccol + w])
            at = kcol + w
        parts.append(jnp.zeros(conts[0].shape[:-1] + (self.pw - at,), conts[0].dtype))
        return jnp.concatenate(parts, axis=-1)

    def from_kernel(self, g):
        off, where = 0, {}
        for n in self.order:
            w = self.segs[n][1] - self.segs[n][0]
            where[n] = (off, off + w)
            off += w
        names = sorted(self.segs, key=lambda n: self.segs[n][0])
        return jnp.concatenate([g[..., where[n][0]:where[n][1]] for n in names], axis=-1)


def _norm_proj_fwd(h, nw, wp):
    n, d = h.shape
    pw = wp.shape[1]
    tm, tn = _tile(n, 1408, HD), _ctile(pw)

    def body(h_ref, nw_ref, w_ref, o_ref, xt_ref, xn_ref):
        @pl.when(pl.program_id(1) == 0)
        def _():
            x = h_ref[...]
            r = lax.rsqrt(jnp.mean(x * x, axis=-1, keepdims=True) + EPS)
            xn = (x * r * nw_ref[...]).astype(BF16)
            xn_ref[...] = xn
            xt_ref[...] = xn.T

        o_ref[...] = jnp.dot(xn_ref[...], w_ref[...], preferred_element_type=F32)

    return pl.pallas_call(
        body, grid=(n // tm, pw // tn),
        in_specs=[pl.BlockSpec((tm, d), lambda i, j: (i, 0)), pl.BlockSpec((1, d), lambda i, j: (0, 0)),
                  pl.BlockSpec((d, tn), lambda i, j: (0, j))],
        out_specs=[pl.BlockSpec((tm, tn), lambda i, j: (i, j)), pl.BlockSpec((d, tm), lambda i, j: (0, i))],
        out_shape=[SDS((n, pw), F32), SDS((d, n), BF16)], scratch_shapes=[pltpu.VMEM((tm, d), BF16)],
        compiler_params=_params(2), name="norm_proj_fwd")(h, nw, wp)


def _row_valid(tm, tp, base):
    row = base + _iota2((tm, 1), 0)
    return lax.rem(row, tp) >= PAD


def _proj_bwd_dx(dproj, wp, h, nw, dhn, tp):
    n, d = h.shape
    pw = wp.shape[1]
    tm, tk = _tile16(n, 1056), _ctile(pw)
    nk = pw // tk

    def body(dp_ref, w_ref, h_ref, nw_ref, dhn_ref, dh_ref, dnw_ref, acc_ref):
        i, k = pl.program_id(0), pl.program_id(1)

        @pl.when(k == 0)
        def _():
            acc_ref[...] = jnp.zeros_like(acc_ref)

        @pl.when((i == 0) & (k == 0))
        def _():
            dnw_ref[...] = jnp.zeros_like(dnw_ref)

        valid = _row_valid(tm, tp, i * tm)
        dp = jnp.where(valid, dp_ref[...], 0.0)
        acc_ref[...] += _bnt(dp, w_ref[...])

        @pl.when(k == nk - 1)
        def _():
            x = h_ref[...]
            r = lax.rsqrt(jnp.mean(x * x, axis=-1, keepdims=True) + EPS)
            xh = x * r
            dxn = acc_ref[...]
            dnw_ref[...] += jnp.sum(dxn * xh, axis=0, keepdims=True)
            dxh = dxn * nw_ref[...]
            dh_ref[...] = dhn_ref[...] + r * (dxh - xh * jnp.mean(dxh * xh, axis=-1, keepdims=True))

    return pl.pallas_call(
        body, grid=(n // tm, nk),
        in_specs=[pl.BlockSpec((tm, tk), lambda i, k: (i, k)), pl.BlockSpec((d, tk), lambda i, k: (0, k)),
                  pl.BlockSpec((tm, d), lambda i, k: (i, 0)), pl.BlockSpec((1, d), lambda i, k: (0, 0)),
                  pl.BlockSpec((tm, d), lambda i, k: (i, 0))],
        out_specs=[pl.BlockSpec((tm, d), lambda i, k: (i, 0)), pl.BlockSpec((1, d), lambda i, k: (0, 0))],
        out_shape=[SDS((n, d), F32), SDS((1, d), F32)],
        scratch_shapes=[pltpu.VMEM((tm, d), F32)], compiler_params=_params(2), name="proj_bwd_dx")(dproj, wp, h, nw, dhn)


def _proj_bwd_dw(dproj, xt, tp):
    d, n = xt.shape
    pw = dproj.shape[1]
    tm, tn = _tile(n, 1408, HD), _ctile(pw)

    def body(dp_ref, xt_ref, dw_ref):
        i = pl.program_id(1)

        @pl.when(i == 0)
        def _():
            dw_ref[...] = jnp.zeros_like(dw_ref)

        dp = jnp.where(_row_valid(tm, tp, i * tm), dp_ref[...], 0.0)
        dw_ref[...] += jnp.dot(xt_ref[...], dp.astype(BF16), preferred_element_type=F32)

    return pl.pallas_call(
        body, grid=(pw // tn, n // tm),
        in_specs=[pl.BlockSpec((tm, tn), lambda j, i: (i, j)), pl.BlockSpec((d, tm), lambda j, i: (0, i))],
        out_specs=pl.BlockSpec((d, tn), lambda j, i: (0, j)), out_shape=SDS((d, pw), F32),
        compiler_params=_params(2), name="proj_bwd_dw")(dproj, xt)


def _conv_silu(x, w, row):
    c = x * w[CONV_K - 1:CONV_K, :]
    for k in range(1, CONV_K):
        c = c + jnp.where(row >= k, pltpu.roll(x, k, axis=0), 0.0) * w[CONV_K - 1 - k:CONV_K - k, :]
    return c


def _gdn_prep_fwd(proj, conv_w, lay, nb, tp):
    n = proj.shape[0]
    nblk = 3 * NH
    cb = lay.c_qkv // HD

    def body(p_ref, w_ref, o_ref):
        j = pl.program_id(1)
        x = p_ref[...]
        row = _iota2(x.shape, 0)
        c = _conv_silu(x, w_ref[...], row)
        s = _silu(c)
        r = lax.rsqrt(_rs(s * s) + EPS)
        scale = jnp.where(j < NH, Q_SCALE, 1.0)
        y = jnp.where(j < 2 * NH, s * r * scale, s)
        o_ref[...] = jnp.where(row >= PAD, y, 0.0)

    return pl.pallas_call(
        body, grid=(nb, nblk),
        in_specs=[pl.BlockSpec((tp, HD), lambda b, j: (b, cb + j)), pl.BlockSpec((CONV_K, HD), lambda b, j: (0, j))],
        out_specs=pl.BlockSpec((tp, HD), lambda b, j: (b, j)), out_shape=SDS((n, nblk * HD), F32),
        compiler_params=_params(2), name="gdn_prep_fwd")(proj, conv_w)


def _gdn_prep_bwd(proj, conv_w, dqkv, dproj, lay, nb, tp):
    nblk = 3 * NH
    cb = lay.c_qkv // HD

    def body(p_ref, w_ref, dy_ref, dp_in, dp_ref, dw_ref):
        j, b = pl.program_id(0), pl.program_id(1)
        x = p_ref[...]
        w = w_ref[...]
        row = _iota2(x.shape, 0)
        c = _conv_silu(x, w, row)
        s = _silu(c)
        dy = jnp.where(row >= PAD, dy_ref[...], 0.0)
        r = lax.rsqrt(_rs(s * s) + EPS)
        nh = s * r
        scale = jnp.where(j < NH, Q_SCALE, 1.0)
        ds_n = scale * r * (dy - nh * _rs(dy * nh))
        ds = jnp.where(j < 2 * NH, ds_n, dy)
        dc = ds * _dsilu(c)
        dx = dc * w[CONV_K - 1:CONV_K, :]
        dws = [jnp.sum(dc * x, axis=0, keepdims=True)]
        for k in range(1, CONV_K):
            dx = dx + jnp.where(row < tp - k, pltpu.roll(dc, tp - k, axis=0), 0.0) * w[CONV_K - 1 - k:CONV_K - k, :]
            xs = jnp.where(row >= k, pltpu.roll(x, k, axis=0), 0.0)
            dws.append(jnp.sum(dc * xs, axis=0, keepdims=True))
        dp_ref[...] = dx.astype(dp_ref.dtype)
        r4 = _iota2((CONV_K, HD), 0)
        dw = jnp.zeros((CONV_K, HD), F32)
        for k in range(CONV_K):
            dw = dw + jnp.where(r4 == CONV_K - 1 - k, dws[k], 0.0)

        @pl.when(b == 0)
        def _():
            dw_ref[...] = dw

        @pl.when(b > 0)
        def _():
            dw_ref[...] += dw

    return pl.pallas_call(
        body, grid=(nblk, nb),
        in_specs=[pl.BlockSpec((tp, HD), lambda j, b: (b, cb + j)), pl.BlockSpec((CONV_K, HD), lambda j, b: (0, j)),
                  pl.BlockSpec((tp, HD), lambda j, b: (b, j)), pl.BlockSpec(memory_space=pl.ANY)],
        out_specs=[pl.BlockSpec((tp, HD), lambda j, b: (b, cb + j)), pl.BlockSpec((CONV_K, HD), lambda j, b: (0, j))],
        out_shape=[SDS(dproj.shape, dproj.dtype), SDS((CONV_K, nblk * HD), F32)],
        input_output_aliases={3: 0}, compiler_params=_params(2), name="gdn_prep_bwd")(proj, conv_w, dqkv, dproj)


def _gate_consts():
    e = np.zeros((HD, 2 * HW), np.float32)
    s = np.zeros((2 * HW, HD), np.float32)
    for h in range(NH):
        e[h, h * HD:(h + 1) * HD] = 1.0
        e[NH + h, HW + h * HD:HW + (h + 1) * HD] = 1.0
        s[h * HD, h] = 1.0
        s[HW + h * HD, NH + h] = 1.0
    return jnp.asarray(e), jnp.asarray(s)


def _gdn_tri():
    i, j = _iota2((CH, CH), 0), _iota2((CH, CH), 1)
    return i >= j, i > j


def _each(fn, *lists):
    return [fn(*xs) for xs in zip(*lists)]


def _tri_inv(a_list, eye):
    p = [-a for a in a_list]
    t = [eye + x for x in p]
    for _ in range(5):
        p = _each(_rnn, p, p)
        tp_ = _each(_rnn, t, p)
        t = _each(lambda x, y: x + y, t, tp_)
    return t


def _gdn_chunks(args, solved=None):
    causal, strict = _gdn_tri()
    eye = jnp.where(_iota2((CH, CH), 0) == _iota2((CH, CH), 1), 1.0, 0.0)
    q, k, v, beta, g, s0 = (list(t) for t in zip(*args))
    gc = [_scan_rows(x, CH) for x in g]
    dm = [jnp.where(causal, jnp.exp(jnp.where(causal, x[:, :CH] - x[:, :CH].T, 0.0)), 0.0) for x in gc]
    ds = [jnp.where(strict, x, 0.0) for x in dm]
    kb = _each(lambda x, y: x * y, k, beta)
    by_k = _each(_ent, [jnp.concatenate([x, y], axis=0) for x, y in zip(kb, q)], k)
    kk = [x[:CH] for x in by_k]
    qk = [x[CH:] for x in by_k]
    a = _each(lambda x, y: x * y, kk, ds)
    eg = [jnp.exp(x) for x in gc]
    rw = _each(lambda x, y: x * y, kb, eg)
    if solved is None:
        tinv = _tri_inv(a, eye)
        rv = _each(lambda x, y: x * y, v, beta)
        u = _each(_rnn, tinv, rv)
        w = _each(_rnn, tinv, rw)
    else:
        tinv, u, w = (list(t) for t in zip(*solved))
    ws = _each(_enn, w, s0)
    vn = _each(lambda x, y: x - y, u, ws)
    p = _each(lambda x, y: x * y, qk, dm)
    qg = _each(lambda x, y: x * y, q, eg)
    out = []
    for i in range(len(args)):
        gl = gc[i][CH - 1:CH, :]
        ek = jnp.exp(gl - gc[i])
        out.append(dict(gc=gc[i], dm=dm[i], ds=ds[i], kb=kb[i], a=a[i], tinv=tinv[i], eg=eg[i], rw=rw[i], u=u[i], w=w[i],
                        vn=vn[i], p=p[i], qg=qg[i], egl=jnp.exp(gl), ek=ek, kd=k[i] * ek))
    return out


def _gdn_gates(ba, e, alog, dtb):
    raw = _nn(ba, e)
    beta = _sig(raw[:, :HW])
    za = raw[:, HW:] + dtb
    g = -jnp.exp(alog) * _softplus(za)
    return beta, g, za


def _seqs_per_step(nb):
    return 4 if nb % 4 == 0 else (2 if nb % 2 == 0 else 1)


def _gdn_fwd(qkv, proj, e_mat, alog, dtb, lay, nb, nc):
    n = qkv.shape[0]
    tp = n // nb
    cba = lay.c_ba // HD
    gb = _seqs_per_step(nb)

    def body(x_ref, ba_ref, e_ref, al_ref, dt_ref, o_ref, so_ref, sv_ref, s_ref):
        @pl.when(pl.program_id(1) == 0)
        def _():
            s_ref[...] = jnp.zeros_like(s_ref)

        args = []
        for j in range(gb):
            beta, g, _ = _gdn_gates(ba_ref[j], e_ref[...], al_ref[...], dt_ref[...])
            for h in range(NH):
                hs = slice(h * HD, (h + 1) * HD)
                args.append((x_ref[j, :, hs], x_ref[j, :, HW + h * HD:HW + (h + 1) * HD],
                             x_ref[j, :, 2 * HW + h * HD:2 * HW + (h + 1) * HD], beta[:, hs], g[:, hs], s_ref[j, h]))
        cs = _gdn_chunks(args)
        s0s = [a[5] for a in args]
        o1 = _each(lambda c, s0: _enn(c["qg"], s0), cs, s0s)
        o2 = [_enn(c["p"], c["vn"]) for c in cs]
        upd = [_etn(c["kd"], c["vn"]) for c in cs]
        res = [(o1[i] + o2[i], s0s[i] * cs[i]["egl"] + upd[i]) for i in range(len(cs))]
        zero = jnp.zeros((CH, HD - CH), F32)
        for j in range(gb):
            for h in range(NH):
                c = cs[j * NH + h]
                so_ref[j, h] = args[j * NH + h][5]
                sv_ref[j, h] = jnp.concatenate([c["u"], c["w"], c["tinv"], zero], axis=-1)
                s_ref[j, h] = res[j * NH + h][1]
            o_ref[j] = jnp.concatenate([res[j * NH + h][0] for h in range(NH)], axis=-1)

    o, st, sv = pl.pallas_call(
        body, grid=(nb // gb, nc),
        in_specs=[pl.BlockSpec((gb, CH, 3 * HW), lambda b, c: (b, c, 0)), pl.BlockSpec((gb, CH, HD), lambda b, c: (b, c, cba)),
                  pl.BlockSpec((HD, 2 * HW), lambda b, c: (0, 0)), pl.BlockSpec((1, HW), lambda b, c: (0, 0)),
                  pl.BlockSpec((1, HW), lambda b, c: (0, 0))],
        out_specs=[pl.BlockSpec((gb, CH, HW), lambda b, c: (b, c, 0)),
                   pl.BlockSpec((gb, None, NH, HD, HD), lambda b, c: (b, c, 0, 0, 0)),
                   pl.BlockSpec((gb, None, NH, CH, 3 * HD), lambda b, c: (b, c, 0, 0, 0))],
        out_shape=[SDS((nb, tp, HW), F32), SDS((nb, nc, NH, HD, HD), F32), SDS((nb, nc, NH, CH, 3 * HD), F32)],
        scratch_shapes=[pltpu.VMEM((gb, NH, HD, HD), F32)], compiler_params=_params(2), name="gdn_fwd")(
            qkv.reshape(nb, tp, 3 * HW), proj.reshape(nb, tp, -1), e_mat, alog, dtb)
    return o.reshape(n, HW), st, sv


def _gdn_bwd(qkv, proj, e_mat, s_mat, alog, dtb, states, solved, do, dproj, lay, nb, nc):
    n = qkv.shape[0]
    tp = n // nb
    cba = lay.c_ba // HD
    gb = _seqs_per_step(nb)

    def body(x_ref, ba_ref, e_ref, sm_ref, al_ref, dt_ref, st_ref, sv_ref, do_ref, dp_in, dx_ref, dba_ref, acc_ref, ds_ref):
        ci = pl.program_id(1)

        @pl.when(ci == 0)
        def _():
            ds_ref[...] = jnp.zeros_like(ds_ref)

        @pl.when((ci == 0) & (pl.program_id(0) == 0))
        def _():
            acc_ref[...] = jnp.zeros_like(acc_ref)

        causal, strict = _gdn_tri()
        alog = al_ref[...]
        row = _iota2((CH, 1), 0)
        valid = (row >= PAD) | (ci < nc - 1)
        last = row == CH - 1
        gates = [_gdn_gates(ba_ref[j], e_ref[...], alog, dt_ref[...]) for j in range(gb)]
        args, do, ds1, solved = [], [], [], []
        for j in range(gb):
            beta, g, _ = gates[j]
            for h in range(NH):
                hs = slice(h * HD, (h + 1) * HD)
                args.append((x_ref[j, :, hs], x_ref[j, :, HW + h * HD:HW + (h + 1) * HD],
                             x_ref[j, :, 2 * HW + h * HD:2 * HW + (h + 1) * HD], beta[:, hs], g[:, hs], st_ref[j, h]))
                do.append(do_ref[j, :, hs])
                ds1.append(ds_ref[j, h])
                solved.append((sv_ref[j, h, :, 2 * HD:2 * HD + CH], sv_ref[j, h, :, 0:HD], sv_ref[j, h, :, HD:2 * HD]))
        q, k, v, bh, _, s0 = (list(t) for t in zip(*args))
        cs = _gdn_chunks(args, solved)
        get = lambda name: [c[name] for c in cs]
        mul = lambda x, y: x * y
        add = lambda x, y: x + y
        dvn = _each(add, _each(_etn, get("p"), do), _each(_enn, get("kd"), ds1))
        by_s0 = _each(_ent, [jnp.concatenate([x, y], axis=0) for x, y in zip(do, dvn)], s0)
        dqg = [x[:CH] for x in by_s0]
        dw = [-x[CH:] for x in by_s0]
        dp = [jnp.where(causal, x, 0.0) for x in _each(_ent, do, get("vn"))]
        dkd = _each(_ent, get("vn"), ds1)
        ds_a = _each(_etn, get("qg"), do)
        ds_b = _each(_etn, get("w"), dvn)
        ds_new = [ds_a[i] - ds_b[i] + ds1[i] * cs[i]["egl"] for i in range(len(cs))]
        drvw = _each(_rtn, get("tinv"), [jnp.concatenate([x, y], axis=-1) for x, y in zip(dvn, dw)])
        drv = [x[:, :HD] for x in drvw]
        drw = [x[:, HD:] for x in drvw]
        uw = [sv_ref[j, h, :, 0:2 * HD] for j in range(gb) for h in range(NH)]
        da = [jnp.where(strict, -x, 0.0) for x in _each(_rnt, drvw, uw)]
        m = [da[i] * cs[i]["a"] + dp[i] * cs[i]["p"] for i in range(len(cs))]
        dkk = _each(mul, da, get("ds"))
        dqk = _each(mul, dp, get("dm"))
        by_k = _each(_enn, [jnp.concatenate([x, y], axis=0) for x, y in zip(dqk, dkk)], k)
        dq = _each(add, [x[:CH] for x in by_k], _each(mul, dqg, get("eg")))
        dkb = _each(add, [x[CH:] for x in by_k], _each(mul, drw, get("eg")))
        dk_1 = _each(_etn, dqk, q)
        dk_2 = _each(_etn, dkk, get("kb"))
        dk = [dk_1[i] + dk_2[i] + dkd[i] * cs[i]["ek"] + dkb[i] * bh[i] for i in range(len(cs))]
        dv = _each(mul, drv, bh)
        dbeta, dg = [], []
        for i, c in enumerate(cs):
            dbeta.append(_rs(drv[i] * v[i]) + _rs(dkb[i] * k[i]) + jnp.zeros((CH, HD), F32))
            t_kd = _rs(dkd[i] * c["kd"])
            dgc = _rs(m[i]) - _rs(m[i].T) + _rs(dqg[i] * c["qg"]) + _rs(drw[i] * c["rw"]) - t_kd
            tail = jnp.sum(t_kd, axis=0, keepdims=True) + c["egl"] * jnp.sum(_rs(s0[i] * ds1[i]), axis=0, keepdims=True)
            dgc = dgc + jnp.where(last, tail, 0.0)
            dg.append(_scan_rows(dgc + jnp.zeros((CH, HD), F32), CH, reverse=True))
        r8 = _iota2((8, HW), 0)
        upd = jnp.zeros((8, HW), F32)
        for j in range(gb):
            sl = slice(j * NH, (j + 1) * NH)
            beta, g, za = gates[j]
            for h in range(NH):
                ds_ref[j, h] = ds_new[j * NH + h]
            dx_ref[j] = jnp.concatenate(dq[sl] + dk[sl] + dv[sl], axis=-1)
            dbeta_j = jnp.where(valid, jnp.concatenate(dbeta[sl], axis=-1), 0.0)
            dg_j = jnp.where(valid, jnp.concatenate(dg[sl], axis=-1), 0.0)
            draw_b = dbeta_j * beta * (1.0 - beta)
            draw_a = dg_j * (-jnp.exp(alog)) * _sig(za)
            dba_ref[j] = _nn(jnp.concatenate([draw_b, draw_a], axis=-1), sm_ref[...]).astype(dba_ref.dtype)
            upd = upd + jnp.where(r8 == 0, jnp.sum(dg_j * g, axis=0, keepdims=True), 0.0) + jnp.where(
                r8 == 1, jnp.sum(draw_a, axis=0, keepdims=True), 0.0)
        acc_ref[...] += upd

    rc = lambda c: nc - 1 - c
    dqkv, dproj3, acc = pl.pallas_call(
        body, grid=(nb // gb, nc),
        in_specs=[pl.BlockSpec((gb, CH, 3 * HW), lambda b, c: (b, rc(c), 0)), pl.BlockSpec((gb, CH, HD), lambda b, c: (b, rc(c), cba)),
                  pl.BlockSpec((HD, 2 * HW), lambda b, c: (0, 0)), pl.BlockSpec((2 * HW, HD), lambda b, c: (0, 0)),
                  pl.BlockSpec((1, HW), lambda b, c: (0, 0)), pl.BlockSpec((1, HW), lambda b, c: (0, 0)),
                  pl.BlockSpec((gb, None, NH, HD, HD), lambda b, c: (b, rc(c), 0, 0, 0)),
                  pl.BlockSpec((gb, None, NH, CH, 3 * HD), lambda b, c: (b, rc(c), 0, 0, 0)),
                  pl.BlockSpec((gb, CH, HW), lambda b, c: (b, rc(c), 0)), pl.BlockSpec(memory_space=pl.ANY)],
        out_specs=[pl.BlockSpec((gb, CH, 3 * HW), lambda b, c: (b, rc(c), 0)), pl.BlockSpec((gb, CH, HD), lambda b, c: (b, rc(c), cba)),
                   pl.BlockSpec((8, HW), lambda b, c: (0, 0))],
        out_shape=[SDS((nb, tp, 3 * HW), F32), SDS((nb, tp, dproj.shape[1]), dproj.dtype), SDS((8, HW), F32)],
        input_output_aliases={9: 1},
        scratch_shapes=[pltpu.VMEM((gb, NH, HD, HD), F32)], compiler_params=_params(2), name="gdn_bwd")(
            qkv.reshape(nb, tp, 3 * HW), proj.reshape(nb, tp, -1), e_mat, s_mat, alog, dtb, states, solved, do.reshape(nb, tp, HW),
            dproj.reshape(nb, tp, -1))
    return dqkv.reshape(n, 3 * HW), dproj3.reshape(dproj.shape), acc


def _hgrn_inputs(zq, zf, lb):
    sg = _sig(zf)
    sgn = _sig(-zf)
    pos = lb > 0.0
    lbp = jnp.where(pos, lb, 0.0)
    fpos = lbp + (1.0 - lbp) * sg
    lf = jnp.where(pos, jnp.log(jnp.where(pos, fpos, 1.0)), _logsig(zf))
    k = (1.0 - lbp) * sgn
    q = _silu(zq) * Q_SCALE
    return q, k, lf, sg, sgn, pos, lbp, fpos


def _hgrn_consts():
    i3, j3 = _iota2((SUB, SUB, HD), 0), _iota2((SUB, SUB, HD), 1)
    return i3 >= j3


def _sum_j(x):
    return jnp.sum(x.reshape(SUB, SUB, HD), axis=1)


def _sum_i(x):
    return jnp.sum(x.reshape(SUB, SUB, HD), axis=0)


def _pairs(a, b):
    return (a[:, None, :] * b[None, :, :]).reshape(SUB * SUB, HD)


def _hgrn_sub(q, k, v, bc, st, consts):
    mask3 = consts
    bl = bc[SUB - 1:SUB, :]
    p3 = jnp.where(mask3, jnp.exp(jnp.where(mask3, bc[:, None, :] - bc[None, :, :], 0.0)), 0.0).reshape(SUB * SUB, HD)
    x = _pairs(q, k) * p3
    srep = _rs(x)
    vt = jnp.broadcast_to(v[None, :, :], (SUB, SUB, HD)).reshape(SUB * SUB, HD)
    eb = jnp.exp(bc)
    qe = q * eb
    o = _hnt(qe, st) + _sum_j(_rr(srep) * _rr(vt))
    ek = jnp.exp(bl - bc)
    kd = k * ek
    ebl = jnp.exp(bl)
    st1 = st * ebl + _htn(v, kd)
    return o, st1, dict(bc=bc, p3=p3, srep=srep, vt=vt, eb=eb, qe=qe, ek=ek, kd=kd, ebl=ebl)


def _hgrn_fwd(proj, lb, lay, nb, nc):
    n = proj.shape[0]
    tp = n // nb
    cbb = lay.c_b // (3 * HW)
    gb = _seqs_per_step(nb)

    def body(z_ref, lb_ref, o_ref, so_ref, s_ref):
        @pl.when(pl.program_id(1) == 0)
        def _():
            s_ref[...] = jnp.zeros_like(s_ref)

        consts = _hgrn_consts()
        for j in range(gb):
            outs = []
            for h in range(NH):
                hs = slice(h * HD, (h + 1) * HD)
                q, k, lf = _hgrn_inputs(z_ref[j, :, hs], z_ref[j, :, HW + h * HD:HW + (h + 1) * HD], lb_ref[:, hs])[:3]
                v = z_ref[j, :, 2 * HW + h * HD:2 * HW + (h + 1) * HD]
                st = s_ref[j, h]
                so_ref[j, h] = st
                bc = _scan_rows(lf, SUB)
                oh = []
                for s in range(CH // SUB):
                    rs = slice(s * SUB, (s + 1) * SUB)
                    o, st, _ = _hgrn_sub(q[rs], k[rs], v[rs], bc[rs], st, consts)
                    oh.append(o)
                s_ref[j, h] = st
                outs.append(jnp.concatenate(oh, axis=0))
            o_ref[j] = jnp.concatenate(outs, axis=-1)

    o, st = pl.pallas_call(
        body, grid=(nb // gb, nc),
        in_specs=[pl.BlockSpec((gb, CH, 3 * HW), lambda b, c: (b, c, cbb)), pl.BlockSpec((1, HW), lambda b, c: (0, 0))],
        out_specs=[pl.BlockSpec((gb, CH, HW), lambda b, c: (b, c, 0)),
                   pl.BlockSpec((gb, None, NH, HD, HD), lambda b, c: (b, c, 0, 0, 0))],
        out_shape=[SDS((nb, tp, HW), F32), SDS((nb, nc, NH, HD, HD), F32)],
        scratch_shapes=[pltpu.VMEM((gb, NH, HD, HD), F32)], compiler_params=_params(2), name="hgrn_fwd")(
            proj.reshape(nb, tp, -1), lb)
    return o.reshape(n, HW), st


def _hgrn_bwd(proj, lb, states, do, dproj, lay, nb, nc):
    n = proj.shape[0]
    tp = n // nb
    cbb = lay.c_b // (3 * HW)
    nsub = CH // SUB
    gb = _seqs_per_step(nb)

    def body(z_ref, lb_ref, st_ref, do_ref, dp_in, dz_ref, acc_ref, ds_ref):
        ci = pl.program_id(1)

        @pl.when(ci == 0)
        def _():
            ds_ref[...] = jnp.zeros_like(ds_ref)

        @pl.when((ci == 0) & (pl.program_id(0) == 0))
        def _():
            acc_ref[...] = jnp.zeros_like(acc_ref)

        upd = jnp.zeros((8, HW), F32)
        for j in range(gb):
            upd = upd + one_seq(j, ci, z_ref, lb_ref, st_ref, do_ref, dz_ref, ds_ref)
        acc_ref[...] += upd

    def one_seq(j, ci, z_ref, lb_ref, st_ref, do_ref, dz_ref, ds_ref):
        consts = _hgrn_consts()
        row = _iota2((CH, 1), 0)
        valid = (row >= PAD) | (ci < nc - 1)
        lastrow = _iota2((SUB, 1), 0) == SUB - 1
        dzq, dzf, dzi, dlbs = [], [], [], []
        for h in range(NH):
            hs = slice(h * HD, (h + 1) * HD)
            zq, zf = z_ref[j, :, hs], z_ref[j, :, HW + h * HD:HW + (h + 1) * HD]
            q, k, lf, sg, sgn, pos, lbp, fpos = _hgrn_inputs(zq, zf, lb_ref[:, hs])
            v = z_ref[j, :, 2 * HW + h * HD:2 * HW + (h + 1) * HD]
            doh = do_ref[j, :, hs]
            sts, fw = [st_ref[j, h]], []
            bc = _scan_rows(lf, SUB)
            for s in range(nsub):
                rs = slice(s * SUB, (s + 1) * SUB)
                _, st1, c = _hgrn_sub(q[rs], k[rs], v[rs], bc[rs], sts[-1], consts)
                sts.append(st1)
                fw.append(c)
            dst = ds_ref[j, h]
            dq_l, dk_l, dv_l, dlf_l = [None] * nsub, [None] * nsub, [None] * nsub, [None] * nsub
            for s in reversed(range(nsub)):
                rs = slice(s * SUB, (s + 1) * SUB)
                c, st = fw[s], sts[s]
                qs, ks, vs, dos = q[rs], k[rs], v[rs], doh[rs]
                dqe = _hnn(dos, st)
                dkd = _hnn(vs, dst)
                dsrep = _rs(_pairs(_rr(dos), _rr(vs)))
                w = dsrep * c["p3"]
                kt = jnp.broadcast_to(ks[None, :, :], (SUB, SUB, HD)).reshape(SUB * SUB, HD)
                qt = jnp.broadcast_to(qs[:, None, :], (SUB, SUB, HD)).reshape(SUB * SUB, HD)
                dq_i = _sum_j(w * kt)
                dk_i = _sum_i(w * qt)
                dot = jnp.broadcast_to(_rr(dos)[:, None, :], (SUB, SUB, HD)).reshape(SUB * SUB, HD)
                dvv = _sum_i(_rr(c["srep"]) * dot) + _hnt(c["kd"], dst)
                t_kd = dkd * c["kd"]
                dbc = dqe * c["qe"] - t_kd + qs * dq_i - ks * dk_i
                tail = jnp.sum(t_kd, axis=0, keepdims=True) + c["ebl"] * jnp.sum(st * dst, axis=0, keepdims=True)
                dbc = dbc + jnp.where(lastrow, tail, 0.0)
                dlf_l[s] = dbc
                dq_l[s] = dq_i + dqe * c["eb"]
                dk_l[s] = dk_i + dkd * c["ek"]
                dv_l[s] = dvv
                dst = _htn(dos, c["qe"]) + dst * c["ebl"]
            ds_ref[j, h] = dst
            dq, dk, dv, dbc = (jnp.concatenate(t, axis=0) for t in (dq_l, dk_l, dv_l, dlf_l))
            dlf = _scan_rows(dbc, SUB, reverse=True)
            dlft = dlf - dk * (1.0 - k)
            dlf_dz = jnp.where(pos, (1.0 - lbp) * sg * sgn / jnp.where(pos, fpos, 1.0), sgn)
            dlf_dlb = jnp.where(pos, sgn / jnp.where(pos, fpos, 1.0), 0.0)
            dzq.append(dq * Q_SCALE * _dsilu(zq))
            dzf.append(dlft * dlf_dz)
            dzi.append(dv)
            dlbs.append(jnp.sum(jnp.where(valid, dlft * dlf_dlb, 0.0), axis=0, keepdims=True))
        dz_ref[j] = jnp.concatenate(dzq + dzf + dzi, axis=-1).astype(dz_ref.dtype)
        return jnp.where(_iota2((8, HW), 0) == 0, jnp.concatenate(dlbs, axis=-1), 0.0)

    rc = lambda c: nc - 1 - c
    dproj3, acc = pl.pallas_call(
        body, grid=(nb // gb, nc),
        in_specs=[pl.BlockSpec((gb, CH, 3 * HW), lambda b, c: (b, rc(c), cbb)), pl.BlockSpec((1, HW), lambda b, c: (0, 0)),
                  pl.BlockSpec((gb, None, NH, HD, HD), lambda b, c: (b, rc(c), 0, 0, 0)),
                  pl.BlockSpec((gb, CH, HW), lambda b, c: (b, rc(c), 0)), pl.BlockSpec(memory_space=pl.ANY)],
        out_specs=[pl.BlockSpec((gb, CH, 3 * HW), lambda b, c: (b, rc(c), cbb)), pl.BlockSpec((8, HW), lambda b, c: (0, 0))],
        out_shape=[SDS((nb, tp, dproj.shape[1]), dproj.dtype), SDS((8, HW), F32)],
        input_output_aliases={4: 0},
        scratch_shapes=[pltpu.VMEM((gb, NH, HD, HD), F32)], compiler_params=_params(2), name="hgrn_bwd")(
            proj.reshape(nb, tp, -1), lb, states, do.reshape(nb, tp, HW), dproj.reshape(nb, tp, -1))
    return dproj3.reshape(dproj.shape), acc


def _gated_norm(o, z, gamma):
    ys, ns, rs = [], [], []
    for h in range(NH):
        hs = slice(h * HD, (h + 1) * HD)
        oh = o[:, hs]
        r = lax.rsqrt(jnp.mean(oh * oh, axis=-1, keepdims=True) + EPS)
        nh = oh * r
        ys.append(nh * gamma * _silu(z[:, hs]))
        ns.append(nh)
        rs.append(r)
    return jnp.concatenate(ys, axis=-1), ns, rs


def _merge_fwd(h, oa, ob, proj, ga, gb, wa, wb, wo, lay):
    n, d = h.shape
    tm = _tile(n, 384)
    wm = lay.wm

    def body(h_ref, oa_ref, ob_ref, p_ref, ga_ref, gb_ref, wa_ref, wb_ref, wo_ref, out_ref):
        ya, _, _ = _gated_norm(oa_ref[...], p_ref[:, 0:HW], ga_ref[...])
        yb, _, _ = _gated_norm(ob_ref[...], p_ref[:, HW:2 * HW], gb_ref[...])
        ya2 = _bnn(ya, wa_ref[...])
        yb2 = _bnn(yb, wb_ref[...])
        mixed = _sig(p_ref[:, 2 * HW:2 * HW + d]) * ya2 + _sig(p_ref[:, 2 * HW + d:2 * HW + 2 * d]) * yb2
        out_ref[...] = h_ref[...] + _bnn(mixed, wo_ref[...])

    full = lambda shape: pl.BlockSpec(shape, lambda i: (0, 0))
    return pl.pallas_call(
        body, grid=(n // tm,),
        in_specs=[pl.BlockSpec((tm, d), lambda i: (i, 0)), pl.BlockSpec((tm, HW), lambda i: (i, 0)),
                  pl.BlockSpec((tm, HW), lambda i: (i, 0)), pl.BlockSpec((tm, wm), lambda i: (i, 0)),
                  full((1, HD)), full((1, HD)), full((HW, d)), full((HW, d)), full((d, d))],
        out_specs=pl.BlockSpec((tm, d), lambda i: (i, 0)), out_shape=SDS((n, d), F32),
        compiler_params=_params(1), name="merge_fwd")(h, oa, ob, proj, ga, gb, wa, wb, wo)


def _gated_norm_bwd(dy, o, z, gamma):
    dos, dzs = [], []
    dgam = jnp.zeros((1, HD), F32)
    for h in range(NH):
        hs = slice(h * HD, (h + 1) * HD)
        oh, zh, dyh = o[:, hs], z[:, hs], dy[:, hs]
        r = lax.rsqrt(jnp.mean(oh * oh, axis=-1, keepdims=True) + EPS)
        nh = oh * r
        dzs.append(dyh * nh * gamma * _dsilu(zh))
        dng = dyh * _silu(zh)
        dgam = dgam + jnp.sum(dng * nh, axis=0, keepdims=True)
        dn = dng * gamma
        dos.append(r * (dn - nh * jnp.mean(dn * nh, axis=-1, keepdims=True)))
    return jnp.concatenate(dos, axis=-1), jnp.concatenate(dzs, axis=-1), dgam


def _merge_bwd(dhn, oa, ob, proj, ga, gb, wa, wb, wo, lay, tp):
    n, d = dhn.shape
    tm = _tile(n, 256)
    wm = lay.wm

    def body(dh_ref, oa_ref, ob_ref, p_ref, ga_ref, gb_ref, wa_ref, wb_ref, wo_ref,
             dp_ref, doa_ref, dob_ref, dwa_ref, dwb_ref, dwo_ref, dga_ref, dgb_ref):
        i = pl.program_id(0)

        @pl.when(i == 0)
        def _():
            for r in (dwa_ref, dwb_ref, dwo_ref, dga_ref, dgb_ref):
                r[...] = jnp.zeros_like(r)

        dh = jnp.where(_row_valid(tm, tp, i * tm), dh_ref[...], 0.0)
        oa, ob = oa_ref[...], ob_ref[...]
        za, zb = p_ref[:, 0:HW], p_ref[:, HW:2 * HW]
        gta, gtb = p_ref[:, 2 * HW:2 * HW + d], p_ref[:, 2 * HW + d:2 * HW + 2 * d]
        ya, _, _ = _gated_norm(oa, za, ga_ref[...])
        yb, _, _ = _gated_norm(ob, zb, gb_ref[...])
        ya2 = _bnn(ya, wa_ref[...])
        yb2 = _bnn(yb, wb_ref[...])
        sa, sb = _sig(gta), _sig(gtb)
        mixed = sa * ya2 + sb * yb2
        dmixed = _bnt(dh, wo_ref[...])
        dwo_ref[...] += _btn(mixed, dh)
        dya2 = dmixed * sa
        dyb2 = dmixed * sb
        dwa_ref[...] += _btn(ya, dya2)
        dwb_ref[...] += _btn(yb, dyb2)
        doa, dza, dga = _gated_norm_bwd(_bnt(dya2, wa_ref[...]), oa, za, ga_ref[...])
        dob, dzb, dgb = _gated_norm_bwd(_bnt(dyb2, wb_ref[...]), ob, zb, gb_ref[...])
        dga_ref[...] += dga
        dgb_ref[...] += dgb
        doa_ref[...] = doa
        dob_ref[...] = dob
        dt = dp_ref.dtype
        dp_ref[:, 0:HW] = dza.astype(dt)
        dp_ref[:, HW:2 * HW] = dzb.astype(dt)
        dp_ref[:, 2 * HW:2 * HW + d] = (dmixed * ya2 * sa * (1.0 - sa)).astype(dt)
        dp_ref[:, 2 * HW + d:2 * HW + 2 * d] = (dmixed * yb2 * sb * (1.0 - sb)).astype(dt)

    full = lambda shape: pl.BlockSpec(shape, lambda i: (0, 0))
    rows = lambda w: pl.BlockSpec((tm, w), lambda i: (i, 0))
    return pl.pallas_call(
        body, grid=(n // tm,),
        in_specs=[rows(d), rows(HW), rows(HW), rows(wm), full((1, HD)), full((1, HD)), full((HW, d)), full((HW, d)), full((d, d))],
        out_specs=[rows(wm), rows(HW), rows(HW), full((HW, d)), full((HW, d)), full((d, d)), full((1, HD)), full((1, HD))],
        out_shape=[SDS((n, lay.pw), BF16), SDS((n, HW), F32), SDS((n, HW), F32), SDS((HW, d), F32), SDS((HW, d), F32),
                   SDS((d, d), F32), SDS((1, HD), F32), SDS((1, HD), F32)],
        compiler_params=_params(1), name="merge_bwd")(dhn, oa, ob, proj, ga, gb, wa, wb, wo)


def _loss_head(h, target, fw, nb, tp):
    n, d = h.shape
    tr = _tile(tp, 768)
    nr = tp // tr

    def body(h_ref, t_ref, fw_ref, lp_ref, dh_ref, dfw_ref):
        b, i = pl.program_id(0), pl.program_id(1)

        @pl.when((b == 0) & (i == 0))
        def _():
            dfw_ref[...] = jnp.zeros_like(dfw_ref)

        x = h_ref[...]
        r = lax.rsqrt(jnp.mean(x * x, axis=-1, keepdims=True) + EPS)
        xh = x * r
        live = i * tr + _iota2((tr, 1), 0) >= CH
        err = jnp.where(live, xh * fw_ref[...] - t_ref[...], 0.0)
        lp_ref[...] = jnp.zeros_like(lp_ref) + 0.5 * jnp.sum(_rs(err * err), axis=0, keepdims=True) / d
        dy = err / d
        dfw_ref[...] += jnp.sum(dy * xh, axis=0, keepdims=True)
        dxh = dy * fw_ref[...]
        dh_ref[...] = r * (dxh - xh * jnp.mean(dxh * xh, axis=-1, keepdims=True))

    rows = pl.BlockSpec((tr, d), lambda b, i: (b * nr + i, 0))
    return pl.pallas_call(
        body, grid=(nb, nr), in_specs=[rows, rows, pl.BlockSpec((1, d), lambda b, i: (0, 0))],
        out_specs=[pl.BlockSpec((8, HD), lambda b, i: (b * nr + i, 0)), rows, pl.BlockSpec((1, d), lambda b, i: (0, 0))],
        out_shape=[SDS((nb * nr * 8, HD), F32), SDS((n, d), F32), SDS((1, d), F32)],
        compiler_params=_params(2), name="loss_head")(h, target, fw)


def _lb_fwd(lb):
    def body(x_ref, o_ref):
        x = x_ref[...]
        mx = jnp.max(x, axis=0, keepdims=True)
        e = jnp.exp(x - mx)
        sm = e / jnp.sum(e, axis=0, keepdims=True)
        run = jnp.zeros((1, HW), F32)
        for l in range(DEPTH):
            run = run + sm[l:l + 1, :]
            o_ref[l:l + 1, :] = run - sm[0:1, :]

    return pl.pallas_call(body, out_shape=SDS(lb.shape, F32), name="lb_fwd")(lb)


def _lb_bwd(lb, dlb_all):
    def body(x_ref, d_ref, o_ref):
        x = x_ref[...]
        dl = d_ref[...]
        mx = jnp.max(x, axis=0, keepdims=True)
        e = jnp.exp(x - mx)
        sm = e / jnp.sum(e, axis=0, keepdims=True)
        tot = jnp.sum(dl, axis=0, keepdims=True)
        dsm = []
        run = tot
        for l in range(DEPTH):
            dsm.append(run - (tot if l == 0 else 0.0))
            run = run - dl[l:l + 1, :]
        inner = sum(sm[l:l + 1, :] * dsm[l] for l in range(DEPTH))
        for l in range(DEPTH):
            o_ref[l:l + 1, :] = sm[l:l + 1, :] * (dsm[l] - inner)

    return pl.pallas_call(body, out_shape=SDS(lb.shape, F32), name="lb_bwd")(lb, dlb_all)


def _adamw(g, w, m, v):
    r, c = g.shape
    tr = _tile(r, 264)
    c1 = 1.0 / (1.0 - ADAM_B1 ** ADAM_STEP)
    c2 = 1.0 / (1.0 - ADAM_B2 ** ADAM_STEP)

    def body(g_ref, w_ref, m_ref, v_ref, d_ref, mo_ref, vo_ref):
        gg = g_ref[...]
        mn = ADAM_B1 * m_ref[...] + (1.0 - ADAM_B1) * gg
        vn = ADAM_B2 * v_ref[...] + (1.0 - ADAM_B2) * gg * gg
        d_ref[...] = -ADAM_LR * ((mn * c1) / (jnp.sqrt(vn * c2) + ADAM_EPS) + ADAM_WD * w_ref[...])
        mo_ref[...] = mn
        vo_ref[...] = vn

    spec = pl.BlockSpec((tr, c), lambda i: (i, 0))
    return pl.pallas_call(body, grid=(r // tr,), in_specs=[spec] * 4, out_specs=[spec] * 3, out_shape=[SDS(g.shape, F32)] * 3,
                          compiler_params=_params(1), name="adamw")(g, w, m, v)


def _tile16(n, target):
    return _tile(n // 2, target // 2) * 2 if n % 16 == 0 else _tile(n, target)


def _add_cores(g, got, core):
    k, r, c = got.shape
    tr = _tile16(r, 264)

    def body(c_ref, a_ref, b_ref, o_ref):
        o_ref[...] = (a_ref[...] + b_ref[...].astype(F32)).astype(o_ref.dtype)

    spec = pl.BlockSpec((None, tr, c), lambda s, i, cr: (s, i, 0))
    return pl.pallas_call(
        body, grid_spec=pltpu.PrefetchScalarGridSpec(
            num_scalar_prefetch=1, grid=(k, r // tr),
            in_specs=[pl.BlockSpec((None, None, tr, c), lambda s, i, cr: (cr[0], s, i, 0)), spec], out_specs=spec),
        out_shape=SDS(got.shape, got.dtype), compiler_params=_params(2), name="add_cores")(core, g, got)


def _sum_chips(parts, own, chip, core):
    k, r, c = parts.shape
    tr = _tile16(r, 264)

    def body(chip_ref, core_ref, *refs):
        part_refs, own_ref, o_ref = refs[:k], refs[k], refs[k + 1]
        mine = own_ref[...].astype(F32)
        acc = None
        for s in range(k):
            term = jnp.where(chip_ref[0] == s, mine, part_refs[s][...].astype(F32))
            acc = term if acc is None else acc + term
        o_ref[...] = acc

    def other(s):
        return pl.BlockSpec((None, tr, c), lambda i, ch, co: (jnp.where(ch[0] == s, (s + 1) % k, s), i, 0))

    return pl.pallas_call(
        body, grid_spec=pltpu.PrefetchScalarGridSpec(
            num_scalar_prefetch=2, grid=(r // tr,),
            in_specs=[other(s) for s in range(k)] + [pl.BlockSpec((None, tr, c), lambda i, ch, co: (ch[0], i, 0))],
            out_specs=pl.BlockSpec((None, tr, c), lambda i, ch, co: (co[0], i, 0))),
        out_shape=SDS((2, r, c), F32), compiler_params=_params(1), name="sum_chips")(chip, core, *([parts] * k), own)


def _meta_grad(dh, nb, nc):
    d = dh.shape[1]

    def body(x_ref, o_ref):
        @pl.when(pl.program_id(0) == 0)
        def _():
            o_ref[...] = jnp.zeros_like(o_ref)

        o_ref[...] += x_ref[PAD:CH, :]

    return pl.pallas_call(body, grid=(nb,), in_specs=[pl.BlockSpec((CH, d), lambda b: (b * nc, 0))],
                          out_specs=pl.BlockSpec((N_META, d), lambda b: (0, 0)), out_shape=SDS((N_META, d), F32),
                          compiler_params=_params(1), name="meta_grad")(dh)


ANY = pl.BlockSpec(memory_space=pl.ANY)


def _place():
    x, y, c = lax.axis_index("x"), lax.axis_index("y"), lax.axis_index("c")
    chips = [(1 - x, y), (x, 1 - y), (1 - x, 1 - y)]
    return x, y, c, chips


def _remote(src, dst, send_sems, recv_sems, k, to):
    return pltpu.make_async_remote_copy(src_ref=src, dst_ref=dst, send_sem=send_sems.at[k], recv_sem=recv_sems.at[k],
                                        device_id=to, device_id_type=MESH)


def _gather_weights(pbs, ps):
    nt = len(pbs)

    def body(*refs):
        pb_refs, ps_ref, gb_refs, gs_ref = refs[:nt], refs[nt], refs[nt + 1:2 * nt + 1], refs[2 * nt + 1]
        send_sems, recv_sems, local_sems = refs[2 * nt + 2:]
        x, y, c, chips = _place()
        s = 2 * x + y
        sib = (x, y, 1 - c)
        l1 = pltpu.make_async_copy(ps_ref, gs_ref.at[s], local_sems.at[0])
        l1.start()
        sends = []
        for k, (px, py) in enumerate(chips):
            for t in range(nt):
                sends.append(_remote(pb_refs[t].at[c], gb_refs[t].at[s, c], send_sems, recv_sems, 6 * t + k, (px, py, c)))
            sends.append(_remote(ps_ref, gs_ref.at[s], send_sems, recv_sems, 6 * nt + k, (px, py, c)))
        for cp in sends:
            cp.start()
        for k, (px, py) in enumerate(chips):
            sk = 2 * px + py
            for t in range(nt):
                _remote(pb_refs[t].at[c], gb_refs[t].at[sk, c], send_sems, recv_sems, 6 * t + k, sib).wait_recv()
                fwd = _remote(gb_refs[t].at[sk, c], gb_refs[t].at[sk, c], send_sems, recv_sems, 6 * t + 3 + k, sib)
                fwd.start()
                sends.append(fwd)
        for k, (px, py) in enumerate(chips):
            sk = 2 * px + py
            for t in range(nt):
                _remote(pb_refs[t].at[c], gb_refs[t].at[sk, 1 - c], send_sems, recv_sems, 6 * t + 3 + k, sib).wait_recv()
            _remote(ps_ref, gs_ref.at[sk], send_sems, recv_sems, 6 * nt + k, sib).wait_recv()
        for cp in sends:
            cp.wait_send()
        l1.wait()

    nsem = 6 * nt + 3
    out = pl.pallas_call(
        body, in_specs=[ANY] * (nt + 1), out_specs=[ANY] * (nt + 1),
        out_shape=[SDS((4,) + pb.shape, pb.dtype) for pb in pbs] + [SDS((4,) + ps.shape, ps.dtype)],
        scratch_shapes=[pltpu.SemaphoreType.DMA((nsem,)), pltpu.SemaphoreType.DMA((nsem,)), pltpu.SemaphoreType.DMA((1,))],
        name="gather_weights")(*pbs, ps)
    return out[:nt], out[nt]


def _contain(wpad, shift):
    r, cw = wpad.shape
    tr = _tile16(r, 256)

    def body(n_ref, x_ref, o_ref):
        o_ref[...] = pltpu.roll(x_ref[...], n_ref[0], axis=1).astype(o_ref.dtype)

    spec = pl.BlockSpec((tr, cw), lambda i, n: (i, 0))
    return pl.pallas_call(
        body, grid_spec=pltpu.PrefetchScalarGridSpec(num_scalar_prefetch=1, grid=(r // tr,), in_specs=[spec], out_specs=spec),
        out_shape=SDS((r, cw), BF16), compiler_params=_params(1), name="contain")(shift, wpad)


def _place_own(gb, pb, chip):
    _, _, r, c = gb.shape
    tr = _tile16(r, 1100)

    def body(s_ref, p_ref, g_in, o_ref):
        o_ref[...] = p_ref[...]

    return pl.pallas_call(
        body, grid_spec=pltpu.PrefetchScalarGridSpec(
            num_scalar_prefetch=1, grid=(2, r // tr),
            in_specs=[pl.BlockSpec((None, tr, c), lambda h, i, s: (h, i, 0)), ANY],
            out_specs=pl.BlockSpec((None, None, tr, c), lambda h, i, s: (s[0], h, i, 0))),
        out_shape=SDS(gb.shape, gb.dtype), input_output_aliases={2: 0}, compiler_params=_params(2),
        name="place_own")(chip, pb, gb)


def _sem_scratch(n_remote, n_local):
    return [pltpu.SemaphoreType.DMA((n_remote,)), pltpu.SemaphoreType.DMA((n_remote,)), pltpu.SemaphoreType.DMA((n_local,))]


def _swap_halves(sends):
    nt = len(sends)

    def body(*refs):
        s_refs, got_refs = refs[:nt], refs[nt:2 * nt]
        send_sems, recv_sems = refs[2 * nt:]
        x, y, c, _ = _place()
        sib = (x, y, 1 - c)
        remote = [_remote(s_refs[t].at[1 - c, s], got_refs[t].at[s], send_sems, recv_sems, 4 * t + s, sib)
                  for t in range(nt) for s in range(4)]
        for cp in remote:
            cp.start()
        for cp in remote:
            cp.wait()

    return pl.pallas_call(
        body, in_specs=[ANY] * nt, out_specs=[ANY] * nt, out_shape=[SDS(g.shape[1:], g.dtype) for g in sends],
        scratch_shapes=[pltpu.SemaphoreType.DMA((4 * nt,)), pltpu.SemaphoreType.DMA((4 * nt,))], name="swap_halves")(*sends)


def _scatter_chip_sums(parts):
    nt = len(parts)

    def body(*refs):
        a_refs, r_refs = refs[:nt], refs[nt:2 * nt]
        send_sems, recv_sems = refs[2 * nt:]
        x, y, c, chips = _place()
        s = 2 * x + y
        sends = [_remote(a_refs[t].at[2 * px + py], r_refs[t].at[s], send_sems, recv_sems, 3 * t + k, (px, py, c))
                 for t in range(nt) for k, (px, py) in enumerate(chips)]
        for cp in sends:
            cp.start()
        for t in range(nt):
            for k, (px, py) in enumerate(chips):
                _remote(a_refs[t].at[s], r_refs[t].at[2 * px + py], send_sems, recv_sems, 3 * t + k, (px, py, c)).wait_recv()
        for cp in sends:
            cp.wait_send()

    return pl.pallas_call(
        body, in_specs=[ANY] * nt, out_specs=[ANY] * nt, out_shape=[SDS(a.shape, a.dtype) for a in parts],
        scratch_shapes=[pltpu.SemaphoreType.DMA((3 * nt,)), pltpu.SemaphoreType.DMA((3 * nt,))],
        name="scatter_chip_sums")(*parts)


def _join_halves(fs):
    nt = len(fs)

    def body(*refs):
        f_refs = refs[nt:2 * nt]
        send_sems, recv_sems = refs[2 * nt:]
        x, y, c, _ = _place()
        sib = (x, y, 1 - c)
        sends = [_remote(f_refs[t].at[c], f_refs[t].at[c], send_sems, recv_sems, t, sib) for t in range(nt)]
        for cp in sends:
            cp.start()
        for t in range(nt):
            _remote(f_refs[t].at[c], f_refs[t].at[1 - c], send_sems, recv_sems, t, sib).wait_recv()
        for cp in sends:
            cp.wait_send()

    return pl.pallas_call(
        body, in_specs=[ANY] * nt, out_specs=[ANY] * nt, out_shape=[SDS(f.shape, f.dtype) for f in fs],
        input_output_aliases={t: t for t in range(nt)},
        scratch_shapes=[pltpu.SemaphoreType.DMA((nt,)), pltpu.SemaphoreType.DMA((nt,))], name="join_halves")(*fs)


def _uncontain(cont, n_head, width):
    r, cw = cont.shape
    tr = _tile(r, 256)

    def body(n_ref, x_ref, o_ref):
        o_ref[...] = pltpu.roll(x_ref[...], n_ref[0], axis=1)[:, :width]

    return pl.pallas_call(
        body, grid_spec=pltpu.PrefetchScalarGridSpec(
            num_scalar_prefetch=1, grid=(r // tr,), in_specs=[pl.BlockSpec((tr, cw), lambda i, n: (i, 0))],
            out_specs=pl.BlockSpec((tr, width), lambda i, n: (i, 0))),
        out_shape=SDS((r, width), F32), compiler_params=_params(1), name="uncontain")(n_head, cont)


WEIGHTS = ("meta_tokens", "norm_w", "w_in", "conv_w", "a_log", "dt_bias", "gnorm_a", "gnorm_b", "hgrn_lower_bounds",
           "w_branch_a", "w_branch_b", "w_out", "final_norm_w")
SHARD_AXIS = {"meta_tokens": 1, "w_in": 2, "conv_w": 2, "w_branch_a": 2, "w_branch_b": 2, "w_out": 1}
FLAT_C = 1024


def _flat(parts, rows, cols=FLAT_C):
    v = jnp.concatenate([p.reshape(-1) for p in parts])
    return jnp.pad(v, (0, rows * cols - v.shape[0])).reshape(rows, cols)


def _local_step(x, target, w, lay):
    nb, seq, d = x.shape
    tp = CH + seq
    nc = tp // CH
    n = nb * tp
    e_mat, s_mat = _gate_consts()
    lb_all = _lb_fwd(w["hgrn_lower_bounds"])
    h = jnp.concatenate([jnp.zeros((nb, PAD, d), F32), jnp.broadcast_to(w["meta_tokens"][None], (nb, N_META, d)), x],
                        axis=1).reshape(n, d)
    rep = lambda a: jnp.repeat(a, HD)[None, :]
    saved = []
    for l in range(DEPTH):
        nw = w["norm_w"][l][None, :]
        proj, xn = _norm_proj_fwd(h, nw, w["w_in"][l])
        qkv = _gdn_prep_fwd(proj, w["conv_w"][l], lay, nb, tp)
        alog, dtb = rep(w["a_log"][l]), rep(w["dt_bias"][l])
        oa, sa, sva = _gdn_fwd(qkv, proj, e_mat, alog, dtb, lay, nb, nc)
        lbl = lb_all[l][None, :]
        ob, sb = _hgrn_fwd(proj, lbl, lay, nb, nc)
        ga, gb = w["gnorm_a"][l][None, :], w["gnorm_b"][l][None, :]
        hn = _merge_fwd(h, oa, ob, proj, ga, gb, w["w_branch_a"][l], w["w_branch_b"][l], w["w_out"][l], lay)
        saved.append((h, nw, proj, qkv, alog, dtb, oa, sa, lbl, ob, sb, ga, gb, xn, sva))
        h = hn
    target_p = jnp.pad(target, ((0, 0), (CH, 0), (0, 0))).reshape(n, d)
    lp, dh, dfw = _loss_head(h, target_p, w["final_norm_w"][None, :], nb, tp)
    loss = jnp.sum(lp[::8, 0])
    g = {n_: [None] * DEPTH for n_ in WEIGHTS}
    dlb_all = [None] * DEPTH
    for l in reversed(range(DEPTH)):
        h, nw, proj, qkv, alog, dtb, oa, sa, lbl, ob, sb, ga, gb, xn, sva = saved[l]
        dproj, doa, dob, dwa, dwb, dwo, dga, dgb = _merge_bwd(dh, oa, ob, proj, ga, gb, w["w_branch_a"][l],
                                                             w["w_branch_b"][l], w["w_out"][l], lay, tp)
        dproj, acc_b = _hgrn_bwd(proj, lbl, sb, dob, dproj, lay, nb, nc)
        dqkv, dproj, acc_a = _gdn_bwd(qkv, proj, e_mat, s_mat, alog, dtb, sa, sva, doa, dproj, lay, nb, nc)
        dproj, dconv = _gdn_prep_bwd(proj, w["conv_w"][l], dqkv, dproj, lay, nb, tp)
        dh, dnw = _proj_bwd_dx(dproj, w["w_in"][l], h, nw, dh, tp)
        g["w_in"][l] = _proj_bwd_dw(dproj, xn, tp)
        g["norm_w"][l] = dnw[0]
        g["conv_w"][l] = dconv
        g["a_log"][l] = acc_a[0, ::HD]
        g["dt_bias"][l] = acc_a[1, ::HD]
        g["gnorm_a"][l], g["gnorm_b"][l] = dga[0], dgb[0]
        g["w_branch_a"][l], g["w_branch_b"][l], g["w_out"][l] = dwa, dwb, dwo
        dlb_all[l] = acc_b[0]
    grads = {n_: jnp.stack(v) for n_, v in g.items() if v[0] is not None}
    grads["hgrn_lower_bounds"] = _lb_bwd(w["hgrn_lower_bounds"], jnp.stack(dlb_all))
    grads["final_norm_w"] = dfw[0]
    grads["meta_tokens"] = _meta_grad(dh, nb, nc)
    grad_x = dh.reshape(nb, tp, d)[:, CH:, :]
    return loss, grad_x, grads


def kernel(x, meta_tokens, norm_w, w_in, conv_w, a_log, dt_bias, gnorm_a, gnorm_b, hgrn_lower_bounds, w_branch_a, w_branch_b, w_out, final_norm_w, loss_target, m_meta_tokens, m_norm_w, m_w_in, m_conv_w, m_a_log, m_dt_bias, m_gnorm_a, m_gnorm_b, m_hgrn_lower_bounds, m_w_branch_a, m_w_branch_b, m_w_out, m_final_norm_w, v_meta_tokens, v_norm_w, v_w_in, v_conv_w, v_a_log, v_dt_bias, v_gnorm_a, v_gnorm_b, v_hgrn_lower_bounds, v_w_branch_a, v_w_branch_b, v_w_out, v_final_norm_w):
    wl = dict(meta_tokens=meta_tokens, norm_w=norm_w, w_in=w_in, conv_w=conv_w, a_log=a_log, dt_bias=dt_bias, gnorm_a=gnorm_a,
              gnorm_b=gnorm_b, hgrn_lower_bounds=hgrn_lower_bounds, w_branch_a=w_branch_a, w_branch_b=w_branch_b, w_out=w_out,
              final_norm_w=final_norm_w)
    ml = dict(zip(WEIGHTS, (m_meta_tokens, m_norm_w, m_w_in, m_conv_w, m_a_log, m_dt_bias, m_gnorm_a, m_gnorm_b,
                            m_hgrn_lower_bounds, m_w_branch_a, m_w_branch_b, m_w_out, m_final_norm_w)))
    vl = dict(zip(WEIGHTS, (v_meta_tokens, v_norm_w, v_w_in, v_conv_w, v_a_log, v_dt_bias, v_gnorm_a, v_gnorm_b,
                            v_hgrn_lower_bounds, v_w_branch_a, v_w_branch_b, v_w_out, v_final_norm_w)))
    d = x.shape[2]
    lay = _Layout(d)
    nchip = 4

    big = ("w_in", "w_branch_a", "w_branch_b", "w_out")
    small = ("conv_w", "meta_tokens")
    table, heads, cw = lay.pieces(nchip)
    sw = wl["w_in"].shape[2]
    chip_id = (2 * lax.axis_index("x") + lax.axis_index("y")).astype(jnp.int32)
    n_head = sum(jnp.where(chip_id == s, heads[s], 0) for s in range(nchip)).astype(jnp.int32)
    w_pad = jnp.pad(wl["w_in"], ((0, 0), (0, 0), (0, cw - sw))).reshape(DEPTH * d, cw)
    shift = jnp.where(n_head == 0, 0, cw - n_head).astype(jnp.int32).reshape(1)
    pbs = [_contain(w_pad, shift).reshape(DEPTH, d, cw)] + [wl[n].astype(BF16) for n in big[1:]]
    nsmall = sum(int(np.prod(wl[n].shape)) for n in small)
    rs = -(-nsmall // (HD * 8)) * 8
    ps = jnp.pad(jnp.concatenate([wl[n].reshape(-1) for n in small]), (0, rs * HD - nsmall)).reshape(rs, HD)
    gbig, gsmall = _gather_weights(pbs, ps)
    gbig = [_place_own(g, p, chip_id.reshape(1)) for g, p in zip(gbig, pbs)]
    gsmall = gsmall.reshape(nchip, -1)

    wf = dict(wl)
    wf["w_in"] = lay.from_containers([gbig[0][s] for s in range(nchip)])
    for i, n in enumerate(big[1:], start=1):
        wf[n] = jnp.concatenate([gbig[i][s] for s in range(nchip)], axis=SHARD_AXIS[n])
    o = 0
    for n in small:
        sz = int(np.prod(wl[n].shape))
        a = gsmall[:, o:o + sz].reshape((nchip,) + wl[n].shape)
        wf[n] = jnp.concatenate([a[s] for s in range(nchip)], axis=SHARD_AXIS[n])
        o += sz

    loss_part, grad_x, gfull = _local_step(x, loss_target, wf, lay)
    loss = lax.psum(loss_part, ("x", "y", "c"))

    sw = wl["w_in"].shape[2]
    conts, heads = lay.containers(gfull["w_in"], nchip)
    dd = wl["w_branch_a"].shape[2]
    rows_o = wl["w_out"].shape[1]
    by_dest = lambda g, n: [lax.slice_in_dim(g, s * wl[n].shape[SHARD_AXIS[n]], (s + 1) * wl[n].shape[SHARD_AXIS[n]],
                                            axis=SHARD_AXIS[n]) if n in SHARD_AXIS else g for s in range(nchip)]
    small_names = tuple(n for n in WEIGHTS if n not in big)
    nsm = sum(int(np.prod(wl[n].shape)) for n in small_names)
    rsm = -(-nsm // (2 * HD * 8)) * 8
    pack_small = lambda parts: _flat(parts, 2 * rsm, HD).reshape(2, rsm, HD)
    small_by_dest = [by_dest(gfull[n], n) for n in small_names]
    gs = [jnp.stack(conts, axis=1),
          jnp.stack(by_dest(gfull["w_branch_a"], "w_branch_a"), axis=1),
          jnp.stack(by_dest(gfull["w_branch_b"], "w_branch_b"), axis=1),
          gfull["w_out"].reshape(DEPTH, nchip, rows_o, d),
          jnp.stack([pack_small([p[s] for p in small_by_dest]) for s in range(nchip)], axis=1)]
    gs = [g.reshape((2, nchip, -1, g.shape[-1])) for g in gs]
    my_chip = (2 * lax.axis_index("x") + lax.axis_index("y")).astype(jnp.int32)
    my_core = lax.axis_index("c").astype(jnp.int32)
    got = _swap_halves([g.astype(BF16) for g in gs[:4]] + gs[4:])
    chip_sums = [_add_cores(g, b, my_core.reshape(1)) for g, b in zip(gs, got)]
    by_chip = _scatter_chip_sums(chip_sums)
    full = _join_halves([_sum_chips(p, a, my_chip.reshape(1), my_core.reshape(1)) for p, a in zip(by_chip, chip_sums)])
    n_head = sum(jnp.where(my_chip == s, heads[s], 0) for s in range(nchip)).astype(jnp.int32).reshape(1)
    g_w_in = _uncontain(full[0].reshape(DEPTH * d, -1), n_head, sw)
    g2 = {"w_in": g_w_in, "w_branch_a": full[1].reshape(-1, dd), "w_branch_b": full[2].reshape(-1, dd),
          "w_out": full[3].reshape(-1, d), "small": full[4].reshape(2 * rsm, HD)}

    def two_d(src, n):
        if n == "small":
            return _flat([src[k] for k in small_names], 2 * rsm, HD)
        return src[n].reshape(g2[n].shape)

    outs = {}
    for n in big + ("small",):
        delta, mnew, vnew = _adamw(g2[n], two_d(wl, n), two_d(ml, n), two_d(vl, n))
        outs[n] = (g2[n], delta, mnew, vnew)
    res = [{}, {}, {}, {}]
    for i in range(4):
        for n in big:
            res[i][n] = outs[n][i].reshape(wl[n].shape)
        v, o = outs["small"][i].reshape(-1), 0
        for n in small_names:
            sz = int(np.prod(wl[n].shape))
            res[i][n] = v[o:o + sz].reshape(wl[n].shape)
            o += sz
    return (loss, grad_x, *[res[0][n] for n in WEIGHTS], *[res[1][n] for n in WEIGHTS], *[res[2][n] for n in WEIGHTS],
            *[res[3][n] for n in WEIGHTS])
```

```python
import functools

import numpy as np
import jax
import jax.numpy as jnp
from jax import lax
from jax.experimental import pallas as pl
from jax.experimental.pallas import tpu as pltpu

F32 = jnp.float32
BF16 = jnp.bfloat16
HI = lax.Precision.HIGHEST
SDS = jax.ShapeDtypeStruct

NH = 4
HD = 128
HW = NH * HD
N_META = 16
CH = 64
SUB = 16
PAD = CH - N_META
EPS = 1e-6
Q_SCALE = HD ** -0.5
DEPTH = 2
CONV_K = 4
PREP_HEADS = 2
VMEM_LIMIT = 56 * 1024 * 1024
ADAM_LR, ADAM_B1, ADAM_B2, ADAM_EPS, ADAM_WD, ADAM_STEP = 0.001, 0.9, 0.999, 1e-08, 0.01, 10
MESH = pl.DeviceIdType.MESH


def _nn(a, b):
    return jnp.dot(a, b, precision=HI, preferred_element_type=F32)


def _nt(a, b):
    return lax.dot_general(a, b, (((1,), (1,)), ((), ())), precision=HI, preferred_element_type=F32)


def _tn(a, b):
    return _nn(a.T, b)


def _scan_rows(x, group, reverse=False):
    n = x.shape[0]
    pos = lax.bitwise_and(_iota2(x.shape, 0), group - 1)
    s = 1
    while s < group:
        if reverse:
            x = x + jnp.where(pos < group - s, pltpu.roll(x, n - s, axis=0), 0.0)
        else:
            x = x + jnp.where(pos >= s, pltpu.roll(x, s, axis=0), 0.0)
        s *= 2
    return x


def _bnn(a, b):
    return jnp.dot(a.astype(BF16), b.astype(BF16), preferred_element_type=F32)


def _bnt(a, b):
    return lax.dot_general(a.astype(BF16), b.astype(BF16), (((1,), (1,)), ((), ())), preferred_element_type=F32)


def _btn(a, b):
    return lax.dot_general(a.astype(BF16), b.astype(BF16), (((0,), (0,)), ((), ())), preferred_element_type=F32)


def _hi_lo(x):
    hi = x.astype(jnp.bfloat16)
    return hi, (x - hi.astype(F32)).astype(jnp.bfloat16)


def _dot3(dims):
    def f(a, b):
        ah, al = _hi_lo(a)
        bh, bl = _hi_lo(b)
        d = lambda p, q: lax.dot_general(p, q, (dims, ((), ())), preferred_element_type=F32)
        return d(ah, bh) + (d(ah, bl) + d(al, bh))
    return f


_rnn, _rnt, _rtn = _dot3(((1,), (0,))), _dot3(((1,), (1,))), _dot3(((0,), (0,)))
_enn, _ent, _etn = _bnn, _bnt, _btn
_hnn, _hnt, _htn = _bnn, _bnt, _btn


def _rr(x):
    return x


def _sig(x):
    return jax.nn.sigmoid(x)


def _silu(x):
    return x * _sig(x)


def _dsilu(x):
    s = _sig(x)
    return s * (1.0 + x * (1.0 - s))


def _softplus(x):
    return jnp.maximum(x, 0.0) + jnp.log(1.0 + jnp.exp(-jnp.abs(x)))


def _logsig(x):
    return jnp.minimum(x, 0.0) - jnp.log(1.0 + jnp.exp(-jnp.abs(x)))


def _rs(x):
    return jnp.sum(x, axis=-1, keepdims=True)


def _params(n_axes):
    return pltpu.CompilerParams(dimension_semantics=("arbitrary",) * n_axes, vmem_limit_bytes=VMEM_LIMIT)


def _tile(n, target, mult=8):
    best = mult
    for t in range(mult, target + 1, mult):
        if n % t == 0:
            best = t
    assert n % best == 0, (n, mult)
    return best


def _ctile(pw, most=7):
    return HD * max(k for k in range(1, most + 1) if (pw // HD) % k == 0)


def _iota2(shape, axis):
    return lax.broadcasted_iota(jnp.int32, shape, axis)


class _Layout:
    def __init__(self, d):
        self.d = d
        self.wm = 2 * HW + 2 * d
        self.c_qkv = self.wm
        self.c_b = self.wm + 3 * HW
        self.c_ba = self.wm + 6 * HW
        self.pw = self.c_ba + HD
        assert self.c_b % (3 * HW) == 0
        o = 0
        segs = {}
        for name, w in (("a_q", HW), ("a_k", HW), ("a_v", HW), ("ba", 2 * NH), ("a_z", HW), ("b_q", HW), ("b_f", HW),
                        ("b_i", HW), ("b_g", HW), ("gate_a", d), ("gate_b", d)):
            segs[name] = (o, o + w)
            o += w
        self.segs = segs
        self.width = o
        self.order = ("a_z", "b_g", "gate_a", "gate_b", "a_q", "a_k", "a_v", "b_q", "b_f", "b_i", "ba")

    def to_kernel(self, w):
        parts = [w[..., self.segs[n][0]:self.segs[n][1]] for n in self.order]
        parts.append(jnp.zeros(w.shape[:-1] + (HD - 2 * NH,), w.dtype))
        return jnp.concatenate(parts, axis=-1)

    def containers(self, g, nchip):
        table, heads, cw = self.pieces(nchip)
        out = []
        for s in range(nchip):
            parts, at = [], 0
            for kcol, w, ccol in sorted(table[s], key=lambda p: p[2]):
                if ccol > at:
                    parts.append(jnp.zeros(g.shape[:-1] + (ccol - at,), g.dtype))
                parts.append(g[..., kcol:kcol + w])
                at = ccol + w
            if at < cw:
                parts.append(jnp.zeros(g.shape[:-1] + (cw - at,), g.dtype))
            out.append(jnp.concatenate(parts, axis=-1))
        return out, heads

    def pieces(self, nchip):
        off, where = 0, {}
        for n in self.order:
            where[n] = off
            off += self.segs[n][1] - self.segs[n][0]
        names = sorted(self.segs, key=lambda n: self.segs[n][0])
        sw = self.width // nchip
        cw = -(-sw // HD) * HD
        table, heads = [], []
        for s in range(nchip):
            lo, hi = s * sw, (s + 1) * sw
            pieces = []
            for n in names:
                a, b = max(lo, self.segs[n][0]), min(hi, self.segs[n][1])
                if a < b:
                    pieces.append((where[n] + a - self.segs[n][0], b - a))
            start, width = pieces[0]
            n_head = min((-start) % HD, width)
            body = ([(start + n_head, width - n_head)] if width > n_head else []) + pieces[1:]
            rows, at = [], 0
            for c, w in body:
                rows.append((c, w, at))
                at += w
            if n_head:
                rows.append((start, n_head, cw - n_head))
            table.append(rows)
            heads.append(n_head)
        return table, heads, cw

    def from_containers(self, conts):
        table, _, _ = self.pieces(len(conts))
        cut = sorted((kcol, w, s, ccol) for s, rows in enumerate(table) for kcol, w, ccol in rows)
        parts, at = [], 0
        for kcol, w, s, ccol in cut:
            assert kcol == at, (kcol, at)
            parts.append(conts[s][..., ccol:ccol + w])
            at = kcol + w
        parts.append(jnp.zeros(conts[0].shape[:-1] + (self.pw - at,), conts[0].dtype))
        return jnp.concatenate(parts, axis=-1)

    def from_kernel(self, g):
        off, where = 0, {}
        for n in self.order:
            w = self.segs[n][1] - self.segs[n][0]
            where[n] = (off, off + w)
            off += w
        names = sorted(self.segs, key=lambda n: self.segs[n][0])
        return jnp.concatenate([g[..., where[n][0]:where[n][1]] for n in names], axis=-1)


def _norm_proj_fwd(h, nw, wp):
    n, d = h.shape
    pw = wp.shape[1]
    tm, tn = _tile(n, 1408, HD), _ctile(pw)

    def body(h_ref, nw_ref, w_ref, o_ref, xt_ref, xn_ref):
        @pl.when(pl.program_id(1) == 0)
        def _():
            x = h_ref[...]
            r = lax.rsqrt(jnp.mean(x * x, axis=-1, keepdims=True) + EPS)
            xn = (x * r * nw_ref[...]).astype(BF16)
            xn_ref[...] = xn
            xt_ref[...] = xn.T

        o_ref[...] = jnp.dot(xn_ref[...], w_ref[...], preferred_element_type=F32)

    return pl.pallas_call(
        body, grid=(n // tm, pw // tn),
        in_specs=[pl.BlockSpec((tm, d), lambda i, j: (i, 0)), pl.BlockSpec((1, d), lambda i, j: (0, 0)),
                  pl.BlockSpec((d, tn), lambda i, j: (0, j))],
        out_specs=[pl.BlockSpec((tm, tn), lambda i, j: (i, j)), pl.BlockSpec((d, tm), lambda i, j: (0, i))],
        out_shape=[SDS((n, pw), F32), SDS((d, n), BF16)], scratch_shapes=[pltpu.VMEM((tm, d), BF16)],
        compiler_params=_params(2), name="norm_proj_fwd")(h, nw, wp)


def _row_valid(tm, tp, base):
    row = base + _iota2((tm, 1), 0)
    return lax.rem(row, tp) >= PAD


def _proj_bwd_dx(dproj, wp, h, nw, dhn, tp):
    n, d = h.shape
    pw = wp.shape[1]
    tm, tk = _tile16(n, 1056), _ctile(pw)
    nk = pw // tk

    def body(dp_ref, w_ref, h_ref, nw_ref, dhn_ref, dh_ref, dnw_ref, acc_ref):
        i, k = pl.program_id(0), pl.program_id(1)

        @pl.when(k == 0)
        def _():
            acc_ref[...] = jnp.zeros_like(acc_ref)

        @pl.when((i == 0) & (k == 0))
        def _():
            dnw_ref[...] = jnp.zeros_like(dnw_ref)

        valid = _row_valid(tm, tp, i * tm)
        dp = jnp.where(valid, dp_ref[...], 0.0)
        acc_ref[...] += _bnt(dp, w_ref[...])

        @pl.when(k == nk - 1)
        def _():
            x = h_ref[...]
            r = lax.rsqrt(jnp.mean(x * x, axis=-1, keepdims=True) + EPS)
            xh = x * r
            dxn = acc_ref[...]
            dnw_ref[...] += jnp.sum(dxn * xh, axis=0, keepdims=True)
            dxh = dxn * nw_ref[...]
            dh_ref[...] = dhn_ref[...] + r * (dxh - xh * jnp.mean(dxh * xh, axis=-1, keepdims=True))

    return pl.pallas_call(
        body, grid=(n // tm, nk),
        in_specs=[pl.BlockSpec((tm, tk), lambda i, k: (i, k)), pl.BlockSpec((d, tk), lambda i, k: (0, k)),
                  pl.BlockSpec((tm, d), lambda i, k: (i, 0)), pl.BlockSpec((1, d), lambda i, k: (0, 0)),
                  pl.BlockSpec((tm, d), lambda i, k: (i, 0))],
        out_specs=[pl.BlockSpec((tm, d), lambda i, k: (i, 0)), pl.BlockSpec((1, d), lambda i, k: (0, 0))],
        out_shape=[SDS((n, d), F32), SDS((1, d), F32)],
        scratch_shapes=[pltpu.VMEM((tm, d), F32)], compiler_params=_params(2), name="proj_bwd_dx")(dproj, wp, h, nw, dhn)


def _proj_bwd_dw(dproj, xt, tp):
    d, n = xt.shape
    pw = dproj.shape[1]
    tm, tn = _tile(n, 1408, HD), _ctile(pw)

    def body(dp_ref, xt_ref, dw_ref):
        i = pl.program_id(1)

        @pl.when(i == 0)
        def _():
            dw_ref[...] = jnp.zeros_like(dw_ref)

        dp = jnp.where(_row_valid(tm, tp, i * tm), dp_ref[...], 0.0)
        dw_ref[...] += jnp.dot(xt_ref[...], dp.astype(BF16), preferred_element_type=F32)

    return pl.pallas_call(
        body, grid=(pw // tn, n // tm),
        in_specs=[pl.BlockSpec((tm, tn), lambda j, i: (i, j)), pl.BlockSpec((d, tm), lambda j, i: (0, i))],
        out_specs=pl.BlockSpec((d, tn), lambda j, i: (0, j)), out_shape=SDS((d, pw), F32),
        compiler_params=_params(2), name="proj_bwd_dw")(dproj, xt)


def _conv_silu(x, w, row):
    c = x * w[CONV_K - 1:CONV_K, :]
    for k in range(1, CONV_K):
        c = c + jnp.where(row >= k, pltpu.roll(x, k, axis=0), 0.0) * w[CONV_K - 1 - k:CONV_K - k, :]
    return c


def _gdn_prep_fwd(proj, conv_w, lay, nb, tp):
    n = proj.shape[0]
    nblk = 3 * NH
    cb = lay.c_qkv // HD
    assert cb % PREP_HEADS == 0 and nblk % PREP_HEADS == 0

    def body(p_ref, w_ref, o_ref):
        row = _iota2((tp, HD), 0)
        for t in range(PREP_HEADS):
            j = pl.program_id(1) * PREP_HEADS + t
            ls = slice(t * HD, (t + 1) * HD)
            c = _conv_silu(p_ref[:, ls], w_ref[:, ls], row)
            s = _silu(c)
            r = lax.rsqrt(_rs(s * s) + EPS)
            scale = jnp.where(j < NH, Q_SCALE, 1.0)
            y = jnp.where(j < 2 * NH, s * r * scale, s)
            o_ref[:, ls] = jnp.where(row >= PAD, y, 0.0)

    wd = PREP_HEADS * HD
    return pl.pallas_call(
        body, grid=(nb, nblk // PREP_HEADS),
        in_specs=[pl.BlockSpec((tp, wd), lambda b, j: (b, cb // PREP_HEADS + j)), pl.BlockSpec((CONV_K, wd), lambda b, j: (0, j))],
        out_specs=pl.BlockSpec((tp, wd), lambda b, j: (b, j)), out_shape=SDS((n, nblk * HD), F32),
        compiler_params=_params(2), name="gdn_prep_fwd")(proj, conv_w)


def _gdn_prep_bwd(proj, conv_w, dqkv, dproj, lay, nb, tp):
    nblk = 3 * NH
    cb = lay.c_qkv // HD

    def body(p_ref, w_ref, dy_ref, dp_in, dp_ref, dw_ref):
        b = pl.program_id(1)
        row = _iota2((tp, HD), 0)
        r4 = _iota2((CONV_K, HD), 0)
        for t in range(PREP_HEADS):
            j = pl.program_id(0) * PREP_HEADS + t
            ls = slice(t * HD, (t + 1) * HD)
            x = p_ref[:, ls]
            w = w_ref[:, ls]
            c = _conv_silu(x, w, row)
            s = _silu(c)
            dy = jnp.where(row >= PAD, dy_ref[:, ls], 0.0)
            r = lax.rsqrt(_rs(s * s) + EPS)
            nh = s * r
            scale = jnp.where(j < NH, Q_SCALE, 1.0)
            ds_n = scale * r * (dy - nh * _rs(dy * nh))
            ds = jnp.where(j < 2 * NH, ds_n, dy)
            dc = ds * _dsilu(c)
            dx = dc * w[CONV_K - 1:CONV_K, :]
            dws = [jnp.sum(dc * x, axis=0, keepdims=True)]
            for k in range(1, CONV_K):
                dx = dx + jnp.where(row < tp - k, pltpu.roll(dc, tp - k, axis=0), 0.0) * w[CONV_K - 1 - k:CONV_K - k, :]
                xs = jnp.where(row >= k, pltpu.roll(x, k, axis=0), 0.0)
                dws.append(jnp.sum(dc * xs, axis=0, keepdims=True))
            dp_ref[:, ls] = dx.astype(dp_ref.dtype)
            dw = jnp.zeros((CONV_K, HD), F32)
            for k in range(CONV_K):
                dw = dw + jnp.where(r4 == CONV_K - 1 - k, dws[k], 0.0)

            @pl.when(b == 0)
            def _():
                dw_ref[:, ls] = dw

            @pl.when(b > 0)
            def _():
                dw_ref[:, ls] += dw

    wd = PREP_HEADS * HD
    return pl.pallas_call(
        body, grid=(nblk // PREP_HEADS, nb),
        in_specs=[pl.BlockSpec((tp, wd), lambda j, b: (b, cb // PREP_HEADS + j)), pl.BlockSpec((CONV_K, wd), lambda j, b: (0, j)),
                  pl.BlockSpec((tp, wd), lambda j, b: (b, j)), pl.BlockSpec(memory_space=pl.ANY)],
        out_specs=[pl.BlockSpec((tp, wd), lambda j, b: (b, cb // PREP_HEADS + j)), pl.BlockSpec((CONV_K, wd), lambda j, b: (0, j))],
        out_shape=[SDS(dproj.shape, dproj.dtype), SDS((CONV_K, nblk * HD), F32)],
        input_output_aliases={3: 0}, compiler_params=_params(2), name="gdn_prep_bwd")(proj, conv_w, dqkv, dproj)


def _gate_consts():
    e = np.zeros((HD, 2 * HW), np.float32)
    s = np.zeros((2 * HW, HD), np.float32)
    for h in range(NH):
        e[h, h * HD:(h + 1) * HD] = 1.0
        e[NH + h, HW + h * HD:HW + (h + 1) * HD] = 1.0
        s[h * HD, h] = 1.0
        s[HW + h * HD, NH + h] = 1.0
    return jnp.asarray(e), jnp.asarray(s)


def _gdn_tri():
    i, j = _iota2((CH, CH), 0), _iota2((CH, CH), 1)
    return i >= j, i > j


def _each(fn, *lists):
    return [fn(*xs) for xs in zip(*lists)]


def _tri_inv(a_list, eye):
    p = [-a for a in a_list]
    t = [eye + x for x in p]
    for _ in range(5):
        p = _each(_rnn, p, p)
        tp_ = _each(_rnn, t, p)
        t = _each(lambda x, y: x + y, t, tp_)
    return t


def _gdn_chunks(args, solved=None):
    causal, strict = _gdn_tri()
    eye = jnp.where(_iota2((CH, CH), 0) == _iota2((CH, CH), 1), 1.0, 0.0)
    q, k, v, beta, g, s0 = (list(t) for t in zip(*args))
    gc = [_scan_rows(x, CH) for x in g]
    dm = [jnp.where(causal, jnp.exp(jnp.where(causal, x[:, :CH] - x[:, :CH].T, 0.0)), 0.0) for x in gc]
    ds = [jnp.where(strict, x, 0.0) for x in dm]
    kb = _each(lambda x, y: x * y, k, beta)
    by_k = _each(_ent, [jnp.concatenate([x, y], axis=0) for x, y in zip(kb, q)], k)
    kk = [x[:CH] for x in by_k]
    qk = [x[CH:] for x in by_k]
    a = _each(lambda x, y: x * y, kk, ds)
    eg = [jnp.exp(x) for x in gc]
    rw = _each(lambda x, y: x * y, kb, eg)
    if solved is None:
        tinv = _tri_inv(a, eye)
        rv = _each(lambda x, y: x * y, v, beta)
        u = _each(_rnn, tinv, rv)
        w = _each(_rnn, tinv, rw)
    else:
        tinv, u, w = (list(t) for t in zip(*solved))
    ws = _each(_enn, w, s0)
    vn = _each(lambda x, y: x - y, u, ws)
    p = _each(lambda x, y: x * y, qk, dm)
    qg = _each(lambda x, y: x * y, q, eg)
    out = []
    for i in range(len(args)):
        gl = gc[i][CH - 1:CH, :]
        ek = jnp.exp(gl - gc[i])
        out.append(dict(gc=gc[i], dm=dm[i], ds=ds[i], kb=kb[i], a=a[i], tinv=tinv[i], eg=eg[i], rw=rw[i], u=u[i], w=w[i],
                        vn=vn[i], p=p[i], qg=qg[i], egl=jnp.exp(gl), ek=ek, kd=k[i] * ek))
    return out


def _gdn_gates(ba, e, alog, dtb):
    raw = _nn(ba, e)
    beta = _sig(raw[:, :HW])
    za = raw[:, HW:] + dtb
    g = -jnp.exp(alog) * _softplus(za)
    return beta, g, za


def _seqs_per_step(nb):
    return 4 if nb % 4 == 0 else (2 if nb % 2 == 0 else 1)


def _gdn_fwd(qkv, proj, e_mat, alog, dtb, lay, nb, nc):
    n = qkv.shape[0]
    tp = n // nb
    cba = lay.c_ba // HD
    gb = _seqs_per_step(nb)

    def body(x_ref, ba_ref, e_ref, al_ref, dt_ref, o_ref, so_ref, sv_ref, s_ref):
        @pl.when(pl.program_id(1) == 0)
        def _():
            s_ref[...] = jnp.zeros_like(s_ref)

        args = []
        for j in range(gb):
            beta, g, _ = _gdn_gates(ba_ref[j], e_ref[...], al_ref[...], dt_ref[...])
            for h in range(NH):
                hs = slice(h * HD, (h + 1) * HD)
                args.append((x_ref[j, :, hs], x_ref[j, :, HW + h * HD:HW + (h + 1) * HD],
                             x_ref[j, :, 2 * HW + h * HD:2 * HW + (h + 1) * HD], beta[:, hs], g[:, hs], s_ref[j, h]))
        cs = _gdn_chunks(args)
        s0s = [a[5] for a in args]
        o1 = _each(lambda c, s0: _enn(c["qg"], s0), cs, s0s)
        o2 = [_enn(c["p"], c["vn"]) for c in cs]
        upd = [_etn(c["kd"], c["vn"]) for c in cs]
        res = [(o1[i] + o2[i], s0s[i] * cs[i]["egl"] + upd[i]) for i in range(len(cs))]
        zero = jnp.zeros((CH, HD - CH), F32)
        for j in range(gb):
            for h in range(NH):
                c = cs[j * NH + h]
                so_ref[j, h] = args[j * NH + h][5]
                sv_ref[j, h] = jnp.concatenate([c["u"], c["w"], c["tinv"], zero], axis=-1)
                s_ref[j, h] = res[j * NH + h][1]
            o_ref[j] = jnp.concatenate([res[j * NH + h][0] for h in range(NH)], axis=-1)

    o, st, sv = pl.pallas_call(
        body, grid=(nb // gb, nc),
        in_specs=[pl.BlockSpec((gb, CH, 3 * HW), lambda b, c: (b, c, 0)), pl.BlockSpec((gb, CH, HD), lambda b, c: (b, c, cba)),
                  pl.BlockSpec((HD, 2 * HW), lambda b, c: (0, 0)), pl.BlockSpec((1, HW), lambda b, c: (0, 0)),
                  pl.BlockSpec((1, HW), lambda b, c: (0, 0))],
        out_specs=[pl.BlockSpec((gb, CH, HW), lambda b, c: (b, c, 0)),
                   pl.BlockSpec((gb, None, NH, HD, HD), lambda b, c: (b, c, 0, 0, 0)),
                   pl.BlockSpec((gb, None, NH, CH, 3 * HD), lambda b, c: (b, c, 0, 0, 0))],
        out_shape=[SDS((nb, tp, HW), F32), SDS((nb, nc, NH, HD, HD), F32), SDS((nb, nc, NH, CH, 3 * HD), F32)],
        scratch_shapes=[pltpu.VMEM((gb, NH, HD, HD), F32)], compiler_params=_params(2), name="gdn_fwd")(
            qkv.reshape(nb, tp, 3 * HW), proj.reshape(nb, tp, -1), e_mat, alog, dtb)
    return o.reshape(n, HW), st, sv


def _gdn_bwd(qkv, proj, e_mat, s_mat, alog, dtb, states, solved, do, dproj, lay, nb, nc):
    n = qkv.shape[0]
    tp = n // nb
    cba = lay.c_ba // HD
    gb = _seqs_per_step(nb)

    def body(x_ref, ba_ref, e_ref, sm_ref, al_ref, dt_ref, st_ref, sv_ref, do_ref, dp_in, dx_ref, dba_ref, acc_ref, ds_ref):
        ci = pl.program_id(1)

        @pl.when(ci == 0)
        def _():
            ds_ref[...] = jnp.zeros_like(ds_ref)

        @pl.when((ci == 0) & (pl.program_id(0) == 0))
        def _():
            acc_ref[...] = jnp.zeros_like(acc_ref)

        causal, strict = _gdn_tri()
        alog = al_ref[...]
        row = _iota2((CH, 1), 0)
        valid = (row >= PAD) | (ci < nc - 1)
        last = row == CH - 1
        gates = [_gdn_gates(ba_ref[j], e_ref[...], alog, dt_ref[...]) for j in range(gb)]
        args, do, ds1, solved = [], [], [], []
        for j in range(gb):
            beta, g, _ = gates[j]
            for h in range(NH):
                hs = slice(h * HD, (h + 1) * HD)
                args.append((x_ref[j, :, hs], x_ref[j, :, HW + h * HD:HW + (h + 1) * HD],
                             x_ref[j, :, 2 * HW + h * HD:2 * HW + (h + 1) * HD], beta[:, hs], g[:, hs], st_ref[j, h]))
                do.append(do_ref[j, :, hs])
                ds1.append(ds_ref[j, h])
                solved.append((sv_ref[j, h, :, 2 * HD:2 * HD + CH], sv_ref[j, h, :, 0:HD], sv_ref[j, h, :, HD:2 * HD]))
        q, k, v, bh, _, s0 = (list(t) for t in zip(*args))
        cs = _gdn_chunks(args, solved)
        get = lambda name: [c[name] for c in cs]
        mul = lambda x, y: x * y
        add = lambda x, y: x + y
        dvn = _each(add, _each(_etn, get("p"), do), _each(_enn, get("kd"), ds1))
        by_s0 = _each(_ent, [jnp.concatenate([x, y], axis=0) for x, y in zip(do, dvn)], s0)
        dqg = [x[:CH] for x in by_s0]
        dw = [-x[CH:] for x in by_s0]
        dp = [jnp.where(causal, x, 0.0) for x in _each(_ent, do, get("vn"))]
        dkd = _each(_ent, get("vn"), ds1)
        ds_a = _each(_etn, get("qg"), do)
        ds_b = _each(_etn, get("w"), dvn)
        ds_new = [ds_a[i] - ds_b[i] + ds1[i] * cs[i]["egl"] for i in range(len(cs))]
        drvw = _each(_rtn, get("tinv"), [jnp.concatenate([x, y], axis=-1) for x, y in zip(dvn, dw)])
        drv = [x[:, :HD] for x in drvw]
        drw = [x[:, HD:] for x in drvw]
        uw = [sv_ref[j, h, :, 0:2 * HD] for j in range(gb) for h in range(NH)]
        da = [jnp.where(strict, -x, 0.0) for x in _each(_rnt, drvw, uw)]
        m = [da[i] * cs[i]["a"] + dp[i] * cs[i]["p"] for i in range(len(cs))]
        dkk = _each(mul, da, get("ds"))
        dqk = _each(mul, dp, get("dm"))
        by_k = _each(_enn, [jnp.concatenate([x, y], axis=0) for x, y in zip(dqk, dkk)], k)
        dq = _each(add, [x[:CH] for x in by_k], _each(mul, dqg, get("eg")))
        dkb = _each(add, [x[CH:] for x in by_k], _each(mul, drw, get("eg")))
        dk_1 = _each(_etn, dqk, q)
        dk_2 = _each(_etn, dkk, get("kb"))
        dk = [dk_1[i] + dk_2[i] + dkd[i] * cs[i]["ek"] + dkb[i] * bh[i] for i in range(len(cs))]
        dv = _each(mul, drv, bh)
        dbeta, dg = [], []
        for i, c in enumerate(cs):
            dbeta.append(_rs(drv[i] * v[i]) + _rs(dkb[i] * k[i]) + jnp.zeros((CH, HD), F32))
            t_kd = _rs(dkd[i] * c["kd"])
            dgc = _rs(m[i]) - _rs(m[i].T) + _rs(dqg[i] * c["qg"]) + _rs(drw[i] * c["rw"]) - t_kd
            tail = jnp.sum(t_kd, axis=0, keepdims=True) + c["egl"] * jnp.sum(_rs(s0[i] * ds1[i]), axis=0, keepdims=True)
            dgc = dgc + jnp.where(last, tail, 0.0)
            dg.append(_scan_rows(dgc + jnp.zeros((CH, HD), F32), CH, reverse=True))
        r8 = _iota2((8, HW), 0)
        upd = jnp.zeros((8, HW), F32)
        for j in range(gb):
            sl = slice(j * NH, (j + 1) * NH)
            beta, g, za = gates[j]
            for h in range(NH):
                ds_ref[j, h] = ds_new[j * NH + h]
            dx_ref[j] = jnp.concatenate(dq[sl] + dk[sl] + dv[sl], axis=-1)
            dbeta_j = jnp.where(valid, jnp.concatenate(dbeta[sl], axis=-1), 0.0)
            dg_j = jnp.where(valid, jnp.concatenate(dg[sl], axis=-1), 0.0)
            draw_b = dbeta_j * beta * (1.0 - beta)
            draw_a = dg_j * (-jnp.exp(alog)) * _sig(za)
            dba_ref[j] = _nn(jnp.concatenate([draw_b, draw_a], axis=-1), sm_ref[...]).astype(dba_ref.dtype)
            upd = upd + jnp.where(r8 == 0, jnp.sum(dg_j * g, axis=0, keepdims=True), 0.0) + jnp.where(
                r8 == 1, jnp.sum(draw_a, axis=0, keepdims=True), 0.0)
        acc_ref[...] += upd

    rc = lambda c: nc - 1 - c
    dqkv, dproj3, acc = pl.pallas_call(
        body, grid=(nb // gb, nc),
        in_specs=[pl.BlockSpec((gb, CH, 3 * HW), lambda b, c: (b, rc(c), 0)), pl.BlockSpec((gb, CH, HD), lambda b, c: (b, rc(c), cba)),
                  pl.BlockSpec((HD, 2 * HW), lambda b, c: (0, 0)), pl.BlockSpec((2 * HW, HD), lambda b, c: (0, 0)),
                  pl.BlockSpec((1, HW), lambda b, c: (0, 0)), pl.BlockSpec((1, HW), lambda b, c: (0, 0)),
                  pl.BlockSpec((gb, None, NH, HD, HD), lambda b, c: (b, rc(c), 0, 0, 0)),
                  pl.BlockSpec((gb, None, NH, CH, 3 * HD), lambda b, c: (b, rc(c), 0, 0, 0)),
                  pl.BlockSpec((gb, CH, HW), lambda b, c: (b, rc(c), 0)), pl.BlockSpec(memory_space=pl.ANY)],
        out_specs=[pl.BlockSpec((gb, CH, 3 * HW), lambda b, c: (b, rc(c), 0)), pl.BlockSpec((gb, CH, HD), lambda b, c: (b, rc(c), cba)),
                   pl.BlockSpec((8, HW), lambda b, c: (0, 0))],
        out_shape=[SDS((nb, tp, 3 * HW), F32), SDS((nb, tp, dproj.shape[1]), dproj.dtype), SDS((8, HW), F32)],
        input_output_aliases={9: 1},
        scratch_shapes=[pltpu.VMEM((gb, NH, HD, HD), F32)], compiler_params=_params(2), name="gdn_bwd")(
            qkv.reshape(nb, tp, 3 * HW), proj.reshape(nb, tp, -1), e_mat, s_mat, alog, dtb, states, solved, do.reshape(nb, tp, HW),
            dproj.reshape(nb, tp, -1))
    return dqkv.reshape(n, 3 * HW), dproj3.reshape(dproj.shape), acc


def _hgrn_inputs(zq, zf, lb):
    sg = _sig(zf)
    sgn = _sig(-zf)
    pos = lb > 0.0
    lbp = jnp.where(pos, lb, 0.0)
    fpos = lbp + (1.0 - lbp) * sg
    lf = jnp.where(pos, jnp.log(jnp.where(pos, fpos, 1.0)), _logsig(zf))
    k = (1.0 - lbp) * sgn
    q = _silu(zq) * Q_SCALE
    return q, k, lf, sg, sgn, pos, lbp, fpos


def _hgrn_consts():
    i3, j3 = _iota2((SUB, SUB, HD), 0), _iota2((SUB, SUB, HD), 1)
    return i3 >= j3


def _sum_j(x):
    return jnp.sum(x.reshape(SUB, SUB, HD), axis=1)


def _sum_i(x):
    return jnp.sum(x.reshape(SUB, SUB, HD), axis=0)


def _pairs(a, b):
    return (a[:, None, :] * b[None, :, :]).reshape(SUB * SUB, HD)


def _hgrn_sub(q, k, v, bc, st, consts):
    mask3 = consts
    bl = bc[SUB - 1:SUB, :]
    p3 = jnp.where(mask3, jnp.exp(jnp.where(mask3, bc[:, None, :] - bc[None, :, :], 0.0)), 0.0).reshape(SUB * SUB, HD)
    x = _pairs(q, k) * p3
    srep = _rs(x)
    vt = jnp.broadcast_to(v[None, :, :], (SUB, SUB, HD)).reshape(SUB * SUB, HD)
    eb = jnp.exp(bc)
    qe = q * eb
    o = _hnt(qe, st) + _sum_j(_rr(srep) * _rr(vt))
    ek = jnp.exp(bl - bc)
    kd = k * ek
    ebl = jnp.exp(bl)
    st1 = st * ebl + _htn(v, kd)
    return o, st1, dict(bc=bc, p3=p3, srep=srep, vt=vt, eb=eb, qe=qe, ek=ek, kd=kd, ebl=ebl)


def _hgrn_fwd(proj, lb, lay, nb, nc):
    n = proj.shape[0]
    tp = n // nb
    cbb = lay.c_b // (3 * HW)
    gb = _seqs_per_step(nb)

    def body(z_ref, lb_ref, o_ref, so_ref, s_ref):
        @pl.when(pl.program_id(1) == 0)
        def _():
            s_ref[...] = jnp.zeros_like(s_ref)

        consts = _hgrn_consts()
        for j in range(gb):
            outs = []
            for h in range(NH):
                hs = slice(h * HD, (h + 1) * HD)
                q, k, lf = _hgrn_inputs(z_ref[j, :, hs], z_ref[j, :, HW + h * HD:HW + (h + 1) * HD], lb_ref[:, hs])[:3]
                v = z_ref[j, :, 2 * HW + h * HD:2 * HW + (h + 1) * HD]
                st = s_ref[j, h]
                so_ref[j, h] = st
                bc = _scan_rows(lf, SUB)
                oh = []
                for s in range(CH // SUB):
                    rs = slice(s * SUB, (s + 1) * SUB)
                    o, st, _ = _hgrn_sub(q[rs], k[rs], v[rs], bc[rs], st, consts)
                    oh.append(o)
                s_ref[j, h] = st
                outs.append(jnp.concatenate(oh, axis=0))
            o_ref[j] = jnp.concatenate(outs, axis=-1)

    o, st = pl.pallas_call(
        body, grid=(nb // gb, nc),
        in_specs=[pl.BlockSpec((gb, CH, 3 * HW), lambda b, c: (b, c, cbb)), pl.BlockSpec((1, HW), lambda b, c: (0, 0))],
        out_specs=[pl.BlockSpec((gb, CH, HW), lambda b, c: (b, c, 0)),
                   pl.BlockSpec((gb, None, NH, HD, HD), lambda b, c: (b, c, 0, 0, 0))],
        out_shape=[SDS((nb, tp, HW), F32), SDS((nb, nc, NH, HD, HD), F32)],
        scratch_shapes=[pltpu.VMEM((gb, NH, HD, HD), F32)], compiler_params=_params(2), name="hgrn_fwd")(
            proj.reshape(nb, tp, -1), lb)
    return o.reshape(n, HW), st


def _hgrn_bwd(proj, lb, states, do, dproj, lay, nb, nc):
    n = proj.shape[0]
    tp = n // nb
    cbb = lay.c_b // (3 * HW)
    nsub = CH // SUB
    gb = _seqs_per_step(nb)

    def body(z_ref, lb_ref, st_ref, do_ref, dp_in, dz_ref, acc_ref, ds_ref):
        ci = pl.program_id(1)

        @pl.when(ci == 0)
        def _():
            ds_ref[...] = jnp.zeros_like(ds_ref)

        @pl.when((ci == 0) & (pl.program_id(0) == 0))
        def _():
            acc_ref[...] = jnp.zeros_like(acc_ref)

        upd = jnp.zeros((8, HW), F32)
        for j in range(gb):
            upd = upd + one_seq(j, ci, z_ref, lb_ref, st_ref, do_ref, dz_ref, ds_ref)
        acc_ref[...] += upd

    def one_seq(j, ci, z_ref, lb_ref, st_ref, do_ref, dz_ref, ds_ref):
        consts = _hgrn_consts()
        row = _iota2((CH, 1), 0)
        valid = (row >= PAD) | (ci < nc - 1)
        lastrow = _iota2((SUB, 1), 0) == SUB - 1
        dzq, dzf, dzi, dlbs = [], [], [], []
        for h in range(NH):
            hs = slice(h * HD, (h + 1) * HD)
            zq, zf = z_ref[j, :, hs], z_ref[j, :, HW + h * HD:HW + (h + 1) * HD]
            q, k, lf, sg, sgn, pos, lbp, fpos = _hgrn_inputs(zq, zf, lb_ref[:, hs])
            v = z_ref[j, :, 2 * HW + h * HD:2 * HW + (h + 1) * HD]
            doh = do_ref[j, :, hs]
            sts, fw = [st_ref[j, h]], []
            bc = _scan_rows(lf, SUB)
            for s in range(nsub):
                rs = slice(s * SUB, (s + 1) * SUB)
                _, st1, c = _hgrn_sub(q[rs], k[rs], v[rs], bc[rs], sts[-1], consts)
                sts.append(st1)
                fw.append(c)
            dst = ds_ref[j, h]
            dq_l, dk_l, dv_l, dlf_l = [None] * nsub, [None] * nsub, [None] * nsub, [None] * nsub
            for s in reversed(range(nsub)):
                rs = slice(s * SUB, (s + 1) * SUB)
                c, st = fw[s], sts[s]
                qs, ks, vs, dos = q[rs], k[rs], v[rs], doh[rs]
                dqe = _hnn(dos, st)
                dkd = _hnn(vs, dst)
                dsrep = _rs(_pairs(_rr(dos), _rr(vs)))
                w = dsrep * c["p3"]
                kt = jnp.broadcast_to(ks[None, :, :], (SUB, SUB, HD)).reshape(SUB * SUB, HD)
                qt = jnp.broadcast_to(qs[:, None, :], (SUB, SUB, HD)).reshape(SUB * SUB, HD)
                dq_i = _sum_j(w * kt)
                dk_i = _sum_i(w * qt)
                dot = jnp.broadcast_to(_rr(dos)[:, None, :], (SUB, SUB, HD)).reshape(SUB * SUB, HD)
                dvv = _sum_i(_rr(c["srep"]) * dot) + _hnt(c["kd"], dst)
                t_kd = dkd * c["kd"]
                dbc = dqe * c["qe"] - t_kd + qs * dq_i - ks * dk_i
                tail = jnp.sum(t_kd, axis=0, keepdims=True) + c["ebl"] * jnp.sum(st * dst, axis=0, keepdims=True)
                dbc = dbc + jnp.where(lastrow, tail, 0.0)
                dlf_l[s] = dbc
                dq_l[s] = dq_i + dqe * c["eb"]
                dk_l[s] = dk_i + dkd * c["ek"]
                dv_l[s] = dvv
                dst = _htn(dos, c["qe"]) + dst * c["ebl"]
            ds_ref[j, h] = dst
            dq, dk, dv, dbc = (jnp.concatenate(t, axis=0) for t in (dq_l, dk_l, dv_l, dlf_l))
            dlf = _scan_rows(dbc, SUB, reverse=True)
            dlft = dlf - dk * (1.0 - k)
            dlf_dz = jnp.where(pos, (1.0 - lbp) * sg * sgn / jnp.where(pos, fpos, 1.0), sgn)
            dlf_dlb = jnp.where(pos, sgn / jnp.where(pos, fpos, 1.0), 0.0)
            dzq.append(dq * Q_SCALE * _dsilu(zq))
            dzf.append(dlft * dlf_dz)
            dzi.append(dv)
            dlbs.append(jnp.sum(jnp.where(valid, dlft * dlf_dlb, 0.0), axis=0, keepdims=True))
        dz_ref[j] = jnp.concatenate(dzq + dzf + dzi, axis=-1).astype(dz_ref.dtype)
        return jnp.where(_iota2((8, HW), 0) == 0, jnp.concatenate(dlbs, axis=-1), 0.0)

    rc = lambda c: nc - 1 - c
    dproj3, acc = pl.pallas_call(
        body, grid=(nb // gb, nc),
        in_specs=[pl.BlockSpec((gb, CH, 3 * HW), lambda b, c: (b, rc(c), cbb)), pl.BlockSpec((1, HW), lambda b, c: (0, 0)),
                  pl.BlockSpec((gb, None, NH, HD, HD), lambda b, c: (b, rc(c), 0, 0, 0)),
                  pl.BlockSpec((gb, CH, HW), lambda b, c: (b, rc(c), 0)), pl.BlockSpec(memory_space=pl.ANY)],
        out_specs=[pl.BlockSpec((gb, CH, 3 * HW), lambda b, c: (b, rc(c), cbb)), pl.BlockSpec((8, HW), lambda b, c: (0, 0))],
        out_shape=[SDS((nb, tp, dproj.shape[1]), dproj.dtype), SDS((8, HW), F32)],
        input_output_aliases={4: 0},
        scratch_shapes=[pltpu.VMEM((gb, NH, HD, HD), F32)], compiler_params=_params(2), name="hgrn_bwd")(
            proj.reshape(nb, tp, -1), lb, states, do.reshape(nb, tp, HW), dproj.reshape(nb, tp, -1))
    return dproj3.reshape(dproj.shape), acc


def _gated_norm(o, z, gamma):
    ys, ns, rs = [], [], []
    for h in range(NH):
        hs = slice(h * HD, (h + 1) * HD)
        oh = o[:, hs]
        r = lax.rsqrt(jnp.mean(oh * oh, axis=-1, keepdims=True) + EPS)
        nh = oh * r
        ys.append(nh * gamma * _silu(z[:, hs]))
        ns.append(nh)
        rs.append(r)
    return jnp.concatenate(ys, axis=-1), ns, rs


def _merge_fwd(h, oa, ob, proj, ga, gb, wa, wb, wo, lay):
    n, d = h.shape
    tm = _tile(n, 384)
    wm = lay.wm

    def body(h_ref, oa_ref, ob_ref, p_ref, ga_ref, gb_ref, wa_ref, wb_ref, wo_ref, out_ref):
        ya, _, _ = _gated_norm(oa_ref[...], p_ref[:, 0:HW], ga_ref[...])
        yb, _, _ = _gated_norm(ob_ref[...], p_ref[:, HW:2 * HW], gb_ref[...])
        ya2 = _bnn(ya, wa_ref[...])
        yb2 = _bnn(yb, wb_ref[...])
        mixed = _sig(p_ref[:, 2 * HW:2 * HW + d]) * ya2 + _sig(p_ref[:, 2 * HW + d:2 * HW + 2 * d]) * yb2
        out_ref[...] = h_ref[...] + _bnn(mixed, wo_ref[...])

    full = lambda shape: pl.BlockSpec(shape, lambda i: (0, 0))
    return pl.pallas_call(
        body, grid=(n // tm,),
        in_specs=[pl.BlockSpec((tm, d), lambda i: (i, 0)), pl.BlockSpec((tm, HW), lambda i: (i, 0)),
                  pl.BlockSpec((tm, HW), lambda i: (i, 0)), pl.BlockSpec((tm, wm), lambda i: (i, 0)),
                  full((1, HD)), full((1, HD)), full((HW, d)), full((HW, d)), full((d, d))],
        out_specs=pl.BlockSpec((tm, d), lambda i: (i, 0)), out_shape=SDS((n, d), F32),
        compiler_params=_params(1), name="merge_fwd")(h, oa, ob, proj, ga, gb, wa, wb, wo)


def _gated_norm_bwd(dy, o, z, gamma):
    dos, dzs = [], []
    dgam = jnp.zeros((1, HD), F32)
    for h in range(NH):
        hs = slice(h * HD, (h + 1) * HD)
        oh, zh, dyh = o[:, hs], z[:, hs], dy[:, hs]
        r = lax.rsqrt(jnp.mean(oh * oh, axis=-1, keepdims=True) + EPS)
        nh = oh * r
        dzs.append(dyh * nh * gamma * _dsilu(zh))
        dng = dyh * _silu(zh)
        dgam = dgam + jnp.sum(dng * nh, axis=0, keepdims=True)
        dn = dng * gamma
        dos.append(r * (dn - nh * jnp.mean(dn * nh, axis=-1, keepdims=True)))
    return jnp.concatenate(dos, axis=-1), jnp.concatenate(dzs, axis=-1), dgam


def _merge_bwd(dhn, oa, ob, proj, ga, gb, wa, wb, wo, lay, tp):
    n, d = dhn.shape
    tm = _tile(n, 256)
    wm = lay.wm

    def body(dh_ref, oa_ref, ob_ref, p_ref, ga_ref, gb_ref, wa_ref, wb_ref, wo_ref,
             dp_ref, doa_ref, dob_ref, dwa_ref, dwb_ref, dwo_ref, dga_ref, dgb_ref):
        i = pl.program_id(0)

        @pl.when(i == 0)
        def _():
            for r in (dwa_ref, dwb_ref, dwo_ref, dga_ref, dgb_ref):
                r[...] = jnp.zeros_like(r)

        dh = jnp.where(_row_valid(tm, tp, i * tm), dh_ref[...], 0.0)
        oa, ob = oa_ref[...], ob_ref[...]
        za, zb = p_ref[:, 0:HW], p_ref[:, HW:2 * HW]
        gta, gtb = p_ref[:, 2 * HW:2 * HW + d], p_ref[:, 2 * HW + d:2 * HW + 2 * d]
        ya, _, _ = _gated_norm(oa, za, ga_ref[...])
        yb, _, _ = _gated_norm(ob, zb, gb_ref[...])
        ya2 = _bnn(ya, wa_ref[...])
        yb2 = _bnn(yb, wb_ref[...])
        sa, sb = _sig(gta), _sig(gtb)
        mixed = sa * ya2 + sb * yb2
        dmixed = _bnt(dh, wo_ref[...])
        dwo_ref[...] += _btn(mixed, dh)
        dya2 = dmixed * sa
        dyb2 = dmixed * sb
        dwa_ref[...] += _btn(ya, dya2)
        dwb_ref[...] += _btn(yb, dyb2)
        doa, dza, dga = _gated_norm_bwd(_bnt(dya2, wa_ref[...]), oa, za, ga_ref[...])
        dob, dzb, dgb = _gated_norm_bwd(_bnt(dyb2, wb_ref[...]), ob, zb, gb_ref[...])
        dga_ref[...] += dga
        dgb_ref[...] += dgb
        doa_ref[...] = doa
        dob_ref[...] = dob
        dt = dp_ref.dtype
        dp_ref[:, 0:HW] = dza.astype(dt)
        dp_ref[:, HW:2 * HW] = dzb.astype(dt)
        dp_ref[:, 2 * HW:2 * HW + d] = (dmixed * ya2 * sa * (1.0 - sa)).astype(dt)
        dp_ref[:, 2 * HW + d:2 * HW + 2 * d] = (dmixed * yb2 * sb * (1.0 - sb)).astype(dt)

    full = lambda shape: pl.BlockSpec(shape, lambda i: (0, 0))
    rows = lambda w: pl.BlockSpec((tm, w), lambda i: (i, 0))
    return pl.pallas_call(
        body, grid=(n // tm,),
        in_specs=[rows(d), rows(HW), rows(HW), rows(wm), full((1, HD)), full((1, HD)), full((HW, d)), full((HW, d)), full((d, d))],
        out_specs=[rows(wm), rows(HW), rows(HW), full((HW, d)), full((HW, d)), full((d, d)), full((1, HD)), full((1, HD))],
        out_shape=[SDS((n, lay.pw), BF16), SDS((n, HW), F32), SDS((n, HW), F32), SDS((HW, d), F32), SDS((HW, d), F32),
                   SDS((d, d), F32), SDS((1, HD), F32), SDS((1, HD), F32)],
        compiler_params=_params(1), name="merge_bwd")(dhn, oa, ob, proj, ga, gb, wa, wb, wo)


def _loss_head(h, target, fw, nb, tp):
    n, d = h.shape
    tr = _tile(tp, 768)
    nr = tp // tr

    def body(h_ref, t_ref, fw_ref, lp_ref, dh_ref, dfw_ref):
        b, i = pl.program_id(0), pl.program_id(1)

        @pl.when((b == 0) & (i == 0))
        def _():
            dfw_ref[...] = jnp.zeros_like(dfw_ref)

        x = h_ref[...]
        r = lax.rsqrt(jnp.mean(x * x, axis=-1, keepdims=True) + EPS)
        xh = x * r
        live = i * tr + _iota2((tr, 1), 0) >= CH
        err = jnp.where(live, xh * fw_ref[...] - t_ref[...], 0.0)
        lp_ref[...] = jnp.zeros_like(lp_ref) + 0.5 * jnp.sum(_rs(err * err), axis=0, keepdims=True) / d
        dy = err / d
        dfw_ref[...] += jnp.sum(dy * xh, axis=0, keepdims=True)
        dxh = dy * fw_ref[...]
        dh_ref[...] = r * (dxh - xh * jnp.mean(dxh * xh, axis=-1, keepdims=True))

    rows = pl.BlockSpec((tr, d), lambda b, i: (b * nr + i, 0))
    return pl.pallas_call(
        body, grid=(nb, nr), in_specs=[rows, rows, pl.BlockSpec((1, d), lambda b, i: (0, 0))],
        out_specs=[pl.BlockSpec((8, HD), lambda b, i: (b * nr + i, 0)), rows, pl.BlockSpec((1, d), lambda b, i: (0, 0))],
        out_shape=[SDS((nb * nr * 8, HD), F32), SDS((n, d), F32), SDS((1, d), F32)],
        compiler_params=_params(2), name="loss_head")(h, target, fw)


def _lb_fwd(lb):
    def body(x_ref, o_ref):
        x = x_ref[...]
        mx = jnp.max(x, axis=0, keepdims=True)
        e = jnp.exp(x - mx)
        sm = e / jnp.sum(e, axis=0, keepdims=True)
        run = jnp.zeros((1, HW), F32)
        for l in range(DEPTH):
            run = run + sm[l:l + 1, :]
            o_ref[l:l + 1, :] = run - sm[0:1, :]

    return pl.pallas_call(body, out_shape=SDS(lb.shape, F32), name="lb_fwd")(lb)


def _lb_bwd(lb, dlb_all):
    def body(x_ref, d_ref, o_ref):
        x = x_ref[...]
        dl = d_ref[...]
        mx = jnp.max(x, axis=0, keepdims=True)
        e = jnp.exp(x - mx)
        sm = e / jnp.sum(e, axis=0, keepdims=True)
        tot = jnp.sum(dl, axis=0, keepdims=True)
        dsm = []
        run = tot
        for l in range(DEPTH):
            dsm.append(run - (tot if l == 0 else 0.0))
            run = run - dl[l:l + 1, :]
        inner = sum(sm[l:l + 1, :] * dsm[l] for l in range(DEPTH))
        for l in range(DEPTH):
            o_ref[l:l + 1, :] = sm[l:l + 1, :] * (dsm[l] - inner)

    return pl.pallas_call(body, out_shape=SDS(lb.shape, F32), name="lb_bwd")(lb, dlb_all)


def _adamw(g, w, m, v):
    r, c = g.shape
    tr = _tile(r, 264)
    c1 = 1.0 / (1.0 - ADAM_B1 ** ADAM_STEP)
    c2 = 1.0 / (1.0 - ADAM_B2 ** ADAM_STEP)

    def body(g_ref, w_ref, m_ref, v_ref, d_ref, mo_ref, vo_ref):
        gg = g_ref[...]
        mn = ADAM_B1 * m_ref[...] + (1.0 - ADAM_B1) * gg
        vn = ADAM_B2 * v_ref[...] + (1.0 - ADAM_B2) * gg * gg
        d_ref[...] = -ADAM_LR * ((mn * c1) / (jnp.sqrt(vn * c2) + ADAM_EPS) + ADAM_WD * w_ref[...])
        mo_ref[...] = mn
        vo_ref[...] = vn

    spec = pl.BlockSpec((tr, c), lambda i: (i, 0))
    return pl.pallas_call(body, grid=(r // tr,), in_specs=[spec] * 4, out_specs=[spec] * 3, out_shape=[SDS(g.shape, F32)] * 3,
                          compiler_params=_params(1), name="adamw")(g, w, m, v)


def _tile16(n, target):
    return _tile(n // 2, target // 2) * 2 if n % 16 == 0 else _tile(n, target)


def _add_cores(g, got, core):
    k, r, c = got.shape
    tr = _tile16(r, 264)

    def body(c_ref, a_ref, b_ref, o_ref):
        o_ref[...] = (a_ref[...] + b_ref[...].astype(F32)).astype(o_ref.dtype)

    spec = pl.BlockSpec((None, tr, c), lambda s, i, cr: (s, i, 0))
    return pl.pallas_call(
        body, grid_spec=pltpu.PrefetchScalarGridSpec(
            num_scalar_prefetch=1, grid=(k, r // tr),
            in_specs=[pl.BlockSpec((None, None, tr, c), lambda s, i, cr: (cr[0], s, i, 0)), spec], out_specs=spec),
        out_shape=SDS(got.shape, got.dtype), compiler_params=_params(2), name="add_cores")(core, g, got)


def _sum_chips(parts, own, chip, core):
    k, r, c = parts.shape
    tr = _tile16(r, 264)

    def body(chip_ref, core_ref, *refs):
        part_refs, own_ref, o_ref = refs[:k], refs[k], refs[k + 1]
        mine = own_ref[...].astype(F32)
        acc = None
        for s in range(k):
            term = jnp.where(chip_ref[0] == s, mine, part_refs[s][...].astype(F32))
            acc = term if acc is None else acc + term
        o_ref[...] = acc

    def other(s):
        return pl.BlockSpec((None, tr, c), lambda i, ch, co: (jnp.where(ch[0] == s, (s + 1) % k, s), i, 0))

    return pl.pallas_call(
        body, grid_spec=pltpu.PrefetchScalarGridSpec(
            num_scalar_prefetch=2, grid=(r // tr,),
            in_specs=[other(s) for s in range(k)] + [pl.BlockSpec((None, tr, c), lambda i, ch, co: (ch[0], i, 0))],
            out_specs=pl.BlockSpec((None, tr, c), lambda i, ch, co: (co[0], i, 0))),
        out_shape=SDS((2, r, c), F32), compiler_params=_params(1), name="sum_chips")(chip, core, *([parts] * k), own)


def _meta_grad(dh, nb, nc):
    d = dh.shape[1]

    def body(x_ref, o_ref):
        @pl.when(pl.program_id(0) == 0)
        def _():
            o_ref[...] = jnp.zeros_like(o_ref)

        o_ref[...] += x_ref[PAD:CH, :]

    return pl.pallas_call(body, grid=(nb,), in_specs=[pl.BlockSpec((CH, d), lambda b: (b * nc, 0))],
                          out_specs=pl.BlockSpec((N_META, d), lambda b: (0, 0)), out_shape=SDS((N_META, d), F32),
                          compiler_params=_params(1), name="meta_grad")(dh)


ANY = pl.BlockSpec(memory_space=pl.ANY)


def _place():
    x, y, c = lax.axis_index("x"), lax.axis_index("y"), lax.axis_index("c")
    chips = [(1 - x, y), (x, 1 - y), (1 - x, 1 - y)]
    return x, y, c, chips


def _remote(src, dst, send_sems, recv_sems, k, to):
    return pltpu.make_async_remote_copy(src_ref=src, dst_ref=dst, send_sem=send_sems.at[k], recv_sem=recv_sems.at[k],
                                        device_id=to, device_id_type=MESH)


def _gather_weights(pbs, ps):
    nt = len(pbs)

    def body(*refs):
        pb_refs, ps_ref, gb_refs, gs_ref = refs[:nt], refs[nt], refs[nt + 1:2 * nt + 1], refs[2 * nt + 1]
        send_sems, recv_sems, local_sems = refs[2 * nt + 2:]
        x, y, c, chips = _place()
        s = 2 * x + y
        sib = (x, y, 1 - c)
        l1 = pltpu.make_async_copy(ps_ref, gs_ref.at[s], local_sems.at[0])
        l1.start()
        sends = []
        for k, (px, py) in enumerate(chips):
            for t in range(nt):
                sends.append(_remote(pb_refs[t].at[c], gb_refs[t].at[s, c], send_sems, recv_sems, 6 * t + k, (px, py, c)))
            sends.append(_remote(ps_ref, gs_ref.at[s], send_sems, recv_sems, 6 * nt + k, (px, py, c)))
        for cp in sends:
            cp.start()
        for k, (px, py) in enumerate(chips):
            sk = 2 * px + py
            for t in range(nt):
                _remote(pb_refs[t].at[c], gb_refs[t].at[sk, c], send_sems, recv_sems, 6 * t + k, sib).wait_recv()
                fwd = _remote(gb_refs[t].at[sk, c], gb_refs[t].at[sk, c], send_sems, recv_sems, 6 * t + 3 + k, sib)
                fwd.start()
                sends.append(fwd)
        for k, (px, py) in enumerate(chips):
            sk = 2 * px + py
            for t in range(nt):
                _remote(pb_refs[t].at[c], gb_refs[t].at[sk, 1 - c], send_sems, recv_sems, 6 * t + 3 + k, sib).wait_recv()
            _remote(ps_ref, gs_ref.at[sk], send_sems, recv_sems, 6 * nt + k, sib).wait_recv()
        for cp in sends:
            cp.wait_send()
        l1.wait()

    nsem = 6 * nt + 3
    out = pl.pallas_call(
        body, in_specs=[ANY] * (nt + 1), out_specs=[ANY] * (nt + 1),
        out_shape=[SDS((4,) + pb.shape, pb.dtype) for pb in pbs] + [SDS((4,) + ps.shape, ps.dtype)],
        scratch_shapes=[pltpu.SemaphoreType.DMA((nsem,)), pltpu.SemaphoreType.DMA((nsem,)), pltpu.SemaphoreType.DMA((1,))],
        name="gather_weights")(*pbs, ps)
    return out[:nt], out[nt]


def _contain(wpad, shift):
    r, cw = wpad.shape
    tr = _tile16(r, 256)

    def body(n_ref, x_ref, o_ref):
        o_ref[...] = pltpu.roll(x_ref[...], n_ref[0], axis=1).astype(o_ref.dtype)

    spec = pl.BlockSpec((tr, cw), lambda i, n: (i, 0))
    return pl.pallas_call(
        body, grid_spec=pltpu.PrefetchScalarGridSpec(num_scalar_prefetch=1, grid=(r // tr,), in_specs=[spec], out_specs=spec),
        out_shape=SDS((r, cw), BF16), compiler_params=_params(1), name="contain")(shift, wpad)


def _place_own(gb, pb, chip):
    _, _, r, c = gb.shape
    tr = _tile16(r, 1100)

    def body(s_ref, p_ref, g_in, o_ref):
        o_ref[...] = p_ref[...]

    return pl.pallas_call(
        body, grid_spec=pltpu.PrefetchScalarGridSpec(
            num_scalar_prefetch=1, grid=(2, r // tr),
            in_specs=[pl.BlockSpec((None, tr, c), lambda h, i, s: (h, i, 0)), ANY],
            out_specs=pl.BlockSpec((None, None, tr, c), lambda h, i, s: (s[0], h, i, 0))),
        out_shape=SDS(gb.shape, gb.dtype), input_output_aliases={2: 0}, compiler_params=_params(2),
        name="place_own")(chip, pb, gb)


def _sem_scratch(n_remote, n_local):
    return [pltpu.SemaphoreType.DMA((n_remote,)), pltpu.SemaphoreType.DMA((n_remote,)), pltpu.SemaphoreType.DMA((n_local,))]


def _swap_halves(sends):
    nt = len(sends)

    def body(*refs):
        s_refs, got_refs = refs[:nt], refs[nt:2 * nt]
        send_sems, recv_sems = refs[2 * nt:]
        x, y, c, _ = _place()
        sib = (x, y, 1 - c)
        remote = [_remote(s_refs[t].at[1 - c, s], got_refs[t].at[s], send_sems, recv_sems, 4 * t + s, sib)
                  for t in range(nt) for s in range(4)]
        for cp in remote:
            cp.start()
        for cp in remote:
            cp.wait()

    return pl.pallas_call(
        body, in_specs=[ANY] * nt, out_specs=[ANY] * nt, out_shape=[SDS(g.shape[1:], g.dtype) for g in sends],
        scratch_shapes=[pltpu.SemaphoreType.DMA((4 * nt,)), pltpu.SemaphoreType.DMA((4 * nt,))], name="swap_halves")(*sends)


def _scatter_chip_sums(parts):
    nt = len(parts)

    def body(*refs):
        a_refs, r_refs = refs[:nt], refs[nt:2 * nt]
        send_sems, recv_sems = refs[2 * nt:]
        x, y, c, chips = _place()
        s = 2 * x + y
        sends = [_remote(a_refs[t].at[2 * px + py], r_refs[t].at[s], send_sems, recv_sems, 3 * t + k, (px, py, c))
                 for t in range(nt) for k, (px, py) in enumerate(chips)]
        for cp in sends:
            cp.start()
        for t in range(nt):
            for k, (px, py) in enumerate(chips):
                _remote(a_refs[t].at[s], r_refs[t].at[2 * px + py], send_sems, recv_sems, 3 * t + k, (px, py, c)).wait_recv()
        for cp in sends:
            cp.wait_send()

    return pl.pallas_call(
        body, in_specs=[ANY] * nt, out_specs=[ANY] * nt, out_shape=[SDS(a.shape, a.dtype) for a in parts],
        scratch_shapes=[pltpu.SemaphoreType.DMA((3 * nt,)), pltpu.SemaphoreType.DMA((3 * nt,))],
        name="scatter_chip_sums")(*parts)


def _join_halves(fs):
    nt = len(fs)

    def body(*refs):
        f_refs = refs[nt:2 * nt]
        send_sems, recv_sems = refs[2 * nt:]
        x, y, c, _ = _place()
        sib = (x, y, 1 - c)
        sends = [_remote(f_refs[t].at[c], f_refs[t].at[c], send_sems, recv_sems, t, sib) for t in range(nt)]
        for cp in sends:
            cp.start()
        for t in range(nt):
            _remote(f_refs[t].at[c], f_refs[t].at[1 - c], send_sems, recv_sems, t, sib).wait_recv()
        for cp in sends:
            cp.wait_send()

    return pl.pallas_call(
        body, in_specs=[ANY] * nt, out_specs=[ANY] * nt, out_shape=[SDS(f.shape, f.dtype) for f in fs],
        input_output_aliases={t: t for t in range(nt)},
        scratch_shapes=[pltpu.SemaphoreType.DMA((nt,)), pltpu.SemaphoreType.DMA((nt,))], name="join_halves")(*fs)


def _uncontain(cont, n_head, width):
    r, cw = cont.shape
    tr = _tile(r, 256)

    def body(n_ref, x_ref, o_ref):
        o_ref[...] = pltpu.roll(x_ref[...], n_ref[0], axis=1)[:, :width]

    return pl.pallas_call(
        body, grid_spec=pltpu.PrefetchScalarGridSpec(
            num_scalar_prefetch=1, grid=(r // tr,), in_specs=[pl.BlockSpec((tr, cw), lambda i, n: (i, 0))],
            out_specs=pl.BlockSpec((tr, width), lambda i, n: (i, 0))),
        out_shape=SDS((r, width), F32), compiler_params=_params(1), name="uncontain")(n_head, cont)


WEIGHTS = ("meta_tokens", "norm_w", "w_in", "conv_w", "a_log", "dt_bias", "gnorm_a", "gnorm_b", "hgrn_lower_bounds",
           "w_branch_a", "w_branch_b", "w_out", "final_norm_w")
SHARD_AXIS = {"meta_tokens": 1, "w_in": 2, "conv_w": 2, "w_branch_a": 2, "w_branch_b": 2, "w_out": 1}
FLAT_C = 1024


def _flat(parts, rows, cols=FLAT_C):
    v = jnp.concatenate([p.reshape(-1) for p in parts])
    return jnp.pad(v, (0, rows * cols - v.shape[0])).reshape(rows, cols)


def _local_step(x, target, w, lay):
    nb, seq, d = x.shape
    tp = CH + seq
    nc = tp // CH
    n = nb * tp
    e_mat, s_mat = _gate_consts()
    lb_all = _lb_fwd(w["hgrn_lower_bounds"])
    h = jnp.concatenate([jnp.zeros((nb, PAD, d), F32), jnp.broadcast_to(w["meta_tokens"][None], (nb, N_META, d)), x],
                        axis=1).reshape(n, d)
    rep = lambda a: jnp.repeat(a, HD)[None, :]
    saved = []
    for l in range(DEPTH):
        nw = w["norm_w"][l][None, :]
        proj, xn = _norm_proj_fwd(h, nw, w["w_in"][l])
        qkv = _gdn_prep_fwd(proj, w["conv_w"][l], lay, nb, tp)
        alog, dtb = rep(w["a_log"][l]), rep(w["dt_bias"][l])
        oa, sa, sva = _gdn_fwd(qkv, proj, e_mat, alog, dtb, lay, nb, nc)
        lbl = lb_all[l][None, :]
        ob, sb = _hgrn_fwd(proj, lbl, lay, nb, nc)
        ga, gb = w["gnorm_a"][l][None, :], w["gnorm_b"][l][None, :]
        hn = _merge_fwd(h, oa, ob, proj, ga, gb, w["w_branch_a"][l], w["w_branch_b"][l], w["w_out"][l], lay)
        saved.append((h, nw, proj, qkv, alog, dtb, oa, sa, lbl, ob, sb, ga, gb, xn, sva))
        h = hn
    target_p = jnp.pad(target, ((0, 0), (CH, 0), (0, 0))).reshape(n, d)
    lp, dh, dfw = _loss_head(h, target_p, w["final_norm_w"][None, :], nb, tp)
    loss = jnp.sum(lp[::8, 0])
    g = {n_: [None] * DEPTH for n_ in WEIGHTS}
    dlb_all = [None] * DEPTH
    for l in reversed(range(DEPTH)):
        h, nw, proj, qkv, alog, dtb, oa, sa, lbl, ob, sb, ga, gb, xn, sva = saved[l]
        dproj, doa, dob, dwa, dwb, dwo, dga, dgb = _merge_bwd(dh, oa, ob, proj, ga, gb, w["w_branch_a"][l],
                                                             w["w_branch_b"][l], w["w_out"][l], lay, tp)
        dproj, acc_b = _hgrn_bwd(proj, lbl, sb, dob, dproj, lay, nb, nc)
        dqkv, dproj, acc_a = _gdn_bwd(qkv, proj, e_mat, s_mat, alog, dtb, sa, sva, doa, dproj, lay, nb, nc)
        dproj, dconv = _gdn_prep_bwd(proj, w["conv_w"][l], dqkv, dproj, lay, nb, tp)
        dh, dnw = _proj_bwd_dx(dproj, w["w_in"][l], h, nw, dh, tp)
        g["w_in"][l] = _proj_bwd_dw(dproj, xn, tp)
        g["norm_w"][l] = dnw[0]
        g["conv_w"][l] = dconv
        g["a_log"][l] = acc_a[0, ::HD]
        g["dt_bias"][l] = acc_a[1, ::HD]
        g["gnorm_a"][l], g["gnorm_b"][l] = dga[0], dgb[0]
        g["w_branch_a"][l], g["w_branch_b"][l], g["w_out"][l] = dwa, dwb, dwo
        dlb_all[l] = acc_b[0]
    grads = {n_: jnp.stack(v) for n_, v in g.items() if v[0] is not None}
    grads["hgrn_lower_bounds"] = _lb_bwd(w["hgrn_lower_bounds"], jnp.stack(dlb_all))
    grads["final_norm_w"] = dfw[0]
    grads["meta_tokens"] = _meta_grad(dh, nb, nc)
    grad_x = dh.reshape(nb, tp, d)[:, CH:, :]
    return loss, grad_x, grads


def kernel(x, meta_tokens, norm_w, w_in, conv_w, a_log, dt_bias, gnorm_a, gnorm_b, hgrn_lower_bounds, w_branch_a, w_branch_b, w_out, final_norm_w, loss_target, m_meta_tokens, m_norm_w, m_w_in, m_conv_w, m_a_log, m_dt_bias, m_gnorm_a, m_gnorm_b, m_hgrn_lower_bounds, m_w_branch_a, m_w_branch_b, m_w_out, m_final_norm_w, v_meta_tokens, v_norm_w, v_w_in, v_conv_w, v_a_log, v_dt_bias, v_gnorm_a, v_gnorm_b, v_hgrn_lower_bounds, v_w_branch_a, v_w_branch_b, v_w_out, v_final_norm_w):
    wl = dict(meta_tokens=meta_tokens, norm_w=norm_w, w_in=w_in, conv_w=conv_w, a_log=a_log, dt_bias=dt_bias, gnorm_a=gnorm_a,
              gnorm_b=gnorm_b, hgrn_lower_bounds=hgrn_lower_bounds, w_branch_a=w_branch_a, w_branch_b=w_branch_b, w_out=w_out,
              final_norm_w=final_norm_w)
    ml = dict(zip(WEIGHTS, (m_meta_tokens, m_norm_w, m_w_in, m_conv_w, m_a_log, m_dt_bias, m_gnorm_a, m_gnorm_b,
                            m_hgrn_lower_bounds, m_w_branch_a, m_w_branch_b, m_w_out, m_final_norm_w)))
    vl = dict(zip(WEIGHTS, (v_meta_tokens, v_norm_w, v_w_in, v_conv_w, v_a_log, v_dt_bias, v_gnorm_a, v_gnorm_b,
                            v_hgrn_lower_bounds, v_w_branch_a, v_w_branch_b, v_w_out, v_final_norm_w)))
    d = x.shape[2]
    lay = _Layout(d)
    nchip = 4

    big = ("w_in", "w_branch_a", "w_branch_b", "w_out")
    small = ("conv_w", "meta_tokens")
    table, heads, cw = lay.pieces(nchip)
    sw = wl["w_in"].shape[2]
    chip_id = (2 * lax.axis_index("x") + lax.axis_index("y")).astype(jnp.int32)
    n_head = sum(jnp.where(chip_id == s, heads[s], 0) for s in range(nchip)).astype(jnp.int32)
    w_pad = jnp.pad(wl["w_in"], ((0, 0), (0, 0), (0, cw - sw))).reshape(DEPTH * d, cw)
    shift = jnp.where(n_head == 0, 0, cw - n_head).astype(jnp.int32).reshape(1)
    pbs = [_contain(w_pad, shift).reshape(DEPTH, d, cw)] + [wl[n].astype(BF16) for n in big[1:]]
    nsmall = sum(int(np.prod(wl[n].shape)) for n in small)
    rs = -(-nsmall // (HD * 8)) * 8
    ps = jnp.pad(jnp.concatenate([wl[n].reshape(-1) for n in small]), (0, rs * HD - nsmall)).reshape(rs, HD)
    gbig, gsmall = _gather_weights(pbs, ps)
    gbig = [_place_own(g, p, chip_id.reshape(1)) for g, p in zip(gbig, pbs)]
    gsmall = gsmall.reshape(nchip, -1)

    wf = dict(wl)
    wf["w_in"] = lay.from_containers([gbig[0][s] for s in range(nchip)])
    for i, n in enumerate(big[1:], start=1):
        wf[n] = jnp.concatenate([gbig[i][s] for s in range(nchip)], axis=SHARD_AXIS[n])
    o = 0
    for n in small:
        sz = int(np.prod(wl[n].shape))
        a = gsmall[:, o:o + sz].reshape((nchip,) + wl[n].shape)
        wf[n] = jnp.concatenate([a[s] for s in range(nchip)], axis=SHARD_AXIS[n])
        o += sz

    loss_part, grad_x, gfull = _local_step(x, loss_target, wf, lay)
    loss = lax.psum(loss_part, ("x", "y", "c"))

    sw = wl["w_in"].shape[2]
    conts, heads = lay.containers(gfull["w_in"], nchip)
    dd = wl["w_branch_a"].shape[2]
    rows_o = wl["w_out"].shape[1]
    by_dest = lambda g, n: [lax.slice_in_dim(g, s * wl[n].shape[SHARD_AXIS[n]], (s + 1) * wl[n].shape[SHARD_AXIS[n]],
                                            axis=SHARD_AXIS[n]) if n in SHARD_AXIS else g for s in range(nchip)]
    small_names = tuple(n for n in WEIGHTS if n not in big)
    nsm = sum(int(np.prod(wl[n].shape)) for n in small_names)
    rsm = -(-nsm // (2 * HD * 8)) * 8
    pack_small = lambda parts: _flat(parts, 2 * rsm, HD).reshape(2, rsm, HD)
    small_by_dest = [by_dest(gfull[n], n) for n in small_names]
    gs = [jnp.stack(conts, axis=1),
          jnp.stack(by_dest(gfull["w_branch_a"], "w_branch_a"), axis=1),
          jnp.stack(by_dest(gfull["w_branch_b"], "w_branch_b"), axis=1),
          gfull["w_out"].reshape(DEPTH, nchip, rows_o, d),
          jnp.stack([pack_small([p[s] for p in small_by_dest]) for s in range(nchip)], axis=1)]
    gs = [g.reshape((2, nchip, -1, g.shape[-1])) for g in gs]
    my_chip = (2 * lax.axis_index("x") + lax.axis_index("y")).astype(jnp.int32)
    my_core = lax.axis_index("c").astype(jnp.int32)
    got = _swap_halves([g.astype(BF16) for g in gs[:4]] + gs[4:])
    chip_sums = [_add_cores(g, b, my_core.reshape(1)) for g, b in zip(gs, got)]
    by_chip = _scatter_chip_sums(chip_sums)
    full = _join_halves([_sum_chips(p, a, my_chip.reshape(1), my_core.reshape(1)) for p, a in zip(by_chip, chip_sums)])
    n_head = sum(jnp.where(my_chip == s, heads[s], 0) for s in range(nchip)).astype(jnp.int32).reshape(1)
    g_w_in = _uncontain(full[0].reshape(DEPTH * d, -1), n_head, sw)
    g2 = {"w_in": g_w_in, "w_branch_a": full[1].reshape(-1, dd), "w_branch_b": full[2].reshape(-1, dd),
          "w_out": full[3].reshape(-1, d), "small": full[4].reshape(2 * rsm, HD)}

    def two_d(src, n):
        if n == "small":
            return _flat([src[k] for k in small_names], 2 * rsm, HD)
        return src[n].reshape(g2[n].shape)

    outs = {}
    for n in big + ("small",):
        delta, mnew, vnew = _adamw(g2[n], two_d(wl, n), two_d(ml, n), two_d(vl, n))
        outs[n] = (g2[n], delta, mnew, vnew)
    res = [{}, {}, {}, {}]
    for i in range(4):
        for n in big:
            res[i][n] = outs[n][i].reshape(wl[n].shape)
        v, o = outs["small"][i].reshape(-1), 0
        for n in small_names:
            sz = int(np.prod(wl[n].shape))
            res[i][n] = v[o:o + sz].reshape(wl[n].shape)
            o += sz
    return (loss, grad_x, *[res[0][n] for n in WEIGHTS], *[res[1][n] for n in WEIGHTS], *[res[2][n] for n in WEIGHTS],
            *[res[3][n] for n in WEIGHTS])
```

```python
import functools

import numpy as np
import jax
import jax.numpy as jnp
from jax import lax
from jax.experimental import pallas as pl
from jax.experimental.pallas import tpu as pltpu

F32 = jnp.float32
BF16 = jnp.bfloat16
HI = lax.Precision.HIGHEST
SDS = jax.ShapeDtypeStruct

NH = 4
HD = 128
HW = NH * HD
N_META = 16
CH = 64
SUB = 16
PAD = CH - N_META
EPS = 1e-6
Q_SCALE = HD ** -0.5
DEPTH = 2
CONV_K = 4
PREP_HEADS = 2
VMEM_LIMIT = 56 * 1024 * 1024
ADAM_LR, ADAM_B1, ADAM_B2, ADAM_EPS, ADAM_WD, ADAM_STEP = 0.001, 0.9, 0.999, 1e-08, 0.01, 10
MESH = pl.DeviceIdType.MESH


def _nn(a, b):
    return jnp.dot(a, b, precision=HI, preferred_element_type=F32)


def _nt(a, b):
    return lax.dot_general(a, b, (((1,), (1,)), ((), ())), precision=HI, preferred_element_type=F32)


def _tn(a, b):
    return _nn(a.T, b)


def _scan_rows(x, group, reverse=False):
    n = x.shape[0]
    pos = lax.bitwise_and(_iota2(x.shape, 0), group - 1)
    s = 1
    while s < group:
        if reverse:
            x = x + jnp.where(pos < group - s, pltpu.roll(x, n - s, axis=0), 0.0)
        else:
            x = x + jnp.where(pos >= s, pltpu.roll(x, s, axis=0), 0.0)
        s *= 2
    return x


def _bnn(a, b):
    return jnp.dot(a.astype(BF16), b.astype(BF16), preferred_element_type=F32)


def _bnt(a, b):
    return lax.dot_general(a.astype(BF16), b.astype(BF16), (((1,), (1,)), ((), ())), preferred_element_type=F32)


def _btn(a, b):
    return lax.dot_general(a.astype(BF16), b.astype(BF16), (((0,), (0,)), ((), ())), preferred_element_type=F32)


def _hi_lo(x):
    hi = x.astype(jnp.bfloat16)
    return hi, (x - hi.astype(F32)).astype(jnp.bfloat16)


def _dot3(dims):
    def f(a, b):
        ah, al = _hi_lo(a)
        bh, bl = _hi_lo(b)
        d = lambda p, q: lax.dot_general(p, q, (dims, ((), ())), preferred_element_type=F32)
        return d(ah, bh) + (d(ah, bl) + d(al, bh))
    return f


_rnn, _rnt, _rtn = _dot3(((1,), (0,))), _dot3(((1,), (1,))), _dot3(((0,), (0,)))
_enn, _ent, _etn = _bnn, _bnt, _btn
_hnn, _hnt, _htn = _bnn, _bnt, _btn


def _rr(x):
    return x


def _sig(x):
    return jax.nn.sigmoid(x)


def _silu(x):
    return x * _sig(x)


def _dsilu(x):
    s = _sig(x)
    return s * (1.0 + x * (1.0 - s))


def _softplus(x):
    return jnp.maximum(x, 0.0) + jnp.log(1.0 + jnp.exp(-jnp.abs(x)))


def _logsig(x):
    return jnp.minimum(x, 0.0) - jnp.log(1.0 + jnp.exp(-jnp.abs(x)))


def _rs(x):
    return jnp.sum(x, axis=-1, keepdims=True)


def _params(n_axes):
    return pltpu.CompilerParams(dimension_semantics=("arbitrary",) * n_axes, vmem_limit_bytes=VMEM_LIMIT)


def _tile(n, target, mult=8):
    best = mult
    for t in range(mult, target + 1, mult):
        if n % t == 0:
            best = t
    assert n % best == 0, (n, mult)
    return best


def _ctile(pw, most=7):
    return HD * max(k for k in range(1, most + 1) if (pw // HD) % k == 0)


def _iota2(shape, axis):
    return lax.broadcasted_iota(jnp.int32, shape, axis)


class _Layout:
    def __init__(self, d):
        self.d = d
        self.wm = 2 * HW + 2 * d
        self.c_qkv = self.wm
        self.c_b = self.wm + 3 * HW
        self.c_ba = self.wm + 6 * HW
        self.pw = self.c_ba + HD
        assert self.c_b % (3 * HW) == 0
        o = 0
        segs = {}
        for name, w in (("a_q", HW), ("a_k", HW), ("a_v", HW), ("ba", 2 * NH), ("a_z", HW), ("b_q", HW), ("b_f", HW),
                        ("b_i", HW), ("b_g", HW), ("gate_a", d), ("gate_b", d)):
            segs[name] = (o, o + w)
            o += w
        self.segs = segs
        self.width = o
        self.order = ("a_z", "b_g", "gate_a", "gate_b", "a_q", "a_k", "a_v", "b_q", "b_f", "b_i", "ba")

    def to_kernel(self, w):
        parts = [w[..., self.segs[n][0]:self.segs[n][1]] for n in self.order]
        parts.append(jnp.zeros(w.shape[:-1] + (HD - 2 * NH,), w.dtype))
        return jnp.concatenate(parts, axis=-1)

    def containers(self, g, nchip):
        table, heads, cw = self.pieces(nchip)
        out = []
        for s in range(nchip):
            parts, at = [], 0
            for kcol, w, ccol in sorted(table[s], key=lambda p: p[2]):
                if ccol > at:
                    parts.append(jnp.zeros(g.shape[:-1] + (ccol - at,), g.dtype))
                parts.append(g[..., kcol:kcol + w])
                at = ccol + w
            if at < cw:
                parts.append(jnp.zeros(g.shape[:-1] + (cw - at,), g.dtype))
            out.append(jnp.concatenate(parts, axis=-1))
        return out, heads

    def pieces(self, nchip):
        off, where = 0, {}
        for n in self.order:
            where[n] = off
            off += self.segs[n][1] - self.segs[n][0]
        names = sorted(self.segs, key=lambda n: self.segs[n][0])
        sw = self.width // nchip
        cw = -(-sw // HD) * HD
        table, heads = [], []
        for s in range(nchip):
            lo, hi = s * sw, (s + 1) * sw
            pieces = []
            for n in names:
                a, b = max(lo, self.segs[n][0]), min(hi, self.segs[n][1])
                if a < b:
                    pieces.append((where[n] + a - self.segs[n][0], b - a))
            start, width = pieces[0]
            n_head = min((-start) % HD, width)
            body = ([(start + n_head, width - n_head)] if width > n_head else []) + pieces[1:]
            rows, at = [], 0
            for c, w in body:
                rows.append((c, w, at))
                at += w
            if n_head:
                rows.append((start, n_head, cw - n_head))
            table.append(rows)
            heads.append(n_head)
        return table, heads, cw

    def from_containers(self, conts):
        table, _, _ = self.pieces(len(conts))
        cut = sorted((kcol, w, s, ccol) for s, rows in enumerate(table) for kcol, w, ccol in rows)
        parts, at = [], 0
        for kcol, w, s, ccol in cut:
            assert kcol == at, (kcol, at)
            parts.append(conts[s][..., ccol:ccol + w])
            at = kcol + w
        parts.append(jnp.zeros(conts[0].shape[:-1] + (self.pw - at,), conts[0].dtype))
        return jnp.concatenate(parts, axis=-1)

    def from_kernel(self, g):
        off, where = 0, {}
        for n in self.order:
            w = self.segs[n][1] - self.segs[n][0]
            where[n] = (off, off + w)
            off += w
        names = sorted(self.segs, key=lambda n: self.segs[n][0])
        return jnp.concatenate([g[..., where[n][0]:where[n][1]] for n in names], axis=-1)


def _norm_proj_fwd(h, nw, wp):
    n, d = h.shape
    pw = wp.shape[1]
    tm, tn = _tile(n, 1408, HD), _ctile(pw)

    def body(h_ref, nw_ref, w_ref, o_ref, xt_ref, xn_ref):
        @pl.when(pl.program_id(1) == 0)
        def _():
            x = h_ref[...]
            r = lax.rsqrt(jnp.mean(x * x, axis=-1, keepdims=True) + EPS)
            xn = (x * r * nw_ref[...]).astype(BF16)
            xn_ref[...] = xn
            xt_ref[...] = xn.T

        o_ref[...] = jnp.dot(xn_ref[...], w_ref[...], preferred_element_type=F32)

    return pl.pallas_call(
        body, grid=(n // tm, pw // tn),
        in_specs=[pl.BlockSpec((tm, d), lambda i, j: (i, 0)), pl.BlockSpec((1, d), lambda i, j: (0, 0)),
                  pl.BlockSpec((d, tn), lambda i, j: (0, j))],
        out_specs=[pl.BlockSpec((tm, tn), lambda i, j: (i, j)), pl.BlockSpec((d, tm), lambda i, j: (0, i))],
        out_shape=[SDS((n, pw), F32), SDS((d, n), BF16)], scratch_shapes=[pltpu.VMEM((tm, d), BF16)],
        compiler_params=_params(2), name="norm_proj_fwd")(h, nw, wp)


def _row_valid(tm, tp, base):
    row = base + _iota2((tm, 1), 0)
    return lax.rem(row, tp) >= PAD


def _proj_bwd_dx(dproj, wp, h, nw, dhn, tp):
    n, d = h.shape
    pw = wp.shape[1]
    tm, tk = _tile16(n, 1056), _ctile(pw)
    nk = pw // tk

    def body(dp_ref, w_ref, h_ref, nw_ref, dhn_ref, dh_ref, dnw_ref, acc_ref):
        i, k = pl.program_id(0), pl.program_id(1)

        @pl.when(k == 0)
        def _():
            acc_ref[...] = jnp.zeros_like(acc_ref)

        @pl.when((i == 0) & (k == 0))
        def _():
            dnw_ref[...] = jnp.zeros_like(dnw_ref)

        valid = _row_valid(tm, tp, i * tm)
        dp = jnp.where(valid, dp_ref[...], 0.0)
        acc_ref[...] += _bnt(dp, w_ref[...])

        @pl.when(k == nk - 1)
        def _():
            x = h_ref[...]
            r = lax.rsqrt(jnp.mean(x * x, axis=-1, keepdims=True) + EPS)
            xh = x * r
            dxn = acc_ref[...]
            dnw_ref[...] += jnp.sum(dxn * xh, axis=0, keepdims=True)
            dxh = dxn * nw_ref[...]
            dh_ref[...] = dhn_ref[...] + r * (dxh - xh * jnp.mean(dxh * xh, axis=-1, keepdims=True))

    return pl.pallas_call(
        body, grid=(n // tm, nk),
        in_specs=[pl.BlockSpec((tm, tk), lambda i, k: (i, k)), pl.BlockSpec((d, tk), lambda i, k: (0, k)),
                  pl.BlockSpec((tm, d), lambda i, k: (i, 0)), pl.BlockSpec((1, d), lambda i, k: (0, 0)),
                  pl.BlockSpec((tm, d), lambda i, k: (i, 0))],
        out_specs=[pl.BlockSpec((tm, d), lambda i, k: (i, 0)), pl.BlockSpec((1, d), lambda i, k: (0, 0))],
        out_shape=[SDS((n, d), F32), SDS((1, d), F32)],
        scratch_shapes=[pltpu.VMEM((tm, d), F32)], compiler_params=_params(2), name="proj_bwd_dx")(dproj, wp, h, nw, dhn)


def _proj_bwd_dw(dproj, xt, tp):
    d, n = xt.shape
    pw = dproj.shape[1]
    tm, tn = _tile(n, 1408, HD), _ctile(pw)

    def body(dp_ref, xt_ref, dw_ref):
        i = pl.program_id(1)

        @pl.when(i == 0)
        def _():
            dw_ref[...] = jnp.zeros_like(dw_ref)

        dp = jnp.where(_row_valid(tm, tp, i * tm), dp_ref[...], 0.0)
        dw_ref[...] += jnp.dot(xt_ref[...], dp.astype(BF16), preferred_element_type=F32)

    return pl.pallas_call(
        body, grid=(pw // tn, n // tm),
        in_specs=[pl.BlockSpec((tm, tn), lambda j, i: (i, j)), pl.BlockSpec((d, tm), lambda j, i: (0, i))],
        out_specs=pl.BlockSpec((d, tn), lambda j, i: (0, j)), out_shape=SDS((d, pw), F32),
        compiler_params=_params(2), name="proj_bwd_dw")(dproj, xt)


def _conv_silu(x, w, row):
    c = x * w[CONV_K - 1:CONV_K, :]
    for k in range(1, CONV_K):
        c = c + jnp.where(row >= k, pltpu.roll(x, k, axis=0), 0.0) * w[CONV_K - 1 - k:CONV_K - k, :]
    return c


def _gdn_prep_fwd(proj, conv_w, lay, nb, tp):
    n = proj.shape[0]
    nblk = 3 * NH
    cb = lay.c_qkv // HD
    assert cb % PREP_HEADS == 0 and nblk % PREP_HEADS == 0

    def body(p_ref, w_ref, o_ref):
        row = _iota2((tp, HD), 0)
        for t in range(PREP_HEADS):
            j = pl.program_id(1) * PREP_HEADS + t
            ls = slice(t * HD, (t + 1) * HD)
            c = _conv_silu(p_ref[:, ls], w_ref[:, ls], row)
            s = _silu(c)
            r = lax.rsqrt(_rs(s * s) + EPS)
            scale = jnp.where(j < NH, Q_SCALE, 1.0)
            y = jnp.where(j < 2 * NH, s * r * scale, s)
            o_ref[:, ls] = jnp.where(row >= PAD, y, 0.0)

    wd = PREP_HEADS * HD
    return pl.pallas_call(
        body, grid=(nb, nblk // PREP_HEADS),
        in_specs=[pl.BlockSpec((tp, wd), lambda b, j: (b, cb // PREP_HEADS + j)), pl.BlockSpec((CONV_K, wd), lambda b, j: (0, j))],
        out_specs=pl.BlockSpec((tp, wd), lambda b, j: (b, j)), out_shape=SDS((n, nblk * HD), F32),
        compiler_params=_params(2), name="gdn_prep_fwd")(proj, conv_w)


def _gdn_prep_bwd(proj, conv_w, dqkv, dproj, lay, nb, tp):
    nblk = 3 * NH
    cb = lay.c_qkv // HD

    def body(p_ref, w_ref, dy_ref, dp_in, dp_ref, dw_ref):
        b = pl.program_id(1)
        row = _iota2((tp, HD), 0)
        r4 = _iota2((CONV_K, HD), 0)
        for t in range(PREP_HEADS):
            j = pl.program_id(0) * PREP_HEADS + t
            ls = slice(t * HD, (t + 1) * HD)
            x = p_ref[:, ls]
            w = w_ref[:, ls]
            c = _conv_silu(x, w, row)
            s = _silu(c)
            dy = jnp.where(row >= PAD, dy_ref[:, ls], 0.0)
            r = lax.rsqrt(_rs(s * s) + EPS)
            nh = s * r
            scale = jnp.where(j < NH, Q_SCALE, 1.0)
            ds_n = scale * r * (dy - nh * _rs(dy * nh))
            ds = jnp.where(j < 2 * NH, ds_n, dy)
            dc = ds * _dsilu(c)
            dx = dc * w[CONV_K - 1:CONV_K, :]
            dws = [jnp.sum(dc * x, axis=0, keepdims=True)]
            for k in range(1, CONV_K):
                dx = dx + jnp.where(row < tp - k, pltpu.roll(dc, tp - k, axis=0), 0.0) * w[CONV_K - 1 - k:CONV_K - k, :]
                xs = jnp.where(row >= k, pltpu.roll(x, k, axis=0), 0.0)
                dws.append(jnp.sum(dc * xs, axis=0, keepdims=True))
            dp_ref[:, ls] = dx.astype(dp_ref.dtype)
            dw = jnp.zeros((CONV_K, HD), F32)
            for k in range(CONV_K):
                dw = dw + jnp.where(r4 == CONV_K - 1 - k, dws[k], 0.0)

            @pl.when(b == 0)
            def _():
                dw_ref[:, ls] = dw

            @pl.when(b > 0)
            def _():
                dw_ref[:, ls] += dw

    wd = PREP_HEADS * HD
    return pl.pallas_call(
        body, grid=(nblk // PREP_HEADS, nb),
        in_specs=[pl.BlockSpec((tp, wd), lambda j, b: (b, cb // PREP_HEADS + j)), pl.BlockSpec((CONV_K, wd), lambda j, b: (0, j)),
                  pl.BlockSpec((tp, wd), lambda j, b: (b, j)), pl.BlockSpec(memory_space=pl.ANY)],
        out_specs=[pl.BlockSpec((tp, wd), lambda j, b: (b, cb // PREP_HEADS + j)), pl.BlockSpec((CONV_K, wd), lambda j, b: (0, j))],
        out_shape=[SDS(dproj.shape, dproj.dtype), SDS((CONV_K, nblk * HD), F32)],
        input_output_aliases={3: 0}, compiler_params=_params(2), name="gdn_prep_bwd")(proj, conv_w, dqkv, dproj)


def _gate_consts():
    e = np.zeros((HD, 2 * HW), np.float32)
    s = np.zeros((2 * HW, HD), np.float32)
    for h in range(NH):
        e[h, h * HD:(h + 1) * HD] = 1.0
        e[NH + h, HW + h * HD:HW + (h + 1) * HD] = 1.0
        s[h * HD, h] = 1.0
        s[HW + h * HD, NH + h] = 1.0
    return jnp.asarray(e), jnp.asarray(s)


def _gdn_tri():
    i, j = _iota2((CH, CH), 0), _iota2((CH, CH), 1)
    return i >= j, i > j


def _each(fn, *lists):
    return [fn(*xs) for xs in zip(*lists)]


def _tri_inv(a_list, eye):
    p = [-a for a in a_list]
    t = [eye + x for x in p]
    for _ in range(5):
        p = _each(_rnn, p, p)
        tp_ = _each(_rnn, t, p)
        t = _each(lambda x, y: x + y, t, tp_)
    return t


def _gdn_chunks(args, solved=None):
    causal, strict = _gdn_tri()
    eye = jnp.where(_iota2((CH, CH), 0) == _iota2((CH, CH), 1), 1.0, 0.0)
    q, k, v, beta, g, s0 = (list(t) for t in zip(*args))
    gc = [_scan_rows(x, CH) for x in g]
    dm = [jnp.where(causal, jnp.exp(jnp.where(causal, x[:, :CH] - x[:, :CH].T, 0.0)), 0.0) for x in gc]
    ds = [jnp.where(strict, x, 0.0) for x in dm]
    kb = _each(lambda x, y: x * y, k, beta)
    by_k = _each(_ent, [jnp.concatenate([x, y], axis=0) for x, y in zip(kb, q)], k)
    kk = [x[:CH] for x in by_k]
    qk = [x[CH:] for x in by_k]
    a = _each(lambda x, y: x * y, kk, ds)
    eg = [jnp.exp(x) for x in gc]
    rw = _each(lambda x, y: x * y, kb, eg)
    if solved is None:
        tinv = _tri_inv(a, eye)
        rv = _each(lambda x, y: x * y, v, beta)
        u = _each(_rnn, tinv, rv)
        w = _each(_rnn, tinv, rw)
    else:
        tinv, u, w = (list(t) for t in zip(*solved))
    ws = _each(_enn, w, s0)
    vn = _each(lambda x, y: x - y, u, ws)
    p = _each(lambda x, y: x * y, qk, dm)
    qg = _each(lambda x, y: x * y, q, eg)
    out = []
    for i in range(len(args)):
        gl = gc[i][CH - 1:CH, :]
        ek = jnp.exp(gl - gc[i])
        out.append(dict(gc=gc[i], dm=dm[i], ds=ds[i], kb=kb[i], a=a[i], tinv=tinv[i], eg=eg[i], rw=rw[i], u=u[i], w=w[i],
                        vn=vn[i], p=p[i], qg=qg[i], egl=jnp.exp(gl), ek=ek, kd=k[i] * ek))
    return out


def _gdn_gates(ba, e, alog, dtb):
    raw = _nn(ba, e)
    beta = _sig(raw[:, :HW])
    za = raw[:, HW:] + dtb
    g = -jnp.exp(alog) * _softplus(za)
    return beta, g, za


def _seqs_per_step(nb):
    return 4 if nb % 4 == 0 else (2 if nb % 2 == 0 else 1)


def _gdn_fwd(qkv, proj, e_mat, alog, dtb, lay, nb, nc):
    n = qkv.shape[0]
    tp = n // nb
    cba = lay.c_ba // HD
    gb = _seqs_per_step(nb)

    def body(x_ref, ba_ref, e_ref, al_ref, dt_ref, o_ref, so_ref, sv_ref, s_ref):
        @pl.when(pl.program_id(1) == 0)
        def _():
            s_ref[...] = jnp.zeros_like(s_ref)

        args = []
        for j in range(gb):
            beta, g, _ = _gdn_gates(ba_ref[j], e_ref[...], al_ref[...], dt_ref[...])
            for h in range(NH):
                hs = slice(h * HD, (h + 1) * HD)
                args.append((x_ref[j, :, hs], x_ref[j, :, HW + h * HD:HW + (h + 1) * HD],
                             x_ref[j, :, 2 * HW + h * HD:2 * HW + (h + 1) * HD], beta[:, hs], g[:, hs], s_ref[j, h]))
        cs = _gdn_chunks(args)
        s0s = [a[5] for a in args]
        o1 = _each(lambda c, s0: _enn(c["qg"], s0), cs, s0s)
        o2 = [_enn(c["p"], c["vn"]) for c in cs]
        upd = [_etn(c["kd"], c["vn"]) for c in cs]
        res = [(o1[i] + o2[i], s0s[i] * cs[i]["egl"] + upd[i]) for i in range(len(cs))]
        zero = jnp.zeros((CH, HD - CH), F32)
        for j in range(gb):
            for h in range(NH):
                c = cs[j * NH + h]
                so_ref[j, h] = args[j * NH + h][5]
                sv_ref[j, h] = jnp.concatenate([c["u"], c["w"], c["tinv"], zero], axis=-1)
                s_ref[j, h] = res[j * NH + h][1]
            o_ref[j] = jnp.concatenate([res[j * NH + h][0] for h in range(NH)], axis=-1)

    o, st, sv = pl.pallas_call(
        body, grid=(nb // gb, nc),
        in_specs=[pl.BlockSpec((gb, CH, 3 * HW), lambda b, c: (b, c, 0)), pl.BlockSpec((gb, CH, HD), lambda b, c: (b, c, cba)),
                  pl.BlockSpec((HD, 2 * HW), lambda b, c: (0, 0)), pl.BlockSpec((1, HW), lambda b, c: (0, 0)),
                  pl.BlockSpec((1, HW), lambda b, c: (0, 0))],
        out_specs=[pl.BlockSpec((gb, CH, HW), lambda b, c: (b, c, 0)),
                   pl.BlockSpec((gb, None, NH, HD, HD), lambda b, c: (b, c, 0, 0, 0)),
                   pl.BlockSpec((gb, None, NH, CH, 3 * HD), lambda b, c: (b, c, 0, 0, 0))],
        out_shape=[SDS((nb, tp, HW), F32), SDS((nb, nc, NH, HD, HD), F32), SDS((nb, nc, NH, CH, 3 * HD), F32)],
        scratch_shapes=[pltpu.VMEM((gb, NH, HD, HD), F32)], compiler_params=_params(2), name="gdn_fwd")(
            qkv.reshape(nb, tp, 3 * HW), proj.reshape(nb, tp, -1), e_mat, alog, dtb)
    return o.reshape(n, HW), st, sv


def _gdn_bwd(qkv, proj, e_mat, s_mat, alog, dtb, states, solved, do, dproj, lay, nb, nc):
    n = qkv.shape[0]
    tp = n // nb
    cba = lay.c_ba // HD
    gb = _seqs_per_step(nb)

    def body(x_ref, ba_ref, e_ref, sm_ref, al_ref, dt_ref, st_ref, sv_ref, do_ref, dp_in, dx_ref, dba_ref, acc_ref, ds_ref):
        ci = pl.program_id(1)

        @pl.when(ci == 0)
        def _():
            ds_ref[...] = jnp.zeros_like(ds_ref)

        @pl.when((ci == 0) & (pl.program_id(0) == 0))
        def _():
            acc_ref[...] = jnp.zeros_like(acc_ref)

        causal, strict = _gdn_tri()
        alog = al_ref[...]
        row = _iota2((CH, 1), 0)
        valid = (row >= PAD) | (ci < nc - 1)
        last = row == CH - 1
        gates = [_gdn_gates(ba_ref[j], e_ref[...], alog, dt_ref[...]) for j in range(gb)]
        args, do, ds1, solved = [], [], [], []
        for j in range(gb):
            beta, g, _ = gates[j]
            for h in range(NH):
                hs = slice(h * HD, (h + 1) * HD)
                args.append((x_ref[j, :, hs], x_ref[j, :, HW + h * HD:HW + (h + 1) * HD],
                             x_ref[j, :, 2 * HW + h * HD:2 * HW + (h + 1) * HD], beta[:, hs], g[:, hs], st_ref[j, h]))
                do.append(do_ref[j, :, hs])
                ds1.append(ds_ref[j, h])
                solved.append((sv_ref[j, h, :, 2 * HD:2 * HD + CH], sv_ref[j, h, :, 0:HD], sv_ref[j, h, :, HD:2 * HD]))
        q, k, v, bh, _, s0 = (list(t) for t in zip(*args))
        cs = _gdn_chunks(args, solved)
        get = lambda name: [c[name] for c in cs]
        mul = lambda x, y: x * y
        add = lambda x, y: x + y
        dvn = _each(add, _each(_etn, get("p"), do), _each(_enn, get("kd"), ds1))
        by_s0 = _each(_ent, [jnp.concatenate([x, y], axis=0) for x, y in zip(do, dvn)], s0)
        dqg = [x[:CH] for x in by_s0]
        dw = [-x[CH:] for x in by_s0]
        dp = [jnp.where(causal, x, 0.0) for x in _each(_ent, do, get("vn"))]
        dkd = _each(_ent, get("vn"), ds1)
        ds_a = _each(_etn, get("qg"), do)
        ds_b = _each(_etn, get("w"), dvn)
        ds_new = [ds_a[i] - ds_b[i] + ds1[i] * cs[i]["egl"] for i in range(len(cs))]
        drvw = _each(_rtn, get("tinv"), [jnp.concatenate([x, y], axis=-1) for x, y in zip(dvn, dw)])
        drv = [x[:, :HD] for x in drvw]
        drw = [x[:, HD:] for x in drvw]
        uw = [sv_ref[j, h, :, 0:2 * HD] for j in range(gb) for h in range(NH)]
        da = [jnp.where(strict, -x, 0.0) for x in _each(_rnt, drvw, uw)]
        m = [da[i] * cs[i]["a"] + dp[i] * cs[i]["p"] for i in range(len(cs))]
        dkk = _each(mul, da, get("ds"))
        dqk = _each(mul, dp, get("dm"))
        by_k = _each(_enn, [jnp.concatenate([x, y], axis=0) for x, y in zip(dqk, dkk)], k)
        dq = _each(add, [x[:CH] for x in by_k], _each(mul, dqg, get("eg")))
        dkb = _each(add, [x[CH:] for x in by_k], _each(mul, drw, get("eg")))
        dk_1 = _each(_etn, dqk, q)
        dk_2 = _each(_etn, dkk, get("kb"))
        dk = [dk_1[i] + dk_2[i] + dkd[i] * cs[i]["ek"] + dkb[i] * bh[i] for i in range(len(cs))]
        dv = _each(mul, drv, bh)
        dbeta, dg = [], []
        for i, c in enumerate(cs):
            dbeta.append(_rs(drv[i] * v[i]) + _rs(dkb[i] * k[i]) + jnp.zeros((CH, HD), F32))
            t_kd = _rs(dkd[i] * c["kd"])
            dgc = _rs(m[i]) - _rs(m[i].T) + _rs(dqg[i] * c["qg"]) + _rs(drw[i] * c["rw"]) - t_kd
            tail = jnp.sum(t_kd, axis=0, keepdims=True) + c["egl"] * jnp.sum(_rs(s0[i] * ds1[i]), axis=0, keepdims=True)
            dgc = dgc + jnp.where(last, tail, 0.0)
            dg.append(_scan_rows(dgc + jnp.zeros((CH, HD), F32), CH, reverse=True))
        r8 = _iota2((8, HW), 0)
        upd = jnp.zeros((8, HW), F32)
        for j in range(gb):
            sl = slice(j * NH, (j + 1) * NH)
            beta, g, za = gates[j]
            for h in range(NH):
                ds_ref[j, h] = ds_new[j * NH + h]
            dx_ref[j] = jnp.concatenate(dq[sl] + dk[sl] + dv[sl], axis=-1)
            dbeta_j = jnp.where(valid, jnp.concatenate(dbeta[sl], axis=-1), 0.0)
            dg_j = jnp.where(valid, jnp.concatenate(dg[sl], axis=-1), 0.0)
            draw_b = dbeta_j * beta * (1.0 - beta)
            draw_a = dg_j * (-jnp.exp(alog)) * _sig(za)
            dba_ref[j] = _nn(jnp.concatenate([draw_b, draw_a], axis=-1), sm_ref[...]).astype(dba_ref.dtype)
            upd = upd + jnp.where(r8 == 0, jnp.sum(dg_j * g, axis=0, keepdims=True), 0.0) + jnp.where(
                r8 == 1, jnp.sum(draw_a, axis=0, keepdims=True), 0.0)
        acc_ref[...] += upd

    rc = lambda c: nc - 1 - c
    dqkv, dproj3, acc = pl.pallas_call(
        body, grid=(nb // gb, nc),
        in_specs=[pl.BlockSpec((gb, CH, 3 * HW), lambda b, c: (b, rc(c), 0)), pl.BlockSpec((gb, CH, HD), lambda b, c: (b, rc(c), cba)),
                  pl.BlockSpec((HD, 2 * HW), lambda b, c: (0, 0)), pl.BlockSpec((2 * HW, HD), lambda b, c: (0, 0)),
                  pl.BlockSpec((1, HW), lambda b, c: (0, 0)), pl.BlockSpec((1, HW), lambda b, c: (0, 0)),
                  pl.BlockSpec((gb, None, NH, HD, HD), lambda b, c: (b, rc(c), 0, 0, 0)),
                  pl.BlockSpec((gb, None, NH, CH, 3 * HD), lambda b, c: (b, rc(c), 0, 0, 0)),
                  pl.BlockSpec((gb, CH, HW), lambda b, c: (b, rc(c), 0)), pl.BlockSpec(memory_space=pl.ANY)],
        out_specs=[pl.BlockSpec((gb, CH, 3 * HW), lambda b, c: (b, rc(c), 0)), pl.BlockSpec((gb, CH, HD), lambda b, c: (b, rc(c), cba)),
                   pl.BlockSpec((8, HW), lambda b, c: (0, 0))],
        out_shape=[SDS((nb, tp, 3 * HW), F32), SDS((nb, tp, dproj.shape[1]), dproj.dtype), SDS((8, HW), F32)],
        input_output_aliases={9: 1},
        scratch_shapes=[pltpu.VMEM((gb, NH, HD, HD), F32)], compiler_params=_params(2), name="gdn_bwd")(
            qkv.reshape(nb, tp, 3 * HW), proj.reshape(nb, tp, -1), e_mat, s_mat, alog, dtb, states, solved, do.reshape(nb, tp, HW),
            dproj.reshape(nb, tp, -1))
    return dqkv.reshape(n, 3 * HW), dproj3.reshape(dproj.shape), acc


def _hgrn_inputs(zq, zf, lb):
    sg = _sig(zf)
    sgn = _sig(-zf)
    pos = lb > 0.0
    lbp = jnp.where(pos, lb, 0.0)
    fpos = lbp + (1.0 - lbp) * sg
    lf = jnp.where(pos, jnp.log(jnp.where(pos, fpos, 1.0)), _logsig(zf))
    k = (1.0 - lbp) * sgn
    q = _silu(zq) * Q_SCALE
    return q, k, lf, sg, sgn, pos, lbp, fpos


def _hgrn_consts():
    i3, j3 = _iota2((SUB, SUB, HD), 0), _iota2((SUB, SUB, HD), 1)
    return i3 >= j3


def _sum_j(x):
    return jnp.sum(x.reshape(SUB, SUB, HD), axis=1)


def _sum_i(x):
    return jnp.sum(x.reshape(SUB, SUB, HD), axis=0)


def _pairs(a, b):
    return (a[:, None, :] * b[None, :, :]).reshape(SUB * SUB, HD)


def _hgrn_sub(q, k, v, bc, st, consts):
    mask3 = consts
    bl = bc[SUB - 1:SUB, :]
    p3 = jnp.where(mask3, jnp.exp(jnp.where(mask3, bc[:, None, :] - bc[None, :, :], 0.0)), 0.0).reshape(SUB * SUB, HD)
    x = _pairs(q, k) * p3
    srep = _rs(x)
    vt = jnp.broadcast_to(v[None, :, :], (SUB, SUB, HD)).reshape(SUB * SUB, HD)
    eb = jnp.exp(bc)
    qe = q * eb
    o = _hnt(qe, st) + _sum_j(_rr(srep) * _rr(vt))
    ek = jnp.exp(bl - bc)
    kd = k * ek
    ebl = jnp.exp(bl)
    st1 = st * ebl + _htn(v, kd)
    return o, st1, dict(bc=bc, p3=p3, srep=srep, vt=vt, eb=eb, qe=qe, ek=ek, kd=kd, ebl=ebl)


def _hgrn_fwd(proj, lb, lay, nb, nc):
    n = proj.shape[0]
    tp = n // nb
    cbb = lay.c_b // (3 * HW)
    gb = _seqs_per_step(nb)

    def body(z_ref, lb_ref, o_ref, so_ref, s_ref):
        @pl.when(pl.program_id(1) == 0)
        def _():
            s_ref[...] = jnp.zeros_like(s_ref)

        consts = _hgrn_consts()
        for j in range(gb):
            outs = []
            for h in range(NH):
                hs = slice(h * HD, (h + 1) * HD)
                q, k, lf = _hgrn_inputs(z_ref[j, :, hs], z_ref[j, :, HW + h * HD:HW + (h + 1) * HD], lb_ref[:, hs])[:3]
                v = z_ref[j, :, 2 * HW + h * HD:2 * HW + (h + 1) * HD]
                st = s_ref[j, h]
                so_ref[j, h] = st
                bc = _scan_rows(lf, SUB)
                oh = []
                for s in range(CH // SUB):
                    rs = slice(s * SUB, (s + 1) * SUB)
                    o, st, _ = _hgrn_sub(q[rs], k[rs], v[rs], bc[rs], st, consts)
                    oh.append(o)
                s_ref[j, h] = st
                outs.append(jnp.concatenate(oh, axis=0))
            o_ref[j] = jnp.concatenate(outs, axis=-1)

    o, st = pl.pallas_call(
        body, grid=(nb // gb, nc),
        in_specs=[pl.BlockSpec((gb, CH, 3 * HW), lambda b, c: (b, c, cbb)), pl.BlockSpec((1, HW), lambda b, c: (0, 0))],
        out_specs=[pl.BlockSpec((gb, CH, HW), lambda b, c: (b, c, 0)),
                   pl.BlockSpec((gb, None, NH, HD, HD), lambda b, c: (b, c, 0, 0, 0))],
        out_shape=[SDS((nb, tp, HW), F32), SDS((nb, nc, NH, HD, HD), F32)],
        scratch_shapes=[pltpu.VMEM((gb, NH, HD, HD), F32)], compiler_params=_params(2), name="hgrn_fwd")(
            proj.reshape(nb, tp, -1), lb)
    return o.reshape(n, HW), st


def _hgrn_bwd(proj, lb, states, do, dproj, lay, nb, nc):
    n = proj.shape[0]
    tp = n // nb
    cbb = lay.c_b // (3 * HW)
    nsub = CH // SUB
    gb = _seqs_per_step(nb)

    def body(z_ref, lb_ref, st_ref, do_ref, dp_in, dz_ref, acc_ref, ds_ref):
        ci = pl.program_id(1)

        @pl.when(ci == 0)
        def _():
            ds_ref[...] = jnp.zeros_like(ds_ref)

        @pl.when((ci == 0) & (pl.program_id(0) == 0))
        def _():
            acc_ref[...] = jnp.zeros_like(acc_ref)

        upd = jnp.zeros((8, HW), F32)
        for j in range(gb):
            upd = upd + one_seq(j, ci, z_ref, lb_ref, st_ref, do_ref, dz_ref, ds_ref)
        acc_ref[...] += upd

    def one_seq(j, ci, z_ref, lb_ref, st_ref, do_ref, dz_ref, ds_ref):
        consts = _hgrn_consts()
        row = _iota2((CH, 1), 0)
        valid = (row >= PAD) | (ci < nc - 1)
        lastrow = _iota2((SUB, 1), 0) == SUB - 1
        dzq, dzf, dzi, dlbs = [], [], [], []
        for h in range(NH):
            hs = slice(h * HD, (h + 1) * HD)
            zq, zf = z_ref[j, :, hs], z_ref[j, :, HW + h * HD:HW + (h + 1) * HD]
            q, k, lf, sg, sgn, pos, lbp, fpos = _hgrn_inputs(zq, zf, lb_ref[:, hs])
            v = z_ref[j, :, 2 * HW + h * HD:2 * HW + (h + 1) * HD]
            doh = do_ref[j, :, hs]
            sts, fw = [st_ref[j, h]], []
            bc = _scan_rows(lf, SUB)
            for s in range(nsub):
                rs = slice(s * SUB, (s + 1) * SUB)
                _, st1, c = _hgrn_sub(q[rs], k[rs], v[rs], bc[rs], sts[-1], consts)
                sts.append(st1)
                fw.append(c)
            dst = ds_ref[j, h]
            dq_l, dk_l, dv_l, dlf_l = [None] * nsub, [None] * nsub, [None] * nsub, [None] * nsub
            for s in reversed(range(nsub)):
                rs = slice(s * SUB, (s + 1) * SUB)
                c, st = fw[s], sts[s]
                qs, ks, vs, dos = q[rs], k[rs], v[rs], doh[rs]
                dqe = _hnn(dos, st)
                dkd = _hnn(vs, dst)
                dsrep = _rs(_pairs(_rr(dos), _rr(vs)))
                w = dsrep * c["p3"]
                kt = jnp.broadcast_to(ks[None, :, :], (SUB, SUB, HD)).reshape(SUB * SUB, HD)
                qt = jnp.broadcast_to(qs[:, None, :], (SUB, SUB, HD)).reshape(SUB * SUB, HD)
                dq_i = _sum_j(w * kt)
                dk_i = _sum_i(w * qt)
                dot = jnp.broadcast_to(_rr(dos)[:, None, :], (SUB, SUB, HD)).reshape(SUB * SUB, HD)
                dvv = _sum_i(_rr(c["srep"]) * dot) + _hnt(c["kd"], dst)
                t_kd = dkd * c["kd"]
                dbc = dqe * c["qe"] - t_kd + qs * dq_i - ks * dk_i
                tail = jnp.sum(t_kd, axis=0, keepdims=True) + c["ebl"] * jnp.sum(st * dst, axis=0, keepdims=True)
                dbc = dbc + jnp.where(lastrow, tail, 0.0)
                dlf_l[s] = dbc
                dq_l[s] = dq_i + dqe * c["eb"]
                dk_l[s] = dk_i + dkd * c["ek"]
                dv_l[s] = dvv
                dst = _htn(dos, c["qe"]) + dst * c["ebl"]
            ds_ref[j, h] = dst
            dq, dk, dv, dbc = (jnp.concatenate(t, axis=0) for t in (dq_l, dk_l, dv_l, dlf_l))
            dlf = _scan_rows(dbc, SUB, reverse=True)
            dlft = dlf - dk * (1.0 - k)
            dlf_dz = jnp.where(pos, (1.0 - lbp) * sg * sgn / jnp.where(pos, fpos, 1.0), sgn)
            dlf_dlb = jnp.where(pos, sgn / jnp.where(pos, fpos, 1.0), 0.0)
            dzq.append(dq * Q_SCALE * _dsilu(zq))
            dzf.append(dlft * dlf_dz)
            dzi.append(dv)
            dlbs.append(jnp.sum(jnp.where(valid, dlft * dlf_dlb, 0.0), axis=0, keepdims=True))
        dz_ref[j] = jnp.concatenate(dzq + dzf + dzi, axis=-1).astype(dz_ref.dtype)
        return jnp.where(_iota2((8, HW), 0) == 0, jnp.concatenate(dlbs, axis=-1), 0.0)

    rc = lambda c: nc - 1 - c
    dproj3, acc = pl.pallas_call(
        body, grid=(nb // gb, nc),
        in_specs=[pl.BlockSpec((gb, CH, 3 * HW), lambda b, c: (b, rc(c), cbb)), pl.BlockSpec((1, HW), lambda b, c: (0, 0)),
                  pl.BlockSpec((gb, None, NH, HD, HD), lambda b, c: (b, rc(c), 0, 0, 0)),
                  pl.BlockSpec((gb, CH, HW), lambda b, c: (b, rc(c), 0)), pl.BlockSpec(memory_space=pl.ANY)],
        out_specs=[pl.BlockSpec((gb, CH, 3 * HW), lambda b, c: (b, rc(c), cbb)), pl.BlockSpec((8, HW), lambda b, c: (0, 0))],
        out_shape=[SDS((nb, tp, dproj.shape[1]), dproj.dtype), SDS((8, HW), F32)],
        input_output_aliases={4: 0},
        scratch_shapes=[pltpu.VMEM((gb, NH, HD, HD), F32)], compiler_params=_params(2), name="hgrn_bwd")(
            proj.reshape(nb, tp, -1), lb, states, do.reshape(nb, tp, HW), dproj.reshape(nb, tp, -1))
    return dproj3.reshape(dproj.shape), acc


def _gated_norm(o, z, gamma):
    ys, ns, rs = [], [], []
    for h in range(NH):
        hs = slice(h * HD, (h + 1) * HD)
        oh = o[:, hs]
        r = lax.rsqrt(jnp.mean(oh * oh, axis=-1, keepdims=True) + EPS)
        nh = oh * r
        ys.append(nh * gamma * _silu(z[:, hs]))
        ns.append(nh)
        rs.append(r)
    return jnp.concatenate(ys, axis=-1), ns, rs


def _merge_fwd(h, oa, ob, proj, ga, gb, wa, wb, wo, lay):
    n, d = h.shape
    tm = _tile(n, 384)
    wm = lay.wm

    def body(h_ref, oa_ref, ob_ref, p_ref, ga_ref, gb_ref, wa_ref, wb_ref, wo_ref, out_ref):
        ya, _, _ = _gated_norm(oa_ref[...], p_ref[:, 0:HW], ga_ref[...])
        yb, _, _ = _gated_norm(ob_ref[...], p_ref[:, HW:2 * HW], gb_ref[...])
        ya2 = _bnn(ya, wa_ref[...])
        yb2 = _bnn(yb, wb_ref[...])
        mixed = _sig(p_ref[:, 2 * HW:2 * HW + d]) * ya2 + _sig(p_ref[:, 2 * HW + d:2 * HW + 2 * d]) * yb2
        out_ref[...] = h_ref[...] + _bnn(mixed, wo_ref[...])

    full = lambda shape: pl.BlockSpec(shape, lambda i: (0, 0))
    return pl.pallas_call(
        body, grid=(n // tm,),
        in_specs=[pl.BlockSpec((tm, d), lambda i: (i, 0)), pl.BlockSpec((tm, HW), lambda i: (i, 0)),
                  pl.BlockSpec((tm, HW), lambda i: (i, 0)), pl.BlockSpec((tm, wm), lambda i: (i, 0)),
                  full((1, HD)), full((1, HD)), full((HW, d)), full((HW, d)), full((d, d))],
        out_specs=pl.BlockSpec((tm, d), lambda i: (i, 0)), out_shape=SDS((n, d), F32),
        compiler_params=_params(1), name="merge_fwd")(h, oa, ob, proj, ga, gb, wa, wb, wo)


def _gated_norm_bwd(dy, o, z, gamma):
    dos, dzs = [], []
    dgam = jnp.zeros((1, HD), F32)
    for h in range(NH):
        hs = slice(h * HD, (h + 1) * HD)
        oh, zh, dyh = o[:, hs], z[:, hs], dy[:, hs]
        r = lax.rsqrt(jnp.mean(oh * oh, axis=-1, keepdims=True) + EPS)
        nh = oh * r
        dzs.append(dyh * nh * gamma * _dsilu(zh))
        dng = dyh * _silu(zh)
        dgam = dgam + jnp.sum(dng * nh, axis=0, keepdims=True)
        dn = dng * gamma
        dos.append(r * (dn - nh * jnp.mean(dn * nh, axis=-1, keepdims=True)))
    return jnp.concatenate(dos, axis=-1), jnp.concatenate(dzs, axis=-1), dgam


def _merge_bwd(dhn, oa, ob, proj, ga, gb, wa, wb, wo, lay, tp):
    n, d = dhn.shape
    tm = _tile(n, 256)
    wm = lay.wm

    def body(dh_ref, oa_ref, ob_ref, p_ref, ga_ref, gb_ref, wa_ref, wb_ref, wo_ref,
             dp_ref, doa_ref, dob_ref, dwa_ref, dwb_ref, dwo_ref, dga_ref, dgb_ref):
        i = pl.program_id(0)

        @pl.when(i == 0)
        def _():
            for r in (dwa_ref, dwb_ref, dwo_ref, dga_ref, dgb_ref):
                r[...] = jnp.zeros_like(r)

        dh = jnp.where(_row_valid(tm, tp, i * tm), dh_ref[...], 0.0)
        oa, ob = oa_ref[...], ob_ref[...]
        za, zb = p_ref[:, 0:HW], p_ref[:, HW:2 * HW]
        gta, gtb = p_ref[:, 2 * HW:2 * HW + d], p_ref[:, 2 * HW + d:2 * HW + 2 * d]
        ya, _, _ = _gated_norm(oa, za, ga_ref[...])
        yb, _, _ = _gated_norm(ob, zb, gb_ref[...])
        ya2 = _bnn(ya, wa_ref[...])
        yb2 = _bnn(yb, wb_ref[...])
        sa, sb = _sig(gta), _sig(gtb)
        mixed = sa * ya2 + sb * yb2
        dmixed = _bnt(dh, wo_ref[...])
        dwo_ref[...] += _btn(mixed, dh)
        dya2 = dmixed * sa
        dyb2 = dmixed * sb
        dwa_ref[...] += _btn(ya, dya2)
        dwb_ref[...] += _btn(yb, dyb2)
        doa, dza, dga = _gated_norm_bwd(_bnt(dya2, wa_ref[...]), oa, za, ga_ref[...])
        dob, dzb, dgb = _gated_norm_bwd(_bnt(dyb2, wb_ref[...]), ob, zb, gb_ref[...])
        dga_ref[...] += dga
        dgb_ref[...] += dgb
        doa_ref[...] = doa
        dob_ref[...] = dob
        dt = dp_ref.dtype
        dp_ref[:, 0:HW] = dza.astype(dt)
        dp_ref[:, HW:2 * HW] = dzb.astype(dt)
        dp_ref[:, 2 * HW:2 * HW + d] = (dmixed * ya2 * sa * (1.0 - sa)).astype(dt)
        dp_ref[:, 2 * HW + d:2 * HW + 2 * d] = (dmixed * yb2 * sb * (1.0 - sb)).astype(dt)

    full = lambda shape: pl.BlockSpec(shape, lambda i: (0, 0))
    rows = lambda w: pl.BlockSpec((tm, w), lambda i: (i, 0))
    return pl.pallas_call(
        body, grid=(n // tm,),
        in_specs=[rows(d), rows(HW), rows(HW), rows(wm), full((1, HD)), full((1, HD)), full((HW, d)), full((HW, d)), full((d, d))],
        out_specs=[rows(wm), rows(HW), rows(HW), full((HW, d)), full((HW, d)), full((d, d)), full((1, HD)), full((1, HD))],
        out_shape=[SDS((n, lay.pw), BF16), SDS((n, HW), F32), SDS((n, HW), F32), SDS((HW, d), F32), SDS((HW, d), F32),
                   SDS((d, d), F32), SDS((1, HD), F32), SDS((1, HD), F32)],
        compiler_params=_params(1), name="merge_bwd")(dhn, oa, ob, proj, ga, gb, wa, wb, wo)


def _loss_head(h, target, fw, nb, tp):
    n, d = h.shape
    tr = _tile(tp, 768)
    nr = tp // tr

    def body(h_ref, t_ref, fw_ref, lp_ref, dh_ref, dfw_ref):
        b, i = pl.program_id(0), pl.program_id(1)

        @pl.when((b == 0) & (i == 0))
        def _():
            dfw_ref[...] = jnp.zeros_like(dfw_ref)

        x = h_ref[...]
        r = lax.rsqrt(jnp.mean(x * x, axis=-1, keepdims=True) + EPS)
        xh = x * r
        live = i * tr + _iota2((tr, 1), 0) >= CH
        err = jnp.where(live, xh * fw_ref[...] - t_ref[...], 0.0)
        lp_ref[...] = jnp.zeros_like(lp_ref) + 0.5 * jnp.sum(_rs(err * err), axis=0, keepdims=True) / d
        dy = err / d
        dfw_ref[...] += jnp.sum(dy * xh, axis=0, keepdims=True)
        dxh = dy * fw_ref[...]
        dh_ref[...] = r * (dxh - xh * jnp.mean(dxh * xh, axis=-1, keepdims=True))

    rows = pl.BlockSpec((tr, d), lambda b, i: (b * nr + i, 0))
    return pl.pallas_call(
        body, grid=(nb, nr), in_specs=[rows, rows, pl.BlockSpec((1, d), lambda b, i: (0, 0))],
        out_specs=[pl.BlockSpec((8, HD), lambda b, i: (b * nr + i, 0)), rows, pl.BlockSpec((1, d), lambda b, i: (0, 0))],
        out_shape=[SDS((nb * nr * 8, HD), F32), SDS((n, d), F32), SDS((1, d), F32)],
        compiler_params=_params(2), name="loss_head")(h, target, fw)


def _lb_fwd(lb):
    def body(x_ref, o_ref):
        x = x_ref[...]
        mx = jnp.max(x, axis=0, keepdims=True)
        e = jnp.exp(x - mx)
        sm = e / jnp.sum(e, axis=0, keepdims=True)
        run = jnp.zeros((1, HW), F32)
        for l in range(DEPTH):
            run = run + sm[l:l + 1, :]
            o_ref[l:l + 1, :] = run - sm[0:1, :]

    return pl.pallas_call(body, out_shape=SDS(lb.shape, F32), name="lb_fwd")(lb)


def _lb_bwd(lb, dlb_all):
    def body(x_ref, d_ref, o_ref):
        x = x_ref[...]
        dl = d_ref[...]
        mx = jnp.max(x, axis=0, keepdims=True)
        e = jnp.exp(x - mx)
        sm = e / jnp.sum(e, axis=0, keepdims=True)
        tot = jnp.sum(dl, axis=0, keepdims=True)
        dsm = []
        run = tot
        for l in range(DEPTH):
            dsm.append(run - (tot if l == 0 else 0.0))
            run = run - dl[l:l + 1, :]
        inner = sum(sm[l:l + 1, :] * dsm[l] for l in range(DEPTH))
        for l in range(DEPTH):
            o_ref[l:l + 1, :] = sm[l:l + 1, :] * (dsm[l] - inner)

    return pl.pallas_call(body, out_shape=SDS(lb.shape, F32), name="lb_bwd")(lb, dlb_all)


def _adamw(g, w, m, v):
    r, c = g.shape[-2:]
    tr = _tile(r, 264) if g.ndim == 2 else None
    c1 = 1.0 / (1.0 - ADAM_B1 ** ADAM_STEP)
    c2 = 1.0 / (1.0 - ADAM_B2 ** ADAM_STEP)

    def body(g_ref, w_ref, m_ref, v_ref, d_ref, mo_ref, vo_ref):
        gg = g_ref[...]
        mn = ADAM_B1 * m_ref[...] + (1.0 - ADAM_B1) * gg
        vn = ADAM_B2 * v_ref[...] + (1.0 - ADAM_B2) * gg * gg
        d_ref[...] = -ADAM_LR * ((mn * c1) / (jnp.sqrt(vn * c2) + ADAM_EPS) + ADAM_WD * w_ref[...])
        mo_ref[...] = mn
        vo_ref[...] = vn

    if g.ndim == 3:
        spec = pl.BlockSpec(g.shape[:2] + (HD,), lambda i: (0, 0, i))
        steps = g.shape[2] // HD
    else:
        spec = pl.BlockSpec((tr, c), lambda i: (i, 0))
        steps = r // tr
    return pl.pallas_call(body, grid=(steps,), in_specs=[spec] * 4, out_specs=[spec] * 3, out_shape=[SDS(g.shape, F32)] * 3,
                          compiler_params=_params(1), name="adamw")(g, w, m, v)


def _tile16(n, target):
    return _tile(n // 2, target // 2) * 2 if n % 16 == 0 else _tile(n, target)


def _add_cores(g, got, core):
    k, r, c = got.shape
    tr = _tile16(r, 264)

    def body(c_ref, a_ref, b_ref, o_ref):
        o_ref[...] = (a_ref[...] + b_ref[...].astype(F32)).astype(o_ref.dtype)

    spec = pl.BlockSpec((None, tr, c), lambda s, i, cr: (s, i, 0))
    return pl.pallas_call(
        body, grid_spec=pltpu.PrefetchScalarGridSpec(
            num_scalar_prefetch=1, grid=(k, r // tr),
            in_specs=[pl.BlockSpec((None, None, tr, c), lambda s, i, cr: (cr[0], s, i, 0)), spec], out_specs=spec),
        out_shape=SDS(got.shape, got.dtype), compiler_params=_params(2), name="add_cores")(core, g, got)


def _sum_chips(parts, own, chip, core):
    k, r, c = parts.shape
    tr = _tile16(r, 264)

    def body(chip_ref, core_ref, *refs):
        part_refs, own_ref, o_ref = refs[:k], refs[k], refs[k + 1]
        mine = own_ref[...].astype(F32)
        acc = None
        for s in range(k):
            term = jnp.where(chip_ref[0] == s, mine, part_refs[s][...].astype(F32))
            acc = term if acc is None else acc + term
        o_ref[...] = acc

    def other(s):
        return pl.BlockSpec((None, tr, c), lambda i, ch, co: (jnp.where(ch[0] == s, (s + 1) % k, s), i, 0))

    return pl.pallas_call(
        body, grid_spec=pltpu.PrefetchScalarGridSpec(
            num_scalar_prefetch=2, grid=(r // tr,),
            in_specs=[other(s) for s in range(k)] + [pl.BlockSpec((None, tr, c), lambda i, ch, co: (ch[0], i, 0))],
            out_specs=pl.BlockSpec((None, tr, c), lambda i, ch, co: (co[0], i, 0))),
        out_shape=SDS((2, r, c), F32), compiler_params=_params(1), name="sum_chips")(chip, core, *([parts] * k), own)


def _meta_grad(dh, nb, nc):
    d = dh.shape[1]

    def body(x_ref, o_ref):
        @pl.when(pl.program_id(0) == 0)
        def _():
            o_ref[...] = jnp.zeros_like(o_ref)

        o_ref[...] += x_ref[PAD:CH, :]

    return pl.pallas_call(body, grid=(nb,), in_specs=[pl.BlockSpec((CH, d), lambda b: (b * nc, 0))],
                          out_specs=pl.BlockSpec((N_META, d), lambda b: (0, 0)), out_shape=SDS((N_META, d), F32),
                          compiler_params=_params(1), name="meta_grad")(dh)


ANY = pl.BlockSpec(memory_space=pl.ANY)


def _place():
    x, y, c = lax.axis_index("x"), lax.axis_index("y"), lax.axis_index("c")
    chips = [(1 - x, y), (x, 1 - y), (1 - x, 1 - y)]
    return x, y, c, chips


def _remote(src, dst, send_sems, recv_sems, k, to):
    return pltpu.make_async_remote_copy(src_ref=src, dst_ref=dst, send_sem=send_sems.at[k], recv_sem=recv_sems.at[k],
                                        device_id=to, device_id_type=MESH)


def _gather_weights(pbs, ps):
    nt = len(pbs)

    def body(*refs):
        pb_refs, ps_ref, gb_refs, gs_ref = refs[:nt], refs[nt], refs[nt + 1:2 * nt + 1], refs[2 * nt + 1]
        send_sems, recv_sems, local_sems = refs[2 * nt + 2:]
        x, y, c, chips = _place()
        s = 2 * x + y
        sib = (x, y, 1 - c)
        l1 = pltpu.make_async_copy(ps_ref, gs_ref.at[s], local_sems.at[0])
        l1.start()
        sends = []
        for k, (px, py) in enumerate(chips):
            for t in range(nt):
                sends.append(_remote(pb_refs[t].at[c], gb_refs[t].at[s, c], send_sems, recv_sems, 6 * t + k, (px, py, c)))
            sends.append(_remote(ps_ref, gs_ref.at[s], send_sems, recv_sems, 6 * nt + k, (px, py, c)))
        for cp in sends:
            cp.start()
        for k, (px, py) in enumerate(chips):
            sk = 2 * px + py
            for t in range(nt):
                _remote(pb_refs[t].at[c], gb_refs[t].at[sk, c], send_sems, recv_sems, 6 * t + k, sib).wait_recv()
                fwd = _remote(gb_refs[t].at[sk, c], gb_refs[t].at[sk, c], send_sems, recv_sems, 6 * t + 3 + k, sib)
                fwd.start()
                sends.append(fwd)
        for k, (px, py) in enumerate(chips):
            sk = 2 * px + py
            for t in range(nt):
                _remote(pb_refs[t].at[c], gb_refs[t].at[sk, 1 - c], send_sems, recv_sems, 6 * t + 3 + k, sib).wait_recv()
            _remote(ps_ref, gs_ref.at[sk], send_sems, recv_sems, 6 * nt + k, sib).wait_recv()
        for cp in sends:
            cp.wait_send()
        l1.wait()

    nsem = 6 * nt + 3
    out = pl.pallas_call(
        body, in_specs=[ANY] * (nt + 1), out_specs=[ANY] * (nt + 1),
        out_shape=[SDS((4,) + pb.shape, pb.dtype) for pb in pbs] + [SDS((4,) + ps.shape, ps.dtype)],
        scratch_shapes=[pltpu.SemaphoreType.DMA((nsem,)), pltpu.SemaphoreType.DMA((nsem,)), pltpu.SemaphoreType.DMA((1,))],
        name="gather_weights")(*pbs, ps)
    return out[:nt], out[nt]


def _contain(wpad, shift):
    r, cw = wpad.shape
    tr = _tile16(r, 256)

    def body(n_ref, x_ref, o_ref):
        o_ref[...] = pltpu.roll(x_ref[...], n_ref[0], axis=1).astype(o_ref.dtype)

    spec = pl.BlockSpec((tr, cw), lambda i, n: (i, 0))
    return pl.pallas_call(
        body, grid_spec=pltpu.PrefetchScalarGridSpec(num_scalar_prefetch=1, grid=(r // tr,), in_specs=[spec], out_specs=spec),
        out_shape=SDS((r, cw), BF16), compiler_params=_params(1), name="contain")(shift, wpad)


def _place_own(gb, pb, chip):
    _, _, r, c = gb.shape
    tr = _tile16(r, 1100)

    def body(s_ref, p_ref, g_in, o_ref):
        o_ref[...] = p_ref[...]

    return pl.pallas_call(
        body, grid_spec=pltpu.PrefetchScalarGridSpec(
            num_scalar_prefetch=1, grid=(2, r // tr),
            in_specs=[pl.BlockSpec((None, tr, c), lambda h, i, s: (h, i, 0)), ANY],
            out_specs=pl.BlockSpec((None, None, tr, c), lambda h, i, s: (s[0], h, i, 0))),
        out_shape=SDS(gb.shape, gb.dtype), input_output_aliases={2: 0}, compiler_params=_params(2),
        name="place_own")(chip, pb, gb)


def _sem_scratch(n_remote, n_local):
    return [pltpu.SemaphoreType.DMA((n_remote,)), pltpu.SemaphoreType.DMA((n_remote,)), pltpu.SemaphoreType.DMA((n_local,))]


def _swap_halves(sends):
    nt = len(sends)

    def body(*refs):
        s_refs, got_refs = refs[:nt], refs[nt:2 * nt]
        send_sems, recv_sems = refs[2 * nt:]
        x, y, c, _ = _place()
        sib = (x, y, 1 - c)
        remote = [_remote(s_refs[t].at[1 - c, s], got_refs[t].at[s], send_sems, recv_sems, 4 * t + s, sib)
                  for t in range(nt) for s in range(4)]
        for cp in remote:
            cp.start()
        for cp in remote:
            cp.wait()

    return pl.pallas_call(
        body, in_specs=[ANY] * nt, out_specs=[ANY] * nt, out_shape=[SDS(g.shape[1:], g.dtype) for g in sends],
        scratch_shapes=[pltpu.SemaphoreType.DMA((4 * nt,)), pltpu.SemaphoreType.DMA((4 * nt,))], name="swap_halves")(*sends)


def _scatter_chip_sums(parts):
    nt = len(parts)

    def body(*refs):
        a_refs, r_refs = refs[:nt], refs[nt:2 * nt]
        send_sems, recv_sems = refs[2 * nt:]
        x, y, c, chips = _place()
        s = 2 * x + y
        sends = [_remote(a_refs[t].at[2 * px + py], r_refs[t].at[s], send_sems, recv_sems, 3 * t + k, (px, py, c))
                 for t in range(nt) for k, (px, py) in enumerate(chips)]
        for cp in sends:
            cp.start()
        for t in range(nt):
            for k, (px, py) in enumerate(chips):
                _remote(a_refs[t].at[s], r_refs[t].at[2 * px + py], send_sems, recv_sems, 3 * t + k, (px, py, c)).wait_recv()
        for cp in sends:
            cp.wait_send()

    return pl.pallas_call(
        body, in_specs=[ANY] * nt, out_specs=[ANY] * nt, out_shape=[SDS(a.shape, a.dtype) for a in parts],
        scratch_shapes=[pltpu.SemaphoreType.DMA((3 * nt,)), pltpu.SemaphoreType.DMA((3 * nt,))],
        name="scatter_chip_sums")(*parts)


def _join_halves(fs):
    nt = len(fs)

    def body(*refs):
        f_refs = refs[nt:2 * nt]
        send_sems, recv_sems = refs[2 * nt:]
        x, y, c, _ = _place()
        sib = (x, y, 1 - c)
        sends = [_remote(f_refs[t].at[c], f_refs[t].at[c], send_sems, recv_sems, t, sib) for t in range(nt)]
        for cp in sends:
            cp.start()
        for t in range(nt):
            _remote(f_refs[t].at[c], f_refs[t].at[1 - c], send_sems, recv_sems, t, sib).wait_recv()
        for cp in sends:
            cp.wait_send()

    return pl.pallas_call(
        body, in_specs=[ANY] * nt, out_specs=[ANY] * nt, out_shape=[SDS(f.shape, f.dtype) for f in fs],
        input_output_aliases={t: t for t in range(nt)},
        scratch_shapes=[pltpu.SemaphoreType.DMA((nt,)), pltpu.SemaphoreType.DMA((nt,))], name="join_halves")(*fs)


def _uncontain(cont, n_head, width):
    r, cw = cont.shape
    tr = _tile(r, 256)

    def body(n_ref, x_ref, o_ref):
        o_ref[...] = pltpu.roll(x_ref[...], n_ref[0], axis=1)[:, :width]

    return pl.pallas_call(
        body, grid_spec=pltpu.PrefetchScalarGridSpec(
            num_scalar_prefetch=1, grid=(r // tr,), in_specs=[pl.BlockSpec((tr, cw), lambda i, n: (i, 0))],
            out_specs=pl.BlockSpec((tr, width), lambda i, n: (i, 0))),
        out_shape=SDS((r, width), F32), compiler_params=_params(1), name="uncontain")(n_head, cont)


WEIGHTS = ("meta_tokens", "norm_w", "w_in", "conv_w", "a_log", "dt_bias", "gnorm_a", "gnorm_b", "hgrn_lower_bounds",
           "w_branch_a", "w_branch_b", "w_out", "final_norm_w")
SHARD_AXIS = {"meta_tokens": 1, "w_in": 2, "conv_w": 2, "w_branch_a": 2, "w_branch_b": 2, "w_out": 1}
FLAT_C = 1024


def _flat(parts, rows, cols=FLAT_C):
    v = jnp.concatenate([p.reshape(-1) for p in parts])
    return jnp.pad(v, (0, rows * cols - v.shape[0])).reshape(rows, cols)


def _local_step(x, target, w, lay):
    nb, seq, d = x.shape
    tp = CH + seq
    nc = tp // CH
    n = nb * tp
    e_mat, s_mat = _gate_consts()
    lb_all = _lb_fwd(w["hgrn_lower_bounds"])
    h = jnp.concatenate([jnp.zeros((nb, PAD, d), F32), jnp.broadcast_to(w["meta_tokens"][None], (nb, N_META, d)), x],
                        axis=1).reshape(n, d)
    rep = lambda a: jnp.repeat(a, HD)[None, :]
    saved = []
    for l in range(DEPTH):
        nw = w["norm_w"][l][None, :]
        proj, xn = _norm_proj_fwd(h, nw, w["w_in"][l])
        qkv = _gdn_prep_fwd(proj, w["conv_w"][l], lay, nb, tp)
        alog, dtb = rep(w["a_log"][l]), rep(w["dt_bias"][l])
        oa, sa, sva = _gdn_fwd(qkv, proj, e_mat, alog, dtb, lay, nb, nc)
        lbl = lb_all[l][None, :]
        ob, sb = _hgrn_fwd(proj, lbl, lay, nb, nc)
        ga, gb = w["gnorm_a"][l][None, :], w["gnorm_b"][l][None, :]
        hn = _merge_fwd(h, oa, ob, proj, ga, gb, w["w_branch_a"][l], w["w_branch_b"][l], w["w_out"][l], lay)
        saved.append((h, nw, proj, qkv, alog, dtb, oa, sa, lbl, ob, sb, ga, gb, xn, sva))
        h = hn
    target_p = jnp.pad(target, ((0, 0), (CH, 0), (0, 0))).reshape(n, d)
    lp, dh, dfw = _loss_head(h, target_p, w["final_norm_w"][None, :], nb, tp)
    loss = jnp.sum(lp[::8, 0])
    g = {n_: [None] * DEPTH for n_ in WEIGHTS}
    dlb_all = [None] * DEPTH
    for l in reversed(range(DEPTH)):
        h, nw, proj, qkv, alog, dtb, oa, sa, lbl, ob, sb, ga, gb, xn, sva = saved[l]
        dproj, doa, dob, dwa, dwb, dwo, dga, dgb = _merge_bwd(dh, oa, ob, proj, ga, gb, w["w_branch_a"][l],
                                                             w["w_branch_b"][l], w["w_out"][l], lay, tp)
        dproj, acc_b = _hgrn_bwd(proj, lbl, sb, dob, dproj, lay, nb, nc)
        dqkv, dproj, acc_a = _gdn_bwd(qkv, proj, e_mat, s_mat, alog, dtb, sa, sva, doa, dproj, lay, nb, nc)
        dproj, dconv = _gdn_prep_bwd(proj, w["conv_w"][l], dqkv, dproj, lay, nb, tp)
        dh, dnw = _proj_bwd_dx(dproj, w["w_in"][l], h, nw, dh, tp)
        g["w_in"][l] = _proj_bwd_dw(dproj, xn, tp)
        g["norm_w"][l] = dnw[0]
        g["conv_w"][l] = dconv
        g["a_log"][l] = acc_a[0, ::HD]
        g["dt_bias"][l] = acc_a[1, ::HD]
        g["gnorm_a"][l], g["gnorm_b"][l] = dga[0], dgb[0]
        g["w_branch_a"][l], g["w_branch_b"][l], g["w_out"][l] = dwa, dwb, dwo
        dlb_all[l] = acc_b[0]
    grads = {n_: jnp.stack(v) for n_, v in g.items() if v[0] is not None}
    grads["hgrn_lower_bounds"] = _lb_bwd(w["hgrn_lower_bounds"], jnp.stack(dlb_all))
    grads["final_norm_w"] = dfw[0]
    grads["meta_tokens"] = _meta_grad(dh, nb, nc)
    grad_x = dh.reshape(nb, tp, d)[:, CH:, :]
    return loss, grad_x, grads


def kernel(x, meta_tokens, norm_w, w_in, conv_w, a_log, dt_bias, gnorm_a, gnorm_b, hgrn_lower_bounds, w_branch_a, w_branch_b, w_out, final_norm_w, loss_target, m_meta_tokens, m_norm_w, m_w_in, m_conv_w, m_a_log, m_dt_bias, m_gnorm_a, m_gnorm_b, m_hgrn_lower_bounds, m_w_branch_a, m_w_branch_b, m_w_out, m_final_norm_w, v_meta_tokens, v_norm_w, v_w_in, v_conv_w, v_a_log, v_dt_bias, v_gnorm_a, v_gnorm_b, v_hgrn_lower_bounds, v_w_branch_a, v_w_branch_b, v_w_out, v_final_norm_w):
    wl = dict(meta_tokens=meta_tokens, norm_w=norm_w, w_in=w_in, conv_w=conv_w, a_log=a_log, dt_bias=dt_bias, gnorm_a=gnorm_a,
              gnorm_b=gnorm_b, hgrn_lower_bounds=hgrn_lower_bounds, w_branch_a=w_branch_a, w_branch_b=w_branch_b, w_out=w_out,
              final_norm_w=final_norm_w)
    ml = dict(zip(WEIGHTS, (m_meta_tokens, m_norm_w, m_w_in, m_conv_w, m_a_log, m_dt_bias, m_gnorm_a, m_gnorm_b,
                            m_hgrn_lower_bounds, m_w_branch_a, m_w_branch_b, m_w_out, m_final_norm_w)))
    vl = dict(zip(WEIGHTS, (v_meta_tokens, v_norm_w, v_w_in, v_conv_w, v_a_log, v_dt_bias, v_gnorm_a, v_gnorm_b,
                            v_hgrn_lower_bounds, v_w_branch_a, v_w_branch_b, v_w_out, v_final_norm_w)))
    d = x.shape[2]
    lay = _Layout(d)
    nchip = 4

    big = ("w_in", "w_branch_a", "w_branch_b", "w_out")
    small = ("conv_w", "meta_tokens")
    table, heads, cw = lay.pieces(nchip)
    sw = wl["w_in"].shape[2]
    chip_id = (2 * lax.axis_index("x") + lax.axis_index("y")).astype(jnp.int32)
    n_head = sum(jnp.where(chip_id == s, heads[s], 0) for s in range(nchip)).astype(jnp.int32)
    w_pad = jnp.pad(wl["w_in"], ((0, 0), (0, 0), (0, cw - sw))).reshape(DEPTH * d, cw)
    shift = jnp.where(n_head == 0, 0, cw - n_head).astype(jnp.int32).reshape(1)
    pbs = [_contain(w_pad, shift).reshape(DEPTH, d, cw)] + [wl[n].astype(BF16) for n in big[1:]]
    nsmall = sum(int(np.prod(wl[n].shape)) for n in small)
    rs = -(-nsmall // (HD * 8)) * 8
    ps = jnp.pad(jnp.concatenate([wl[n].reshape(-1) for n in small]), (0, rs * HD - nsmall)).reshape(rs, HD)
    gbig, gsmall = _gather_weights(pbs, ps)
    gbig = [_place_own(g, p, chip_id.reshape(1)) for g, p in zip(gbig, pbs)]
    gsmall = gsmall.reshape(nchip, -1)

    wf = dict(wl)
    wf["w_in"] = lay.from_containers([gbig[0][s] for s in range(nchip)])
    for i, n in enumerate(big[1:], start=1):
        wf[n] = jnp.concatenate([gbig[i][s] for s in range(nchip)], axis=SHARD_AXIS[n])
    o = 0
    for n in small:
        sz = int(np.prod(wl[n].shape))
        a = gsmall[:, o:o + sz].reshape((nchip,) + wl[n].shape)
        wf[n] = jnp.concatenate([a[s] for s in range(nchip)], axis=SHARD_AXIS[n])
        o += sz

    loss_part, grad_x, gfull = _local_step(x, loss_target, wf, lay)
    loss = lax.psum(loss_part, ("x", "y", "c"))

    sw = wl["w_in"].shape[2]
    conts, heads = lay.containers(gfull["w_in"], nchip)
    dd = wl["w_branch_a"].shape[2]
    rows_o = wl["w_out"].shape[1]
    by_dest = lambda g, n: [lax.slice_in_dim(g, s * wl[n].shape[SHARD_AXIS[n]], (s + 1) * wl[n].shape[SHARD_AXIS[n]],
                                            axis=SHARD_AXIS[n]) if n in SHARD_AXIS else g for s in range(nchip)]
    small_names = tuple(n for n in WEIGHTS if n not in big)
    nsm = sum(int(np.prod(wl[n].shape)) for n in small_names)
    rsm = -(-nsm // (2 * HD * 8)) * 8
    pack_small = lambda parts: _flat(parts, 2 * rsm, HD).reshape(2, rsm, HD)
    small_by_dest = [by_dest(gfull[n], n) for n in small_names]
    gs = [jnp.stack(conts, axis=1),
          jnp.stack(by_dest(gfull["w_branch_a"], "w_branch_a"), axis=1),
          jnp.stack(by_dest(gfull["w_branch_b"], "w_branch_b"), axis=1),
          gfull["w_out"].reshape(DEPTH, nchip, rows_o, d),
          jnp.stack([pack_small([p[s] for p in small_by_dest]) for s in range(nchip)], axis=1)]
    gs = [g.reshape((2, nchip, -1, g.shape[-1])) for g in gs]
    my_chip = (2 * lax.axis_index("x") + lax.axis_index("y")).astype(jnp.int32)
    my_core = lax.axis_index("c").astype(jnp.int32)
    got = _swap_halves([g.astype(BF16) for g in gs[:4]] + gs[4:])
    chip_sums = [_add_cores(g, b, my_core.reshape(1)) for g, b in zip(gs, got)]
    by_chip = _scatter_chip_sums(chip_sums)
    full = _join_halves([_sum_chips(p, a, my_chip.reshape(1), my_core.reshape(1)) for p, a in zip(by_chip, chip_sums)])
    n_head = sum(jnp.where(my_chip == s, heads[s], 0) for s in range(nchip)).astype(jnp.int32).reshape(1)
    g_w_in = _uncontain(full[0].reshape(DEPTH * d, -1), n_head, sw)
    g2 = {"w_in": g_w_in, "w_branch_a": full[1].reshape(-1, dd), "w_branch_b": full[2].reshape(-1, dd),
          "w_out": full[3].reshape(-1, d), "small": full[4].reshape(2 * rsm, HD)}

    def two_d(src, n):
        if n == "small":
            return _flat([src[k] for k in small_names], 2 * rsm, HD)
        return src[n].reshape(g2[n].shape)

    outs = {}
    for n in big[1:] + ("small",):
        delta, mnew, vnew = _adamw(g2[n], two_d(wl, n), two_d(ml, n), two_d(vl, n))
        outs[n] = (g2[n], delta, mnew, vnew)
    cols = lambda a: jnp.transpose(a.reshape(wl["w_in"].shape), (2, 0, 1))
    g_cols = cols(g2["w_in"])
    outs["w_in"] = tuple(jnp.transpose(a, (1, 2, 0)) for a in
                         (g_cols,) + tuple(_adamw(g_cols, cols(wl["w_in"]), cols(ml["w_in"]), cols(vl["w_in"]))))
    res = [{}, {}, {}, {}]
    for i in range(4):
        for n in big:
            res[i][n] = outs[n][i].reshape(wl[n].shape)
        v, o = outs["small"][i].reshape(-1), 0
        for n in small_names:
            sz = int(np.prod(wl[n].shape))
            res[i][n] = v[o:o + sz].reshape(wl[n].shape)
            o += sz
    return (loss, grad_x, *[res[0][n] for n in WEIGHTS], *[res[1][n] for n in WEIGHTS], *[res[2][n] for n in WEIGHTS],
            *[res[3][n] for n in WEIGHTS])
```

```python
import functools

import numpy as np
import jax
import jax.numpy as jnp
from jax import lax
from jax.experimental import pallas as pl
from jax.experimental.pallas import tpu as pltpu

F32 = jnp.float32
BF16 = jnp.bfloat16
HI = lax.Precision.HIGHEST
SDS = jax.ShapeDtypeStruct

NH = 4
HD = 128
HW = NH * HD
N_META = 16
CH = 64
SUB = 16
PAD = CH - N_META
EPS = 1e-6
Q_SCALE = HD ** -0.5
DEPTH = 2
CONV_K = 4
PREP_HEADS = 2
VMEM_LIMIT = 56 * 1024 * 1024
ADAM_LR, ADAM_B1, ADAM_B2, ADAM_EPS, ADAM_WD, ADAM_STEP = 0.001, 0.9, 0.999, 1e-08, 0.01, 10
MESH = pl.DeviceIdType.MESH


def _nn(a, b):
    return jnp.dot(a, b, precision=HI, preferred_element_type=F32)


def _nt(a, b):
    return lax.dot_general(a, b, (((1,), (1,)), ((), ())), precision=HI, preferred_element_type=F32)


def _tn(a, b):
    return _nn(a.T, b)


def _scan_rows(x, group, reverse=False):
    n = x.shape[0]
    pos = lax.bitwise_and(_iota2(x.shape, 0), group - 1)
    s = 1
    while s < group:
        if reverse:
            x = x + jnp.where(pos < group - s, pltpu.roll(x, n - s, axis=0), 0.0)
        else:
            x = x + jnp.where(pos >= s, pltpu.roll(x, s, axis=0), 0.0)
        s *= 2
    return x


def _bnn(a, b):
    return jnp.dot(a.astype(BF16), b.astype(BF16), preferred_element_type=F32)


def _bnt(a, b):
    return lax.dot_general(a.astype(BF16), b.astype(BF16), (((1,), (1,)), ((), ())), preferred_element_type=F32)


def _btn(a, b):
    return lax.dot_general(a.astype(BF16), b.astype(BF16), (((0,), (0,)), ((), ())), preferred_element_type=F32)


def _hi_lo(x):
    hi = x.astype(jnp.bfloat16)
    return hi, (x - hi.astype(F32)).astype(jnp.bfloat16)


def _dot3(dims):
    def f(a, b):
        ah, al = _hi_lo(a)
        bh, bl = _hi_lo(b)
        d = lambda p, q: lax.dot_general(p, q, (dims, ((), ())), preferred_element_type=F32)
        return d(ah, bh) + (d(ah, bl) + d(al, bh))
    return f


_rnn, _rnt, _rtn = _dot3(((1,), (0,))), _dot3(((1,), (1,))), _dot3(((0,), (0,)))
_enn, _ent, _etn = _bnn, _bnt, _btn
_hnn, _hnt, _htn = _bnn, _bnt, _btn


def _rr(x):
    return x


def _sig(x):
    return jax.nn.sigmoid(x)


def _silu(x):
    return x * _sig(x)


def _dsilu(x):
    s = _sig(x)
    return s * (1.0 + x * (1.0 - s))


def _softplus(x):
    return jnp.maximum(x, 0.0) + jnp.log(1.0 + jnp.exp(-jnp.abs(x)))


def _logsig(x):
    return jnp.minimum(x, 0.0) - jnp.log(1.0 + jnp.exp(-jnp.abs(x)))


def _rs(x):
    return jnp.sum(x, axis=-1, keepdims=True)


def _params(n_axes):
    return pltpu.CompilerParams(dimension_semantics=("arbitrary",) * n_axes, vmem_limit_bytes=VMEM_LIMIT)


def _tile(n, target, mult=8):
    best = mult
    for t in range(mult, target + 1, mult):
        if n % t == 0:
            best = t
    assert n % best == 0, (n, mult)
    return best


def _ctile(pw, most=7):
    return HD * max(k for k in range(1, most + 1) if (pw // HD) % k == 0)


def _iota2(shape, axis):
    return lax.broadcasted_iota(jnp.int32, shape, axis)


class _Layout:
    def __init__(self, d):
        self.d = d
        self.wm = 2 * HW + 2 * d
        self.c_qkv = self.wm
        self.c_b = self.wm + 3 * HW
        self.c_ba = self.wm + 6 * HW
        self.pw = self.c_ba + HD
        assert self.c_b % (3 * HW) == 0
        o = 0
        segs = {}
        for name, w in (("a_q", HW), ("a_k", HW), ("a_v", HW), ("ba", 2 * NH), ("a_z", HW), ("b_q", HW), ("b_f", HW),
                        ("b_i", HW), ("b_g", HW), ("gate_a", d), ("gate_b", d)):
            segs[name] = (o, o + w)
            o += w
        self.segs = segs
        self.width = o
        self.order = ("a_z", "b_g", "gate_a", "gate_b", "a_q", "a_k", "a_v", "b_q", "b_f", "b_i", "ba")

    def to_kernel(self, w):
        parts = [w[..., self.segs[n][0]:self.segs[n][1]] for n in self.order]
        parts.append(jnp.zeros(w.shape[:-1] + (HD - 2 * NH,), w.dtype))
        return jnp.concatenate(parts, axis=-1)

    def containers(self, g, nchip):
        table, heads, cw = self.pieces(nchip)
        out = []
        for s in range(nchip):
            parts, at = [], 0
            for kcol, w, ccol in sorted(table[s], key=lambda p: p[2]):
                if ccol > at:
                    parts.append(jnp.zeros(g.shape[:-1] + (ccol - at,), g.dtype))
                parts.append(g[..., kcol:kcol + w])
                at = ccol + w
            if at < cw:
                parts.append(jnp.zeros(g.shape[:-1] + (cw - at,), g.dtype))
            out.append(jnp.concatenate(parts, axis=-1))
        return out, heads

    def pieces(self, nchip):
        off, where = 0, {}
        for n in self.order:
            where[n] = off
            off += self.segs[n][1] - self.segs[n][0]
        names = sorted(self.segs, key=lambda n: self.segs[n][0])
        sw = self.width // nchip
        cw = -(-sw // HD) * HD
        table, heads = [], []
        for s in range(nchip):
            lo, hi = s * sw, (s + 1) * sw
            pieces = []
            for n in names:
                a, b = max(lo, self.segs[n][0]), min(hi, self.segs[n][1])
                if a < b:
                    pieces.append((where[n] + a - self.segs[n][0], b - a))
            start, width = pieces[0]
            n_head = min((-start) % HD, width)
            body = ([(start + n_head, width - n_head)] if width > n_head else []) + pieces[1:]
            rows, at = [], 0
            for c, w in body:
                rows.append((c, w, at))
                at += w
            if n_head:
                rows.append((start, n_head, cw - n_head))
            table.append(rows)
            heads.append(n_head)
        return table, heads, cw

    def from_containers(self, conts):
        table, _, _ = self.pieces(len(conts))
        cut = sorted((kcol, w, s, ccol) for s, rows in enumerate(table) for kcol, w, ccol in rows)
        parts, at = [], 0
        for kcol, w, s, ccol in cut:
            assert kcol == at, (kcol, at)
            parts.append(conts[s][..., ccol:ccol + w])
            at = kcol + w
        parts.append(jnp.zeros(conts[0].shape[:-1] + (self.pw - at,), conts[0].dtype))
        return jnp.concatenate(parts, axis=-1)

    def from_kernel(self, g):
        off, where = 0, {}
        for n in self.order:
            w = self.segs[n][1] - self.segs[n][0]
            where[n] = (off, off + w)
            off += w
        names = sorted(self.segs, key=lambda n: self.segs[n][0])
        return jnp.concatenate([g[..., where[n][0]:where[n][1]] for n in names], axis=-1)


def _norm_proj_fwd(h, nw, wp):
    n, d = h.shape
    pw = wp.shape[1]
    tm, tn = _tile(n, 1408, HD), _ctile(pw)

    def body(h_ref, nw_ref, w_ref, o_ref, xt_ref, xn_ref):
        @pl.when(pl.program_id(1) == 0)
        def _():
            x = h_ref[...]
            r = lax.rsqrt(jnp.mean(x * x, axis=-1, keepdims=True) + EPS)
            xn = (x * r * nw_ref[...]).astype(BF16)
            xn_ref[...] = xn
            xt_ref[...] = xn.T

        o_ref[...] = jnp.dot(xn_ref[...], w_ref[...], preferred_element_type=F32)

    return pl.pallas_call(
        body, grid=(n // tm, pw // tn),
        in_specs=[pl.BlockSpec((tm, d), lambda i, j: (i, 0)), pl.BlockSpec((1, d), lambda i, j: (0, 0)),
                  pl.BlockSpec((d, tn), lambda i, j: (0, j))],
        out_specs=[pl.BlockSpec((tm, tn), lambda i, j: (i, j)), pl.BlockSpec((d, tm), lambda i, j: (0, i))],
        out_shape=[SDS((n, pw), F32), SDS((d, n), BF16)], scratch_shapes=[pltpu.VMEM((tm, d), BF16)],
        compiler_params=_params(2), name="norm_proj_fwd")(h, nw, wp)


def _row_valid(tm, tp, base):
    row = base + _iota2((tm, 1), 0)
    return lax.rem(row, tp) >= PAD


def _proj_bwd_dx(dproj, wp, h, nw, dhn, tp):
    n, d = h.shape
    pw = wp.shape[1]
    tm, tk = _tile16(n, 1056), _ctile(pw)
    nk = pw // tk

    def body(dp_ref, w_ref, h_ref, nw_ref, dhn_ref, dh_ref, dnw_ref, acc_ref):
        i, k = pl.program_id(0), pl.program_id(1)

        @pl.when(k == 0)
        def _():
            acc_ref[...] = jnp.zeros_like(acc_ref)

        @pl.when((i == 0) & (k == 0))
        def _():
            dnw_ref[...] = jnp.zeros_like(dnw_ref)

        valid = _row_valid(tm, tp, i * tm)
        dp = jnp.where(valid, dp_ref[...], 0.0)
        acc_ref[...] += _bnt(dp, w_ref[...])

        @pl.when(k == nk - 1)
        def _():
            x = h_ref[...]
            r = lax.rsqrt(jnp.mean(x * x, axis=-1, keepdims=True) + EPS)
            xh = x * r
            dxn = acc_ref[...]
            dnw_ref[...] += jnp.sum(dxn * xh, axis=0, keepdims=True)
            dxh = dxn * nw_ref[...]
            dh_ref[...] = dhn_ref[...] + r * (dxh - xh * jnp.mean(dxh * xh, axis=-1, keepdims=True))

    return pl.pallas_call(
        body, grid=(n // tm, nk),
        in_specs=[pl.BlockSpec((tm, tk), lambda i, k: (i, k)), pl.BlockSpec((d, tk), lambda i, k: (0, k)),
                  pl.BlockSpec((tm, d), lambda i, k: (i, 0)), pl.BlockSpec((1, d), lambda i, k: (0, 0)),
                  pl.BlockSpec((tm, d), lambda i, k: (i, 0))],
        out_specs=[pl.BlockSpec((tm, d), lambda i, k: (i, 0)), pl.BlockSpec((1, d), lambda i, k: (0, 0))],
        out_shape=[SDS((n, d), F32), SDS((1, d), F32)],
        scratch_shapes=[pltpu.VMEM((tm, d), F32)], compiler_params=_params(2), name="proj_bwd_dx")(dproj, wp, h, nw, dhn)


def _proj_bwd_dw(dproj, xt, tp):
    d, n = xt.shape
    pw = dproj.shape[1]
    tm, tn = _tile(n, 1408, HD), _ctile(pw)

    def body(dp_ref, xt_ref, dw_ref):
        i = pl.program_id(1)

        @pl.when(i == 0)
        def _():
            dw_ref[...] = jnp.zeros_like(dw_ref)

        dp = jnp.where(_row_valid(tm, tp, i * tm), dp_ref[...], 0.0)
        dw_ref[...] += jnp.dot(xt_ref[...], dp.astype(BF16), preferred_element_type=F32)

    return pl.pallas_call(
        body, grid=(pw // tn, n // tm),
        in_specs=[pl.BlockSpec((tm, tn), lambda j, i: (i, j)), pl.BlockSpec((d, tm), lambda j, i: (0, i))],
        out_specs=pl.BlockSpec((d, tn), lambda j, i: (0, j)), out_shape=SDS((d, pw), F32),
        compiler_params=_params(2), name="proj_bwd_dw")(dproj, xt)


def _conv_silu(x, w, row):
    c = x * w[CONV_K - 1:CONV_K, :]
    for k in range(1, CONV_K):
        c = c + jnp.where(row >= k, pltpu.roll(x, k, axis=0), 0.0) * w[CONV_K - 1 - k:CONV_K - k, :]
    return c


def _gdn_prep_fwd(proj, conv_w, lay, nb, tp):
    n = proj.shape[0]
    nblk = 3 * NH
    cb = lay.c_qkv // HD
    assert cb % PREP_HEADS == 0 and nblk % PREP_HEADS == 0

    def body(p_ref, w_ref, o_ref):
        row = _iota2((tp, HD), 0)
        for t in range(PREP_HEADS):
            j = pl.program_id(1) * PREP_HEADS + t
            ls = slice(t * HD, (t + 1) * HD)
            c = _conv_silu(p_ref[:, ls], w_ref[:, ls], row)
            s = _silu(c)
            r = lax.rsqrt(_rs(s * s) + EPS)
            scale = jnp.where(j < NH, Q_SCALE, 1.0)
            y = jnp.where(j < 2 * NH, s * r * scale, s)
            o_ref[:, ls] = jnp.where(row >= PAD, y, 0.0)

    wd = PREP_HEADS * HD
    return pl.pallas_call(
        body, grid=(nb, nblk // PREP_HEADS),
        in_specs=[pl.BlockSpec((tp, wd), lambda b, j: (b, cb // PREP_HEADS + j)), pl.BlockSpec((CONV_K, wd), lambda b, j: (0, j))],
        out_specs=pl.BlockSpec((tp, wd), lambda b, j: (b, j)), out_shape=SDS((n, nblk * HD), F32),
        compiler_params=_params(2), name="gdn_prep_fwd")(proj, conv_w)


def _gdn_prep_bwd(proj, conv_w, dqkv, dproj, lay, nb, tp):
    nblk = 3 * NH
    cb = lay.c_qkv // HD

    def body(p_ref, w_ref, dy_ref, dp_in, dp_ref, dw_ref):
        b = pl.program_id(1)
        row = _iota2((tp, HD), 0)
        r4 = _iota2((CONV_K, HD), 0)
        for t in range(PREP_HEADS):
            j = pl.program_id(0) * PREP_HEADS + t
            ls = slice(t * HD, (t + 1) * HD)
            x = p_ref[:, ls]
            w = w_ref[:, ls]
            c = _conv_silu(x, w, row)
            s = _silu(c)
            dy = jnp.where(row >= PAD, dy_ref[:, ls], 0.0)
            r = lax.rsqrt(_rs(s * s) + EPS)
            nh = s * r
            scale = jnp.where(j < NH, Q_SCALE, 1.0)
            ds_n = scale * r * (dy - nh * _rs(dy * nh))
            ds = jnp.where(j < 2 * NH, ds_n, dy)
            dc = ds * _dsilu(c)
            dx = dc * w[CONV_K - 1:CONV_K, :]
            dws = [jnp.sum(dc * x, axis=0, keepdims=True)]
            for k in range(1, CONV_K):
                dx = dx + jnp.where(row < tp - k, pltpu.roll(dc, tp - k, axis=0), 0.0) * w[CONV_K - 1 - k:CONV_K - k, :]
                xs = jnp.where(row >= k, pltpu.roll(x, k, axis=0), 0.0)
                dws.append(jnp.sum(dc * xs, axis=0, keepdims=True))
            dp_ref[:, ls] = dx.astype(dp_ref.dtype)
            dw = jnp.zeros((CONV_K, HD), F32)
            for k in range(CONV_K):
                dw = dw + jnp.where(r4 == CONV_K - 1 - k, dws[k], 0.0)

            @pl.when(b == 0)
            def _():
                dw_ref[:, ls] = dw

            @pl.when(b > 0)
            def _():
                dw_ref[:, ls] += dw

    wd = PREP_HEADS * HD
    return pl.pallas_call(
        body, grid=(nblk // PREP_HEADS, nb),
        in_specs=[pl.BlockSpec((tp, wd), lambda j, b: (b, cb // PREP_HEADS + j)), pl.BlockSpec((CONV_K, wd), lambda j, b: (0, j)),
                  pl.BlockSpec((tp, wd), lambda j, b: (b, j)), pl.BlockSpec(memory_space=pl.ANY)],
        out_specs=[pl.BlockSpec((tp, wd), lambda j, b: (b, cb // PREP_HEADS + j)), pl.BlockSpec((CONV_K, wd), lambda j, b: (0, j))],
        out_shape=[SDS(dproj.shape, dproj.dtype), SDS((CONV_K, nblk * HD), F32)],
        input_output_aliases={3: 0}, compiler_params=_params(2), name="gdn_prep_bwd")(proj, conv_w, dqkv, dproj)


def _gate_consts():
    e = np.zeros((HD, 2 * HW), np.float32)
    s = np.zeros((2 * HW, HD), np.float32)
    for h in range(NH):
        e[h, h * HD:(h + 1) * HD] = 1.0
        e[NH + h, HW + h * HD:HW + (h + 1) * HD] = 1.0
        s[h * HD, h] = 1.0
        s[HW + h * HD, NH + h] = 1.0
    return jnp.asarray(e), jnp.asarray(s)


def _gdn_tri():
    i, j = _iota2((CH, CH), 0), _iota2((CH, CH), 1)
    return i >= j, i > j


def _each(fn, *lists):
    return [fn(*xs) for xs in zip(*lists)]


def _tri_inv(a_list, eye):
    p = [-a for a in a_list]
    t = [eye + x for x in p]
    for _ in range(5):
        p = _each(_rnn, p, p)
        tp_ = _each(_rnn, t, p)
        t = _each(lambda x, y: x + y, t, tp_)
    return t


def _gdn_chunks(args, solved=None):
    causal, strict = _gdn_tri()
    eye = jnp.where(_iota2((CH, CH), 0) == _iota2((CH, CH), 1), 1.0, 0.0)
    q, k, v, beta, g, s0 = (list(t) for t in zip(*args))
    gc = [_scan_rows(x, CH) for x in g]
    dm = [jnp.where(causal, jnp.exp(jnp.where(causal, x[:, :CH] - x[:, :CH].T, 0.0)), 0.0) for x in gc]
    ds = [jnp.where(strict, x, 0.0) for x in dm]
    kb = _each(lambda x, y: x * y, k, beta)
    by_k = _each(_ent, [jnp.concatenate([x, y], axis=0) for x, y in zip(kb, q)], k)
    kk = [x[:CH] for x in by_k]
    qk = [x[CH:] for x in by_k]
    a = _each(lambda x, y: x * y, kk, ds)
    eg = [jnp.exp(x) for x in gc]
    rw = _each(lambda x, y: x * y, kb, eg)
    if solved is None:
        tinv = _tri_inv(a, eye)
        rv = _each(lambda x, y: x * y, v, beta)
        u = _each(_rnn, tinv, rv)
        w = _each(_rnn, tinv, rw)
    else:
        tinv, u, w = (list(t) for t in zip(*solved))
    ws = _each(_enn, w, s0)
    vn = _each(lambda x, y: x - y, u, ws)
    p = _each(lambda x, y: x * y, qk, dm)
    qg = _each(lambda x, y: x * y, q, eg)
    out = []
    for i in range(len(args)):
        gl = gc[i][CH - 1:CH, :]
        ek = jnp.exp(gl - gc[i])
        out.append(dict(gc=gc[i], dm=dm[i], ds=ds[i], kb=kb[i], a=a[i], tinv=tinv[i], eg=eg[i], rw=rw[i], u=u[i], w=w[i],
                        vn=vn[i], p=p[i], qg=qg[i], egl=jnp.exp(gl), ek=ek, kd=k[i] * ek))
    return out


def _gdn_gates(ba, e, alog, dtb):
    raw = _nn(ba, e)
    beta = _sig(raw[:, :HW])
    za = raw[:, HW:] + dtb
    g = -jnp.exp(alog) * _softplus(za)
    return beta, g, za


def _seqs_per_step(nb):
    return 4 if nb % 4 == 0 else (2 if nb % 2 == 0 else 1)


def _gdn_fwd(qkv, proj, e_mat, alog, dtb, lay, nb, nc):
    n = qkv.shape[0]
    tp = n // nb
    cba = lay.c_ba // HD
    gb = _seqs_per_step(nb)

    def body(x_ref, ba_ref, e_ref, al_ref, dt_ref, o_ref, so_ref, sv_ref, s_ref):
        @pl.when(pl.program_id(1) == 0)
        def _():
            s_ref[...] = jnp.zeros_like(s_ref)

        args = []
        for j in range(gb):
            beta, g, _ = _gdn_gates(ba_ref[j], e_ref[...], al_ref[...], dt_ref[...])
            for h in range(NH):
                hs = slice(h * HD, (h + 1) * HD)
                args.append((x_ref[j, :, hs], x_ref[j, :, HW + h * HD:HW + (h + 1) * HD],
                             x_ref[j, :, 2 * HW + h * HD:2 * HW + (h + 1) * HD], beta[:, hs], g[:, hs], s_ref[j, h]))
        cs = _gdn_chunks(args)
        s0s = [a[5] for a in args]
        o1 = _each(lambda c, s0: _enn(c["qg"], s0), cs, s0s)
        o2 = [_enn(c["p"], c["vn"]) for c in cs]
        upd = [_etn(c["kd"], c["vn"]) for c in cs]
        res = [(o1[i] + o2[i], s0s[i] * cs[i]["egl"] + upd[i]) for i in range(len(cs))]
        zero = jnp.zeros((CH, HD - CH), F32)
        for j in range(gb):
            for h in range(NH):
                c = cs[j * NH + h]
                so_ref[j, h] = args[j * NH + h][5]
                sv_ref[j, h] = jnp.concatenate([c["u"], c["w"], c["tinv"], zero], axis=-1)
                s_ref[j, h] = res[j * NH + h][1]
            o_ref[j] = jnp.concatenate([res[j * NH + h][0] for h in range(NH)], axis=-1)

    o, st, sv = pl.pallas_call(
        body, grid=(nb // gb, nc),
        in_specs=[pl.BlockSpec((gb, CH, 3 * HW), lambda b, c: (b, c, 0)), pl.BlockSpec((gb, CH, HD), lambda b, c: (b, c, cba)),
                  pl.BlockSpec((HD, 2 * HW), lambda b, c: (0, 0)), pl.BlockSpec((1, HW), lambda b, c: (0, 0)),
                  pl.BlockSpec((1, HW), lambda b, c: (0, 0))],
        out_specs=[pl.BlockSpec((gb, CH, HW), lambda b, c: (b, c, 0)),
                   pl.BlockSpec((gb, None, NH, HD, HD), lambda b, c: (b, c, 0, 0, 0)),
                   pl.BlockSpec((gb, None, NH, CH, 3 * HD), lambda b, c: (b, c, 0, 0, 0))],
        out_shape=[SDS((nb, tp, HW), F32), SDS((nb, nc, NH, HD, HD), F32), SDS((nb, nc, NH, CH, 3 * HD), F32)],
        scratch_shapes=[pltpu.VMEM((gb, NH, HD, HD), F32)], compiler_params=_params(2), name="gdn_fwd")(
            qkv.reshape(nb, tp, 3 * HW), proj.reshape(nb, tp, -1), e_mat, alog, dtb)
    return o.reshape(n, HW), st, sv


def _gdn_bwd(qkv, proj, e_mat, s_mat, alog, dtb, states, solved, do, dproj, lay, nb, nc):
    n = qkv.shape[0]
    tp = n // nb
    cba = lay.c_ba // HD
    gb = _seqs_per_step(nb)

    def body(x_ref, ba_ref, e_ref, sm_ref, al_ref, dt_ref, st_ref, sv_ref, do_ref, dp_in, dx_ref, dba_ref, acc_ref, ds_ref):
        ci = pl.program_id(1)

        @pl.when(ci == 0)
        def _():
            ds_ref[...] = jnp.zeros_like(ds_ref)

        @pl.when((ci == 0) & (pl.program_id(0) == 0))
        def _():
            acc_ref[...] = jnp.zeros_like(acc_ref)

        causal, strict = _gdn_tri()
        alog = al_ref[...]
        row = _iota2((CH, 1), 0)
        valid = (row >= PAD) | (ci < nc - 1)
        last = row == CH - 1
        gates = [_gdn_gates(ba_ref[j], e_ref[...], alog, dt_ref[...]) for j in range(gb)]
        args, do, ds1, solved = [], [], [], []
        for j in range(gb):
            beta, g, _ = gates[j]
            for h in range(NH):
                hs = slice(h * HD, (h + 1) * HD)
                args.append((x_ref[j, :, hs], x_ref[j, :, HW + h * HD:HW + (h + 1) * HD],
                             x_ref[j, :, 2 * HW + h * HD:2 * HW + (h + 1) * HD], beta[:, hs], g[:, hs], st_ref[j, h]))
                do.append(do_ref[j, :, hs])
                ds1.append(ds_ref[j, h])
                solved.append((sv_ref[j, h, :, 2 * HD:2 * HD + CH], sv_ref[j, h, :, 0:HD], sv_ref[j, h, :, HD:2 * HD]))
        q, k, v, bh, _, s0 = (list(t) for t in zip(*args))
        cs = _gdn_chunks(args, solved)
        get = lambda name: [c[name] for c in cs]
        mul = lambda x, y: x * y
        add = lambda x, y: x + y
        dvn = _each(add, _each(_etn, get("p"), do), _each(_enn, get("kd"), ds1))
        by_s0 = _each(_ent, [jnp.concatenate([x, y], axis=0) for x, y in zip(do, dvn)], s0)
        dqg = [x[:CH] for x in by_s0]
        dw = [-x[CH:] for x in by_s0]
        dp = [jnp.where(causal, x, 0.0) for x in _each(_ent, do, get("vn"))]
        dkd = _each(_ent, get("vn"), ds1)
        ds_a = _each(_etn, get("qg"), do)
        ds_b = _each(_etn, get("w"), dvn)
        ds_new = [ds_a[i] - ds_b[i] + ds1[i] * cs[i]["egl"] for i in range(len(cs))]
        drvw = _each(_rtn, get("tinv"), [jnp.concatenate([x, y], axis=-1) for x, y in zip(dvn, dw)])
        drv = [x[:, :HD] for x in drvw]
        drw = [x[:, HD:] for x in drvw]
        uw = [sv_ref[j, h, :, 0:2 * HD] for j in range(gb) for h in range(NH)]
        da = [jnp.where(strict, -x, 0.0) for x in _each(_rnt, drvw, uw)]
        m = [da[i] * cs[i]["a"] + dp[i] * cs[i]["p"] for i in range(len(cs))]
        dkk = _each(mul, da, get("ds"))
        dqk = _each(mul, dp, get("dm"))
        by_k = _each(_enn, [jnp.concatenate([x, y], axis=0) for x, y in zip(dqk, dkk)], k)
        dq = _each(add, [x[:CH] for x in by_k], _each(mul, dqg, get("eg")))
        dkb = _each(add, [x[CH:] for x in by_k], _each(mul, drw, get("eg")))
        dk_1 = _each(_etn, dqk, q)
        dk_2 = _each(_etn, dkk, get("kb"))
        dk = [dk_1[i] + dk_2[i] + dkd[i] * cs[i]["ek"] + dkb[i] * bh[i] for i in range(len(cs))]
        dv = _each(mul, drv, bh)
        dbeta, dg = [], []
        for i, c in enumerate(cs):
            dbeta.append(_rs(drv[i] * v[i]) + _rs(dkb[i] * k[i]) + jnp.zeros((CH, HD), F32))
            t_kd = _rs(dkd[i] * c["kd"])
            dgc = _rs(m[i]) - _rs(m[i].T) + _rs(dqg[i] * c["qg"]) + _rs(drw[i] * c["rw"]) - t_kd
            tail = jnp.sum(t_kd, axis=0, keepdims=True) + c["egl"] * jnp.sum(_rs(s0[i] * ds1[i]), axis=0, keepdims=True)
            dgc = dgc + jnp.where(last, tail, 0.0)
            dg.append(_scan_rows(dgc + jnp.zeros((CH, HD), F32), CH, reverse=True))
        r8 = _iota2((8, HW), 0)
        upd = jnp.zeros((8, HW), F32)
        for j in range(gb):
            sl = slice(j * NH, (j + 1) * NH)
            beta, g, za = gates[j]
            for h in range(NH):
                ds_ref[j, h] = ds_new[j * NH + h]
            dx_ref[j] = jnp.concatenate(dq[sl] + dk[sl] + dv[sl], axis=-1)
            dbeta_j = jnp.where(valid, jnp.concatenate(dbeta[sl], axis=-1), 0.0)
            dg_j = jnp.where(valid, jnp.concatenate(dg[sl], axis=-1), 0.0)
            draw_b = dbeta_j * beta * (1.0 - beta)
            draw_a = dg_j * (-jnp.exp(alog)) * _sig(za)
            dba_ref[j] = _nn(jnp.concatenate([draw_b, draw_a], axis=-1), sm_ref[...]).astype(dba_ref.dtype)
            upd = upd + jnp.where(r8 == 0, jnp.sum(dg_j * g, axis=0, keepdims=True), 0.0) + jnp.where(
                r8 == 1, jnp.sum(draw_a, axis=0, keepdims=True), 0.0)
        acc_ref[...] += upd

    rc = lambda c: nc - 1 - c
    dqkv, dproj3, acc = pl.pallas_call(
        body, grid=(nb // gb, nc),
        in_specs=[pl.BlockSpec((gb, CH, 3 * HW), lambda b, c: (b, rc(c), 0)), pl.BlockSpec((gb, CH, HD), lambda b, c: (b, rc(c), cba)),
                  pl.BlockSpec((HD, 2 * HW), lambda b, c: (0, 0)), pl.BlockSpec((2 * HW, HD), lambda b, c: (0, 0)),
                  pl.BlockSpec((1, HW), lambda b, c: (0, 0)), pl.BlockSpec((1, HW), lambda b, c: (0, 0)),
                  pl.BlockSpec((gb, None, NH, HD, HD), lambda b, c: (b, rc(c), 0, 0, 0)),
                  pl.BlockSpec((gb, None, NH, CH, 3 * HD), lambda b, c: (b, rc(c), 0, 0, 0)),
                  pl.BlockSpec((gb, CH, HW), lambda b, c: (b, rc(c), 0)), pl.BlockSpec(memory_space=pl.ANY)],
        out_specs=[pl.BlockSpec((gb, CH, 3 * HW), lambda b, c: (b, rc(c), 0)), pl.BlockSpec((gb, CH, HD), lambda b, c: (b, rc(c), cba)),
                   pl.BlockSpec((8, HW), lambda b, c: (0, 0))],
        out_shape=[SDS((nb, tp, 3 * HW), F32), SDS((nb, tp, dproj.shape[1]), dproj.dtype), SDS((8, HW), F32)],
        input_output_aliases={9: 1},
        scratch_shapes=[pltpu.VMEM((gb, NH, HD, HD), F32)], compiler_params=_params(2), name="gdn_bwd")(
            qkv.reshape(nb, tp, 3 * HW), proj.reshape(nb, tp, -1), e_mat, s_mat, alog, dtb, states, solved, do.reshape(nb, tp, HW),
            dproj.reshape(nb, tp, -1))
    return dqkv.reshape(n, 3 * HW), dproj3.reshape(dproj.shape), acc


def _hgrn_inputs(zq, zf, lb):
    sg = _sig(zf)
    sgn = _sig(-zf)
    pos = lb > 0.0
    lbp = jnp.where(pos, lb, 0.0)
    fpos = lbp + (1.0 - lbp) * sg
    lf = jnp.where(pos, jnp.log(jnp.where(pos, fpos, 1.0)), _logsig(zf))
    k = (1.0 - lbp) * sgn
    q = _silu(zq) * Q_SCALE
    return q, k, lf, sg, sgn, pos, lbp, fpos


def _hgrn_consts():
    i3, j3 = _iota2((SUB, SUB, HD), 0), _iota2((SUB, SUB, HD), 1)
    return i3 >= j3


def _sum_j(x):
    return jnp.sum(x.reshape(SUB, SUB, HD), axis=1)


def _sum_i(x):
    return jnp.sum(x.reshape(SUB, SUB, HD), axis=0)


def _pairs(a, b):
    return (a[:, None, :] * b[None, :, :]).reshape(SUB * SUB, HD)


def _hgrn_sub(q, k, v, bc, st, consts):
    mask3 = consts
    bl = bc[SUB - 1:SUB, :]
    p3 = jnp.where(mask3, jnp.exp(jnp.where(mask3, bc[:, None, :] - bc[None, :, :], 0.0)), 0.0).reshape(SUB * SUB, HD)
    x = _pairs(q, k) * p3
    srep = _rs(x)
    vt = jnp.broadcast_to(v[None, :, :], (SUB, SUB, HD)).reshape(SUB * SUB, HD)
    eb = jnp.exp(bc)
    qe = q * eb
    o = _hnt(qe, st) + _sum_j(_rr(srep) * _rr(vt))
    ek = jnp.exp(bl - bc)
    kd = k * ek
    ebl = jnp.exp(bl)
    st1 = st * ebl + _htn(v, kd)
    return o, st1, dict(bc=bc, p3=p3, srep=srep, vt=vt, eb=eb, qe=qe, ek=ek, kd=kd, ebl=ebl)


def _hgrn_fwd(proj, lb, lay, nb, nc):
    n = proj.shape[0]
    tp = n // nb
    cbb = lay.c_b // (3 * HW)
    gb = _seqs_per_step(nb)

    def body(z_ref, lb_ref, o_ref, so_ref, s_ref):
        @pl.when(pl.program_id(1) == 0)
        def _():
            s_ref[...] = jnp.zeros_like(s_ref)

        consts = _hgrn_consts()
        for j in range(gb):
            outs = []
            for h in range(NH):
                hs = slice(h * HD, (h + 1) * HD)
                q, k, lf = _hgrn_inputs(z_ref[j, :, hs], z_ref[j, :, HW + h * HD:HW + (h + 1) * HD], lb_ref[:, hs])[:3]
                v = z_ref[j, :, 2 * HW + h * HD:2 * HW + (h + 1) * HD]
                st = s_ref[j, h]
                so_ref[j, h] = st
                bc = _scan_rows(lf, SUB)
                oh = []
                for s in range(CH // SUB):
                    rs = slice(s * SUB, (s + 1) * SUB)
                    o, st, _ = _hgrn_sub(q[rs], k[rs], v[rs], bc[rs], st, consts)
                    oh.append(o)
                s_ref[j, h] = st
                outs.append(jnp.concatenate(oh, axis=0))
            o_ref[j] = jnp.concatenate(outs, axis=-1)

    o, st = pl.pallas_call(
        body, grid=(nb // gb, nc),
        in_specs=[pl.BlockSpec((gb, CH, 3 * HW), lambda b, c: (b, c, cbb)), pl.BlockSpec((1, HW), lambda b, c: (0, 0))],
        out_specs=[pl.BlockSpec((gb, CH, HW), lambda b, c: (b, c, 0)),
                   pl.BlockSpec((gb, None, NH, HD, HD), lambda b, c: (b, c, 0, 0, 0))],
        out_shape=[SDS((nb, tp, HW), F32), SDS((nb, nc, NH, HD, HD), F32)],
        scratch_shapes=[pltpu.VMEM((gb, NH, HD, HD), F32)], compiler_params=_params(2), name="hgrn_fwd")(
            proj.reshape(nb, tp, -1), lb)
    return o.reshape(n, HW), st


def _hgrn_bwd(proj, lb, states, do, dproj, lay, nb, nc):
    n = proj.shape[0]
    tp = n // nb
    cbb = lay.c_b // (3 * HW)
    nsub = CH // SUB
    gb = _seqs_per_step(nb)

    def body(z_ref, lb_ref, st_ref, do_ref, dp_in, dz_ref, acc_ref, ds_ref):
        ci = pl.program_id(1)

        @pl.when(ci == 0)
        def _():
            ds_ref[...] = jnp.zeros_like(ds_ref)

        @pl.when((ci == 0) & (pl.program_id(0) == 0))
        def _():
            acc_ref[...] = jnp.zeros_like(acc_ref)

        upd = jnp.zeros((8, HW), F32)
        for j in range(gb):
            upd = upd + one_seq(j, ci, z_ref, lb_ref, st_ref, do_ref, dz_ref, ds_ref)
        acc_ref[...] += upd

    def one_seq(j, ci, z_ref, lb_ref, st_ref, do_ref, dz_ref, ds_ref):
        consts = _hgrn_consts()
        row = _iota2((CH, 1), 0)
        valid = (row >= PAD) | (ci < nc - 1)
        lastrow = _iota2((SUB, 1), 0) == SUB - 1
        dzq, dzf, dzi, dlbs = [], [], [], []
        for h in range(NH):
            hs = slice(h * HD, (h + 1) * HD)
            zq, zf = z_ref[j, :, hs], z_ref[j, :, HW + h * HD:HW + (h + 1) * HD]
            q, k, lf, sg, sgn, pos, lbp, fpos = _hgrn_inputs(zq, zf, lb_ref[:, hs])
            v = z_ref[j, :, 2 * HW + h * HD:2 * HW + (h + 1) * HD]
            doh = do_ref[j, :, hs]
            sts, fw = [st_ref[j, h]], []
            bc = _scan_rows(lf, SUB)
            for s in range(nsub):
                rs = slice(s * SUB, (s + 1) * SUB)
                _, st1, c = _hgrn_sub(q[rs], k[rs], v[rs], bc[rs], sts[-1], consts)
                sts.append(st1)
                fw.append(c)
            dst = ds_ref[j, h]
            dq_l, dk_l, dv_l, dlf_l = [None] * nsub, [None] * nsub, [None] * nsub, [None] * nsub
            for s in reversed(range(nsub)):
                rs = slice(s * SUB, (s + 1) * SUB)
                c, st = fw[s], sts[s]
                qs, ks, vs, dos = q[rs], k[rs], v[rs], doh[rs]
                dqe = _hnn(dos, st)
                dkd = _hnn(vs, dst)
                dsrep = _rs(_pairs(_rr(dos), _rr(vs)))
                w = dsrep * c["p3"]
                kt = jnp.broadcast_to(ks[None, :, :], (SUB, SUB, HD)).reshape(SUB * SUB, HD)
                qt = jnp.broadcast_to(qs[:, None, :], (SUB, SUB, HD)).reshape(SUB * SUB, HD)
                dq_i = _sum_j(w * kt)
                dk_i = _sum_i(w * qt)
                dot = jnp.broadcast_to(_rr(dos)[:, None, :], (SUB, SUB, HD)).reshape(SUB * SUB, HD)
                dvv = _sum_i(_rr(c["srep"]) * dot) + _hnt(c["kd"], dst)
                t_kd = dkd * c["kd"]
                dbc = dqe * c["qe"] - t_kd + qs * dq_i - ks * dk_i
                tail = jnp.sum(t_kd, axis=0, keepdims=True) + c["ebl"] * jnp.sum(st * dst, axis=0, keepdims=True)
                dbc = dbc + jnp.where(lastrow, tail, 0.0)
                dlf_l[s] = dbc
                dq_l[s] = dq_i + dqe * c["eb"]
                dk_l[s] = dk_i + dkd * c["ek"]
                dv_l[s] = dvv
                dst = _htn(dos, c["qe"]) + dst * c["ebl"]
            ds_ref[j, h] = dst
            dq, dk, dv, dbc = (jnp.concatenate(t, axis=0) for t in (dq_l, dk_l, dv_l, dlf_l))
            dlf = _scan_rows(dbc, SUB, reverse=True)
            dlft = dlf - dk * (1.0 - k)
            dlf_dz = jnp.where(pos, (1.0 - lbp) * sg * sgn / jnp.where(pos, fpos, 1.0), sgn)
            dlf_dlb = jnp.where(pos, sgn / jnp.where(pos, fpos, 1.0), 0.0)
            dzq.append(dq * Q_SCALE * _dsilu(zq))
            dzf.append(dlft * dlf_dz)
            dzi.append(dv)
            dlbs.append(jnp.sum(jnp.where(valid, dlft * dlf_dlb, 0.0), axis=0, keepdims=True))
        dz_ref[j] = jnp.concatenate(dzq + dzf + dzi, axis=-1).astype(dz_ref.dtype)
        return jnp.where(_iota2((8, HW), 0) == 0, jnp.concatenate(dlbs, axis=-1), 0.0)

    rc = lambda c: nc - 1 - c
    dproj3, acc = pl.pallas_call(
        body, grid=(nb // gb, nc),
        in_specs=[pl.BlockSpec((gb, CH, 3 * HW), lambda b, c: (b, rc(c), cbb)), pl.BlockSpec((1, HW), lambda b, c: (0, 0)),
                  pl.BlockSpec((gb, None, NH, HD, HD), lambda b, c: (b, rc(c), 0, 0, 0)),
                  pl.BlockSpec((gb, CH, HW), lambda b, c: (b, rc(c), 0)), pl.BlockSpec(memory_space=pl.ANY)],
        out_specs=[pl.BlockSpec((gb, CH, 3 * HW), lambda b, c: (b, rc(c), cbb)), pl.BlockSpec((8, HW), lambda b, c: (0, 0))],
        out_shape=[SDS((nb, tp, dproj.shape[1]), dproj.dtype), SDS((8, HW), F32)],
        input_output_aliases={4: 0},
        scratch_shapes=[pltpu.VMEM((gb, NH, HD, HD), F32)], compiler_params=_params(2), name="hgrn_bwd")(
            proj.reshape(nb, tp, -1), lb, states, do.reshape(nb, tp, HW), dproj.reshape(nb, tp, -1))
    return dproj3.reshape(dproj.shape), acc


def _gated_norm(o, z, gamma):
    ys, ns, rs = [], [], []
    for h in range(NH):
        hs = slice(h * HD, (h + 1) * HD)
        oh = o[:, hs]
        r = lax.rsqrt(jnp.mean(oh * oh, axis=-1, keepdims=True) + EPS)
        nh = oh * r
        ys.append(nh * gamma * _silu(z[:, hs]))
        ns.append(nh)
        rs.append(r)
    return jnp.concatenate(ys, axis=-1), ns, rs


def _merge_fwd(h, oa, ob, proj, ga, gb, wa, wb, wo, lay):
    n, d = h.shape
    tm = _tile(n, 384)
    wm = lay.wm

    def body(h_ref, oa_ref, ob_ref, p_ref, ga_ref, gb_ref, wa_ref, wb_ref, wo_ref, out_ref):
        ya, _, _ = _gated_norm(oa_ref[...], p_ref[:, 0:HW], ga_ref[...])
        yb, _, _ = _gated_norm(ob_ref[...], p_ref[:, HW:2 * HW], gb_ref[...])
        ya2 = _bnn(ya, wa_ref[...])
        yb2 = _bnn(yb, wb_ref[...])
        mixed = _sig(p_ref[:, 2 * HW:2 * HW + d]) * ya2 + _sig(p_ref[:, 2 * HW + d:2 * HW + 2 * d]) * yb2
        out_ref[...] = h_ref[...] + _bnn(mixed, wo_ref[...])

    full = lambda shape: pl.BlockSpec(shape, lambda i: (0, 0))
    return pl.pallas_call(
        body, grid=(n // tm,),
        in_specs=[pl.BlockSpec((tm, d), lambda i: (i, 0)), pl.BlockSpec((tm, HW), lambda i: (i, 0)),
                  pl.BlockSpec((tm, HW), lambda i: (i, 0)), pl.BlockSpec((tm, wm), lambda i: (i, 0)),
                  full((1, HD)), full((1, HD)), full((HW, d)), full((HW, d)), full((d, d))],
        out_specs=pl.BlockSpec((tm, d), lambda i: (i, 0)), out_shape=SDS((n, d), F32),
        compiler_params=_params(1), name="merge_fwd")(h, oa, ob, proj, ga, gb, wa, wb, wo)


def _gated_norm_bwd(dy, o, z, gamma):
    dos, dzs = [], []
    dgam = jnp.zeros((1, HD), F32)
    for h in range(NH):
        hs = slice(h * HD, (h + 1) * HD)
        oh, zh, dyh = o[:, hs], z[:, hs], dy[:, hs]
        r = lax.rsqrt(jnp.mean(oh * oh, axis=-1, keepdims=True) + EPS)
        nh = oh * r
        dzs.append(dyh * nh * gamma * _dsilu(zh))
        dng = dyh * _silu(zh)
        dgam = dgam + jnp.sum(dng * nh, axis=0, keepdims=True)
        dn = dng * gamma
        dos.append(r * (dn - nh * jnp.mean(dn * nh, axis=-1, keepdims=True)))
    return jnp.concatenate(dos, axis=-1), jnp.concatenate(dzs, axis=-1), dgam


def _merge_bwd(dhn, oa, ob, proj, ga, gb, wa, wb, wo, lay, tp):
    n, d = dhn.shape
    tm = _tile(n, 256)
    wm = lay.wm

    def body(dh_ref, oa_ref, ob_ref, p_ref, ga_ref, gb_ref, wa_ref, wb_ref, wo_ref,
             dp_ref, doa_ref, dob_ref, dwa_ref, dwb_ref, dwo_ref, dga_ref, dgb_ref):
        i = pl.program_id(0)

        @pl.when(i == 0)
        def _():
            for r in (dwa_ref, dwb_ref, dwo_ref, dga_ref, dgb_ref):
                r[...] = jnp.zeros_like(r)

        dh = jnp.where(_row_valid(tm, tp, i * tm), dh_ref[...], 0.0)
        oa, ob = oa_ref[...], ob_ref[...]
        za, zb = p_ref[:, 0:HW], p_ref[:, HW:2 * HW]
        gta, gtb = p_ref[:, 2 * HW:2 * HW + d], p_ref[:, 2 * HW + d:2 * HW + 2 * d]
        ya, _, _ = _gated_norm(oa, za, ga_ref[...])
        yb, _, _ = _gated_norm(ob, zb, gb_ref[...])
        ya2 = _bnn(ya, wa_ref[...])
        yb2 = _bnn(yb, wb_ref[...])
        sa, sb = _sig(gta), _sig(gtb)
        mixed = sa * ya2 + sb * yb2
        dmixed = _bnt(dh, wo_ref[...])
        dwo_ref[...] += _btn(mixed, dh)
        dya2 = dmixed * sa
        dyb2 = dmixed * sb
        dwa_ref[...] += _btn(ya, dya2)
        dwb_ref[...] += _btn(yb, dyb2)
        doa, dza, dga = _gated_norm_bwd(_bnt(dya2, wa_ref[...]), oa, za, ga_ref[...])
        dob, dzb, dgb = _gated_norm_bwd(_bnt(dyb2, wb_ref[...]), ob, zb, gb_ref[...])
        dga_ref[...] += dga
        dgb_ref[...] += dgb
        doa_ref[...] = doa
        dob_ref[...] = dob
        dt = dp_ref.dtype
        dp_ref[:, 0:HW] = dza.astype(dt)
        dp_ref[:, HW:2 * HW] = dzb.astype(dt)
        dp_ref[:, 2 * HW:2 * HW + d] = (dmixed * ya2 * sa * (1.0 - sa)).astype(dt)
        dp_ref[:, 2 * HW + d:2 * HW + 2 * d] = (dmixed * yb2 * sb * (1.0 - sb)).astype(dt)

    full = lambda shape: pl.BlockSpec(shape, lambda i: (0, 0))
    rows = lambda w: pl.BlockSpec((tm, w), lambda i: (i, 0))
    return pl.pallas_call(
        body, grid=(n // tm,),
        in_specs=[rows(d), rows(HW), rows(HW), rows(wm), full((1, HD)), full((1, HD)), full((HW, d)), full((HW, d)), full((d, d))],
        out_specs=[rows(wm), rows(HW), rows(HW), full((HW, d)), full((HW, d)), full((d, d)), full((1, HD)), full((1, HD))],
        out_shape=[SDS((n, lay.pw), BF16), SDS((n, HW), F32), SDS((n, HW), F32), SDS((HW, d), F32), SDS((HW, d), F32),
                   SDS((d, d), F32), SDS((1, HD), F32), SDS((1, HD), F32)],
        compiler_params=_params(1), name="merge_bwd")(dhn, oa, ob, proj, ga, gb, wa, wb, wo)


def _loss_head(h, target, fw, nb, tp):
    n, d = h.shape
    tr = _tile(tp, 768)
    nr = tp // tr

    def body(h_ref, t_ref, fw_ref, lp_ref, dh_ref, dfw_ref):
        b, i = pl.program_id(0), pl.program_id(1)

        @pl.when((b == 0) & (i == 0))
        def _():
            dfw_ref[...] = jnp.zeros_like(dfw_ref)

        x = h_ref[...]
        r = lax.rsqrt(jnp.mean(x * x, axis=-1, keepdims=True) + EPS)
        xh = x * r
        live = i * tr + _iota2((tr, 1), 0) >= CH
        err = jnp.where(live, xh * fw_ref[...] - t_ref[...], 0.0)
        lp_ref[...] = jnp.zeros_like(lp_ref) + 0.5 * jnp.sum(_rs(err * err), axis=0, keepdims=True) / d
        dy = err / d
        dfw_ref[...] += jnp.sum(dy * xh, axis=0, keepdims=True)
        dxh = dy * fw_ref[...]
        dh_ref[...] = r * (dxh - xh * jnp.mean(dxh * xh, axis=-1, keepdims=True))

    rows = pl.BlockSpec((tr, d), lambda b, i: (b * nr + i, 0))
    return pl.pallas_call(
        body, grid=(nb, nr), in_specs=[rows, rows, pl.BlockSpec((1, d), lambda b, i: (0, 0))],
        out_specs=[pl.BlockSpec((8, HD), lambda b, i: (b * nr + i, 0)), rows, pl.BlockSpec((1, d), lambda b, i: (0, 0))],
        out_shape=[SDS((nb * nr * 8, HD), F32), SDS((n, d), F32), SDS((1, d), F32)],
        compiler_params=_params(2), name="loss_head")(h, target, fw)


def _lb_fwd(lb):
    def body(x_ref, o_ref):
        x = x_ref[...]
        mx = jnp.max(x, axis=0, keepdims=True)
        e = jnp.exp(x - mx)
        sm = e / jnp.sum(e, axis=0, keepdims=True)
        run = jnp.zeros((1, HW), F32)
        for l in range(DEPTH):
            run = run + sm[l:l + 1, :]
            o_ref[l:l + 1, :] = run - sm[0:1, :]

    return pl.pallas_call(body, out_shape=SDS(lb.shape, F32), name="lb_fwd")(lb)


def _lb_bwd(lb, dlb_all):
    def body(x_ref, d_ref, o_ref):
        x = x_ref[...]
        dl = d_ref[...]
        mx = jnp.max(x, axis=0, keepdims=True)
        e = jnp.exp(x - mx)
        sm = e / jnp.sum(e, axis=0, keepdims=True)
        tot = jnp.sum(dl, axis=0, keepdims=True)
        dsm = []
        run = tot
        for l in range(DEPTH):
            dsm.append(run - (tot if l == 0 else 0.0))
            run = run - dl[l:l + 1, :]
        inner = sum(sm[l:l + 1, :] * dsm[l] for l in range(DEPTH))
        for l in range(DEPTH):
            o_ref[l:l + 1, :] = sm[l:l + 1, :] * (dsm[l] - inner)

    return pl.pallas_call(body, out_shape=SDS(lb.shape, F32), name="lb_bwd")(lb, dlb_all)


def _adamw(g, w, m, v):
    r, c = g.shape[-2:]
    tr = _tile(r, 264) if g.ndim == 2 else None
    c1 = 1.0 / (1.0 - ADAM_B1 ** ADAM_STEP)
    c2 = 1.0 / (1.0 - ADAM_B2 ** ADAM_STEP)

    def body(g_ref, w_ref, m_ref, v_ref, d_ref, mo_ref, vo_ref):
        gg = g_ref[...]
        mn = ADAM_B1 * m_ref[...] + (1.0 - ADAM_B1) * gg
        vn = ADAM_B2 * v_ref[...] + (1.0 - ADAM_B2) * gg * gg
        d_ref[...] = -ADAM_LR * ((mn * c1) / (jnp.sqrt(vn * c2) + ADAM_EPS) + ADAM_WD * w_ref[...])
        mo_ref[...] = mn
        vo_ref[...] = vn

    if g.ndim == 3:
        spec = pl.BlockSpec(g.shape[:2] + (HD,), lambda i: (0, 0, i))
        steps = g.shape[2] // HD
    else:
        spec = pl.BlockSpec((tr, c), lambda i: (i, 0))
        steps = r // tr
    return pl.pallas_call(body, grid=(steps,), in_specs=[spec] * 4, out_specs=[spec] * 3, out_shape=[SDS(g.shape, F32)] * 3,
                          compiler_params=_params(1), name="adamw")(g, w, m, v)


def _tile16(n, target):
    return _tile(n // 2, target // 2) * 2 if n % 16 == 0 else _tile(n, target)


def _add_cores(g, got, core):
    k, r, c = got.shape
    tr = _tile16(r, 264)

    def body(c_ref, a_ref, b_ref, o_ref):
        o_ref[...] = (a_ref[...] + b_ref[...].astype(F32)).astype(o_ref.dtype)

    spec = pl.BlockSpec((None, tr, c), lambda s, i, cr: (s, i, 0))
    if isinstance(g, (list, tuple)):
        def body_k(c_ref, *refs):
            for s in range(k):
                refs[k + 1][s] = (refs[s][...] + refs[k][s].astype(F32)).astype(refs[k + 1].dtype)

        whole = pl.BlockSpec((k, tr, c), lambda i, cr: (0, i, 0))
        return pl.pallas_call(
            body_k, grid_spec=pltpu.PrefetchScalarGridSpec(
                num_scalar_prefetch=1, grid=(r // tr,),
                in_specs=[pl.BlockSpec((None, tr, c), lambda i, cr: (cr[0], i, 0))] * k + [whole], out_specs=whole),
            out_shape=SDS(got.shape, got.dtype), compiler_params=_params(1), name="add_cores")(core, *g, got)
    return pl.pallas_call(
        body, grid_spec=pltpu.PrefetchScalarGridSpec(
            num_scalar_prefetch=1, grid=(k, r // tr),
            in_specs=[pl.BlockSpec((None, None, tr, c), lambda s, i, cr: (cr[0], s, i, 0)), spec], out_specs=spec),
        out_shape=SDS(got.shape, got.dtype), compiler_params=_params(2), name="add_cores")(core, g, got)


def _sum_chips(parts, own, chip, core):
    k, r, c = parts.shape
    tr = _tile16(r, 264)

    def body(chip_ref, core_ref, *refs):
        part_refs, own_ref, o_ref = refs[:k], refs[k], refs[k + 1]
        mine = own_ref[...].astype(F32)
        acc = None
        for s in range(k):
            term = jnp.where(chip_ref[0] == s, mine, part_refs[s][...].astype(F32))
            acc = term if acc is None else acc + term
        o_ref[...] = acc

    def other(s):
        return pl.BlockSpec((None, tr, c), lambda i, ch, co: (jnp.where(ch[0] == s, (s + 1) % k, s), i, 0))

    return pl.pallas_call(
        body, grid_spec=pltpu.PrefetchScalarGridSpec(
            num_scalar_prefetch=2, grid=(r // tr,),
            in_specs=[other(s) for s in range(k)] + [pl.BlockSpec((None, tr, c), lambda i, ch, co: (ch[0], i, 0))],
            out_specs=pl.BlockSpec((None, tr, c), lambda i, ch, co: (co[0], i, 0))),
        out_shape=SDS((2, r, c), F32), compiler_params=_params(1), name="sum_chips")(chip, core, *([parts] * k), own)


def _meta_grad(dh, nb, nc):
    d = dh.shape[1]

    def body(x_ref, o_ref):
        @pl.when(pl.program_id(0) == 0)
        def _():
            o_ref[...] = jnp.zeros_like(o_ref)

        o_ref[...] += x_ref[PAD:CH, :]

    return pl.pallas_call(body, grid=(nb,), in_specs=[pl.BlockSpec((CH, d), lambda b: (b * nc, 0))],
                          out_specs=pl.BlockSpec((N_META, d), lambda b: (0, 0)), out_shape=SDS((N_META, d), F32),
                          compiler_params=_params(1), name="meta_grad")(dh)


ANY = pl.BlockSpec(memory_space=pl.ANY)


def _place():
    x, y, c = lax.axis_index("x"), lax.axis_index("y"), lax.axis_index("c")
    chips = [(1 - x, y), (x, 1 - y), (1 - x, 1 - y)]
    return x, y, c, chips


def _remote(src, dst, send_sems, recv_sems, k, to):
    return pltpu.make_async_remote_copy(src_ref=src, dst_ref=dst, send_sem=send_sems.at[k], recv_sem=recv_sems.at[k],
                                        device_id=to, device_id_type=MESH)


def _gather_weights(pbs, ps):
    nt = len(pbs)

    def body(*refs):
        pb_refs, ps_ref, gb_refs, gs_ref = refs[:nt], refs[nt], refs[nt + 1:2 * nt + 1], refs[2 * nt + 1]
        send_sems, recv_sems, local_sems = refs[2 * nt + 2:]
        x, y, c, chips = _place()
        s = 2 * x + y
        sib = (x, y, 1 - c)
        l1 = pltpu.make_async_copy(ps_ref, gs_ref.at[s], local_sems.at[0])
        l1.start()
        sends = []
        for k, (px, py) in enumerate(chips):
            for t in range(nt):
                sends.append(_remote(pb_refs[t].at[c], gb_refs[t].at[s, c], send_sems, recv_sems, 6 * t + k, (px, py, c)))
            sends.append(_remote(ps_ref, gs_ref.at[s], send_sems, recv_sems, 6 * nt + k, (px, py, c)))
        for cp in sends:
            cp.start()
        for k, (px, py) in enumerate(chips):
            sk = 2 * px + py
            for t in range(nt):
                _remote(pb_refs[t].at[c], gb_refs[t].at[sk, c], send_sems, recv_sems, 6 * t + k, sib).wait_recv()
                fwd = _remote(gb_refs[t].at[sk, c], gb_refs[t].at[sk, c], send_sems, recv_sems, 6 * t + 3 + k, sib)
                fwd.start()
                sends.append(fwd)
        for k, (px, py) in enumerate(chips):
            sk = 2 * px + py
            for t in range(nt):
                _remote(pb_refs[t].at[c], gb_refs[t].at[sk, 1 - c], send_sems, recv_sems, 6 * t + 3 + k, sib).wait_recv()
            _remote(ps_ref, gs_ref.at[sk], send_sems, recv_sems, 6 * nt + k, sib).wait_recv()
        for cp in sends:
            cp.wait_send()
        l1.wait()

    nsem = 6 * nt + 3
    out = pl.pallas_call(
        body, in_specs=[ANY] * (nt + 1), out_specs=[ANY] * (nt + 1),
        out_shape=[SDS((4,) + pb.shape, pb.dtype) for pb in pbs] + [SDS((4,) + ps.shape, ps.dtype)],
        scratch_shapes=[pltpu.SemaphoreType.DMA((nsem,)), pltpu.SemaphoreType.DMA((nsem,)), pltpu.SemaphoreType.DMA((1,))],
        name="gather_weights")(*pbs, ps)
    return out[:nt], out[nt]


def _contain(wpad, shift):
    r, cw = wpad.shape
    tr = _tile16(r, 256)

    def body(n_ref, x_ref, o_ref):
        o_ref[...] = pltpu.roll(x_ref[...], n_ref[0], axis=1).astype(o_ref.dtype)

    spec = pl.BlockSpec((tr, cw), lambda i, n: (i, 0))
    return pl.pallas_call(
        body, grid_spec=pltpu.PrefetchScalarGridSpec(num_scalar_prefetch=1, grid=(r // tr,), in_specs=[spec], out_specs=spec),
        out_shape=SDS((r, cw), BF16), compiler_params=_params(1), name="contain")(shift, wpad)


def _place_own(gb, pb, chip):
    _, _, r, c = gb.shape
    tr = _tile16(r, 1100)

    def body(s_ref, p_ref, g_in, o_ref):
        o_ref[...] = p_ref[...]

    return pl.pallas_call(
        body, grid_spec=pltpu.PrefetchScalarGridSpec(
            num_scalar_prefetch=1, grid=(2, r // tr),
            in_specs=[pl.BlockSpec((None, tr, c), lambda h, i, s: (h, i, 0)), ANY],
            out_specs=pl.BlockSpec((None, None, tr, c), lambda h, i, s: (s[0], h, i, 0))),
        out_shape=SDS(gb.shape, gb.dtype), input_output_aliases={2: 0}, compiler_params=_params(2),
        name="place_own")(chip, pb, gb)


def _sem_scratch(n_remote, n_local):
    return [pltpu.SemaphoreType.DMA((n_remote,)), pltpu.SemaphoreType.DMA((n_remote,)), pltpu.SemaphoreType.DMA((n_local,))]


def _swap_halves(sends):
    nt = len(sends)

    def body(*refs):
        s_refs, got_refs = refs[:nt], refs[nt:2 * nt]
        send_sems, recv_sems = refs[2 * nt:]
        x, y, c, _ = _place()
        sib = (x, y, 1 - c)
        remote = [_remote(s_refs[t].at[1 - c, s], got_refs[t].at[s], send_sems, recv_sems, 4 * t + s, sib)
                  for t in range(nt) for s in range(4)]
        for cp in remote:
            cp.start()
        for cp in remote:
            cp.wait()

    return pl.pallas_call(
        body, in_specs=[ANY] * nt, out_specs=[ANY] * nt, out_shape=[SDS(g.shape[1:], g.dtype) for g in sends],
        scratch_shapes=[pltpu.SemaphoreType.DMA((4 * nt,)), pltpu.SemaphoreType.DMA((4 * nt,))], name="swap_halves")(*sends)


def _scatter_chip_sums(parts):
    nt = len(parts)

    def body(*refs):
        a_refs, r_refs = refs[:nt], refs[nt:2 * nt]
        send_sems, recv_sems = refs[2 * nt:]
        x, y, c, chips = _place()
        s = 2 * x + y
        sends = [_remote(a_refs[t].at[2 * px + py], r_refs[t].at[s], send_sems, recv_sems, 3 * t + k, (px, py, c))
                 for t in range(nt) for k, (px, py) in enumerate(chips)]
        for cp in sends:
            cp.start()
        for t in range(nt):
            for k, (px, py) in enumerate(chips):
                _remote(a_refs[t].at[s], r_refs[t].at[2 * px + py], send_sems, recv_sems, 3 * t + k, (px, py, c)).wait_recv()
        for cp in sends:
            cp.wait_send()

    return pl.pallas_call(
        body, in_specs=[ANY] * nt, out_specs=[ANY] * nt, out_shape=[SDS(a.shape, a.dtype) for a in parts],
        scratch_shapes=[pltpu.SemaphoreType.DMA((3 * nt,)), pltpu.SemaphoreType.DMA((3 * nt,))],
        name="scatter_chip_sums")(*parts)


def _join_halves(fs):
    nt = len(fs)

    def body(*refs):
        f_refs = refs[nt:2 * nt]
        send_sems, recv_sems = refs[2 * nt:]
        x, y, c, _ = _place()
        sib = (x, y, 1 - c)
        sends = [_remote(f_refs[t].at[c], f_refs[t].at[c], send_sems, recv_sems, t, sib) for t in range(nt)]
        for cp in sends:
            cp.start()
        for t in range(nt):
            _remote(f_refs[t].at[c], f_refs[t].at[1 - c], send_sems, recv_sems, t, sib).wait_recv()
        for cp in sends:
            cp.wait_send()

    return pl.pallas_call(
        body, in_specs=[ANY] * nt, out_specs=[ANY] * nt, out_shape=[SDS(f.shape, f.dtype) for f in fs],
        input_output_aliases={t: t for t in range(nt)},
        scratch_shapes=[pltpu.SemaphoreType.DMA((nt,)), pltpu.SemaphoreType.DMA((nt,))], name="join_halves")(*fs)


def _uncontain(cont, n_head, width):
    r, cw = cont.shape
    tr = _tile(r, 256)

    def body(n_ref, x_ref, o_ref):
        o_ref[...] = pltpu.roll(x_ref[...], n_ref[0], axis=1)[:, :width]

    return pl.pallas_call(
        body, grid_spec=pltpu.PrefetchScalarGridSpec(
            num_scalar_prefetch=1, grid=(r // tr,), in_specs=[pl.BlockSpec((tr, cw), lambda i, n: (i, 0))],
            out_specs=pl.BlockSpec((tr, width), lambda i, n: (i, 0))),
        out_shape=SDS((r, width), F32), compiler_params=_params(1), name="uncontain")(n_head, cont)


WEIGHTS = ("meta_tokens", "norm_w", "w_in", "conv_w", "a_log", "dt_bias", "gnorm_a", "gnorm_b", "hgrn_lower_bounds",
           "w_branch_a", "w_branch_b", "w_out", "final_norm_w")
SHARD_AXIS = {"meta_tokens": 1, "w_in": 2, "conv_w": 2, "w_branch_a": 2, "w_branch_b": 2, "w_out": 1}
FLAT_C = 1024


def _flat(parts, rows, cols=FLAT_C):
    v = jnp.concatenate([p.reshape(-1) for p in parts])
    return jnp.pad(v, (0, rows * cols - v.shape[0])).reshape(rows, cols)


def _local_step(x, target, w, lay):
    nb, seq, d = x.shape
    tp = CH + seq
    nc = tp // CH
    n = nb * tp
    e_mat, s_mat = _gate_consts()
    lb_all = _lb_fwd(w["hgrn_lower_bounds"])
    h = jnp.concatenate([jnp.zeros((nb, PAD, d), F32), jnp.broadcast_to(w["meta_tokens"][None], (nb, N_META, d)), x],
                        axis=1).reshape(n, d)
    rep = lambda a: jnp.repeat(a, HD)[None, :]
    saved = []
    for l in range(DEPTH):
        nw = w["norm_w"][l][None, :]
        proj, xn = _norm_proj_fwd(h, nw, w["w_in"][l])
        qkv = _gdn_prep_fwd(proj, w["conv_w"][l], lay, nb, tp)
        alog, dtb = rep(w["a_log"][l]), rep(w["dt_bias"][l])
        oa, sa, sva = _gdn_fwd(qkv, proj, e_mat, alog, dtb, lay, nb, nc)
        lbl = lb_all[l][None, :]
        ob, sb = _hgrn_fwd(proj, lbl, lay, nb, nc)
        ga, gb = w["gnorm_a"][l][None, :], w["gnorm_b"][l][None, :]
        hn = _merge_fwd(h, oa, ob, proj, ga, gb, w["w_branch_a"][l], w["w_branch_b"][l], w["w_out"][l], lay)
        saved.append((h, nw, proj, qkv, alog, dtb, oa, sa, lbl, ob, sb, ga, gb, xn, sva))
        h = hn
    target_p = jnp.pad(target, ((0, 0), (CH, 0), (0, 0))).reshape(n, d)
    lp, dh, dfw = _loss_head(h, target_p, w["final_norm_w"][None, :], nb, tp)
    loss = jnp.sum(lp[::8, 0])
    g = {n_: [None] * DEPTH for n_ in WEIGHTS}
    dlb_all = [None] * DEPTH
    for l in reversed(range(DEPTH)):
        h, nw, proj, qkv, alog, dtb, oa, sa, lbl, ob, sb, ga, gb, xn, sva = saved[l]
        dproj, doa, dob, dwa, dwb, dwo, dga, dgb = _merge_bwd(dh, oa, ob, proj, ga, gb, w["w_branch_a"][l],
                                                             w["w_branch_b"][l], w["w_out"][l], lay, tp)
        dproj, acc_b = _hgrn_bwd(proj, lbl, sb, dob, dproj, lay, nb, nc)
        dqkv, dproj, acc_a = _gdn_bwd(qkv, proj, e_mat, s_mat, alog, dtb, sa, sva, doa, dproj, lay, nb, nc)
        dproj, dconv = _gdn_prep_bwd(proj, w["conv_w"][l], dqkv, dproj, lay, nb, tp)
        dh, dnw = _proj_bwd_dx(dproj, w["w_in"][l], h, nw, dh, tp)
        g["w_in"][l] = _proj_bwd_dw(dproj, xn, tp)
        g["norm_w"][l] = dnw[0]
        g["conv_w"][l] = dconv
        g["a_log"][l] = acc_a[0, ::HD]
        g["dt_bias"][l] = acc_a[1, ::HD]
        g["gnorm_a"][l], g["gnorm_b"][l] = dga[0], dgb[0]
        g["w_branch_a"][l], g["w_branch_b"][l], g["w_out"][l] = dwa, dwb, dwo
        dlb_all[l] = acc_b[0]
    grads = {n_: jnp.stack(v) for n_, v in g.items() if v[0] is not None}
    grads["hgrn_lower_bounds"] = _lb_bwd(w["hgrn_lower_bounds"], jnp.stack(dlb_all))
    grads["final_norm_w"] = dfw[0]
    grads["meta_tokens"] = _meta_grad(dh, nb, nc)
    grad_x = dh.reshape(nb, tp, d)[:, CH:, :]
    return loss, grad_x, grads


def kernel(x, meta_tokens, norm_w, w_in, conv_w, a_log, dt_bias, gnorm_a, gnorm_b, hgrn_lower_bounds, w_branch_a, w_branch_b, w_out, final_norm_w, loss_target, m_meta_tokens, m_norm_w, m_w_in, m_conv_w, m_a_log, m_dt_bias, m_gnorm_a, m_gnorm_b, m_hgrn_lower_bounds, m_w_branch_a, m_w_branch_b, m_w_out, m_final_norm_w, v_meta_tokens, v_norm_w, v_w_in, v_conv_w, v_a_log, v_dt_bias, v_gnorm_a, v_gnorm_b, v_hgrn_lower_bounds, v_w_branch_a, v_w_branch_b, v_w_out, v_final_norm_w):
    wl = dict(meta_tokens=meta_tokens, norm_w=norm_w, w_in=w_in, conv_w=conv_w, a_log=a_log, dt_bias=dt_bias, gnorm_a=gnorm_a,
              gnorm_b=gnorm_b, hgrn_lower_bounds=hgrn_lower_bounds, w_branch_a=w_branch_a, w_branch_b=w_branch_b, w_out=w_out,
              final_norm_w=final_norm_w)
    ml = dict(zip(WEIGHTS, (m_meta_tokens, m_norm_w, m_w_in, m_conv_w, m_a_log, m_dt_bias, m_gnorm_a, m_gnorm_b,
                            m_hgrn_lower_bounds, m_w_branch_a, m_w_branch_b, m_w_out, m_final_norm_w)))
    vl = dict(zip(WEIGHTS, (v_meta_tokens, v_norm_w, v_w_in, v_conv_w, v_a_log, v_dt_bias, v_gnorm_a, v_gnorm_b,
                            v_hgrn_lower_bounds, v_w_branch_a, v_w_branch_b, v_w_out, v_final_norm_w)))
    d = x.shape[2]
    lay = _Layout(d)
    nchip = 4

    big = ("w_in", "w_branch_a", "w_branch_b", "w_out")
    small = ("conv_w", "meta_tokens")
    table, heads, cw = lay.pieces(nchip)
    sw = wl["w_in"].shape[2]
    chip_id = (2 * lax.axis_index("x") + lax.axis_index("y")).astype(jnp.int32)
    n_head = sum(jnp.where(chip_id == s, heads[s], 0) for s in range(nchip)).astype(jnp.int32)
    w_pad = jnp.pad(wl["w_in"], ((0, 0), (0, 0), (0, cw - sw))).reshape(DEPTH * d, cw)
    shift = jnp.where(n_head == 0, 0, cw - n_head).astype(jnp.int32).reshape(1)
    pbs = [_contain(w_pad, shift).reshape(DEPTH, d, cw)] + [wl[n].astype(BF16) for n in big[1:]]
    nsmall = sum(int(np.prod(wl[n].shape)) for n in small)
    rs = -(-nsmall // (HD * 8)) * 8
    ps = jnp.pad(jnp.concatenate([wl[n].reshape(-1) for n in small]), (0, rs * HD - nsmall)).reshape(rs, HD)
    gbig, gsmall = _gather_weights(pbs, ps)
    gbig = [_place_own(g, p, chip_id.reshape(1)) for g, p in zip(gbig, pbs)]
    gsmall = gsmall.reshape(nchip, -1)

    wf = dict(wl)
    wf["w_in"] = lay.from_containers([gbig[0][s] for s in range(nchip)])
    for i, n in enumerate(big[1:], start=1):
        wf[n] = jnp.concatenate([gbig[i][s] for s in range(nchip)], axis=SHARD_AXIS[n])
    o = 0
    for n in small:
        sz = int(np.prod(wl[n].shape))
        a = gsmall[:, o:o + sz].reshape((nchip,) + wl[n].shape)
        wf[n] = jnp.concatenate([a[s] for s in range(nchip)], axis=SHARD_AXIS[n])
        o += sz

    loss_part, grad_x, gfull = _local_step(x, loss_target, wf, lay)
    loss = lax.psum(loss_part, ("x", "y", "c"))

    sw = wl["w_in"].shape[2]
    conts, heads = lay.containers(gfull["w_in"], nchip)
    dd = wl["w_branch_a"].shape[2]
    rows_o = wl["w_out"].shape[1]
    by_dest = lambda g, n: [lax.slice_in_dim(g, s * wl[n].shape[SHARD_AXIS[n]], (s + 1) * wl[n].shape[SHARD_AXIS[n]],
                                            axis=SHARD_AXIS[n]) if n in SHARD_AXIS else g for s in range(nchip)]
    small_names = tuple(n for n in WEIGHTS if n not in big)
    nsm = sum(int(np.prod(wl[n].shape)) for n in small_names)
    rsm = -(-nsm // (2 * HD * 8)) * 8
    pack_small = lambda parts: _flat(parts, 2 * rsm, HD).reshape(2, rsm, HD)
    small_by_dest = [by_dest(gfull[n], n) for n in small_names]
    gs = [None,
          jnp.stack(by_dest(gfull["w_branch_a"], "w_branch_a"), axis=1),
          jnp.stack(by_dest(gfull["w_branch_b"], "w_branch_b"), axis=1),
          gfull["w_out"].reshape(DEPTH, nchip, rows_o, d),
          jnp.stack([pack_small([p[s] for p in small_by_dest]) for s in range(nchip)], axis=1)]
    gs = [conts] + [g.reshape((2, nchip, -1, g.shape[-1])) for g in gs[1:]]
    my_chip = (2 * lax.axis_index("x") + lax.axis_index("y")).astype(jnp.int32)
    my_core = lax.axis_index("c").astype(jnp.int32)
    got = _swap_halves([jnp.stack([c.astype(BF16) for c in conts], axis=1)] + [g.astype(BF16) for g in gs[1:4]] + gs[4:])
    chip_sums = [_add_cores(g, b, my_core.reshape(1)) for g, b in zip(gs, got)]
    by_chip = _scatter_chip_sums(chip_sums)
    full = _join_halves([_sum_chips(p, a, my_chip.reshape(1), my_core.reshape(1)) for p, a in zip(by_chip, chip_sums)])
    n_head = sum(jnp.where(my_chip == s, heads[s], 0) for s in range(nchip)).astype(jnp.int32).reshape(1)
    g_w_in = _uncontain(full[0].reshape(DEPTH * d, -1), n_head, sw)
    g2 = {"w_in": g_w_in, "w_branch_a": full[1].reshape(-1, dd), "w_branch_b": full[2].reshape(-1, dd),
          "w_out": full[3].reshape(-1, d), "small": full[4].reshape(2 * rsm, HD)}

    def two_d(src, n):
        if n == "small":
            return _flat([src[k] for k in small_names], 2 * rsm, HD)
        return src[n].reshape(g2[n].shape)

    outs = {}
    for n in big[1:] + ("small",):
        delta, mnew, vnew = _adamw(g2[n], two_d(wl, n), two_d(ml, n), two_d(vl, n))
        outs[n] = (g2[n], delta, mnew, vnew)
    cols = lambda a: jnp.transpose(a.reshape(wl["w_in"].shape), (2, 0, 1))
    g_cols = cols(g2["w_in"])
    outs["w_in"] = tuple(jnp.transpose(a, (1, 2, 0)) for a in
                         (g_cols,) + tuple(_adamw(g_cols, cols(wl["w_in"]), cols(ml["w_in"]), cols(vl["w_in"]))))
    res = [{}, {}, {}, {}]
    for i in range(4):
        for n in big:
            res[i][n] = outs[n][i].reshape(wl[n].shape)
        v, o = outs["small"][i].reshape(-1), 0
        for n in small_names:
            sz = int(np.prod(wl[n].shape))
            res[i][n] = v[o:o + sz].reshape(wl[n].shape)
            o += sz
    return (loss, grad_x, *[res[0][n] for n in WEIGHTS], *[res[1][n] for n in WEIGHTS], *[res[2][n] for n in WEIGHTS],
            *[res[3][n] for n in WEIGHTS])
```

```python
import functools

import numpy as np
import jax
import jax.numpy as jnp
from jax import lax
from jax.experimental import pallas as pl
from jax.experimental.pallas import tpu as pltpu

F32 = jnp.float32
BF16 = jnp.bfloat16
HI = lax.Precision.HIGHEST
SDS = jax.ShapeDtypeStruct

NH = 4
HD = 128
HW = NH * HD
N_META = 16
CH = 64
SUB = 16
PAD = CH - N_META
EPS = 1e-6
Q_SCALE = HD ** -0.5
DEPTH = 2
CONV_K = 4
PREP_HEADS = 2
VMEM_LIMIT = 56 * 1024 * 1024
ADAM_LR, ADAM_B1, ADAM_B2, ADAM_EPS, ADAM_WD, ADAM_STEP = 0.001, 0.9, 0.999, 1e-08, 0.01, 10
MESH = pl.DeviceIdType.MESH


def _nn(a, b):
    return jnp.dot(a, b, precision=HI, preferred_element_type=F32)


def _nt(a, b):
    return lax.dot_general(a, b, (((1,), (1,)), ((), ())), precision=HI, preferred_element_type=F32)


def _tn(a, b):
    return _nn(a.T, b)


def _scan_rows(x, group, reverse=False):
    n = x.shape[0]
    pos = lax.bitwise_and(_iota2(x.shape, 0), group - 1)
    s = 1
    while s < group:
        if reverse:
            x = x + jnp.where(pos < group - s, pltpu.roll(x, n - s, axis=0), 0.0)
        else:
            x = x + jnp.where(pos >= s, pltpu.roll(x, s, axis=0), 0.0)
        s *= 2
    return x


def _bnn(a, b):
    return jnp.dot(a.astype(BF16), b.astype(BF16), preferred_element_type=F32)


def _bnt(a, b):
    return lax.dot_general(a.astype(BF16), b.astype(BF16), (((1,), (1,)), ((), ())), preferred_element_type=F32)


def _btn(a, b):
    return lax.dot_general(a.astype(BF16), b.astype(BF16), (((0,), (0,)), ((), ())), preferred_element_type=F32)


def _hi_lo(x):
    hi = x.astype(jnp.bfloat16)
    return hi, (x - hi.astype(F32)).astype(jnp.bfloat16)


def _dot3(dims):
    def f(a, b):
        ah, al = _hi_lo(a)
        bh, bl = _hi_lo(b)
        d = lambda p, q: lax.dot_general(p, q, (dims, ((), ())), preferred_element_type=F32)
        return d(ah, bh) + (d(ah, bl) + d(al, bh))
    return f


_rnn, _rnt, _rtn = _dot3(((1,), (0,))), _dot3(((1,), (1,))), _dot3(((0,), (0,)))
_enn, _ent, _etn = _bnn, _bnt, _btn
_hnn, _hnt, _htn = _bnn, _bnt, _btn


def _rr(x):
    return x


def _sig(x):
    return jax.nn.sigmoid(x)


def _silu(x):
    return x * _sig(x)


def _dsilu(x):
    s = _sig(x)
    return s * (1.0 + x * (1.0 - s))


def _softplus(x):
    return jnp.maximum(x, 0.0) + jnp.log(1.0 + jnp.exp(-jnp.abs(x)))


def _logsig(x):
    return jnp.minimum(x, 0.0) - jnp.log(1.0 + jnp.exp(-jnp.abs(x)))


def _rs(x):
    return jnp.sum(x, axis=-1, keepdims=True)


def _params(n_axes):
    return pltpu.CompilerParams(dimension_semantics=("arbitrary",) * n_axes, vmem_limit_bytes=VMEM_LIMIT)


def _tile(n, target, mult=8):
    best = mult
    for t in range(mult, target + 1, mult):
        if n % t == 0:
            best = t
    assert n % best == 0, (n, mult)
    return best


def _ctile(pw, most=7):
    return HD * max(k for k in range(1, most + 1) if (pw // HD) % k == 0)


def _iota2(shape, axis):
    return lax.broadcasted_iota(jnp.int32, shape, axis)


class _Layout:
    def __init__(self, d):
        self.d = d
        self.wm = 2 * HW + 2 * d
        self.c_qkv = self.wm
        self.c_b = self.wm + 3 * HW
        self.c_ba = self.wm + 6 * HW
        self.pw = self.c_ba + HD
        assert self.c_b % (3 * HW) == 0
        o = 0
        segs = {}
        for name, w in (("a_q", HW), ("a_k", HW), ("a_v", HW), ("ba", 2 * NH), ("a_z", HW), ("b_q", HW), ("b_f", HW),
                        ("b_i", HW), ("b_g", HW), ("gate_a", d), ("gate_b", d)):
            segs[name] = (o, o + w)
            o += w
        self.segs = segs
        self.width = o
        self.order = ("a_z", "b_g", "gate_a", "gate_b", "a_q", "a_k", "a_v", "b_q", "b_f", "b_i", "ba")

    def to_kernel(self, w):
        parts = [w[..., self.segs[n][0]:self.segs[n][1]] for n in self.order]
        parts.append(jnp.zeros(w.shape[:-1] + (HD - 2 * NH,), w.dtype))
        return jnp.concatenate(parts, axis=-1)

    def containers(self, g, nchip):
        table, heads, cw = self.pieces(nchip)
        out = []
        for s in range(nchip):
            parts, at = [], 0
            for kcol, w, ccol in sorted(table[s], key=lambda p: p[2]):
                if ccol > at:
                    parts.append(jnp.zeros(g.shape[:-1] + (ccol - at,), g.dtype))
                parts.append(g[..., kcol:kcol + w])
                at = ccol + w
            if at < cw:
                parts.append(jnp.zeros(g.shape[:-1] + (cw - at,), g.dtype))
            out.append(jnp.concatenate(parts, axis=-1))
        return out, heads

    def pieces(self, nchip):
        off, where = 0, {}
        for n in self.order:
            where[n] = off
            off += self.segs[n][1] - self.segs[n][0]
        names = sorted(self.segs, key=lambda n: self.segs[n][0])
        sw = self.width // nchip
        cw = -(-sw // HD) * HD
        table, heads = [], []
        for s in range(nchip):
            lo, hi = s * sw, (s + 1) * sw
            pieces = []
            for n in names:
                a, b = max(lo, self.segs[n][0]), min(hi, self.segs[n][1])
                if a < b:
                    pieces.append((where[n] + a - self.segs[n][0], b - a))
            start, width = pieces[0]
            n_head = min((-start) % HD, width)
            body = ([(start + n_head, width - n_head)] if width > n_head else []) + pieces[1:]
            rows, at = [], 0
            for c, w in body:
                rows.append((c, w, at))
                at += w
            if n_head:
                rows.append((start, n_head, cw - n_head))
            table.append(rows)
            heads.append(n_head)
        return table, heads, cw

    def from_containers(self, conts):
        table, _, _ = self.pieces(len(conts))
        cut = sorted((kcol, w, s, ccol) for s, rows in enumerate(table) for kcol, w, ccol in rows)
        parts, at = [], 0
        for kcol, w, s, ccol in cut:
            assert kcol == at, (kcol, at)
            parts.append(conts[s][..., ccol:ccol + w])
            at = kcol + w
        parts.append(jnp.zeros(conts[0].shape[:-1] + (self.pw - at,), conts[0].dtype))
        return jnp.concatenate(parts, axis=-1)

    def from_kernel(self, g):
        off, where = 0, {}
        for n in self.order:
            w = self.segs[n][1] - self.segs[n][0]
            where[n] = (off, off + w)
            off += w
        names = sorted(self.segs, key=lambda n: self.segs[n][0])
        return jnp.concatenate([g[..., where[n][0]:where[n][1]] for n in names], axis=-1)


def _norm_proj_fwd(h, nw, wp):
    n, d = h.shape
    pw = wp.shape[1]
    tm, tn = _tile(n, 1408, HD), _ctile(pw)

    def body(h_ref, nw_ref, w_ref, o_ref, xt_ref, xn_ref):
        @pl.when(pl.program_id(1) == 0)
        def _():
            x = h_ref[...]
            r = lax.rsqrt(jnp.mean(x * x, axis=-1, keepdims=True) + EPS)
            xn = (x * r * nw_ref[...]).astype(BF16)
            xn_ref[...] = xn
            xt_ref[...] = xn.T

        o_ref[...] = jnp.dot(xn_ref[...], w_ref[...], preferred_element_type=F32)

    return pl.pallas_call(
        body, grid=(n // tm, pw // tn),
        in_specs=[pl.BlockSpec((tm, d), lambda i, j: (i, 0)), pl.BlockSpec((1, d), lambda i, j: (0, 0)),
                  pl.BlockSpec((d, tn), lambda i, j: (0, j))],
        out_specs=[pl.BlockSpec((tm, tn), lambda i, j: (i, j)), pl.BlockSpec((d, tm), lambda i, j: (0, i))],
        out_shape=[SDS((n, pw), F32), SDS((d, n), BF16)], scratch_shapes=[pltpu.VMEM((tm, d), BF16)],
        compiler_params=_params(2), name="norm_proj_fwd")(h, nw, wp)


def _row_valid(tm, tp, base):
    row = base + _iota2((tm, 1), 0)
    return lax.rem(row, tp) >= PAD


def _proj_bwd_dx(dproj, wp, h, nw, dhn, tp):
    n, d = h.shape
    pw = wp.shape[1]
    tm, tk = _tile16(n, 1056), _ctile(pw)
    nk = pw // tk

    def body(dp_ref, w_ref, h_ref, nw_ref, dhn_ref, dh_ref, dnw_ref, acc_ref):
        i, k = pl.program_id(0), pl.program_id(1)

        @pl.when(k == 0)
        def _():
            acc_ref[...] = jnp.zeros_like(acc_ref)

        @pl.when((i == 0) & (k == 0))
        def _():
            dnw_ref[...] = jnp.zeros_like(dnw_ref)

        valid = _row_valid(tm, tp, i * tm)
        dp = jnp.where(valid, dp_ref[...], 0.0)
        acc_ref[...] += _bnt(dp, w_ref[...])

        @pl.when(k == nk - 1)
        def _():
            x = h_ref[...]
            r = lax.rsqrt(jnp.mean(x * x, axis=-1, keepdims=True) + EPS)
            xh = x * r
            dxn = acc_ref[...]
            dnw_ref[...] += jnp.sum(dxn * xh, axis=0, keepdims=True)
            dxh = dxn * nw_ref[...]
            dh_ref[...] = dhn_ref[...] + r * (dxh - xh * jnp.mean(dxh * xh, axis=-1, keepdims=True))

    return pl.pallas_call(
        body, grid=(n // tm, nk),
        in_specs=[pl.BlockSpec((tm, tk), lambda i, k: (i, k)), pl.BlockSpec((d, tk), lambda i, k: (0, k)),
                  pl.BlockSpec((tm, d), lambda i, k: (i, 0)), pl.BlockSpec((1, d), lambda i, k: (0, 0)),
                  pl.BlockSpec((tm, d), lambda i, k: (i, 0))],
        out_specs=[pl.BlockSpec((tm, d), lambda i, k: (i, 0)), pl.BlockSpec((1, d), lambda i, k: (0, 0))],
        out_shape=[SDS((n, d), F32), SDS((1, d), F32)],
        scratch_shapes=[pltpu.VMEM((tm, d), F32)], compiler_params=_params(2), name="proj_bwd_dx")(dproj, wp, h, nw, dhn)


def _proj_bwd_dw(dproj, xt, tp):
    d, n = xt.shape
    pw = dproj.shape[1]
    tm, tn = _tile(n, 1408, HD), _ctile(pw)

    def body(dp_ref, xt_ref, dw_ref):
        i = pl.program_id(1)

        @pl.when(i == 0)
        def _():
            dw_ref[...] = jnp.zeros_like(dw_ref)

        dp = jnp.where(_row_valid(tm, tp, i * tm), dp_ref[...], 0.0)
        dw_ref[...] += jnp.dot(xt_ref[...], dp.astype(BF16), preferred_element_type=F32)

    return pl.pallas_call(
        body, grid=(pw // tn, n // tm),
        in_specs=[pl.BlockSpec((tm, tn), lambda j, i: (i, j)), pl.BlockSpec((d, tm), lambda j, i: (0, i))],
        out_specs=pl.BlockSpec((d, tn), lambda j, i: (0, j)), out_shape=SDS((d, pw), F32),
        compiler_params=_params(2), name="proj_bwd_dw")(dproj, xt)


def _conv_silu(x, w, row):
    c = x * w[CONV_K - 1:CONV_K, :]
    for k in range(1, CONV_K):
        c = c + jnp.where(row >= k, pltpu.roll(x, k, axis=0), 0.0) * w[CONV_K - 1 - k:CONV_K - k, :]
    return c


def _gdn_prep_fwd(proj, conv_w, lay, nb, tp):
    n = proj.shape[0]
    nblk = 3 * NH
    cb = lay.c_qkv // HD
    assert cb % PREP_HEADS == 0 and nblk % PREP_HEADS == 0

    def body(p_ref, w_ref, o_ref):
        row = _iota2((tp, HD), 0)
        for t in range(PREP_HEADS):
            j = pl.program_id(1) * PREP_HEADS + t
            ls = slice(t * HD, (t + 1) * HD)
            c = _conv_silu(p_ref[:, ls], w_ref[:, ls], row)
            s = _silu(c)
            r = lax.rsqrt(_rs(s * s) + EPS)
            scale = jnp.where(j < NH, Q_SCALE, 1.0)
            y = jnp.where(j < 2 * NH, s * r * scale, s)
            o_ref[:, ls] = jnp.where(row >= PAD, y, 0.0)

    wd = PREP_HEADS * HD
    return pl.pallas_call(
        body, grid=(nb, nblk // PREP_HEADS),
        in_specs=[pl.BlockSpec((tp, wd), lambda b, j: (b, cb // PREP_HEADS + j)), pl.BlockSpec((CONV_K, wd), lambda b, j: (0, j))],
        out_specs=pl.BlockSpec((tp, wd), lambda b, j: (b, j)), out_shape=SDS((n, nblk * HD), F32),
        compiler_params=_params(2), name="gdn_prep_fwd")(proj, conv_w)


def _gdn_prep_bwd(proj, conv_w, dqkv, dproj, lay, nb, tp):
    nblk = 3 * NH
    cb = lay.c_qkv // HD

    def body(p_ref, w_ref, dy_ref, dp_in, dp_ref, dw_ref):
        b = pl.program_id(1)
        row = _iota2((tp, HD), 0)
        r4 = _iota2((CONV_K, HD), 0)
        for t in range(PREP_HEADS):
            j = pl.program_id(0) * PREP_HEADS + t
            ls = slice(t * HD, (t + 1) * HD)
            x = p_ref[:, ls]
            w = w_ref[:, ls]
            c = _conv_silu(x, w, row)
            s = _silu(c)
            dy = jnp.where(row >= PAD, dy_ref[:, ls], 0.0)
            r = lax.rsqrt(_rs(s * s) + EPS)
            nh = s * r
            scale = jnp.where(j < NH, Q_SCALE, 1.0)
            ds_n = scale * r * (dy - nh * _rs(dy * nh))
            ds = jnp.where(j < 2 * NH, ds_n, dy)
            dc = ds * _dsilu(c)
            dx = dc * w[CONV_K - 1:CONV_K, :]
            dws = [jnp.sum(dc * x, axis=0, keepdims=True)]
            for k in range(1, CONV_K):
                dx = dx + jnp.where(row < tp - k, pltpu.roll(dc, tp - k, axis=0), 0.0) * w[CONV_K - 1 - k:CONV_K - k, :]
                xs = jnp.where(row >= k, pltpu.roll(x, k, axis=0), 0.0)
                dws.append(jnp.sum(dc * xs, axis=0, keepdims=True))
            dp_ref[:, ls] = dx.astype(dp_ref.dtype)
            dw = jnp.zeros((CONV_K, HD), F32)
            for k in range(CONV_K):
                dw = dw + jnp.where(r4 == CONV_K - 1 - k, dws[k], 0.0)

            @pl.when(b == 0)
            def _():
                dw_ref[:, ls] = dw

            @pl.when(b > 0)
            def _():
                dw_ref[:, ls] += dw

    wd = PREP_HEADS * HD
    return pl.pallas_call(
        body, grid=(nblk // PREP_HEADS, nb),
        in_specs=[pl.BlockSpec((tp, wd), lambda j, b: (b, cb // PREP_HEADS + j)), pl.BlockSpec((CONV_K, wd), lambda j, b: (0, j)),
                  pl.BlockSpec((tp, wd), lambda j, b: (b, j)), pl.BlockSpec(memory_space=pl.ANY)],
        out_specs=[pl.BlockSpec((tp, wd), lambda j, b: (b, cb // PREP_HEADS + j)), pl.BlockSpec((CONV_K, wd), lambda j, b: (0, j))],
        out_shape=[SDS(dproj.shape, dproj.dtype), SDS((CONV_K, nblk * HD), F32)],
        input_output_aliases={3: 0}, compiler_params=_params(2), name="gdn_prep_bwd")(proj, conv_w, dqkv, dproj)


def _gate_consts():
    e = np.zeros((HD, 2 * HW), np.float32)
    s = np.zeros((2 * HW, HD), np.float32)
    for h in range(NH):
        e[h, h * HD:(h + 1) * HD] = 1.0
        e[NH + h, HW + h * HD:HW + (h + 1) * HD] = 1.0
        s[h * HD, h] = 1.0
        s[HW + h * HD, NH + h] = 1.0
    return jnp.asarray(e), jnp.asarray(s)


def _gdn_tri():
    i, j = _iota2((CH, CH), 0), _iota2((CH, CH), 1)
    return i >= j, i > j


def _each(fn, *lists):
    return [fn(*xs) for xs in zip(*lists)]


def _tri_inv(a_list, eye):
    p = [-a for a in a_list]
    t = [eye + x for x in p]
    for _ in range(5):
        p = _each(_rnn, p, p)
        tp_ = _each(_rnn, t, p)
        t = _each(lambda x, y: x + y, t, tp_)
    return t


def _gdn_chunks(args, solved=None):
    causal, strict = _gdn_tri()
    eye = jnp.where(_iota2((CH, CH), 0) == _iota2((CH, CH), 1), 1.0, 0.0)
    q, k, v, beta, g, s0 = (list(t) for t in zip(*args))
    gc = [_scan_rows(x, CH) for x in g]
    dm = [jnp.where(causal, jnp.exp(jnp.where(causal, x[:, :CH] - x[:, :CH].T, 0.0)), 0.0) for x in gc]
    ds = [jnp.where(strict, x, 0.0) for x in dm]
    kb = _each(lambda x, y: x * y, k, beta)
    by_k = _each(_ent, [jnp.concatenate([x, y], axis=0) for x, y in zip(kb, q)], k)
    kk = [x[:CH] for x in by_k]
    qk = [x[CH:] for x in by_k]
    a = _each(lambda x, y: x * y, kk, ds)
    eg = [jnp.exp(x) for x in gc]
    rw = _each(lambda x, y: x * y, kb, eg)
    if solved is None:
        tinv = _tri_inv(a, eye)
        rv = _each(lambda x, y: x * y, v, beta)
        u = _each(_rnn, tinv, rv)
        w = _each(_rnn, tinv, rw)
    else:
        tinv, u, w = (list(t) for t in zip(*solved))
    ws = _each(_enn, w, s0)
    vn = _each(lambda x, y: x - y, u, ws)
    p = _each(lambda x, y: x * y, qk, dm)
    qg = _each(lambda x, y: x * y, q, eg)
    out = []
    for i in range(len(args)):
        gl = gc[i][CH - 1:CH, :]
        ek = jnp.exp(gl - gc[i])
        out.append(dict(gc=gc[i], dm=dm[i], ds=ds[i], kb=kb[i], a=a[i], tinv=tinv[i], eg=eg[i], rw=rw[i], u=u[i], w=w[i],
                        vn=vn[i], p=p[i], qg=qg[i], egl=jnp.exp(gl), ek=ek, kd=k[i] * ek))
    return out


def _gdn_gates(ba, e, alog, dtb):
    raw = _nn(ba, e)
    beta = _sig(raw[:, :HW])
    za = raw[:, HW:] + dtb
    g = -jnp.exp(alog) * _softplus(za)
    return beta, g, za


def _seqs_per_step(nb):
    return 4 if nb % 4 == 0 else (2 if nb % 2 == 0 else 1)


def _gdn_fwd(qkv, proj, e_mat, alog, dtb, lay, nb, nc):
    n = qkv.shape[0]
    tp = n // nb
    cba = lay.c_ba // HD
    gb = _seqs_per_step(nb)

    def body(x_ref, ba_ref, e_ref, al_ref, dt_ref, o_ref, so_ref, sv_ref, s_ref):
        @pl.when(pl.program_id(1) == 0)
        def _():
            s_ref[...] = jnp.zeros_like(s_ref)

        args = []
        for j in range(gb):
            beta, g, _ = _gdn_gates(ba_ref[j], e_ref[...], al_ref[...], dt_ref[...])
            for h in range(NH):
                hs = slice(h * HD, (h + 1) * HD)
                args.append((x_ref[j, :, hs], x_ref[j, :, HW + h * HD:HW + (h + 1) * HD],
                             x_ref[j, :, 2 * HW + h * HD:2 * HW + (h + 1) * HD], beta[:, hs], g[:, hs], s_ref[j, h]))
        cs = _gdn_chunks(args)
        s0s = [a[5] for a in args]
        o1 = _each(lambda c, s0: _enn(c["qg"], s0), cs, s0s)
        o2 = [_enn(c["p"], c["vn"]) for c in cs]
        upd = [_etn(c["kd"], c["vn"]) for c in cs]
        res = [(o1[i] + o2[i], s0s[i] * cs[i]["egl"] + upd[i]) for i in range(len(cs))]
        zero = jnp.zeros((CH, HD - CH), F32)
        for j in range(gb):
            for h in range(NH):
                c = cs[j * NH + h]
                so_ref[j, h] = args[j * NH + h][5]
                sv_ref[j, h] = jnp.concatenate([c["u"], c["w"], c["tinv"], zero], axis=-1)
                s_ref[j, h] = res[j * NH + h][1]
            o_ref[j] = jnp.concatenate([res[j * NH + h][0] for h in range(NH)], axis=-1)

    o, st, sv = pl.pallas_call(
        body, grid=(nb // gb, nc),
        in_specs=[pl.BlockSpec((gb, CH, 3 * HW), lambda b, c: (b, c, 0)), pl.BlockSpec((gb, CH, HD), lambda b, c: (b, c, cba)),
                  pl.BlockSpec((HD, 2 * HW), lambda b, c: (0, 0)), pl.BlockSpec((1, HW), lambda b, c: (0, 0)),
                  pl.BlockSpec((1, HW), lambda b, c: (0, 0))],
        out_specs=[pl.BlockSpec((gb, CH, HW), lambda b, c: (b, c, 0)),
                   pl.BlockSpec((gb, None, NH, HD, HD), lambda b, c: (b, c, 0, 0, 0)),
                   pl.BlockSpec((gb, None, NH, CH, 3 * HD), lambda b, c: (b, c, 0, 0, 0))],
        out_shape=[SDS((nb, tp, HW), F32), SDS((nb, nc, NH, HD, HD), F32), SDS((nb, nc, NH, CH, 3 * HD), F32)],
        scratch_shapes=[pltpu.VMEM((gb, NH, HD, HD), F32)], compiler_params=_params(2), name="gdn_fwd")(
            qkv.reshape(nb, tp, 3 * HW), proj.reshape(nb, tp, -1), e_mat, alog, dtb)
    return o.reshape(n, HW), st, sv


def _gdn_bwd(qkv, proj, e_mat, s_mat, alog, dtb, states, solved, do, dproj, lay, nb, nc):
    n = qkv.shape[0]
    tp = n // nb
    cba = lay.c_ba // HD
    gb = _seqs_per_step(nb)

    def body(x_ref, ba_ref, e_ref, sm_ref, al_ref, dt_ref, st_ref, sv_ref, do_ref, dp_in, dx_ref, dba_ref, acc_ref, ds_ref):
        ci = pl.program_id(1)

        @pl.when(ci == 0)
        def _():
            ds_ref[...] = jnp.zeros_like(ds_ref)

        @pl.when((ci == 0) & (pl.program_id(0) == 0))
        def _():
            acc_ref[...] = jnp.zeros_like(acc_ref)

        causal, strict = _gdn_tri()
        alog = al_ref[...]
        row = _iota2((CH, 1), 0)
        valid = (row >= PAD) | (ci < nc - 1)
        last = row == CH - 1
        gates = [_gdn_gates(ba_ref[j], e_ref[...], alog, dt_ref[...]) for j in range(gb)]
        args, do, ds1, solved = [], [], [], []
        for j in range(gb):
            beta, g, _ = gates[j]
            for h in range(NH):
                hs = slice(h * HD, (h + 1) * HD)
                args.append((x_ref[j, :, hs], x_ref[j, :, HW + h * HD:HW + (h + 1) * HD],
                             x_ref[j, :, 2 * HW + h * HD:2 * HW + (h + 1) * HD], beta[:, hs], g[:, hs], st_ref[j, h]))
                do.append(do_ref[j, :, hs])
                ds1.append(ds_ref[j, h])
                solved.append((sv_ref[j, h, :, 2 * HD:2 * HD + CH], sv_ref[j, h, :, 0:HD], sv_ref[j, h, :, HD:2 * HD]))
        q, k, v, bh, _, s0 = (list(t) for t in zip(*args))
        cs = _gdn_chunks(args, solved)
        get = lambda name: [c[name] for c in cs]
        mul = lambda x, y: x * y
        add = lambda x, y: x + y
        dvn = _each(add, _each(_etn, get("p"), do), _each(_enn, get("kd"), ds1))
        by_s0 = _each(_ent, [jnp.concatenate([x, y], axis=0) for x, y in zip(do, dvn)], s0)
        dqg = [x[:CH] for x in by_s0]
        dw = [-x[CH:] for x in by_s0]
        dp = [jnp.where(causal, x, 0.0) for x in _each(_ent, do, get("vn"))]
        dkd = _each(_ent, get("vn"), ds1)
        ds_a = _each(_etn, get("qg"), do)
        ds_b = _each(_etn, get("w"), dvn)
        ds_new = [ds_a[i] - ds_b[i] + ds1[i] * cs[i]["egl"] for i in range(len(cs))]
        drvw = _each(_rtn, get("tinv"), [jnp.concatenate([x, y], axis=-1) for x, y in zip(dvn, dw)])
        drv = [x[:, :HD] for x in drvw]
        drw = [x[:, HD:] for x in drvw]
        uw = [sv_ref[j, h, :, 0:2 * HD] for j in range(gb) for h in range(NH)]
        da = [jnp.where(strict, -x, 0.0) for x in _each(_rnt, drvw, uw)]
        m = [da[i] * cs[i]["a"] + dp[i] * cs[i]["p"] for i in range(len(cs))]
        dkk = _each(mul, da, get("ds"))
        dqk = _each(mul, dp, get("dm"))
        by_k = _each(_enn, [jnp.concatenate([x, y], axis=0) for x, y in zip(dqk, dkk)], k)
        dq = _each(add, [x[:CH] for x in by_k], _each(mul, dqg, get("eg")))
        dkb = _each(add, [x[CH:] for x in by_k], _each(mul, drw, get("eg")))
        dk_1 = _each(_etn, dqk, q)
        dk_2 = _each(_etn, dkk, get("kb"))
        dk = [dk_1[i] + dk_2[i] + dkd[i] * cs[i]["ek"] + dkb[i] * bh[i] for i in range(len(cs))]
        dv = _each(mul, drv, bh)
        dbeta, dg = [], []
        for i, c in enumerate(cs):
            dbeta.append(_rs(drv[i] * v[i]) + _rs(dkb[i] * k[i]) + jnp.zeros((CH, HD), F32))
            t_kd = _rs(dkd[i] * c["kd"])
            dgc = _rs(m[i]) - _rs(m[i].T) + _rs(dqg[i] * c["qg"]) + _rs(drw[i] * c["rw"]) - t_kd
            tail = jnp.sum(t_kd, axis=0, keepdims=True) + c["egl"] * jnp.sum(_rs(s0[i] * ds1[i]), axis=0, keepdims=True)
            dgc = dgc + jnp.where(last, tail, 0.0)
            dg.append(_scan_rows(dgc + jnp.zeros((CH, HD), F32), CH, reverse=True))
        r8 = _iota2((8, HW), 0)
        upd = jnp.zeros((8, HW), F32)
        for j in range(gb):
            sl = slice(j * NH, (j + 1) * NH)
            beta, g, za = gates[j]
            for h in range(NH):
                ds_ref[j, h] = ds_new[j * NH + h]
            dx_ref[j] = jnp.concatenate(dq[sl] + dk[sl] + dv[sl], axis=-1)
            dbeta_j = jnp.where(valid, jnp.concatenate(dbeta[sl], axis=-1), 0.0)
            dg_j = jnp.where(valid, jnp.concatenate(dg[sl], axis=-1), 0.0)
            draw_b = dbeta_j * beta * (1.0 - beta)
            draw_a = dg_j * (-jnp.exp(alog)) * _sig(za)
            dba_ref[j] = _nn(jnp.concatenate([draw_b, draw_a], axis=-1), sm_ref[...]).astype(dba_ref.dtype)
            upd = upd + jnp.where(r8 == 0, jnp.sum(dg_j * g, axis=0, keepdims=True), 0.0) + jnp.where(
                r8 == 1, jnp.sum(draw_a, axis=0, keepdims=True), 0.0)
        acc_ref[...] += upd

    rc = lambda c: nc - 1 - c
    dqkv, dproj3, acc = pl.pallas_call(
        body, grid=(nb // gb, nc),
        in_specs=[pl.BlockSpec((gb, CH, 3 * HW), lambda b, c: (b, rc(c), 0)), pl.BlockSpec((gb, CH, HD), lambda b, c: (b, rc(c), cba)),
                  pl.BlockSpec((HD, 2 * HW), lambda b, c: (0, 0)), pl.BlockSpec((2 * HW, HD), lambda b, c: (0, 0)),
                  pl.BlockSpec((1, HW), lambda b, c: (0, 0)), pl.BlockSpec((1, HW), lambda b, c: (0, 0)),
                  pl.BlockSpec((gb, None, NH, HD, HD), lambda b, c: (b, rc(c), 0, 0, 0)),
                  pl.BlockSpec((gb, None, NH, CH, 3 * HD), lambda b, c: (b, rc(c), 0, 0, 0)),
                  pl.BlockSpec((gb, CH, HW), lambda b, c: (b, rc(c), 0)), pl.BlockSpec(memory_space=pl.ANY)],
        out_specs=[pl.BlockSpec((gb, CH, 3 * HW), lambda b, c: (b, rc(c), 0)), pl.BlockSpec((gb, CH, HD), lambda b, c: (b, rc(c), cba)),
                   pl.BlockSpec((8, HW), lambda b, c: (0, 0))],
        out_shape=[SDS((nb, tp, 3 * HW), F32), SDS((nb, tp, dproj.shape[1]), dproj.dtype), SDS((8, HW), F32)],
        input_output_aliases={9: 1},
        scratch_shapes=[pltpu.VMEM((gb, NH, HD, HD), F32)], compiler_params=_params(2), name="gdn_bwd")(
            qkv.reshape(nb, tp, 3 * HW), proj.reshape(nb, tp, -1), e_mat, s_mat, alog, dtb, states, solved, do.reshape(nb, tp, HW),
            dproj.reshape(nb, tp, -1))
    return dqkv.reshape(n, 3 * HW), dproj3.reshape(dproj.shape), acc


def _hgrn_inputs(zq, zf, lb):
    sg = _sig(zf)
    sgn = _sig(-zf)
    pos = lb > 0.0
    lbp = jnp.where(pos, lb, 0.0)
    fpos = lbp + (1.0 - lbp) * sg
    lf = jnp.where(pos, jnp.log(jnp.where(pos, fpos, 1.0)), _logsig(zf))
    k = (1.0 - lbp) * sgn
    q = _silu(zq) * Q_SCALE
    return q, k, lf, sg, sgn, pos, lbp, fpos


def _hgrn_consts():
    i3, j3 = _iota2((SUB, SUB, HD), 0), _iota2((SUB, SUB, HD), 1)
    return i3 >= j3


def _sum_j(x):
    return jnp.sum(x.reshape(SUB, SUB, HD), axis=1)


def _sum_i(x):
    return jnp.sum(x.reshape(SUB, SUB, HD), axis=0)


def _pairs(a, b):
    return (a[:, None, :] * b[None, :, :]).reshape(SUB * SUB, HD)


def _hgrn_sub(q, k, v, bc, st, consts):
    mask3 = consts
    bl = bc[SUB - 1:SUB, :]
    p3 = jnp.where(mask3, jnp.exp(jnp.where(mask3, bc[:, None, :] - bc[None, :, :], 0.0)), 0.0).reshape(SUB * SUB, HD)
    x = _pairs(q, k) * p3
    srep = _rs(x)
    vt = jnp.broadcast_to(v[None, :, :], (SUB, SUB, HD)).reshape(SUB * SUB, HD)
    eb = jnp.exp(bc)
    qe = q * eb
    o = _hnt(qe, st) + _sum_j(_rr(srep) * _rr(vt))
    ek = jnp.exp(bl - bc)
    kd = k * ek
    ebl = jnp.exp(bl)
    st1 = st * ebl + _htn(v, kd)
    return o, st1, dict(bc=bc, p3=p3, srep=srep, vt=vt, eb=eb, qe=qe, ek=ek, kd=kd, ebl=ebl)


def _hgrn_fwd(proj, lb, lay, nb, nc):
    n = proj.shape[0]
    tp = n // nb
    cbb = lay.c_b // (3 * HW)
    gb = _seqs_per_step(nb)

    def body(z_ref, lb_ref, o_ref, so_ref, s_ref):
        @pl.when(pl.program_id(1) == 0)
        def _():
            s_ref[...] = jnp.zeros_like(s_ref)

        consts = _hgrn_consts()
        for j in range(gb):
            outs = []
            for h in range(NH):
                hs = slice(h * HD, (h + 1) * HD)
                q, k, lf = _hgrn_inputs(z_ref[j, :, hs], z_ref[j, :, HW + h * HD:HW + (h + 1) * HD], lb_ref[:, hs])[:3]
                v = z_ref[j, :, 2 * HW + h * HD:2 * HW + (h + 1) * HD]
                st = s_ref[j, h]
                so_ref[j, h] = st
                bc = _scan_rows(lf, SUB)
                oh = []
                for s in range(CH // SUB):
                    rs = slice(s * SUB, (s + 1) * SUB)
                    o, st, _ = _hgrn_sub(q[rs], k[rs], v[rs], bc[rs], st, consts)
                    oh.append(o)
                s_ref[j, h] = st
                outs.append(jnp.concatenate(oh, axis=0))
            o_ref[j] = jnp.concatenate(outs, axis=-1)

    o, st = pl.pallas_call(
        body, grid=(nb // gb, nc),
        in_specs=[pl.BlockSpec((gb, CH, 3 * HW), lambda b, c: (b, c, cbb)), pl.BlockSpec((1, HW), lambda b, c: (0, 0))],
        out_specs=[pl.BlockSpec((gb, CH, HW), lambda b, c: (b, c, 0)),
                   pl.BlockSpec((gb, None, NH, HD, HD), lambda b, c: (b, c, 0, 0, 0))],
        out_shape=[SDS((nb, tp, HW), F32), SDS((nb, nc, NH, HD, HD), F32)],
        scratch_shapes=[pltpu.VMEM((gb, NH, HD, HD), F32)], compiler_params=_params(2), name="hgrn_fwd")(
            proj.reshape(nb, tp, -1), lb)
    return o.reshape(n, HW), st


def _hgrn_bwd(proj, lb, states, do, dproj, lay, nb, nc):
    n = proj.shape[0]
    tp = n // nb
    cbb = lay.c_b // (3 * HW)
    nsub = CH // SUB
    gb = _seqs_per_step(nb)

    def body(z_ref, lb_ref, st_ref, do_ref, dp_in, dz_ref, acc_ref, ds_ref):
        ci = pl.program_id(1)

        @pl.when(ci == 0)
        def _():
            ds_ref[...] = jnp.zeros_like(ds_ref)

        @pl.when((ci == 0) & (pl.program_id(0) == 0))
        def _():
            acc_ref[...] = jnp.zeros_like(acc_ref)

        upd = jnp.zeros((8, HW), F32)
        for j in range(gb):
            upd = upd + one_seq(j, ci, z_ref, lb_ref, st_ref, do_ref, dz_ref, ds_ref)
        acc_ref[...] += upd

    def one_seq(j, ci, z_ref, lb_ref, st_ref, do_ref, dz_ref, ds_ref):
        consts = _hgrn_consts()
        row = _iota2((CH, 1), 0)
        valid = (row >= PAD) | (ci < nc - 1)
        lastrow = _iota2((SUB, 1), 0) == SUB - 1
        dzq, dzf, dzi, dlbs = [], [], [], []
        for h in range(NH):
            hs = slice(h * HD, (h + 1) * HD)
            zq, zf = z_ref[j, :, hs], z_ref[j, :, HW + h * HD:HW + (h + 1) * HD]
            q, k, lf, sg, sgn, pos, lbp, fpos = _hgrn_inputs(zq, zf, lb_ref[:, hs])
            v = z_ref[j, :, 2 * HW + h * HD:2 * HW + (h + 1) * HD]
            doh = do_ref[j, :, hs]
            sts, fw = [st_ref[j, h]], []
            bc = _scan_rows(lf, SUB)
            for s in range(nsub):
                rs = slice(s * SUB, (s + 1) * SUB)
                _, st1, c = _hgrn_sub(q[rs], k[rs], v[rs], bc[rs], sts[-1], consts)
                sts.append(st1)
                fw.append(c)
            dst = ds_ref[j, h]
            dq_l, dk_l, dv_l, dlf_l = [None] * nsub, [None] * nsub, [None] * nsub, [None] * nsub
            for s in reversed(range(nsub)):
                rs = slice(s * SUB, (s + 1) * SUB)
                c, st = fw[s], sts[s]
                qs, ks, vs, dos = q[rs], k[rs], v[rs], doh[rs]
                dqe = _hnn(dos, st)
                dkd = _hnn(vs, dst)
                dsrep = _rs(_pairs(_rr(dos), _rr(vs)))
                w = dsrep * c["p3"]
                kt = jnp.broadcast_to(ks[None, :, :], (SUB, SUB, HD)).reshape(SUB * SUB, HD)
                qt = jnp.broadcast_to(qs[:, None, :], (SUB, SUB, HD)).reshape(SUB * SUB, HD)
                dq_i = _sum_j(w * kt)
                dk_i = _sum_i(w * qt)
                dot = jnp.broadcast_to(_rr(dos)[:, None, :], (SUB, SUB, HD)).reshape(SUB * SUB, HD)
                dvv = _sum_i(_rr(c["srep"]) * dot) + _hnt(c["kd"], dst)
                t_kd = dkd * c["kd"]
                dbc = dqe * c["qe"] - t_kd + qs * dq_i - ks * dk_i
                tail = jnp.sum(t_kd, axis=0, keepdims=True) + c["ebl"] * jnp.sum(st * dst, axis=0, keepdims=True)
                dbc = dbc + jnp.where(lastrow, tail, 0.0)
                dlf_l[s] = dbc
                dq_l[s] = dq_i + dqe * c["eb"]
                dk_l[s] = dk_i + dkd * c["ek"]
                dv_l[s] = dvv
                dst = _htn(dos, c["qe"]) + dst * c["ebl"]
            ds_ref[j, h] = dst
            dq, dk, dv, dbc = (jnp.concatenate(t, axis=0) for t in (dq_l, dk_l, dv_l, dlf_l))
            dlf = _scan_rows(dbc, SUB, reverse=True)
            dlft = dlf - dk * (1.0 - k)
            dlf_dz = jnp.where(pos, (1.0 - lbp) * sg * sgn / jnp.where(pos, fpos, 1.0), sgn)
            dlf_dlb = jnp.where(pos, sgn / jnp.where(pos, fpos, 1.0), 0.0)
            dzq.append(dq * Q_SCALE * _dsilu(zq))
            dzf.append(dlft * dlf_dz)
            dzi.append(dv)
            dlbs.append(jnp.sum(jnp.where(valid, dlft * dlf_dlb, 0.0), axis=0, keepdims=True))
        dz_ref[j] = jnp.concatenate(dzq + dzf + dzi, axis=-1).astype(dz_ref.dtype)
        return jnp.where(_iota2((8, HW), 0) == 0, jnp.concatenate(dlbs, axis=-1), 0.0)

    rc = lambda c: nc - 1 - c
    dproj3, acc = pl.pallas_call(
        body, grid=(nb // gb, nc),
        in_specs=[pl.BlockSpec((gb, CH, 3 * HW), lambda b, c: (b, rc(c), cbb)), pl.BlockSpec((1, HW), lambda b, c: (0, 0)),
                  pl.BlockSpec((gb, None, NH, HD, HD), lambda b, c: (b, rc(c), 0, 0, 0)),
                  pl.BlockSpec((gb, CH, HW), lambda b, c: (b, rc(c), 0)), pl.BlockSpec(memory_space=pl.ANY)],
        out_specs=[pl.BlockSpec((gb, CH, 3 * HW), lambda b, c: (b, rc(c), cbb)), pl.BlockSpec((8, HW), lambda b, c: (0, 0))],
        out_shape=[SDS((nb, tp, dproj.shape[1]), dproj.dtype), SDS((8, HW), F32)],
        input_output_aliases={4: 0},
        scratch_shapes=[pltpu.VMEM((gb, NH, HD, HD), F32)], compiler_params=_params(2), name="hgrn_bwd")(
            proj.reshape(nb, tp, -1), lb, states, do.reshape(nb, tp, HW), dproj.reshape(nb, tp, -1))
    return dproj3.reshape(dproj.shape), acc


def _gated_norm(o, z, gamma):
    ys, ns, rs = [], [], []
    for h in range(NH):
        hs = slice(h * HD, (h + 1) * HD)
        oh = o[:, hs]
        r = lax.rsqrt(jnp.mean(oh * oh, axis=-1, keepdims=True) + EPS)
        nh = oh * r
        ys.append(nh * gamma * _silu(z[:, hs]))
        ns.append(nh)
        rs.append(r)
    return jnp.concatenate(ys, axis=-1), ns, rs


def _merge_fwd(h, oa, ob, proj, ga, gb, wa, wb, wo, lay):
    n, d = h.shape
    tm = _tile(n, 384)
    wm = lay.wm

    def body(h_ref, oa_ref, ob_ref, p_ref, ga_ref, gb_ref, wa_ref, wb_ref, wo_ref, out_ref):
        ya, _, _ = _gated_norm(oa_ref[...], p_ref[:, 0:HW], ga_ref[...])
        yb, _, _ = _gated_norm(ob_ref[...], p_ref[:, HW:2 * HW], gb_ref[...])
        ya2 = _bnn(ya, wa_ref[...])
        yb2 = _bnn(yb, wb_ref[...])
        mixed = _sig(p_ref[:, 2 * HW:2 * HW + d]) * ya2 + _sig(p_ref[:, 2 * HW + d:2 * HW + 2 * d]) * yb2
        out_ref[...] = h_ref[...] + _bnn(mixed, wo_ref[...])

    full = lambda shape: pl.BlockSpec(shape, lambda i: (0, 0))
    return pl.pallas_call(
        body, grid=(n // tm,),
        in_specs=[pl.BlockSpec((tm, d), lambda i: (i, 0)), pl.BlockSpec((tm, HW), lambda i: (i, 0)),
                  pl.BlockSpec((tm, HW), lambda i: (i, 0)), pl.BlockSpec((tm, wm), lambda i: (i, 0)),
                  full((1, HD)), full((1, HD)), full((HW, d)), full((HW, d)), full((d, d))],
        out_specs=pl.BlockSpec((tm, d), lambda i: (i, 0)), out_shape=SDS((n, d), F32),
        compiler_params=_params(1), name="merge_fwd")(h, oa, ob, proj, ga, gb, wa, wb, wo)


def _gated_norm_bwd(dy, o, z, gamma):
    dos, dzs = [], []
    dgam = jnp.zeros((1, HD), F32)
    for h in range(NH):
        hs = slice(h * HD, (h + 1) * HD)
        oh, zh, dyh = o[:, hs], z[:, hs], dy[:, hs]
        r = lax.rsqrt(jnp.mean(oh * oh, axis=-1, keepdims=True) + EPS)
        nh = oh * r
        dzs.append(dyh * nh * gamma * _dsilu(zh))
        dng = dyh * _silu(zh)
        dgam = dgam + jnp.sum(dng * nh, axis=0, keepdims=True)
        dn = dng * gamma
        dos.append(r * (dn - nh * jnp.mean(dn * nh, axis=-1, keepdims=True)))
    return jnp.concatenate(dos, axis=-1), jnp.concatenate(dzs, axis=-1), dgam


def _merge_bwd(dhn, oa, ob, proj, ga, gb, wa, wb, wo, lay, tp):
    n, d = dhn.shape
    tm = _tile(n, 256)
    wm = lay.wm

    def body(dh_ref, oa_ref, ob_ref, p_ref, ga_ref, gb_ref, wa_ref, wb_ref, wo_ref,
             dp_ref, doa_ref, dob_ref, dwa_ref, dwb_ref, dwo_ref, dga_ref, dgb_ref):
        i = pl.program_id(0)

        @pl.when(i == 0)
        def _():
            for r in (dwa_ref, dwb_ref, dwo_ref, dga_ref, dgb_ref):
                r[...] = jnp.zeros_like(r)

        dh = jnp.where(_row_valid(tm, tp, i * tm), dh_ref[...], 0.0)
        oa, ob = oa_ref[...], ob_ref[...]
        za, zb = p_ref[:, 0:HW], p_ref[:, HW:2 * HW]
        gta, gtb = p_ref[:, 2 * HW:2 * HW + d], p_ref[:, 2 * HW + d:2 * HW + 2 * d]
        ya, _, _ = _gated_norm(oa, za, ga_ref[...])
        yb, _, _ = _gated_norm(ob, zb, gb_ref[...])
        ya16, yb16, dh16 = ya.astype(BF16), yb.astype(BF16), dh.astype(BF16)
        ya2 = _bnn(ya16, wa_ref[...])
        yb2 = _bnn(yb16, wb_ref[...])
        sa, sb = _sig(gta), _sig(gtb)
        mixed = sa * ya2 + sb * yb2
        dmixed = _bnt(dh16, wo_ref[...])
        dwo_ref[...] += _btn(mixed, dh16)
        dya2 = dmixed * sa
        dyb2 = dmixed * sb
        dya16, dyb16 = dya2.astype(BF16), dyb2.astype(BF16)
        dwa_ref[...] += _btn(ya16, dya16)
        dwb_ref[...] += _btn(yb16, dyb16)
        doa, dza, dga = _gated_norm_bwd(_bnt(dya16, wa_ref[...]), oa, za, ga_ref[...])
        dob, dzb, dgb = _gated_norm_bwd(_bnt(dyb16, wb_ref[...]), ob, zb, gb_ref[...])
        dga_ref[...] += dga
        dgb_ref[...] += dgb
        doa_ref[...] = doa
        dob_ref[...] = dob
        dt = dp_ref.dtype
        dp_ref[:, 0:HW] = dza.astype(dt)
        dp_ref[:, HW:2 * HW] = dzb.astype(dt)
        dp_ref[:, 2 * HW:2 * HW + d] = (dmixed * ya2 * sa * (1.0 - sa)).astype(dt)
        dp_ref[:, 2 * HW + d:2 * HW + 2 * d] = (dmixed * yb2 * sb * (1.0 - sb)).astype(dt)

    full = lambda shape: pl.BlockSpec(shape, lambda i: (0, 0))
    rows = lambda w: pl.BlockSpec((tm, w), lambda i: (i, 0))
    return pl.pallas_call(
        body, grid=(n // tm,),
        in_specs=[rows(d), rows(HW), rows(HW), rows(wm), full((1, HD)), full((1, HD)), full((HW, d)), full((HW, d)), full((d, d))],
        out_specs=[rows(wm), rows(HW), rows(HW), full((HW, d)), full((HW, d)), full((d, d)), full((1, HD)), full((1, HD))],
        out_shape=[SDS((n, lay.pw), BF16), SDS((n, HW), F32), SDS((n, HW), F32), SDS((HW, d), F32), SDS((HW, d), F32),
                   SDS((d, d), F32), SDS((1, HD), F32), SDS((1, HD), F32)],
        compiler_params=_params(1), name="merge_bwd")(dhn, oa, ob, proj, ga, gb, wa, wb, wo)


def _loss_head(h, target, fw, nb, tp):
    n, d = h.shape
    tr = _tile(tp, 768)
    nr = tp // tr

    def body(h_ref, t_ref, fw_ref, lp_ref, dh_ref, dfw_ref):
        b, i = pl.program_id(0), pl.program_id(1)

        @pl.when((b == 0) & (i == 0))
        def _():
            dfw_ref[...] = jnp.zeros_like(dfw_ref)

        x = h_ref[...]
        r = lax.rsqrt(jnp.mean(x * x, axis=-1, keepdims=True) + EPS)
        xh = x * r
        live = i * tr + _iota2((tr, 1), 0) >= CH
        err = jnp.where(live, xh * fw_ref[...] - t_ref[...], 0.0)
        lp_ref[...] = jnp.zeros_like(lp_ref) + 0.5 * jnp.sum(_rs(err * err), axis=0, keepdims=True) / d
        dy = err / d
        dfw_ref[...] += jnp.sum(dy * xh, axis=0, keepdims=True)
        dxh = dy * fw_ref[...]
        dh_ref[...] = r * (dxh - xh * jnp.mean(dxh * xh, axis=-1, keepdims=True))

    rows = pl.BlockSpec((tr, d), lambda b, i: (b * nr + i, 0))
    return pl.pallas_call(
        body, grid=(nb, nr), in_specs=[rows, rows, pl.BlockSpec((1, d), lambda b, i: (0, 0))],
        out_specs=[pl.BlockSpec((8, HD), lambda b, i: (b * nr + i, 0)), rows, pl.BlockSpec((1, d), lambda b, i: (0, 0))],
        out_shape=[SDS((nb * nr * 8, HD), F32), SDS((n, d), F32), SDS((1, d), F32)],
        compiler_params=_params(2), name="loss_head")(h, target, fw)


def _lb_fwd(lb):
    def body(x_ref, o_ref):
        x = x_ref[...]
        mx = jnp.max(x, axis=0, keepdims=True)
        e = jnp.exp(x - mx)
        sm = e / jnp.sum(e, axis=0, keepdims=True)
        run = jnp.zeros((1, HW), F32)
        for l in range(DEPTH):
            run = run + sm[l:l + 1, :]
            o_ref[l:l + 1, :] = run - sm[0:1, :]

    return pl.pallas_call(body, out_shape=SDS(lb.shape, F32), name="lb_fwd")(lb)


def _lb_bwd(lb, dlb_all):
    def body(x_ref, d_ref, o_ref):
        x = x_ref[...]
        dl = d_ref[...]
        mx = jnp.max(x, axis=0, keepdims=True)
        e = jnp.exp(x - mx)
        sm = e / jnp.sum(e, axis=0, keepdims=True)
        tot = jnp.sum(dl, axis=0, keepdims=True)
        dsm = []
        run = tot
        for l in range(DEPTH):
            dsm.append(run - (tot if l == 0 else 0.0))
            run = run - dl[l:l + 1, :]
        inner = sum(sm[l:l + 1, :] * dsm[l] for l in range(DEPTH))
        for l in range(DEPTH):
            o_ref[l:l + 1, :] = sm[l:l + 1, :] * (dsm[l] - inner)

    return pl.pallas_call(body, out_shape=SDS(lb.shape, F32), name="lb_bwd")(lb, dlb_all)


def _adamw(g, w, m, v):
    r, c = g.shape[-2:]
    tr = _tile(r, 264) if g.ndim == 2 else None
    c1 = 1.0 / (1.0 - ADAM_B1 ** ADAM_STEP)
    c2 = 1.0 / (1.0 - ADAM_B2 ** ADAM_STEP)

    def body(g_ref, w_ref, m_ref, v_ref, d_ref, mo_ref, vo_ref):
        gg = g_ref[...]
        mn = ADAM_B1 * m_ref[...] + (1.0 - ADAM_B1) * gg
        vn = ADAM_B2 * v_ref[...] + (1.0 - ADAM_B2) * gg * gg
        d_ref[...] = -ADAM_LR * ((mn * c1) / (jnp.sqrt(vn * c2) + ADAM_EPS) + ADAM_WD * w_ref[...])
        mo_ref[...] = mn
        vo_ref[...] = vn

    if g.ndim == 3:
        spec = pl.BlockSpec(g.shape[:2] + (HD,), lambda i: (0, 0, i))
        steps = g.shape[2] // HD
    else:
        spec = pl.BlockSpec((tr, c), lambda i: (i, 0))
        steps = r // tr
    return pl.pallas_call(body, grid=(steps,), in_specs=[spec] * 4, out_specs=[spec] * 3, out_shape=[SDS(g.shape, F32)] * 3,
                          compiler_params=_params(1), name="adamw")(g, w, m, v)


def _tile16(n, target):
    return _tile(n // 2, target // 2) * 2 if n % 16 == 0 else _tile(n, target)


def _add_cores(g, got, core):
    k, r, c = got.shape
    tr = _tile16(r, 264)

    def body(c_ref, a_ref, b_ref, o_ref):
        o_ref[...] = (a_ref[...] + b_ref[...].astype(F32)).astype(o_ref.dtype)

    spec = pl.BlockSpec((None, tr, c), lambda s, i, cr: (s, i, 0))
    return pl.pallas_call(
        body, grid_spec=pltpu.PrefetchScalarGridSpec(
            num_scalar_prefetch=1, grid=(k, r // tr),
            in_specs=[pl.BlockSpec((None, None, tr, c), lambda s, i, cr: (cr[0], s, i, 0)), spec], out_specs=spec),
        out_shape=SDS(got.shape, got.dtype), compiler_params=_params(2), name="add_cores")(core, g, got)


def _sum_chips(parts, own, chip, core):
    k, r, c = parts.shape
    tr = _tile16(r, 264)

    def body(chip_ref, core_ref, *refs):
        part_refs, own_ref, o_ref = refs[:k], refs[k], refs[k + 1]
        mine = own_ref[...].astype(F32)
        acc = None
        for s in range(k):
            term = jnp.where(chip_ref[0] == s, mine, part_refs[s][...].astype(F32))
            acc = term if acc is None else acc + term
        o_ref[...] = acc

    def other(s):
        return pl.BlockSpec((None, tr, c), lambda i, ch, co: (jnp.where(ch[0] == s, (s + 1) % k, s), i, 0))

    return pl.pallas_call(
        body, grid_spec=pltpu.PrefetchScalarGridSpec(
            num_scalar_prefetch=2, grid=(r // tr,),
            in_specs=[other(s) for s in range(k)] + [pl.BlockSpec((None, tr, c), lambda i, ch, co: (ch[0], i, 0))],
            out_specs=pl.BlockSpec((None, tr, c), lambda i, ch, co: (co[0], i, 0))),
        out_shape=SDS((2, r, c), F32), compiler_params=_params(1), name="sum_chips")(chip, core, *([parts] * k), own)


def _meta_grad(dh, nb, nc):
    d = dh.shape[1]

    def body(x_ref, o_ref):
        @pl.when(pl.program_id(0) == 0)
        def _():
            o_ref[...] = jnp.zeros_like(o_ref)

        o_ref[...] += x_ref[PAD:CH, :]

    return pl.pallas_call(body, grid=(nb,), in_specs=[pl.BlockSpec((CH, d), lambda b: (b * nc, 0))],
                          out_specs=pl.BlockSpec((N_META, d), lambda b: (0, 0)), out_shape=SDS((N_META, d), F32),
                          compiler_params=_params(1), name="meta_grad")(dh)


ANY = pl.BlockSpec(memory_space=pl.ANY)


def _place():
    x, y, c = lax.axis_index("x"), lax.axis_index("y"), lax.axis_index("c")
    chips = [(1 - x, y), (x, 1 - y), (1 - x, 1 - y)]
    return x, y, c, chips


def _remote(src, dst, send_sems, recv_sems, k, to):
    return pltpu.make_async_remote_copy(src_ref=src, dst_ref=dst, send_sem=send_sems.at[k], recv_sem=recv_sems.at[k],
                                        device_id=to, device_id_type=MESH)


def _gather_weights(pbs, ps):
    nt = len(pbs)

    def body(*refs):
        pb_refs, ps_ref, gb_refs, gs_ref = refs[:nt], refs[nt], refs[nt + 1:2 * nt + 1], refs[2 * nt + 1]
        send_sems, recv_sems, local_sems = refs[2 * nt + 2:]
        x, y, c, chips = _place()
        s = 2 * x + y
        sib = (x, y, 1 - c)
        l1 = pltpu.make_async_copy(ps_ref, gs_ref.at[s], local_sems.at[0])
        l1.start()
        sends = []
        for k, (px, py) in enumerate(chips):
            for t in range(nt):
                sends.append(_remote(pb_refs[t].at[c], gb_refs[t].at[s, c], send_sems, recv_sems, 6 * t + k, (px, py, c)))
            sends.append(_remote(ps_ref, gs_ref.at[s], send_sems, recv_sems, 6 * nt + k, (px, py, c)))
        for cp in sends:
            cp.start()
        for k, (px, py) in enumerate(chips):
            sk = 2 * px + py
            for t in range(nt):
                _remote(pb_refs[t].at[c], gb_refs[t].at[sk, c], send_sems, recv_sems, 6 * t + k, sib).wait_recv()
                fwd = _remote(gb_refs[t].at[sk, c], gb_refs[t].at[sk, c], send_sems, recv_sems, 6 * t + 3 + k, sib)
                fwd.start()
                sends.append(fwd)
        for k, (px, py) in enumerate(chips):
            sk = 2 * px + py
            for t in range(nt):
                _remote(pb_refs[t].at[c], gb_refs[t].at[sk, 1 - c], send_sems, recv_sems, 6 * t + 3 + k, sib).wait_recv()
            _remote(ps_ref, gs_ref.at[sk], send_sems, recv_sems, 6 * nt + k, sib).wait_recv()
        for cp in sends:
            cp.wait_send()
        l1.wait()

    nsem = 6 * nt + 3
    out = pl.pallas_call(
        body, in_specs=[ANY] * (nt + 1), out_specs=[ANY] * (nt + 1),
        out_shape=[SDS((4,) + pb.shape, pb.dtype) for pb in pbs] + [SDS((4,) + ps.shape, ps.dtype)],
        scratch_shapes=[pltpu.SemaphoreType.DMA((nsem,)), pltpu.SemaphoreType.DMA((nsem,)), pltpu.SemaphoreType.DMA((1,))],
        name="gather_weights")(*pbs, ps)
    return out[:nt], out[nt]


def _contain(wpad, shift):
    r, cw = wpad.shape
    tr = _tile16(r, 256)

    def body(n_ref, x_ref, o_ref):
        o_ref[...] = pltpu.roll(x_ref[...], n_ref[0], axis=1).astype(o_ref.dtype)

    spec = pl.BlockSpec((tr, cw), lambda i, n: (i, 0))
    return pl.pallas_call(
        body, grid_spec=pltpu.PrefetchScalarGridSpec(num_scalar_prefetch=1, grid=(r // tr,), in_specs=[spec], out_specs=spec),
        out_shape=SDS((r, cw), BF16), compiler_params=_params(1), name="contain")(shift, wpad)


def _place_own(gb, pb, chip):
    _, _, r, c = gb.shape
    tr = _tile16(r, 1100)

    def body(s_ref, p_ref, g_in, o_ref):
        o_ref[...] = p_ref[...]

    return pl.pallas_call(
        body, grid_spec=pltpu.PrefetchScalarGridSpec(
            num_scalar_prefetch=1, grid=(2, r // tr),
            in_specs=[pl.BlockSpec((None, tr, c), lambda h, i, s: (h, i, 0)), ANY],
            out_specs=pl.BlockSpec((None, None, tr, c), lambda h, i, s: (s[0], h, i, 0))),
        out_shape=SDS(gb.shape, gb.dtype), input_output_aliases={2: 0}, compiler_params=_params(2),
        name="place_own")(chip, pb, gb)


def _sem_scratch(n_remote, n_local):
    return [pltpu.SemaphoreType.DMA((n_remote,)), pltpu.SemaphoreType.DMA((n_remote,)), pltpu.SemaphoreType.DMA((n_local,))]


def _swap_halves(sends):
    nt = len(sends)

    def body(*refs):
        s_refs, got_refs = refs[:nt], refs[nt:2 * nt]
        send_sems, recv_sems = refs[2 * nt:]
        x, y, c, _ = _place()
        sib = (x, y, 1 - c)
        remote = [_remote(s_refs[t].at[1 - c, s], got_refs[t].at[s], send_sems, recv_sems, 4 * t + s, sib)
                  for t in range(nt) for s in range(4)]
        for cp in remote:
            cp.start()
        for cp in remote:
            cp.wait()

    return pl.pallas_call(
        body, in_specs=[ANY] * nt, out_specs=[ANY] * nt, out_shape=[SDS(g.shape[1:], g.dtype) for g in sends],
        scratch_shapes=[pltpu.SemaphoreType.DMA((4 * nt,)), pltpu.SemaphoreType.DMA((4 * nt,))], name="swap_halves")(*sends)


def _scatter_chip_sums(parts):
    nt = len(parts)

    def body(*refs):
        a_refs, r_refs = refs[:nt], refs[nt:2 * nt]
        send_sems, recv_sems = refs[2 * nt:]
        x, y, c, chips = _place()
        s = 2 * x + y
        sends = [_remote(a_refs[t].at[2 * px + py], r_refs[t].at[s], send_sems, recv_sems, 3 * t + k, (px, py, c))
                 for t in range(nt) for k, (px, py) in enumerate(chips)]
        for cp in sends:
            cp.start()
        for t in range(nt):
            for k, (px, py) in enumerate(chips):
                _remote(a_refs[t].at[s], r_refs[t].at[2 * px + py], send_sems, recv_sems, 3 * t + k, (px, py, c)).wait_recv()
        for cp in sends:
            cp.wait_send()

    return pl.pallas_call(
        body, in_specs=[ANY] * nt, out_specs=[ANY] * nt, out_shape=[SDS(a.shape, a.dtype) for a in parts],
        scratch_shapes=[pltpu.SemaphoreType.DMA((3 * nt,)), pltpu.SemaphoreType.DMA((3 * nt,))],
        name="scatter_chip_sums")(*parts)


def _join_halves(fs):
    nt = len(fs)

    def body(*refs):
        f_refs = refs[nt:2 * nt]
        send_sems, recv_sems = refs[2 * nt:]
        x, y, c, _ = _place()
        sib = (x, y, 1 - c)
        sends = [_remote(f_refs[t].at[c], f_refs[t].at[c], send_sems, recv_sems, t, sib) for t in range(nt)]
        for cp in sends:
            cp.start()
        for t in range(nt):
            _remote(f_refs[t].at[c], f_refs[t].at[1 - c], send_sems, recv_sems, t, sib).wait_recv()
        for cp in sends:
            cp.wait_send()

    return pl.pallas_call(
        body, in_specs=[ANY] * nt, out_specs=[ANY] * nt, out_shape=[SDS(f.shape, f.dtype) for f in fs],
        input_output_aliases={t: t for t in range(nt)},
        scratch_shapes=[pltpu.SemaphoreType.DMA((nt,)), pltpu.SemaphoreType.DMA((nt,))], name="join_halves")(*fs)


def _uncontain(cont, n_head, width):
    r, cw = cont.shape
    tr = _tile(r, 256)

    def body(n_ref, x_ref, o_ref):
        o_ref[...] = pltpu.roll(x_ref[...], n_ref[0], axis=1)[:, :width]

    return pl.pallas_call(
        body, grid_spec=pltpu.PrefetchScalarGridSpec(
            num_scalar_prefetch=1, grid=(r // tr,), in_specs=[pl.BlockSpec((tr, cw), lambda i, n: (i, 0))],
            out_specs=pl.BlockSpec((tr, width), lambda i, n: (i, 0))),
        out_shape=SDS((r, width), F32), compiler_params=_params(1), name="uncontain")(n_head, cont)


WEIGHTS = ("meta_tokens", "norm_w", "w_in", "conv_w", "a_log", "dt_bias", "gnorm_a", "gnorm_b", "hgrn_lower_bounds",
           "w_branch_a", "w_branch_b", "w_out", "final_norm_w")
SHARD_AXIS = {"meta_tokens": 1, "w_in": 2, "conv_w": 2, "w_branch_a": 2, "w_branch_b": 2, "w_out": 1}
FLAT_C = 1024


def _flat(parts, rows, cols=FLAT_C):
    v = jnp.concatenate([p.reshape(-1) for p in parts])
    return jnp.pad(v, (0, rows * cols - v.shape[0])).reshape(rows, cols)


def _local_step(x, target, w, lay):
    nb, seq, d = x.shape
    tp = CH + seq
    nc = tp // CH
    n = nb * tp
    e_mat, s_mat = _gate_consts()
    lb_all = _lb_fwd(w["hgrn_lower_bounds"])
    h = jnp.concatenate([jnp.zeros((nb, PAD, d), F32), jnp.broadcast_to(w["meta_tokens"][None], (nb, N_META, d)), x],
                        axis=1).reshape(n, d)
    rep = lambda a: jnp.repeat(a, HD)[None, :]
    saved = []
    for l in range(DEPTH):
        nw = w["norm_w"][l][None, :]
        proj, xn = _norm_proj_fwd(h, nw, w["w_in"][l])
        qkv = _gdn_prep_fwd(proj, w["conv_w"][l], lay, nb, tp)
        alog, dtb = rep(w["a_log"][l]), rep(w["dt_bias"][l])
        oa, sa, sva = _gdn_fwd(qkv, proj, e_mat, alog, dtb, lay, nb, nc)
        lbl = lb_all[l][None, :]
        ob, sb = _hgrn_fwd(proj, lbl, lay, nb, nc)
        ga, gb = w["gnorm_a"][l][None, :], w["gnorm_b"][l][None, :]
        hn = _merge_fwd(h, oa, ob, proj, ga, gb, w["w_branch_a"][l], w["w_branch_b"][l], w["w_out"][l], lay)
        saved.append((h, nw, proj, qkv, alog, dtb, oa, sa, lbl, ob, sb, ga, gb, xn, sva))
        h = hn
    target_p = jnp.pad(target, ((0, 0), (CH, 0), (0, 0))).reshape(n, d)
    lp, dh, dfw = _loss_head(h, target_p, w["final_norm_w"][None, :], nb, tp)
    loss = jnp.sum(lp[::8, 0])
    g = {n_: [None] * DEPTH for n_ in WEIGHTS}
    dlb_all = [None] * DEPTH
    for l in reversed(range(DEPTH)):
        h, nw, proj, qkv, alog, dtb, oa, sa, lbl, ob, sb, ga, gb, xn, sva = saved[l]
        dproj, doa, dob, dwa, dwb, dwo, dga, dgb = _merge_bwd(dh, oa, ob, proj, ga, gb, w["w_branch_a"][l],
                                                             w["w_branch_b"][l], w["w_out"][l], lay, tp)
        dproj, acc_b = _hgrn_bwd(proj, lbl, sb, dob, dproj, lay, nb, nc)
        dqkv, dproj, acc_a = _gdn_bwd(qkv, proj, e_mat, s_mat, alog, dtb, sa, sva, doa, dproj, lay, nb, nc)
        dproj, dconv = _gdn_prep_bwd(proj, w["conv_w"][l], dqkv, dproj, lay, nb, tp)
        dh, dnw = _proj_bwd_dx(dproj, w["w_in"][l], h, nw, dh, tp)
        g["w_in"][l] = _proj_bwd_dw(dproj, xn, tp)
        g["norm_w"][l] = dnw[0]
        g["conv_w"][l] = dconv
        g["a_log"][l] = acc_a[0, ::HD]
        g["dt_bias"][l] = acc_a[1, ::HD]
        g["gnorm_a"][l], g["gnorm_b"][l] = dga[0], dgb[0]
        g["w_branch_a"][l], g["w_branch_b"][l], g["w_out"][l] = dwa, dwb, dwo
        dlb_all[l] = acc_b[0]
    grads = {n_: jnp.stack(v) for n_, v in g.items() if v[0] is not None}
    grads["hgrn_lower_bounds"] = _lb_bwd(w["hgrn_lower_bounds"], jnp.stack(dlb_all))
    grads["final_norm_w"] = dfw[0]
    grads["meta_tokens"] = _meta_grad(dh, nb, nc)
    grad_x = dh.reshape(nb, tp, d)[:, CH:, :]
    return loss, grad_x, grads


def kernel(x, meta_tokens, norm_w, w_in, conv_w, a_log, dt_bias, gnorm_a, gnorm_b, hgrn_lower_bounds, w_branch_a, w_branch_b, w_out, final_norm_w, loss_target, m_meta_tokens, m_norm_w, m_w_in, m_conv_w, m_a_log, m_dt_bias, m_gnorm_a, m_gnorm_b, m_hgrn_lower_bounds, m_w_branch_a, m_w_branch_b, m_w_out, m_final_norm_w, v_meta_tokens, v_norm_w, v_w_in, v_conv_w, v_a_log, v_dt_bias, v_gnorm_a, v_gnorm_b, v_hgrn_lower_bounds, v_w_branch_a, v_w_branch_b, v_w_out, v_final_norm_w):
    wl = dict(meta_tokens=meta_tokens, norm_w=norm_w, w_in=w_in, conv_w=conv_w, a_log=a_log, dt_bias=dt_bias, gnorm_a=gnorm_a,
              gnorm_b=gnorm_b, hgrn_lower_bounds=hgrn_lower_bounds, w_branch_a=w_branch_a, w_branch_b=w_branch_b, w_out=w_out,
              final_norm_w=final_norm_w)
    ml = dict(zip(WEIGHTS, (m_meta_tokens, m_norm_w, m_w_in, m_conv_w, m_a_log, m_dt_bias, m_gnorm_a, m_gnorm_b,
                            m_hgrn_lower_bounds, m_w_branch_a, m_w_branch_b, m_w_out, m_final_norm_w)))
    vl = dict(zip(WEIGHTS, (v_meta_tokens, v_norm_w, v_w_in, v_conv_w, v_a_log, v_dt_bias, v_gnorm_a, v_gnorm_b,
                            v_hgrn_lower_bounds, v_w_branch_a, v_w_branch_b, v_w_out, v_final_norm_w)))
    d = x.shape[2]
    lay = _Layout(d)
    nchip = 4

    big = ("w_in", "w_branch_a", "w_branch_b", "w_out")
    small = ("conv_w", "meta_tokens")
    table, heads, cw = lay.pieces(nchip)
    sw = wl["w_in"].shape[2]
    chip_id = (2 * lax.axis_index("x") + lax.axis_index("y")).astype(jnp.int32)
    n_head = sum(jnp.where(chip_id == s, heads[s], 0) for s in range(nchip)).astype(jnp.int32)
    w_pad = jnp.pad(wl["w_in"], ((0, 0), (0, 0), (0, cw - sw))).reshape(DEPTH * d, cw)
    shift = jnp.where(n_head == 0, 0, cw - n_head).astype(jnp.int32).reshape(1)
    pbs = [_contain(w_pad, shift).reshape(DEPTH, d, cw)] + [wl[n].astype(BF16) for n in big[1:]]
    nsmall = sum(int(np.prod(wl[n].shape)) for n in small)
    rs = -(-nsmall // (HD * 8)) * 8
    ps = jnp.pad(jnp.concatenate([wl[n].reshape(-1) for n in small]), (0, rs * HD - nsmall)).reshape(rs, HD)
    gbig, gsmall = _gather_weights(pbs, ps)
    gbig = [_place_own(g, p, chip_id.reshape(1)) for g, p in zip(gbig, pbs)]
    gsmall = gsmall.reshape(nchip, -1)

    wf = dict(wl)
    wf["w_in"] = lay.from_containers([gbig[0][s] for s in range(nchip)])
    for i, n in enumerate(big[1:], start=1):
        wf[n] = jnp.concatenate([gbig[i][s] for s in range(nchip)], axis=SHARD_AXIS[n])
    o = 0
    for n in small:
        sz = int(np.prod(wl[n].shape))
        a = gsmall[:, o:o + sz].reshape((nchip,) + wl[n].shape)
        wf[n] = jnp.concatenate([a[s] for s in range(nchip)], axis=SHARD_AXIS[n])
        o += sz

    loss_part, grad_x, gfull = _local_step(x, loss_target, wf, lay)
    loss = lax.psum(loss_part, ("x", "y", "c"))

    sw = wl["w_in"].shape[2]
    conts, heads = lay.containers(gfull["w_in"], nchip)
    dd = wl["w_branch_a"].shape[2]
    rows_o = wl["w_out"].shape[1]
    by_dest = lambda g, n: [lax.slice_in_dim(g, s * wl[n].shape[SHARD_AXIS[n]], (s + 1) * wl[n].shape[SHARD_AXIS[n]],
                                            axis=SHARD_AXIS[n]) if n in SHARD_AXIS else g for s in range(nchip)]
    small_names = tuple(n for n in WEIGHTS if n not in big)
    nsm = sum(int(np.prod(wl[n].shape)) for n in small_names)
    rsm = -(-nsm // (2 * HD * 8)) * 8
    pack_small = lambda parts: _flat(parts, 2 * rsm, HD).reshape(2, rsm, HD)
    small_by_dest = [by_dest(gfull[n], n) for n in small_names]
    gs = [jnp.stack(conts, axis=1),
          jnp.stack(by_dest(gfull["w_branch_a"], "w_branch_a"), axis=1),
          jnp.stack(by_dest(gfull["w_branch_b"], "w_branch_b"), axis=1),
          gfull["w_out"].reshape(DEPTH, nchip, rows_o, d),
          jnp.stack([pack_small([p[s] for p in small_by_dest]) for s in range(nchip)], axis=1)]
    gs = [g.reshape((2, nchip, -1, g.shape[-1])) for g in gs]
    my_chip = (2 * lax.axis_index("x") + lax.axis_index("y")).astype(jnp.int32)
    my_core = lax.axis_index("c").astype(jnp.int32)
    got = _swap_halves([g.astype(BF16) for g in gs[:4]] + gs[4:])
    chip_sums = [_add_cores(g, b, my_core.reshape(1)) for g, b in zip(gs, got)]
    by_chip = _scatter_chip_sums(chip_sums)
    full = _join_halves([_sum_chips(p, a, my_chip.reshape(1), my_core.reshape(1)) for p, a in zip(by_chip, chip_sums)])
    n_head = sum(jnp.where(my_chip == s, heads[s], 0) for s in range(nchip)).astype(jnp.int32).reshape(1)
    g_w_in = _uncontain(full[0].reshape(DEPTH * d, -1), n_head, sw)
    g2 = {"w_in": g_w_in, "w_branch_a": full[1].reshape(-1, dd), "w_branch_b": full[2].reshape(-1, dd),
          "w_out": full[3].reshape(-1, d), "small": full[4].reshape(2 * rsm, HD)}

    def two_d(src, n):
        if n == "small":
            return _flat([src[k] for k in small_names], 2 * rsm, HD)
        return src[n].reshape(g2[n].shape)

    outs = {}
    for n in big[1:] + ("small",):
        delta, mnew, vnew = _adamw(g2[n], two_d(wl, n), two_d(ml, n), two_d(vl, n))
        outs[n] = (g2[n], delta, mnew, vnew)
    cols = lambda a: jnp.transpose(a.reshape(wl["w_in"].shape), (2, 0, 1))
    g_cols = cols(g2["w_in"])
    outs["w_in"] = tuple(jnp.transpose(a, (1, 2, 0)) for a in
                         (g_cols,) + tuple(_adamw(g_cols, cols(wl["w_in"]), cols(ml["w_in"]), cols(vl["w_in"]))))
    res = [{}, {}, {}, {}]
    for i in range(4):
        for n in big:
            res[i][n] = outs[n][i].reshape(wl[n].shape)
        v, o = outs["small"][i].reshape(-1), 0
        for n in small_names:
            sz = int(np.prod(wl[n].shape))
            res[i][n] = v[o:o + sz].reshape(wl[n].shape)
            o += sz
    return (loss, grad_x, *[res[0][n] for n in WEIGHTS], *[res[1][n] for n in WEIGHTS], *[res[2][n] for n in WEIGHTS],
            *[res[3][n] for n in WEIGHTS])
```
